```python
import math
import jax, jax.numpy as jnp
from jax import lax
import numpy as np

D_MODEL = 1024
BATCH = 8
SEQ = 8192
DEPTH = 4

N_MIXERS = 3
N_A = (DEPTH + 2) // 3
N_B = (DEPTH + 1) // 3
N_C = DEPTH // 3

HEAD_DIM = 64
N_HEADS = D_MODEL // HEAD_DIM
N_KV = 4
GROUP = N_HEADS // N_KV
WINDOW = 128
BLOCK = 128
QKV_DIM = (N_HEADS + 2 * N_KV) * HEAD_DIM

CONV_K = 31

CHUNK = 128
SGU_FFN = 4 * D_MODEL
SGU_HALF = SGU_FFN // 2
SGU_GROUPS = 8
SGU_GROUP_DIM = SGU_HALF // SGU_GROUPS

FFN_DIM = 2816
FFN_CONV_K = 3

NORM_EPS = 1e-6
NEG_INF = -1e30

kernel_name = "hybrid_swa_conformer_sgu_trunk"


def rmsnorm(x, g):
    xf = x.astype(jnp.float32)
    y = xf * lax.rsqrt(jnp.mean(xf * xf, axis=-1, keepdims=True) + NORM_EPS)
    return (y * g.astype(jnp.float32)).astype(x.dtype)


def layernorm(x, g, b):
    xf = x.astype(jnp.float32)
    mu = jnp.mean(xf, axis=-1, keepdims=True)
    xc = xf - mu
    var = jnp.mean(xc * xc, axis=-1, keepdims=True)
    y = xc * lax.rsqrt(var + NORM_EPS)
    return (y * g.astype(jnp.float32) + b.astype(jnp.float32)).astype(x.dtype)


def causal_dwconv(x, w, b):
    K, C = w.shape
    y = lax.conv_general_dilated(
        x, w[:, None, :].astype(x.dtype), window_strides=(1,), padding=[(K - 1, 0)],
        dimension_numbers=("NWC", "WIO", "NWC"), feature_group_count=C)
    return y + b


def alibi_slopes(n):
    return 2.0 ** (-8.0 * jnp.arange(1, n + 1, dtype=jnp.float32) / n)


def sliding_window_attention(h, wqkv, bqkv, sinks, wo, bo):
    B, T, _ = h.shape
    nb = T // BLOCK
    qkv = h @ wqkv + bqkv
    q, k, v = jnp.split(qkv, [N_HEADS * HEAD_DIM, (N_HEADS + N_KV) * HEAD_DIM], axis=-1)
    q = q.reshape(B, nb, BLOCK, N_KV, GROUP, HEAD_DIM)
    k = k.reshape(B, nb, BLOCK, N_KV, HEAD_DIM)
    v = v.reshape(B, nb, BLOCK, N_KV, HEAD_DIM)

    def with_prev(z):
        prev = jnp.pad(z, ((0, 0), (1, 0), (0, 0), (0, 0), (0, 0)))[:, :-1]
        return jnp.concatenate([prev, z], axis=2)

    k2, v2 = with_prev(k), with_prev(v)
    s = jnp.einsum("bnqkgd,bnskd->bnkgqs", q, k2).astype(jnp.float32) * (HEAD_DIM ** -0.5)

    qpos = jnp.arange(BLOCK) + BLOCK
    kpos = jnp.arange(2 * BLOCK)
    dist = qpos[:, None] - kpos[None, :]
    band = (dist >= 0) & (dist < WINDOW)
    blk = jnp.arange(nb)
    valid = band[None] & ~((blk[:, None, None] == 0) & (kpos[None, None, :] < BLOCK))

    slopes = alibi_slopes(N_HEADS).reshape(N_KV, GROUP)
    s = s - slopes[:, :, None, None] * dist.astype(jnp.float32)[None, None]
    s = jnp.where(valid[None, :, None, None], s, NEG_INF)

    sink = jnp.broadcast_to(sinks.astype(jnp.float32).reshape(N_KV, GROUP)[None, None, :, :, None, None],
                            s.shape[:-1] + (1,))
    p = jax.nn.softmax(jnp.concatenate([s, sink], axis=-1), axis=-1)[..., :-1]
    o = jnp.einsum("bnkgqs,bnskd->bnqkgd", p.astype(v2.dtype), v2).reshape(B, T, N_HEADS * HEAD_DIM)
    return o @ wo + bo


def conformer_conv(h, w_in, b_in, dw, dw_b, ln_g, ln_b, w_out, b_out):
    a, g = jnp.split(h @ w_in + b_in, 2, axis=-1)
    z = a * jax.nn.sigmoid(g)
    z = causal_dwconv(z, dw, dw_b)
    z = jax.nn.silu(layernorm(z, ln_g, ln_b))
    return z @ w_out + b_out


def chunked_sgu(h, w_in, b_in, ln_g, ln_b, ws, bs, w_out, b_out):
    B, T, _ = h.shape
    z = jax.nn.gelu(h @ w_in + b_in, approximate=False)
    u, v = jnp.split(z, 2, axis=-1)
    v = layernorm(v, ln_g, ln_b)
    v = v.reshape(B, T // CHUNK, CHUNK, SGU_GROUPS, SGU_GROUP_DIM)
    mask = jnp.tril(jnp.ones((CHUNK, CHUNK), dtype=bool))
    wm = jnp.where(mask[None], ws, jnp.zeros_like(ws))
    v = jnp.einsum("gts,bcsgd->bctgd", wm.astype(v.dtype), v) + bs.T[None, None, :, :, None]
    v = v.reshape(B, T, SGU_HALF)
    return (u * v) @ w_out + b_out


def conv_glu_ffn(h, w_in, dw, dw_b, w_out):
    z = causal_dwconv(h @ w_in, dw, dw_b)
    g, u = jnp.split(z, 2, axis=-1)
    return (jax.nn.silu(g) * u) @ w_out


def _fwd_setup_inputs(seed: int = 0) -> dict:
    key = jax.random.key(seed)
    ks = jax.random.split(key, 40)
    D = D_MODEL

    def nrm(k, shape, scale):
        return jax.random.normal(k, shape, jnp.float32) * scale

    return {
        "x": nrm(ks[0], (BATCH, SEQ, D), 1.0),
        "c": nrm(ks[1], (BATCH, D), 1.0),
        "norm1_g": 1.0 + nrm(ks[2], (DEPTH, D), 0.05),
        "norm2_g": 1.0 + nrm(ks[3], (DEPTH, D), 0.05),
        "ada_w": nrm(ks[4], (DEPTH, D, 6 * D), 0.5 * D ** -0.5),
        "ada_b": nrm(ks[5], (DEPTH, 6 * D), 0.01),
        "attn_wqkv": nrm(ks[6], (N_A, D, QKV_DIM), D ** -0.5),
        "attn_bqkv": nrm(ks[7], (N_A, QKV_DIM), 0.01),
        "attn_sinks": nrm(ks[8], (N_A, N_HEADS), 0.5),
        "attn_wo": nrm(ks[9], (N_A, N_HEADS * HEAD_DIM, D), (N_HEADS * HEAD_DIM) ** -0.5),
        "attn_bo": nrm(ks[10], (N_A, D), 0.01),
        "conv_w_in": nrm(ks[11], (N_B, D, 2 * D), D ** -0.5),
        "conv_b_in": nrm(ks[12], (N_B, 2 * D), 0.01),
        "conv_dw": nrm(ks[13], (N_B, CONV_K, D), CONV_K ** -0.5),
        "conv_dw_b": nrm(ks[14], (N_B, D), 0.01),
        "conv_ln_g": 1.0 + nrm(ks[15], (N_B, D), 0.05),
        "conv_ln_b": nrm(ks[16], (N_B, D), 0.01),
        "conv_w_out": nrm(ks[17], (N_B, D, D), D ** -0.5),
        "conv_b_out": nrm(ks[18], (N_B, D), 0.01),
        "sgu_w_in": nrm(ks[19], (N_C, D, SGU_FFN), D ** -0.5),
        "sgu_b_in": nrm(ks[20], (N_C, SGU_FFN), 0.01),
        "sgu_ln_g": 1.0 + nrm(ks[21], (N_C, SGU_HALF), 0.05),
        "sgu_ln_b": nrm(ks[22], (N_C, SGU_HALF), 0.01),
        "sgu_ws": nrm(ks[23], (N_C, SGU_GROUPS, CHUNK, CHUNK), CHUNK ** -0.5),
        "sgu_bs": 1.0 + nrm(ks[24], (N_C, SGU_GROUPS, CHUNK), 0.01),
        "sgu_w_out": nrm(ks[25], (N_C, SGU_HALF, D), SGU_HALF ** -0.5),
        "sgu_b_out": nrm(ks[26], (N_C, D), 0.01),
        "ffn_w_in": nrm(ks[27], (DEPTH, D, 2 * FFN_DIM), D ** -0.5),
        "ffn_dw": nrm(ks[28], (DEPTH, FFN_CONV_K, 2 * FFN_DIM), FFN_CONV_K ** -0.5),
        "ffn_dw_b": nrm(ks[29], (DEPTH, 2 * FFN_DIM), 0.01),
        "ffn_w_out": nrm(ks[30], (DEPTH, FFN_DIM, D), FFN_DIM ** -0.5),
        "final_g": 1.0 + nrm(ks[31], (D,), 0.05),
    }


def _fwd_reference(x, c, norm1_g, norm2_g, ada_w, ada_b,
              attn_wqkv, attn_bqkv, attn_sinks, attn_wo, attn_bo,
              conv_w_in, conv_b_in, conv_dw, conv_dw_b, conv_ln_g, conv_ln_b, conv_w_out, conv_b_out,
              sgu_w_in, sgu_b_in, sgu_ln_g, sgu_ln_b, sgu_ws, sgu_bs, sgu_w_out, sgu_b_out,
              ffn_w_in, ffn_dw, ffn_dw_b, ffn_w_out, final_g):
    c_act = jax.nn.silu(c)
    for i in range(DEPTH):
        mod = (c_act @ ada_w[i] + ada_b[i])[:, None, :]
        sh1, sc1, g1, sh2, sc2, g2 = jnp.split(mod, 6, axis=-1)

        h = rmsnorm(x, norm1_g[i]) * (1.0 + sc1) + sh1
        kind, j = i % N_MIXERS, i // N_MIXERS
        if kind == 0:
            y = sliding_window_attention(h, attn_wqkv[j], attn_bqkv[j], attn_sinks[j], attn_wo[j], attn_bo[j])
        elif kind == 1:
            y = conformer_conv(h, conv_w_in[j], conv_b_in[j], conv_dw[j], conv_dw_b[j],
                               conv_ln_g[j], conv_ln_b[j], conv_w_out[j], conv_b_out[j])
        else:
            y = chunked_sgu(h, sgu_w_in[j], sgu_b_in[j], sgu_ln_g[j], sgu_ln_b[j],
                            sgu_ws[j], sgu_bs[j], sgu_w_out[j], sgu_b_out[j])
        x = x + g1 * y

        h = rmsnorm(x, norm2_g[i]) * (1.0 + sc2) + sh2
        x = x + g2 * conv_glu_ffn(h, ffn_w_in[i], ffn_dw[i], ffn_dw_b[i], ffn_w_out[i])
    return rmsnorm(x, final_g)


import jax as _jax
import jax.numpy as _jnp

TWIN_FORMAT = 'train_step'
FWD_PARAMS = ['x', 'c', 'norm1_g', 'norm2_g', 'ada_w', 'ada_b', 'attn_wqkv', 'attn_bqkv', 'attn_sinks', 'attn_wo', 'attn_bo', 'conv_w_in', 'conv_b_in', 'conv_dw', 'conv_dw_b', 'conv_ln_g', 'conv_ln_b', 'conv_w_out', 'conv_b_out', 'sgu_w_in', 'sgu_b_in', 'sgu_ln_g', 'sgu_ln_b', 'sgu_ws', 'sgu_bs', 'sgu_w_out', 'sgu_b_out', 'ffn_w_in', 'ffn_dw', 'ffn_dw_b', 'ffn_w_out', 'final_g']
TWIN_WEIGHTS = ['norm1_g', 'norm2_g', 'ada_w', 'ada_b', 'attn_wqkv', 'attn_bqkv', 'attn_sinks', 'attn_wo', 'attn_bo', 'conv_w_in', 'conv_b_in', 'conv_dw', 'conv_dw_b', 'conv_ln_g', 'conv_ln_b', 'conv_w_out', 'conv_b_out', 'sgu_w_in', 'sgu_b_in', 'sgu_ln_g', 'sgu_ln_b', 'sgu_ws', 'sgu_bs', 'sgu_w_out', 'sgu_b_out', 'ffn_w_in', 'ffn_dw', 'ffn_dw_b', 'ffn_w_out', 'final_g']
TWIN_DIFF_INPUT = 'x'
TWIN_INPUTS = ['x', 'c', 'norm1_g', 'norm2_g', 'ada_w', 'ada_b', 'attn_wqkv', 'attn_bqkv', 'attn_sinks', 'attn_wo', 'attn_bo', 'conv_w_in', 'conv_b_in', 'conv_dw', 'conv_dw_b', 'conv_ln_g', 'conv_ln_b', 'conv_w_out', 'conv_b_out', 'sgu_w_in', 'sgu_b_in', 'sgu_ln_g', 'sgu_ln_b', 'sgu_ws', 'sgu_bs', 'sgu_w_out', 'sgu_b_out', 'ffn_w_in', 'ffn_dw', 'ffn_dw_b', 'ffn_w_out', 'final_g', 'loss_target', 'm_norm1_g', 'm_norm2_g', 'm_ada_w', 'm_ada_b', 'm_attn_wqkv', 'm_attn_bqkv', 'm_attn_sinks', 'm_attn_wo', 'm_attn_bo', 'm_conv_w_in', 'm_conv_b_in', 'm_conv_dw', 'm_conv_dw_b', 'm_conv_ln_g', 'm_conv_ln_b', 'm_conv_w_out', 'm_conv_b_out', 'm_sgu_w_in', 'm_sgu_b_in', 'm_sgu_ln_g', 'm_sgu_ln_b', 'm_sgu_ws', 'm_sgu_bs', 'm_sgu_w_out', 'm_sgu_b_out', 'm_ffn_w_in', 'm_ffn_dw', 'm_ffn_dw_b', 'm_ffn_w_out', 'm_final_g', 'v_norm1_g', 'v_norm2_g', 'v_ada_w', 'v_ada_b', 'v_attn_wqkv', 'v_attn_bqkv', 'v_attn_sinks', 'v_attn_wo', 'v_attn_bo', 'v_conv_w_in', 'v_conv_b_in', 'v_conv_dw', 'v_conv_dw_b', 'v_conv_ln_g', 'v_conv_ln_b', 'v_conv_w_out', 'v_conv_b_out', 'v_sgu_w_in', 'v_sgu_b_in', 'v_sgu_ln_g', 'v_sgu_ln_b', 'v_sgu_ws', 'v_sgu_bs', 'v_sgu_w_out', 'v_sgu_b_out', 'v_ffn_w_in', 'v_ffn_dw', 'v_ffn_dw_b', 'v_ffn_w_out', 'v_final_g']
TWIN_OUTPUTS = ['loss', 'grad_x', 'grad_norm1_g', 'grad_norm2_g', 'grad_ada_w', 'grad_ada_b', 'grad_attn_wqkv', 'grad_attn_bqkv', 'grad_attn_sinks', 'grad_attn_wo', 'grad_attn_bo', 'grad_conv_w_in', 'grad_conv_b_in', 'grad_conv_dw', 'grad_conv_dw_b', 'grad_conv_ln_g', 'grad_conv_ln_b', 'grad_conv_w_out', 'grad_conv_b_out', 'grad_sgu_w_in', 'grad_sgu_b_in', 'grad_sgu_ln_g', 'grad_sgu_ln_b', 'grad_sgu_ws', 'grad_sgu_bs', 'grad_sgu_w_out', 'grad_sgu_b_out', 'grad_ffn_w_in', 'grad_ffn_dw', 'grad_ffn_dw_b', 'grad_ffn_w_out', 'grad_final_g', 'delta_norm1_g', 'delta_norm2_g', 'delta_ada_w', 'delta_ada_b', 'delta_attn_wqkv', 'delta_attn_bqkv', 'delta_attn_sinks', 'delta_attn_wo', 'delta_attn_bo', 'delta_conv_w_in', 'delta_conv_b_in', 'delta_conv_dw', 'delta_conv_dw_b', 'delta_conv_ln_g', 'delta_conv_ln_b', 'delta_conv_w_out', 'delta_conv_b_out', 'delta_sgu_w_in', 'delta_sgu_b_in', 'delta_sgu_ln_g', 'delta_sgu_ln_b', 'delta_sgu_ws', 'delta_sgu_bs', 'delta_sgu_w_out', 'delta_sgu_b_out', 'delta_ffn_w_in', 'delta_ffn_dw', 'delta_ffn_dw_b', 'delta_ffn_w_out', 'delta_final_g', 'new_m_norm1_g', 'new_m_norm2_g', 'new_m_ada_w', 'new_m_ada_b', 'new_m_attn_wqkv', 'new_m_attn_bqkv', 'new_m_attn_sinks', 'new_m_attn_wo', 'new_m_attn_bo', 'new_m_conv_w_in', 'new_m_conv_b_in', 'new_m_conv_dw', 'new_m_conv_dw_b', 'new_m_conv_ln_g', 'new_m_conv_ln_b', 'new_m_conv_w_out', 'new_m_conv_b_out', 'new_m_sgu_w_in', 'new_m_sgu_b_in', 'new_m_sgu_ln_g', 'new_m_sgu_ln_b', 'new_m_sgu_ws', 'new_m_sgu_bs', 'new_m_sgu_w_out', 'new_m_sgu_b_out', 'new_m_ffn_w_in', 'new_m_ffn_dw', 'new_m_ffn_dw_b', 'new_m_ffn_w_out', 'new_m_final_g', 'new_v_norm1_g', 'new_v_norm2_g', 'new_v_ada_w', 'new_v_ada_b', 'new_v_attn_wqkv', 'new_v_attn_bqkv', 'new_v_attn_sinks', 'new_v_attn_wo', 'new_v_attn_bo', 'new_v_conv_w_in', 'new_v_conv_b_in', 'new_v_conv_dw', 'new_v_conv_dw_b', 'new_v_conv_ln_g', 'new_v_conv_ln_b', 'new_v_conv_w_out', 'new_v_conv_b_out', 'new_v_sgu_w_in', 'new_v_sgu_b_in', 'new_v_sgu_ln_g', 'new_v_sgu_ln_b', 'new_v_sgu_ws', 'new_v_sgu_bs', 'new_v_sgu_w_out', 'new_v_sgu_b_out', 'new_v_ffn_w_in', 'new_v_ffn_dw', 'new_v_ffn_dw_b', 'new_v_ffn_w_out', 'new_v_final_g']
TWIN_LEAF_KINDS = {'loss': 'loss', 'grad_x': 'grad_x', 'grad_norm1_g': 'grad_w', 'grad_norm2_g': 'grad_w', 'grad_ada_w': 'grad_w', 'grad_ada_b': 'grad_w', 'grad_attn_wqkv': 'grad_w', 'grad_attn_bqkv': 'grad_w', 'grad_attn_sinks': 'grad_w', 'grad_attn_wo': 'grad_w', 'grad_attn_bo': 'grad_w', 'grad_conv_w_in': 'grad_w', 'grad_conv_b_in': 'grad_w', 'grad_conv_dw': 'grad_w', 'grad_conv_dw_b': 'grad_w', 'grad_conv_ln_g': 'grad_w', 'grad_conv_ln_b': 'grad_w', 'grad_conv_w_out': 'grad_w', 'grad_conv_b_out': 'grad_w', 'grad_sgu_w_in': 'grad_w', 'grad_sgu_b_in': 'grad_w', 'grad_sgu_ln_g': 'grad_w', 'grad_sgu_ln_b': 'grad_w', 'grad_sgu_ws': 'grad_w', 'grad_sgu_bs': 'grad_w', 'grad_sgu_w_out': 'grad_w', 'grad_sgu_b_out': 'grad_w', 'grad_ffn_w_in': 'grad_w', 'grad_ffn_dw': 'grad_w', 'grad_ffn_dw_b': 'grad_w', 'grad_ffn_w_out': 'grad_w', 'grad_final_g': 'grad_w', 'delta_norm1_g': 'delta_w', 'delta_norm2_g': 'delta_w', 'delta_ada_w': 'delta_w', 'delta_ada_b': 'delta_w', 'delta_attn_wqkv': 'delta_w', 'delta_attn_bqkv': 'delta_w', 'delta_attn_sinks': 'delta_w', 'delta_attn_wo': 'delta_w', 'delta_attn_bo': 'delta_w', 'delta_conv_w_in': 'delta_w', 'delta_conv_b_in': 'delta_w', 'delta_conv_dw': 'delta_w', 'delta_conv_dw_b': 'delta_w', 'delta_conv_ln_g': 'delta_w', 'delta_conv_ln_b': 'delta_w', 'delta_conv_w_out': 'delta_w', 'delta_conv_b_out': 'delta_w', 'delta_sgu_w_in': 'delta_w', 'delta_sgu_b_in': 'delta_w', 'delta_sgu_ln_g': 'delta_w', 'delta_sgu_ln_b': 'delta_w', 'delta_sgu_ws': 'delta_w', 'delta_sgu_bs': 'delta_w', 'delta_sgu_w_out': 'delta_w', 'delta_sgu_b_out': 'delta_w', 'delta_ffn_w_in': 'delta_w', 'delta_ffn_dw': 'delta_w', 'delta_ffn_dw_b': 'delta_w', 'delta_ffn_w_out': 'delta_w', 'delta_final_g': 'delta_w', 'new_m_norm1_g': 'new_m', 'new_m_norm2_g': 'new_m', 'new_m_ada_w': 'new_m', 'new_m_ada_b': 'new_m', 'new_m_attn_wqkv': 'new_m', 'new_m_attn_bqkv': 'new_m', 'new_m_attn_sinks': 'new_m', 'new_m_attn_wo': 'new_m', 'new_m_attn_bo': 'new_m', 'new_m_conv_w_in': 'new_m', 'new_m_conv_b_in': 'new_m', 'new_m_conv_dw': 'new_m', 'new_m_conv_dw_b': 'new_m', 'new_m_conv_ln_g': 'new_m', 'new_m_conv_ln_b': 'new_m', 'new_m_conv_w_out': 'new_m', 'new_m_conv_b_out': 'new_m', 'new_m_sgu_w_in': 'new_m', 'new_m_sgu_b_in': 'new_m', 'new_m_sgu_ln_g': 'new_m', 'new_m_sgu_ln_b': 'new_m', 'new_m_sgu_ws': 'new_m', 'new_m_sgu_bs': 'new_m', 'new_m_sgu_w_out': 'new_m', 'new_m_sgu_b_out': 'new_m', 'new_m_ffn_w_in': 'new_m', 'new_m_ffn_dw': 'new_m', 'new_m_ffn_dw_b': 'new_m', 'new_m_ffn_w_out': 'new_m', 'new_m_final_g': 'new_m', 'new_v_norm1_g': 'new_v', 'new_v_norm2_g': 'new_v', 'new_v_ada_w': 'new_v', 'new_v_ada_b': 'new_v', 'new_v_attn_wqkv': 'new_v', 'new_v_attn_bqkv': 'new_v', 'new_v_attn_sinks': 'new_v', 'new_v_attn_wo': 'new_v', 'new_v_attn_bo': 'new_v', 'new_v_conv_w_in': 'new_v', 'new_v_conv_b_in': 'new_v', 'new_v_conv_dw': 'new_v', 'new_v_conv_dw_b': 'new_v', 'new_v_conv_ln_g': 'new_v', 'new_v_conv_ln_b': 'new_v', 'new_v_conv_w_out': 'new_v', 'new_v_conv_b_out': 'new_v', 'new_v_sgu_w_in': 'new_v', 'new_v_sgu_b_in': 'new_v', 'new_v_sgu_ln_g': 'new_v', 'new_v_sgu_ln_b': 'new_v', 'new_v_sgu_ws': 'new_v', 'new_v_sgu_bs': 'new_v', 'new_v_sgu_w_out': 'new_v', 'new_v_sgu_b_out': 'new_v', 'new_v_ffn_w_in': 'new_v', 'new_v_ffn_dw': 'new_v', 'new_v_ffn_dw_b': 'new_v', 'new_v_ffn_w_out': 'new_v', 'new_v_final_g': 'new_v'}


def _forward(args):
    return _fwd_reference(*[args[k] for k in FWD_PARAMS])


def _output_shape():
    def fwd():
        inp = _fwd_setup_inputs(0)
        return _fwd_reference(*[inp[k] for k in FWD_PARAMS])
    out = _jax.eval_shape(fwd)
    return out.shape, out.dtype

N_MICROBATCH = 1
ADAM_LR = 0.001
ADAM_B1 = 0.9
ADAM_B2 = 0.999
ADAM_EPS = 1e-08
ADAM_WD = 0.01
ADAM_STEP = 10
PER_EXAMPLE_BATCH_AXIS = {'x': 0, 'c': 0, 'loss_target': 0}
SHARED_INPUTS = []
_WEIGHT_DTYPES = {'norm1_g': _jnp.float32, 'norm2_g': _jnp.float32, 'ada_w': _jnp.float32, 'ada_b': _jnp.float32, 'attn_wqkv': _jnp.float32, 'attn_bqkv': _jnp.float32, 'attn_sinks': _jnp.float32, 'attn_wo': _jnp.float32, 'attn_bo': _jnp.float32, 'conv_w_in': _jnp.float32, 'conv_b_in': _jnp.float32, 'conv_dw': _jnp.float32, 'conv_dw_b': _jnp.float32, 'conv_ln_g': _jnp.float32, 'conv_ln_b': _jnp.float32, 'conv_w_out': _jnp.float32, 'conv_b_out': _jnp.float32, 'sgu_w_in': _jnp.float32, 'sgu_b_in': _jnp.float32, 'sgu_ln_g': _jnp.float32, 'sgu_ln_b': _jnp.float32, 'sgu_ws': _jnp.float32, 'sgu_bs': _jnp.float32, 'sgu_w_out': _jnp.float32, 'sgu_b_out': _jnp.float32, 'ffn_w_in': _jnp.float32, 'ffn_dw': _jnp.float32, 'ffn_dw_b': _jnp.float32, 'ffn_w_out': _jnp.float32, 'final_g': _jnp.float32}
MOMENT_SCALE = {'norm1_g': 5.321346e-02, 'norm2_g': 7.286024e-02, 'ada_w': 8.630749e-02, 'ada_b': 1.583902e-01, 'attn_wqkv': 3.950416e-02, 'attn_bqkv': 9.764316e-02, 'attn_sinks': 3.710072e-02, 'attn_wo': 3.960718e-02, 'attn_bo': 1.146547e-01, 'conv_w_in': 3.750759e-02, 'conv_b_in': 4.550445e-02, 'conv_dw': 4.906584e-02, 'conv_dw_b': 1.006462e-01, 'conv_ln_g': 6.854406e-02, 'conv_ln_b': 6.852052e-02, 'conv_w_out': 5.074144e-02, 'conv_b_out': 1.209428e-01, 'sgu_w_in': 3.974892e-02, 'sgu_b_in': 4.416667e-02, 'sgu_ln_g': 2.618914e-02, 'sgu_ln_b': 2.690569e-02, 'sgu_ws': 3.700726e-02, 'sgu_bs': 5.339789e-02, 'sgu_w_out': 7.317013e-02, 'sgu_b_out': 1.409243e-01, 'ffn_w_in': 3.272761e-02, 'ffn_dw': 3.301839e-02, 'ffn_dw_b': 3.128723e-02, 'ffn_w_out': 5.397822e-02, 'final_g': 6.419644e+01}


def _to_microbatches(a, axis):
    t = _jnp.moveaxis(a, axis, 0)
    t = t.reshape((N_MICROBATCH, t.shape[0] // N_MICROBATCH) + t.shape[1:])
    return _jnp.moveaxis(t, 1, axis + 1)


def setup_inputs(seed: int = 0) -> dict:
    inp = _fwd_setup_inputs(seed)
    key = _jax.random.fold_in(_jax.random.key(seed), 7919)
    shape, _ = _output_shape()
    out = dict(inp)
    out["loss_target"] = _jax.random.normal(_jax.random.fold_in(key, 0), shape, _jnp.float32)
    for i, name in enumerate(TWIN_WEIGHTS):
        w = inp[name].astype(_jnp.float32)
        if MOMENT_SCALE is None:
            s = _jnp.sqrt(_jnp.mean(_jnp.square(w)) + 1e-30)
        else:
            s = MOMENT_SCALE[name]
        km, kv = _jax.random.split(_jax.random.fold_in(key, i + 1))
        out[name] = w
        out["m_" + name] = s * _jax.random.normal(km, w.shape, _jnp.float32)
        out["v_" + name] = (s * s) * _jax.random.uniform(kv, w.shape, _jnp.float32, 0.5, 1.5)
    if N_MICROBATCH > 1:
        for name, axis in PER_EXAMPLE_BATCH_AXIS.items():
            out[name] = _to_microbatches(out[name], axis)
    return {'x': out['x'], 'c': out['c'], 'norm1_g': out['norm1_g'], 'norm2_g': out['norm2_g'], 'ada_w': out['ada_w'], 'ada_b': out['ada_b'], 'attn_wqkv': out['attn_wqkv'], 'attn_bqkv': out['attn_bqkv'], 'attn_sinks': out['attn_sinks'], 'attn_wo': out['attn_wo'], 'attn_bo': out['attn_bo'], 'conv_w_in': out['conv_w_in'], 'conv_b_in': out['conv_b_in'], 'conv_dw': out['conv_dw'], 'conv_dw_b': out['conv_dw_b'], 'conv_ln_g': out['conv_ln_g'], 'conv_ln_b': out['conv_ln_b'], 'conv_w_out': out['conv_w_out'], 'conv_b_out': out['conv_b_out'], 'sgu_w_in': out['sgu_w_in'], 'sgu_b_in': out['sgu_b_in'], 'sgu_ln_g': out['sgu_ln_g'], 'sgu_ln_b': out['sgu_ln_b'], 'sgu_ws': out['sgu_ws'], 'sgu_bs': out['sgu_bs'], 'sgu_w_out': out['sgu_w_out'], 'sgu_b_out': out['sgu_b_out'], 'ffn_w_in': out['ffn_w_in'], 'ffn_dw': out['ffn_dw'], 'ffn_dw_b': out['ffn_dw_b'], 'ffn_w_out': out['ffn_w_out'], 'final_g': out['final_g'], 'loss_target': out['loss_target'], 'm_norm1_g': out['m_norm1_g'], 'm_norm2_g': out['m_norm2_g'], 'm_ada_w': out['m_ada_w'], 'm_ada_b': out['m_ada_b'], 'm_attn_wqkv': out['m_attn_wqkv'], 'm_attn_bqkv': out['m_attn_bqkv'], 'm_attn_sinks': out['m_attn_sinks'], 'm_attn_wo': out['m_attn_wo'], 'm_attn_bo': out['m_attn_bo'], 'm_conv_w_in': out['m_conv_w_in'], 'm_conv_b_in': out['m_conv_b_in'], 'm_conv_dw': out['m_conv_dw'], 'm_conv_dw_b': out['m_conv_dw_b'], 'm_conv_ln_g': out['m_conv_ln_g'], 'm_conv_ln_b': out['m_conv_ln_b'], 'm_conv_w_out': out['m_conv_w_out'], 'm_conv_b_out': out['m_conv_b_out'], 'm_sgu_w_in': out['m_sgu_w_in'], 'm_sgu_b_in': out['m_sgu_b_in'], 'm_sgu_ln_g': out['m_sgu_ln_g'], 'm_sgu_ln_b': out['m_sgu_ln_b'], 'm_sgu_ws': out['m_sgu_ws'], 'm_sgu_bs': out['m_sgu_bs'], 'm_sgu_w_out': out['m_sgu_w_out'], 'm_sgu_b_out': out['m_sgu_b_out'], 'm_ffn_w_in': out['m_ffn_w_in'], 'm_ffn_dw': out['m_ffn_dw'], 'm_ffn_dw_b': out['m_ffn_dw_b'], 'm_ffn_w_out': out['m_ffn_w_out'], 'm_final_g': out['m_final_g'], 'v_norm1_g': out['v_norm1_g'], 'v_norm2_g': out['v_norm2_g'], 'v_ada_w': out['v_ada_w'], 'v_ada_b': out['v_ada_b'], 'v_attn_wqkv': out['v_attn_wqkv'], 'v_attn_bqkv': out['v_attn_bqkv'], 'v_attn_sinks': out['v_attn_sinks'], 'v_attn_wo': out['v_attn_wo'], 'v_attn_bo': out['v_attn_bo'], 'v_conv_w_in': out['v_conv_w_in'], 'v_conv_b_in': out['v_conv_b_in'], 'v_conv_dw': out['v_conv_dw'], 'v_conv_dw_b': out['v_conv_dw_b'], 'v_conv_ln_g': out['v_conv_ln_g'], 'v_conv_ln_b': out['v_conv_ln_b'], 'v_conv_w_out': out['v_conv_w_out'], 'v_conv_b_out': out['v_conv_b_out'], 'v_sgu_w_in': out['v_sgu_w_in'], 'v_sgu_b_in': out['v_sgu_b_in'], 'v_sgu_ln_g': out['v_sgu_ln_g'], 'v_sgu_ln_b': out['v_sgu_ln_b'], 'v_sgu_ws': out['v_sgu_ws'], 'v_sgu_bs': out['v_sgu_bs'], 'v_sgu_w_out': out['v_sgu_w_out'], 'v_sgu_b_out': out['v_sgu_b_out'], 'v_ffn_w_in': out['v_ffn_w_in'], 'v_ffn_dw': out['v_ffn_dw'], 'v_ffn_dw_b': out['v_ffn_dw_b'], 'v_ffn_w_out': out['v_ffn_w_out'], 'v_final_g': out['v_final_g']}


def _loss(weights, diff, rest, loss_target):
    with _jax.named_scope("forward"):
        args = {**rest, TWIN_DIFF_INPUT: diff, **{k: w.astype(_WEIGHT_DTYPES[k]) for k, w in weights.items()}}
        y = _forward(args)
    with _jax.named_scope("loss_head"):
        err = _jnp.square(y.astype(_jnp.float32) - loss_target)
        return 0.5 * _jnp.sum(_jnp.mean(err, axis=-1)) if err.ndim else 0.5 * err


def _adamw(w, g, m, v):
    m = ADAM_B1 * m + (1.0 - ADAM_B1) * g
    v = ADAM_B2 * v + (1.0 - ADAM_B2) * _jnp.square(g)
    m_hat = m / (1.0 - ADAM_B1 ** ADAM_STEP)
    v_hat = v / (1.0 - ADAM_B2 ** ADAM_STEP)
    delta = -ADAM_LR * (m_hat / (_jnp.sqrt(v_hat) + ADAM_EPS) + ADAM_WD * w)
    return delta, m, v


def reference(x, c, norm1_g, norm2_g, ada_w, ada_b, attn_wqkv, attn_bqkv, attn_sinks, attn_wo, attn_bo, conv_w_in, conv_b_in, conv_dw, conv_dw_b, conv_ln_g, conv_ln_b, conv_w_out, conv_b_out, sgu_w_in, sgu_b_in, sgu_ln_g, sgu_ln_b, sgu_ws, sgu_bs, sgu_w_out, sgu_b_out, ffn_w_in, ffn_dw, ffn_dw_b, ffn_w_out, final_g, loss_target, m_norm1_g, m_norm2_g, m_ada_w, m_ada_b, m_attn_wqkv, m_attn_bqkv, m_attn_sinks, m_attn_wo, m_attn_bo, m_conv_w_in, m_conv_b_in, m_conv_dw, m_conv_dw_b, m_conv_ln_g, m_conv_ln_b, m_conv_w_out, m_conv_b_out, m_sgu_w_in, m_sgu_b_in, m_sgu_ln_g, m_sgu_ln_b, m_sgu_ws, m_sgu_bs, m_sgu_w_out, m_sgu_b_out, m_ffn_w_in, m_ffn_dw, m_ffn_dw_b, m_ffn_w_out, m_final_g, v_norm1_g, v_norm2_g, v_ada_w, v_ada_b, v_attn_wqkv, v_attn_bqkv, v_attn_sinks, v_attn_wo, v_attn_bo, v_conv_w_in, v_conv_b_in, v_conv_dw, v_conv_dw_b, v_conv_ln_g, v_conv_ln_b, v_conv_w_out, v_conv_b_out, v_sgu_w_in, v_sgu_b_in, v_sgu_ln_g, v_sgu_ln_b, v_sgu_ws, v_sgu_bs, v_sgu_w_out, v_sgu_b_out, v_ffn_w_in, v_ffn_dw, v_ffn_dw_b, v_ffn_w_out, v_final_g):
    given = dict(x=x, c=c, norm1_g=norm1_g, norm2_g=norm2_g, ada_w=ada_w, ada_b=ada_b, attn_wqkv=attn_wqkv, attn_bqkv=attn_bqkv, attn_sinks=attn_sinks, attn_wo=attn_wo, attn_bo=attn_bo, conv_w_in=conv_w_in, conv_b_in=conv_b_in, conv_dw=conv_dw, conv_dw_b=conv_dw_b, conv_ln_g=conv_ln_g, conv_ln_b=conv_ln_b, conv_w_out=conv_w_out, conv_b_out=conv_b_out, sgu_w_in=sgu_w_in, sgu_b_in=sgu_b_in, sgu_ln_g=sgu_ln_g, sgu_ln_b=sgu_ln_b, sgu_ws=sgu_ws, sgu_bs=sgu_bs, sgu_w_out=sgu_w_out, sgu_b_out=sgu_b_out, ffn_w_in=ffn_w_in, ffn_dw=ffn_dw, ffn_dw_b=ffn_dw_b, ffn_w_out=ffn_w_out, final_g=final_g, loss_target=loss_target, m_norm1_g=m_norm1_g, m_norm2_g=m_norm2_g, m_ada_w=m_ada_w, m_ada_b=m_ada_b, m_attn_wqkv=m_attn_wqkv, m_attn_bqkv=m_attn_bqkv, m_attn_sinks=m_attn_sinks, m_attn_wo=m_attn_wo, m_attn_bo=m_attn_bo, m_conv_w_in=m_conv_w_in, m_conv_b_in=m_conv_b_in, m_conv_dw=m_conv_dw, m_conv_dw_b=m_conv_dw_b, m_conv_ln_g=m_conv_ln_g, m_conv_ln_b=m_conv_ln_b, m_conv_w_out=m_conv_w_out, m_conv_b_out=m_conv_b_out, m_sgu_w_in=m_sgu_w_in, m_sgu_b_in=m_sgu_b_in, m_sgu_ln_g=m_sgu_ln_g, m_sgu_ln_b=m_sgu_ln_b, m_sgu_ws=m_sgu_ws, m_sgu_bs=m_sgu_bs, m_sgu_w_out=m_sgu_w_out, m_sgu_b_out=m_sgu_b_out, m_ffn_w_in=m_ffn_w_in, m_ffn_dw=m_ffn_dw, m_ffn_dw_b=m_ffn_dw_b, m_ffn_w_out=m_ffn_w_out, m_final_g=m_final_g, v_norm1_g=v_norm1_g, v_norm2_g=v_norm2_g, v_ada_w=v_ada_w, v_ada_b=v_ada_b, v_attn_wqkv=v_attn_wqkv, v_attn_bqkv=v_attn_bqkv, v_attn_sinks=v_attn_sinks, v_attn_wo=v_attn_wo, v_attn_bo=v_attn_bo, v_conv_w_in=v_conv_w_in, v_conv_b_in=v_conv_b_in, v_conv_dw=v_conv_dw, v_conv_dw_b=v_conv_dw_b, v_conv_ln_g=v_conv_ln_g, v_conv_ln_b=v_conv_ln_b, v_conv_w_out=v_conv_w_out, v_conv_b_out=v_conv_b_out, v_sgu_w_in=v_sgu_w_in, v_sgu_b_in=v_sgu_b_in, v_sgu_ln_g=v_sgu_ln_g, v_sgu_ln_b=v_sgu_ln_b, v_sgu_ws=v_sgu_ws, v_sgu_bs=v_sgu_bs, v_sgu_w_out=v_sgu_w_out, v_sgu_b_out=v_sgu_b_out, v_ffn_w_in=v_ffn_w_in, v_ffn_dw=v_ffn_dw, v_ffn_dw_b=v_ffn_dw_b, v_ffn_w_out=v_ffn_w_out, v_final_g=v_final_g)
    weights = {n: given[n] for n in TWIN_WEIGHTS}
    shared = {n: given[n] for n in SHARED_INPUTS}
    per_example = {n: given[n] for n in ['x', 'c']}
    grad_fn = _jax.value_and_grad(_loss, argnums=(0, 1))

    def one_microbatch(ex, loss_target):
        ex = dict(ex)
        diff = ex.pop(TWIN_DIFF_INPUT)
        return grad_fn(weights, diff, {**shared, **ex}, loss_target)

    if N_MICROBATCH == 1:
        loss, (grad_w, grad_x) = one_microbatch(per_example, given["loss_target"])
    else:
        def body(carry, xs):
            loss_sum, grad_sum = carry
            l_k, (gw_k, gx_k) = one_microbatch(xs[0], xs[1])
            with _jax.named_scope("update"):
                return (loss_sum + l_k, _jax.tree.map(_jnp.add, grad_sum, gw_k)), gx_k

        init = (_jnp.zeros((), _jnp.float32), _jax.tree.map(_jnp.zeros_like, weights))
        (loss, grad_w), grad_x = _jax.lax.scan(body, init, (per_example, given["loss_target"]))
    with _jax.named_scope("update"):
        delta_w, new_m, new_v = {}, {}, {}
        for n in TWIN_WEIGHTS:
            delta_w[n], new_m[n], new_v[n] = _adamw(weights[n], grad_w[n], given["m_" + n], given["v_" + n])
    return (loss, grad_x, *[grad_w[n] for n in TWIN_WEIGHTS], *[delta_w[n] for n in TWIN_WEIGHTS],
            *[new_m[n] for n in TWIN_WEIGHTS], *[new_v[n] for n in TWIN_WEIGHTS])
```

```python
import functools
import math

import jax
import jax.numpy as jnp
from jax import lax
from jax.experimental import pallas as pl
from jax.experimental.pallas import tpu as pltpu

F32 = jnp.float32
BF16 = jnp.bfloat16

N_DEV = 8
HEAD_DIM = 64
N_KV = 4
BLOCK = 128
NORM_EPS = 1e-6
NEG_INF = -1e30
ADAM_LR = 0.001
ADAM_B1 = 0.9
ADAM_B2 = 0.999
ADAM_EPS = 1e-08
ADAM_WD = 0.01
ADAM_STEP = 10
V7X_VMEM_LIMIT = 56 * 1024 * 1024
MESH = pl.DeviceIdType.MESH


def _cparams():
    return pltpu.CompilerParams(vmem_limit_bytes=V7X_VMEM_LIMIT)


def _tile(n, cap, align):
    best = None
    for t in range(align, min(n, cap) + 1, align):
        if n % t == 0:
            best = t
    return best if best is not None else n


def _dot(a, b, ca, cb):
    return lax.dot_general(a, b, (((ca,), (cb,)), ((), ())), preferred_element_type=F32)


def _sigmoid(x):
    return 1.0 / (1.0 + jnp.exp(-x))


def _rows(x, start, n):
    return lax.slice_in_dim(x, start, start + n, axis=0)


def _my_coords():
    return lax.axis_index("x"), lax.axis_index("y"), lax.axis_index("c")


def _peer(m, mx, my, mc):
    px = (mx + ((m >> 2) & 1)) % 2
    py = (my + ((m >> 1) & 1)) % 2
    pc = (mc + (m & 1)) % 2
    return px, py, pc


def _all_gather(x, name):
    def body(x_ref, o_ref, send_sems, recv_sems, local_sem):
        mx, my, mc = _my_coords()
        me = 4 * mx + 2 * my + mc
        local = pltpu.make_async_copy(x_ref, o_ref.at[me], local_sem)
        local.start()
        copies = []
        for m in range(1, N_DEV):
            cp = pltpu.make_async_remote_copy(
                src_ref=x_ref, dst_ref=o_ref.at[me], send_sem=send_sems.at[m - 1], recv_sem=recv_sems.at[m - 1],
                device_id=_peer(m, mx, my, mc), device_id_type=MESH)
            cp.start()
            copies.append(cp)
        for cp in copies:
            cp.wait_recv()
        for cp in copies:
            cp.wait_send()
        local.wait()

    return pl.pallas_call(
        body, name=name,
        out_shape=jax.ShapeDtypeStruct((N_DEV,) + x.shape, x.dtype),
        in_specs=[pl.BlockSpec(memory_space=pl.ANY)],
        out_specs=pl.BlockSpec(memory_space=pl.ANY),
        scratch_shapes=[pltpu.SemaphoreType.DMA((N_DEV - 1,)), pltpu.SemaphoreType.DMA((N_DEV - 1,)),
                        pltpu.SemaphoreType.DMA],
    )(x)


def _all_to_all(x, name):
    def body(x_ref, o_ref, send_sems, recv_sems, local_sem):
        mx, my, mc = _my_coords()
        me = 4 * mx + 2 * my + mc
        local = pltpu.make_async_copy(x_ref.at[me], o_ref.at[me], local_sem)
        local.start()
        copies = []
        for m in range(1, N_DEV):
            px, py, pc = _peer(m, mx, my, mc)
            cp = pltpu.make_async_remote_copy(
                src_ref=x_ref.at[4 * px + 2 * py + pc], dst_ref=o_ref.at[me],
                send_sem=send_sems.at[m - 1], recv_sem=recv_sems.at[m - 1],
                device_id=(px, py, pc), device_id_type=MESH)
            cp.start()
            copies.append(cp)
        for cp in copies:
            cp.wait_recv()
        for cp in copies:
            cp.wait_send()
        local.wait()

    return pl.pallas_call(
        body, name=name,
        out_shape=jax.ShapeDtypeStruct(x.shape, x.dtype),
        in_specs=[pl.BlockSpec(memory_space=pl.ANY)],
        out_specs=pl.BlockSpec(memory_space=pl.ANY),
        scratch_shapes=[pltpu.SemaphoreType.DMA((N_DEV - 1,)), pltpu.SemaphoreType.DMA((N_DEV - 1,)),
                        pltpu.SemaphoreType.DMA],
    )(x)


def _mm(a, b, mode, name, out_dtype=F32, bias=None, tm_cap=1024, tn_cap=1408, tk_cap=1408):
    if mode == "nn":
        (M, K), (K2, N) = a.shape, b.shape
    elif mode == "nt":
        (M, K), (N, K2) = a.shape, b.shape
    else:
        (K, M), (K2, N) = a.shape, b.shape
    assert K == K2, (a.shape, b.shape, mode)
    tm = _tile(M, tm_cap, 128 if mode == "tn" else 16)
    tn = _tile(N, tn_cap, 128)
    tk = _tile(K, tk_cap, 128)
    nk = K // tk
    ca, cb = {"nn": (1, 0), "nt": (1, 1), "tn": (0, 0)}[mode]
    has_bias = bias is not None

    def body(*refs):
        if has_bias:
            a_ref, b_ref, bias_ref, o_ref, acc_ref = refs
        else:
            a_ref, b_ref, o_ref, acc_ref = refs
        k = pl.program_id(2)
        part = _dot(a_ref[...].astype(BF16), b_ref[...].astype(BF16), ca, cb)

        @pl.when(k == 0)
        def _():
            acc_ref[...] = part

        @pl.when(k > 0)
        def _():
            acc_ref[...] += part

        @pl.when(k == nk - 1)
        def _():
            r = acc_ref[...]
            if has_bias:
                r = r + bias_ref[...]
            o_ref[...] = r.astype(out_dtype)

    if mode == "tn":
        a_spec = pl.BlockSpec((tk, tm), lambda i, j, k: (k, i))
    else:
        a_spec = pl.BlockSpec((tm, tk), lambda i, j, k: (i, k))
    if mode == "nt":
        b_spec = pl.BlockSpec((tn, tk), lambda i, j, k: (j, k))
    else:
        b_spec = pl.BlockSpec((tk, tn), lambda i, j, k: (k, j))
    in_specs = [a_spec, b_spec]
    args = [a, b]
    if has_bias:
        in_specs.append(pl.BlockSpec((1, tn), lambda i, j, k: (0, j)))
        args.append(bias)
    return pl.pallas_call(
        body, name=name, grid=(M // tm, N // tn, nk),
        out_shape=jax.ShapeDtypeStruct((M, N), out_dtype),
        in_specs=in_specs, out_specs=pl.BlockSpec((tm, tn), lambda i, j, k: (i, j)),
        scratch_shapes=[pltpu.VMEM((tm, tn), F32)],
        compiler_params=_cparams(),
    )(*args)


def _vec_spec(n):
    return pl.BlockSpec((1, n), lambda t: (0, 0))


def _norm_mod_fwd(x, g, sc, sh, name, y=None, gate=None):
    T, D = x.shape
    tr = _tile(T, 256, 16)
    has_y = y is not None

    def body(*refs):
        if has_y:
            x_ref, y_ref, gate_ref, g_ref, sc_ref, sh_ref, xn_ref, h_ref = refs
            xv = x_ref[...] + gate_ref[...] * y_ref[...]
            xn_ref[...] = xv
        else:
            x_ref, g_ref, sc_ref, sh_ref, h_ref = refs
            xv = x_ref[...]
        r = lax.rsqrt(jnp.mean(xv * xv, axis=-1, keepdims=True) + NORM_EPS)
        h = (xv * r * g_ref[...]) * (1.0 + sc_ref[...]) + sh_ref[...]
        h_ref[...] = h.astype(BF16)

    row = pl.BlockSpec((tr, D), lambda t: (t, 0))
    vec = _vec_spec(D)
    if has_y:
        in_specs, args = [row, row, vec, vec, vec, vec], [x, y, gate, g, sc, sh]
        out_shape = (jax.ShapeDtypeStruct((T, D), F32), jax.ShapeDtypeStruct((T, D), BF16))
        out_specs = (row, row)
    else:
        in_specs, args = [row, vec, vec, vec], [x, g, sc, sh]
        out_shape = jax.ShapeDtypeStruct((T, D), BF16)
        out_specs = row
    return pl.pallas_call(body, name=name, grid=(T // tr,), out_shape=out_shape, in_specs=in_specs,
                          out_specs=out_specs, compiler_params=_cparams())(*args)


def _gate_bwd(dxn, y, gate, name):
    T, D = dxn.shape
    tr = _tile(T, 256, 16)

    def body(dxn_ref, y_ref, gate_ref, dy_ref, dgate_ref, dbias_ref):
        @pl.when(pl.program_id(0) == 0)
        def _():
            dgate_ref[...] = jnp.zeros_like(dgate_ref)
            dbias_ref[...] = jnp.zeros_like(dbias_ref)
        d = dxn_ref[...]
        dy = d * gate_ref[...]
        dy_ref[...] = dy.astype(BF16)
        dgate_ref[...] += jnp.sum(d * y_ref[...], axis=0, keepdims=True)
        dbias_ref[...] += jnp.sum(dy, axis=0, keepdims=True)

    row = pl.BlockSpec((tr, D), lambda t: (t, 0))
    vec = _vec_spec(D)
    return pl.pallas_call(
        body, name=name, grid=(T // tr,),
        out_shape=(jax.ShapeDtypeStruct((T, D), BF16), jax.ShapeDtypeStruct((1, D), F32),
                   jax.ShapeDtypeStruct((1, D), F32)),
        in_specs=[row, row, vec], out_specs=(row, vec, vec), compiler_params=_cparams())(dxn, y, gate)


def _norm_mod_bwd(dh, x, g, sc, sh, dxn, name):
    T, D = x.shape
    tr = _tile(T, 256, 16)

    def body(dh_ref, x_ref, g_ref, sc_ref, sh_ref, dxn_ref, dx_ref, dg_ref, dsc_ref, dsh_ref):
        @pl.when(pl.program_id(0) == 0)
        def _():
            dg_ref[...] = jnp.zeros_like(dg_ref)
            dsc_ref[...] = jnp.zeros_like(dsc_ref)
            dsh_ref[...] = jnp.zeros_like(dsh_ref)
        dh = dh_ref[...].astype(F32)
        xv = x_ref[...]
        gv = g_ref[...]
        r = lax.rsqrt(jnp.mean(xv * xv, axis=-1, keepdims=True) + NORM_EPS)
        yv = xv * r
        dsh_ref[...] += jnp.sum(dh, axis=0, keepdims=True)
        dsc_ref[...] += jnp.sum(dh * (yv * gv), axis=0, keepdims=True)
        dn = dh * (1.0 + sc_ref[...])
        dg_ref[...] += jnp.sum(dn * yv, axis=0, keepdims=True)
        dy = dn * gv
        dx = r * (dy - yv * jnp.mean(dy * yv, axis=-1, keepdims=True))
        dx_ref[...] = dxn_ref[...] + dx

    row = pl.BlockSpec((tr, D), lambda t: (t, 0))
    vec = _vec_spec(D)
    vshape = jax.ShapeDtypeStruct((1, D), F32)
    return pl.pallas_call(
        body, name=name, grid=(T // tr,),
        out_shape=(jax.ShapeDtypeStruct((T, D), F32), vshape, vshape, vshape),
        in_specs=[row, row, vec, vec, vec, row], out_specs=(row, vec, vec, vec),
        compiler_params=_cparams())(dh, x, g, sc, sh, dxn)


def _final_loss(x, y, gate, tgt, g, name):
    T, D = x.shape
    tr = _tile(T, 256, 16)

    def body(x_ref, y_ref, gate_ref, tgt_ref, g_ref, dx_ref, dg_ref, loss_ref):
        @pl.when(pl.program_id(0) == 0)
        def _():
            dg_ref[...] = jnp.zeros_like(dg_ref)
            loss_ref[...] = jnp.zeros_like(loss_ref)
        xv = x_ref[...] + gate_ref[...] * y_ref[...]
        gv = g_ref[...]
        r = lax.rsqrt(jnp.mean(xv * xv, axis=-1, keepdims=True) + NORM_EPS)
        yv = xv * r
        e = yv * gv - tgt_ref[...]
        per_row = jnp.mean(e * e, axis=-1, keepdims=True)
        loss_ref[...] += 0.5 * jnp.sum(per_row, axis=0, keepdims=True)
        dout = e * (1.0 / D)
        dg_ref[...] += jnp.sum(dout * yv, axis=0, keepdims=True)
        dy = dout * gv
        dx_ref[...] = r * (dy - yv * jnp.mean(dy * yv, axis=-1, keepdims=True))

    row = pl.BlockSpec((tr, D), lambda t: (t, 0))
    vec = _vec_spec(D)
    return pl.pallas_call(
        body, name=name, grid=(T // tr,),
        out_shape=(jax.ShapeDtypeStruct((T, D), F32), jax.ShapeDtypeStruct((1, D), F32),
                   jax.ShapeDtypeStruct((1, 128), F32)),
        in_specs=[row, row, vec, row, vec], out_specs=(row, vec, _vec_spec(128)),
        compiler_params=_cparams())(x, y, gate, tgt, g)


FFN_HALO = 16


def _ffn_mid_fwd(z, dw, b, name):
    T, F2 = z.shape
    F = F2 // 2
    K = dw.shape[0]
    H = FFN_HALO
    tr = _tile(T, 512, H)
    cw = _tile(F, 512, 128)
    nc = F // cw

    def body(gc_ref, gh_ref, uc_ref, uh_ref, wg_ref, wu_ref, bg_ref, bu_ref, a_ref):
        first = pl.program_id(1) == 0

        def conv(cur_ref, halo_ref, w_ref, b_ref):
            halo = jnp.where(first, 0.0, halo_ref[...].astype(F32))
            xx = jnp.concatenate([halo, cur_ref[...].astype(F32)], axis=0)
            w = w_ref[...]
            acc = b_ref[...] + w[K - 1:K, :] * _rows(xx, H, tr)
            for k in range(K - 1):
                acc = acc + w[k:k + 1, :] * _rows(xx, H - (K - 1 - k), tr)
            return acc

        gv = conv(gc_ref, gh_ref, wg_ref, bg_ref)
        uv = conv(uc_ref, uh_ref, wu_ref, bu_ref)
        a_ref[...] = (gv * _sigmoid(gv) * uv).astype(BF16)

    rb = tr // H
    in_specs = [
        pl.BlockSpec((tr, cw), lambda j, t: (t, j)),
        pl.BlockSpec((H, cw), lambda j, t: (jnp.maximum(t * rb - 1, 0), j)),
        pl.BlockSpec((tr, cw), lambda j, t: (t, j + nc)),
        pl.BlockSpec((H, cw), lambda j, t: (jnp.maximum(t * rb - 1, 0), j + nc)),
        pl.BlockSpec((K, cw), lambda j, t: (0, j)),
        pl.BlockSpec((K, cw), lambda j, t: (0, j + nc)),
        pl.BlockSpec((1, cw), lambda j, t: (0, j)),
        pl.BlockSpec((1, cw), lambda j, t: (0, j + nc)),
    ]
    return pl.pallas_call(
        body, name=name, grid=(nc, T // tr),
        out_shape=jax.ShapeDtypeStruct((T, F), BF16),
        in_specs=in_specs, out_specs=pl.BlockSpec((tr, cw), lambda j, t: (t, j)),
        compiler_params=_cparams())(z, z, z, z, dw, dw, b, b)


def _ffn_mid_bwd(z, da, dw, b, name):
    T, F2 = z.shape
    F = F2 // 2
    K = dw.shape[0]
    H = FFN_HALO
    tr = _tile(T, 512, H)
    cw = _tile(F, 512, 128)
    nc = F // cw
    nt = T // tr
    rb = tr // H
    ext = tr + H

    def body(gp_ref, gc_ref, gn_ref, up_ref, uc_ref, un_ref, dac_ref, dan_ref, wg_ref, wu_ref, bg_ref, bu_ref,
             dzg_ref, dzu_ref, dwg_ref, dwu_ref, dbg_ref, dbu_ref):
        t = pl.program_id(1)
        first = t == 0
        last = t == nt - 1

        @pl.when(first)
        def _():
            for r in (dwg_ref, dwu_ref, dbg_ref, dbu_ref):
                r[...] = jnp.zeros_like(r)

        def window(p_ref, c_ref, n_ref):
            p = jnp.where(first, 0.0, p_ref[...].astype(F32))
            n = jnp.where(last, 0.0, n_ref[...].astype(F32))
            return jnp.concatenate([p, c_ref[...].astype(F32), n], axis=0)

        def conv_ext(xx, w, bv):
            acc = bv + w[K - 1:K, :] * _rows(xx, H, ext)
            for k in range(K - 1):
                acc = acc + w[k:k + 1, :] * _rows(xx, H - (K - 1 - k), ext)
            return acc

        xg = window(gp_ref, gc_ref, gn_ref)
        xu = window(up_ref, uc_ref, un_ref)
        wg, wu = wg_ref[...], wu_ref[...]
        gv = conv_ext(xg, wg, bg_ref[...])
        uv = conv_ext(xu, wu, bu_ref[...])
        da = jnp.concatenate([dac_ref[...].astype(F32), jnp.where(last, 0.0, dan_ref[...].astype(F32))], axis=0)
        s = _sigmoid(gv)
        dg = da * uv * (s * (1.0 + gv * (1.0 - s)))
        du = da * (gv * s)

        def finish(dzc, xx, w, dz_ref, dw_ref, db_ref):
            cur = _rows(dzc, 0, tr)
            db_ref[...] += jnp.sum(cur, axis=0, keepdims=True)
            parts = [jnp.sum(cur * _rows(xx, H - (K - 1 - k), tr), axis=0, keepdims=True) for k in range(K)]
            dw_ref[...] += jnp.concatenate(parts, axis=0)
            dz = w[K - 1:K, :] * cur
            for k in range(K - 1):
                dz = dz + w[k:k + 1, :] * _rows(dzc, K - 1 - k, tr)
            dz_ref[...] = dz.astype(BF16)

        finish(dg, xg, wg, dzg_ref, dwg_ref, dbg_ref)
        finish(du, xu, wu, dzu_ref, dwu_ref, dbu_ref)

    def prev(t):
        return jnp.maximum(t * rb - 1, 0)

    def nxt(t):
        return jnp.minimum((t + 1) * rb, T // H - 1)

    def zspecs(off):
        return [pl.BlockSpec((H, cw), lambda j, t: (prev(t), j + off)),
                pl.BlockSpec((tr, cw), lambda j, t: (t, j + off)),
                pl.BlockSpec((H, cw), lambda j, t: (nxt(t), j + off))]

    in_specs = zspecs(0) + zspecs(nc) + [
        pl.BlockSpec((tr, cw), lambda j, t: (t, j)),
        pl.BlockSpec((H, cw), lambda j, t: (nxt(t), j)),
        pl.BlockSpec((K, cw), lambda j, t: (0, j)),
        pl.BlockSpec((K, cw), lambda j, t: (0, j + nc)),
        pl.BlockSpec((1, cw), lambda j, t: (0, j)),
        pl.BlockSpec((1, cw), lambda j, t: (0, j + nc)),
    ]
    out_shape = (jax.ShapeDtypeStruct((T, F), BF16), jax.ShapeDtypeStruct((T, F), BF16),
                 jax.ShapeDtypeStruct((K, F), F32), jax.ShapeDtypeStruct((K, F), F32),
                 jax.ShapeDtypeStruct((1, F), F32), jax.ShapeDtypeStruct((1, F), F32))
    out_specs = (pl.BlockSpec((tr, cw), lambda j, t: (t, j)), pl.BlockSpec((tr, cw), lambda j, t: (t, j)),
                 pl.BlockSpec((K, cw), lambda j, t: (0, j)), pl.BlockSpec((K, cw), lambda j, t: (0, j)),
                 pl.BlockSpec((1, cw), lambda j, t: (0, j)), pl.BlockSpec((1, cw), lambda j, t: (0, j)))
    dzg, dzu, dwg, dwu, dbg, dbu = pl.pallas_call(
        body, name=name, grid=(nc, nt), out_shape=out_shape, in_specs=in_specs, out_specs=out_specs,
        compiler_params=_cparams())(z, z, z, z, z, z, da, da, dw, dw, b, b)
    return (jnp.concatenate([dzg, dzu], axis=1), jnp.concatenate([dwg, dwu], axis=1),
            jnp.concatenate([dbg, dbu], axis=1))


CONV_HALO = 32


def _convmix_fwd(p, dw, dwb, lng, lnb, name):
    T, D2 = p.shape
    D = D2 // 2
    K = dw.shape[0]
    H = CONV_HALO
    tr = _tile(T, 256, H)
    rb = tr // H

    def body(pc_ref, ph_ref, w_ref, wb_ref, lng_ref, lnb_ref, zc_ref, s_ref):
        first = pl.program_id(0) == 0
        pp = jnp.concatenate([jnp.where(first, 0.0, ph_ref[...].astype(F32)), pc_ref[...].astype(F32)], axis=0)
        zg = pp[:, :D] * _sigmoid(pp[:, D:])
        w = w_ref[...]
        acc = wb_ref[...] + w[K - 1:K, :] * _rows(zg, H, tr)
        for k in range(K - 1):
            acc = acc + w[k:k + 1, :] * _rows(zg, H - (K - 1 - k), tr)
        zc_ref[...] = acc
        mu = jnp.mean(acc, axis=-1, keepdims=True)
        xc = acc - mu
        rstd = lax.rsqrt(jnp.mean(xc * xc, axis=-1, keepdims=True) + NORM_EPS)
        ln = xc * rstd * lng_ref[...] + lnb_ref[...]
        s_ref[...] = (ln * _sigmoid(ln)).astype(BF16)

    in_specs = [pl.BlockSpec((tr, D2), lambda t: (t, 0)),
                pl.BlockSpec((H, D2), lambda t: (jnp.maximum(t * rb - 1, 0), 0)),
                pl.BlockSpec((K, D), lambda t: (0, 0)), _vec_spec(D), _vec_spec(D), _vec_spec(D)]
    row = pl.BlockSpec((tr, D), lambda t: (t, 0))
    return pl.pallas_call(
        body, name=name, grid=(T // tr,),
        out_shape=(jax.ShapeDtypeStruct((T, D), F32), jax.ShapeDtypeStruct((T, D), BF16)),
        in_specs=in_specs, out_specs=(row, row), compiler_params=_cparams())(p, p, dw, dwb, lng, lnb)


def _convmix_bwd(ds, zc, p, dw, lng, lnb, name):
    T, D = zc.shape
    D2 = 2 * D
    K = dw.shape[0]
    H = CONV_HALO
    tr = _tile(T, 256, H)
    rb = tr // H
    nt = T // tr
    ext = tr + H

    def body(dsc_ref, dsn_ref, zcc_ref, zcn_ref, pp_ref, pc_ref, w_ref, lng_ref, lnb_ref,
             dp_ref, dbin_ref, ddw_ref, ddwb_ref, dlng_ref, dlnb_ref):
        t = pl.program_id(0)
        first = t == 0
        last = t == nt - 1

        @pl.when(first)
        def _():
            for r in (dbin_ref, ddw_ref, ddwb_ref, dlng_ref, dlnb_ref):
                r[...] = jnp.zeros_like(r)

        dsv = jnp.concatenate([dsc_ref[...].astype(F32), jnp.where(last, 0.0, dsn_ref[...].astype(F32))], axis=0)
        zcv = jnp.concatenate([zcc_ref[...], zcn_ref[...]], axis=0)
        lg = lng_ref[...]
        mu = jnp.mean(zcv, axis=-1, keepdims=True)
        xc = zcv - mu
        rstd = lax.rsqrt(jnp.mean(xc * xc, axis=-1, keepdims=True) + NORM_EPS)
        xh = xc * rstd
        ln = xh * lg + lnb_ref[...]
        sg = _sigmoid(ln)
        dln = dsv * (sg * (1.0 + ln * (1.0 - sg)))
        dlnc = _rows(dln, 0, tr)
        dlnb_ref[...] += jnp.sum(dlnc, axis=0, keepdims=True)
        dlng_ref[...] += jnp.sum(dlnc * _rows(xh, 0, tr), axis=0, keepdims=True)
        dxh = dln * lg
        dzc = rstd * (dxh - jnp.mean(dxh, axis=-1, keepdims=True) - xh * jnp.mean(dxh * xh, axis=-1, keepdims=True))
        dzcc = _rows(dzc, 0, tr)
        ddwb_ref[...] += jnp.sum(dzcc, axis=0, keepdims=True)

        pv = jnp.concatenate([jnp.where(first, 0.0, pp_ref[...].astype(F32)), pc_ref[...].astype(F32)], axis=0)
        av, gv = pv[:, :D], pv[:, D:]
        sgg = _sigmoid(gv)
        zg = av * sgg
        w = w_ref[...]
        parts = []
        dzg = w[K - 1:K, :] * dzcc
        for k in range(K):
            parts.append(jnp.sum(dzcc * _rows(zg, H - (K - 1 - k), tr), axis=0, keepdims=True))
            if k < K - 1:
                dzg = dzg + w[k:k + 1, :] * _rows(dzc, K - 1 - k, tr)
        ddw_ref[...] += jnp.concatenate(parts, axis=0)
        ac, sc_ = _rows(av, H, tr), _rows(sgg, H, tr)
        dpa = dzg * sc_
        dpg = dzg * ac * sc_ * (1.0 - sc_)
        dpv = jnp.concatenate([dpa, dpg], axis=1)
        dp_ref[...] = dpv.astype(BF16)
        dbin_ref[...] += jnp.sum(dpv, axis=0, keepdims=True)

    def prev(t):
        return jnp.maximum(t * rb - 1, 0)

    def nxt(t):
        return jnp.minimum((t + 1) * rb, T // H - 1)

    in_specs = [pl.BlockSpec((tr, D), lambda t: (t, 0)), pl.BlockSpec((H, D), lambda t: (nxt(t), 0)),
                pl.BlockSpec((tr, D), lambda t: (t, 0)), pl.BlockSpec((H, D), lambda t: (nxt(t), 0)),
                pl.BlockSpec((H, D2), lambda t: (prev(t), 0)), pl.BlockSpec((tr, D2), lambda t: (t, 0)),
                pl.BlockSpec((K, D), lambda t: (0, 0)), _vec_spec(D), _vec_spec(D)]
    out_shape = (jax.ShapeDtypeStruct((T, D2), BF16), jax.ShapeDtypeStruct((1, D2), F32),
                 jax.ShapeDtypeStruct((K, D), F32), jax.ShapeDtypeStruct((1, D), F32),
                 jax.ShapeDtypeStruct((1, D), F32), jax.ShapeDtypeStruct((1, D), F32))
    out_specs = (pl.BlockSpec((tr, D2), lambda t: (t, 0)), _vec_spec(D2), pl.BlockSpec((K, D), lambda t: (0, 0)),
                 _vec_spec(D), _vec_spec(D), _vec_spec(D))
    return pl.pallas_call(
        body, name=name, grid=(nt,), out_shape=out_shape, in_specs=in_specs, out_specs=out_specs,
        compiler_params=_cparams())(ds, ds, zc, zc, p, p, dw, lng, lnb)


_INV_SQRT2 = 1.0 / math.sqrt(2.0)
_INV_SQRT2PI = 1.0 / math.sqrt(2.0 * math.pi)


def _gelu(x):
    return 0.5 * x * (1.0 + lax.erf(x * _INV_SQRT2))


def _gelu_grad(x):
    return 0.5 * (1.0 + lax.erf(x * _INV_SQRT2)) + x * jnp.exp(-0.5 * x * x) * _INV_SQRT2PI


def _tril_mask(n, transposed=False):
    r = lax.broadcasted_iota(jnp.int32, (n, n), 0)
    c = lax.broadcasted_iota(jnp.int32, (n, n), 1)
    return (r <= c) if transposed else (r >= c)


def _sgu_fwd(p, lng, lnb, ws, bs_t, name):
    T, S2 = p.shape
    S = S2 // 2
    G, C, _ = ws.shape
    gd = S // G
    cpt = 2 if (T // C) % 2 == 0 else 1
    tr = C * cpt

    def body(p_ref, lng_ref, lnb_ref, ws_ref, bst_ref, m_ref):
        mask = _tril_mask(C)
        lg, lb = lng_ref[...], lnb_ref[...]
        bst = bst_ref[...]
        for ci in range(cpt):
            pc = p_ref[ci * C:(ci + 1) * C, :].astype(F32)
            z = _gelu(pc)
            u, v = z[:, :S], z[:, S:]
            mu = jnp.mean(v, axis=-1, keepdims=True)
            xc = v - mu
            rstd = lax.rsqrt(jnp.mean(xc * xc, axis=-1, keepdims=True) + NORM_EPS)
            vn = (xc * rstd * lg + lb).astype(BF16)
            outs = []
            for g in range(G):
                wm = jnp.where(mask, ws_ref[g], 0.0).astype(BF16)
                vs = _dot(wm, vn[:, g * gd:(g + 1) * gd], 1, 0) + bst[:, g:g + 1]
                outs.append(u[:, g * gd:(g + 1) * gd] * vs)
            m_ref[ci * C:(ci + 1) * C, :] = jnp.concatenate(outs, axis=1).astype(BF16)

    in_specs = [pl.BlockSpec((tr, S2), lambda t: (t, 0)), _vec_spec(S), _vec_spec(S),
                pl.BlockSpec((G, C, C), lambda t: (0, 0, 0)), pl.BlockSpec((C, G), lambda t: (0, 0))]
    return pl.pallas_call(
        body, name=name, grid=(T // tr,), out_shape=jax.ShapeDtypeStruct((T, S), BF16),
        in_specs=in_specs, out_specs=pl.BlockSpec((tr, S), lambda t: (t, 0)),
        compiler_params=_cparams())(p, lng, lnb, ws, bs_t)


def _sgu_bwd(p, dm, lng, lnb, ws, bs_t, name):
    T, S2 = p.shape
    S = S2 // 2
    G, C, _ = ws.shape
    gd = S // G
    nt = T // C

    def body(p_ref, dm_ref, lng_ref, lnb_ref, ws_ref, bst_ref, dp_ref, dbin_ref, dlng_ref, dlnb_ref, dws_ref, dbst_ref):
        @pl.when(pl.program_id(0) == 0)
        def _():
            for r in (dbin_ref, dlng_ref, dlnb_ref, dws_ref, dbst_ref):
                r[...] = jnp.zeros_like(r)

        mask = _tril_mask(C)
        lg, lb = lng_ref[...], lnb_ref[...]
        bst = bst_ref[...]
        pc = p_ref[...].astype(F32)
        dmv = dm_ref[...].astype(F32)
        z = _gelu(pc)
        u, v = z[:, :S], z[:, S:]
        mu = jnp.mean(v, axis=-1, keepdims=True)
        xc = v - mu
        rstd = lax.rsqrt(jnp.mean(xc * xc, axis=-1, keepdims=True) + NORM_EPS)
        vh = xc * rstd
        vn = (vh * lg + lb).astype(BF16)
        dus, dvns, dbcols = [], [], []
        for g in range(G):
            sl = slice(g * gd, (g + 1) * gd)
            wm = jnp.where(mask, ws_ref[g], 0.0).astype(BF16)
            vs = _dot(wm, vn[:, sl], 1, 0) + bst[:, g:g + 1]
            dmg = dmv[:, sl]
            dus.append(dmg * vs)
            dvs = dmg * u[:, sl]
            dbcols.append(jnp.sum(dvs, axis=-1, keepdims=True))
            dvsb = dvs.astype(BF16)
            dws_ref[g] += jnp.where(mask, _dot(dvsb, vn[:, sl], 1, 1), 0.0)
            dvns.append(_dot(wm, dvsb, 0, 0))
        dbst_ref[...] += jnp.concatenate(dbcols, axis=1)
        dvn = jnp.concatenate(dvns, axis=1)
        dlnb_ref[...] += jnp.sum(dvn, axis=0, keepdims=True)
        dlng_ref[...] += jnp.sum(dvn * vh, axis=0, keepdims=True)
        dvh = dvn * lg
        dv = rstd * (dvh - jnp.mean(dvh, axis=-1, keepdims=True) - vh * jnp.mean(dvh * vh, axis=-1, keepdims=True))
        dz = jnp.concatenate([jnp.concatenate(dus, axis=1), dv], axis=1)
        dpv = dz * _gelu_grad(pc)
        dp_ref[...] = dpv.astype(BF16)
        dbin_ref[...] += jnp.sum(dpv, axis=0, keepdims=True)

    in_specs = [pl.BlockSpec((C, S2), lambda t: (t, 0)), pl.BlockSpec((C, S), lambda t: (t, 0)),
                _vec_spec(S), _vec_spec(S), pl.BlockSpec((G, C, C), lambda t: (0, 0, 0)),
                pl.BlockSpec((C, G), lambda t: (0, 0))]
    out_shape = (jax.ShapeDtypeStruct((T, S2), BF16), jax.ShapeDtypeStruct((1, S2), F32),
                 jax.ShapeDtypeStruct((1, S), F32), jax.ShapeDtypeStruct((1, S), F32),
                 jax.ShapeDtypeStruct((G, C, C), F32), jax.ShapeDtypeStruct((C, G), F32))
    out_specs = (pl.BlockSpec((C, S2), lambda t: (t, 0)), _vec_spec(S2), _vec_spec(S), _vec_spec(S),
                 pl.BlockSpec((G, C, C), lambda t: (0, 0, 0)), pl.BlockSpec((C, G), lambda t: (0, 0)))
    return pl.pallas_call(
        body, name=name, grid=(nt,), out_shape=out_shape, in_specs=in_specs, out_specs=out_specs,
        compiler_params=_cparams())(p, dm, lng, lnb, ws, bs_t)


def _alibi_slope(h, n_heads):
    return 2.0 ** (-8.0 * (h + 1) / n_heads)


def _attn_scores(q_h, k_h, slope, sink, first_block):
    B = BLOCK
    qi = lax.broadcasted_iota(jnp.int32, (B, 2 * B), 0)
    kj = lax.broadcasted_iota(jnp.int32, (B, 2 * B), 1)
    dist = qi + B - kj
    valid = (dist >= 0) & (dist < B) & jnp.logical_not(first_block & (kj < B))
    s = _dot(q_h, k_h, 1, 1) * (HEAD_DIM ** -0.5) - slope * dist.astype(F32)
    s = jnp.where(valid, s, NEG_INF)
    mx = jnp.maximum(jnp.max(s, axis=-1, keepdims=True), sink)
    e = jnp.exp(s - mx)
    es = jnp.exp(sink - mx)
    denom = jnp.sum(e, axis=-1, keepdims=True) + es
    return e, es, denom


def _attn_fwd(qkv, sinks, n_heads, name):
    T = qkv.shape[0]
    B = BLOCK
    HQ = n_heads * HEAD_DIM
    KVW = N_KV * HEAD_DIM
    group = n_heads // N_KV
    nb = T // B
    kcol = HQ // KVW

    def body(q_ref, kp_ref, kc_ref, vp_ref, vc_ref, sink_ref, o_ref):
        first = pl.program_id(0) == 0
        q = q_ref[...]
        k2 = jnp.concatenate([kp_ref[...], kc_ref[...]], axis=0)
        v2 = jnp.concatenate([vp_ref[...], vc_ref[...]], axis=0)
        sk = sink_ref[...]
        outs = []
        for h in range(n_heads):
            kv = h // group
            e, es, denom = _attn_scores(q[:, h * HEAD_DIM:(h + 1) * HEAD_DIM], k2[:, kv * HEAD_DIM:(kv + 1) * HEAD_DIM],
                                        _alibi_slope(h, n_heads), sk[:, h:h + 1], first)
            pr = (e / denom).astype(BF16)
            outs.append(_dot(pr, v2[:, kv * HEAD_DIM:(kv + 1) * HEAD_DIM], 1, 0))
        o_ref[...] = jnp.concatenate(outs, axis=1).astype(BF16)

    def prev(n):
        return jnp.maximum(n - 1, 0)

    in_specs = [pl.BlockSpec((B, HQ), lambda n: (n, 0)),
                pl.BlockSpec((B, KVW), lambda n: (prev(n), kcol)), pl.BlockSpec((B, KVW), lambda n: (n, kcol)),
                pl.BlockSpec((B, KVW), lambda n: (prev(n), kcol + 1)), pl.BlockSpec((B, KVW), lambda n: (n, kcol + 1)),
                _vec_spec(n_heads)]
    return pl.pallas_call(
        body, name=name, grid=(nb,), out_shape=jax.ShapeDtypeStruct((T, HQ), BF16),
        in_specs=in_specs, out_specs=pl.BlockSpec((B, HQ), lambda n: (n, 0)),
        compiler_params=_cparams())(qkv, qkv, qkv, qkv, qkv, sinks)


def _attn_bwd(qkv, do, sinks, n_heads, name):
    T = qkv.shape[0]
    B = BLOCK
    HQ = n_heads * HEAD_DIM
    KVW = N_KV * HEAD_DIM
    group = n_heads // N_KV
    nb = T // B
    kcol = HQ // KVW
    scale = HEAD_DIM ** -0.5

    def body(q_ref, kp_ref, kc_ref, vp_ref, vc_ref, do_ref, sink_ref,
             dq_ref, dk_ref, dv_ref, dbq_ref, dbk_ref, dbv_ref, dsink_ref, ck_ref, cv_ref):
        n = pl.program_id(0)
        first = n == 0

        @pl.when(first)
        def _():
            for r in (dbq_ref, dbk_ref, dbv_ref, dsink_ref, ck_ref, cv_ref):
                r[...] = jnp.zeros_like(r)

        @pl.when(n < nb)
        def _():
            q = q_ref[...]
            dov = do_ref[...]
            k2 = jnp.concatenate([kp_ref[...], kc_ref[...]], axis=0)
            v2 = jnp.concatenate([vp_ref[...], vc_ref[...]], axis=0)
            sk = sink_ref[...]
            dqs, dsinks = [], []
            dks = [None] * N_KV
            dvs = [None] * N_KV
            for h in range(n_heads):
                kv = h // group
                hs = slice(h * HEAD_DIM, (h + 1) * HEAD_DIM)
                ks = slice(kv * HEAD_DIM, (kv + 1) * HEAD_DIM)
                e, es, denom = _attn_scores(q[:, hs], k2[:, ks], _alibi_slope(h, n_heads), sk[:, h:h + 1], first)
                inv = 1.0 / denom
                pr = e * inv
                dp = _dot(dov[:, hs], v2[:, ks], 1, 1)
                dsum = jnp.sum(pr * dp, axis=-1, keepdims=True)
                ds = (pr * (dp - dsum) * scale).astype(BF16)
                dsinks.append(-jnp.sum(es * inv * dsum, axis=0, keepdims=True))
                dqs.append(_dot(ds, k2[:, ks], 1, 0))
                dk_h = _dot(ds, q[:, hs], 0, 0)
                dv_h = _dot(pr.astype(BF16), dov[:, hs], 0, 0)
                dks[kv] = dk_h if dks[kv] is None else dks[kv] + dk_h
                dvs[kv] = dv_h if dvs[kv] is None else dvs[kv] + dv_h
            dq = jnp.concatenate(dqs, axis=1)
            dq_ref[...] = dq.astype(BF16)
            dbq_ref[...] += jnp.sum(dq, axis=0, keepdims=True)
            dsink_ref[...] += jnp.concatenate(dsinks, axis=1)
            dk2 = jnp.concatenate(dks, axis=1)
            dv2 = jnp.concatenate(dvs, axis=1)
            dbk_ref[...] += jnp.sum(dk2, axis=0, keepdims=True)
            dbv_ref[...] += jnp.sum(dv2, axis=0, keepdims=True)
            dk_ref[...] = (ck_ref[...] + dk2[:B]).astype(BF16)
            dv_ref[...] = (cv_ref[...] + dv2[:B]).astype(BF16)
            ck_ref[...] = dk2[B:]
            cv_ref[...] = dv2[B:]

        @pl.when(n == nb)
        def _():
            dk_ref[...] = ck_ref[...].astype(BF16)
            dv_ref[...] = cv_ref[...].astype(BF16)

    def cur(n):
        return jnp.minimum(n, nb - 1)

    def prev(n):
        return jnp.maximum(cur(n) - 1, 0)

    def outp(n):
        return jnp.maximum(n - 1, 0)

    in_specs = [pl.BlockSpec((B, HQ), lambda n: (cur(n), 0)),
                pl.BlockSpec((B, KVW), lambda n: (prev(n), kcol)), pl.BlockSpec((B, KVW), lambda n: (cur(n), kcol)),
                pl.BlockSpec((B, KVW), lambda n: (prev(n), kcol + 1)), pl.BlockSpec((B, KVW), lambda n: (cur(n), kcol + 1)),
                pl.BlockSpec((B, HQ), lambda n: (cur(n), 0)), _vec_spec(n_heads)]
    out_shape = (jax.ShapeDtypeStruct((T, HQ), BF16), jax.ShapeDtypeStruct((T, KVW), BF16),
                 jax.ShapeDtypeStruct((T, KVW), BF16), jax.ShapeDtypeStruct((1, HQ), F32),
                 jax.ShapeDtypeStruct((1, KVW), F32), jax.ShapeDtypeStruct((1, KVW), F32),
                 jax.ShapeDtypeStruct((1, n_heads), F32))
    out_specs = (pl.BlockSpec((B, HQ), lambda n: (cur(n), 0)), pl.BlockSpec((B, KVW), lambda n: (outp(n), 0)),
                 pl.BlockSpec((B, KVW), lambda n: (outp(n), 0)), _vec_spec(HQ), _vec_spec(KVW), _vec_spec(KVW),
                 _vec_spec(n_heads))
    return pl.pallas_call(
        body, name=name, grid=(nb + 1,), out_shape=out_shape, in_specs=in_specs, out_specs=out_specs,
        scratch_shapes=[pltpu.VMEM((B, KVW), F32), pltpu.VMEM((B, KVW), F32)],
        compiler_params=_cparams())(qkv, qkv, qkv, qkv, qkv, do, sinks)


def _sum8(r, name):
    _, R, C = r.shape
    tr = _tile(R, 512, 8)

    def body(r_ref, o_ref):
        acc = r_ref[0]
        for d in range(1, N_DEV):
            acc = acc + r_ref[d]
        o_ref[...] = acc

    return pl.pallas_call(
        body, name=name, grid=(R // tr,), out_shape=jax.ShapeDtypeStruct((R, C), F32),
        in_specs=[pl.BlockSpec((N_DEV, tr, C), lambda i: (0, i, 0))],
        out_specs=pl.BlockSpec((tr, C), lambda i: (i, 0)), compiler_params=_cparams())(r)


def _adamw(w, g, m, v, name):
    R, C = w.shape
    tr = _tile(R, 512, 8)
    c1 = 1.0 - ADAM_B1 ** ADAM_STEP
    c2 = 1.0 - ADAM_B2 ** ADAM_STEP

    def body(w_ref, g_ref, m_ref, v_ref, d_ref, nm_ref, nv_ref):
        gv = g_ref[...]
        nm = ADAM_B1 * m_ref[...] + (1.0 - ADAM_B1) * gv
        nv = ADAM_B2 * v_ref[...] + (1.0 - ADAM_B2) * (gv * gv)
        nm_ref[...] = nm
        nv_ref[...] = nv
        d_ref[...] = -ADAM_LR * ((nm / c1) / (jnp.sqrt(nv / c2) + ADAM_EPS) + ADAM_WD * w_ref[...])

    spec = pl.BlockSpec((tr, C), lambda i: (i, 0))
    shp = jax.ShapeDtypeStruct((R, C), F32)
    return pl.pallas_call(
        body, name=name, grid=(R // tr,), out_shape=(shp, shp, shp), in_specs=[spec] * 4, out_specs=(spec,) * 3,
        compiler_params=_cparams())(w, g, m, v)


def _adamw_nd(w, g, m, v, name):
    shape = w.shape
    c = shape[-1]
    f = lambda a: a.reshape(-1, c)
    d, nm, nv = _adamw(f(w), f(g), f(m), f(v), name)
    return d.reshape(shape), nm.reshape(shape), nv.reshape(shape)


def _pack(arrs, width):
    flat = jnp.concatenate([a.reshape(-1).astype(F32) for a in arrs])
    n = flat.shape[0]
    quantum = 8 * width
    total = -(-n // quantum) * quantum
    return jnp.pad(flat, (0, total - n)).reshape(-1, width)


def _unpack(flat, shapes):
    out, off = [], 0
    for s in shapes:
        n = math.prod(s)
        out.append(flat[off:off + n].reshape(s))
        off += n
    return out


def kernel(x, c, norm1_g, norm2_g, ada_w, ada_b, attn_wqkv, attn_bqkv, attn_sinks, attn_wo, attn_bo, conv_w_in, conv_b_in, conv_dw, conv_dw_b, conv_ln_g, conv_ln_b, conv_w_out, conv_b_out, sgu_w_in, sgu_b_in, sgu_ln_g, sgu_ln_b, sgu_ws, sgu_bs, sgu_w_out, sgu_b_out, ffn_w_in, ffn_dw, ffn_dw_b, ffn_w_out, final_g, loss_target, m_norm1_g, m_norm2_g, m_ada_w, m_ada_b, m_attn_wqkv, m_attn_bqkv, m_attn_sinks, m_attn_wo, m_attn_bo, m_conv_w_in, m_conv_b_in, m_conv_dw, m_conv_dw_b, m_conv_ln_g, m_conv_ln_b, m_conv_w_out, m_conv_b_out, m_sgu_w_in, m_sgu_b_in, m_sgu_ln_g, m_sgu_ln_b, m_sgu_ws, m_sgu_bs, m_sgu_w_out, m_sgu_b_out, m_ffn_w_in, m_ffn_dw, m_ffn_dw_b, m_ffn_w_out, m_final_g, v_norm1_g, v_norm2_g, v_ada_w, v_ada_b, v_attn_wqkv, v_attn_bqkv, v_attn_sinks, v_attn_wo, v_attn_bo, v_conv_w_in, v_conv_b_in, v_conv_dw, v_conv_dw_b, v_conv_ln_g, v_conv_ln_b, v_conv_w_out, v_conv_b_out, v_sgu_w_in, v_sgu_b_in, v_sgu_ln_g, v_sgu_ln_b, v_sgu_ws, v_sgu_bs, v_sgu_w_out, v_sgu_b_out, v_ffn_w_in, v_ffn_dw, v_ffn_dw_b, v_ffn_w_out, v_final_g):
    W = dict(norm1_g=norm1_g, norm2_g=norm2_g, ada_w=ada_w, ada_b=ada_b, attn_wqkv=attn_wqkv, attn_bqkv=attn_bqkv, attn_sinks=attn_sinks, attn_wo=attn_wo, attn_bo=attn_bo, conv_w_in=conv_w_in, conv_b_in=conv_b_in, conv_dw=conv_dw, conv_dw_b=conv_dw_b, conv_ln_g=conv_ln_g, conv_ln_b=conv_ln_b, conv_w_out=conv_w_out, conv_b_out=conv_b_out, sgu_w_in=sgu_w_in, sgu_b_in=sgu_b_in, sgu_ln_g=sgu_ln_g, sgu_ln_b=sgu_ln_b, sgu_ws=sgu_ws, sgu_bs=sgu_bs, sgu_w_out=sgu_w_out, sgu_b_out=sgu_b_out, ffn_w_in=ffn_w_in, ffn_dw=ffn_dw, ffn_dw_b=ffn_dw_b, ffn_w_out=ffn_w_out, final_g=final_g)
    MOM = dict(norm1_g=m_norm1_g, norm2_g=m_norm2_g, ada_w=m_ada_w, ada_b=m_ada_b, attn_wqkv=m_attn_wqkv, attn_bqkv=m_attn_bqkv, attn_sinks=m_attn_sinks, attn_wo=m_attn_wo, attn_bo=m_attn_bo, conv_w_in=m_conv_w_in, conv_b_in=m_conv_b_in, conv_dw=m_conv_dw, conv_dw_b=m_conv_dw_b, conv_ln_g=m_conv_ln_g, conv_ln_b=m_conv_ln_b, conv_w_out=m_conv_w_out, conv_b_out=m_conv_b_out, sgu_w_in=m_sgu_w_in, sgu_b_in=m_sgu_b_in, sgu_ln_g=m_sgu_ln_g, sgu_ln_b=m_sgu_ln_b, sgu_ws=m_sgu_ws, sgu_bs=m_sgu_bs, sgu_w_out=m_sgu_w_out, sgu_b_out=m_sgu_b_out, ffn_w_in=m_ffn_w_in, ffn_dw=m_ffn_dw, ffn_dw_b=m_ffn_dw_b, ffn_w_out=m_ffn_w_out, final_g=m_final_g)
    VAR = dict(norm1_g=v_norm1_g, norm2_g=v_norm2_g, ada_w=v_ada_w, ada_b=v_ada_b, attn_wqkv=v_attn_wqkv, attn_bqkv=v_attn_bqkv, attn_sinks=v_attn_sinks, attn_wo=v_attn_wo, attn_bo=v_attn_bo, conv_w_in=v_conv_w_in, conv_b_in=v_conv_b_in, conv_dw=v_conv_dw, conv_dw_b=v_conv_dw_b, conv_ln_g=v_conv_ln_g, conv_ln_b=v_conv_ln_b, conv_w_out=v_conv_w_out, conv_b_out=v_conv_b_out, sgu_w_in=v_sgu_w_in, sgu_b_in=v_sgu_b_in, sgu_ln_g=v_sgu_ln_g, sgu_ln_b=v_sgu_ln_b, sgu_ws=v_sgu_ws, sgu_bs=v_sgu_bs, sgu_w_out=v_sgu_w_out, sgu_b_out=v_sgu_b_out, ffn_w_in=v_ffn_w_in, ffn_dw=v_ffn_dw, ffn_dw_b=v_ffn_dw_b, ffn_w_out=v_ffn_w_out, final_g=v_final_g)
    ORDER = list(W)

    _, T, D = x.shape
    depth = norm1_g.shape[0]
    n_heads = D // HEAD_DIM
    me = 4 * lax.axis_index("x") + 2 * lax.axis_index("y") + lax.axis_index("c")
    x0 = x.reshape(T, D)
    tgt = loss_target.reshape(T, D)

    small_sharded = ["attn_bqkv", "attn_bo", "conv_dw", "sgu_b_in", "sgu_ln_g", "sgu_ln_b", "sgu_b_out", "ffn_dw"]
    s_in = [c] + [W[n] for n in small_sharded]
    s_shapes = [a.shape for a in s_in]
    gathered_small = _all_gather(_pack(s_in, 128), "gather_small").reshape(N_DEV, -1)
    per_dev = [_unpack(gathered_small[d], s_shapes) for d in range(N_DEV)]

    def full(idx):
        return jnp.concatenate([per_dev[d][idx] for d in range(N_DEV)], axis=-1)

    c_all = jnp.concatenate([per_dev[d][0] for d in range(N_DEV)], axis=0)
    F_small = {n: full(1 + i) for i, n in enumerate(small_sharded)}

    c_act = c_all * jax.nn.sigmoid(c_all)
    c_pad = jnp.pad(c_act, ((0, 8), (0, 0))).astype(BF16)
    n_ada = ada_w.shape[-1]
    ada_cols = lax.dynamic_slice_in_dim(ada_b, me * n_ada, n_ada, axis=1)
    mod_loc = jnp.stack([_mm(c_pad, ada_w[i].astype(BF16), "nn", "ada_mod", bias=ada_cols[i:i + 1])
                         for i in range(depth)])
    mod_all = _all_gather(mod_loc, "gather_mod")
    mod_mine = lax.dynamic_index_in_dim(mod_all, me, axis=2, keepdims=False)
    mod = jnp.transpose(mod_mine, (1, 0, 2)).reshape(depth, 6, 1, D)

    pieces = []
    for n in ["attn_wqkv", "conv_w_in", "sgu_w_in", "ffn_w_in"]:
        for l in range(W[n].shape[0]):
            pieces.append((n, l, W[n][l].T))
    for n in ["attn_wo", "conv_w_out", "sgu_w_out", "ffn_w_out"]:
        for l in range(W[n].shape[0]):
            pieces.append((n, l, W[n][l]))
    rows = [p[2].shape[0] for p in pieces]
    offs = [sum(rows[:i]) for i in range(len(rows))]
    packed_w = jnp.concatenate([p[2].astype(BF16) for p in pieces], axis=0)
    gathered_w = _all_gather(packed_w, "gather_weights")
    WF = {}
    for (n, l, _), r, o in zip(pieces, rows, offs):
        WF[(n, l)] = gathered_w[:, o:o + r, :].reshape(N_DEV * r, D)

    def vec(a):
        return a.reshape(1, -1)

    saved = []
    xs, y_prev, gate_prev = x0, None, None
    for i in range(depth):
        sh1, sc1, g1, sh2, sc2, g2 = [mod[i, k] for k in range(6)]
        kind, j = i % 3, i // 3
        st = dict(kind=kind, j=j)
        if y_prev is None:
            h1 = _norm_mod_fwd(xs, vec(norm1_g[i]), sc1, sh1, "norm1_fwd")
        else:
            xs, h1 = _norm_mod_fwd(xs, vec(norm1_g[i]), sc1, sh1, "norm1_fwd", y=y_prev, gate=gate_prev)
        st.update(x_in=xs, h1=h1)
        if kind == 0:
            qkv = _mm(h1, WF[("attn_wqkv", j)], "nt", "attn_qkv", out_dtype=BF16, bias=vec(F_small["attn_bqkv"][j]))
            o = _attn_fwd(qkv, vec(attn_sinks[j]), n_heads, "attn_fwd")
            y1 = _mm(o, WF[("attn_wo", j)], "nn", "attn_out", bias=vec(F_small["attn_bo"][j]))
            st.update(qkv=qkv, o=o)
        elif kind == 1:
            p = _mm(h1, WF[("conv_w_in", j)], "nt", "conv_in", out_dtype=BF16, bias=vec(conv_b_in[j]))
            zc, s = _convmix_fwd(p, F_small["conv_dw"][j], vec(conv_dw_b[j]), vec(conv_ln_g[j]), vec(conv_ln_b[j]),
                                 "convmix_fwd")
            y1 = _mm(s, WF[("conv_w_out", j)], "nn", "conv_out", bias=vec(conv_b_out[j]))
            st.update(p=p, zc=zc, s=s)
        else:
            p = _mm(h1, WF[("sgu_w_in", j)], "nt", "sgu_in", out_dtype=BF16, bias=vec(F_small["sgu_b_in"][j]))
            mm_ = _sgu_fwd(p, vec(F_small["sgu_ln_g"][j]), vec(F_small["sgu_ln_b"][j]), sgu_ws[j], sgu_bs[j].T, "sgu_fwd")
            y1 = _mm(mm_, WF[("sgu_w_out", j)], "nn", "sgu_out", bias=vec(F_small["sgu_b_out"][j]))
            st.update(p=p, m=mm_)
        xs, h2 = _norm_mod_fwd(xs, vec(norm2_g[i]), sc2, sh2, "norm2_fwd", y=y1, gate=g1)
        z = _mm(h2, WF[("ffn_w_in", i)], "nt", "ffn_in", out_dtype=BF16)
        a = _ffn_mid_fwd(z, F_small["ffn_dw"][i], vec(ffn_dw_b[i]), "ffn_mid_fwd")
        y2 = _mm(a, WF[("ffn_w_out", i)], "nn", "ffn_out")
        st.update(y1=y1, x_mid=xs, h2=h2, z=z, a=a, y2=y2)
        saved.append(st)
        y_prev, gate_prev = y2, g2

    dx, d_final_g, loss_row = _final_loss(xs, y_prev, gate_prev, tgt, vec(final_g), "final_loss")
    loss = lax.psum(loss_row[0, 0], ("x", "y", "c"))

    G = {n: [None] * W[n].shape[0] for n in ORDER if n != "final_g"}
    GW = {}
    dmod = [None] * depth
    for i in reversed(range(depth)):
        st = saved[i]
        sh1, sc1, g1, sh2, sc2, g2 = [mod[i, k] for k in range(6)]
        kind, j = st["kind"], st["j"]
        dy2, dg2, _ = _gate_bwd(dx, st["y2"], g2, "gate_bwd")
        GW[("ffn_w_out", i)] = _mm(st["a"], dy2, "tn", "ffn_out_dw")
        da = _mm(dy2, WF[("ffn_w_out", i)], "nt", "ffn_out_dx", out_dtype=BF16)
        dz, ddw, ddwb = _ffn_mid_bwd(st["z"], da, F_small["ffn_dw"][i], vec(ffn_dw_b[i]), "ffn_mid_bwd")
        G["ffn_dw"][i], G["ffn_dw_b"][i] = ddw, ddwb[0]
        GW[("ffn_w_in", i)] = _mm(dz, st["h2"], "tn", "ffn_in_dw")
        dh2 = _mm(dz, WF[("ffn_w_in", i)], "nn", "ffn_in_dx")
        dx, dn2, dsc2, dsh2 = _norm_mod_bwd(dh2, st["x_mid"], vec(norm2_g[i]), sc2, sh2, dx, "norm_bwd")
        G["norm2_g"][i] = dn2[0]
        dy1, dg1, dbo = _gate_bwd(dx, st["y1"], g1, "gate_bwd")
        if kind == 0:
            G["attn_bo"][j] = dbo[0]
            GW[("attn_wo", j)] = _mm(st["o"], dy1, "tn", "attn_out_dw")
            do = _mm(dy1, WF[("attn_wo", j)], "nt", "attn_out_dx", out_dtype=BF16)
            dq, dk, dv, dbq, dbk, dbv, dsk = _attn_bwd(st["qkv"], do, vec(attn_sinks[j]), n_heads, "attn_bwd")
            dqkv = jnp.concatenate([dq, dk, dv], axis=1)
            G["attn_bqkv"][j] = jnp.concatenate([dbq, dbk, dbv], axis=1)[0]
            G["attn_sinks"][j] = dsk[0]
            GW[("attn_wqkv", j)] = _mm(dqkv, st["h1"], "tn", "attn_qkv_dw")
            dh1 = _mm(dqkv, WF[("attn_wqkv", j)], "nn", "attn_qkv_dx")
        elif kind == 1:
            G["conv_b_out"][j] = dbo[0]
            GW[("conv_w_out", j)] = _mm(st["s"], dy1, "tn", "conv_out_dw")
            ds = _mm(dy1, WF[("conv_w_out", j)], "nt", "conv_out_dx")
            dp, dbin, ddw, ddwb, dlng, dlnb = _convmix_bwd(ds, st["zc"], st["p"], F_small["conv_dw"][j],
                                                            vec(conv_ln_g[j]), vec(conv_ln_b[j]), "convmix_bwd")
            G["conv_b_in"][j], G["conv_dw"][j], G["conv_dw_b"][j] = dbin[0], ddw, ddwb[0]
            G["conv_ln_g"][j], G["conv_ln_b"][j] = dlng[0], dlnb[0]
            GW[("conv_w_in", j)] = _mm(dp, st["h1"], "tn", "conv_in_dw")
            dh1 = _mm(dp, WF[("conv_w_in", j)], "nn", "conv_in_dx")
        else:
            G["sgu_b_out"][j] = dbo[0]
            GW[("sgu_w_out", j)] = _mm(st["m"], dy1, "tn", "sgu_out_dw")
            dm = _mm(dy1, WF[("sgu_w_out", j)], "nt", "sgu_out_dx")
            dp, dbin, dlng, dlnb, dws, dbst = _sgu_bwd(st["p"], dm, vec(F_small["sgu_ln_g"][j]),
                                                       vec(F_small["sgu_ln_b"][j]), sgu_ws[j], sgu_bs[j].T, "sgu_bwd")
            G["sgu_b_in"][j], G["sgu_ln_g"][j], G["sgu_ln_b"][j] = dbin[0], dlng[0], dlnb[0]
            G["sgu_ws"][j], G["sgu_bs"][j] = dws, dbst.T
            GW[("sgu_w_in", j)] = _mm(dp, st["h1"], "tn", "sgu_in_dw")
            dh1 = _mm(dp, WF[("sgu_w_in", j)], "nn", "sgu_in_dx")
        dx, dn1, dsc1, dsh1 = _norm_mod_bwd(dh1, st["x_in"], vec(norm1_g[i]), sc1, sh1, dx, "norm_bwd")
        G["norm1_g"][i] = dn1[0]
        dmod[i] = jnp.concatenate([dsh1, dsc1, dg1, dsh2, dsc2, dg2], axis=1)[0]
    grad_x = dx.reshape(x.shape)

    small_names = [n for n in ORDER if n not in
                   ("ada_w", "ada_b", "attn_wqkv", "attn_wo", "conv_w_in", "conv_w_out", "sgu_w_in", "sgu_w_out",
                    "ffn_w_in", "ffn_w_out", "final_g")]
    g_small_full = [jnp.stack(G[n]) for n in small_names] + [d_final_g[0]]
    small_names = small_names + ["final_g"]
    dmod_arr = jnp.stack(dmod)
    g_small_in = g_small_full + [dmod_arr]
    g_shapes = [a.shape for a in g_small_in]
    packed_gs = _pack(g_small_in, 128)
    gathered_gs = _all_gather(packed_gs, "gather_small_grads")
    summed_gs = _sum8(gathered_gs, "sum_small_grads").reshape(-1)
    gsum = dict(zip(small_names + ["ada_b"], _unpack(summed_gs, g_shapes)))

    def local_shard(n, a):
        if n in small_sharded:
            w = W[n].shape[-1]
            return lax.dynamic_slice_in_dim(a, me * w, w, axis=a.ndim - 1)
        return a

    gsum = {n: local_shard(n, a) for n, a in gsum.items()}

    dmod_all = jnp.stack([_unpack(gathered_gs[d].reshape(-1), g_shapes)[-1] for d in range(N_DEV)])
    dmod_cols = lax.dynamic_slice_in_dim(dmod_all, me * n_ada, n_ada, axis=2)
    dmod_pad = jnp.pad(dmod_cols, ((0, 8), (0, 0), (0, 0))).astype(BF16)
    g_ada_w = jnp.stack([_mm(c_pad, dmod_pad[:, i, :], "tn", "ada_dw") for i in range(depth)])

    send = jnp.concatenate([GW[(n, l)].reshape(N_DEV, r, D) for (n, l, _), r in zip(pieces, rows)], axis=1)
    recv = _all_to_all(send, "exchange_weight_grads")
    g_rows = _sum8(recv, "sum_weight_grads")
    big = {}
    for (n, l, loc), r, o in zip(pieces, rows, offs):
        gp = g_rows[o:o + r]
        big.setdefault(n, []).append(gp.T if n.endswith("_in") or n == "attn_wqkv" else gp)
    grads = {n: jnp.stack(v) for n, v in big.items()}
    grads["ada_w"] = g_ada_w
    grads.update(gsum)

    delta, new_m, new_v = {}, {}, {}
    big_names = ["ada_w", "attn_wqkv", "attn_wo", "conv_w_in", "conv_w_out", "sgu_w_in", "sgu_w_out", "ffn_w_in",
                 "ffn_w_out"]
    for n in big_names:
        delta[n], new_m[n], new_v[n] = _adamw_nd(W[n], grads[n], MOM[n], VAR[n], "adamw_" + n)
    rest = [n for n in ORDER if n not in big_names]
    rest_shapes = [W[n].shape for n in rest]
    pk = lambda d: _pack([d[n] for n in rest], 128)
    d_s, m_s, v_s = _adamw(pk(W), pk(grads), pk(MOM), pk(VAR), "adamw_small")
    for n, a, b_, c_ in zip(rest, _unpack(d_s.reshape(-1), rest_shapes), _unpack(m_s.reshape(-1), rest_shapes),
                            _unpack(v_s.reshape(-1), rest_shapes)):
        delta[n], new_m[n], new_v[n] = a, b_, c_

    return (loss, grad_x, *[grads[n] for n in ORDER], *[delta[n] for n in ORDER],
            *[new_m[n] for n in ORDER], *[new_v[n] for n in ORDER])
```

```python
import functools
import math

import jax
import jax.numpy as jnp
from jax import lax
from jax.experimental import pallas as pl
from jax.experimental.pallas import tpu as pltpu

F32 = jnp.float32
BF16 = jnp.bfloat16

N_DEV = 8
HEAD_DIM = 64
N_KV = 4
BLOCK = 128
NORM_EPS = 1e-6
NEG_INF = -1e30
ADAM_LR = 0.001
ADAM_B1 = 0.9
ADAM_B2 = 0.999
ADAM_EPS = 1e-08
ADAM_WD = 0.01
ADAM_STEP = 10
V7X_VMEM_LIMIT = 56 * 1024 * 1024
MESH = pl.DeviceIdType.MESH


def _cparams():
    return pltpu.CompilerParams(vmem_limit_bytes=V7X_VMEM_LIMIT)


def _tile(n, cap, align):
    best = None
    for t in range(align, min(n, cap) + 1, align):
        if n % t == 0:
            best = t
    return best if best is not None else n


def _dot(a, b, ca, cb):
    return lax.dot_general(a, b, (((ca,), (cb,)), ((), ())), preferred_element_type=F32)


def _sigmoid(x):
    return 1.0 / (1.0 + jnp.exp(-x))


def _rows(x, start, n):
    return lax.slice_in_dim(x, start, start + n, axis=0)


def _my_coords():
    return lax.axis_index("x"), lax.axis_index("y"), lax.axis_index("c")


def _peer(m, mx, my, mc):
    px = (mx + ((m >> 2) & 1)) % 2
    py = (my + ((m >> 1) & 1)) % 2
    pc = (mc + (m & 1)) % 2
    return px, py, pc


def _exchange_copies(kind, x_ref, o_ref, send_sems, recv_sems, local_sem):
    mx, my, mc = _my_coords()
    me = 4 * mx + 2 * my + mc
    local = pltpu.make_async_copy(x_ref if kind == "gather" else x_ref.at[me], o_ref.at[me], local_sem)
    remote = []
    for m in range(1, N_DEV):
        px, py, pc = _peer(m, mx, my, mc)
        src = x_ref if kind == "gather" else x_ref.at[4 * px + 2 * py + pc]
        remote.append(pltpu.make_async_remote_copy(
            src_ref=src, dst_ref=o_ref.at[me], send_sem=send_sems.at[m - 1], recv_sem=recv_sems.at[m - 1],
            device_id=(px, py, pc), device_id_type=MESH))
    return local, remote


def _exchange_start(kind, *refs):
    local, remote = _exchange_copies(kind, *refs)
    local.start()
    for cp in remote:
        cp.start()


def _exchange_wait(kind, *refs):
    local, remote = _exchange_copies(kind, *refs)
    for cp in remote:
        cp.wait_recv()
    for cp in remote:
        cp.wait_send()
    local.wait()


_EXCHANGE_SEMS = [pltpu.SemaphoreType.DMA((N_DEV - 1,)), pltpu.SemaphoreType.DMA((N_DEV - 1,)), pltpu.SemaphoreType.DMA]


def _exchange_shape(kind, x):
    return jax.ShapeDtypeStruct(((N_DEV,) + x.shape) if kind == "gather" else x.shape, x.dtype)


def _exchange(kind, x, name):
    def body(x_ref, o_ref, send_sems, recv_sems, local_sem):
        _exchange_start(kind, x_ref, o_ref, send_sems, recv_sems, local_sem)
        _exchange_wait(kind, x_ref, o_ref, send_sems, recv_sems, local_sem)

    return pl.pallas_call(
        body, name=name, out_shape=_exchange_shape(kind, x),
        in_specs=[pl.BlockSpec(memory_space=pl.ANY)], out_specs=pl.BlockSpec(memory_space=pl.ANY),
        scratch_shapes=list(_EXCHANGE_SEMS),
    )(x)


def _all_gather(x, name):
    return _exchange("gather", x, name)


def _call(body, name, grid, in_specs, out_specs, out_shape, args, scratch=(), side=None):
    single = not isinstance(out_shape, (tuple, list))
    if single:
        out_shape, out_specs = (out_shape,), (out_specs,)
    if side is None:
        res = pl.pallas_call(body, name=name, grid=grid, in_specs=list(in_specs), out_specs=tuple(out_specs),
                             out_shape=tuple(out_shape), scratch_shapes=list(scratch), compiler_params=_cparams())(*args)
        return res[0] if single else tuple(res)
    kind, x = side
    n_in, n_out, n_scr = len(in_specs), len(out_shape), len(scratch)

    def wrapped(*refs):
        ins, x_ref = refs[:n_in], refs[n_in]
        outs, o_ref = refs[n_in + 1:n_in + 1 + n_out], refs[n_in + 1 + n_out]
        scr = refs[n_in + 2 + n_out:n_in + 2 + n_out + n_scr]
        sems = refs[n_in + 2 + n_out + n_scr:]
        ids = [pl.program_id(a) for a in range(len(grid))]
        first = functools.reduce(jnp.logical_and, [i == 0 for i in ids])
        last = functools.reduce(jnp.logical_and, [i == g - 1 for i, g in zip(ids, grid)])

        @pl.when(first)
        def _():
            _exchange_start(kind, x_ref, o_ref, *sems)

        body(*ins, *outs, *scr)

        @pl.when(last)
        def _():
            _exchange_wait(kind, x_ref, o_ref, *sems)

    any_spec = pl.BlockSpec(memory_space=pl.ANY)
    res = pl.pallas_call(
        wrapped, name=name, grid=grid, in_specs=list(in_specs) + [any_spec], out_specs=tuple(out_specs) + (any_spec,),
        out_shape=tuple(out_shape) + (_exchange_shape(kind, x),), scratch_shapes=list(scratch) + list(_EXCHANGE_SEMS),
        compiler_params=_cparams())(*args, x)
    return tuple(res)


def _mm(a, b, mode, name, out_dtype=F32, bias=None, side=None, tm_cap=1024, tn_cap=1408, tk_cap=1408):
    if mode == "nn":
        (M, K), (K2, N) = a.shape, b.shape
    elif mode == "nt":
        (M, K), (N, K2) = a.shape, b.shape
    else:
        (K, M), (K2, N) = a.shape, b.shape
    assert K == K2, (a.shape, b.shape, mode)
    tm = _tile(M, tm_cap, 128 if mode == "tn" else 16)
    tn = _tile(N, tn_cap, 128)
    tk = _tile(K, tk_cap, 128)
    nk = K // tk
    ca, cb = {"nn": (1, 0), "nt": (1, 1), "tn": (0, 0)}[mode]
    has_bias = bias is not None

    def body(*refs):
        if has_bias:
            a_ref, b_ref, bias_ref, o_ref, acc_ref = refs
        else:
            a_ref, b_ref, o_ref, acc_ref = refs
        k = pl.program_id(2)
        part = _dot(a_ref[...].astype(BF16), b_ref[...].astype(BF16), ca, cb)

        @pl.when(k == 0)
        def _():
            acc_ref[...] = part

        @pl.when(k > 0)
        def _():
            acc_ref[...] += part

        @pl.when(k == nk - 1)
        def _():
            r = acc_ref[...]
            if has_bias:
                r = r + bias_ref[...]
            o_ref[...] = r.astype(out_dtype)

    if mode == "tn":
        a_spec = pl.BlockSpec((tk, tm), lambda i, j, k: (k, i))
    else:
        a_spec = pl.BlockSpec((tm, tk), lambda i, j, k: (i, k))
    if mode == "nt":
        b_spec = pl.BlockSpec((tn, tk), lambda i, j, k: (j, k))
    else:
        b_spec = pl.BlockSpec((tk, tn), lambda i, j, k: (k, j))
    in_specs = [a_spec, b_spec]
    args = [a, b]
    if has_bias:
        in_specs.append(pl.BlockSpec((1, tn), lambda i, j, k: (0, j)))
        args.append(bias)
    return _call(body, name, (M // tm, N // tn, nk), in_specs, pl.BlockSpec((tm, tn), lambda i, j, k: (i, j)),
                 jax.ShapeDtypeStruct((M, N), out_dtype), args, scratch=[pltpu.VMEM((tm, tn), F32)], side=side)


def _vec_spec(n):
    return pl.BlockSpec((1, n), lambda t: (0, 0))


def _norm_mod_fwd(x, g, sc, sh, name, y=None, gate=None):
    T, D = x.shape
    tr = _tile(T, 256, 16)
    has_y = y is not None

    def body(*refs):
        if has_y:
            x_ref, y_ref, gate_ref, g_ref, sc_ref, sh_ref, xn_ref, h_ref = refs
            xv = x_ref[...] + gate_ref[...] * y_ref[...]
            xn_ref[...] = xv
        else:
            x_ref, g_ref, sc_ref, sh_ref, h_ref = refs
            xv = x_ref[...]
        r = lax.rsqrt(jnp.mean(xv * xv, axis=-1, keepdims=True) + NORM_EPS)
        h = (xv * r * g_ref[...]) * (1.0 + sc_ref[...]) + sh_ref[...]
        h_ref[...] = h.astype(BF16)

    row = pl.BlockSpec((tr, D), lambda t: (t, 0))
    vec = _vec_spec(D)
    if has_y:
        in_specs, args = [row, row, vec, vec, vec, vec], [x, y, gate, g, sc, sh]
        out_shape = (jax.ShapeDtypeStruct((T, D), F32), jax.ShapeDtypeStruct((T, D), BF16))
        out_specs = (row, row)
    else:
        in_specs, args = [row, vec, vec, vec], [x, g, sc, sh]
        out_shape = jax.ShapeDtypeStruct((T, D), BF16)
        out_specs = row
    return _call(body, name, (T // tr,), in_specs, out_specs, out_shape, args)


def _gate_bwd(dxn, y, gate, name):
    T, D = dxn.shape
    tr = _tile(T, 256, 16)

    def body(dxn_ref, y_ref, gate_ref, dy_ref, dgate_ref, dbias_ref):
        @pl.when(pl.program_id(0) == 0)
        def _():
            dgate_ref[...] = jnp.zeros_like(dgate_ref)
            dbias_ref[...] = jnp.zeros_like(dbias_ref)
        d = dxn_ref[...]
        dy = d * gate_ref[...]
        dy_ref[...] = dy.astype(BF16)
        dgate_ref[...] += jnp.sum(d * y_ref[...], axis=0, keepdims=True)
        dbias_ref[...] += jnp.sum(dy, axis=0, keepdims=True)

    row = pl.BlockSpec((tr, D), lambda t: (t, 0))
    vec = _vec_spec(D)
    return _call(body, name, (T // tr,), [row, row, vec], (row, vec, vec),
                 (jax.ShapeDtypeStruct((T, D), BF16), jax.ShapeDtypeStruct((1, D), F32),
                  jax.ShapeDtypeStruct((1, D), F32)), (dxn, y, gate))


def _norm_mod_bwd(dh, x, g, sc, sh, dxn, name):
    T, D = x.shape
    tr = _tile(T, 256, 16)

    def body(dh_ref, x_ref, g_ref, sc_ref, sh_ref, dxn_ref, dx_ref, dg_ref, dsc_ref, dsh_ref):
        @pl.when(pl.program_id(0) == 0)
        def _():
            dg_ref[...] = jnp.zeros_like(dg_ref)
            dsc_ref[...] = jnp.zeros_like(dsc_ref)
            dsh_ref[...] = jnp.zeros_like(dsh_ref)
        dh = dh_ref[...].astype(F32)
        xv = x_ref[...]
        gv = g_ref[...]
        r = lax.rsqrt(jnp.mean(xv * xv, axis=-1, keepdims=True) + NORM_EPS)
        yv = xv * r
        dsh_ref[...] += jnp.sum(dh, axis=0, keepdims=True)
        dsc_ref[...] += jnp.sum(dh * (yv * gv), axis=0, keepdims=True)
        dn = dh * (1.0 + sc_ref[...])
        dg_ref[...] += jnp.sum(dn * yv, axis=0, keepdims=True)
        dy = dn * gv
        dx = r * (dy - yv * jnp.mean(dy * yv, axis=-1, keepdims=True))
        dx_ref[...] = dxn_ref[...] + dx

    row = pl.BlockSpec((tr, D), lambda t: (t, 0))
    vec = _vec_spec(D)
    vshape = jax.ShapeDtypeStruct((1, D), F32)
    return _call(body, name, (T // tr,), [row, row, vec, vec, vec, row], (row, vec, vec, vec),
                 (jax.ShapeDtypeStruct((T, D), F32), vshape, vshape, vshape), (dh, x, g, sc, sh, dxn))


def _final_loss(x, y, gate, tgt, g, name):
    T, D = x.shape
    tr = _tile(T, 256, 16)

    def body(x_ref, y_ref, gate_ref, tgt_ref, g_ref, dx_ref, dg_ref, loss_ref):
        @pl.when(pl.program_id(0) == 0)
        def _():
            dg_ref[...] = jnp.zeros_like(dg_ref)
            loss_ref[...] = jnp.zeros_like(loss_ref)
        xv = x_ref[...] + gate_ref[...] * y_ref[...]
        gv = g_ref[...]
        r = lax.rsqrt(jnp.mean(xv * xv, axis=-1, keepdims=True) + NORM_EPS)
        yv = xv * r
        e = yv * gv - tgt_ref[...]
        per_row = jnp.mean(e * e, axis=-1, keepdims=True)
        loss_ref[...] += 0.5 * jnp.sum(per_row, axis=0, keepdims=True)
        dout = e * (1.0 / D)
        dg_ref[...] += jnp.sum(dout * yv, axis=0, keepdims=True)
        dy = dout * gv
        dx_ref[...] = r * (dy - yv * jnp.mean(dy * yv, axis=-1, keepdims=True))

    row = pl.BlockSpec((tr, D), lambda t: (t, 0))
    vec = _vec_spec(D)
    return _call(body, name, (T // tr,), [row, row, vec, row, vec], (row, vec, _vec_spec(128)),
                 (jax.ShapeDtypeStruct((T, D), F32), jax.ShapeDtypeStruct((1, D), F32),
                  jax.ShapeDtypeStruct((1, 128), F32)), (x, y, gate, tgt, g))


FFN_HALO = 16


def _ffn_mid_fwd(z, dw, b, name, side=None):
    T, F2 = z.shape
    F = F2 // 2
    K = dw.shape[0]
    H = FFN_HALO
    tr = _tile(T, 512, H)
    cw = _tile(F, 512, 128)
    nc = F // cw

    def body(gc_ref, gh_ref, uc_ref, uh_ref, wg_ref, wu_ref, bg_ref, bu_ref, a_ref):
        first = pl.program_id(1) == 0

        def conv(cur_ref, halo_ref, w_ref, b_ref):
            halo = jnp.where(first, 0.0, halo_ref[...].astype(F32))
            xx = jnp.concatenate([halo, cur_ref[...].astype(F32)], axis=0)
            w = w_ref[...]
            acc = b_ref[...] + w[K - 1:K, :] * _rows(xx, H, tr)
            for k in range(K - 1):
                acc = acc + w[k:k + 1, :] * _rows(xx, H - (K - 1 - k), tr)
            return acc

        gv = conv(gc_ref, gh_ref, wg_ref, bg_ref)
        uv = conv(uc_ref, uh_ref, wu_ref, bu_ref)
        a_ref[...] = (gv * _sigmoid(gv) * uv).astype(BF16)

    rb = tr // H
    in_specs = [
        pl.BlockSpec((tr, cw), lambda j, t: (t, j)),
        pl.BlockSpec((H, cw), lambda j, t: (jnp.maximum(t * rb - 1, 0), j)),
        pl.BlockSpec((tr, cw), lambda j, t: (t, j + nc)),
        pl.BlockSpec((H, cw), lambda j, t: (jnp.maximum(t * rb - 1, 0), j + nc)),
        pl.BlockSpec((K, cw), lambda j, t: (0, j)),
        pl.BlockSpec((K, cw), lambda j, t: (0, j + nc)),
        pl.BlockSpec((1, cw), lambda j, t: (0, j)),
        pl.BlockSpec((1, cw), lambda j, t: (0, j + nc)),
    ]
    return _call(body, name, (nc, T // tr), in_specs, pl.BlockSpec((tr, cw), lambda j, t: (t, j)),
                 jax.ShapeDtypeStruct((T, F), BF16), (z, z, z, z, dw, dw, b, b), side=side)


def _ffn_mid_bwd(z, da, dw, b, name, side=None):
    T, F2 = z.shape
    F = F2 // 2
    K = dw.shape[0]
    H = FFN_HALO
    tr = _tile(T, 512, H)
    cw = _tile(F, 512, 128)
    nc = F // cw
    nt = T // tr
    rb = tr // H
    ext = tr + H

    def body(gp_ref, gc_ref, gn_ref, up_ref, uc_ref, un_ref, dac_ref, dan_ref, wg_ref, wu_ref, bg_ref, bu_ref,
             dzg_ref, dzu_ref, dwg_ref, dwu_ref, dbg_ref, dbu_ref):
        t = pl.program_id(1)
        first = t == 0
        last = t == nt - 1

        @pl.when(first)
        def _():
            for r in (dwg_ref, dwu_ref, dbg_ref, dbu_ref):
                r[...] = jnp.zeros_like(r)

        def window(p_ref, c_ref, n_ref):
            p = jnp.where(first, 0.0, p_ref[...].astype(F32))
            n = jnp.where(last, 0.0, n_ref[...].astype(F32))
            return jnp.concatenate([p, c_ref[...].astype(F32), n], axis=0)

        def conv_ext(xx, w, bv):
            acc = bv + w[K - 1:K, :] * _rows(xx, H, ext)
            for k in range(K - 1):
                acc = acc + w[k:k + 1, :] * _rows(xx, H - (K - 1 - k), ext)
            return acc

        xg = window(gp_ref, gc_ref, gn_ref)
        xu = window(up_ref, uc_ref, un_ref)
        wg, wu = wg_ref[...], wu_ref[...]
        gv = conv_ext(xg, wg, bg_ref[...])
        uv = conv_ext(xu, wu, bu_ref[...])
        da = jnp.concatenate([dac_ref[...].astype(F32), jnp.where(last, 0.0, dan_ref[...].astype(F32))], axis=0)
        s = _sigmoid(gv)
        dg = da * uv * (s * (1.0 + gv * (1.0 - s)))
        du = da * (gv * s)

        def finish(dzc, xx, w, dz_ref, dw_ref, db_ref):
            cur = _rows(dzc, 0, tr)
            db_ref[...] += jnp.sum(cur, axis=0, keepdims=True)
            parts = [jnp.sum(cur * _rows(xx, H - (K - 1 - k), tr), axis=0, keepdims=True) for k in range(K)]
            dw_ref[...] += jnp.concatenate(parts, axis=0)
            dz = w[K - 1:K, :] * cur
            for k in range(K - 1):
                dz = dz + w[k:k + 1, :] * _rows(dzc, K - 1 - k, tr)
            dz_ref[...] = dz.astype(BF16)

        finish(dg, xg, wg, dzg_ref, dwg_ref, dbg_ref)
        finish(du, xu, wu, dzu_ref, dwu_ref, dbu_ref)

    def prev(t):
        return jnp.maximum(t * rb - 1, 0)

    def nxt(t):
        return jnp.minimum((t + 1) * rb, T // H - 1)

    def zspecs(off):
        return [pl.BlockSpec((H, cw), lambda j, t: (prev(t), j + off)),
                pl.BlockSpec((tr, cw), lambda j, t: (t, j + off)),
                pl.BlockSpec((H, cw), lambda j, t: (nxt(t), j + off))]

    in_specs = zspecs(0) + zspecs(nc) + [
        pl.BlockSpec((tr, cw), lambda j, t: (t, j)),
        pl.BlockSpec((H, cw), lambda j, t: (nxt(t), j)),
        pl.BlockSpec((K, cw), lambda j, t: (0, j)),
        pl.BlockSpec((K, cw), lambda j, t: (0, j + nc)),
        pl.BlockSpec((1, cw), lambda j, t: (0, j)),
        pl.BlockSpec((1, cw), lambda j, t: (0, j + nc)),
    ]
    out_shape = (jax.ShapeDtypeStruct((T, F), BF16), jax.ShapeDtypeStruct((T, F), BF16),
                 jax.ShapeDtypeStruct((K, F), F32), jax.ShapeDtypeStruct((K, F), F32),
                 jax.ShapeDtypeStruct((1, F), F32), jax.ShapeDtypeStruct((1, F), F32))
    out_specs = (pl.BlockSpec((tr, cw), lambda j, t: (t, j)), pl.BlockSpec((tr, cw), lambda j, t: (t, j)),
                 pl.BlockSpec((K, cw), lambda j, t: (0, j)), pl.BlockSpec((K, cw), lambda j, t: (0, j)),
                 pl.BlockSpec((1, cw), lambda j, t: (0, j)), pl.BlockSpec((1, cw), lambda j, t: (0, j)))
    res = _call(body, name, (nc, nt), in_specs, out_specs, out_shape, (z, z, z, z, z, z, da, da, dw, dw, b, b), side=side)
    dzg, dzu, dwg, dwu, dbg, dbu = res[:6]
    return (jnp.concatenate([dzg, dzu], axis=1), jnp.concatenate([dwg, dwu], axis=1),
            jnp.concatenate([dbg, dbu], axis=1)) + tuple(res[6:])


CONV_HALO = 32


def _convmix_fwd(p, dw, dwb, lng, lnb, name, side=None):
    T, D2 = p.shape
    D = D2 // 2
    K = dw.shape[0]
    H = CONV_HALO
    tr = _tile(T, 256, H)
    rb = tr // H

    def body(pc_ref, ph_ref, w_ref, wb_ref, lng_ref, lnb_ref, zc_ref, s_ref):
        first = pl.program_id(0) == 0
        pp = jnp.concatenate([jnp.where(first, 0.0, ph_ref[...].astype(F32)), pc_ref[...].astype(F32)], axis=0)
        zg = pp[:, :D] * _sigmoid(pp[:, D:])
        w = w_ref[...]
        acc = wb_ref[...] + w[K - 1:K, :] * _rows(zg, H, tr)
        for k in range(K - 1):
            acc = acc + w[k:k + 1, :] * _rows(zg, H - (K - 1 - k), tr)
        zc_ref[...] = acc
        mu = jnp.mean(acc, axis=-1, keepdims=True)
        xc = acc - mu
        rstd = lax.rsqrt(jnp.mean(xc * xc, axis=-1, keepdims=True) + NORM_EPS)
        ln = xc * rstd * lng_ref[...] + lnb_ref[...]
        s_ref[...] = (ln * _sigmoid(ln)).astype(BF16)

    in_specs = [pl.BlockSpec((tr, D2), lambda t: (t, 0)),
                pl.BlockSpec((H, D2), lambda t: (jnp.maximum(t * rb - 1, 0), 0)),
                pl.BlockSpec((K, D), lambda t: (0, 0)), _vec_spec(D), _vec_spec(D), _vec_spec(D)]
    row = pl.BlockSpec((tr, D), lambda t: (t, 0))
    return _call(body, name, (T // tr,), in_specs, (row, row),
                 (jax.ShapeDtypeStruct((T, D), F32), jax.ShapeDtypeStruct((T, D), BF16)), (p, p, dw, dwb, lng, lnb), side=side)


def _convmix_bwd(ds, zc, p, dw, lng, lnb, name, side=None):
    T, D = zc.shape
    D2 = 2 * D
    K = dw.shape[0]
    H = CONV_HALO
    tr = _tile(T, 256, H)
    rb = tr // H
    nt = T // tr
    ext = tr + H

    def body(dsc_ref, dsn_ref, zcc_ref, zcn_ref, pp_ref, pc_ref, w_ref, lng_ref, lnb_ref,
             dp_ref, dbin_ref, ddw_ref, ddwb_ref, dlng_ref, dlnb_ref):
        t = pl.program_id(0)
        first = t == 0
        last = t == nt - 1

        @pl.when(first)
        def _():
            for r in (dbin_ref, ddw_ref, ddwb_ref, dlng_ref, dlnb_ref):
                r[...] = jnp.zeros_like(r)

        dsv = jnp.concatenate([dsc_ref[...].astype(F32), jnp.where(last, 0.0, dsn_ref[...].astype(F32))], axis=0)
        zcv = jnp.concatenate([zcc_ref[...], zcn_ref[...]], axis=0)
        lg = lng_ref[...]
        mu = jnp.mean(zcv, axis=-1, keepdims=True)
        xc = zcv - mu
        rstd = lax.rsqrt(jnp.mean(xc * xc, axis=-1, keepdims=True) + NORM_EPS)
        xh = xc * rstd
        ln = xh * lg + lnb_ref[...]
        sg = _sigmoid(ln)
        dln = dsv * (sg * (1.0 + ln * (1.0 - sg)))
        dlnc = _rows(dln, 0, tr)
        dlnb_ref[...] += jnp.sum(dlnc, axis=0, keepdims=True)
        dlng_ref[...] += jnp.sum(dlnc * _rows(xh, 0, tr), axis=0, keepdims=True)
        dxh = dln * lg
        dzc = rstd * (dxh - jnp.mean(dxh, axis=-1, keepdims=True) - xh * jnp.mean(dxh * xh, axis=-1, keepdims=True))
        dzcc = _rows(dzc, 0, tr)
        ddwb_ref[...] += jnp.sum(dzcc, axis=0, keepdims=True)

        pv = jnp.concatenate([jnp.where(first, 0.0, pp_ref[...].astype(F32)), pc_ref[...].astype(F32)], axis=0)
        av, gv = pv[:, :D], pv[:, D:]
        sgg = _sigmoid(gv)
        zg = av * sgg
        w = w_ref[...]
        parts = []
        dzg = w[K - 1:K, :] * dzcc
        for k in range(K):
            parts.append(jnp.sum(dzcc * _rows(zg, H - (K - 1 - k), tr), axis=0, keepdims=True))
            if k < K - 1:
                dzg = dzg + w[k:k + 1, :] * _rows(dzc, K - 1 - k, tr)
        ddw_ref[...] += jnp.concatenate(parts, axis=0)
        ac, sc_ = _rows(av, H, tr), _rows(sgg, H, tr)
        dpa = dzg * sc_
        dpg = dzg * ac * sc_ * (1.0 - sc_)
        dpv = jnp.concatenate([dpa, dpg], axis=1)
        dp_ref[...] = dpv.astype(BF16)
        dbin_ref[...] += jnp.sum(dpv, axis=0, keepdims=True)

    def prev(t):
        return jnp.maximum(t * rb - 1, 0)

    def nxt(t):
        return jnp.minimum((t + 1) * rb, T // H - 1)

    in_specs = [pl.BlockSpec((tr, D), lambda t: (t, 0)), pl.BlockSpec((H, D), lambda t: (nxt(t), 0)),
                pl.BlockSpec((tr, D), lambda t: (t, 0)), pl.BlockSpec((H, D), lambda t: (nxt(t), 0)),
                pl.BlockSpec((H, D2), lambda t: (prev(t), 0)), pl.BlockSpec((tr, D2), lambda t: (t, 0)),
                pl.BlockSpec((K, D), lambda t: (0, 0)), _vec_spec(D), _vec_spec(D)]
    out_shape = (jax.ShapeDtypeStruct((T, D2), BF16), jax.ShapeDtypeStruct((1, D2), F32),
                 jax.ShapeDtypeStruct((K, D), F32), jax.ShapeDtypeStruct((1, D), F32),
                 jax.ShapeDtypeStruct((1, D), F32), jax.ShapeDtypeStruct((1, D), F32))
    out_specs = (pl.BlockSpec((tr, D2), lambda t: (t, 0)), _vec_spec(D2), pl.BlockSpec((K, D), lambda t: (0, 0)),
                 _vec_spec(D), _vec_spec(D), _vec_spec(D))
    return _call(body, name, (nt,), in_specs, out_specs, out_shape, (ds, ds, zc, zc, p, p, dw, lng, lnb), side=side)


_INV_SQRT2 = 1.0 / math.sqrt(2.0)
_INV_SQRT2PI = 1.0 / math.sqrt(2.0 * math.pi)


def _gelu(x):
    return 0.5 * x * (1.0 + lax.erf(x * _INV_SQRT2))


def _gelu_grad(x):
    return 0.5 * (1.0 + lax.erf(x * _INV_SQRT2)) + x * jnp.exp(-0.5 * x * x) * _INV_SQRT2PI


def _tril_mask(n, transposed=False):
    r = lax.broadcasted_iota(jnp.int32, (n, n), 0)
    c = lax.broadcasted_iota(jnp.int32, (n, n), 1)
    return (r <= c) if transposed else (r >= c)


def _sgu_fwd(p, lng, lnb, ws, bs_t, name):
    T, S2 = p.shape
    S = S2 // 2
    G, C, _ = ws.shape
    gd = S // G
    cpt = 2 if (T // C) % 2 == 0 else 1
    tr = C * cpt

    def body(p_ref, lng_ref, lnb_ref, ws_ref, bst_ref, m_ref):
        mask = _tril_mask(C)
        lg, lb = lng_ref[...], lnb_ref[...]
        bst = bst_ref[...]
        for ci in range(cpt):
            pc = p_ref[ci * C:(ci + 1) * C, :].astype(F32)
            z = _gelu(pc)
            u, v = z[:, :S], z[:, S:]
            mu = jnp.mean(v, axis=-1, keepdims=True)
            xc = v - mu
            rstd = lax.rsqrt(jnp.mean(xc * xc, axis=-1, keepdims=True) + NORM_EPS)
            vn = (xc * rstd * lg + lb).astype(BF16)
            outs = []
            for g in range(G):
                wm = jnp.where(mask, ws_ref[g], 0.0).astype(BF16)
                vs = _dot(wm, vn[:, g * gd:(g + 1) * gd], 1, 0) + bst[:, g:g + 1]
                outs.append(u[:, g * gd:(g + 1) * gd] * vs)
            m_ref[ci * C:(ci + 1) * C, :] = jnp.concatenate(outs, axis=1).astype(BF16)

    in_specs = [pl.BlockSpec((tr, S2), lambda t: (t, 0)), _vec_spec(S), _vec_spec(S),
                pl.BlockSpec((G, C, C), lambda t: (0, 0, 0)), pl.BlockSpec((C, G), lambda t: (0, 0))]
    return _call(body, name, (T // tr,), in_specs, pl.BlockSpec((tr, S), lambda t: (t, 0)),
                 jax.ShapeDtypeStruct((T, S), BF16), (p, lng, lnb, ws, bs_t))


def _sgu_bwd(p, dm, lng, lnb, ws, bs_t, name, side=None):
    T, S2 = p.shape
    S = S2 // 2
    G, C, _ = ws.shape
    gd = S // G
    nt = T // C

    def body(p_ref, dm_ref, lng_ref, lnb_ref, ws_ref, bst_ref, dp_ref, dbin_ref, dlng_ref, dlnb_ref, dws_ref, dbst_ref):
        @pl.when(pl.program_id(0) == 0)
        def _():
            for r in (dbin_ref, dlng_ref, dlnb_ref, dws_ref, dbst_ref):
                r[...] = jnp.zeros_like(r)

        mask = _tril_mask(C)
        lg, lb = lng_ref[...], lnb_ref[...]
        bst = bst_ref[...]
        pc = p_ref[...].astype(F32)
        dmv = dm_ref[...].astype(F32)
        z = _gelu(pc)
        u, v = z[:, :S], z[:, S:]
        mu = jnp.mean(v, axis=-1, keepdims=True)
        xc = v - mu
        rstd = lax.rsqrt(jnp.mean(xc * xc, axis=-1, keepdims=True) + NORM_EPS)
        vh = xc * rstd
        vn = (vh * lg + lb).astype(BF16)
        dus, dvns, dbcols = [], [], []
        for g in range(G):
            sl = slice(g * gd, (g + 1) * gd)
            wm = jnp.where(mask, ws_ref[g], 0.0).astype(BF16)
            vs = _dot(wm, vn[:, sl], 1, 0) + bst[:, g:g + 1]
            dmg = dmv[:, sl]
            dus.append(dmg * vs)
            dvs = dmg * u[:, sl]
            dbcols.append(jnp.sum(dvs, axis=-1, keepdims=True))
            dvsb = dvs.astype(BF16)
            dws_ref[g] += jnp.where(mask, _dot(dvsb, vn[:, sl], 1, 1), 0.0)
            dvns.append(_dot(wm, dvsb, 0, 0))
        dbst_ref[...] += jnp.concatenate(dbcols, axis=1)
        dvn = jnp.concatenate(dvns, axis=1)
        dlnb_ref[...] += jnp.sum(dvn, axis=0, keepdims=True)
        dlng_ref[...] += jnp.sum(dvn * vh, axis=0, keepdims=True)
        dvh = dvn * lg
        dv = rstd * (dvh - jnp.mean(dvh, axis=-1, keepdims=True) - vh * jnp.mean(dvh * vh, axis=-1, keepdims=True))
        dz = jnp.concatenate([jnp.concatenate(dus, axis=1), dv], axis=1)
        dpv = dz * _gelu_grad(pc)
        dp_ref[...] = dpv.astype(BF16)
        dbin_ref[...] += jnp.sum(dpv, axis=0, keepdims=True)

    in_specs = [pl.BlockSpec((C, S2), lambda t: (t, 0)), pl.BlockSpec((C, S), lambda t: (t, 0)),
                _vec_spec(S), _vec_spec(S), pl.BlockSpec((G, C, C), lambda t: (0, 0, 0)),
                pl.BlockSpec((C, G), lambda t: (0, 0))]
    out_shape = (jax.ShapeDtypeStruct((T, S2), BF16), jax.ShapeDtypeStruct((1, S2), F32),
                 jax.ShapeDtypeStruct((1, S), F32), jax.ShapeDtypeStruct((1, S), F32),
                 jax.ShapeDtypeStruct((G, C, C), F32), jax.ShapeDtypeStruct((C, G), F32))
    out_specs = (pl.BlockSpec((C, S2), lambda t: (t, 0)), _vec_spec(S2), _vec_spec(S), _vec_spec(S),
                 pl.BlockSpec((G, C, C), lambda t: (0, 0, 0)), pl.BlockSpec((C, G), lambda t: (0, 0)))
    return _call(body, name, (nt,), in_specs, out_specs, out_shape, (p, dm, lng, lnb, ws, bs_t), side=side)


def _alibi_slope(h, n_heads):
    return 2.0 ** (-8.0 * (h + 1) / n_heads)


def _attn_scores(q_h, k_h, slope, sink, first_block):
    B = BLOCK
    qi = lax.broadcasted_iota(jnp.int32, (B, 2 * B), 0)
    kj = lax.broadcasted_iota(jnp.int32, (B, 2 * B), 1)
    dist = qi + B - kj
    valid = (dist >= 0) & (dist < B) & jnp.logical_not(first_block & (kj < B))
    s = _dot(q_h, k_h, 1, 1) * (HEAD_DIM ** -0.5) - slope * dist.astype(F32)
    s = jnp.where(valid, s, NEG_INF)
    mx = jnp.maximum(jnp.max(s, axis=-1, keepdims=True), sink)
    e = jnp.exp(s - mx)
    es = jnp.exp(sink - mx)
    denom = jnp.sum(e, axis=-1, keepdims=True) + es
    return e, es, denom


def _attn_fwd(qkv, sinks, n_heads, name, side=None):
    T = qkv.shape[0]
    B = BLOCK
    HQ = n_heads * HEAD_DIM
    KVW = N_KV * HEAD_DIM
    group = n_heads // N_KV
    nb = T // B
    kcol = HQ // KVW

    def body(q_ref, kp_ref, kc_ref, vp_ref, vc_ref, sink_ref, o_ref):
        first = pl.program_id(0) == 0
        q = q_ref[...]
        k2 = jnp.concatenate([kp_ref[...], kc_ref[...]], axis=0)
        v2 = jnp.concatenate([vp_ref[...], vc_ref[...]], axis=0)
        sk = sink_ref[...]
        outs = []
        for h in range(n_heads):
            kv = h // group
            e, es, denom = _attn_scores(q[:, h * HEAD_DIM:(h + 1) * HEAD_DIM], k2[:, kv * HEAD_DIM:(kv + 1) * HEAD_DIM],
                                        _alibi_slope(h, n_heads), sk[:, h:h + 1], first)
            pr = (e * (1.0 / denom)).astype(BF16)
            outs.append(_dot(pr, v2[:, kv * HEAD_DIM:(kv + 1) * HEAD_DIM], 1, 0))
        o_ref[...] = jnp.concatenate(outs, axis=1).astype(BF16)

    def prev(n):
        return jnp.maximum(n - 1, 0)

    in_specs = [pl.BlockSpec((B, HQ), lambda n: (n, 0)),
                pl.BlockSpec((B, KVW), lambda n: (prev(n), kcol)), pl.BlockSpec((B, KVW), lambda n: (n, kcol)),
                pl.BlockSpec((B, KVW), lambda n: (prev(n), kcol + 1)), pl.BlockSpec((B, KVW), lambda n: (n, kcol + 1)),
                _vec_spec(n_heads)]
    return _call(body, name, (nb,), in_specs, pl.BlockSpec((B, HQ), lambda n: (n, 0)),
                 jax.ShapeDtypeStruct((T, HQ), BF16), (qkv, qkv, qkv, qkv, qkv, sinks), side=side)


def _attn_bwd(qkv, do, sinks, n_heads, name, side=None):
    T = qkv.shape[0]
    B = BLOCK
    HQ = n_heads * HEAD_DIM
    KVW = N_KV * HEAD_DIM
    group = n_heads // N_KV
    nb = T // B
    kcol = HQ // KVW
    scale = HEAD_DIM ** -0.5

    def body(q_ref, kp_ref, kc_ref, vp_ref, vc_ref, do_ref, sink_ref,
             dq_ref, dk_ref, dv_ref, dbq_ref, dbk_ref, dbv_ref, dsink_ref, ck_ref, cv_ref):
        n = pl.program_id(0)
        first = n == 0

        @pl.when(first)
        def _():
            for r in (dbq_ref, dbk_ref, dbv_ref, dsink_ref, ck_ref, cv_ref):
                r[...] = jnp.zeros_like(r)

        @pl.when(n < nb)
        def _():
            q = q_ref[...]
            dov = do_ref[...]
            k2 = jnp.concatenate([kp_ref[...], kc_ref[...]], axis=0)
            v2 = jnp.concatenate([vp_ref[...], vc_ref[...]], axis=0)
            sk = sink_ref[...]
            dqs, dsinks = [], []
            dks = [None] * N_KV
            dvs = [None] * N_KV
            for h in range(n_heads):
                kv = h // group
                hs = slice(h * HEAD_DIM, (h + 1) * HEAD_DIM)
                ks = slice(kv * HEAD_DIM, (kv + 1) * HEAD_DIM)
                e, es, denom = _attn_scores(q[:, hs], k2[:, ks], _alibi_slope(h, n_heads), sk[:, h:h + 1], first)
                inv = 1.0 / denom
                pr = e * inv
                dp = _dot(dov[:, hs], v2[:, ks], 1, 1)
                dsum = jnp.sum(pr * dp, axis=-1, keepdims=True)
                ds = (pr * (dp - dsum) * scale).astype(BF16)
                dsinks.append(-jnp.sum(es * inv * dsum, axis=0, keepdims=True))
                dqs.append(_dot(ds, k2[:, ks], 1, 0))
                dk_h = _dot(ds, q[:, hs], 0, 0)
                dv_h = _dot(pr.astype(BF16), dov[:, hs], 0, 0)
                dks[kv] = dk_h if dks[kv] is None else dks[kv] + dk_h
                dvs[kv] = dv_h if dvs[kv] is None else dvs[kv] + dv_h
            dq = jnp.concatenate(dqs, axis=1)
            dq_ref[...] = dq.astype(BF16)
            dbq_ref[...] += jnp.sum(dq, axis=0, keepdims=True)
            dsink_ref[...] += jnp.concatenate(dsinks, axis=1)
            dk2 = jnp.concatenate(dks, axis=1)
            dv2 = jnp.concatenate(dvs, axis=1)
            dbk_ref[...] += jnp.sum(dk2, axis=0, keepdims=True)
            dbv_ref[...] += jnp.sum(dv2, axis=0, keepdims=True)
            dk_ref[...] = (ck_ref[...] + dk2[:B]).astype(BF16)
            dv_ref[...] = (cv_ref[...] + dv2[:B]).astype(BF16)
            ck_ref[...] = dk2[B:]
            cv_ref[...] = dv2[B:]

        @pl.when(n == nb)
        def _():
            dk_ref[...] = ck_ref[...].astype(BF16)
            dv_ref[...] = cv_ref[...].astype(BF16)

    def cur(n):
        return jnp.minimum(n, nb - 1)

    def prev(n):
        return jnp.maximum(cur(n) - 1, 0)

    def outp(n):
        return jnp.maximum(n - 1, 0)

    in_specs = [pl.BlockSpec((B, HQ), lambda n: (cur(n), 0)),
                pl.BlockSpec((B, KVW), lambda n: (prev(n), kcol)), pl.BlockSpec((B, KVW), lambda n: (cur(n), kcol)),
                pl.BlockSpec((B, KVW), lambda n: (prev(n), kcol + 1)), pl.BlockSpec((B, KVW), lambda n: (cur(n), kcol + 1)),
                pl.BlockSpec((B, HQ), lambda n: (cur(n), 0)), _vec_spec(n_heads)]
    out_shape = (jax.ShapeDtypeStruct((T, HQ), BF16), jax.ShapeDtypeStruct((T, KVW), BF16),
                 jax.ShapeDtypeStruct((T, KVW), BF16), jax.ShapeDtypeStruct((1, HQ), F32),
                 jax.ShapeDtypeStruct((1, KVW), F32), jax.ShapeDtypeStruct((1, KVW), F32),
                 jax.ShapeDtypeStruct((1, n_heads), F32))
    out_specs = (pl.BlockSpec((B, HQ), lambda n: (cur(n), 0)), pl.BlockSpec((B, KVW), lambda n: (outp(n), 0)),
                 pl.BlockSpec((B, KVW), lambda n: (outp(n), 0)), _vec_spec(HQ), _vec_spec(KVW), _vec_spec(KVW),
                 _vec_spec(n_heads))
    return _call(body, name, (nb + 1,), in_specs, out_specs, out_shape, (qkv, qkv, qkv, qkv, qkv, do, sinks),
                 scratch=[pltpu.VMEM((B, KVW), F32), pltpu.VMEM((B, KVW), F32)], side=side)


def _sum8(r, name):
    _, R, C = r.shape
    tr = _tile(R, 512, 16)

    def body(r_ref, o_ref):
        acc = r_ref[0].astype(F32)
        for d in range(1, N_DEV):
            acc = acc + r_ref[d].astype(F32)
        o_ref[...] = acc

    return _call(body, name, (R // tr,), [pl.BlockSpec((N_DEV, tr, C), lambda i: (0, i, 0))],
                 pl.BlockSpec((tr, C), lambda i: (i, 0)), jax.ShapeDtypeStruct((R, C), F32), (r,))


def _adamw(w, g, m, v, name):
    R, C = w.shape
    tr = _tile(R, 512, 8)
    c1 = 1.0 - ADAM_B1 ** ADAM_STEP
    c2 = 1.0 - ADAM_B2 ** ADAM_STEP

    def body(w_ref, g_ref, m_ref, v_ref, d_ref, nm_ref, nv_ref):
        gv = g_ref[...]
        nm = ADAM_B1 * m_ref[...] + (1.0 - ADAM_B1) * gv
        nv = ADAM_B2 * v_ref[...] + (1.0 - ADAM_B2) * (gv * gv)
        nm_ref[...] = nm
        nv_ref[...] = nv
        d_ref[...] = -ADAM_LR * ((nm / c1) / (jnp.sqrt(nv / c2) + ADAM_EPS) + ADAM_WD * w_ref[...])

    spec = pl.BlockSpec((tr, C), lambda i: (i, 0))
    shp = jax.ShapeDtypeStruct((R, C), F32)
    return _call(body, name, (R // tr,), [spec] * 4, (spec,) * 3, (shp, shp, shp), (w, g, m, v))


def _adamw_nd(w, g, m, v, name):
    shape = w.shape
    c = shape[-1]
    f = lambda a: a.reshape(-1, c)
    d, nm, nv = _adamw(f(w), f(g), f(m), f(v), name)
    return d.reshape(shape), nm.reshape(shape), nv.reshape(shape)


def _pack(arrs, width):
    flat = jnp.concatenate([a.reshape(-1).astype(F32) for a in arrs])
    n = flat.shape[0]
    quantum = 8 * width
    total = -(-n // quantum) * quantum
    return jnp.pad(flat, (0, total - n)).reshape(-1, width)


def _unpack(flat, shapes):
    out, off = [], 0
    for s in shapes:
        n = math.prod(s)
        out.append(flat[off:off + n].reshape(s))
        off += n
    return out


def kernel(x, c, norm1_g, norm2_g, ada_w, ada_b, attn_wqkv, attn_bqkv, attn_sinks, attn_wo, attn_bo, conv_w_in, conv_b_in, conv_dw, conv_dw_b, conv_ln_g, conv_ln_b, conv_w_out, conv_b_out, sgu_w_in, sgu_b_in, sgu_ln_g, sgu_ln_b, sgu_ws, sgu_bs, sgu_w_out, sgu_b_out, ffn_w_in, ffn_dw, ffn_dw_b, ffn_w_out, final_g, loss_target, m_norm1_g, m_norm2_g, m_ada_w, m_ada_b, m_attn_wqkv, m_attn_bqkv, m_attn_sinks, m_attn_wo, m_attn_bo, m_conv_w_in, m_conv_b_in, m_conv_dw, m_conv_dw_b, m_conv_ln_g, m_conv_ln_b, m_conv_w_out, m_conv_b_out, m_sgu_w_in, m_sgu_b_in, m_sgu_ln_g, m_sgu_ln_b, m_sgu_ws, m_sgu_bs, m_sgu_w_out, m_sgu_b_out, m_ffn_w_in, m_ffn_dw, m_ffn_dw_b, m_ffn_w_out, m_final_g, v_norm1_g, v_norm2_g, v_ada_w, v_ada_b, v_attn_wqkv, v_attn_bqkv, v_attn_sinks, v_attn_wo, v_attn_bo, v_conv_w_in, v_conv_b_in, v_conv_dw, v_conv_dw_b, v_conv_ln_g, v_conv_ln_b, v_conv_w_out, v_conv_b_out, v_sgu_w_in, v_sgu_b_in, v_sgu_ln_g, v_sgu_ln_b, v_sgu_ws, v_sgu_bs, v_sgu_w_out, v_sgu_b_out, v_ffn_w_in, v_ffn_dw, v_ffn_dw_b, v_ffn_w_out, v_final_g):
    W = dict(norm1_g=norm1_g, norm2_g=norm2_g, ada_w=ada_w, ada_b=ada_b, attn_wqkv=attn_wqkv, attn_bqkv=attn_bqkv, attn_sinks=attn_sinks, attn_wo=attn_wo, attn_bo=attn_bo, conv_w_in=conv_w_in, conv_b_in=conv_b_in, conv_dw=conv_dw, conv_dw_b=conv_dw_b, conv_ln_g=conv_ln_g, conv_ln_b=conv_ln_b, conv_w_out=conv_w_out, conv_b_out=conv_b_out, sgu_w_in=sgu_w_in, sgu_b_in=sgu_b_in, sgu_ln_g=sgu_ln_g, sgu_ln_b=sgu_ln_b, sgu_ws=sgu_ws, sgu_bs=sgu_bs, sgu_w_out=sgu_w_out, sgu_b_out=sgu_b_out, ffn_w_in=ffn_w_in, ffn_dw=ffn_dw, ffn_dw_b=ffn_dw_b, ffn_w_out=ffn_w_out, final_g=final_g)
    MOM = dict(norm1_g=m_norm1_g, norm2_g=m_norm2_g, ada_w=m_ada_w, ada_b=m_ada_b, attn_wqkv=m_attn_wqkv, attn_bqkv=m_attn_bqkv, attn_sinks=m_attn_sinks, attn_wo=m_attn_wo, attn_bo=m_attn_bo, conv_w_in=m_conv_w_in, conv_b_in=m_conv_b_in, conv_dw=m_conv_dw, conv_dw_b=m_conv_dw_b, conv_ln_g=m_conv_ln_g, conv_ln_b=m_conv_ln_b, conv_w_out=m_conv_w_out, conv_b_out=m_conv_b_out, sgu_w_in=m_sgu_w_in, sgu_b_in=m_sgu_b_in, sgu_ln_g=m_sgu_ln_g, sgu_ln_b=m_sgu_ln_b, sgu_ws=m_sgu_ws, sgu_bs=m_sgu_bs, sgu_w_out=m_sgu_w_out, sgu_b_out=m_sgu_b_out, ffn_w_in=m_ffn_w_in, ffn_dw=m_ffn_dw, ffn_dw_b=m_ffn_dw_b, ffn_w_out=m_ffn_w_out, final_g=m_final_g)
    VAR = dict(norm1_g=v_norm1_g, norm2_g=v_norm2_g, ada_w=v_ada_w, ada_b=v_ada_b, attn_wqkv=v_attn_wqkv, attn_bqkv=v_attn_bqkv, attn_sinks=v_attn_sinks, attn_wo=v_attn_wo, attn_bo=v_attn_bo, conv_w_in=v_conv_w_in, conv_b_in=v_conv_b_in, conv_dw=v_conv_dw, conv_dw_b=v_conv_dw_b, conv_ln_g=v_conv_ln_g, conv_ln_b=v_conv_ln_b, conv_w_out=v_conv_w_out, conv_b_out=v_conv_b_out, sgu_w_in=v_sgu_w_in, sgu_b_in=v_sgu_b_in, sgu_ln_g=v_sgu_ln_g, sgu_ln_b=v_sgu_ln_b, sgu_ws=v_sgu_ws, sgu_bs=v_sgu_bs, sgu_w_out=v_sgu_w_out, sgu_b_out=v_sgu_b_out, ffn_w_in=v_ffn_w_in, ffn_dw=v_ffn_dw, ffn_dw_b=v_ffn_dw_b, ffn_w_out=v_ffn_w_out, final_g=v_final_g)
    ORDER = list(W)

    _, T, D = x.shape
    depth = norm1_g.shape[0]
    n_heads = D // HEAD_DIM
    me = 4 * lax.axis_index("x") + 2 * lax.axis_index("y") + lax.axis_index("c")
    x0 = x.reshape(T, D)
    tgt = loss_target.reshape(T, D)

    small_sharded = ["attn_bqkv", "attn_bo", "conv_dw", "sgu_b_in", "sgu_ln_g", "sgu_ln_b", "sgu_b_out", "ffn_dw"]
    s_in = [c] + [W[n] for n in small_sharded]
    s_shapes = [a.shape for a in s_in]
    gathered_small = _all_gather(_pack(s_in, 128), "gather_small").reshape(N_DEV, -1)
    per_dev = [_unpack(gathered_small[d], s_shapes) for d in range(N_DEV)]

    def full(idx):
        return jnp.concatenate([per_dev[d][idx] for d in range(N_DEV)], axis=-1)

    c_all = jnp.concatenate([per_dev[d][0] for d in range(N_DEV)], axis=0)
    F_small = {n: full(1 + i) for i, n in enumerate(small_sharded)}

    c_act = c_all * jax.nn.sigmoid(c_all)
    c_pad = jnp.pad(c_act, ((0, 8), (0, 0))).astype(BF16)
    n_ada = ada_w.shape[-1]
    ada_cols = lax.dynamic_slice_in_dim(ada_b, me * n_ada, n_ada, axis=1)
    mod_loc = jnp.stack([_mm(c_pad, ada_w[i].astype(BF16), "nn", "ada_mod", bias=ada_cols[i:i + 1])
                         for i in range(depth)])
    mod_all = _all_gather(mod_loc, "gather_mod")
    mod_mine = lax.dynamic_index_in_dim(mod_all, me, axis=2, keepdims=False)
    mod = jnp.transpose(mod_mine, (1, 0, 2)).reshape(depth, 6, 1, D)

    mixer_names = {0: ("attn_wqkv", "attn_wo"), 1: ("conv_w_in", "conv_w_out"), 2: ("sgu_w_in", "sgu_w_out")}

    def group_pieces(key):
        part, i = key
        w_in, w_out = ("ffn_w_in", "ffn_w_out") if part == "ffn" else mixer_names[i % 3]
        l = i if part == "ffn" else i // 3
        return [(w_in, l, W[w_in].shape[-1]), (w_out, l, W[w_out].shape[1])]

    def packed_weights(key):
        (w_in, l, _), (w_out, _, _) = group_pieces(key)
        return jnp.concatenate([W[w_in][l].T.astype(BF16), W[w_out][l].astype(BF16)], axis=0)

    WF = {}

    def land_weights(key, gathered):
        o = 0
        for n, l, r in group_pieces(key):
            WF[(n, l)] = gathered[:, o:o + r, :].reshape(N_DEV * r, D)
            o += r

    def gather_side(key):
        return ("gather", packed_weights(key))

    def vec(a):
        return a.reshape(1, -1)

    def kind_of(i):
        return i % 3

    land_weights(("mix", 0), _all_gather(packed_weights(("mix", 0)), "gather_first"))

    saved = []
    xs, y_prev, gate_prev = x0, None, None
    for i in range(depth):
        sh1, sc1, g1, sh2, sc2, g2 = [mod[i, k] for k in range(6)]
        kind, j = i % 3, i // 3
        st = dict(kind=kind, j=j)
        if y_prev is None:
            h1 = _norm_mod_fwd(xs, vec(norm1_g[i]), sc1, sh1, "norm1_fwd")
        else:
            xs, h1 = _norm_mod_fwd(xs, vec(norm1_g[i]), sc1, sh1, "norm1_fwd", y=y_prev, gate=gate_prev)
        st.update(x_in=xs, h1=h1)
        if kind == 0:
            qkv = _mm(h1, WF[("attn_wqkv", j)], "nt", "attn_qkv", out_dtype=BF16, bias=vec(F_small["attn_bqkv"][j]))
            o, landed = _attn_fwd(qkv, vec(attn_sinks[j]), n_heads, "attn_fwd", side=gather_side(("ffn", i)))
            land_weights(("ffn", i), landed)
            y1 = _mm(o, WF[("attn_wo", j)], "nn", "attn_out", bias=vec(F_small["attn_bo"][j]))
            st.update(qkv=qkv, o=o)
        elif kind == 1:
            p = _mm(h1, WF[("conv_w_in", j)], "nt", "conv_in", out_dtype=BF16, bias=vec(conv_b_in[j]))
            zc, s, landed = _convmix_fwd(p, F_small["conv_dw"][j], vec(conv_dw_b[j]), vec(conv_ln_g[j]),
                                         vec(conv_ln_b[j]), "convmix_fwd", side=gather_side(("ffn", i)))
            land_weights(("ffn", i), landed)
            y1 = _mm(s, WF[("conv_w_out", j)], "nn", "conv_out", bias=vec(conv_b_out[j]))
            st.update(p=p, zc=zc, s=s)
        else:
            p = _mm(h1, WF[("sgu_w_in", j)], "nt", "sgu_in", out_dtype=BF16, bias=vec(F_small["sgu_b_in"][j]))
            mm_ = _sgu_fwd(p, vec(F_small["sgu_ln_g"][j]), vec(F_small["sgu_ln_b"][j]), sgu_ws[j], sgu_bs[j].T, "sgu_fwd")
            y1 = _mm(mm_, WF[("sgu_w_out", j)], "nn", "sgu_out", bias=vec(F_small["sgu_b_out"][j]))
            st.update(p=p, m=mm_)
        xs, h2 = _norm_mod_fwd(xs, vec(norm2_g[i]), sc2, sh2, "norm2_fwd", y=y1, gate=g1)
        nxt = i + 1
        if nxt < depth:
            z, landed = _mm(h2, WF[("ffn_w_in", i)], "nt", "ffn_in", out_dtype=BF16, side=gather_side(("mix", nxt)))
            land_weights(("mix", nxt), landed)
        else:
            z = _mm(h2, WF[("ffn_w_in", i)], "nt", "ffn_in", out_dtype=BF16)
        if nxt < depth and kind_of(nxt) == 2:
            a, landed = _ffn_mid_fwd(z, F_small["ffn_dw"][i], vec(ffn_dw_b[i]), "ffn_mid_fwd", side=gather_side(("ffn", nxt)))
            land_weights(("ffn", nxt), landed)
        else:
            a = _ffn_mid_fwd(z, F_small["ffn_dw"][i], vec(ffn_dw_b[i]), "ffn_mid_fwd")
        y2 = _mm(a, WF[("ffn_w_out", i)], "nn", "ffn_out")
        st.update(y1=y1, x_mid=xs, h2=h2, z=z, a=a, y2=y2)
        saved.append(st)
        y_prev, gate_prev = y2, g2

    dx, d_final_g, loss_row = _final_loss(xs, y_prev, gate_prev, tgt, vec(final_g), "final_loss")
    loss = lax.psum(loss_row[0, 0], ("x", "y", "c"))

    G = {n: [None] * W[n].shape[0] for n in ORDER if n != "final_g"}
    GW = {}
    recv = {}
    dmod = [None] * depth

    def scatter_side(key):
        return ("scatter", jnp.concatenate([GW[(n, l)].reshape(N_DEV, r, D) for n, l, r in group_pieces(key)], axis=1))

    for i in reversed(range(depth)):
        st = saved[i]
        sh1, sc1, g1, sh2, sc2, g2 = [mod[i, k] for k in range(6)]
        kind, j = st["kind"], st["j"]
        dy2, dg2, _ = _gate_bwd(dx, st["y2"], g2, "gate_bwd")
        GW[("ffn_w_out", i)] = _mm(st["a"], dy2, "tn", "ffn_out_dw", out_dtype=BF16)
        da = _mm(dy2, WF[("ffn_w_out", i)], "nt", "ffn_out_dx", out_dtype=BF16)
        if i + 1 < depth:
            dz, ddw, ddwb, recv[("mix", i + 1)] = _ffn_mid_bwd(st["z"], da, F_small["ffn_dw"][i], vec(ffn_dw_b[i]),
                                                               "ffn_mid_bwd", side=scatter_side(("mix", i + 1)))
        else:
            dz, ddw, ddwb = _ffn_mid_bwd(st["z"], da, F_small["ffn_dw"][i], vec(ffn_dw_b[i]), "ffn_mid_bwd")
        G["ffn_dw"][i], G["ffn_dw_b"][i] = ddw, ddwb[0]
        GW[("ffn_w_in", i)] = _mm(dz, st["h2"], "tn", "ffn_in_dw", out_dtype=BF16)
        dh2 = _mm(dz, WF[("ffn_w_in", i)], "nn", "ffn_in_dx")
        dx, dn2, dsc2, dsh2 = _norm_mod_bwd(dh2, st["x_mid"], vec(norm2_g[i]), sc2, sh2, dx, "norm_bwd")
        G["norm2_g"][i] = dn2[0]
        dy1, dg1, dbo = _gate_bwd(dx, st["y1"], g1, "gate_bwd")
        ffn_side = scatter_side(("ffn", i))
        if kind == 0:
            G["attn_bo"][j] = dbo[0]
            GW[("attn_wo", j)] = _mm(st["o"], dy1, "tn", "attn_out_dw", out_dtype=BF16)
            do = _mm(dy1, WF[("attn_wo", j)], "nt", "attn_out_dx", out_dtype=BF16)
            dq, dk, dv, dbq, dbk, dbv, dsk, recv[("ffn", i)] = _attn_bwd(st["qkv"], do, vec(attn_sinks[j]), n_heads,
                                                                         "attn_bwd", side=ffn_side)
            dqkv = jnp.concatenate([dq, dk, dv], axis=1)
            G["attn_bqkv"][j] = jnp.concatenate([dbq, dbk, dbv], axis=1)[0]
            G["attn_sinks"][j] = dsk[0]
            GW[("attn_wqkv", j)] = _mm(dqkv, st["h1"], "tn", "attn_qkv_dw", out_dtype=BF16)
            dh1 = _mm(dqkv, WF[("attn_wqkv", j)], "nn", "attn_qkv_dx")
        elif kind == 1:
            G["conv_b_out"][j] = dbo[0]
            GW[("conv_w_out", j)] = _mm(st["s"], dy1, "tn", "conv_out_dw", out_dtype=BF16)
            ds = _mm(dy1, WF[("conv_w_out", j)], "nt", "conv_out_dx")
            dp, dbin, ddw, ddwb, dlng, dlnb, recv[("ffn", i)] = _convmix_bwd(
                ds, st["zc"], st["p"], F_small["conv_dw"][j], vec(conv_ln_g[j]), vec(conv_ln_b[j]), "convmix_bwd",
                side=ffn_side)
            G["conv_b_in"][j], G["conv_dw"][j], G["conv_dw_b"][j] = dbin[0], ddw, ddwb[0]
            G["conv_ln_g"][j], G["conv_ln_b"][j] = dlng[0], dlnb[0]
            GW[("conv_w_in", j)] = _mm(dp, st["h1"], "tn", "conv_in_dw", out_dtype=BF16)
            dh1 = _mm(dp, WF[("conv_w_in", j)], "nn", "conv_in_dx")
        else:
            G["sgu_b_out"][j] = dbo[0]
            GW[("sgu_w_out", j)] = _mm(st["m"], dy1, "tn", "sgu_out_dw", out_dtype=BF16)
            dm = _mm(dy1, WF[("sgu_w_out", j)], "nt", "sgu_out_dx")
            dp, dbin, dlng, dlnb, dws, dbst, recv[("ffn", i)] = _sgu_bwd(
                st["p"], dm, vec(F_small["sgu_ln_g"][j]), vec(F_small["sgu_ln_b"][j]), sgu_ws[j], sgu_bs[j].T, "sgu_bwd",
                side=ffn_side)
            G["sgu_b_in"][j], G["sgu_ln_g"][j], G["sgu_ln_b"][j] = dbin[0], dlng[0], dlnb[0]
            G["sgu_ws"][j], G["sgu_bs"][j] = dws, dbst.T
            GW[("sgu_w_in", j)] = _mm(dp, st["h1"], "tn", "sgu_in_dw", out_dtype=BF16)
            dh1 = _mm(dp, WF[("sgu_w_in", j)], "nn", "sgu_in_dx")
        dx, dn1, dsc1, dsh1 = _norm_mod_bwd(dh1, st["x_in"], vec(norm1_g[i]), sc1, sh1, dx, "norm_bwd")
        G["norm1_g"][i] = dn1[0]
        dmod[i] = jnp.concatenate([dsh1, dsc1, dg1, dsh2, dsc2, dg2], axis=1)[0]
    grad_x = dx.reshape(x.shape)
    recv[("mix", 0)] = _exchange(*scatter_side(("mix", 0)), "exchange_last")

    small_names = [n for n in ORDER if n not in
                   ("ada_w", "ada_b", "attn_wqkv", "attn_wo", "conv_w_in", "conv_w_out", "sgu_w_in", "sgu_w_out",
                    "ffn_w_in", "ffn_w_out", "final_g")]
    g_small_full = [jnp.stack(G[n]) for n in small_names] + [d_final_g[0]]
    small_names = small_names + ["final_g"]
    dmod_arr = jnp.stack(dmod)
    g_small_in = g_small_full + [dmod_arr]
    g_shapes = [a.shape for a in g_small_in]
    packed_gs = _pack(g_small_in, 128)
    gathered_gs = _all_gather(packed_gs, "gather_small_grads")
    summed_gs = _sum8(gathered_gs, "sum_small_grads").reshape(-1)
    gsum = dict(zip(small_names + ["ada_b"], _unpack(summed_gs, g_shapes)))

    def local_shard(n, a):
        if n in small_sharded:
            w = W[n].shape[-1]
            return lax.dynamic_slice_in_dim(a, me * w, w, axis=a.ndim - 1)
        return a

    gsum = {n: local_shard(n, a) for n, a in gsum.items()}

    dmod_all = jnp.stack([_unpack(gathered_gs[d].reshape(-1), g_shapes)[-1] for d in range(N_DEV)])
    dmod_cols = lax.dynamic_slice_in_dim(dmod_all, me * n_ada, n_ada, axis=2)
    dmod_pad = jnp.pad(dmod_cols, ((0, 8), (0, 0), (0, 0))).astype(BF16)
    g_ada_w = jnp.stack([_mm(c_pad, dmod_pad[:, i, :], "tn", "ada_dw") for i in range(depth)])

    big = {}
    for i in range(depth):
        for part in ("mix", "ffn"):
            g_rows = _sum8(recv[(part, i)], "sum_weight_grads")
            o = 0
            for k, (n, l, r) in enumerate(group_pieces((part, i))):
                gp = g_rows[o:o + r]
                big.setdefault(n, {})[l] = gp.T if k == 0 else gp
                o += r
    grads = {n: jnp.stack([v[l] for l in range(len(v))]) for n, v in big.items()}
    grads["ada_w"] = g_ada_w
    grads.update(gsum)

    delta, new_m, new_v = {}, {}, {}
    big_names = ["ada_w", "attn_wqkv", "attn_wo", "conv_w_in", "conv_w_out", "sgu_w_in", "sgu_w_out", "ffn_w_in",
                 "ffn_w_out"]
    for n in big_names:
        delta[n], new_m[n], new_v[n] = _adamw_nd(W[n], grads[n], MOM[n], VAR[n], "adamw_" + n)
    rest = [n for n in ORDER if n not in big_names]
    rest_shapes = [W[n].shape for n in rest]
    pk = lambda d: _pack([d[n] for n in rest], 128)
    d_s, m_s, v_s = _adamw(pk(W), pk(grads), pk(MOM), pk(VAR), "adamw_small")
    for n, a, b_, c_ in zip(rest, _unpack(d_s.reshape(-1), rest_shapes), _unpack(m_s.reshape(-1), rest_shapes),
                            _unpack(v_s.reshape(-1), rest_shapes)):
        delta[n], new_m[n], new_v[n] = a, b_, c_

    return (loss, grad_x, *[grads[n] for n in ORDER], *[delta[n] for n in ORDER],
            *[new_m[n] for n in ORDER], *[new_v[n] for n in ORDER])
```

```python
import functools
import math

import jax
import jax.numpy as jnp
from jax import lax
from jax.experimental import pallas as pl
from jax.experimental.pallas import tpu as pltpu

F32 = jnp.float32
BF16 = jnp.bfloat16

N_DEV = 8
HEAD_DIM = 64
N_KV = 4
BLOCK = 128
NORM_EPS = 1e-6
NEG_INF = -1e30
ADAM_LR = 0.001
ADAM_B1 = 0.9
ADAM_B2 = 0.999
ADAM_EPS = 1e-08
ADAM_WD = 0.01
ADAM_STEP = 10
V7X_VMEM_LIMIT = 56 * 1024 * 1024
MESH = pl.DeviceIdType.MESH


def _cparams():
    return pltpu.CompilerParams(vmem_limit_bytes=V7X_VMEM_LIMIT)


def _tile(n, cap, align):
    best = None
    for t in range(align, min(n, cap) + 1, align):
        if n % t == 0:
            best = t
    return best if best is not None else n


def _dot(a, b, ca, cb):
    return lax.dot_general(a, b, (((ca,), (cb,)), ((), ())), preferred_element_type=F32)


def _sigmoid(x):
    return 1.0 / (1.0 + jnp.exp(-x))


def _rows(x, start, n):
    return lax.slice_in_dim(x, start, start + n, axis=0)


def _my_coords():
    return lax.axis_index("x"), lax.axis_index("y"), lax.axis_index("c")


def _peer(m, mx, my, mc):
    px = (mx + ((m >> 2) & 1)) % 2
    py = (my + ((m >> 1) & 1)) % 2
    pc = (mc + (m & 1)) % 2
    return px, py, pc


def _exchange_copies(kind, x_ref, o_ref, send_sems, recv_sems, local_sem):
    mx, my, mc = _my_coords()
    me = 4 * mx + 2 * my + mc
    local = pltpu.make_async_copy(x_ref if kind == "gather" else x_ref.at[me], o_ref.at[me], local_sem)
    remote = []
    for m in range(1, N_DEV):
        px, py, pc = _peer(m, mx, my, mc)
        src = x_ref if kind == "gather" else x_ref.at[4 * px + 2 * py + pc]
        remote.append(pltpu.make_async_remote_copy(
            src_ref=src, dst_ref=o_ref.at[me], send_sem=send_sems.at[m - 1], recv_sem=recv_sems.at[m - 1],
            device_id=(px, py, pc), device_id_type=MESH))
    return local, remote


def _exchange_start(kind, *refs):
    local, remote = _exchange_copies(kind, *refs)
    local.start()
    for cp in remote:
        cp.start()


def _exchange_wait(kind, *refs):
    local, remote = _exchange_copies(kind, *refs)
    for cp in remote:
        cp.wait_recv()
    for cp in remote:
        cp.wait_send()
    local.wait()


_EXCHANGE_SEMS = [pltpu.SemaphoreType.DMA((N_DEV - 1,)), pltpu.SemaphoreType.DMA((N_DEV - 1,)), pltpu.SemaphoreType.DMA]


def _exchange_shape(kind, x):
    return jax.ShapeDtypeStruct(((N_DEV,) + x.shape) if kind == "gather" else x.shape, x.dtype)


def _exchange(kind, xs, name):
    n = len(xs)

    def body(*refs):
        x_refs, o_refs, sems = refs[:n], refs[n:2 * n], refs[2 * n:]
        for e in range(n):
            _exchange_start(kind, x_refs[e], o_refs[e], *sems[3 * e:3 * e + 3])
        for e in range(n):
            _exchange_wait(kind, x_refs[e], o_refs[e], *sems[3 * e:3 * e + 3])

    any_spec = pl.BlockSpec(memory_space=pl.ANY)
    return tuple(pl.pallas_call(
        body, name=name, out_shape=tuple(_exchange_shape(kind, x) for x in xs),
        in_specs=[any_spec] * n, out_specs=(any_spec,) * n, scratch_shapes=list(_EXCHANGE_SEMS) * n,
    )(*xs))


def _all_gather(x, name):
    return _exchange("gather", [x], name)[0]


def _call(body, name, grid, in_specs, out_specs, out_shape, args, scratch=(), side=None):
    single = not isinstance(out_shape, (tuple, list))
    if single:
        out_shape, out_specs = (out_shape,), (out_specs,)
    if side is None:
        res = pl.pallas_call(body, name=name, grid=grid, in_specs=list(in_specs), out_specs=tuple(out_specs),
                             out_shape=tuple(out_shape), scratch_shapes=list(scratch), compiler_params=_cparams())(*args)
        return res[0] if single else tuple(res)
    kind, xs = side
    n_in, n_out, n_scr, n_x = len(in_specs), len(out_shape), len(scratch), len(xs)

    def wrapped(*refs):
        ins, x_refs = refs[:n_in], refs[n_in:n_in + n_x]
        o0 = n_in + n_x
        outs, o_refs = refs[o0:o0 + n_out], refs[o0 + n_out:o0 + n_out + n_x]
        s0 = o0 + n_out + n_x
        scr, sems = refs[s0:s0 + n_scr], refs[s0 + n_scr:]
        ids = [pl.program_id(a) for a in range(len(grid))]
        first = functools.reduce(jnp.logical_and, [i == 0 for i in ids])
        last = functools.reduce(jnp.logical_and, [i == g - 1 for i, g in zip(ids, grid)])

        @pl.when(first)
        def _():
            for e in range(n_x):
                _exchange_start(kind, x_refs[e], o_refs[e], *sems[3 * e:3 * e + 3])

        body(*ins, *outs, *scr)

        @pl.when(last)
        def _():
            for e in range(n_x):
                _exchange_wait(kind, x_refs[e], o_refs[e], *sems[3 * e:3 * e + 3])

    any_spec = pl.BlockSpec(memory_space=pl.ANY)
    res = pl.pallas_call(
        wrapped, name=name, grid=grid, in_specs=list(in_specs) + [any_spec] * n_x,
        out_specs=tuple(out_specs) + (any_spec,) * n_x,
        out_shape=tuple(out_shape) + tuple(_exchange_shape(kind, x) for x in xs),
        scratch_shapes=list(scratch) + list(_EXCHANGE_SEMS) * n_x, compiler_params=_cparams())(*args, *xs)
    return tuple(res[:n_out]) + (tuple(res[n_out:]),)


def _mm(a, b, mode, name, out_dtype=F32, bias=None, side=None, tm_cap=1024, tn_cap=1408, tk_cap=1408):
    if mode == "nn":
        (M, K), (K2, N) = a.shape, b.shape
    elif mode == "nt":
        (M, K), (N, K2) = a.shape, b.shape
    else:
        (K, M), (K2, N) = a.shape, b.shape
    assert K == K2, (a.shape, b.shape, mode)
    if mode == "tn":
        tm_cap, tk_cap = 256, 8192
    tm = _tile(M, tm_cap, 128 if mode == "tn" else 16)
    tn = _tile(N, tn_cap, 128)
    tk = _tile(K, tk_cap, 128)
    nk = K // tk
    ca, cb = {"nn": (1, 0), "nt": (1, 1), "tn": (0, 0)}[mode]
    has_bias = bias is not None

    def body(*refs):
        if has_bias:
            a_ref, b_ref, bias_ref, o_ref, acc_ref = refs
        else:
            a_ref, b_ref, o_ref, acc_ref = refs
        k = pl.program_id(2)
        part = _dot(a_ref[...].astype(BF16), b_ref[...].astype(BF16), ca, cb)

        @pl.when(k == 0)
        def _():
            acc_ref[...] = part

        @pl.when(k > 0)
        def _():
            acc_ref[...] += part

        @pl.when(k == nk - 1)
        def _():
            r = acc_ref[...]
            if has_bias:
                r = r + bias_ref[...]
            o_ref[...] = r.astype(out_dtype)

    if mode == "tn":
        a_spec = pl.BlockSpec((tk, tm), lambda i, j, k: (k, i))
    else:
        a_spec = pl.BlockSpec((tm, tk), lambda i, j, k: (i, k))
    if mode == "nt":
        b_spec = pl.BlockSpec((tn, tk), lambda i, j, k: (j, k))
    else:
        b_spec = pl.BlockSpec((tk, tn), lambda i, j, k: (k, j))
    in_specs = [a_spec, b_spec]
    args = [a, b]
    if has_bias:
        in_specs.append(pl.BlockSpec((1, tn), lambda i, j, k: (0, j)))
        args.append(bias)
    return _call(body, name, (M // tm, N // tn, nk), in_specs, pl.BlockSpec((tm, tn), lambda i, j, k: (i, j)),
                 jax.ShapeDtypeStruct((M, N), out_dtype), args, scratch=[pltpu.VMEM((tm, tn), F32)], side=side)


def _vec_spec(n):
    return pl.BlockSpec((1, n), lambda t: (0, 0))


def _norm_mod_fwd(x, g, sc, sh, name, y=None, gate=None):
    T, D = x.shape
    tr = _tile(T, 256, 16)
    has_y = y is not None

    def body(*refs):
        if has_y:
            x_ref, y_ref, gate_ref, g_ref, sc_ref, sh_ref, xn_ref, h_ref = refs
            xv = x_ref[...] + gate_ref[...] * y_ref[...]
            xn_ref[...] = xv
        else:
            x_ref, g_ref, sc_ref, sh_ref, h_ref = refs
            xv = x_ref[...]
        r = lax.rsqrt(jnp.mean(xv * xv, axis=-1, keepdims=True) + NORM_EPS)
        h = (xv * r * g_ref[...]) * (1.0 + sc_ref[...]) + sh_ref[...]
        h_ref[...] = h.astype(BF16)

    row = pl.BlockSpec((tr, D), lambda t: (t, 0))
    vec = _vec_spec(D)
    if has_y:
        in_specs, args = [row, row, vec, vec, vec, vec], [x, y, gate, g, sc, sh]
        out_shape = (jax.ShapeDtypeStruct((T, D), F32), jax.ShapeDtypeStruct((T, D), BF16))
        out_specs = (row, row)
    else:
        in_specs, args = [row, vec, vec, vec], [x, g, sc, sh]
        out_shape = jax.ShapeDtypeStruct((T, D), BF16)
        out_specs = row
    return _call(body, name, (T // tr,), in_specs, out_specs, out_shape, args)


def _gate_bwd(dxn, y, gate, name):
    T, D = dxn.shape
    tr = _tile(T, 256, 16)

    def body(dxn_ref, y_ref, gate_ref, dy_ref, dgate_ref, dbias_ref):
        @pl.when(pl.program_id(0) == 0)
        def _():
            dgate_ref[...] = jnp.zeros_like(dgate_ref)
            dbias_ref[...] = jnp.zeros_like(dbias_ref)
        d = dxn_ref[...]
        dy = d * gate_ref[...]
        dy_ref[...] = dy.astype(BF16)
        dgate_ref[...] += jnp.sum(d * y_ref[...], axis=0, keepdims=True)
        dbias_ref[...] += jnp.sum(dy, axis=0, keepdims=True)

    row = pl.BlockSpec((tr, D), lambda t: (t, 0))
    vec = _vec_spec(D)
    return _call(body, name, (T // tr,), [row, row, vec], (row, vec, vec),
                 (jax.ShapeDtypeStruct((T, D), BF16), jax.ShapeDtypeStruct((1, D), F32),
                  jax.ShapeDtypeStruct((1, D), F32)), (dxn, y, gate))


def _norm_mod_bwd(dh, x, g, sc, sh, dxn, name):
    T, D = x.shape
    tr = _tile(T, 256, 16)

    def body(dh_ref, x_ref, g_ref, sc_ref, sh_ref, dxn_ref, dx_ref, dg_ref, dsc_ref, dsh_ref):
        @pl.when(pl.program_id(0) == 0)
        def _():
            dg_ref[...] = jnp.zeros_like(dg_ref)
            dsc_ref[...] = jnp.zeros_like(dsc_ref)
            dsh_ref[...] = jnp.zeros_like(dsh_ref)
        dh = dh_ref[...].astype(F32)
        xv = x_ref[...]
        gv = g_ref[...]
        r = lax.rsqrt(jnp.mean(xv * xv, axis=-1, keepdims=True) + NORM_EPS)
        yv = xv * r
        dsh_ref[...] += jnp.sum(dh, axis=0, keepdims=True)
        dsc_ref[...] += jnp.sum(dh * (yv * gv), axis=0, keepdims=True)
        dn = dh * (1.0 + sc_ref[...])
        dg_ref[...] += jnp.sum(dn * yv, axis=0, keepdims=True)
        dy = dn * gv
        dx = r * (dy - yv * jnp.mean(dy * yv, axis=-1, keepdims=True))
        dx_ref[...] = dxn_ref[...] + dx

    row = pl.BlockSpec((tr, D), lambda t: (t, 0))
    vec = _vec_spec(D)
    vshape = jax.ShapeDtypeStruct((1, D), F32)
    return _call(body, name, (T // tr,), [row, row, vec, vec, vec, row], (row, vec, vec, vec),
                 (jax.ShapeDtypeStruct((T, D), F32), vshape, vshape, vshape), (dh, x, g, sc, sh, dxn))


def _final_loss(x, y, gate, tgt, g, name):
    T, D = x.shape
    tr = _tile(T, 256, 16)

    def body(x_ref, y_ref, gate_ref, tgt_ref, g_ref, dx_ref, dg_ref, loss_ref):
        @pl.when(pl.program_id(0) == 0)
        def _():
            dg_ref[...] = jnp.zeros_like(dg_ref)
            loss_ref[...] = jnp.zeros_like(loss_ref)
        xv = x_ref[...] + gate_ref[...] * y_ref[...]
        gv = g_ref[...]
        r = lax.rsqrt(jnp.mean(xv * xv, axis=-1, keepdims=True) + NORM_EPS)
        yv = xv * r
        e = yv * gv - tgt_ref[...]
        per_row = jnp.mean(e * e, axis=-1, keepdims=True)
        loss_ref[...] += 0.5 * jnp.sum(per_row, axis=0, keepdims=True)
        dout = e * (1.0 / D)
        dg_ref[...] += jnp.sum(dout * yv, axis=0, keepdims=True)
        dy = dout * gv
        dx_ref[...] = r * (dy - yv * jnp.mean(dy * yv, axis=-1, keepdims=True))

    row = pl.BlockSpec((tr, D), lambda t: (t, 0))
    vec = _vec_spec(D)
    return _call(body, name, (T // tr,), [row, row, vec, row, vec], (row, vec, _vec_spec(128)),
                 (jax.ShapeDtypeStruct((T, D), F32), jax.ShapeDtypeStruct((1, D), F32),
                  jax.ShapeDtypeStruct((1, 128), F32)), (x, y, gate, tgt, g))


FFN_HALO = 16
FFN_PAD = 8
FFN_CHUNK = 64


def _lane_groups(cw):
    lw = 128 if cw % 128 == 0 else cw
    return lw, [pl.ds(g * lw, lw) for g in range(cw // lw)]


def _ffn_mid_fwd(z, dw, b, name, side=None):
    T, F2 = z.shape
    F = F2 // 2
    K = dw.shape[0]
    H, P = FFN_HALO, FFN_PAD
    tr = _tile(T, 512, H)
    cw = _tile(F, 512, 128)
    nc = F // cw
    rc = _tile(tr, FFN_CHUNK, 16)
    lw, groups = _lane_groups(cw)

    def body(gc_ref, gh_ref, uc_ref, uh_ref, wg_ref, wu_ref, bg_ref, bu_ref, a_ref, xg_s, xu_s):
        first = pl.program_id(1) == 0

        def fill(s, h_ref, c_ref):
            for g, ls in enumerate(groups):
                s[g, 0:P, :] = jnp.where(first, 0.0, h_ref[:, ls].astype(F32)[H - P:H])
                s[g, P:P + tr, :] = c_ref[:, ls].astype(F32)

        fill(xg_s, gh_ref, gc_ref)
        fill(xu_s, uh_ref, uc_ref)

        def chunk(ci, carry):
            r0 = pl.multiple_of(ci * rc, rc)
            for g, ls in enumerate(groups):
                def conv(s, w_ref, b_ref):
                    acc = b_ref[:, ls] + w_ref[K - 1:K, ls] * s[g, pl.ds(r0 + P, rc), :]
                    for k in range(K - 1):
                        acc = acc + w_ref[k:k + 1, ls] * s[g, pl.ds(r0 + P - (K - 1 - k), rc), :]
                    return acc

                gv = conv(xg_s, wg_ref, bg_ref)
                uv = conv(xu_s, wu_ref, bu_ref)
                a_ref[pl.ds(r0, rc), ls] = (gv * _sigmoid(gv) * uv).astype(BF16)
            return carry

        lax.fori_loop(0, tr // rc, chunk, 0)

    rb = tr // H
    in_specs = [
        pl.BlockSpec((tr, cw), lambda j, t: (t, j)),
        pl.BlockSpec((H, cw), lambda j, t: (jnp.maximum(t * rb - 1, 0), j)),
        pl.BlockSpec((tr, cw), lambda j, t: (t, j + nc)),
        pl.BlockSpec((H, cw), lambda j, t: (jnp.maximum(t * rb - 1, 0), j + nc)),
        pl.BlockSpec((K, cw), lambda j, t: (0, j)),
        pl.BlockSpec((K, cw), lambda j, t: (0, j + nc)),
        pl.BlockSpec((1, cw), lambda j, t: (0, j)),
        pl.BlockSpec((1, cw), lambda j, t: (0, j + nc)),
    ]
    return _call(body, name, (nc, T // tr), in_specs, pl.BlockSpec((tr, cw), lambda j, t: (t, j)),
                 jax.ShapeDtypeStruct((T, F), BF16), (z, z, z, z, dw, dw, b, b),
                 scratch=[pltpu.VMEM((len(groups), P + tr, lw), F32)] * 2, side=side)


def _ffn_mid_bwd(z, da, dw, b, name, side=None):
    T, F2 = z.shape
    F = F2 // 2
    K = dw.shape[0]
    H, P = FFN_HALO, FFN_PAD
    tr = _tile(T, 512, H)
    cw = _tile(F, 512, 128)
    nc = F // cw
    nt = T // tr
    rb = tr // H
    rc = _tile(tr, FFN_CHUNK, 16)
    ext = rc + P
    lw, groups = _lane_groups(cw)

    def body(gp_ref, gc_ref, gn_ref, up_ref, uc_ref, un_ref, dac_ref, dan_ref, wg_ref, wu_ref, bg_ref, bu_ref,
             dzg_ref, dzu_ref, dwg_ref, dwu_ref, dbg_ref, dbu_ref, xg_s, xu_s, da_s, dg_s, du_s, acc_s):
        t = pl.program_id(1)
        first = t == 0
        last = t == nt - 1

        @pl.when(first)
        def _():
            for r in (dwg_ref, dwu_ref, dbg_ref, dbu_ref):
                r[...] = jnp.zeros_like(r)

        def fill(s, p_ref, c_ref, n_ref):
            for g, ls in enumerate(groups):
                s[g, 0:P, :] = jnp.where(first, 0.0, p_ref[:, ls].astype(F32)[H - P:H])
                s[g, P:P + tr, :] = c_ref[:, ls].astype(F32)
                s[g, P + tr:2 * P + tr, :] = jnp.where(last, 0.0, n_ref[:, ls].astype(F32)[0:P])

        fill(xg_s, gp_ref, gc_ref, gn_ref)
        fill(xu_s, up_ref, uc_ref, un_ref)
        for g, ls in enumerate(groups):
            da_s[g, 0:tr, :] = dac_ref[:, ls].astype(F32)
            da_s[g, tr:tr + P, :] = jnp.where(last, 0.0, dan_ref[:, ls].astype(F32)[0:P])
        acc_s[...] = jnp.zeros_like(acc_s)

        def fold(v):
            return jnp.sum(v.reshape(rc // 8, 8, lw), axis=0)

        def chunk(ci, carry):
            r0 = pl.multiple_of(ci * rc, rc)
            for g, ls in enumerate(groups):
                def conv_ext(s, w_ref, b_ref):
                    acc = b_ref[:, ls] + w_ref[K - 1:K, ls] * s[g, pl.ds(r0 + P, ext), :]
                    for k in range(K - 1):
                        acc = acc + w_ref[k:k + 1, ls] * s[g, pl.ds(r0 + P - (K - 1 - k), ext), :]
                    return acc

                gv = conv_ext(xg_s, wg_ref, bg_ref)
                uv = conv_ext(xu_s, wu_ref, bu_ref)
                dav = da_s[g, pl.ds(r0, ext), :]
                sg = _sigmoid(gv)
                dg = dav * uv * (sg * (1.0 + gv * (1.0 - sg)))
                du = dav * (gv * sg)
                dg_s[g] = dg
                du_s[g] = du
                for idx, (d_s, dval, x_s, w_ref, dz_ref) in enumerate(
                        ((dg_s, dg, xg_s, wg_ref, dzg_ref), (du_s, du, xu_s, wu_ref, dzu_ref))):
                    cur = dval[:rc]
                    acc_s[idx, K, g] += fold(cur)
                    for k in range(K):
                        acc_s[idx, k, g] += fold(cur * x_s[g, pl.ds(r0 + P - (K - 1 - k), rc), :])
                    dz = w_ref[K - 1:K, ls] * cur
                    for k in range(K - 1):
                        dz = dz + w_ref[k:k + 1, ls] * d_s[g, pl.ds(K - 1 - k, rc), :]
                    dz_ref[pl.ds(r0, rc), ls] = dz.astype(BF16)
            return carry

        lax.fori_loop(0, tr // rc, chunk, 0)
        for idx, (dw_ref, db_ref) in enumerate(((dwg_ref, dbg_ref), (dwu_ref, dbu_ref))):
            for g, ls in enumerate(groups):
                db_ref[:, ls] += jnp.sum(acc_s[idx, K, g], axis=0, keepdims=True)
                dw_ref[:, ls] += jnp.concatenate(
                    [jnp.sum(acc_s[idx, k, g], axis=0, keepdims=True) for k in range(K)], axis=0)

    def prev(t):
        return jnp.maximum(t * rb - 1, 0)

    def nxt(t):
        return jnp.minimum((t + 1) * rb, T // H - 1)

    def zspecs(off):
        return [pl.BlockSpec((H, cw), lambda j, t: (prev(t), j + off)),
                pl.BlockSpec((tr, cw), lambda j, t: (t, j + off)),
                pl.BlockSpec((H, cw), lambda j, t: (nxt(t), j + off))]

    in_specs = zspecs(0) + zspecs(nc) + [
        pl.BlockSpec((tr, cw), lambda j, t: (t, j)),
        pl.BlockSpec((H, cw), lambda j, t: (nxt(t), j)),
        pl.BlockSpec((K, cw), lambda j, t: (0, j)),
        pl.BlockSpec((K, cw), lambda j, t: (0, j + nc)),
        pl.BlockSpec((1, cw), lambda j, t: (0, j)),
        pl.BlockSpec((1, cw), lambda j, t: (0, j + nc)),
    ]
    out_shape = (jax.ShapeDtypeStruct((T, F), BF16), jax.ShapeDtypeStruct((T, F), BF16),
                 jax.ShapeDtypeStruct((K, F), F32), jax.ShapeDtypeStruct((K, F), F32),
                 jax.ShapeDtypeStruct((1, F), F32), jax.ShapeDtypeStruct((1, F), F32))
    out_specs = (pl.BlockSpec((tr, cw), lambda j, t: (t, j)), pl.BlockSpec((tr, cw), lambda j, t: (t, j)),
                 pl.BlockSpec((K, cw), lambda j, t: (0, j)), pl.BlockSpec((K, cw), lambda j, t: (0, j)),
                 pl.BlockSpec((1, cw), lambda j, t: (0, j)), pl.BlockSpec((1, cw), lambda j, t: (0, j)))
    ng = len(groups)
    scratch = [pltpu.VMEM((ng, 2 * P + tr, lw), F32), pltpu.VMEM((ng, 2 * P + tr, lw), F32),
               pltpu.VMEM((ng, P + tr, lw), F32), pltpu.VMEM((ng, ext, lw), F32), pltpu.VMEM((ng, ext, lw), F32),
               pltpu.VMEM((2, K + 1, ng, 8, lw), F32)]
    res = _call(body, name, (nc, nt), in_specs, out_specs, out_shape, (z, z, z, z, z, z, da, da, dw, dw, b, b),
                scratch=scratch, side=side)
    dzg, dzu, dwg, dwu, dbg, dbu = res[:6]
    return (jnp.concatenate([dzg, dzu], axis=1), jnp.concatenate([dwg, dwu], axis=1),
            jnp.concatenate([dbg, dbu], axis=1)) + tuple(res[6:])


CONV_HALO = 32


def _convmix_fwd(p, dw, dwb, lng, lnb, name, side=None):
    T, D2 = p.shape
    D = D2 // 2
    K = dw.shape[0]
    H = CONV_HALO
    tr = _tile(T, 256, H)
    rb = tr // H

    def body(pc_ref, ph_ref, w_ref, wb_ref, lng_ref, lnb_ref, zc_ref, s_ref):
        first = pl.program_id(0) == 0
        pp = jnp.concatenate([jnp.where(first, 0.0, ph_ref[...].astype(F32)), pc_ref[...].astype(F32)], axis=0)
        zg = pp[:, :D] * _sigmoid(pp[:, D:])
        w = w_ref[...]
        acc = wb_ref[...] + w[K - 1:K, :] * _rows(zg, H, tr)
        for k in range(K - 1):
            acc = acc + w[k:k + 1, :] * _rows(zg, H - (K - 1 - k), tr)
        zc_ref[...] = acc
        mu = jnp.mean(acc, axis=-1, keepdims=True)
        xc = acc - mu
        rstd = lax.rsqrt(jnp.mean(xc * xc, axis=-1, keepdims=True) + NORM_EPS)
        ln = xc * rstd * lng_ref[...] + lnb_ref[...]
        s_ref[...] = (ln * _sigmoid(ln)).astype(BF16)

    in_specs = [pl.BlockSpec((tr, D2), lambda t: (t, 0)),
                pl.BlockSpec((H, D2), lambda t: (jnp.maximum(t * rb - 1, 0), 0)),
                pl.BlockSpec((K, D), lambda t: (0, 0)), _vec_spec(D), _vec_spec(D), _vec_spec(D)]
    row = pl.BlockSpec((tr, D), lambda t: (t, 0))
    return _call(body, name, (T // tr,), in_specs, (row, row),
                 (jax.ShapeDtypeStruct((T, D), F32), jax.ShapeDtypeStruct((T, D), BF16)), (p, p, dw, dwb, lng, lnb), side=side)


def _convmix_bwd(ds, zc, p, dw, lng, lnb, name, side=None):
    T, D = zc.shape
    D2 = 2 * D
    K = dw.shape[0]
    H = CONV_HALO
    tr = _tile(T, 256, H)
    rb = tr // H
    nt = T // tr
    ext = tr + H

    def body(dsc_ref, dsn_ref, zcc_ref, zcn_ref, pp_ref, pc_ref, w_ref, lng_ref, lnb_ref,
             dp_ref, dbin_ref, ddw_ref, ddwb_ref, dlng_ref, dlnb_ref):
        t = pl.program_id(0)
        first = t == 0
        last = t == nt - 1

        @pl.when(first)
        def _():
            for r in (dbin_ref, ddw_ref, ddwb_ref, dlng_ref, dlnb_ref):
                r[...] = jnp.zeros_like(r)

        dsv = jnp.concatenate([dsc_ref[...].astype(F32), jnp.where(last, 0.0, dsn_ref[...].astype(F32))], axis=0)
        zcv = jnp.concatenate([zcc_ref[...], zcn_ref[...]], axis=0)
        lg = lng_ref[...]
        mu = jnp.mean(zcv, axis=-1, keepdims=True)
        xc = zcv - mu
        rstd = lax.rsqrt(jnp.mean(xc * xc, axis=-1, keepdims=True) + NORM_EPS)
        xh = xc * rstd
        ln = xh * lg + lnb_ref[...]
        sg = _sigmoid(ln)
        dln = dsv * (sg * (1.0 + ln * (1.0 - sg)))
        dlnc = _rows(dln, 0, tr)
        dlnb_ref[...] += jnp.sum(dlnc, axis=0, keepdims=True)
        dlng_ref[...] += jnp.sum(dlnc * _rows(xh, 0, tr), axis=0, keepdims=True)
        dxh = dln * lg
        dzc = rstd * (dxh - jnp.mean(dxh, axis=-1, keepdims=True) - xh * jnp.mean(dxh * xh, axis=-1, keepdims=True))
        dzcc = _rows(dzc, 0, tr)
        ddwb_ref[...] += jnp.sum(dzcc, axis=0, keepdims=True)

        pv = jnp.concatenate([jnp.where(first, 0.0, pp_ref[...].astype(F32)), pc_ref[...].astype(F32)], axis=0)
        av, gv = pv[:, :D], pv[:, D:]
        sgg = _sigmoid(gv)
        zg = av * sgg
        w = w_ref[...]
        parts = []
        dzg = w[K - 1:K, :] * dzcc
        for k in range(K):
            parts.append(jnp.sum(dzcc * _rows(zg, H - (K - 1 - k), tr), axis=0, keepdims=True))
            if k < K - 1:
                dzg = dzg + w[k:k + 1, :] * _rows(dzc, K - 1 - k, tr)
        ddw_ref[...] += jnp.concatenate(parts, axis=0)
        ac, sc_ = _rows(av, H, tr), _rows(sgg, H, tr)
        dpa = dzg * sc_
        dpg = dzg * ac * sc_ * (1.0 - sc_)
        dpv = jnp.concatenate([dpa, dpg], axis=1)
        dp_ref[...] = dpv.astype(BF16)
        dbin_ref[...] += jnp.sum(dpv, axis=0, keepdims=True)

    def prev(t):
        return jnp.maximum(t * rb - 1, 0)

    def nxt(t):
        return jnp.minimum((t + 1) * rb, T // H - 1)

    in_specs = [pl.BlockSpec((tr, D), lambda t: (t, 0)), pl.BlockSpec((H, D), lambda t: (nxt(t), 0)),
                pl.BlockSpec((tr, D), lambda t: (t, 0)), pl.BlockSpec((H, D), lambda t: (nxt(t), 0)),
                pl.BlockSpec((H, D2), lambda t: (prev(t), 0)), pl.BlockSpec((tr, D2), lambda t: (t, 0)),
                pl.BlockSpec((K, D), lambda t: (0, 0)), _vec_spec(D), _vec_spec(D)]
    out_shape = (jax.ShapeDtypeStruct((T, D2), BF16), jax.ShapeDtypeStruct((1, D2), F32),
                 jax.ShapeDtypeStruct((K, D), F32), jax.ShapeDtypeStruct((1, D), F32),
                 jax.ShapeDtypeStruct((1, D), F32), jax.ShapeDtypeStruct((1, D), F32))
    out_specs = (pl.BlockSpec((tr, D2), lambda t: (t, 0)), _vec_spec(D2), pl.BlockSpec((K, D), lambda t: (0, 0)),
                 _vec_spec(D), _vec_spec(D), _vec_spec(D))
    return _call(body, name, (nt,), in_specs, out_specs, out_shape, (ds, ds, zc, zc, p, p, dw, lng, lnb), side=side)


_INV_SQRT2 = 1.0 / math.sqrt(2.0)
_INV_SQRT2PI = 1.0 / math.sqrt(2.0 * math.pi)


def _gelu(x):
    return 0.5 * x * (1.0 + lax.erf(x * _INV_SQRT2))


def _gelu_grad(x):
    return 0.5 * (1.0 + lax.erf(x * _INV_SQRT2)) + x * jnp.exp(-0.5 * x * x) * _INV_SQRT2PI


def _tril_mask(n, transposed=False):
    r = lax.broadcasted_iota(jnp.int32, (n, n), 0)
    c = lax.broadcasted_iota(jnp.int32, (n, n), 1)
    return (r <= c) if transposed else (r >= c)


def _sgu_fwd(p, lng, lnb, ws, bs_t, name):
    T, S2 = p.shape
    S = S2 // 2
    G, C, _ = ws.shape
    gd = S // G
    cpt = 2 if (T // C) % 2 == 0 else 1
    tr = C * cpt

    def body(p_ref, lng_ref, lnb_ref, ws_ref, bst_ref, m_ref):
        mask = _tril_mask(C)
        lg, lb = lng_ref[...], lnb_ref[...]
        bst = bst_ref[...]
        for ci in range(cpt):
            pc = p_ref[ci * C:(ci + 1) * C, :].astype(F32)
            z = _gelu(pc)
            u, v = z[:, :S], z[:, S:]
            mu = jnp.mean(v, axis=-1, keepdims=True)
            xc = v - mu
            rstd = lax.rsqrt(jnp.mean(xc * xc, axis=-1, keepdims=True) + NORM_EPS)
            vn = (xc * rstd * lg + lb).astype(BF16)
            outs = []
            for g in range(G):
                wm = jnp.where(mask, ws_ref[g], 0.0).astype(BF16)
                vs = _dot(wm, vn[:, g * gd:(g + 1) * gd], 1, 0) + bst[:, g:g + 1]
                outs.append(u[:, g * gd:(g + 1) * gd] * vs)
            m_ref[ci * C:(ci + 1) * C, :] = jnp.concatenate(outs, axis=1).astype(BF16)

    in_specs = [pl.BlockSpec((tr, S2), lambda t: (t, 0)), _vec_spec(S), _vec_spec(S),
                pl.BlockSpec((G, C, C), lambda t: (0, 0, 0)), pl.BlockSpec((C, G), lambda t: (0, 0))]
    return _call(body, name, (T // tr,), in_specs, pl.BlockSpec((tr, S), lambda t: (t, 0)),
                 jax.ShapeDtypeStruct((T, S), BF16), (p, lng, lnb, ws, bs_t))


def _sgu_bwd(p, dm, lng, lnb, ws, bs_t, name, side=None):
    T, S2 = p.shape
    S = S2 // 2
    G, C, _ = ws.shape
    gd = S // G
    nt = T // C

    def body(p_ref, dm_ref, lng_ref, lnb_ref, ws_ref, bst_ref, dp_ref, dbin_ref, dlng_ref, dlnb_ref, dws_ref, dbst_ref):
        @pl.when(pl.program_id(0) == 0)
        def _():
            for r in (dbin_ref, dlng_ref, dlnb_ref, dws_ref, dbst_ref):
                r[...] = jnp.zeros_like(r)

        mask = _tril_mask(C)
        lg, lb = lng_ref[...], lnb_ref[...]
        bst = bst_ref[...]
        pc = p_ref[...].astype(F32)
        dmv = dm_ref[...].astype(F32)
        z = _gelu(pc)
        u, v = z[:, :S], z[:, S:]
        mu = jnp.mean(v, axis=-1, keepdims=True)
        xc = v - mu
        rstd = lax.rsqrt(jnp.mean(xc * xc, axis=-1, keepdims=True) + NORM_EPS)
        vh = xc * rstd
        vn = (vh * lg + lb).astype(BF16)
        dus, dvns, dbcols = [], [], []
        for g in range(G):
            sl = slice(g * gd, (g + 1) * gd)
            wm = jnp.where(mask, ws_ref[g], 0.0).astype(BF16)
            vs = _dot(wm, vn[:, sl], 1, 0) + bst[:, g:g + 1]
            dmg = dmv[:, sl]
            dus.append(dmg * vs)
            dvs = dmg * u[:, sl]
            dbcols.append(jnp.sum(dvs, axis=-1, keepdims=True))
            dvsb = dvs.astype(BF16)
            dws_ref[g] += jnp.where(mask, _dot(dvsb, vn[:, sl], 1, 1), 0.0)
            dvns.append(_dot(wm, dvsb, 0, 0))
        dbst_ref[...] += jnp.concatenate(dbcols, axis=1)
        dvn = jnp.concatenate(dvns, axis=1)
        dlnb_ref[...] += jnp.sum(dvn, axis=0, keepdims=True)
        dlng_ref[...] += jnp.sum(dvn * vh, axis=0, keepdims=True)
        dvh = dvn * lg
        dv = rstd * (dvh - jnp.mean(dvh, axis=-1, keepdims=True) - vh * jnp.mean(dvh * vh, axis=-1, keepdims=True))
        dz = jnp.concatenate([jnp.concatenate(dus, axis=1), dv], axis=1)
        dpv = dz * _gelu_grad(pc)
        dp_ref[...] = dpv.astype(BF16)
        dbin_ref[...] += jnp.sum(dpv, axis=0, keepdims=True)

    in_specs = [pl.BlockSpec((C, S2), lambda t: (t, 0)), pl.BlockSpec((C, S), lambda t: (t, 0)),
                _vec_spec(S), _vec_spec(S), pl.BlockSpec((G, C, C), lambda t: (0, 0, 0)),
                pl.BlockSpec((C, G), lambda t: (0, 0))]
    out_shape = (jax.ShapeDtypeStruct((T, S2), BF16), jax.ShapeDtypeStruct((1, S2), F32),
                 jax.ShapeDtypeStruct((1, S), F32), jax.ShapeDtypeStruct((1, S), F32),
                 jax.ShapeDtypeStruct((G, C, C), F32), jax.ShapeDtypeStruct((C, G), F32))
    out_specs = (pl.BlockSpec((C, S2), lambda t: (t, 0)), _vec_spec(S2), _vec_spec(S), _vec_spec(S),
                 pl.BlockSpec((G, C, C), lambda t: (0, 0, 0)), pl.BlockSpec((C, G), lambda t: (0, 0)))
    return _call(body, name, (nt,), in_specs, out_specs, out_shape, (p, dm, lng, lnb, ws, bs_t), side=side)


def _alibi_slope(h, n_heads):
    return 2.0 ** (-8.0 * (h + 1) / n_heads)


def _attn_scores(q_h, k_h, slope, sink, first_block):
    B = BLOCK
    qi = lax.broadcasted_iota(jnp.int32, (B, 2 * B), 0)
    kj = lax.broadcasted_iota(jnp.int32, (B, 2 * B), 1)
    dist = qi + B - kj
    valid = (dist >= 0) & (dist < B) & jnp.logical_not(first_block & (kj < B))
    s = _dot(q_h, k_h, 1, 1) * (HEAD_DIM ** -0.5) - slope * dist.astype(F32)
    s = jnp.where(valid, s, NEG_INF)
    mx = jnp.maximum(jnp.max(s, axis=-1, keepdims=True), sink)
    e = jnp.exp(s - mx)
    es = jnp.exp(sink - mx)
    denom = jnp.sum(e, axis=-1, keepdims=True) + es
    return e, es, denom


def _attn_fwd(qkv, sinks, n_heads, name, side=None):
    T = qkv.shape[0]
    B = BLOCK
    HQ = n_heads * HEAD_DIM
    KVW = N_KV * HEAD_DIM
    group = n_heads // N_KV
    nb = T // B
    kcol = HQ // KVW

    def body(q_ref, kp_ref, kc_ref, vp_ref, vc_ref, sink_ref, o_ref):
        first = pl.program_id(0) == 0
        q = q_ref[...]
        k2 = jnp.concatenate([kp_ref[...], kc_ref[...]], axis=0)
        v2 = jnp.concatenate([vp_ref[...], vc_ref[...]], axis=0)
        sk = sink_ref[...]
        outs = []
        for h in range(n_heads):
            kv = h // group
            e, es, denom = _attn_scores(q[:, h * HEAD_DIM:(h + 1) * HEAD_DIM], k2[:, kv * HEAD_DIM:(kv + 1) * HEAD_DIM],
                                        _alibi_slope(h, n_heads), sk[:, h:h + 1], first)
            pr = (e * (1.0 / denom)).astype(BF16)
            outs.append(_dot(pr, v2[:, kv * HEAD_DIM:(kv + 1) * HEAD_DIM], 1, 0))
        o_ref[...] = jnp.concatenate(outs, axis=1).astype(BF16)

    def prev(n):
        return jnp.maximum(n - 1, 0)

    in_specs = [pl.BlockSpec((B, HQ), lambda n: (n, 0)),
                pl.BlockSpec((B, KVW), lambda n: (prev(n), kcol)), pl.BlockSpec((B, KVW), lambda n: (n, kcol)),
                pl.BlockSpec((B, KVW), lambda n: (prev(n), kcol + 1)), pl.BlockSpec((B, KVW), lambda n: (n, kcol + 1)),
                _vec_spec(n_heads)]
    return _call(body, name, (nb,), in_specs, pl.BlockSpec((B, HQ), lambda n: (n, 0)),
                 jax.ShapeDtypeStruct((T, HQ), BF16), (qkv, qkv, qkv, qkv, qkv, sinks), side=side)


def _attn_bwd(qkv, do, sinks, n_heads, name, side=None):
    T = qkv.shape[0]
    B = BLOCK
    HQ = n_heads * HEAD_DIM
    KVW = N_KV * HEAD_DIM
    group = n_heads // N_KV
    nb = T // B
    kcol = HQ // KVW
    scale = HEAD_DIM ** -0.5

    def body(q_ref, kp_ref, kc_ref, vp_ref, vc_ref, do_ref, sink_ref,
             dq_ref, dk_ref, dv_ref, dbq_ref, dbk_ref, dbv_ref, dsink_ref, ck_ref, cv_ref):
        n = pl.program_id(0)
        first = n == 0

        @pl.when(first)
        def _():
            for r in (dbq_ref, dbk_ref, dbv_ref, dsink_ref, ck_ref, cv_ref):
                r[...] = jnp.zeros_like(r)

        @pl.when(n < nb)
        def _():
            q = q_ref[...]
            dov = do_ref[...]
            k2 = jnp.concatenate([kp_ref[...], kc_ref[...]], axis=0)
            v2 = jnp.concatenate([vp_ref[...], vc_ref[...]], axis=0)
            sk = sink_ref[...]
            dqs, dsinks = [], []
            dks = [None] * N_KV
            dvs = [None] * N_KV
            for h in range(n_heads):
                kv = h // group
                hs = slice(h * HEAD_DIM, (h + 1) * HEAD_DIM)
                ks = slice(kv * HEAD_DIM, (kv + 1) * HEAD_DIM)
                e, es, denom = _attn_scores(q[:, hs], k2[:, ks], _alibi_slope(h, n_heads), sk[:, h:h + 1], first)
                inv = 1.0 / denom
                pr = e * inv
                dp = _dot(dov[:, hs], v2[:, ks], 1, 1)
                dsum = jnp.sum(pr * dp, axis=-1, keepdims=True)
                ds = (pr * (dp - dsum) * scale).astype(BF16)
                dsinks.append(-jnp.sum(es * inv * dsum, axis=0, keepdims=True))
                dqs.append(_dot(ds, k2[:, ks], 1, 0))
                dk_h = _dot(ds, q[:, hs], 0, 0)
                dv_h = _dot(pr.astype(BF16), dov[:, hs], 0, 0)
                dks[kv] = dk_h if dks[kv] is None else dks[kv] + dk_h
                dvs[kv] = dv_h if dvs[kv] is None else dvs[kv] + dv_h
            dq = jnp.concatenate(dqs, axis=1)
            dq_ref[...] = dq.astype(BF16)
            dbq_ref[...] += jnp.sum(dq, axis=0, keepdims=True)
            dsink_ref[...] += jnp.concatenate(dsinks, axis=1)
            dk2 = jnp.concatenate(dks, axis=1)
            dv2 = jnp.concatenate(dvs, axis=1)
            dbk_ref[...] += jnp.sum(dk2, axis=0, keepdims=True)
            dbv_ref[...] += jnp.sum(dv2, axis=0, keepdims=True)
            dk_ref[...] = (ck_ref[...] + dk2[:B]).astype(BF16)
            dv_ref[...] = (cv_ref[...] + dv2[:B]).astype(BF16)
            ck_ref[...] = dk2[B:]
            cv_ref[...] = dv2[B:]

        @pl.when(n == nb)
        def _():
            dk_ref[...] = ck_ref[...].astype(BF16)
            dv_ref[...] = cv_ref[...].astype(BF16)

    def cur(n):
        return jnp.minimum(n, nb - 1)

    def prev(n):
        return jnp.maximum(cur(n) - 1, 0)

    def outp(n):
        return jnp.maximum(n - 1, 0)

    in_specs = [pl.BlockSpec((B, HQ), lambda n: (cur(n), 0)),
                pl.BlockSpec((B, KVW), lambda n: (prev(n), kcol)), pl.BlockSpec((B, KVW), lambda n: (cur(n), kcol)),
                pl.BlockSpec((B, KVW), lambda n: (prev(n), kcol + 1)), pl.BlockSpec((B, KVW), lambda n: (cur(n), kcol + 1)),
                pl.BlockSpec((B, HQ), lambda n: (cur(n), 0)), _vec_spec(n_heads)]
    out_shape = (jax.ShapeDtypeStruct((T, HQ), BF16), jax.ShapeDtypeStruct((T, KVW), BF16),
                 jax.ShapeDtypeStruct((T, KVW), BF16), jax.ShapeDtypeStruct((1, HQ), F32),
                 jax.ShapeDtypeStruct((1, KVW), F32), jax.ShapeDtypeStruct((1, KVW), F32),
                 jax.ShapeDtypeStruct((1, n_heads), F32))
    out_specs = (pl.BlockSpec((B, HQ), lambda n: (cur(n), 0)), pl.BlockSpec((B, KVW), lambda n: (outp(n), 0)),
                 pl.BlockSpec((B, KVW), lambda n: (outp(n), 0)), _vec_spec(HQ), _vec_spec(KVW), _vec_spec(KVW),
                 _vec_spec(n_heads))
    return _call(body, name, (nb + 1,), in_specs, out_specs, out_shape, (qkv, qkv, qkv, qkv, qkv, do, sinks),
                 scratch=[pltpu.VMEM((B, KVW), F32), pltpu.VMEM((B, KVW), F32)], side=side)


def _sum8(r, name):
    _, R, C = r.shape
    tr = _tile(R, 512, 16)

    def body(r_ref, o_ref):
        acc = r_ref[0].astype(F32)
        for d in range(1, N_DEV):
            acc = acc + r_ref[d].astype(F32)
        o_ref[...] = acc

    return _call(body, name, (R // tr,), [pl.BlockSpec((N_DEV, tr, C), lambda i: (0, i, 0))],
                 pl.BlockSpec((tr, C), lambda i: (i, 0)), jax.ShapeDtypeStruct((R, C), F32), (r,))


def _adamw(w, g, m, v, name):
    R, C = w.shape
    tr = _tile(R, 512, 8)
    c1 = 1.0 - ADAM_B1 ** ADAM_STEP
    c2 = 1.0 - ADAM_B2 ** ADAM_STEP

    def body(w_ref, g_ref, m_ref, v_ref, d_ref, nm_ref, nv_ref):
        gv = g_ref[...]
        nm = ADAM_B1 * m_ref[...] + (1.0 - ADAM_B1) * gv
        nv = ADAM_B2 * v_ref[...] + (1.0 - ADAM_B2) * (gv * gv)
        nm_ref[...] = nm
        nv_ref[...] = nv
        d_ref[...] = -ADAM_LR * ((nm / c1) / (jnp.sqrt(nv / c2) + ADAM_EPS) + ADAM_WD * w_ref[...])

    spec = pl.BlockSpec((tr, C), lambda i: (i, 0))
    shp = jax.ShapeDtypeStruct((R, C), F32)
    return _call(body, name, (R // tr,), [spec] * 4, (spec,) * 3, (shp, shp, shp), (w, g, m, v))


def _adamw_nd(w, g, m, v, name):
    shape = w.shape
    c = shape[-1]
    f = lambda a: a.reshape(-1, c)
    d, nm, nv = _adamw(f(w), f(g), f(m), f(v), name)
    return d.reshape(shape), nm.reshape(shape), nv.reshape(shape)


def _pack(arrs, width):
    flat = jnp.concatenate([a.reshape(-1).astype(F32) for a in arrs])
    n = flat.shape[0]
    quantum = 8 * width
    total = -(-n // quantum) * quantum
    return jnp.pad(flat, (0, total - n)).reshape(-1, width)


def _unpack(flat, shapes):
    out, off = [], 0
    for s in shapes:
        n = math.prod(s)
        out.append(flat[off:off + n].reshape(s))
        off += n
    return out


def kernel(x, c, norm1_g, norm2_g, ada_w, ada_b, attn_wqkv, attn_bqkv, attn_sinks, attn_wo, attn_bo, conv_w_in, conv_b_in, conv_dw, conv_dw_b, conv_ln_g, conv_ln_b, conv_w_out, conv_b_out, sgu_w_in, sgu_b_in, sgu_ln_g, sgu_ln_b, sgu_ws, sgu_bs, sgu_w_out, sgu_b_out, ffn_w_in, ffn_dw, ffn_dw_b, ffn_w_out, final_g, loss_target, m_norm1_g, m_norm2_g, m_ada_w, m_ada_b, m_attn_wqkv, m_attn_bqkv, m_attn_sinks, m_attn_wo, m_attn_bo, m_conv_w_in, m_conv_b_in, m_conv_dw, m_conv_dw_b, m_conv_ln_g, m_conv_ln_b, m_conv_w_out, m_conv_b_out, m_sgu_w_in, m_sgu_b_in, m_sgu_ln_g, m_sgu_ln_b, m_sgu_ws, m_sgu_bs, m_sgu_w_out, m_sgu_b_out, m_ffn_w_in, m_ffn_dw, m_ffn_dw_b, m_ffn_w_out, m_final_g, v_norm1_g, v_norm2_g, v_ada_w, v_ada_b, v_attn_wqkv, v_attn_bqkv, v_attn_sinks, v_attn_wo, v_attn_bo, v_conv_w_in, v_conv_b_in, v_conv_dw, v_conv_dw_b, v_conv_ln_g, v_conv_ln_b, v_conv_w_out, v_conv_b_out, v_sgu_w_in, v_sgu_b_in, v_sgu_ln_g, v_sgu_ln_b, v_sgu_ws, v_sgu_bs, v_sgu_w_out, v_sgu_b_out, v_ffn_w_in, v_ffn_dw, v_ffn_dw_b, v_ffn_w_out, v_final_g):
    W = dict(norm1_g=norm1_g, norm2_g=norm2_g, ada_w=ada_w, ada_b=ada_b, attn_wqkv=attn_wqkv, attn_bqkv=attn_bqkv, attn_sinks=attn_sinks, attn_wo=attn_wo, attn_bo=attn_bo, conv_w_in=conv_w_in, conv_b_in=conv_b_in, conv_dw=conv_dw, conv_dw_b=conv_dw_b, conv_ln_g=conv_ln_g, conv_ln_b=conv_ln_b, conv_w_out=conv_w_out, conv_b_out=conv_b_out, sgu_w_in=sgu_w_in, sgu_b_in=sgu_b_in, sgu_ln_g=sgu_ln_g, sgu_ln_b=sgu_ln_b, sgu_ws=sgu_ws, sgu_bs=sgu_bs, sgu_w_out=sgu_w_out, sgu_b_out=sgu_b_out, ffn_w_in=ffn_w_in, ffn_dw=ffn_dw, ffn_dw_b=ffn_dw_b, ffn_w_out=ffn_w_out, final_g=final_g)
    MOM = dict(norm1_g=m_norm1_g, norm2_g=m_norm2_g, ada_w=m_ada_w, ada_b=m_ada_b, attn_wqkv=m_attn_wqkv, attn_bqkv=m_attn_bqkv, attn_sinks=m_attn_sinks, attn_wo=m_attn_wo, attn_bo=m_attn_bo, conv_w_in=m_conv_w_in, conv_b_in=m_conv_b_in, conv_dw=m_conv_dw, conv_dw_b=m_conv_dw_b, conv_ln_g=m_conv_ln_g, conv_ln_b=m_conv_ln_b, conv_w_out=m_conv_w_out, conv_b_out=m_conv_b_out, sgu_w_in=m_sgu_w_in, sgu_b_in=m_sgu_b_in, sgu_ln_g=m_sgu_ln_g, sgu_ln_b=m_sgu_ln_b, sgu_ws=m_sgu_ws, sgu_bs=m_sgu_bs, sgu_w_out=m_sgu_w_out, sgu_b_out=m_sgu_b_out, ffn_w_in=m_ffn_w_in, ffn_dw=m_ffn_dw, ffn_dw_b=m_ffn_dw_b, ffn_w_out=m_ffn_w_out, final_g=m_final_g)
    VAR = dict(norm1_g=v_norm1_g, norm2_g=v_norm2_g, ada_w=v_ada_w, ada_b=v_ada_b, attn_wqkv=v_attn_wqkv, attn_bqkv=v_attn_bqkv, attn_sinks=v_attn_sinks, attn_wo=v_attn_wo, attn_bo=v_attn_bo, conv_w_in=v_conv_w_in, conv_b_in=v_conv_b_in, conv_dw=v_conv_dw, conv_dw_b=v_conv_dw_b, conv_ln_g=v_conv_ln_g, conv_ln_b=v_conv_ln_b, conv_w_out=v_conv_w_out, conv_b_out=v_conv_b_out, sgu_w_in=v_sgu_w_in, sgu_b_in=v_sgu_b_in, sgu_ln_g=v_sgu_ln_g, sgu_ln_b=v_sgu_ln_b, sgu_ws=v_sgu_ws, sgu_bs=v_sgu_bs, sgu_w_out=v_sgu_w_out, sgu_b_out=v_sgu_b_out, ffn_w_in=v_ffn_w_in, ffn_dw=v_ffn_dw, ffn_dw_b=v_ffn_dw_b, ffn_w_out=v_ffn_w_out, final_g=v_final_g)
    ORDER = list(W)

    _, T, D = x.shape
    depth = norm1_g.shape[0]
    n_heads = D // HEAD_DIM
    me = 4 * lax.axis_index("x") + 2 * lax.axis_index("y") + lax.axis_index("c")
    x0 = x.reshape(T, D)
    tgt = loss_target.reshape(T, D)

    small_sharded = ["attn_bqkv", "attn_bo", "conv_dw", "sgu_b_in", "sgu_ln_g", "sgu_ln_b", "sgu_b_out", "ffn_dw"]
    s_in = [c] + [W[n] for n in small_sharded]
    s_shapes = [a.shape for a in s_in]
    gathered_small = _all_gather(_pack(s_in, 128), "gather_small").reshape(N_DEV, -1)
    per_dev = [_unpack(gathered_small[d], s_shapes) for d in range(N_DEV)]

    def full(idx):
        return jnp.concatenate([per_dev[d][idx] for d in range(N_DEV)], axis=-1)

    c_all = jnp.concatenate([per_dev[d][0] for d in range(N_DEV)], axis=0)
    F_small = {n: full(1 + i) for i, n in enumerate(small_sharded)}

    c_act = c_all * jax.nn.sigmoid(c_all)
    c_pad = jnp.pad(c_act, ((0, 8), (0, 0))).astype(BF16)
    n_ada = ada_w.shape[-1]
    ada_cols = lax.dynamic_slice_in_dim(ada_b, me * n_ada, n_ada, axis=1)
    mod_loc = jnp.stack([_mm(c_pad, ada_w[i].astype(BF16), "nn", "ada_mod", bias=ada_cols[i:i + 1])
                         for i in range(depth)])
    mod_all = _all_gather(mod_loc, "gather_mod")
    mod_mine = lax.dynamic_index_in_dim(mod_all, me, axis=2, keepdims=False)
    mod = jnp.transpose(mod_mine, (1, 0, 2)).reshape(depth, 6, 1, D)

    mixer_names = {0: ("attn_wqkv", "attn_wo"), 1: ("conv_w_in", "conv_w_out"), 2: ("sgu_w_in", "sgu_w_out")}

    def group_pieces(key):
        part, i = key
        w_in, w_out = ("ffn_w_in", "ffn_w_out") if part == "ffn" else mixer_names[i % 3]
        l = i if part == "ffn" else i // 3
        return [(w_in, l, W[w_in].shape[-1]), (w_out, l, W[w_out].shape[1])]

    def local_weights(key):
        (w_in, l, _), (w_out, _, _) = group_pieces(key)
        return [W[w_in][l].T.astype(BF16), W[w_out][l].astype(BF16)]

    WF = {}

    def land_weights(key, gathered):
        for (n, l, r), g in zip(group_pieces(key), gathered):
            WF[(n, l)] = g.reshape(N_DEV * r, D)

    def gather_side(key):
        return ("gather", local_weights(key))

    def vec(a):
        return a.reshape(1, -1)

    def kind_of(i):
        return i % 3

    land_weights(("mix", 0), _exchange("gather", local_weights(("mix", 0)), "gather_first"))

    saved = []
    xs, y_prev, gate_prev = x0, None, None
    for i in range(depth):
        sh1, sc1, g1, sh2, sc2, g2 = [mod[i, k] for k in range(6)]
        kind, j = i % 3, i // 3
        st = dict(kind=kind, j=j)
        if y_prev is None:
            h1 = _norm_mod_fwd(xs, vec(norm1_g[i]), sc1, sh1, "norm1_fwd")
        else:
            xs, h1 = _norm_mod_fwd(xs, vec(norm1_g[i]), sc1, sh1, "norm1_fwd", y=y_prev, gate=gate_prev)
        st.update(x_in=xs, h1=h1)
        if kind == 0:
            qkv = _mm(h1, WF[("attn_wqkv", j)], "nt", "attn_qkv", out_dtype=BF16, bias=vec(F_small["attn_bqkv"][j]))
            o, landed = _attn_fwd(qkv, vec(attn_sinks[j]), n_heads, "attn_fwd", side=gather_side(("ffn", i)))
            land_weights(("ffn", i), landed)
            y1 = _mm(o, WF[("attn_wo", j)], "nn", "attn_out", bias=vec(F_small["attn_bo"][j]))
            st.update(qkv=qkv, o=o)
        elif kind == 1:
            p = _mm(h1, WF[("conv_w_in", j)], "nt", "conv_in", out_dtype=BF16, bias=vec(conv_b_in[j]))
            zc, s, landed = _convmix_fwd(p, F_small["conv_dw"][j], vec(conv_dw_b[j]), vec(conv_ln_g[j]),
                                         vec(conv_ln_b[j]), "convmix_fwd", side=gather_side(("ffn", i)))
            land_weights(("ffn", i), landed)
            y1 = _mm(s, WF[("conv_w_out", j)], "nn", "conv_out", bias=vec(conv_b_out[j]))
            st.update(p=p, zc=zc, s=s)
        else:
            p = _mm(h1, WF[("sgu_w_in", j)], "nt", "sgu_in", out_dtype=BF16, bias=vec(F_small["sgu_b_in"][j]))
            mm_ = _sgu_fwd(p, vec(F_small["sgu_ln_g"][j]), vec(F_small["sgu_ln_b"][j]), sgu_ws[j], sgu_bs[j].T, "sgu_fwd")
            y1 = _mm(mm_, WF[("sgu_w_out", j)], "nn", "sgu_out", bias=vec(F_small["sgu_b_out"][j]))
            st.update(p=p, m=mm_)
        xs, h2 = _norm_mod_fwd(xs, vec(norm2_g[i]), sc2, sh2, "norm2_fwd", y=y1, gate=g1)
        nxt = i + 1
        if nxt < depth:
            z, landed = _mm(h2, WF[("ffn_w_in", i)], "nt", "ffn_in", out_dtype=BF16, side=gather_side(("mix", nxt)))
            land_weights(("mix", nxt), landed)
        else:
            z = _mm(h2, WF[("ffn_w_in", i)], "nt", "ffn_in", out_dtype=BF16)
        if nxt < depth and kind_of(nxt) == 2:
            a, landed = _ffn_mid_fwd(z, F_small["ffn_dw"][i], vec(ffn_dw_b[i]), "ffn_mid_fwd", side=gather_side(("ffn", nxt)))
            land_weights(("ffn", nxt), landed)
        else:
            a = _ffn_mid_fwd(z, F_small["ffn_dw"][i], vec(ffn_dw_b[i]), "ffn_mid_fwd")
        y2 = _mm(a, WF[("ffn_w_out", i)], "nn", "ffn_out")
        st.update(y1=y1, x_mid=xs, h2=h2, z=z, a=a, y2=y2)
        saved.append(st)
        y_prev, gate_prev = y2, g2

    dx, d_final_g, loss_row = _final_loss(xs, y_prev, gate_prev, tgt, vec(final_g), "final_loss")
    loss = lax.psum(loss_row[0, 0], ("x", "y", "c"))

    G = {n: [None] * W[n].shape[0] for n in ORDER if n != "final_g"}
    GW = {}
    recv = {}
    dmod = [None] * depth

    def scatter_side(key):
        return ("scatter", [GW[(n, l)].reshape(N_DEV, r, D) for n, l, r in group_pieces(key)])

    for i in reversed(range(depth)):
        st = saved[i]
        sh1, sc1, g1, sh2, sc2, g2 = [mod[i, k] for k in range(6)]
        kind, j = st["kind"], st["j"]
        dy2, dg2, _ = _gate_bwd(dx, st["y2"], g2, "gate_bwd")
        GW[("ffn_w_out", i)] = _mm(st["a"], dy2, "tn", "ffn_out_dw", out_dtype=BF16)
        da = _mm(dy2, WF[("ffn_w_out", i)], "nt", "ffn_out_dx", out_dtype=BF16)
        if i + 1 < depth:
            dz, ddw, ddwb, recv[("mix", i + 1)] = _ffn_mid_bwd(st["z"], da, F_small["ffn_dw"][i], vec(ffn_dw_b[i]),
                                                               "ffn_mid_bwd", side=scatter_side(("mix", i + 1)))
        else:
            dz, ddw, ddwb = _ffn_mid_bwd(st["z"], da, F_small["ffn_dw"][i], vec(ffn_dw_b[i]), "ffn_mid_bwd")
        G["ffn_dw"][i], G["ffn_dw_b"][i] = ddw, ddwb[0]
        GW[("ffn_w_in", i)] = _mm(dz, st["h2"], "tn", "ffn_in_dw", out_dtype=BF16)
        dh2 = _mm(dz, WF[("ffn_w_in", i)], "nn", "ffn_in_dx")
        dx, dn2, dsc2, dsh2 = _norm_mod_bwd(dh2, st["x_mid"], vec(norm2_g[i]), sc2, sh2, dx, "norm_bwd")
        G["norm2_g"][i] = dn2[0]
        dy1, dg1, dbo = _gate_bwd(dx, st["y1"], g1, "gate_bwd")
        ffn_side = scatter_side(("ffn", i))
        if kind == 0:
            G["attn_bo"][j] = dbo[0]
            GW[("attn_wo", j)] = _mm(st["o"], dy1, "tn", "attn_out_dw", out_dtype=BF16)
            do = _mm(dy1, WF[("attn_wo", j)], "nt", "attn_out_dx", out_dtype=BF16)
            dq, dk, dv, dbq, dbk, dbv, dsk, recv[("ffn", i)] = _attn_bwd(st["qkv"], do, vec(attn_sinks[j]), n_heads,
                                                                         "attn_bwd", side=ffn_side)
            dqkv = jnp.concatenate([dq, dk, dv], axis=1)
            G["attn_bqkv"][j] = jnp.concatenate([dbq, dbk, dbv], axis=1)[0]
            G["attn_sinks"][j] = dsk[0]
            GW[("attn_wqkv", j)] = _mm(dqkv, st["h1"], "tn", "attn_qkv_dw", out_dtype=BF16)
            dh1 = _mm(dqkv, WF[("attn_wqkv", j)], "nn", "attn_qkv_dx")
        elif kind == 1:
            G["conv_b_out"][j] = dbo[0]
            GW[("conv_w_out", j)] = _mm(st["s"], dy1, "tn", "conv_out_dw", out_dtype=BF16)
            ds = _mm(dy1, WF[("conv_w_out", j)], "nt", "conv_out_dx")
            dp, dbin, ddw, ddwb, dlng, dlnb, recv[("ffn", i)] = _convmix_bwd(
                ds, st["zc"], st["p"], F_small["conv_dw"][j], vec(conv_ln_g[j]), vec(conv_ln_b[j]), "convmix_bwd",
                side=ffn_side)
            G["conv_b_in"][j], G["conv_dw"][j], G["conv_dw_b"][j] = dbin[0], ddw, ddwb[0]
            G["conv_ln_g"][j], G["conv_ln_b"][j] = dlng[0], dlnb[0]
            GW[("conv_w_in", j)] = _mm(dp, st["h1"], "tn", "conv_in_dw", out_dtype=BF16)
            dh1 = _mm(dp, WF[("conv_w_in", j)], "nn", "conv_in_dx")
        else:
            G["sgu_b_out"][j] = dbo[0]
            GW[("sgu_w_out", j)] = _mm(st["m"], dy1, "tn", "sgu_out_dw", out_dtype=BF16)
            dm = _mm(dy1, WF[("sgu_w_out", j)], "nt", "sgu_out_dx")
            dp, dbin, dlng, dlnb, dws, dbst, recv[("ffn", i)] = _sgu_bwd(
                st["p"], dm, vec(F_small["sgu_ln_g"][j]), vec(F_small["sgu_ln_b"][j]), sgu_ws[j], sgu_bs[j].T, "sgu_bwd",
                side=ffn_side)
            G["sgu_b_in"][j], G["sgu_ln_g"][j], G["sgu_ln_b"][j] = dbin[0], dlng[0], dlnb[0]
            G["sgu_ws"][j], G["sgu_bs"][j] = dws, dbst.T
            GW[("sgu_w_in", j)] = _mm(dp, st["h1"], "tn", "sgu_in_dw", out_dtype=BF16)
            dh1 = _mm(dp, WF[("sgu_w_in", j)], "nn", "sgu_in_dx")
        dx, dn1, dsc1, dsh1 = _norm_mod_bwd(dh1, st["x_in"], vec(norm1_g[i]), sc1, sh1, dx, "norm_bwd")
        G["norm1_g"][i] = dn1[0]
        dmod[i] = jnp.concatenate([dsh1, dsc1, dg1, dsh2, dsc2, dg2], axis=1)[0]
    grad_x = dx.reshape(x.shape)
    recv[("mix", 0)] = _exchange(*scatter_side(("mix", 0)), "exchange_last")

    small_names = [n for n in ORDER if n not in
                   ("ada_w", "ada_b", "attn_wqkv", "attn_wo", "conv_w_in", "conv_w_out", "sgu_w_in", "sgu_w_out",
                    "ffn_w_in", "ffn_w_out", "final_g")]
    g_small_full = [jnp.stack(G[n]) for n in small_names] + [d_final_g[0]]
    small_names = small_names + ["final_g"]
    dmod_arr = jnp.stack(dmod)
    g_small_in = g_small_full + [dmod_arr]
    g_shapes = [a.shape for a in g_small_in]
    packed_gs = _pack(g_small_in, 128)
    gathered_gs = _all_gather(packed_gs, "gather_small_grads")
    summed_gs = _sum8(gathered_gs, "sum_small_grads").reshape(-1)
    gsum = dict(zip(small_names + ["ada_b"], _unpack(summed_gs, g_shapes)))

    def local_shard(n, a):
        if n in small_sharded:
            w = W[n].shape[-1]
            return lax.dynamic_slice_in_dim(a, me * w, w, axis=a.ndim - 1)
        return a

    gsum = {n: local_shard(n, a) for n, a in gsum.items()}

    dmod_all = jnp.stack([_unpack(gathered_gs[d].reshape(-1), g_shapes)[-1] for d in range(N_DEV)])
    dmod_cols = lax.dynamic_slice_in_dim(dmod_all, me * n_ada, n_ada, axis=2)
    dmod_pad = jnp.pad(dmod_cols, ((0, 8), (0, 0), (0, 0))).astype(BF16)
    g_ada_w = jnp.stack([_mm(c_pad, dmod_pad[:, i, :], "tn", "ada_dw") for i in range(depth)])

    big = {}
    for i in range(depth):
        for part in ("mix", "ffn"):
            for k, ((n, l, r), got) in enumerate(zip(group_pieces((part, i)), recv[(part, i)])):
                gp = _sum8(got, "sum_weight_grads")
                big.setdefault(n, {})[l] = gp.T if k == 0 else gp
    grads = {n: jnp.stack([v[l] for l in range(len(v))]) for n, v in big.items()}
    grads["ada_w"] = g_ada_w
    grads.update(gsum)

    delta, new_m, new_v = {}, {}, {}
    big_names = ["ada_w", "attn_wqkv", "attn_wo", "conv_w_in", "conv_w_out", "sgu_w_in", "sgu_w_out", "ffn_w_in",
                 "ffn_w_out"]
    for n in big_names:
        delta[n], new_m[n], new_v[n] = _adamw_nd(W[n], grads[n], MOM[n], VAR[n], "adamw_" + n)
    rest = [n for n in ORDER if n not in big_names]
    rest_shapes = [W[n].shape for n in rest]
    pk = lambda d: _pack([d[n] for n in rest], 128)
    d_s, m_s, v_s = _adamw(pk(W), pk(grads), pk(MOM), pk(VAR), "adamw_small")
    for n, a, b_, c_ in zip(rest, _unpack(d_s.reshape(-1), rest_shapes), _unpack(m_s.reshape(-1), rest_shapes),
                            _unpack(v_s.reshape(-1), rest_shapes)):
        delta[n], new_m[n], new_v[n] = a, b_, c_

    return (loss, grad_x, *[grads[n] for n in ORDER], *[delta[n] for n in ORDER],
            *[new_m[n] for n in ORDER], *[new_v[n] for n in ORDER])
```

```python
import functools
import math

import jax
import jax.numpy as jnp
from jax import lax
from jax.experimental import pallas as pl
from jax.experimental.pallas import tpu as pltpu

F32 = jnp.float32
BF16 = jnp.bfloat16

N_DEV = 8
HEAD_DIM = 64
N_KV = 4
BLOCK = 128
NORM_EPS = 1e-6
NEG_INF = -1e30
ADAM_LR = 0.001
ADAM_B1 = 0.9
ADAM_B2 = 0.999
ADAM_EPS = 1e-08
ADAM_WD = 0.01
ADAM_STEP = 10
V7X_VMEM_LIMIT = 56 * 1024 * 1024
MESH = pl.DeviceIdType.MESH


def _cparams():
    return pltpu.CompilerParams(vmem_limit_bytes=V7X_VMEM_LIMIT)


def _tile(n, cap, align):
    best = None
    for t in range(align, min(n, cap) + 1, align):
        if n % t == 0:
            best = t
    return best if best is not None else n


def _dot(a, b, ca, cb):
    return lax.dot_general(a, b, (((ca,), (cb,)), ((), ())), preferred_element_type=F32)


def _sigmoid(x):
    return 1.0 / (1.0 + jnp.exp(-x))


def _rows(x, start, n):
    return lax.slice_in_dim(x, start, start + n, axis=0)


def _my_coords():
    return lax.axis_index("x"), lax.axis_index("y"), lax.axis_index("c")


def _peer(m, mx, my, mc):
    px = (mx + ((m >> 2) & 1)) % 2
    py = (my + ((m >> 1) & 1)) % 2
    pc = (mc + (m & 1)) % 2
    return px, py, pc


def _exchange_copies(kind, x_ref, o_ref, send_sems, recv_sems, local_sem):
    mx, my, mc = _my_coords()
    me = 4 * mx + 2 * my + mc
    local = pltpu.make_async_copy(x_ref if kind == "gather" else x_ref.at[me], o_ref.at[me], local_sem)
    remote = []
    for m in range(1, N_DEV):
        px, py, pc = _peer(m, mx, my, mc)
        src = x_ref if kind == "gather" else x_ref.at[4 * px + 2 * py + pc]
        remote.append(pltpu.make_async_remote_copy(
            src_ref=src, dst_ref=o_ref.at[me], send_sem=send_sems.at[m - 1], recv_sem=recv_sems.at[m - 1],
            device_id=(px, py, pc), device_id_type=MESH))
    return local, remote


def _exchange_start(kind, *refs):
    local, remote = _exchange_copies(kind, *refs)
    local.start()
    for cp in remote:
        cp.start()


def _exchange_wait(kind, *refs):
    local, remote = _exchange_copies(kind, *refs)
    for cp in remote:
        cp.wait_recv()
    for cp in remote:
        cp.wait_send()
    local.wait()


_EXCHANGE_SEMS = [pltpu.SemaphoreType.DMA((N_DEV - 1,)), pltpu.SemaphoreType.DMA((N_DEV - 1,)), pltpu.SemaphoreType.DMA]


def _exchange_shape(kind, x):
    return jax.ShapeDtypeStruct(((N_DEV,) + x.shape) if kind == "gather" else x.shape, x.dtype)


def _exchange(kind, xs, name):
    n = len(xs)

    def body(*refs):
        x_refs, o_refs, sems = refs[:n], refs[n:2 * n], refs[2 * n:]
        for e in range(n):
            _exchange_start(kind, x_refs[e], o_refs[e], *sems[3 * e:3 * e + 3])
        for e in range(n):
            _exchange_wait(kind, x_refs[e], o_refs[e], *sems[3 * e:3 * e + 3])

    any_spec = pl.BlockSpec(memory_space=pl.ANY)
    return tuple(pl.pallas_call(
        body, name=name, out_shape=tuple(_exchange_shape(kind, x) for x in xs),
        in_specs=[any_spec] * n, out_specs=(any_spec,) * n, scratch_shapes=list(_EXCHANGE_SEMS) * n,
    )(*xs))


def _all_gather(x, name):
    return _exchange("gather", [x], name)[0]


def _call(body, name, grid, in_specs, out_specs, out_shape, args, scratch=(), side=None):
    single = not isinstance(out_shape, (tuple, list))
    if single:
        out_shape, out_specs = (out_shape,), (out_specs,)
    if side is None:
        res = pl.pallas_call(body, name=name, grid=grid, in_specs=list(in_specs), out_specs=tuple(out_specs),
                             out_shape=tuple(out_shape), scratch_shapes=list(scratch), compiler_params=_cparams())(*args)
        return res[0] if single else tuple(res)
    kind, xs = side
    n_in, n_out, n_scr, n_x = len(in_specs), len(out_shape), len(scratch), len(xs)

    def wrapped(*refs):
        ins, x_refs = refs[:n_in], refs[n_in:n_in + n_x]
        o0 = n_in + n_x
        outs, o_refs = refs[o0:o0 + n_out], refs[o0 + n_out:o0 + n_out + n_x]
        s0 = o0 + n_out + n_x
        scr, sems = refs[s0:s0 + n_scr], refs[s0 + n_scr:]
        ids = [pl.program_id(a) for a in range(len(grid))]
        first = functools.reduce(jnp.logical_and, [i == 0 for i in ids])
        last = functools.reduce(jnp.logical_and, [i == g - 1 for i, g in zip(ids, grid)])

        @pl.when(first)
        def _():
            for e in range(n_x):
                _exchange_start(kind, x_refs[e], o_refs[e], *sems[3 * e:3 * e + 3])

        body(*ins, *outs, *scr)

        @pl.when(last)
        def _():
            for e in range(n_x):
                _exchange_wait(kind, x_refs[e], o_refs[e], *sems[3 * e:3 * e + 3])

    any_spec = pl.BlockSpec(memory_space=pl.ANY)
    res = pl.pallas_call(
        wrapped, name=name, grid=grid, in_specs=list(in_specs) + [any_spec] * n_x,
        out_specs=tuple(out_specs) + (any_spec,) * n_x,
        out_shape=tuple(out_shape) + tuple(_exchange_shape(kind, x) for x in xs),
        scratch_shapes=list(scratch) + list(_EXCHANGE_SEMS) * n_x, compiler_params=_cparams())(*args, *xs)
    return tuple(res[:n_out]) + (tuple(res[n_out:]),)


def _mm(a, b, mode, name, out_dtype=F32, bias=None, side=None, a2=None, tm_cap=1024, tn_cap=1408, tk_cap=1408):
    w2 = 0 if a2 is None else a2.shape[1]
    assert a2 is None or (mode in ("nn", "tn") and a2.shape[0] == a.shape[0])
    if mode == "nn":
        (M, K), (K2, N) = a.shape, b.shape
        K += w2
    elif mode == "nt":
        (M, K), (N, K2) = a.shape, b.shape
    else:
        (K, M), (K2, N) = a.shape, b.shape
        M += w2
    assert K == K2, (a.shape, b.shape, mode)
    if mode == "tn":
        tm_cap, tk_cap = 256, 8192
    tm = _tile(math.gcd(M, w2) if mode == "tn" else M, tm_cap, 128 if mode == "tn" else 16)
    tn = _tile(N, tn_cap, 128)
    tk = _tile(math.gcd(K, w2) if mode == "nn" else K, tk_cap, 128)
    nk = K // tk
    n1 = a.shape[1] // (tm if mode == "tn" else tk)
    ca, cb = {"nn": (1, 0), "nt": (1, 1), "tn": (0, 0)}[mode]
    has_bias = bias is not None

    def body(*refs):
        refs = list(refs)
        a_ref = refs.pop(0)
        a2_ref = refs.pop(0) if a2 is not None else None
        b_ref = refs.pop(0)
        bias_ref = refs.pop(0) if has_bias else None
        o_ref, acc_ref = refs
        k = pl.program_id(2)

        def step(lhs_ref):
            part = _dot(lhs_ref[...].astype(BF16), b_ref[...].astype(BF16), ca, cb)

            @pl.when(k == 0)
            def _():
                acc_ref[...] = part

            @pl.when(k > 0)
            def _():
                acc_ref[...] += part

        if a2 is None:
            step(a_ref)
        else:
            col = pl.program_id(0) if mode == "tn" else k
            pl.when(col < n1)(lambda: step(a_ref))
            pl.when(col >= n1)(lambda: step(a2_ref))

        @pl.when(k == nk - 1)
        def _():
            r = acc_ref[...]
            if has_bias:
                r = r + bias_ref[...]
            o_ref[...] = r.astype(out_dtype)

    if mode == "tn":
        a_specs = [pl.BlockSpec((tk, tm), lambda i, j, k: (k, jnp.minimum(i, n1 - 1)))]
        if a2 is not None:
            a_specs.append(pl.BlockSpec((tk, tm), lambda i, j, k: (k, jnp.maximum(i - n1, 0))))
    else:
        a_specs = [pl.BlockSpec((tm, tk), lambda i, j, k: (i, jnp.minimum(k, n1 - 1)))]
        if a2 is not None:
            a_specs.append(pl.BlockSpec((tm, tk), lambda i, j, k: (i, jnp.maximum(k - n1, 0))))
    if mode == "nt":
        b_spec = pl.BlockSpec((tn, tk), lambda i, j, k: (j, k))
    else:
        b_spec = pl.BlockSpec((tk, tn), lambda i, j, k: (k, j))
    in_specs = a_specs + [b_spec]
    args = [a] + ([a2] if a2 is not None else []) + [b]
    if has_bias:
        in_specs.append(pl.BlockSpec((1, tn), lambda i, j, k: (0, j)))
        args.append(bias)
    return _call(body, name, (M // tm, N // tn, nk), in_specs, pl.BlockSpec((tm, tn), lambda i, j, k: (i, j)),
                 jax.ShapeDtypeStruct((M, N), out_dtype), args, scratch=[pltpu.VMEM((tm, tn), F32)], side=side)


def _vec_spec(n):
    return pl.BlockSpec((1, n), lambda t: (0, 0))


def _norm_mod_fwd(x, g, sc, sh, name, y=None, gate=None):
    T, D = x.shape
    tr = _tile(T, 256, 16)
    has_y = y is not None

    def body(*refs):
        if has_y:
            x_ref, y_ref, gate_ref, g_ref, sc_ref, sh_ref, xn_ref, h_ref = refs
            xv = x_ref[...] + gate_ref[...] * y_ref[...]
            xn_ref[...] = xv
        else:
            x_ref, g_ref, sc_ref, sh_ref, h_ref = refs
            xv = x_ref[...]
        r = lax.rsqrt(jnp.mean(xv * xv, axis=-1, keepdims=True) + NORM_EPS)
        h = (xv * r * g_ref[...]) * (1.0 + sc_ref[...]) + sh_ref[...]
        h_ref[...] = h.astype(BF16)

    row = pl.BlockSpec((tr, D), lambda t: (t, 0))
    vec = _vec_spec(D)
    if has_y:
        in_specs, args = [row, row, vec, vec, vec, vec], [x, y, gate, g, sc, sh]
        out_shape = (jax.ShapeDtypeStruct((T, D), F32), jax.ShapeDtypeStruct((T, D), BF16))
        out_specs = (row, row)
    else:
        in_specs, args = [row, vec, vec, vec], [x, g, sc, sh]
        out_shape = jax.ShapeDtypeStruct((T, D), BF16)
        out_specs = row
    return _call(body, name, (T // tr,), in_specs, out_specs, out_shape, args)


def _gate_bwd(dxn, y, gate, name):
    T, D = dxn.shape
    tr = _tile(T, 256, 16)

    def body(dxn_ref, y_ref, gate_ref, dy_ref, dgate_ref, dbias_ref):
        @pl.when(pl.program_id(0) == 0)
        def _():
            dgate_ref[...] = jnp.zeros_like(dgate_ref)
            dbias_ref[...] = jnp.zeros_like(dbias_ref)
        d = dxn_ref[...]
        dy = d * gate_ref[...]
        dy_ref[...] = dy.astype(BF16)
        dgate_ref[...] += jnp.sum(d * y_ref[...], axis=0, keepdims=True)
        dbias_ref[...] += jnp.sum(dy, axis=0, keepdims=True)

    row = pl.BlockSpec((tr, D), lambda t: (t, 0))
    vec = _vec_spec(D)
    return _call(body, name, (T // tr,), [row, row, vec], (row, vec, vec),
                 (jax.ShapeDtypeStruct((T, D), BF16), jax.ShapeDtypeStruct((1, D), F32),
                  jax.ShapeDtypeStruct((1, D), F32)), (dxn, y, gate))


def _norm_mod_bwd(dh, x, g, sc, sh, dxn, name):
    T, D = x.shape
    tr = _tile(T, 256, 16)

    def body(dh_ref, x_ref, g_ref, sc_ref, sh_ref, dxn_ref, dx_ref, dg_ref, dsc_ref, dsh_ref):
        @pl.when(pl.program_id(0) == 0)
        def _():
            dg_ref[...] = jnp.zeros_like(dg_ref)
            dsc_ref[...] = jnp.zeros_like(dsc_ref)
            dsh_ref[...] = jnp.zeros_like(dsh_ref)
        dh = dh_ref[...].astype(F32)
        xv = x_ref[...]
        gv = g_ref[...]
        r = lax.rsqrt(jnp.mean(xv * xv, axis=-1, keepdims=True) + NORM_EPS)
        yv = xv * r
        dsh_ref[...] += jnp.sum(dh, axis=0, keepdims=True)
        dsc_ref[...] += jnp.sum(dh * (yv * gv), axis=0, keepdims=True)
        dn = dh * (1.0 + sc_ref[...])
        dg_ref[...] += jnp.sum(dn * yv, axis=0, keepdims=True)
        dy = dn * gv
        dx = r * (dy - yv * jnp.mean(dy * yv, axis=-1, keepdims=True))
        dx_ref[...] = dxn_ref[...] + dx

    row = pl.BlockSpec((tr, D), lambda t: (t, 0))
    vec = _vec_spec(D)
    vshape = jax.ShapeDtypeStruct((1, D), F32)
    return _call(body, name, (T // tr,), [row, row, vec, vec, vec, row], (row, vec, vec, vec),
                 (jax.ShapeDtypeStruct((T, D), F32), vshape, vshape, vshape), (dh, x, g, sc, sh, dxn))


def _final_loss(x, y, gate, tgt, g, name):
    T, D = x.shape
    tr = _tile(T, 256, 16)

    def body(x_ref, y_ref, gate_ref, tgt_ref, g_ref, dx_ref, dg_ref, loss_ref):
        @pl.when(pl.program_id(0) == 0)
        def _():
            dg_ref[...] = jnp.zeros_like(dg_ref)
            loss_ref[...] = jnp.zeros_like(loss_ref)
        xv = x_ref[...] + gate_ref[...] * y_ref[...]
        gv = g_ref[...]
        r = lax.rsqrt(jnp.mean(xv * xv, axis=-1, keepdims=True) + NORM_EPS)
        yv = xv * r
        e = yv * gv - tgt_ref[...]
        per_row = jnp.mean(e * e, axis=-1, keepdims=True)
        loss_ref[...] += 0.5 * jnp.sum(per_row, axis=0, keepdims=True)
        dout = e * (1.0 / D)
        dg_ref[...] += jnp.sum(dout * yv, axis=0, keepdims=True)
        dy = dout * gv
        dx_ref[...] = r * (dy - yv * jnp.mean(dy * yv, axis=-1, keepdims=True))

    row = pl.BlockSpec((tr, D), lambda t: (t, 0))
    vec = _vec_spec(D)
    return _call(body, name, (T // tr,), [row, row, vec, row, vec], (row, vec, _vec_spec(128)),
                 (jax.ShapeDtypeStruct((T, D), F32), jax.ShapeDtypeStruct((1, D), F32),
                  jax.ShapeDtypeStruct((1, 128), F32)), (x, y, gate, tgt, g))


FFN_HALO = 16
FFN_PAD = 8
FFN_CHUNK = 64


def _lane_groups(cw):
    lw = 128 if cw % 128 == 0 else cw
    return lw, [pl.ds(g * lw, lw) for g in range(cw // lw)]


def _ffn_mid_fwd(z, dw, b, name, side=None):
    T, F2 = z.shape
    F = F2 // 2
    K = dw.shape[0]
    H, P = FFN_HALO, FFN_PAD
    tr = _tile(T, 512, H)
    cw = _tile(F, 512, 128)
    nc = F // cw
    rc = _tile(tr, FFN_CHUNK, 16)
    lw, groups = _lane_groups(cw)

    def body(gc_ref, gh_ref, uc_ref, uh_ref, wg_ref, wu_ref, bg_ref, bu_ref, a_ref, xg_s, xu_s):
        first = pl.program_id(1) == 0

        def fill(s, h_ref, c_ref):
            for g, ls in enumerate(groups):
                s[g, 0:P, :] = jnp.where(first, 0.0, h_ref[:, ls].astype(F32)[H - P:H])
                s[g, P:P + tr, :] = c_ref[:, ls].astype(F32)

        fill(xg_s, gh_ref, gc_ref)
        fill(xu_s, uh_ref, uc_ref)

        def chunk(ci, carry):
            r0 = pl.multiple_of(ci * rc, rc)
            for g, ls in enumerate(groups):
                def conv(s, w_ref, b_ref):
                    acc = b_ref[:, ls] + w_ref[K - 1:K, ls] * s[g, pl.ds(r0 + P, rc), :]
                    for k in range(K - 1):
                        acc = acc + w_ref[k:k + 1, ls] * s[g, pl.ds(r0 + P - (K - 1 - k), rc), :]
                    return acc

                gv = conv(xg_s, wg_ref, bg_ref)
                uv = conv(xu_s, wu_ref, bu_ref)
                a_ref[pl.ds(r0, rc), ls] = (gv * _sigmoid(gv) * uv).astype(BF16)
            return carry

        lax.fori_loop(0, tr // rc, chunk, 0)

    rb = tr // H
    in_specs = [
        pl.BlockSpec((tr, cw), lambda j, t: (t, j)),
        pl.BlockSpec((H, cw), lambda j, t: (jnp.maximum(t * rb - 1, 0), j)),
        pl.BlockSpec((tr, cw), lambda j, t: (t, j + nc)),
        pl.BlockSpec((H, cw), lambda j, t: (jnp.maximum(t * rb - 1, 0), j + nc)),
        pl.BlockSpec((K, cw), lambda j, t: (0, j)),
        pl.BlockSpec((K, cw), lambda j, t: (0, j + nc)),
        pl.BlockSpec((1, cw), lambda j, t: (0, j)),
        pl.BlockSpec((1, cw), lambda j, t: (0, j + nc)),
    ]
    return _call(body, name, (nc, T // tr), in_specs, pl.BlockSpec((tr, cw), lambda j, t: (t, j)),
                 jax.ShapeDtypeStruct((T, F), BF16), (z, z, z, z, dw, dw, b, b),
                 scratch=[pltpu.VMEM((len(groups), P + tr, lw), F32)] * 2, side=side)


def _ffn_mid_bwd(z, da, dw, b, name, side=None):
    T, F2 = z.shape
    F = F2 // 2
    K = dw.shape[0]
    H, P = FFN_HALO, FFN_PAD
    tr = _tile(T, 512, H)
    cw = _tile(F, 512, 128)
    nc = F // cw
    nt = T // tr
    rb = tr // H
    rc = _tile(tr, FFN_CHUNK, 16)
    ext = rc + P
    lw, groups = _lane_groups(cw)

    def body(gp_ref, gc_ref, gn_ref, up_ref, uc_ref, un_ref, dac_ref, dan_ref, wg_ref, wu_ref, bg_ref, bu_ref,
             dzg_ref, dzu_ref, dwg_ref, dwu_ref, dbg_ref, dbu_ref, xg_s, xu_s, da_s, dg_s, du_s, acc_s):
        t = pl.program_id(1)
        first = t == 0
        last = t == nt - 1

        @pl.when(first)
        def _():
            for r in (dwg_ref, dwu_ref, dbg_ref, dbu_ref):
                r[...] = jnp.zeros_like(r)

        def fill(s, p_ref, c_ref, n_ref):
            for g, ls in enumerate(groups):
                s[g, 0:P, :] = jnp.where(first, 0.0, p_ref[:, ls].astype(F32)[H - P:H])
                s[g, P:P + tr, :] = c_ref[:, ls].astype(F32)
                s[g, P + tr:2 * P + tr, :] = jnp.where(last, 0.0, n_ref[:, ls].astype(F32)[0:P])

        fill(xg_s, gp_ref, gc_ref, gn_ref)
        fill(xu_s, up_ref, uc_ref, un_ref)
        for g, ls in enumerate(groups):
            da_s[g, 0:tr, :] = dac_ref[:, ls].astype(F32)
            da_s[g, tr:tr + P, :] = jnp.where(last, 0.0, dan_ref[:, ls].astype(F32)[0:P])
        acc_s[...] = jnp.zeros_like(acc_s)

        def fold(v):
            return jnp.sum(v.reshape(rc // 8, 8, lw), axis=0)

        def chunk(ci, carry):
            r0 = pl.multiple_of(ci * rc, rc)
            for g, ls in enumerate(groups):
                def conv_ext(s, w_ref, b_ref):
                    acc = b_ref[:, ls] + w_ref[K - 1:K, ls] * s[g, pl.ds(r0 + P, ext), :]
                    for k in range(K - 1):
                        acc = acc + w_ref[k:k + 1, ls] * s[g, pl.ds(r0 + P - (K - 1 - k), ext), :]
                    return acc

                gv = conv_ext(xg_s, wg_ref, bg_ref)
                uv = conv_ext(xu_s, wu_ref, bu_ref)
                dav = da_s[g, pl.ds(r0, ext), :]
                sg = _sigmoid(gv)
                dg = dav * uv * (sg * (1.0 + gv * (1.0 - sg)))
                du = dav * (gv * sg)
                dg_s[g] = dg
                du_s[g] = du
                for idx, (d_s, dval, x_s, w_ref, dz_ref) in enumerate(
                        ((dg_s, dg, xg_s, wg_ref, dzg_ref), (du_s, du, xu_s, wu_ref, dzu_ref))):
                    cur = dval[:rc]
                    acc_s[idx, K, g] += fold(cur)
                    for k in range(K):
                        acc_s[idx, k, g] += fold(cur * x_s[g, pl.ds(r0 + P - (K - 1 - k), rc), :])
                    dz = w_ref[K - 1:K, ls] * cur
                    for k in range(K - 1):
                        dz = dz + w_ref[k:k + 1, ls] * d_s[g, pl.ds(K - 1 - k, rc), :]
                    dz_ref[pl.ds(r0, rc), ls] = dz.astype(BF16)
            return carry

        lax.fori_loop(0, tr // rc, chunk, 0)
        for idx, (dw_ref, db_ref) in enumerate(((dwg_ref, dbg_ref), (dwu_ref, dbu_ref))):
            for g, ls in enumerate(groups):
                db_ref[:, ls] += jnp.sum(acc_s[idx, K, g], axis=0, keepdims=True)
                dw_ref[:, ls] += jnp.concatenate(
                    [jnp.sum(acc_s[idx, k, g], axis=0, keepdims=True) for k in range(K)], axis=0)

    def prev(t):
        return jnp.maximum(t * rb - 1, 0)

    def nxt(t):
        return jnp.minimum((t + 1) * rb, T // H - 1)

    def zspecs(off):
        return [pl.BlockSpec((H, cw), lambda j, t: (prev(t), j + off)),
                pl.BlockSpec((tr, cw), lambda j, t: (t, j + off)),
                pl.BlockSpec((H, cw), lambda j, t: (nxt(t), j + off))]

    in_specs = zspecs(0) + zspecs(nc) + [
        pl.BlockSpec((tr, cw), lambda j, t: (t, j)),
        pl.BlockSpec((H, cw), lambda j, t: (nxt(t), j)),
        pl.BlockSpec((K, cw), lambda j, t: (0, j)),
        pl.BlockSpec((K, cw), lambda j, t: (0, j + nc)),
        pl.BlockSpec((1, cw), lambda j, t: (0, j)),
        pl.BlockSpec((1, cw), lambda j, t: (0, j + nc)),
    ]
    out_shape = (jax.ShapeDtypeStruct((T, F), BF16), jax.ShapeDtypeStruct((T, F), BF16),
                 jax.ShapeDtypeStruct((K, F), F32), jax.ShapeDtypeStruct((K, F), F32),
                 jax.ShapeDtypeStruct((1, F), F32), jax.ShapeDtypeStruct((1, F), F32))
    out_specs = (pl.BlockSpec((tr, cw), lambda j, t: (t, j)), pl.BlockSpec((tr, cw), lambda j, t: (t, j)),
                 pl.BlockSpec((K, cw), lambda j, t: (0, j)), pl.BlockSpec((K, cw), lambda j, t: (0, j)),
                 pl.BlockSpec((1, cw), lambda j, t: (0, j)), pl.BlockSpec((1, cw), lambda j, t: (0, j)))
    ng = len(groups)
    scratch = [pltpu.VMEM((ng, 2 * P + tr, lw), F32), pltpu.VMEM((ng, 2 * P + tr, lw), F32),
               pltpu.VMEM((ng, P + tr, lw), F32), pltpu.VMEM((ng, ext, lw), F32), pltpu.VMEM((ng, ext, lw), F32),
               pltpu.VMEM((2, K + 1, ng, 8, lw), F32)]
    res = _call(body, name, (nc, nt), in_specs, out_specs, out_shape, (z, z, z, z, z, z, da, da, dw, dw, b, b),
                scratch=scratch, side=side)
    dzg, dzu, dwg, dwu, dbg, dbu = res[:6]
    return (dzg, dzu, jnp.concatenate([dwg, dwu], axis=1), jnp.concatenate([dbg, dbu], axis=1)) + tuple(res[6:])


CONV_HALO = 32


def _convmix_fwd(p, dw, dwb, lng, lnb, name, side=None):
    T, D2 = p.shape
    D = D2 // 2
    K = dw.shape[0]
    H = CONV_HALO
    tr = _tile(T, 256, H)
    rb = tr // H

    def body(pc_ref, ph_ref, w_ref, wb_ref, lng_ref, lnb_ref, zc_ref, s_ref):
        first = pl.program_id(0) == 0
        pp = jnp.concatenate([jnp.where(first, 0.0, ph_ref[...].astype(F32)), pc_ref[...].astype(F32)], axis=0)
        zg = pp[:, :D] * _sigmoid(pp[:, D:])
        w = w_ref[...]
        acc = wb_ref[...] + w[K - 1:K, :] * _rows(zg, H, tr)
        for k in range(K - 1):
            acc = acc + w[k:k + 1, :] * _rows(zg, H - (K - 1 - k), tr)
        zc_ref[...] = acc
        mu = jnp.mean(acc, axis=-1, keepdims=True)
        xc = acc - mu
        rstd = lax.rsqrt(jnp.mean(xc * xc, axis=-1, keepdims=True) + NORM_EPS)
        ln = xc * rstd * lng_ref[...] + lnb_ref[...]
        s_ref[...] = (ln * _sigmoid(ln)).astype(BF16)

    in_specs = [pl.BlockSpec((tr, D2), lambda t: (t, 0)),
                pl.BlockSpec((H, D2), lambda t: (jnp.maximum(t * rb - 1, 0), 0)),
                pl.BlockSpec((K, D), lambda t: (0, 0)), _vec_spec(D), _vec_spec(D), _vec_spec(D)]
    row = pl.BlockSpec((tr, D), lambda t: (t, 0))
    return _call(body, name, (T // tr,), in_specs, (row, row),
                 (jax.ShapeDtypeStruct((T, D), F32), jax.ShapeDtypeStruct((T, D), BF16)), (p, p, dw, dwb, lng, lnb), side=side)


def _convmix_bwd(ds, zc, p, dw, lng, lnb, name, side=None):
    T, D = zc.shape
    D2 = 2 * D
    K = dw.shape[0]
    H = CONV_HALO
    tr = _tile(T, 256, H)
    rb = tr // H
    nt = T // tr
    ext = tr + H

    def body(dsc_ref, dsn_ref, zcc_ref, zcn_ref, pp_ref, pc_ref, w_ref, lng_ref, lnb_ref,
             dp_ref, dbin_ref, ddw_ref, ddwb_ref, dlng_ref, dlnb_ref):
        t = pl.program_id(0)
        first = t == 0
        last = t == nt - 1

        @pl.when(first)
        def _():
            for r in (dbin_ref, ddw_ref, ddwb_ref, dlng_ref, dlnb_ref):
                r[...] = jnp.zeros_like(r)

        dsv = jnp.concatenate([dsc_ref[...].astype(F32), jnp.where(last, 0.0, dsn_ref[...].astype(F32))], axis=0)
        zcv = jnp.concatenate([zcc_ref[...], zcn_ref[...]], axis=0)
        lg = lng_ref[...]
        mu = jnp.mean(zcv, axis=-1, keepdims=True)
        xc = zcv - mu
        rstd = lax.rsqrt(jnp.mean(xc * xc, axis=-1, keepdims=True) + NORM_EPS)
        xh = xc * rstd
        ln = xh * lg + lnb_ref[...]
        sg = _sigmoid(ln)
        dln = dsv * (sg * (1.0 + ln * (1.0 - sg)))
        dlnc = _rows(dln, 0, tr)
        dlnb_ref[...] += jnp.sum(dlnc, axis=0, keepdims=True)
        dlng_ref[...] += jnp.sum(dlnc * _rows(xh, 0, tr), axis=0, keepdims=True)
        dxh = dln * lg
        dzc = rstd * (dxh - jnp.mean(dxh, axis=-1, keepdims=True) - xh * jnp.mean(dxh * xh, axis=-1, keepdims=True))
        dzcc = _rows(dzc, 0, tr)
        ddwb_ref[...] += jnp.sum(dzcc, axis=0, keepdims=True)

        pv = jnp.concatenate([jnp.where(first, 0.0, pp_ref[...].astype(F32)), pc_ref[...].astype(F32)], axis=0)
        av, gv = pv[:, :D], pv[:, D:]
        sgg = _sigmoid(gv)
        zg = av * sgg
        w = w_ref[...]
        parts = []
        dzg = w[K - 1:K, :] * dzcc
        for k in range(K):
            parts.append(jnp.sum(dzcc * _rows(zg, H - (K - 1 - k), tr), axis=0, keepdims=True))
            if k < K - 1:
                dzg = dzg + w[k:k + 1, :] * _rows(dzc, K - 1 - k, tr)
        ddw_ref[...] += jnp.concatenate(parts, axis=0)
        ac, sc_ = _rows(av, H, tr), _rows(sgg, H, tr)
        dpa = dzg * sc_
        dpg = dzg * ac * sc_ * (1.0 - sc_)
        dpv = jnp.concatenate([dpa, dpg], axis=1)
        dp_ref[...] = dpv.astype(BF16)
        dbin_ref[...] += jnp.sum(dpv, axis=0, keepdims=True)

    def prev(t):
        return jnp.maximum(t * rb - 1, 0)

    def nxt(t):
        return jnp.minimum((t + 1) * rb, T // H - 1)

    in_specs = [pl.BlockSpec((tr, D), lambda t: (t, 0)), pl.BlockSpec((H, D), lambda t: (nxt(t), 0)),
                pl.BlockSpec((tr, D), lambda t: (t, 0)), pl.BlockSpec((H, D), lambda t: (nxt(t), 0)),
                pl.BlockSpec((H, D2), lambda t: (prev(t), 0)), pl.BlockSpec((tr, D2), lambda t: (t, 0)),
                pl.BlockSpec((K, D), lambda t: (0, 0)), _vec_spec(D), _vec_spec(D)]
    out_shape = (jax.ShapeDtypeStruct((T, D2), BF16), jax.ShapeDtypeStruct((1, D2), F32),
                 jax.ShapeDtypeStruct((K, D), F32), jax.ShapeDtypeStruct((1, D), F32),
                 jax.ShapeDtypeStruct((1, D), F32), jax.ShapeDtypeStruct((1, D), F32))
    out_specs = (pl.BlockSpec((tr, D2), lambda t: (t, 0)), _vec_spec(D2), pl.BlockSpec((K, D), lambda t: (0, 0)),
                 _vec_spec(D), _vec_spec(D), _vec_spec(D))
    return _call(body, name, (nt,), in_specs, out_specs, out_shape, (ds, ds, zc, zc, p, p, dw, lng, lnb), side=side)


_INV_SQRT2 = 1.0 / math.sqrt(2.0)
_INV_SQRT2PI = 1.0 / math.sqrt(2.0 * math.pi)


def _gelu(x):
    return 0.5 * x * (1.0 + lax.erf(x * _INV_SQRT2))


def _gelu_grad(x):
    return 0.5 * (1.0 + lax.erf(x * _INV_SQRT2)) + x * jnp.exp(-0.5 * x * x) * _INV_SQRT2PI


def _tril_mask(n, transposed=False):
    r = lax.broadcasted_iota(jnp.int32, (n, n), 0)
    c = lax.broadcasted_iota(jnp.int32, (n, n), 1)
    return (r <= c) if transposed else (r >= c)


def _sgu_fwd(p, lng, lnb, ws, bs_t, name):
    T, S2 = p.shape
    S = S2 // 2
    G, C, _ = ws.shape
    gd = S // G
    cpt = 2 if (T // C) % 2 == 0 else 1
    tr = C * cpt

    def body(p_ref, lng_ref, lnb_ref, ws_ref, bst_ref, m_ref):
        mask = _tril_mask(C)
        lg, lb = lng_ref[...], lnb_ref[...]
        bst = bst_ref[...]
        for ci in range(cpt):
            pc = p_ref[ci * C:(ci + 1) * C, :].astype(F32)
            z = _gelu(pc)
            u, v = z[:, :S], z[:, S:]
            mu = jnp.mean(v, axis=-1, keepdims=True)
            xc = v - mu
            rstd = lax.rsqrt(jnp.mean(xc * xc, axis=-1, keepdims=True) + NORM_EPS)
            vn = (xc * rstd * lg + lb).astype(BF16)
            outs = []
            for g in range(G):
                wm = jnp.where(mask, ws_ref[g], 0.0).astype(BF16)
                vs = _dot(wm, vn[:, g * gd:(g + 1) * gd], 1, 0) + bst[:, g:g + 1]
                outs.append(u[:, g * gd:(g + 1) * gd] * vs)
            m_ref[ci * C:(ci + 1) * C, :] = jnp.concatenate(outs, axis=1).astype(BF16)

    in_specs = [pl.BlockSpec((tr, S2), lambda t: (t, 0)), _vec_spec(S), _vec_spec(S),
                pl.BlockSpec((G, C, C), lambda t: (0, 0, 0)), pl.BlockSpec((C, G), lambda t: (0, 0))]
    return _call(body, name, (T // tr,), in_specs, pl.BlockSpec((tr, S), lambda t: (t, 0)),
                 jax.ShapeDtypeStruct((T, S), BF16), (p, lng, lnb, ws, bs_t))


def _sgu_bwd(p, dm, lng, lnb, ws, bs_t, name, side=None):
    T, S2 = p.shape
    S = S2 // 2
    G, C, _ = ws.shape
    gd = S // G
    nt = T // C

    def body(p_ref, dm_ref, lng_ref, lnb_ref, ws_ref, bst_ref, dp_ref, dbin_ref, dlng_ref, dlnb_ref, dws_ref, dbst_ref):
        @pl.when(pl.program_id(0) == 0)
        def _():
            for r in (dbin_ref, dlng_ref, dlnb_ref, dws_ref, dbst_ref):
                r[...] = jnp.zeros_like(r)

        mask = _tril_mask(C)
        lg, lb = lng_ref[...], lnb_ref[...]
        bst = bst_ref[...]
        pc = p_ref[...].astype(F32)
        dmv = dm_ref[...].astype(F32)
        z = _gelu(pc)
        u, v = z[:, :S], z[:, S:]
        mu = jnp.mean(v, axis=-1, keepdims=True)
        xc = v - mu
        rstd = lax.rsqrt(jnp.mean(xc * xc, axis=-1, keepdims=True) + NORM_EPS)
        vh = xc * rstd
        vn = (vh * lg + lb).astype(BF16)
        dus, dvns, dbcols = [], [], []
        for g in range(G):
            sl = slice(g * gd, (g + 1) * gd)
            wm = jnp.where(mask, ws_ref[g], 0.0).astype(BF16)
            vs = _dot(wm, vn[:, sl], 1, 0) + bst[:, g:g + 1]
            dmg = dmv[:, sl]
            dus.append(dmg * vs)
            dvs = dmg * u[:, sl]
            dbcols.append(jnp.sum(dvs, axis=-1, keepdims=True))
            dvsb = dvs.astype(BF16)
            dws_ref[g] += jnp.where(mask, _dot(dvsb, vn[:, sl], 1, 1), 0.0)
            dvns.append(_dot(wm, dvsb, 0, 0))
        dbst_ref[...] += jnp.concatenate(dbcols, axis=1)
        dvn = jnp.concatenate(dvns, axis=1)
        dlnb_ref[...] += jnp.sum(dvn, axis=0, keepdims=True)
        dlng_ref[...] += jnp.sum(dvn * vh, axis=0, keepdims=True)
        dvh = dvn * lg
        dv = rstd * (dvh - jnp.mean(dvh, axis=-1, keepdims=True) - vh * jnp.mean(dvh * vh, axis=-1, keepdims=True))
        dz = jnp.concatenate([jnp.concatenate(dus, axis=1), dv], axis=1)
        dpv = dz * _gelu_grad(pc)
        dp_ref[...] = dpv.astype(BF16)
        dbin_ref[...] += jnp.sum(dpv, axis=0, keepdims=True)

    in_specs = [pl.BlockSpec((C, S2), lambda t: (t, 0)), pl.BlockSpec((C, S), lambda t: (t, 0)),
                _vec_spec(S), _vec_spec(S), pl.BlockSpec((G, C, C), lambda t: (0, 0, 0)),
                pl.BlockSpec((C, G), lambda t: (0, 0))]
    out_shape = (jax.ShapeDtypeStruct((T, S2), BF16), jax.ShapeDtypeStruct((1, S2), F32),
                 jax.ShapeDtypeStruct((1, S), F32), jax.ShapeDtypeStruct((1, S), F32),
                 jax.ShapeDtypeStruct((G, C, C), F32), jax.ShapeDtypeStruct((C, G), F32))
    out_specs = (pl.BlockSpec((C, S2), lambda t: (t, 0)), _vec_spec(S2), _vec_spec(S), _vec_spec(S),
                 pl.BlockSpec((G, C, C), lambda t: (0, 0, 0)), pl.BlockSpec((C, G), lambda t: (0, 0)))
    return _call(body, name, (nt,), in_specs, out_specs, out_shape, (p, dm, lng, lnb, ws, bs_t), side=side)


def _alibi_slope(h, n_heads):
    return 2.0 ** (-8.0 * (h + 1) / n_heads)


def _attn_bias(n_heads):
    B = BLOCK
    qi = jnp.arange(B)[:, None]
    kj = jnp.arange(2 * B)[None, :]
    dist = qi + B - kj
    band = (dist >= 0) & (dist < B)
    slopes = jnp.array([_alibi_slope(h, n_heads) for h in range(n_heads)], F32)
    out = []
    for valid in (band & (kj >= B), band):
        b = jnp.where(valid[None], -slopes[:, None, None] * dist.astype(F32)[None], NEG_INF)
        b = jnp.where(kj[None] == 0, 0.0, b)
        out.append(b.reshape(n_heads // 2, 2, B, 2 * B).transpose(0, 2, 1, 3).reshape(n_heads // 2, B, 4 * B))
    return jnp.stack(out)


def _sink_rows(sinks):
    return jnp.pad(sinks.astype(F32)[:, None], ((0, 0), (0, BLOCK - 1)))


def _block_diag2(x):
    z = jnp.zeros_like(x)
    return jnp.concatenate([jnp.concatenate([x, z], axis=1), jnp.concatenate([z, x], axis=1)], axis=0)


def _attn_core(q_ref, kp_ref, kc_ref, vp_ref, vc_ref, sink_ref, bias_ref, n_heads):
    B, hd = BLOCK, HEAD_DIM
    ppg = n_heads // N_KV // 2
    pairs = range(n_heads // 2)
    row0 = lax.broadcasted_iota(jnp.int32, (2 * B, N_KV * hd), 0) == 0
    k2 = jnp.concatenate([kp_ref[...], kc_ref[...]], axis=0)
    v2 = jnp.concatenate([vp_ref[...], vc_ref[...]], axis=0)
    k2 = jnp.where(row0, jnp.zeros_like(k2), k2)
    v2 = jnp.where(row0, jnp.zeros_like(v2), v2)
    scale = hd ** -0.5
    kks = [_block_diag2(k2[:, kv * hd:(kv + 1) * hd] * scale) for kv in range(N_KV)]
    vvs = [_block_diag2(v2[:, kv * hd:(kv + 1) * hd]) for kv in range(N_KV)]
    first_rows = lax.broadcasted_iota(jnp.int32, (4 * B, 4 * hd), 0) < 2 * B
    first_lanes = lax.broadcasted_iota(jnp.int32, (4 * B, 4 * hd), 1) < 2 * hd
    ones = (first_rows == first_lanes).astype(BF16)
    vxs = [jnp.concatenate([vv, ones], axis=1) for vv in vvs]
    ss = [_dot(q_ref[:, pr * 2 * hd:(pr + 1) * 2 * hd], kks[pr // ppg], 1, 1) + bias_ref[pr] for pr in pairs]
    halves = []
    for h in range(n_heads):
        s = ss[h // 2][:, (h % 2) * 2 * B:(h % 2 + 1) * 2 * B]
        halves.append(jnp.concatenate([s[:, :B] + sink_ref[h:h + 1, :], s[:, B:]], axis=1))
    mxs = [jnp.max(s, axis=-1, keepdims=True) for s in halves]
    es = [jnp.exp(s - m) for s, m in zip(halves, mxs)]
    rs = [_dot(jnp.concatenate([es[2 * pr], es[2 * pr + 1]], axis=1).astype(BF16), vxs[pr // ppg], 1, 0) for pr in pairs]
    return kks, vvs, es, rs


def _attn_fwd(qkv, sink_rows, bias, n_heads, name, side=None):
    T = qkv.shape[0]
    B = BLOCK
    hd = HEAD_DIM
    HQ = n_heads * hd
    KVW = N_KV * hd
    assert (n_heads // N_KV) % 2 == 0, "heads are processed in pairs that share a key/value head"
    nb = T // B
    kcol = HQ // KVW

    def body(q_ref, kp_ref, kc_ref, vp_ref, vc_ref, sink_ref, bias_ref, o_ref):
        _, _, _, rs = _attn_core(q_ref, kp_ref, kc_ref, vp_ref, vc_ref, sink_ref, bias_ref, n_heads)
        low = lax.broadcasted_iota(jnp.int32, (B, 2 * hd), 1) < hd
        for pr, r in enumerate(rs):
            inv = jnp.where(low, 1.0 / r[:, 2 * hd:4 * hd], 1.0 / r[:, 4 * hd:])
            o_ref[:, pr * 2 * hd:(pr + 1) * 2 * hd] = (r[:, :2 * hd] * inv).astype(BF16)

    def prev(n):
        return jnp.maximum(n - 1, 0)

    in_specs = [pl.BlockSpec((B, HQ), lambda n: (n, 0)),
                pl.BlockSpec((B, KVW), lambda n: (prev(n), kcol)), pl.BlockSpec((B, KVW), lambda n: (n, kcol)),
                pl.BlockSpec((B, KVW), lambda n: (prev(n), kcol + 1)), pl.BlockSpec((B, KVW), lambda n: (n, kcol + 1)),
                pl.BlockSpec((n_heads, B), lambda n: (0, 0)),
                pl.BlockSpec((None, n_heads // 2, B, 4 * B), lambda n: (jnp.minimum(n, 1), 0, 0, 0))]
    return _call(body, name, (nb,), in_specs, pl.BlockSpec((B, HQ), lambda n: (n, 0)),
                 jax.ShapeDtypeStruct((T, HQ), BF16), (qkv, qkv, qkv, qkv, qkv, sink_rows, bias), side=side)


def _attn_bwd(qkv, do, sink_rows, bias, n_heads, name, side=None):
    T = qkv.shape[0]
    B = BLOCK
    hd = HEAD_DIM
    HQ = n_heads * hd
    KVW = N_KV * hd
    group = n_heads // N_KV
    assert group % 2 == 0, "heads are processed in pairs that share a key/value head"
    ppg = group // 2
    nb = T // B
    kcol = HQ // KVW
    scale = hd ** -0.5

    def body(q_ref, kp_ref, kc_ref, vp_ref, vc_ref, do_ref, sink_ref, bias_ref,
             dq_ref, dk_ref, dv_ref, dbq_ref, dbk_ref, dbv_ref, dsink_ref, ck_ref, cv_ref):
        n = pl.program_id(0)
        first = n == 0

        @pl.when(first)
        def _():
            for r in (dbq_ref, dbk_ref, dbv_ref, dsink_ref, ck_ref, cv_ref):
                r[...] = jnp.zeros_like(r)

        @pl.when(n < nb)
        def _():
            pairs = range(n_heads // 2)
            heads = range(n_heads)
            lanes = [slice(pr * 2 * hd, (pr + 1) * 2 * hd) for pr in pairs]
            kks, vvs, es, rs = _attn_core(q_ref, kp_ref, kc_ref, vp_ref, vc_ref, sink_ref, bias_ref, n_heads)
            low = lax.broadcasted_iota(jnp.int32, (B, 2 * hd), 1) < hd
            slot0 = lax.broadcasted_iota(jnp.int32, (B, B), 1) == 0
            lane_head = lax.broadcasted_iota(jnp.int32, (2 * hd, 4 * hd), 0) < hd
            out_head = lax.broadcasted_iota(jnp.int32, (2 * hd, 4 * hd), 1) < 2 * hd
            ones2 = (lane_head == out_head).astype(BF16)
            invs = [1.0 / rs[h // 2][:, (2 + 2 * (h % 2)) * hd:(4 + 2 * (h % 2)) * hd] for h in heads]
            dops = [do_ref[:, lanes[pr]] for pr in pairs]
            qps = [q_ref[:, lanes[pr]] for pr in pairs]
            dos = [dops[pr].astype(F32) * rs[pr][:, :2 * hd] * jnp.where(low, invs[2 * pr], invs[2 * pr + 1])
                   for pr in pairs]
            his = [t.astype(BF16) for t in dos]
            los = [(t - hi.astype(F32)).astype(BF16) for t, hi in zip(dos, his)]
            dsums = [_dot(hi, ones2, 1, 0) + _dot(lo, ones2, 1, 0) for hi, lo in zip(his, los)]
            dps = [_dot(dops[pr], vvs[pr // ppg], 1, 1) for pr in pairs]
            wide = lambda a: jnp.concatenate([a, a], axis=1)
            p_s = [es[h] * wide(invs[h]) for h in heads]
            ds_s = [p_s[h] * (dps[h // 2][:, (h % 2) * 2 * B:(h % 2 + 1) * 2 * B]
                              - wide(dsums[h // 2][:, (h % 2) * 2 * hd:(h % 2 + 1) * 2 * hd])) for h in heads]
            for h in heads:
                dsink_ref[h:h + 1, :] += jnp.sum(ds_s[h][:, :B], axis=0, keepdims=True)

            def without_slot0(a):
                return jnp.concatenate([jnp.where(slot0, 0.0, a[:, :B]), a[:, B:]], axis=1).astype(BF16)

            ds_b = [without_slot0(a) for a in ds_s]
            p_b = [without_slot0(a) for a in p_s]
            dqs = [_dot(jnp.concatenate([ds_b[2 * pr], ds_b[2 * pr + 1]], axis=1), kks[pr // ppg], 1, 0) for pr in pairs]
            for pr in pairs:
                dq_ref[:, lanes[pr]] = dqs[pr].astype(BF16)
                dbq_ref[:, lanes[pr]] += jnp.sum(dqs[pr], axis=0, keepdims=True)
            keeps = [low, jnp.logical_not(low)]
            qms = [jnp.where(keeps[h % 2], qps[h // 2], jnp.zeros_like(qps[0])) for h in heads]
            dms = [jnp.where(keeps[h % 2], dops[h // 2], jnp.zeros_like(dops[0])) for h in heads]
            dkh = [_dot(ds_b[h], qms[h], 0, 0) for h in heads]
            dvh = [_dot(p_b[h], dms[h], 0, 0) for h in heads]
            dks, dvs = [], []
            for kv in range(N_KV):
                tk = functools.reduce(jnp.add, dkh[kv * group:(kv + 1) * group])
                tv = functools.reduce(jnp.add, dvh[kv * group:(kv + 1) * group])
                dks.append((tk[:, :hd] + tk[:, hd:]) * scale)
                dvs.append(tv[:, :hd] + tv[:, hd:])
            dk2 = jnp.concatenate(dks, axis=1)
            dv2 = jnp.concatenate(dvs, axis=1)
            dbk_ref[...] += jnp.sum(dk2, axis=0, keepdims=True)
            dbv_ref[...] += jnp.sum(dv2, axis=0, keepdims=True)
            dk_ref[...] = (ck_ref[...] + dk2[:B]).astype(BF16)
            dv_ref[...] = (cv_ref[...] + dv2[:B]).astype(BF16)
            ck_ref[...] = dk2[B:]
            cv_ref[...] = dv2[B:]

        @pl.when(n == nb)
        def _():
            dk_ref[...] = ck_ref[...].astype(BF16)
            dv_ref[...] = cv_ref[...].astype(BF16)

    def cur(n):
        return jnp.minimum(n, nb - 1)

    def prev(n):
        return jnp.maximum(cur(n) - 1, 0)

    def outp(n):
        return jnp.maximum(n - 1, 0)

    in_specs = [pl.BlockSpec((B, HQ), lambda n: (cur(n), 0)),
                pl.BlockSpec((B, KVW), lambda n: (prev(n), kcol)), pl.BlockSpec((B, KVW), lambda n: (cur(n), kcol)),
                pl.BlockSpec((B, KVW), lambda n: (prev(n), kcol + 1)), pl.BlockSpec((B, KVW), lambda n: (cur(n), kcol + 1)),
                pl.BlockSpec((B, HQ), lambda n: (cur(n), 0)), pl.BlockSpec((n_heads, B), lambda n: (0, 0)),
                pl.BlockSpec((None, n_heads // 2, B, 4 * B), lambda n: (jnp.minimum(n, 1), 0, 0, 0))]
    out_shape = (jax.ShapeDtypeStruct((T, HQ), BF16), jax.ShapeDtypeStruct((T, KVW), BF16),
                 jax.ShapeDtypeStruct((T, KVW), BF16), jax.ShapeDtypeStruct((1, HQ), F32),
                 jax.ShapeDtypeStruct((1, KVW), F32), jax.ShapeDtypeStruct((1, KVW), F32),
                 jax.ShapeDtypeStruct((n_heads, B), F32))
    out_specs = (pl.BlockSpec((B, HQ), lambda n: (cur(n), 0)), pl.BlockSpec((B, KVW), lambda n: (outp(n), 0)),
                 pl.BlockSpec((B, KVW), lambda n: (outp(n), 0)), _vec_spec(HQ), _vec_spec(KVW), _vec_spec(KVW),
                 pl.BlockSpec((n_heads, B), lambda n: (0, 0)))
    return _call(body, name, (nb + 1,), in_specs, out_specs, out_shape, (qkv, qkv, qkv, qkv, qkv, do, sink_rows, bias),
                 scratch=[pltpu.VMEM((B, KVW), F32), pltpu.VMEM((B, KVW), F32)], side=side)


def _sum8(r, name):
    _, R, C = r.shape
    tr = _tile(R, 512, 16)

    def body(r_ref, o_ref):
        acc = r_ref[0].astype(F32)
        for d in range(1, N_DEV):
            acc = acc + r_ref[d].astype(F32)
        o_ref[...] = acc

    return _call(body, name, (R // tr,), [pl.BlockSpec((N_DEV, tr, C), lambda i: (0, i, 0))],
                 pl.BlockSpec((tr, C), lambda i: (i, 0)), jax.ShapeDtypeStruct((R, C), F32), (r,))


def _adamw(w, g, m, v, name):
    R, C = w.shape
    tr = _tile(R, 512, 8)
    c1 = 1.0 - ADAM_B1 ** ADAM_STEP
    c2 = 1.0 - ADAM_B2 ** ADAM_STEP

    def body(w_ref, g_ref, m_ref, v_ref, d_ref, nm_ref, nv_ref):
        gv = g_ref[...]
        nm = ADAM_B1 * m_ref[...] + (1.0 - ADAM_B1) * gv
        nv = ADAM_B2 * v_ref[...] + (1.0 - ADAM_B2) * (gv * gv)
        nm_ref[...] = nm
        nv_ref[...] = nv
        d_ref[...] = -ADAM_LR * ((nm / c1) / (jnp.sqrt(nv / c2) + ADAM_EPS) + ADAM_WD * w_ref[...])

    spec = pl.BlockSpec((tr, C), lambda i: (i, 0))
    shp = jax.ShapeDtypeStruct((R, C), F32)
    return _call(body, name, (R // tr,), [spec] * 4, (spec,) * 3, (shp, shp, shp), (w, g, m, v))


def _adamw_nd(w, g, m, v, name):
    shape = w.shape
    c = shape[-1]
    f = lambda a: a.reshape(-1, c)
    d, nm, nv = _adamw(f(w), f(g), f(m), f(v), name)
    return d.reshape(shape), nm.reshape(shape), nv.reshape(shape)


def _pack(arrs, width):
    flat = jnp.concatenate([a.reshape(-1).astype(F32) for a in arrs])
    n = flat.shape[0]
    quantum = 8 * width
    total = -(-n // quantum) * quantum
    return jnp.pad(flat, (0, total - n)).reshape(-1, width)


def _unpack(flat, shapes):
    out, off = [], 0
    for s in shapes:
        n = math.prod(s)
        out.append(flat[off:off + n].reshape(s))
        off += n
    return out


def kernel(x, c, norm1_g, norm2_g, ada_w, ada_b, attn_wqkv, attn_bqkv, attn_sinks, attn_wo, attn_bo, conv_w_in, conv_b_in, conv_dw, conv_dw_b, conv_ln_g, conv_ln_b, conv_w_out, conv_b_out, sgu_w_in, sgu_b_in, sgu_ln_g, sgu_ln_b, sgu_ws, sgu_bs, sgu_w_out, sgu_b_out, ffn_w_in, ffn_dw, ffn_dw_b, ffn_w_out, final_g, loss_target, m_norm1_g, m_norm2_g, m_ada_w, m_ada_b, m_attn_wqkv, m_attn_bqkv, m_attn_sinks, m_attn_wo, m_attn_bo, m_conv_w_in, m_conv_b_in, m_conv_dw, m_conv_dw_b, m_conv_ln_g, m_conv_ln_b, m_conv_w_out, m_conv_b_out, m_sgu_w_in, m_sgu_b_in, m_sgu_ln_g, m_sgu_ln_b, m_sgu_ws, m_sgu_bs, m_sgu_w_out, m_sgu_b_out, m_ffn_w_in, m_ffn_dw, m_ffn_dw_b, m_ffn_w_out, m_final_g, v_norm1_g, v_norm2_g, v_ada_w, v_ada_b, v_attn_wqkv, v_attn_bqkv, v_attn_sinks, v_attn_wo, v_attn_bo, v_conv_w_in, v_conv_b_in, v_conv_dw, v_conv_dw_b, v_conv_ln_g, v_conv_ln_b, v_conv_w_out, v_conv_b_out, v_sgu_w_in, v_sgu_b_in, v_sgu_ln_g, v_sgu_ln_b, v_sgu_ws, v_sgu_bs, v_sgu_w_out, v_sgu_b_out, v_ffn_w_in, v_ffn_dw, v_ffn_dw_b, v_ffn_w_out, v_final_g):
    W = dict(norm1_g=norm1_g, norm2_g=norm2_g, ada_w=ada_w, ada_b=ada_b, attn_wqkv=attn_wqkv, attn_bqkv=attn_bqkv, attn_sinks=attn_sinks, attn_wo=attn_wo, attn_bo=attn_bo, conv_w_in=conv_w_in, conv_b_in=conv_b_in, conv_dw=conv_dw, conv_dw_b=conv_dw_b, conv_ln_g=conv_ln_g, conv_ln_b=conv_ln_b, conv_w_out=conv_w_out, conv_b_out=conv_b_out, sgu_w_in=sgu_w_in, sgu_b_in=sgu_b_in, sgu_ln_g=sgu_ln_g, sgu_ln_b=sgu_ln_b, sgu_ws=sgu_ws, sgu_bs=sgu_bs, sgu_w_out=sgu_w_out, sgu_b_out=sgu_b_out, ffn_w_in=ffn_w_in, ffn_dw=ffn_dw, ffn_dw_b=ffn_dw_b, ffn_w_out=ffn_w_out, final_g=final_g)
    MOM = dict(norm1_g=m_norm1_g, norm2_g=m_norm2_g, ada_w=m_ada_w, ada_b=m_ada_b, attn_wqkv=m_attn_wqkv, attn_bqkv=m_attn_bqkv, attn_sinks=m_attn_sinks, attn_wo=m_attn_wo, attn_bo=m_attn_bo, conv_w_in=m_conv_w_in, conv_b_in=m_conv_b_in, conv_dw=m_conv_dw, conv_dw_b=m_conv_dw_b, conv_ln_g=m_conv_ln_g, conv_ln_b=m_conv_ln_b, conv_w_out=m_conv_w_out, conv_b_out=m_conv_b_out, sgu_w_in=m_sgu_w_in, sgu_b_in=m_sgu_b_in, sgu_ln_g=m_sgu_ln_g, sgu_ln_b=m_sgu_ln_b, sgu_ws=m_sgu_ws, sgu_bs=m_sgu_bs, sgu_w_out=m_sgu_w_out, sgu_b_out=m_sgu_b_out, ffn_w_in=m_ffn_w_in, ffn_dw=m_ffn_dw, ffn_dw_b=m_ffn_dw_b, ffn_w_out=m_ffn_w_out, final_g=m_final_g)
    VAR = dict(norm1_g=v_norm1_g, norm2_g=v_norm2_g, ada_w=v_ada_w, ada_b=v_ada_b, attn_wqkv=v_attn_wqkv, attn_bqkv=v_attn_bqkv, attn_sinks=v_attn_sinks, attn_wo=v_attn_wo, attn_bo=v_attn_bo, conv_w_in=v_conv_w_in, conv_b_in=v_conv_b_in, conv_dw=v_conv_dw, conv_dw_b=v_conv_dw_b, conv_ln_g=v_conv_ln_g, conv_ln_b=v_conv_ln_b, conv_w_out=v_conv_w_out, conv_b_out=v_conv_b_out, sgu_w_in=v_sgu_w_in, sgu_b_in=v_sgu_b_in, sgu_ln_g=v_sgu_ln_g, sgu_ln_b=v_sgu_ln_b, sgu_ws=v_sgu_ws, sgu_bs=v_sgu_bs, sgu_w_out=v_sgu_w_out, sgu_b_out=v_sgu_b_out, ffn_w_in=v_ffn_w_in, ffn_dw=v_ffn_dw, ffn_dw_b=v_ffn_dw_b, ffn_w_out=v_ffn_w_out, final_g=v_final_g)
    ORDER = list(W)

    _, T, D = x.shape
    depth = norm1_g.shape[0]
    n_heads = D // HEAD_DIM
    me = 4 * lax.axis_index("x") + 2 * lax.axis_index("y") + lax.axis_index("c")
    x0 = x.reshape(T, D)
    tgt = loss_target.reshape(T, D)

    small_sharded = ["attn_bqkv", "attn_bo", "conv_dw", "sgu_b_in", "sgu_ln_g", "sgu_ln_b", "sgu_b_out", "ffn_dw"]
    s_in = [c] + [W[n] for n in small_sharded]
    s_shapes = [a.shape for a in s_in]
    gathered_small = _all_gather(_pack(s_in, 128), "gather_small").reshape(N_DEV, -1)
    s_offs = [sum(math.prod(s) for s in s_shapes[:i]) for i in range(len(s_shapes))]

    def full(idx):
        shp = s_shapes[idx]
        a = gathered_small[:, s_offs[idx]:s_offs[idx] + math.prod(shp)].reshape((N_DEV,) + shp)
        return jnp.moveaxis(a, 0, -2).reshape(shp[:-1] + (N_DEV * shp[-1],))

    c_all = full(0).reshape(N_DEV, D)
    F_small = {n: full(1 + i) for i, n in enumerate(small_sharded)}
    attn_bias = _attn_bias(n_heads)

    c_act = c_all * jax.nn.sigmoid(c_all)
    c_pad = jnp.pad(c_act, ((0, 8), (0, 0))).astype(BF16)
    n_ada = ada_w.shape[-1]
    ada_cols = lax.dynamic_slice_in_dim(ada_b, me * n_ada, n_ada, axis=1)
    mod_loc = jnp.stack([_mm(c_pad, ada_w[i].astype(BF16), "nn", "ada_mod", bias=ada_cols[i:i + 1])
                         for i in range(depth)])
    mod_all = _all_gather(mod_loc, "gather_mod")
    mod_mine = lax.dynamic_index_in_dim(mod_all, me, axis=2, keepdims=False)
    mod = jnp.transpose(mod_mine, (1, 0, 2)).reshape(depth, 6, 1, D)

    mixer_names = {0: ("attn_wqkv", "attn_wo"), 1: ("conv_w_in", "conv_w_out"), 2: ("sgu_w_in", "sgu_w_out")}

    def group_pieces(key):
        part, i = key
        w_in, w_out = ("ffn_w_in", "ffn_w_out") if part == "ffn" else mixer_names[i % 3]
        l = i if part == "ffn" else i // 3
        return [(w_in, l, W[w_in].shape[-1]), (w_out, l, W[w_out].shape[1])]

    def local_weights(key):
        (w_in, l, _), (w_out, _, _) = group_pieces(key)
        return [W[w_in][l].T.astype(BF16), W[w_out][l].astype(BF16)]

    WF = {}

    def land_weights(key, gathered):
        for (n, l, r), g in zip(group_pieces(key), gathered):
            WF[(n, l)] = g.reshape(N_DEV * r, D)

    def gather_side(key):
        return ("gather", local_weights(key))

    def vec(a):
        return a.reshape(1, -1)

    def kind_of(i):
        return i % 3

    land_weights(("mix", 0), _exchange("gather", local_weights(("mix", 0)), "gather_first"))

    saved = []
    xs, y_prev, gate_prev = x0, None, None
    for i in range(depth):
        sh1, sc1, g1, sh2, sc2, g2 = [mod[i, k] for k in range(6)]
        kind, j = i % 3, i // 3
        st = dict(kind=kind, j=j)
        if y_prev is None:
            h1 = _norm_mod_fwd(xs, vec(norm1_g[i]), sc1, sh1, "norm1_fwd")
        else:
            xs, h1 = _norm_mod_fwd(xs, vec(norm1_g[i]), sc1, sh1, "norm1_fwd", y=y_prev, gate=gate_prev)
        st.update(x_in=xs, h1=h1)
        if kind == 0:
            qkv = _mm(h1, WF[("attn_wqkv", j)], "nt", "attn_qkv", out_dtype=BF16, bias=vec(F_small["attn_bqkv"][j]))
            o, landed = _attn_fwd(qkv, _sink_rows(attn_sinks[j]), attn_bias, n_heads, "attn_fwd",
                                  side=gather_side(("ffn", i)))
            land_weights(("ffn", i), landed)
            y1 = _mm(o, WF[("attn_wo", j)], "nn", "attn_out", bias=vec(F_small["attn_bo"][j]))
            st.update(qkv=qkv, o=o)
        elif kind == 1:
            p = _mm(h1, WF[("conv_w_in", j)], "nt", "conv_in", out_dtype=BF16, bias=vec(conv_b_in[j]))
            zc, s, landed = _convmix_fwd(p, F_small["conv_dw"][j], vec(conv_dw_b[j]), vec(conv_ln_g[j]),
                                         vec(conv_ln_b[j]), "convmix_fwd", side=gather_side(("ffn", i)))
            land_weights(("ffn", i), landed)
            y1 = _mm(s, WF[("conv_w_out", j)], "nn", "conv_out", bias=vec(conv_b_out[j]))
            st.update(p=p, zc=zc, s=s)
        else:
            p = _mm(h1, WF[("sgu_w_in", j)], "nt", "sgu_in", out_dtype=BF16, bias=vec(F_small["sgu_b_in"][j]))
            mm_ = _sgu_fwd(p, vec(F_small["sgu_ln_g"][j]), vec(F_small["sgu_ln_b"][j]), sgu_ws[j], sgu_bs[j].T, "sgu_fwd")
            y1 = _mm(mm_, WF[("sgu_w_out", j)], "nn", "sgu_out", bias=vec(F_small["sgu_b_out"][j]))
            st.update(p=p, m=mm_)
        xs, h2 = _norm_mod_fwd(xs, vec(norm2_g[i]), sc2, sh2, "norm2_fwd", y=y1, gate=g1)
        nxt = i + 1
        if nxt < depth:
            z, landed = _mm(h2, WF[("ffn_w_in", i)], "nt", "ffn_in", out_dtype=BF16, side=gather_side(("mix", nxt)))
            land_weights(("mix", nxt), landed)
        else:
            z = _mm(h2, WF[("ffn_w_in", i)], "nt", "ffn_in", out_dtype=BF16)
        if nxt < depth and kind_of(nxt) == 2:
            a, landed = _ffn_mid_fwd(z, F_small["ffn_dw"][i], vec(ffn_dw_b[i]), "ffn_mid_fwd", side=gather_side(("ffn", nxt)))
            land_weights(("ffn", nxt), landed)
        else:
            a = _ffn_mid_fwd(z, F_small["ffn_dw"][i], vec(ffn_dw_b[i]), "ffn_mid_fwd")
        y2 = _mm(a, WF[("ffn_w_out", i)], "nn", "ffn_out")
        st.update(y1=y1, x_mid=xs, h2=h2, z=z, a=a, y2=y2)
        saved.append(st)
        y_prev, gate_prev = y2, g2

    dx, d_final_g, loss_row = _final_loss(xs, y_prev, gate_prev, tgt, vec(final_g), "final_loss")
    loss = lax.psum(loss_row[0, 0], ("x", "y", "c"))

    G = {n: [None] * W[n].shape[0] for n in ORDER if n != "final_g"}
    GW = {}
    recv = {}
    dmod = [None] * depth

    def scatter_side(key):
        return ("scatter", [GW[(n, l)].reshape(N_DEV, r, D) for n, l, r in group_pieces(key)])

    for i in reversed(range(depth)):
        st = saved[i]
        sh1, sc1, g1, sh2, sc2, g2 = [mod[i, k] for k in range(6)]
        kind, j = st["kind"], st["j"]
        dy2, dg2, _ = _gate_bwd(dx, st["y2"], g2, "gate_bwd")
        GW[("ffn_w_out", i)] = _mm(st["a"], dy2, "tn", "ffn_out_dw", out_dtype=BF16)
        da = _mm(dy2, WF[("ffn_w_out", i)], "nt", "ffn_out_dx", out_dtype=BF16)
        if i + 1 < depth:
            dzg, dzu, ddw, ddwb, recv[("mix", i + 1)] = _ffn_mid_bwd(
                st["z"], da, F_small["ffn_dw"][i], vec(ffn_dw_b[i]), "ffn_mid_bwd", side=scatter_side(("mix", i + 1)))
        else:
            dzg, dzu, ddw, ddwb = _ffn_mid_bwd(st["z"], da, F_small["ffn_dw"][i], vec(ffn_dw_b[i]), "ffn_mid_bwd")
        G["ffn_dw"][i], G["ffn_dw_b"][i] = ddw, ddwb[0]
        GW[("ffn_w_in", i)] = _mm(dzg, st["h2"], "tn", "ffn_in_dw", out_dtype=BF16, a2=dzu)
        dh2 = _mm(dzg, WF[("ffn_w_in", i)], "nn", "ffn_in_dx", a2=dzu)
        dx, dn2, dsc2, dsh2 = _norm_mod_bwd(dh2, st["x_mid"], vec(norm2_g[i]), sc2, sh2, dx, "norm_bwd")
        G["norm2_g"][i] = dn2[0]
        dy1, dg1, dbo = _gate_bwd(dx, st["y1"], g1, "gate_bwd")
        ffn_side = scatter_side(("ffn", i))
        if kind == 0:
            G["attn_bo"][j] = dbo[0]
            GW[("attn_wo", j)] = _mm(st["o"], dy1, "tn", "attn_out_dw", out_dtype=BF16)
            do = _mm(dy1, WF[("attn_wo", j)], "nt", "attn_out_dx", out_dtype=BF16)
            dq, dk, dv, dbq, dbk, dbv, dsk, recv[("ffn", i)] = _attn_bwd(st["qkv"], do, _sink_rows(attn_sinks[j]), attn_bias, n_heads,
                                                                         "attn_bwd", side=ffn_side)
            dqkv = jnp.concatenate([dq, dk, dv], axis=1)
            G["attn_bqkv"][j] = jnp.concatenate([dbq, dbk, dbv], axis=1)[0]
            G["attn_sinks"][j] = dsk[:, 0]
            GW[("attn_wqkv", j)] = _mm(dqkv, st["h1"], "tn", "attn_qkv_dw", out_dtype=BF16)
            dh1 = _mm(dqkv, WF[("attn_wqkv", j)], "nn", "attn_qkv_dx")
        elif kind == 1:
            G["conv_b_out"][j] = dbo[0]
            GW[("conv_w_out", j)] = _mm(st["s"], dy1, "tn", "conv_out_dw", out_dtype=BF16)
            ds = _mm(dy1, WF[("conv_w_out", j)], "nt", "conv_out_dx")
            dp, dbin, ddw, ddwb, dlng, dlnb, recv[("ffn", i)] = _convmix_bwd(
                ds, st["zc"], st["p"], F_small["conv_dw"][j], vec(conv_ln_g[j]), vec(conv_ln_b[j]), "convmix_bwd",
                side=ffn_side)
            G["conv_b_in"][j], G["conv_dw"][j], G["conv_dw_b"][j] = dbin[0], ddw, ddwb[0]
            G["conv_ln_g"][j], G["conv_ln_b"][j] = dlng[0], dlnb[0]
            GW[("conv_w_in", j)] = _mm(dp, st["h1"], "tn", "conv_in_dw", out_dtype=BF16)
            dh1 = _mm(dp, WF[("conv_w_in", j)], "nn", "conv_in_dx")
        else:
            G["sgu_b_out"][j] = dbo[0]
            GW[("sgu_w_out", j)] = _mm(st["m"], dy1, "tn", "sgu_out_dw", out_dtype=BF16)
            dm = _mm(dy1, WF[("sgu_w_out", j)], "nt", "sgu_out_dx")
            dp, dbin, dlng, dlnb, dws, dbst, recv[("ffn", i)] = _sgu_bwd(
                st["p"], dm, vec(F_small["sgu_ln_g"][j]), vec(F_small["sgu_ln_b"][j]), sgu_ws[j], sgu_bs[j].T, "sgu_bwd",
                side=ffn_side)
            G["sgu_b_in"][j], G["sgu_ln_g"][j], G["sgu_ln_b"][j] = dbin[0], dlng[0], dlnb[0]
            G["sgu_ws"][j], G["sgu_bs"][j] = dws, dbst.T
            GW[("sgu_w_in", j)] = _mm(dp, st["h1"], "tn", "sgu_in_dw", out_dtype=BF16)
            dh1 = _mm(dp, WF[("sgu_w_in", j)], "nn", "sgu_in_dx")
        dx, dn1, dsc1, dsh1 = _norm_mod_bwd(dh1, st["x_in"], vec(norm1_g[i]), sc1, sh1, dx, "norm_bwd")
        G["norm1_g"][i] = dn1[0]
        dmod[i] = jnp.concatenate([dsh1, dsc1, dg1, dsh2, dsc2, dg2], axis=1)[0]
    grad_x = dx.reshape(x.shape)
    recv[("mix", 0)] = _exchange(*scatter_side(("mix", 0)), "exchange_last")

    small_names = [n for n in ORDER if n not in
                   ("ada_w", "ada_b", "attn_wqkv", "attn_wo", "conv_w_in", "conv_w_out", "sgu_w_in", "sgu_w_out",
                    "ffn_w_in", "ffn_w_out", "final_g")]
    g_small_full = [jnp.stack(G[n]) for n in small_names] + [d_final_g[0]]
    small_names = small_names + ["final_g"]
    dmod_arr = jnp.stack(dmod)
    g_small_in = g_small_full + [dmod_arr]
    g_shapes = [a.shape for a in g_small_in]
    packed_gs = _pack(g_small_in, 128)
    gathered_gs = _all_gather(packed_gs, "gather_small_grads")
    summed_gs = _sum8(gathered_gs, "sum_small_grads").reshape(-1)
    gsum = dict(zip(small_names + ["ada_b"], _unpack(summed_gs, g_shapes)))

    def local_shard(n, a):
        if n in small_sharded:
            w = W[n].shape[-1]
            return lax.dynamic_slice_in_dim(a, me * w, w, axis=a.ndim - 1)
        return a

    gsum = {n: local_shard(n, a) for n, a in gsum.items()}

    dmod_all = jnp.stack([_unpack(gathered_gs[d].reshape(-1), g_shapes)[-1] for d in range(N_DEV)])
    dmod_cols = lax.dynamic_slice_in_dim(dmod_all, me * n_ada, n_ada, axis=2)
    dmod_pad = jnp.pad(dmod_cols, ((0, 8), (0, 0), (0, 0))).astype(BF16)
    g_ada_w = jnp.stack([_mm(c_pad, dmod_pad[:, i, :], "tn", "ada_dw") for i in range(depth)])

    big = {}
    for i in range(depth):
        for part in ("mix", "ffn"):
            for k, ((n, l, r), got) in enumerate(zip(group_pieces((part, i)), recv[(part, i)])):
                gp = _sum8(got, "sum_weight_grads")
                big.setdefault(n, {})[l] = gp.T if k == 0 else gp
    grads = {n: jnp.stack([v[l] for l in range(len(v))]) for n, v in big.items()}
    grads["ada_w"] = g_ada_w
    grads.update(gsum)

    delta, new_m, new_v = {}, {}, {}
    big_names = ["ada_w", "attn_wqkv", "attn_wo", "conv_w_in", "conv_w_out", "sgu_w_in", "sgu_w_out", "ffn_w_in",
                 "ffn_w_out"]
    for n in big_names:
        delta[n], new_m[n], new_v[n] = _adamw_nd(W[n], grads[n], MOM[n], VAR[n], "adamw_" + n)
    rest = [n for n in ORDER if n not in big_names]
    rest_shapes = [W[n].shape for n in rest]
    pk = lambda d: _pack([d[n] for n in rest], 128)
    d_s, m_s, v_s = _adamw(pk(W), pk(grads), pk(MOM), pk(VAR), "adamw_small")
    for n, a, b_, c_ in zip(rest, _unpack(d_s.reshape(-1), rest_shapes), _unpack(m_s.reshape(-1), rest_shapes),
                            _unpack(v_s.reshape(-1), rest_shapes)):
        delta[n], new_m[n], new_v[n] = a, b_, c_

    return (loss, grad_x, *[grads[n] for n in ORDER], *[delta[n] for n in ORDER],
            *[new_m[n] for n in ORDER], *[new_v[n] for n in ORDER])
```

```python
import functools
import math

import jax
import jax.numpy as jnp
from jax import lax
from jax.experimental import pallas as pl
from jax.experimental.pallas import tpu as pltpu

F32 = jnp.float32
BF16 = jnp.bfloat16

N_DEV = 8
HEAD_DIM = 64
N_KV = 4
BLOCK = 128
NORM_EPS = 1e-6
NEG_INF = -1e30
ADAM_LR = 0.001
ADAM_B1 = 0.9
ADAM_B2 = 0.999
ADAM_EPS = 1e-08
ADAM_WD = 0.01
ADAM_STEP = 10
V7X_VMEM_LIMIT = 56 * 1024 * 1024
MESH = pl.DeviceIdType.MESH


def _cparams():
    return pltpu.CompilerParams(vmem_limit_bytes=V7X_VMEM_LIMIT)


def _tile(n, cap, align):
    best = None
    for t in range(align, min(n, cap) + 1, align):
        if n % t == 0:
            best = t
    return best if best is not None else n


def _dot(a, b, ca, cb):
    return lax.dot_general(a, b, (((ca,), (cb,)), ((), ())), preferred_element_type=F32)


def _sigmoid(x):
    return 1.0 / (1.0 + jnp.exp(-x))


def _rows(x, start, n):
    return lax.slice_in_dim(x, start, start + n, axis=0)


def _my_coords():
    return lax.axis_index("x"), lax.axis_index("y"), lax.axis_index("c")


def _peer(m, mx, my, mc):
    px = (mx + ((m >> 2) & 1)) % 2
    py = (my + ((m >> 1) & 1)) % 2
    pc = (mc + (m & 1)) % 2
    return px, py, pc


def _exchange_copies(kind, x_ref, o_ref, send_sems, recv_sems, local_sem):
    mx, my, mc = _my_coords()
    me = 4 * mx + 2 * my + mc
    local = pltpu.make_async_copy(x_ref if kind == "gather" else x_ref.at[me], o_ref.at[me], local_sem)
    remote = []
    for m in range(1, N_DEV):
        px, py, pc = _peer(m, mx, my, mc)
        src = x_ref if kind == "gather" else x_ref.at[4 * px + 2 * py + pc]
        remote.append(pltpu.make_async_remote_copy(
            src_ref=src, dst_ref=o_ref.at[me], send_sem=send_sems.at[m - 1], recv_sem=recv_sems.at[m - 1],
            device_id=(px, py, pc), device_id_type=MESH))
    return local, remote


def _exchange_start(kind, *refs):
    local, remote = _exchange_copies(kind, *refs)
    local.start()
    for cp in remote:
        cp.start()


def _exchange_wait(kind, *refs):
    local, remote = _exchange_copies(kind, *refs)
    for cp in remote:
        cp.wait_recv()
    for cp in remote:
        cp.wait_send()
    local.wait()


_EXCHANGE_SEMS = [pltpu.SemaphoreType.DMA((N_DEV - 1,)), pltpu.SemaphoreType.DMA((N_DEV - 1,)), pltpu.SemaphoreType.DMA]


def _exchange_shape(kind, x):
    return jax.ShapeDtypeStruct(((N_DEV,) + x.shape) if kind == "gather" else x.shape, x.dtype)


def _exchange(kind, xs, name):
    n = len(xs)

    def body(*refs):
        x_refs, o_refs, sems = refs[:n], refs[n:2 * n], refs[2 * n:]
        for e in range(n):
            _exchange_start(kind, x_refs[e], o_refs[e], *sems[3 * e:3 * e + 3])
        for e in range(n):
            _exchange_wait(kind, x_refs[e], o_refs[e], *sems[3 * e:3 * e + 3])

    any_spec = pl.BlockSpec(memory_space=pl.ANY)
    return tuple(pl.pallas_call(
        body, name=name, out_shape=tuple(_exchange_shape(kind, x) for x in xs),
        in_specs=[any_spec] * n, out_specs=(any_spec,) * n, scratch_shapes=list(_EXCHANGE_SEMS) * n,
    )(*xs))


def _all_gather(x, name):
    return _exchange("gather", [x], name)[0]


def _call(body, name, grid, in_specs, out_specs, out_shape, args, scratch=(), side=None):
    single = not isinstance(out_shape, (tuple, list))
    if single:
        out_shape, out_specs = (out_shape,), (out_specs,)
    if side is None:
        res = pl.pallas_call(body, name=name, grid=grid, in_specs=list(in_specs), out_specs=tuple(out_specs),
                             out_shape=tuple(out_shape), scratch_shapes=list(scratch), compiler_params=_cparams())(*args)
        return res[0] if single else tuple(res)
    kind, xs = side
    n_in, n_out, n_scr, n_x = len(in_specs), len(out_shape), len(scratch), len(xs)

    def wrapped(*refs):
        ins, x_refs = refs[:n_in], refs[n_in:n_in + n_x]
        o0 = n_in + n_x
        outs, o_refs = refs[o0:o0 + n_out], refs[o0 + n_out:o0 + n_out + n_x]
        s0 = o0 + n_out + n_x
        scr, sems = refs[s0:s0 + n_scr], refs[s0 + n_scr:]
        ids = [pl.program_id(a) for a in range(len(grid))]
        first = functools.reduce(jnp.logical_and, [i == 0 for i in ids])
        last = functools.reduce(jnp.logical_and, [i == g - 1 for i, g in zip(ids, grid)])

        @pl.when(first)
        def _():
            for e in range(n_x):
                _exchange_start(kind, x_refs[e], o_refs[e], *sems[3 * e:3 * e + 3])

        body(*ins, *outs, *scr)

        @pl.when(last)
        def _():
            for e in range(n_x):
                _exchange_wait(kind, x_refs[e], o_refs[e], *sems[3 * e:3 * e + 3])

    any_spec = pl.BlockSpec(memory_space=pl.ANY)
    res = pl.pallas_call(
        wrapped, name=name, grid=grid, in_specs=list(in_specs) + [any_spec] * n_x,
        out_specs=tuple(out_specs) + (any_spec,) * n_x,
        out_shape=tuple(out_shape) + tuple(_exchange_shape(kind, x) for x in xs),
        scratch_shapes=list(scratch) + list(_EXCHANGE_SEMS) * n_x, compiler_params=_cparams())(*args, *xs)
    return tuple(res[:n_out]) + (tuple(res[n_out:]),)


def _mm(a, b, mode, name, out_dtype=F32, bias=None, side=None, a2=None, tm_cap=1024, tn_cap=1408, tk_cap=1408):
    w2 = 0 if a2 is None else a2.shape[1]
    assert a2 is None or (mode in ("nn", "tn") and a2.shape[0] == a.shape[0])
    if mode == "nn":
        (M, K), (K2, N) = a.shape, b.shape
        K += w2
    elif mode == "nt":
        (M, K), (N, K2) = a.shape, b.shape
    else:
        (K, M), (K2, N) = a.shape, b.shape
        M += w2
    assert K == K2, (a.shape, b.shape, mode)
    if mode == "tn":
        tm_cap, tk_cap = 256, 8192
    tm = _tile(math.gcd(M, w2) if mode == "tn" else M, tm_cap, 128 if mode == "tn" else 16)
    tn = _tile(N, tn_cap, 128)
    tk = _tile(math.gcd(K, w2) if mode == "nn" else K, tk_cap, 128)
    nk = K // tk
    n1 = a.shape[1] // (tm if mode == "tn" else tk)
    ca, cb = {"nn": (1, 0), "nt": (1, 1), "tn": (0, 0)}[mode]
    has_bias = bias is not None

    def body(*refs):
        refs = list(refs)
        a_ref = refs.pop(0)
        a2_ref = refs.pop(0) if a2 is not None else None
        b_ref = refs.pop(0)
        bias_ref = refs.pop(0) if has_bias else None
        o_ref, acc_ref = refs
        k = pl.program_id(2)

        def step(lhs_ref):
            part = _dot(lhs_ref[...].astype(BF16), b_ref[...].astype(BF16), ca, cb)

            @pl.when(k == 0)
            def _():
                acc_ref[...] = part

            @pl.when(k > 0)
            def _():
                acc_ref[...] += part

        if a2 is None:
            step(a_ref)
        else:
            col = pl.program_id(0) if mode == "tn" else k
            pl.when(col < n1)(lambda: step(a_ref))
            pl.when(col >= n1)(lambda: step(a2_ref))

        @pl.when(k == nk - 1)
        def _():
            r = acc_ref[...]
            if has_bias:
                r = r + bias_ref[...]
            o_ref[...] = r.astype(out_dtype)

    if mode == "tn":
        a_specs = [pl.BlockSpec((tk, tm), lambda i, j, k: (k, jnp.minimum(i, n1 - 1)))]
        if a2 is not None:
            a_specs.append(pl.BlockSpec((tk, tm), lambda i, j, k: (k, jnp.maximum(i - n1, 0))))
    else:
        a_specs = [pl.BlockSpec((tm, tk), lambda i, j, k: (i, jnp.minimum(k, n1 - 1)))]
        if a2 is not None:
            a_specs.append(pl.BlockSpec((tm, tk), lambda i, j, k: (i, jnp.maximum(k - n1, 0))))
    if mode == "nt":
        b_spec = pl.BlockSpec((tn, tk), lambda i, j, k: (j, k))
    else:
        b_spec = pl.BlockSpec((tk, tn), lambda i, j, k: (k, j))
    in_specs = a_specs + [b_spec]
    args = [a] + ([a2] if a2 is not None else []) + [b]
    if has_bias:
        in_specs.append(pl.BlockSpec((1, tn), lambda i, j, k: (0, j)))
        args.append(bias)
    return _call(body, name, (M // tm, N // tn, nk), in_specs, pl.BlockSpec((tm, tn), lambda i, j, k: (i, j)),
                 jax.ShapeDtypeStruct((M, N), out_dtype), args, scratch=[pltpu.VMEM((tm, tn), F32)], side=side)


def _vec_spec(n):
    return pl.BlockSpec((1, n), lambda t: (0, 0))


def _norm_mod_fwd(x, g, sc, sh, name, y=None, gate=None):
    T, D = x.shape
    tr = _tile(T, 256, 16)
    has_y = y is not None

    def body(*refs):
        if has_y:
            x_ref, y_ref, gate_ref, g_ref, sc_ref, sh_ref, xn_ref, h_ref = refs
            xv = x_ref[...] + gate_ref[...] * y_ref[...]
            xn_ref[...] = xv
        else:
            x_ref, g_ref, sc_ref, sh_ref, h_ref = refs
            xv = x_ref[...]
        r = lax.rsqrt(jnp.mean(xv * xv, axis=-1, keepdims=True) + NORM_EPS)
        h = (xv * r * g_ref[...]) * (1.0 + sc_ref[...]) + sh_ref[...]
        h_ref[...] = h.astype(BF16)

    row = pl.BlockSpec((tr, D), lambda t: (t, 0))
    vec = _vec_spec(D)
    if has_y:
        in_specs, args = [row, row, vec, vec, vec, vec], [x, y, gate, g, sc, sh]
        out_shape = (jax.ShapeDtypeStruct((T, D), F32), jax.ShapeDtypeStruct((T, D), BF16))
        out_specs = (row, row)
    else:
        in_specs, args = [row, vec, vec, vec], [x, g, sc, sh]
        out_shape = jax.ShapeDtypeStruct((T, D), BF16)
        out_specs = row
    return _call(body, name, (T // tr,), in_specs, out_specs, out_shape, args)


def _gate_bwd(dxn, y, gate, name):
    T, D = dxn.shape
    tr = _tile(T, 256, 16)

    def body(dxn_ref, y_ref, gate_ref, dy_ref, dgate_ref, dbias_ref):
        @pl.when(pl.program_id(0) == 0)
        def _():
            dgate_ref[...] = jnp.zeros_like(dgate_ref)
            dbias_ref[...] = jnp.zeros_like(dbias_ref)
        d = dxn_ref[...]
        dy = d * gate_ref[...]
        dy_ref[...] = dy.astype(BF16)
        dgate_ref[...] += jnp.sum(d * y_ref[...], axis=0, keepdims=True)
        dbias_ref[...] += jnp.sum(dy, axis=0, keepdims=True)

    row = pl.BlockSpec((tr, D), lambda t: (t, 0))
    vec = _vec_spec(D)
    return _call(body, name, (T // tr,), [row, row, vec], (row, vec, vec),
                 (jax.ShapeDtypeStruct((T, D), BF16), jax.ShapeDtypeStruct((1, D), F32),
                  jax.ShapeDtypeStruct((1, D), F32)), (dxn, y, gate))


def _norm_mod_bwd(dh, x, g, sc, sh, dxn, name):
    T, D = x.shape
    tr = _tile(T, 256, 16)

    def body(dh_ref, x_ref, g_ref, sc_ref, sh_ref, dxn_ref, dx_ref, dg_ref, dsc_ref, dsh_ref):
        @pl.when(pl.program_id(0) == 0)
        def _():
            dg_ref[...] = jnp.zeros_like(dg_ref)
            dsc_ref[...] = jnp.zeros_like(dsc_ref)
            dsh_ref[...] = jnp.zeros_like(dsh_ref)
        dh = dh_ref[...].astype(F32)
        xv = x_ref[...]
        gv = g_ref[...]
        r = lax.rsqrt(jnp.mean(xv * xv, axis=-1, keepdims=True) + NORM_EPS)
        yv = xv * r
        dsh_ref[...] += jnp.sum(dh, axis=0, keepdims=True)
        dsc_ref[...] += jnp.sum(dh * (yv * gv), axis=0, keepdims=True)
        dn = dh * (1.0 + sc_ref[...])
        dg_ref[...] += jnp.sum(dn * yv, axis=0, keepdims=True)
        dy = dn * gv
        dx = r * (dy - yv * jnp.mean(dy * yv, axis=-1, keepdims=True))
        dx_ref[...] = dxn_ref[...] + dx

    row = pl.BlockSpec((tr, D), lambda t: (t, 0))
    vec = _vec_spec(D)
    vshape = jax.ShapeDtypeStruct((1, D), F32)
    return _call(body, name, (T // tr,), [row, row, vec, vec, vec, row], (row, vec, vec, vec),
                 (jax.ShapeDtypeStruct((T, D), F32), vshape, vshape, vshape), (dh, x, g, sc, sh, dxn))


def _final_loss(x, y, gate, tgt, g, name):
    T, D = x.shape
    tr = _tile(T, 256, 16)

    def body(x_ref, y_ref, gate_ref, tgt_ref, g_ref, dx_ref, dg_ref, loss_ref):
        @pl.when(pl.program_id(0) == 0)
        def _():
            dg_ref[...] = jnp.zeros_like(dg_ref)
            loss_ref[...] = jnp.zeros_like(loss_ref)
        xv = x_ref[...] + gate_ref[...] * y_ref[...]
        gv = g_ref[...]
        r = lax.rsqrt(jnp.mean(xv * xv, axis=-1, keepdims=True) + NORM_EPS)
        yv = xv * r
        e = yv * gv - tgt_ref[...]
        per_row = jnp.mean(e * e, axis=-1, keepdims=True)
        loss_ref[...] += 0.5 * jnp.sum(per_row, axis=0, keepdims=True)
        dout = e * (1.0 / D)
        dg_ref[...] += jnp.sum(dout * yv, axis=0, keepdims=True)
        dy = dout * gv
        dx_ref[...] = r * (dy - yv * jnp.mean(dy * yv, axis=-1, keepdims=True))

    row = pl.BlockSpec((tr, D), lambda t: (t, 0))
    vec = _vec_spec(D)
    return _call(body, name, (T // tr,), [row, row, vec, row, vec], (row, vec, _vec_spec(128)),
                 (jax.ShapeDtypeStruct((T, D), F32), jax.ShapeDtypeStruct((1, D), F32),
                  jax.ShapeDtypeStruct((1, 128), F32)), (x, y, gate, tgt, g))


FFN_HALO = 16
FFN_PAD = 8
FFN_CHUNK = 64


def _lane_groups(cw):
    lw = 128 if cw % 128 == 0 else cw
    return lw, [pl.ds(g * lw, lw) for g in range(cw // lw)]


def _ffn_mid_fwd(z, dw, b, name, side=None):
    T, F2 = z.shape
    F = F2 // 2
    K = dw.shape[0]
    H, P = FFN_HALO, FFN_PAD
    tr = _tile(T, 512, H)
    cw = _tile(F, 512, 128)
    nc = F // cw
    rc = _tile(tr, FFN_CHUNK, 16)
    lw, groups = _lane_groups(cw)

    def body(gc_ref, gh_ref, uc_ref, uh_ref, wg_ref, wu_ref, bg_ref, bu_ref, a_ref, xg_s, xu_s):
        first = pl.program_id(1) == 0

        def fill(s, h_ref, c_ref):
            for g, ls in enumerate(groups):
                s[g, 0:P, :] = jnp.where(first, 0.0, h_ref[:, ls].astype(F32)[H - P:H])
                s[g, P:P + tr, :] = c_ref[:, ls].astype(F32)

        fill(xg_s, gh_ref, gc_ref)
        fill(xu_s, uh_ref, uc_ref)

        def chunk(ci, carry):
            r0 = pl.multiple_of(ci * rc, rc)
            for g, ls in enumerate(groups):
                def conv(s, w_ref, b_ref):
                    acc = b_ref[:, ls] + w_ref[K - 1:K, ls] * s[g, pl.ds(r0 + P, rc), :]
                    for k in range(K - 1):
                        acc = acc + w_ref[k:k + 1, ls] * s[g, pl.ds(r0 + P - (K - 1 - k), rc), :]
                    return acc

                gv = conv(xg_s, wg_ref, bg_ref)
                uv = conv(xu_s, wu_ref, bu_ref)
                a_ref[pl.ds(r0, rc), ls] = (gv * _sigmoid(gv) * uv).astype(BF16)
            return carry

        lax.fori_loop(0, tr // rc, chunk, 0)

    rb = tr // H
    in_specs = [
        pl.BlockSpec((tr, cw), lambda j, t: (t, j)),
        pl.BlockSpec((H, cw), lambda j, t: (jnp.maximum(t * rb - 1, 0), j)),
        pl.BlockSpec((tr, cw), lambda j, t: (t, j + nc)),
        pl.BlockSpec((H, cw), lambda j, t: (jnp.maximum(t * rb - 1, 0), j + nc)),
        pl.BlockSpec((K, cw), lambda j, t: (0, j)),
        pl.BlockSpec((K, cw), lambda j, t: (0, j + nc)),
        pl.BlockSpec((1, cw), lambda j, t: (0, j)),
        pl.BlockSpec((1, cw), lambda j, t: (0, j + nc)),
    ]
    return _call(body, name, (nc, T // tr), in_specs, pl.BlockSpec((tr, cw), lambda j, t: (t, j)),
                 jax.ShapeDtypeStruct((T, F), BF16), (z, z, z, z, dw, dw, b, b),
                 scratch=[pltpu.VMEM((len(groups), P + tr, lw), F32)] * 2, side=side)


def _ffn_mid_bwd(z, da, dw, b, name, side=None):
    T, F2 = z.shape
    F = F2 // 2
    K = dw.shape[0]
    H, P = FFN_HALO, FFN_PAD
    tr = _tile(T, 512, H)
    cw = _tile(F, 512, 128)
    nc = F // cw
    nt = T // tr
    rb = tr // H
    rc = _tile(tr, FFN_CHUNK, 16)
    ext = rc + P
    lw, groups = _lane_groups(cw)

    def body(gp_ref, gc_ref, gn_ref, up_ref, uc_ref, un_ref, dac_ref, dan_ref, wg_ref, wu_ref, bg_ref, bu_ref,
             dzg_ref, dzu_ref, dwg_ref, dwu_ref, dbg_ref, dbu_ref, xg_s, xu_s, da_s, dg_s, du_s, acc_s):
        t = pl.program_id(1)
        first = t == 0
        last = t == nt - 1

        @pl.when(first)
        def _():
            for r in (dwg_ref, dwu_ref, dbg_ref, dbu_ref):
                r[...] = jnp.zeros_like(r)

        def fill(s, p_ref, c_ref, n_ref):
            for g, ls in enumerate(groups):
                s[g, 0:P, :] = jnp.where(first, 0.0, p_ref[:, ls].astype(F32)[H - P:H])
                s[g, P:P + tr, :] = c_ref[:, ls].astype(F32)
                s[g, P + tr:2 * P + tr, :] = jnp.where(last, 0.0, n_ref[:, ls].astype(F32)[0:P])

        fill(xg_s, gp_ref, gc_ref, gn_ref)
        fill(xu_s, up_ref, uc_ref, un_ref)
        for g, ls in enumerate(groups):
            da_s[g, 0:tr, :] = dac_ref[:, ls].astype(F32)
            da_s[g, tr:tr + P, :] = jnp.where(last, 0.0, dan_ref[:, ls].astype(F32)[0:P])
        acc_s[...] = jnp.zeros_like(acc_s)

        def fold(v):
            return jnp.sum(v.reshape(rc // 8, 8, lw), axis=0)

        def chunk(ci, carry):
            r0 = pl.multiple_of(ci * rc, rc)
            for g, ls in enumerate(groups):
                def conv_ext(s, w_ref, b_ref):
                    acc = b_ref[:, ls] + w_ref[K - 1:K, ls] * s[g, pl.ds(r0 + P, ext), :]
                    for k in range(K - 1):
                        acc = acc + w_ref[k:k + 1, ls] * s[g, pl.ds(r0 + P - (K - 1 - k), ext), :]
                    return acc

                gv = conv_ext(xg_s, wg_ref, bg_ref)
                uv = conv_ext(xu_s, wu_ref, bu_ref)
                dav = da_s[g, pl.ds(r0, ext), :]
                sg = _sigmoid(gv)
                dg = dav * uv * (sg * (1.0 + gv * (1.0 - sg)))
                du = dav * (gv * sg)
                dg_s[g] = dg
                du_s[g] = du
                for idx, (d_s, dval, x_s, w_ref, dz_ref) in enumerate(
                        ((dg_s, dg, xg_s, wg_ref, dzg_ref), (du_s, du, xu_s, wu_ref, dzu_ref))):
                    cur = dval[:rc]
                    acc_s[idx, K, g] += fold(cur)
                    for k in range(K):
                        acc_s[idx, k, g] += fold(cur * x_s[g, pl.ds(r0 + P - (K - 1 - k), rc), :])
                    dz = w_ref[K - 1:K, ls] * cur
                    for k in range(K - 1):
                        dz = dz + w_ref[k:k + 1, ls] * d_s[g, pl.ds(K - 1 - k, rc), :]
                    dz_ref[pl.ds(r0, rc), ls] = dz.astype(BF16)
            return carry

        lax.fori_loop(0, tr // rc, chunk, 0)
        for idx, (dw_ref, db_ref) in enumerate(((dwg_ref, dbg_ref), (dwu_ref, dbu_ref))):
            for g, ls in enumerate(groups):
                db_ref[:, ls] += jnp.sum(acc_s[idx, K, g], axis=0, keepdims=True)
                dw_ref[:, ls] += jnp.concatenate(
                    [jnp.sum(acc_s[idx, k, g], axis=0, keepdims=True) for k in range(K)], axis=0)

    def prev(t):
        return jnp.maximum(t * rb - 1, 0)

    def nxt(t):
        return jnp.minimum((t + 1) * rb, T // H - 1)

    def zspecs(off):
        return [pl.BlockSpec((H, cw), lambda j, t: (prev(t), j + off)),
                pl.BlockSpec((tr, cw), lambda j, t: (t, j + off)),
                pl.BlockSpec((H, cw), lambda j, t: (nxt(t), j + off))]

    in_specs = zspecs(0) + zspecs(nc) + [
        pl.BlockSpec((tr, cw), lambda j, t: (t, j)),
        pl.BlockSpec((H, cw), lambda j, t: (nxt(t), j)),
        pl.BlockSpec((K, cw), lambda j, t: (0, j)),
        pl.BlockSpec((K, cw), lambda j, t: (0, j + nc)),
        pl.BlockSpec((1, cw), lambda j, t: (0, j)),
        pl.BlockSpec((1, cw), lambda j, t: (0, j + nc)),
    ]
    out_shape = (jax.ShapeDtypeStruct((T, F), BF16), jax.ShapeDtypeStruct((T, F), BF16),
                 jax.ShapeDtypeStruct((K, F), F32), jax.ShapeDtypeStruct((K, F), F32),
                 jax.ShapeDtypeStruct((1, F), F32), jax.ShapeDtypeStruct((1, F), F32))
    out_specs = (pl.BlockSpec((tr, cw), lambda j, t: (t, j)), pl.BlockSpec((tr, cw), lambda j, t: (t, j)),
                 pl.BlockSpec((K, cw), lambda j, t: (0, j)), pl.BlockSpec((K, cw), lambda j, t: (0, j)),
                 pl.BlockSpec((1, cw), lambda j, t: (0, j)), pl.BlockSpec((1, cw), lambda j, t: (0, j)))
    ng = len(groups)
    scratch = [pltpu.VMEM((ng, 2 * P + tr, lw), F32), pltpu.VMEM((ng, 2 * P + tr, lw), F32),
               pltpu.VMEM((ng, P + tr, lw), F32), pltpu.VMEM((ng, ext, lw), F32), pltpu.VMEM((ng, ext, lw), F32),
               pltpu.VMEM((2, K + 1, ng, 8, lw), F32)]
    res = _call(body, name, (nc, nt), in_specs, out_specs, out_shape, (z, z, z, z, z, z, da, da, dw, dw, b, b),
                scratch=scratch, side=side)
    dzg, dzu, dwg, dwu, dbg, dbu = res[:6]
    return (dzg, dzu, jnp.concatenate([dwg, dwu], axis=1), jnp.concatenate([dbg, dbu], axis=1)) + tuple(res[6:])


CONV_HALO = 32


def _convmix_fwd(p, dw, dwb, lng, lnb, name, side=None):
    T, D2 = p.shape
    D = D2 // 2
    K = dw.shape[0]
    H = CONV_HALO
    tr = _tile(T, 256, H)
    rb = tr // H

    def body(pc_ref, ph_ref, w_ref, wb_ref, lng_ref, lnb_ref, zc_ref, s_ref):
        first = pl.program_id(0) == 0
        pp = jnp.concatenate([jnp.where(first, 0.0, ph_ref[...].astype(F32)), pc_ref[...].astype(F32)], axis=0)
        zg = pp[:, :D] * _sigmoid(pp[:, D:])
        w = w_ref[...]
        acc = wb_ref[...] + w[K - 1:K, :] * _rows(zg, H, tr)
        for k in range(K - 1):
            acc = acc + w[k:k + 1, :] * _rows(zg, H - (K - 1 - k), tr)
        zc_ref[...] = acc
        mu = jnp.mean(acc, axis=-1, keepdims=True)
        xc = acc - mu
        rstd = lax.rsqrt(jnp.mean(xc * xc, axis=-1, keepdims=True) + NORM_EPS)
        ln = xc * rstd * lng_ref[...] + lnb_ref[...]
        s_ref[...] = (ln * _sigmoid(ln)).astype(BF16)

    in_specs = [pl.BlockSpec((tr, D2), lambda t: (t, 0)),
                pl.BlockSpec((H, D2), lambda t: (jnp.maximum(t * rb - 1, 0), 0)),
                pl.BlockSpec((K, D), lambda t: (0, 0)), _vec_spec(D), _vec_spec(D), _vec_spec(D)]
    row = pl.BlockSpec((tr, D), lambda t: (t, 0))
    return _call(body, name, (T // tr,), in_specs, (row, row),
                 (jax.ShapeDtypeStruct((T, D), F32), jax.ShapeDtypeStruct((T, D), BF16)), (p, p, dw, dwb, lng, lnb), side=side)


def _convmix_bwd(ds, zc, p, dw, lng, lnb, name, side=None):
    T, D = zc.shape
    D2 = 2 * D
    K = dw.shape[0]
    H = CONV_HALO
    tr = _tile(T, 256, H)
    rb = tr // H
    nt = T // tr
    ext = tr + H

    def body(dsc_ref, dsn_ref, zcc_ref, zcn_ref, pp_ref, pc_ref, w_ref, lng_ref, lnb_ref,
             dp_ref, dbin_ref, ddw_ref, ddwb_ref, dlng_ref, dlnb_ref):
        t = pl.program_id(0)
        first = t == 0
        last = t == nt - 1

        @pl.when(first)
        def _():
            for r in (dbin_ref, ddw_ref, ddwb_ref, dlng_ref, dlnb_ref):
                r[...] = jnp.zeros_like(r)

        dsv = jnp.concatenate([dsc_ref[...].astype(F32), jnp.where(last, 0.0, dsn_ref[...].astype(F32))], axis=0)
        zcv = jnp.concatenate([zcc_ref[...], zcn_ref[...]], axis=0)
        lg = lng_ref[...]
        mu = jnp.mean(zcv, axis=-1, keepdims=True)
        xc = zcv - mu
        rstd = lax.rsqrt(jnp.mean(xc * xc, axis=-1, keepdims=True) + NORM_EPS)
        xh = xc * rstd
        ln = xh * lg + lnb_ref[...]
        sg = _sigmoid(ln)
        dln = dsv * (sg * (1.0 + ln * (1.0 - sg)))
        dlnc = _rows(dln, 0, tr)
        dlnb_ref[...] += jnp.sum(dlnc, axis=0, keepdims=True)
        dlng_ref[...] += jnp.sum(dlnc * _rows(xh, 0, tr), axis=0, keepdims=True)
        dxh = dln * lg
        dzc = rstd * (dxh - jnp.mean(dxh, axis=-1, keepdims=True) - xh * jnp.mean(dxh * xh, axis=-1, keepdims=True))
        dzcc = _rows(dzc, 0, tr)
        ddwb_ref[...] += jnp.sum(dzcc, axis=0, keepdims=True)

        pv = jnp.concatenate([jnp.where(first, 0.0, pp_ref[...].astype(F32)), pc_ref[...].astype(F32)], axis=0)
        av, gv = pv[:, :D], pv[:, D:]
        sgg = _sigmoid(gv)
        zg = av * sgg
        w = w_ref[...]
        parts = []
        dzg = w[K - 1:K, :] * dzcc
        for k in range(K):
            parts.append(jnp.sum(dzcc * _rows(zg, H - (K - 1 - k), tr), axis=0, keepdims=True))
            if k < K - 1:
                dzg = dzg + w[k:k + 1, :] * _rows(dzc, K - 1 - k, tr)
        ddw_ref[...] += jnp.concatenate(parts, axis=0)
        ac, sc_ = _rows(av, H, tr), _rows(sgg, H, tr)
        dpa = dzg * sc_
        dpg = dzg * ac * sc_ * (1.0 - sc_)
        dpv = jnp.concatenate([dpa, dpg], axis=1)
        dp_ref[...] = dpv.astype(BF16)
        dbin_ref[...] += jnp.sum(dpv, axis=0, keepdims=True)

    def prev(t):
        return jnp.maximum(t * rb - 1, 0)

    def nxt(t):
        return jnp.minimum((t + 1) * rb, T // H - 1)

    in_specs = [pl.BlockSpec((tr, D), lambda t: (t, 0)), pl.BlockSpec((H, D), lambda t: (nxt(t), 0)),
                pl.BlockSpec((tr, D), lambda t: (t, 0)), pl.BlockSpec((H, D), lambda t: (nxt(t), 0)),
                pl.BlockSpec((H, D2), lambda t: (prev(t), 0)), pl.BlockSpec((tr, D2), lambda t: (t, 0)),
                pl.BlockSpec((K, D), lambda t: (0, 0)), _vec_spec(D), _vec_spec(D)]
    out_shape = (jax.ShapeDtypeStruct((T, D2), BF16), jax.ShapeDtypeStruct((1, D2), F32),
                 jax.ShapeDtypeStruct((K, D), F32), jax.ShapeDtypeStruct((1, D), F32),
                 jax.ShapeDtypeStruct((1, D), F32), jax.ShapeDtypeStruct((1, D), F32))
    out_specs = (pl.BlockSpec((tr, D2), lambda t: (t, 0)), _vec_spec(D2), pl.BlockSpec((K, D), lambda t: (0, 0)),
                 _vec_spec(D), _vec_spec(D), _vec_spec(D))
    return _call(body, name, (nt,), in_specs, out_specs, out_shape, (ds, ds, zc, zc, p, p, dw, lng, lnb), side=side)


_INV_SQRT2 = 1.0 / math.sqrt(2.0)
_INV_SQRT2PI = 1.0 / math.sqrt(2.0 * math.pi)


def _gelu(x):
    return 0.5 * x * (1.0 + lax.erf(x * _INV_SQRT2))


def _gelu_grad(x):
    return 0.5 * (1.0 + lax.erf(x * _INV_SQRT2)) + x * jnp.exp(-0.5 * x * x) * _INV_SQRT2PI


def _tril_mask(n, transposed=False):
    r = lax.broadcasted_iota(jnp.int32, (n, n), 0)
    c = lax.broadcasted_iota(jnp.int32, (n, n), 1)
    return (r <= c) if transposed else (r >= c)


def _sgu_fwd(p, lng, lnb, ws, bs_t, name):
    T, S2 = p.shape
    S = S2 // 2
    G, C, _ = ws.shape
    gd = S // G
    cpt = 2 if (T // C) % 2 == 0 else 1
    tr = C * cpt

    def body(p_ref, lng_ref, lnb_ref, ws_ref, bst_ref, m_ref):
        mask = _tril_mask(C)
        lg, lb = lng_ref[...], lnb_ref[...]
        bst = bst_ref[...]
        for ci in range(cpt):
            pc = p_ref[ci * C:(ci + 1) * C, :].astype(F32)
            z = _gelu(pc)
            u, v = z[:, :S], z[:, S:]
            mu = jnp.mean(v, axis=-1, keepdims=True)
            xc = v - mu
            rstd = lax.rsqrt(jnp.mean(xc * xc, axis=-1, keepdims=True) + NORM_EPS)
            vn = (xc * rstd * lg + lb).astype(BF16)
            outs = []
            for g in range(G):
                wm = jnp.where(mask, ws_ref[g], 0.0).astype(BF16)
                vs = _dot(wm, vn[:, g * gd:(g + 1) * gd], 1, 0) + bst[:, g:g + 1]
                outs.append(u[:, g * gd:(g + 1) * gd] * vs)
            m_ref[ci * C:(ci + 1) * C, :] = jnp.concatenate(outs, axis=1).astype(BF16)

    in_specs = [pl.BlockSpec((tr, S2), lambda t: (t, 0)), _vec_spec(S), _vec_spec(S),
                pl.BlockSpec((G, C, C), lambda t: (0, 0, 0)), pl.BlockSpec((C, G), lambda t: (0, 0))]
    return _call(body, name, (T // tr,), in_specs, pl.BlockSpec((tr, S), lambda t: (t, 0)),
                 jax.ShapeDtypeStruct((T, S), BF16), (p, lng, lnb, ws, bs_t))


def _sgu_bwd(p, dm, lng, lnb, ws, bs_t, name, side=None):
    T, S2 = p.shape
    S = S2 // 2
    G, C, _ = ws.shape
    gd = S // G
    nt = T // C

    def body(p_ref, dm_ref, lng_ref, lnb_ref, ws_ref, bst_ref, dp_ref, dbin_ref, dlng_ref, dlnb_ref, dws_ref, dbst_ref):
        @pl.when(pl.program_id(0) == 0)
        def _():
            for r in (dbin_ref, dlng_ref, dlnb_ref, dws_ref, dbst_ref):
                r[...] = jnp.zeros_like(r)

        mask = _tril_mask(C)
        lg, lb = lng_ref[...], lnb_ref[...]
        bst = bst_ref[...]
        pc = p_ref[...].astype(F32)
        dmv = dm_ref[...].astype(F32)
        z = _gelu(pc)
        u, v = z[:, :S], z[:, S:]
        mu = jnp.mean(v, axis=-1, keepdims=True)
        xc = v - mu
        rstd = lax.rsqrt(jnp.mean(xc * xc, axis=-1, keepdims=True) + NORM_EPS)
        vh = xc * rstd
        vn = (vh * lg + lb).astype(BF16)
        dus, dvns, dbcols = [], [], []
        for g in range(G):
            sl = slice(g * gd, (g + 1) * gd)
            wm = jnp.where(mask, ws_ref[g], 0.0).astype(BF16)
            vs = _dot(wm, vn[:, sl], 1, 0) + bst[:, g:g + 1]
            dmg = dmv[:, sl]
            dus.append(dmg * vs)
            dvs = dmg * u[:, sl]
            dbcols.append(jnp.sum(dvs, axis=-1, keepdims=True))
            dvsb = dvs.astype(BF16)
            dws_ref[g] += jnp.where(mask, _dot(dvsb, vn[:, sl], 1, 1), 0.0)
            dvns.append(_dot(wm, dvsb, 0, 0))
        dbst_ref[...] += jnp.concatenate(dbcols, axis=1)
        dvn = jnp.concatenate(dvns, axis=1)
        dlnb_ref[...] += jnp.sum(dvn, axis=0, keepdims=True)
        dlng_ref[...] += jnp.sum(dvn * vh, axis=0, keepdims=True)
        dvh = dvn * lg
        dv = rstd * (dvh - jnp.mean(dvh, axis=-1, keepdims=True) - vh * jnp.mean(dvh * vh, axis=-1, keepdims=True))
        dz = jnp.concatenate([jnp.concatenate(dus, axis=1), dv], axis=1)
        dpv = dz * _gelu_grad(pc)
        dp_ref[...] = dpv.astype(BF16)
        dbin_ref[...] += jnp.sum(dpv, axis=0, keepdims=True)

    in_specs = [pl.BlockSpec((C, S2), lambda t: (t, 0)), pl.BlockSpec((C, S), lambda t: (t, 0)),
                _vec_spec(S), _vec_spec(S), pl.BlockSpec((G, C, C), lambda t: (0, 0, 0)),
                pl.BlockSpec((C, G), lambda t: (0, 0))]
    out_shape = (jax.ShapeDtypeStruct((T, S2), BF16), jax.ShapeDtypeStruct((1, S2), F32),
                 jax.ShapeDtypeStruct((1, S), F32), jax.ShapeDtypeStruct((1, S), F32),
                 jax.ShapeDtypeStruct((G, C, C), F32), jax.ShapeDtypeStruct((C, G), F32))
    out_specs = (pl.BlockSpec((C, S2), lambda t: (t, 0)), _vec_spec(S2), _vec_spec(S), _vec_spec(S),
                 pl.BlockSpec((G, C, C), lambda t: (0, 0, 0)), pl.BlockSpec((C, G), lambda t: (0, 0)))
    return _call(body, name, (nt,), in_specs, out_specs, out_shape, (p, dm, lng, lnb, ws, bs_t), side=side)


def _alibi_slope(h, n_heads):
    return 2.0 ** (-8.0 * (h + 1) / n_heads)


def _attn_bias(n_heads):
    B = BLOCK
    qi = jnp.arange(B)[:, None]
    kj = jnp.arange(2 * B)[None, :]
    dist = qi + B - kj
    band = (dist >= 0) & (dist < B)
    slopes = jnp.array([_alibi_slope(h, n_heads) for h in range(n_heads)], F32)
    out = []
    for valid in (band & (kj >= B), band):
        b = jnp.where(valid[None], -slopes[:, None, None] * dist.astype(F32)[None], NEG_INF)
        b = jnp.where(kj[None] == 0, 0.0, b)
        out.append(b.reshape(n_heads // 2, 2, B, 2 * B).transpose(0, 2, 1, 3).reshape(n_heads // 2, B, 4 * B))
    return jnp.stack(out)


def _sink_rows(sinks):
    return jnp.pad(sinks.astype(F32)[:, None], ((0, 0), (0, BLOCK - 1)))


def _block_diag2(x):
    z = jnp.zeros_like(x)
    return jnp.concatenate([jnp.concatenate([x, z], axis=1), jnp.concatenate([z, x], axis=1)], axis=0)


def _attn_core(q_ref, kp_ref, kc_ref, vp_ref, vc_ref, sink_ref, bias_ref, n_heads):
    B, hd = BLOCK, HEAD_DIM
    ppg = n_heads // N_KV // 2
    pairs = range(n_heads // 2)
    row0 = lax.broadcasted_iota(jnp.int32, (2 * B, N_KV * hd), 0) == 0
    k2 = jnp.concatenate([kp_ref[...], kc_ref[...]], axis=0)
    v2 = jnp.concatenate([vp_ref[...], vc_ref[...]], axis=0)
    k2 = jnp.where(row0, jnp.zeros_like(k2), k2)
    v2 = jnp.where(row0, jnp.zeros_like(v2), v2)
    scale = hd ** -0.5
    kks = [_block_diag2(k2[:, kv * hd:(kv + 1) * hd] * scale) for kv in range(N_KV)]
    vvs = [_block_diag2(v2[:, kv * hd:(kv + 1) * hd]) for kv in range(N_KV)]
    first_rows = lax.broadcasted_iota(jnp.int32, (4 * B, 4 * hd), 0) < 2 * B
    first_lanes = lax.broadcasted_iota(jnp.int32, (4 * B, 4 * hd), 1) < 2 * hd
    ones = (first_rows == first_lanes).astype(BF16)
    vxs = [jnp.concatenate([vv, ones], axis=1) for vv in vvs]
    ss = [_dot(q_ref[:, pr * 2 * hd:(pr + 1) * 2 * hd], kks[pr // ppg], 1, 1) + bias_ref[pr] for pr in pairs]
    halves = []
    for h in range(n_heads):
        s = ss[h // 2][:, (h % 2) * 2 * B:(h % 2 + 1) * 2 * B]
        halves.append(jnp.concatenate([s[:, :B] + sink_ref[h:h + 1, :], s[:, B:]], axis=1))
    mxs = [jnp.max(s, axis=-1, keepdims=True) for s in halves]
    es = [jnp.exp(s - m) for s, m in zip(halves, mxs)]
    rs = [_dot(jnp.concatenate([es[2 * pr], es[2 * pr + 1]], axis=1).astype(BF16), vxs[pr // ppg], 1, 0) for pr in pairs]
    return kks, vvs, es, rs


def _attn_fwd(qkv, sink_rows, bias, n_heads, name, side=None):
    T = qkv.shape[0]
    B = BLOCK
    hd = HEAD_DIM
    HQ = n_heads * hd
    KVW = N_KV * hd
    assert (n_heads // N_KV) % 2 == 0, "heads are processed in pairs that share a key/value head"
    nb = T // B
    kcol = HQ // KVW

    def body(q_ref, kp_ref, kc_ref, vp_ref, vc_ref, sink_ref, bias_ref, o_ref):
        _, _, _, rs = _attn_core(q_ref, kp_ref, kc_ref, vp_ref, vc_ref, sink_ref, bias_ref, n_heads)
        low = lax.broadcasted_iota(jnp.int32, (B, 2 * hd), 1) < hd
        for pr, r in enumerate(rs):
            inv = jnp.where(low, 1.0 / r[:, 2 * hd:4 * hd], 1.0 / r[:, 4 * hd:])
            o_ref[:, pr * 2 * hd:(pr + 1) * 2 * hd] = (r[:, :2 * hd] * inv).astype(BF16)

    def prev(n):
        return jnp.maximum(n - 1, 0)

    in_specs = [pl.BlockSpec((B, HQ), lambda n: (n, 0)),
                pl.BlockSpec((B, KVW), lambda n: (prev(n), kcol)), pl.BlockSpec((B, KVW), lambda n: (n, kcol)),
                pl.BlockSpec((B, KVW), lambda n: (prev(n), kcol + 1)), pl.BlockSpec((B, KVW), lambda n: (n, kcol + 1)),
                pl.BlockSpec((n_heads, B), lambda n: (0, 0)),
                pl.BlockSpec((None, n_heads // 2, B, 4 * B), lambda n: (jnp.minimum(n, 1), 0, 0, 0))]
    return _call(body, name, (nb,), in_specs, pl.BlockSpec((B, HQ), lambda n: (n, 0)),
                 jax.ShapeDtypeStruct((T, HQ), BF16), (qkv, qkv, qkv, qkv, qkv, sink_rows, bias), side=side)


def _attn_bwd(qkv, do, sink_rows, bias, n_heads, name, side=None):
    T = qkv.shape[0]
    B = BLOCK
    hd = HEAD_DIM
    HQ = n_heads * hd
    KVW = N_KV * hd
    group = n_heads // N_KV
    assert group % 2 == 0, "heads are processed in pairs that share a key/value head"
    ppg = group // 2
    nb = T // B
    kcol = HQ // KVW
    scale = hd ** -0.5

    def body(q_ref, kp_ref, kc_ref, vp_ref, vc_ref, do_ref, sink_ref, bias_ref,
             dq_ref, dk_ref, dv_ref, dbq_ref, dbk_ref, dbv_ref, dsink_ref, ck_ref, cv_ref):
        n = pl.program_id(0)
        first = n == 0

        @pl.when(first)
        def _():
            for r in (dbq_ref, dbk_ref, dbv_ref, dsink_ref, ck_ref, cv_ref):
                r[...] = jnp.zeros_like(r)

        @pl.when(n < nb)
        def _():
            pairs = range(n_heads // 2)
            heads = range(n_heads)
            lanes = [slice(pr * 2 * hd, (pr + 1) * 2 * hd) for pr in pairs]
            kks, vvs, es, rs = _attn_core(q_ref, kp_ref, kc_ref, vp_ref, vc_ref, sink_ref, bias_ref, n_heads)
            low = lax.broadcasted_iota(jnp.int32, (B, 2 * hd), 1) < hd
            slot0 = lax.broadcasted_iota(jnp.int32, (B, B), 1) == 0
            lane_head = lax.broadcasted_iota(jnp.int32, (2 * hd, 4 * hd), 0) < hd
            out_head = lax.broadcasted_iota(jnp.int32, (2 * hd, 4 * hd), 1) < 2 * hd
            ones2 = (lane_head == out_head).astype(BF16)
            invs = [1.0 / rs[h // 2][:, (2 + 2 * (h % 2)) * hd:(4 + 2 * (h % 2)) * hd] for h in heads]
            dops = [do_ref[:, lanes[pr]] for pr in pairs]
            qps = [q_ref[:, lanes[pr]] for pr in pairs]
            dos = [dops[pr].astype(F32) * rs[pr][:, :2 * hd] * jnp.where(low, invs[2 * pr], invs[2 * pr + 1])
                   for pr in pairs]
            his = [t.astype(BF16) for t in dos]
            los = [(t - hi.astype(F32)).astype(BF16) for t, hi in zip(dos, his)]
            dsums = [_dot(hi, ones2, 1, 0) + _dot(lo, ones2, 1, 0) for hi, lo in zip(his, los)]
            dps = [_dot(dops[pr], vvs[pr // ppg], 1, 1) for pr in pairs]
            wide = lambda a: jnp.concatenate([a, a], axis=1)
            p_s = [es[h] * wide(invs[h]) for h in heads]
            ds_s = [p_s[h] * (dps[h // 2][:, (h % 2) * 2 * B:(h % 2 + 1) * 2 * B]
                              - wide(dsums[h // 2][:, (h % 2) * 2 * hd:(h % 2 + 1) * 2 * hd])) for h in heads]
            for h in heads:
                dsink_ref[h:h + 1, :] += jnp.sum(ds_s[h][:, :B], axis=0, keepdims=True)

            def without_slot0(a):
                return jnp.concatenate([jnp.where(slot0, 0.0, a[:, :B]), a[:, B:]], axis=1).astype(BF16)

            ds_b = [without_slot0(a) for a in ds_s]
            p_b = [without_slot0(a) for a in p_s]
            dqs = [_dot(jnp.concatenate([ds_b[2 * pr], ds_b[2 * pr + 1]], axis=1), kks[pr // ppg], 1, 0) for pr in pairs]
            for pr in pairs:
                dq_ref[:, lanes[pr]] = dqs[pr].astype(BF16)
                dbq_ref[:, lanes[pr]] += jnp.sum(dqs[pr], axis=0, keepdims=True)
            keeps = [low, jnp.logical_not(low)]
            qms = [jnp.where(keeps[h % 2], qps[h // 2], jnp.zeros_like(qps[0])) for h in heads]
            dms = [jnp.where(keeps[h % 2], dops[h // 2], jnp.zeros_like(dops[0])) for h in heads]
            dkh = [_dot(ds_b[h], qms[h], 0, 0) for h in heads]
            dvh = [_dot(p_b[h], dms[h], 0, 0) for h in heads]
            dks, dvs = [], []
            for kv in range(N_KV):
                tk = functools.reduce(jnp.add, dkh[kv * group:(kv + 1) * group])
                tv = functools.reduce(jnp.add, dvh[kv * group:(kv + 1) * group])
                dks.append((tk[:, :hd] + tk[:, hd:]) * scale)
                dvs.append(tv[:, :hd] + tv[:, hd:])
            dk2 = jnp.concatenate(dks, axis=1)
            dv2 = jnp.concatenate(dvs, axis=1)
            dbk_ref[...] += jnp.sum(dk2, axis=0, keepdims=True)
            dbv_ref[...] += jnp.sum(dv2, axis=0, keepdims=True)
            dk_ref[...] = (ck_ref[...] + dk2[:B]).astype(BF16)
            dv_ref[...] = (cv_ref[...] + dv2[:B]).astype(BF16)
            ck_ref[...] = dk2[B:]
            cv_ref[...] = dv2[B:]

        @pl.when(n == nb)
        def _():
            dk_ref[...] = ck_ref[...].astype(BF16)
            dv_ref[...] = cv_ref[...].astype(BF16)

    def cur(n):
        return jnp.minimum(n, nb - 1)

    def prev(n):
        return jnp.maximum(cur(n) - 1, 0)

    def outp(n):
        return jnp.maximum(n - 1, 0)

    in_specs = [pl.BlockSpec((B, HQ), lambda n: (cur(n), 0)),
                pl.BlockSpec((B, KVW), lambda n: (prev(n), kcol)), pl.BlockSpec((B, KVW), lambda n: (cur(n), kcol)),
                pl.BlockSpec((B, KVW), lambda n: (prev(n), kcol + 1)), pl.BlockSpec((B, KVW), lambda n: (cur(n), kcol + 1)),
                pl.BlockSpec((B, HQ), lambda n: (cur(n), 0)), pl.BlockSpec((n_heads, B), lambda n: (0, 0)),
                pl.BlockSpec((None, n_heads // 2, B, 4 * B), lambda n: (jnp.minimum(n, 1), 0, 0, 0))]
    out_shape = (jax.ShapeDtypeStruct((T, HQ), BF16), jax.ShapeDtypeStruct((T, KVW), BF16),
                 jax.ShapeDtypeStruct((T, KVW), BF16), jax.ShapeDtypeStruct((1, HQ), F32),
                 jax.ShapeDtypeStruct((1, KVW), F32), jax.ShapeDtypeStruct((1, KVW), F32),
                 jax.ShapeDtypeStruct((n_heads, B), F32))
    out_specs = (pl.BlockSpec((B, HQ), lambda n: (cur(n), 0)), pl.BlockSpec((B, KVW), lambda n: (outp(n), 0)),
                 pl.BlockSpec((B, KVW), lambda n: (outp(n), 0)), _vec_spec(HQ), _vec_spec(KVW), _vec_spec(KVW),
                 pl.BlockSpec((n_heads, B), lambda n: (0, 0)))
    return _call(body, name, (nb + 1,), in_specs, out_specs, out_shape, (qkv, qkv, qkv, qkv, qkv, do, sink_rows, bias),
                 scratch=[pltpu.VMEM((B, KVW), F32), pltpu.VMEM((B, KVW), F32)], side=side)


def _sum8(r, name):
    _, R, C = r.shape
    tr = _tile(R, 512, 16)

    def body(r_ref, o_ref):
        acc = r_ref[0].astype(F32)
        for d in range(1, N_DEV):
            acc = acc + r_ref[d].astype(F32)
        o_ref[...] = acc

    return _call(body, name, (R // tr,), [pl.BlockSpec((N_DEV, tr, C), lambda i: (0, i, 0))],
                 pl.BlockSpec((tr, C), lambda i: (i, 0)), jax.ShapeDtypeStruct((R, C), F32), (r,))


def _adamw(w, g, m, v, name):
    R, C = w.shape
    tr = _tile(R, 512, 8)
    c1 = 1.0 - ADAM_B1 ** ADAM_STEP
    c2 = 1.0 - ADAM_B2 ** ADAM_STEP

    def body(w_ref, g_ref, m_ref, v_ref, d_ref, nm_ref, nv_ref):
        gv = g_ref[...]
        nm = ADAM_B1 * m_ref[...] + (1.0 - ADAM_B1) * gv
        nv = ADAM_B2 * v_ref[...] + (1.0 - ADAM_B2) * (gv * gv)
        nm_ref[...] = nm
        nv_ref[...] = nv
        d_ref[...] = -ADAM_LR * ((nm / c1) / (jnp.sqrt(nv / c2) + ADAM_EPS) + ADAM_WD * w_ref[...])

    spec = pl.BlockSpec((tr, C), lambda i: (i, 0))
    shp = jax.ShapeDtypeStruct((R, C), F32)
    return _call(body, name, (R // tr,), [spec] * 4, (spec,) * 3, (shp, shp, shp), (w, g, m, v))


def _adamw_nd(w, g, m, v, name):
    shape = w.shape
    c = shape[-1]
    f = lambda a: a.reshape(-1, c)
    d, nm, nv = _adamw(f(w), f(g), f(m), f(v), name)
    return d.reshape(shape), nm.reshape(shape), nv.reshape(shape)


def _pack(arrs, width):
    flat = jnp.concatenate([a.reshape(-1).astype(F32) for a in arrs])
    n = flat.shape[0]
    quantum = 8 * width
    total = -(-n // quantum) * quantum
    return jnp.pad(flat, (0, total - n)).reshape(-1, width)


def _unpack(flat, shapes):
    out, off = [], 0
    for s in shapes:
        n = math.prod(s)
        out.append(flat[off:off + n].reshape(s))
        off += n
    return out


def kernel(x, c, norm1_g, norm2_g, ada_w, ada_b, attn_wqkv, attn_bqkv, attn_sinks, attn_wo, attn_bo, conv_w_in, conv_b_in, conv_dw, conv_dw_b, conv_ln_g, conv_ln_b, conv_w_out, conv_b_out, sgu_w_in, sgu_b_in, sgu_ln_g, sgu_ln_b, sgu_ws, sgu_bs, sgu_w_out, sgu_b_out, ffn_w_in, ffn_dw, ffn_dw_b, ffn_w_out, final_g, loss_target, m_norm1_g, m_norm2_g, m_ada_w, m_ada_b, m_attn_wqkv, m_attn_bqkv, m_attn_sinks, m_attn_wo, m_attn_bo, m_conv_w_in, m_conv_b_in, m_conv_dw, m_conv_dw_b, m_conv_ln_g, m_conv_ln_b, m_conv_w_out, m_conv_b_out, m_sgu_w_in, m_sgu_b_in, m_sgu_ln_g, m_sgu_ln_b, m_sgu_ws, m_sgu_bs, m_sgu_w_out, m_sgu_b_out, m_ffn_w_in, m_ffn_dw, m_ffn_dw_b, m_ffn_w_out, m_final_g, v_norm1_g, v_norm2_g, v_ada_w, v_ada_b, v_attn_wqkv, v_attn_bqkv, v_attn_sinks, v_attn_wo, v_attn_bo, v_conv_w_in, v_conv_b_in, v_conv_dw, v_conv_dw_b, v_conv_ln_g, v_conv_ln_b, v_conv_w_out, v_conv_b_out, v_sgu_w_in, v_sgu_b_in, v_sgu_ln_g, v_sgu_ln_b, v_sgu_ws, v_sgu_bs, v_sgu_w_out, v_sgu_b_out, v_ffn_w_in, v_ffn_dw, v_ffn_dw_b, v_ffn_w_out, v_final_g):
    W = dict(norm1_g=norm1_g, norm2_g=norm2_g, ada_w=ada_w, ada_b=ada_b, attn_wqkv=attn_wqkv, attn_bqkv=attn_bqkv, attn_sinks=attn_sinks, attn_wo=attn_wo, attn_bo=attn_bo, conv_w_in=conv_w_in, conv_b_in=conv_b_in, conv_dw=conv_dw, conv_dw_b=conv_dw_b, conv_ln_g=conv_ln_g, conv_ln_b=conv_ln_b, conv_w_out=conv_w_out, conv_b_out=conv_b_out, sgu_w_in=sgu_w_in, sgu_b_in=sgu_b_in, sgu_ln_g=sgu_ln_g, sgu_ln_b=sgu_ln_b, sgu_ws=sgu_ws, sgu_bs=sgu_bs, sgu_w_out=sgu_w_out, sgu_b_out=sgu_b_out, ffn_w_in=ffn_w_in, ffn_dw=ffn_dw, ffn_dw_b=ffn_dw_b, ffn_w_out=ffn_w_out, final_g=final_g)
    MOM = dict(norm1_g=m_norm1_g, norm2_g=m_norm2_g, ada_w=m_ada_w, ada_b=m_ada_b, attn_wqkv=m_attn_wqkv, attn_bqkv=m_attn_bqkv, attn_sinks=m_attn_sinks, attn_wo=m_attn_wo, attn_bo=m_attn_bo, conv_w_in=m_conv_w_in, conv_b_in=m_conv_b_in, conv_dw=m_conv_dw, conv_dw_b=m_conv_dw_b, conv_ln_g=m_conv_ln_g, conv_ln_b=m_conv_ln_b, conv_w_out=m_conv_w_out, conv_b_out=m_conv_b_out, sgu_w_in=m_sgu_w_in, sgu_b_in=m_sgu_b_in, sgu_ln_g=m_sgu_ln_g, sgu_ln_b=m_sgu_ln_b, sgu_ws=m_sgu_ws, sgu_bs=m_sgu_bs, sgu_w_out=m_sgu_w_out, sgu_b_out=m_sgu_b_out, ffn_w_in=m_ffn_w_in, ffn_dw=m_ffn_dw, ffn_dw_b=m_ffn_dw_b, ffn_w_out=m_ffn_w_out, final_g=m_final_g)
    VAR = dict(norm1_g=v_norm1_g, norm2_g=v_norm2_g, ada_w=v_ada_w, ada_b=v_ada_b, attn_wqkv=v_attn_wqkv, attn_bqkv=v_attn_bqkv, attn_sinks=v_attn_sinks, attn_wo=v_attn_wo, attn_bo=v_attn_bo, conv_w_in=v_conv_w_in, conv_b_in=v_conv_b_in, conv_dw=v_conv_dw, conv_dw_b=v_conv_dw_b, conv_ln_g=v_conv_ln_g, conv_ln_b=v_conv_ln_b, conv_w_out=v_conv_w_out, conv_b_out=v_conv_b_out, sgu_w_in=v_sgu_w_in, sgu_b_in=v_sgu_b_in, sgu_ln_g=v_sgu_ln_g, sgu_ln_b=v_sgu_ln_b, sgu_ws=v_sgu_ws, sgu_bs=v_sgu_bs, sgu_w_out=v_sgu_w_out, sgu_b_out=v_sgu_b_out, ffn_w_in=v_ffn_w_in, ffn_dw=v_ffn_dw, ffn_dw_b=v_ffn_dw_b, ffn_w_out=v_ffn_w_out, final_g=v_final_g)
    ORDER = list(W)

    _, T, D = x.shape
    depth = norm1_g.shape[0]
    n_heads = D // HEAD_DIM
    me = 4 * lax.axis_index("x") + 2 * lax.axis_index("y") + lax.axis_index("c")
    x0 = x.reshape(T, D)
    tgt = loss_target.reshape(T, D)

    small_sharded = ["attn_bqkv", "attn_bo", "conv_dw", "sgu_b_in", "sgu_ln_g", "sgu_ln_b", "sgu_b_out", "ffn_dw"]
    s_in = [c] + [W[n] for n in small_sharded]
    s_shapes = [a.shape for a in s_in]
    gathered_small = _all_gather(_pack(s_in, 128), "gather_small").reshape(N_DEV, -1)
    s_offs = [sum(math.prod(s) for s in s_shapes[:i]) for i in range(len(s_shapes))]

    def full(idx):
        shp = s_shapes[idx]
        a = gathered_small[:, s_offs[idx]:s_offs[idx] + math.prod(shp)].reshape((N_DEV,) + shp)
        return jnp.moveaxis(a, 0, -2).reshape(shp[:-1] + (N_DEV * shp[-1],))

    c_all = full(0).reshape(N_DEV, D)
    F_small = {n: full(1 + i) for i, n in enumerate(small_sharded)}
    attn_bias = _attn_bias(n_heads)

    c_act = c_all * jax.nn.sigmoid(c_all)
    c_pad = jnp.pad(c_act, ((0, 8), (0, 0))).astype(BF16)
    n_ada = ada_w.shape[-1]
    ada_cols = lax.dynamic_slice_in_dim(ada_b, me * n_ada, n_ada, axis=1)
    mod_loc = jnp.stack([_mm(c_pad, ada_w[i].astype(BF16), "nn", "ada_mod", bias=ada_cols[i:i + 1])
                         for i in range(depth)])
    mod_all = _all_gather(mod_loc, "gather_mod")
    mod_mine = lax.dynamic_index_in_dim(mod_all, me, axis=2, keepdims=False)
    mod = jnp.transpose(mod_mine, (1, 0, 2)).reshape(depth, 6, 1, D)

    mixer_names = {0: ("attn_wqkv", "attn_wo"), 1: ("conv_w_in", "conv_w_out"), 2: ("sgu_w_in", "sgu_w_out")}

    def piece(part, i, which):
        w_in, w_out = ("ffn_w_in", "ffn_w_out") if part == "ffn" else mixer_names[i % 3]
        l = i if part == "ffn" else i // 3
        return (w_in, l, W[w_in].shape[-1]) if which == "in" else (w_out, l, W[w_out].shape[1])

    def both(part, i):
        return [(part, i, "in"), (part, i, "out")] if 0 <= i < depth else []

    def local_weight(key):
        n, l, _ = piece(*key)
        return (W[n][l].T if key[2] == "in" else W[n][l]).astype(BF16)

    WF = {}

    def land_weights(keys, gathered):
        for key, g in zip(keys, gathered):
            n, l, r = piece(*key)
            WF[(n, l)] = g.reshape(N_DEV * r, D)

    def with_gather(keys, run):
        if not keys:
            return run(None)
        *outs, landed = run(("gather", [local_weight(k) for k in keys]))
        land_weights(keys, landed)
        return outs[0] if len(outs) == 1 else tuple(outs)

    def vec(a):
        return a.reshape(1, -1)

    land_weights(both("mix", 0), _exchange("gather", [local_weight(k) for k in both("mix", 0)], "gather_first"))

    saved = []
    xs, y_prev, gate_prev = x0, None, None
    for i in range(depth):
        sh1, sc1, g1, sh2, sc2, g2 = [mod[i, k] for k in range(6)]
        kind, j = i % 3, i // 3
        st = dict(kind=kind, j=j)
        if y_prev is None:
            h1 = _norm_mod_fwd(xs, vec(norm1_g[i]), sc1, sh1, "norm1_fwd")
        else:
            xs, h1 = _norm_mod_fwd(xs, vec(norm1_g[i]), sc1, sh1, "norm1_fwd", y=y_prev, gate=gate_prev)
        st.update(x_in=xs, h1=h1)
        first_ffn_in = [("ffn", 0, "in")] if i == 0 else []
        if kind == 0:
            qkv = _mm(h1, WF[("attn_wqkv", j)], "nt", "attn_qkv", out_dtype=BF16, bias=vec(F_small["attn_bqkv"][j]))
            o = with_gather(first_ffn_in, lambda side: _attn_fwd(qkv, _sink_rows(attn_sinks[j]), attn_bias, n_heads,
                                                                 "attn_fwd", side=side))
            y1 = _mm(o, WF[("attn_wo", j)], "nn", "attn_out", bias=vec(F_small["attn_bo"][j]))
            st.update(qkv=qkv, o=o)
        elif kind == 1:
            p = _mm(h1, WF[("conv_w_in", j)], "nt", "conv_in", out_dtype=BF16, bias=vec(conv_b_in[j]))
            zc, s = with_gather(first_ffn_in, lambda side: _convmix_fwd(
                p, F_small["conv_dw"][j], vec(conv_dw_b[j]), vec(conv_ln_g[j]), vec(conv_ln_b[j]), "convmix_fwd", side=side))
            y1 = _mm(s, WF[("conv_w_out", j)], "nn", "conv_out", bias=vec(conv_b_out[j]))
            st.update(p=p, zc=zc, s=s)
        else:
            p = _mm(h1, WF[("sgu_w_in", j)], "nt", "sgu_in", out_dtype=BF16, bias=vec(F_small["sgu_b_in"][j]))
            mm_ = _sgu_fwd(p, vec(F_small["sgu_ln_g"][j]), vec(F_small["sgu_ln_b"][j]), sgu_ws[j], sgu_bs[j].T, "sgu_fwd")
            y1 = _mm(mm_, WF[("sgu_w_out", j)], "nn", "sgu_out", bias=vec(F_small["sgu_b_out"][j]))
            st.update(p=p, m=mm_)
        xs, h2 = _norm_mod_fwd(xs, vec(norm2_g[i]), sc2, sh2, "norm2_fwd", y=y1, gate=g1)
        nxt = i + 1
        z = with_gather(both("mix", nxt) + ([("ffn", 0, "out")] if i == 0 else []),
                        lambda side: _mm(h2, WF[("ffn_w_in", i)], "nt", "ffn_in", out_dtype=BF16, side=side))
        a = with_gather(both("ffn", nxt)[:1],
                        lambda side: _ffn_mid_fwd(z, F_small["ffn_dw"][i], vec(ffn_dw_b[i]), "ffn_mid_fwd", side=side))
        y2 = with_gather(both("ffn", nxt)[1:], lambda side: _mm(a, WF[("ffn_w_out", i)], "nn", "ffn_out", side=side))
        st.update(y1=y1, x_mid=xs, h2=h2, z=z, a=a, y2=y2)
        saved.append(st)
        y_prev, gate_prev = y2, g2

    dx, d_final_g, loss_row = _final_loss(xs, y_prev, gate_prev, tgt, vec(final_g), "final_loss")
    loss = lax.psum(loss_row[0, 0], ("x", "y", "c"))

    G = {n: [None] * W[n].shape[0] for n in ORDER if n != "final_g"}
    GW = {}
    recv = {}
    dmod = [None] * depth

    def with_scatter(keys, run):
        if not keys:
            return run(None)
        sends = [GW[piece(*k)[:2]].reshape(N_DEV, piece(*k)[2], D) for k in keys]
        *outs, landed = run(("scatter", sends))
        recv.update(zip(keys, landed))
        return outs[0] if len(outs) == 1 else tuple(outs)

    for i in reversed(range(depth)):
        st = saved[i]
        sh1, sc1, g1, sh2, sc2, g2 = [mod[i, k] for k in range(6)]
        kind, j = st["kind"], st["j"]
        last_out = [("mix", 0, "out")] if i == 0 else []
        last_in = [("mix", 0, "in")] if i == 0 else []
        dy2, dg2, _ = _gate_bwd(dx, st["y2"], g2, "gate_bwd")
        GW[("ffn_w_out", i)] = _mm(st["a"], dy2, "tn", "ffn_out_dw", out_dtype=BF16)
        da = _mm(dy2, WF[("ffn_w_out", i)], "nt", "ffn_out_dx", out_dtype=BF16)
        dzg, dzu, ddw, ddwb = with_scatter(both("mix", i + 1) + [("ffn", i, "out")], lambda side: _ffn_mid_bwd(
            st["z"], da, F_small["ffn_dw"][i], vec(ffn_dw_b[i]), "ffn_mid_bwd", side=side))
        G["ffn_dw"][i], G["ffn_dw_b"][i] = ddw, ddwb[0]
        GW[("ffn_w_in", i)] = _mm(dzg, st["h2"], "tn", "ffn_in_dw", out_dtype=BF16, a2=dzu)
        dh2 = with_scatter([("ffn", i, "in")],
                           lambda side: _mm(dzg, WF[("ffn_w_in", i)], "nn", "ffn_in_dx", a2=dzu, side=side))
        dx, dn2, dsc2, dsh2 = _norm_mod_bwd(dh2, st["x_mid"], vec(norm2_g[i]), sc2, sh2, dx, "norm_bwd")
        G["norm2_g"][i] = dn2[0]
        dy1, dg1, dbo = _gate_bwd(dx, st["y1"], g1, "gate_bwd")
        if kind == 0:
            G["attn_bo"][j] = dbo[0]
            GW[("attn_wo", j)] = _mm(st["o"], dy1, "tn", "attn_out_dw", out_dtype=BF16)
            do = _mm(dy1, WF[("attn_wo", j)], "nt", "attn_out_dx", out_dtype=BF16)
            dq, dk, dv, dbq, dbk, dbv, dsk = with_scatter(last_out, lambda side: _attn_bwd(
                st["qkv"], do, _sink_rows(attn_sinks[j]), attn_bias, n_heads, "attn_bwd", side=side))
            dqkv = jnp.concatenate([dq, dk, dv], axis=1)
            G["attn_bqkv"][j] = jnp.concatenate([dbq, dbk, dbv], axis=1)[0]
            G["attn_sinks"][j] = dsk[:, 0]
            GW[("attn_wqkv", j)] = _mm(dqkv, st["h1"], "tn", "attn_qkv_dw", out_dtype=BF16)
            dh1 = with_scatter(last_in, lambda side: _mm(dqkv, WF[("attn_wqkv", j)], "nn", "attn_qkv_dx", side=side))
        elif kind == 1:
            G["conv_b_out"][j] = dbo[0]
            GW[("conv_w_out", j)] = _mm(st["s"], dy1, "tn", "conv_out_dw", out_dtype=BF16)
            ds = _mm(dy1, WF[("conv_w_out", j)], "nt", "conv_out_dx")
            dp, dbin, ddw, ddwb, dlng, dlnb = with_scatter(last_out, lambda side: _convmix_bwd(
                ds, st["zc"], st["p"], F_small["conv_dw"][j], vec(conv_ln_g[j]), vec(conv_ln_b[j]), "convmix_bwd",
                side=side))
            G["conv_b_in"][j], G["conv_dw"][j], G["conv_dw_b"][j] = dbin[0], ddw, ddwb[0]
            G["conv_ln_g"][j], G["conv_ln_b"][j] = dlng[0], dlnb[0]
            GW[("conv_w_in", j)] = _mm(dp, st["h1"], "tn", "conv_in_dw", out_dtype=BF16)
            dh1 = with_scatter(last_in, lambda side: _mm(dp, WF[("conv_w_in", j)], "nn", "conv_in_dx", side=side))
        else:
            G["sgu_b_out"][j] = dbo[0]
            GW[("sgu_w_out", j)] = _mm(st["m"], dy1, "tn", "sgu_out_dw", out_dtype=BF16)
            dm = _mm(dy1, WF[("sgu_w_out", j)], "nt", "sgu_out_dx")
            dp, dbin, dlng, dlnb, dws, dbst = with_scatter(last_out, lambda side: _sgu_bwd(
                st["p"], dm, vec(F_small["sgu_ln_g"][j]), vec(F_small["sgu_ln_b"][j]), sgu_ws[j], sgu_bs[j].T, "sgu_bwd",
                side=side))
            G["sgu_b_in"][j], G["sgu_ln_g"][j], G["sgu_ln_b"][j] = dbin[0], dlng[0], dlnb[0]
            G["sgu_ws"][j], G["sgu_bs"][j] = dws, dbst.T
            GW[("sgu_w_in", j)] = _mm(dp, st["h1"], "tn", "sgu_in_dw", out_dtype=BF16)
            dh1 = with_scatter(last_in, lambda side: _mm(dp, WF[("sgu_w_in", j)], "nn", "sgu_in_dx", side=side))
        dx, dn1, dsc1, dsh1 = _norm_mod_bwd(dh1, st["x_in"], vec(norm1_g[i]), sc1, sh1, dx, "norm_bwd")
        G["norm1_g"][i] = dn1[0]
        dmod[i] = jnp.concatenate([dsh1, dsc1, dg1, dsh2, dsc2, dg2], axis=1)[0]
    grad_x = dx.reshape(x.shape)

    small_names = [n for n in ORDER if n not in
                   ("ada_w", "ada_b", "attn_wqkv", "attn_wo", "conv_w_in", "conv_w_out", "sgu_w_in", "sgu_w_out",
                    "ffn_w_in", "ffn_w_out", "final_g")]
    g_small = {n: jnp.stack(G[n]) for n in small_names}
    g_small["final_g"] = d_final_g[0]
    g_small["ada_b"] = jnp.stack(dmod)
    names16 = ["ffn_dw", "conv_dw", "sgu_ws", "ffn_dw_b", "ada_b"]
    names32 = [n for n in g_small if n not in names16]
    shapes16, shapes32 = [g_small[n].shape for n in names16], [g_small[n].shape for n in names32]
    gathered32, gathered16 = _exchange("gather", [_pack([g_small[n] for n in names32], 1024),
                                                  _pack([g_small[n] for n in names16], 1024).astype(BF16)],
                                       "gather_small_grads")
    gsum = dict(zip(names32, _unpack(_sum8(gathered32, "sum_small_grads").reshape(-1), shapes32)))
    gsum.update(zip(names16, _unpack(_sum8(gathered16, "sum_small_grads").reshape(-1), shapes16)))

    def local_shard(n, a):
        if n in small_sharded:
            w = W[n].shape[-1]
            return lax.dynamic_slice_in_dim(a, me * w, w, axis=a.ndim - 1)
        return a

    gsum = {n: local_shard(n, a) for n, a in gsum.items()}

    off16 = sum(math.prod(s) for s in shapes16[:-1])
    dmod_all = gathered16.reshape(N_DEV, -1)[:, off16:off16 + math.prod(shapes16[-1])].reshape((N_DEV,) + shapes16[-1])
    dmod_cols = lax.dynamic_slice_in_dim(dmod_all, me * n_ada, n_ada, axis=2)
    dmod_pad = jnp.pad(dmod_cols, ((0, 8), (0, 0), (0, 0)))
    g_ada_w = jnp.stack([_mm(c_pad, dmod_pad[:, i, :], "tn", "ada_dw") for i in range(depth)])

    big = {}
    for i in range(depth):
        for key in both("mix", i) + both("ffn", i):
            n, l, _ = piece(*key)
            gp = _sum8(recv[key], "sum_weight_grads")
            big.setdefault(n, {})[l] = gp.T if key[2] == "in" else gp
    grads = {n: jnp.stack([v[l] for l in range(len(v))]) for n, v in big.items()}
    grads["ada_w"] = g_ada_w
    grads.update(gsum)

    delta, new_m, new_v = {}, {}, {}
    big_names = ["ada_w", "attn_wqkv", "attn_wo", "conv_w_in", "conv_w_out", "sgu_w_in", "sgu_w_out", "ffn_w_in",
                 "ffn_w_out"]
    for n in big_names:
        delta[n], new_m[n], new_v[n] = _adamw_nd(W[n], grads[n], MOM[n], VAR[n], "adamw_" + n)
    rest = [n for n in ORDER if n not in big_names]
    rest_shapes = [W[n].shape for n in rest]
    pk = lambda d: _pack([d[n] for n in rest], 128)
    d_s, m_s, v_s = _adamw(pk(W), pk(grads), pk(MOM), pk(VAR), "adamw_small")
    for n, a, b_, c_ in zip(rest, _unpack(d_s.reshape(-1), rest_shapes), _unpack(m_s.reshape(-1), rest_shapes),
                            _unpack(v_s.reshape(-1), rest_shapes)):
        delta[n], new_m[n], new_v[n] = a, b_, c_

    return (loss, grad_x, *[grads[n] for n in ORDER], *[delta[n] for n in ORDER],
            *[new_m[n] for n in ORDER], *[new_v[n] for n in ORDER])
```

```python
import functools
import math

import jax
import jax.numpy as jnp
from jax import lax
from jax.experimental import pallas as pl
from jax.experimental.pallas import tpu as pltpu

F32 = jnp.float32
BF16 = jnp.bfloat16

N_DEV = 8
HEAD_DIM = 64
N_KV = 4
BLOCK = 128
NORM_EPS = 1e-6
NEG_INF = -1e30
ADAM_LR = 0.001
ADAM_B1 = 0.9
ADAM_B2 = 0.999
ADAM_EPS = 1e-08
ADAM_WD = 0.01
ADAM_STEP = 10
V7X_VMEM_LIMIT = 56 * 1024 * 1024
MESH = pl.DeviceIdType.MESH


def _cparams():
    return pltpu.CompilerParams(vmem_limit_bytes=V7X_VMEM_LIMIT)


def _tile(n, cap, align):
    best = None
    for t in range(align, min(n, cap) + 1, align):
        if n % t == 0:
            best = t
    return best if best is not None else n


def _dot(a, b, ca, cb):
    return lax.dot_general(a, b, (((ca,), (cb,)), ((), ())), preferred_element_type=F32)


def _sigmoid(x):
    return 1.0 / (1.0 + jnp.exp(-x))


def _rows(x, start, n):
    return lax.slice_in_dim(x, start, start + n, axis=0)


def _my_coords():
    return lax.axis_index("x"), lax.axis_index("y"), lax.axis_index("c")


def _peer(m, mx, my, mc):
    px = (mx + ((m >> 2) & 1)) % 2
    py = (my + ((m >> 1) & 1)) % 2
    pc = (mc + (m & 1)) % 2
    return px, py, pc


def _exchange_copies(kind, x_ref, o_ref, send_sems, recv_sems, local_sem):
    mx, my, mc = _my_coords()
    me = 4 * mx + 2 * my + mc
    local = pltpu.make_async_copy(x_ref if kind == "gather" else x_ref.at[me], o_ref.at[me], local_sem)
    remote = []
    for m in range(1, N_DEV):
        px, py, pc = _peer(m, mx, my, mc)
        src = x_ref if kind == "gather" else x_ref.at[4 * px + 2 * py + pc]
        remote.append(pltpu.make_async_remote_copy(
            src_ref=src, dst_ref=o_ref.at[me], send_sem=send_sems.at[m - 1], recv_sem=recv_sems.at[m - 1],
            device_id=(px, py, pc), device_id_type=MESH))
    return local, remote


def _exchange_start(kind, *refs):
    local, remote = _exchange_copies(kind, *refs)
    local.start()
    for cp in remote:
        cp.start()


def _exchange_wait(kind, *refs):
    local, remote = _exchange_copies(kind, *refs)
    for cp in remote:
        cp.wait_recv()
    for cp in remote:
        cp.wait_send()
    local.wait()


_EXCHANGE_SEMS = [pltpu.SemaphoreType.DMA((N_DEV - 1,)), pltpu.SemaphoreType.DMA((N_DEV - 1,)), pltpu.SemaphoreType.DMA]


def _exchange_shape(kind, x):
    return jax.ShapeDtypeStruct(((N_DEV,) + x.shape) if kind == "gather" else x.shape, x.dtype)


def _exchange(kind, xs, name):
    n = len(xs)

    def body(*refs):
        x_refs, o_refs, sems = refs[:n], refs[n:2 * n], refs[2 * n:]
        for e in range(n):
            _exchange_start(kind, x_refs[e], o_refs[e], *sems[3 * e:3 * e + 3])
        for e in range(n):
            _exchange_wait(kind, x_refs[e], o_refs[e], *sems[3 * e:3 * e + 3])

    any_spec = pl.BlockSpec(memory_space=pl.ANY)
    return tuple(pl.pallas_call(
        body, name=name, out_shape=tuple(_exchange_shape(kind, x) for x in xs),
        in_specs=[any_spec] * n, out_specs=(any_spec,) * n, scratch_shapes=list(_EXCHANGE_SEMS) * n,
    )(*xs))


def _all_gather(x, name):
    return _exchange("gather", [x], name)[0]


def _call(body, name, grid, in_specs, out_specs, out_shape, args, scratch=(), side=None):
    single = not isinstance(out_shape, (tuple, list))
    if single:
        out_shape, out_specs = (out_shape,), (out_specs,)
    if side is None:
        res = pl.pallas_call(body, name=name, grid=grid, in_specs=list(in_specs), out_specs=tuple(out_specs),
                             out_shape=tuple(out_shape), scratch_shapes=list(scratch), compiler_params=_cparams())(*args)
        return res[0] if single else tuple(res)
    kind, xs = side
    n_in, n_out, n_scr, n_x = len(in_specs), len(out_shape), len(scratch), len(xs)

    def wrapped(*refs):
        ins, x_refs = refs[:n_in], refs[n_in:n_in + n_x]
        o0 = n_in + n_x
        outs, o_refs = refs[o0:o0 + n_out], refs[o0 + n_out:o0 + n_out + n_x]
        s0 = o0 + n_out + n_x
        scr, sems = refs[s0:s0 + n_scr], refs[s0 + n_scr:]
        ids = [pl.program_id(a) for a in range(len(grid))]
        first = functools.reduce(jnp.logical_and, [i == 0 for i in ids])
        last = functools.reduce(jnp.logical_and, [i == g - 1 for i, g in zip(ids, grid)])

        @pl.when(first)
        def _():
            for e in range(n_x):
                _exchange_start(kind, x_refs[e], o_refs[e], *sems[3 * e:3 * e + 3])

        body(*ins, *outs, *scr)

        @pl.when(last)
        def _():
            for e in range(n_x):
                _exchange_wait(kind, x_refs[e], o_refs[e], *sems[3 * e:3 * e + 3])

    any_spec = pl.BlockSpec(memory_space=pl.ANY)
    res = pl.pallas_call(
        wrapped, name=name, grid=grid, in_specs=list(in_specs) + [any_spec] * n_x,
        out_specs=tuple(out_specs) + (any_spec,) * n_x,
        out_shape=tuple(out_shape) + tuple(_exchange_shape(kind, x) for x in xs),
        scratch_shapes=list(scratch) + list(_EXCHANGE_SEMS) * n_x, compiler_params=_cparams())(*args, *xs)
    return tuple(res[:n_out]) + (tuple(res[n_out:]),)


def _mm(a, b, mode, name, out_dtype=F32, bias=None, side=None, a2=None, tm_cap=1024, tn_cap=1408, tk_cap=1408):
    w2 = 0 if a2 is None else a2.shape[1]
    assert a2 is None or (mode in ("nn", "tn") and a2.shape[0] == a.shape[0])
    if mode == "nn":
        (M, K), (K2, N) = a.shape, b.shape
        K += w2
    elif mode == "nt":
        (M, K), (N, K2) = a.shape, b.shape
    else:
        (K, M), (K2, N) = a.shape, b.shape
        M += w2
    assert K == K2, (a.shape, b.shape, mode)
    if mode == "tn":
        tm_cap, tk_cap = 256, 8192
    if mode == "nn":
        tk_cap = 2816
    tm = _tile(math.gcd(M, w2) if mode == "tn" else M, tm_cap, 128 if mode == "tn" else 16)
    tn = _tile(N, tn_cap, 128)
    tk = _tile(math.gcd(K, w2) if mode == "nn" else K, tk_cap, 128)
    nk = K // tk
    n1 = a.shape[1] // (tm if mode == "tn" else tk)
    ca, cb = {"nn": (1, 0), "nt": (1, 1), "tn": (0, 0)}[mode]
    has_bias = bias is not None

    def body(*refs):
        refs = list(refs)
        a_ref = refs.pop(0)
        a2_ref = refs.pop(0) if a2 is not None else None
        b_ref = refs.pop(0)
        bias_ref = refs.pop(0) if has_bias else None
        o_ref, acc_ref = refs
        k = pl.program_id(2)

        def finish(r):
            if has_bias:
                r = r + bias_ref[...]
            o_ref[...] = r.astype(out_dtype)

        def step(lhs_ref):
            part = _dot(lhs_ref[...].astype(BF16), b_ref[...].astype(BF16), ca, cb)
            if nk == 1:
                finish(part)
                return

            @pl.when(k == 0)
            def _():
                acc_ref[...] = part

            @pl.when(k > 0)
            def _():
                acc_ref[...] += part

        if a2 is None:
            step(a_ref)
        else:
            col = pl.program_id(0) if mode == "tn" else k
            pl.when(col < n1)(lambda: step(a_ref))
            pl.when(col >= n1)(lambda: step(a2_ref))

        if nk > 1:
            pl.when(k == nk - 1)(lambda: finish(acc_ref[...]))

    if mode == "tn":
        a_specs = [pl.BlockSpec((tk, tm), lambda i, j, k: (k, jnp.minimum(i, n1 - 1)))]
        if a2 is not None:
            a_specs.append(pl.BlockSpec((tk, tm), lambda i, j, k: (k, jnp.maximum(i - n1, 0))))
    else:
        a_specs = [pl.BlockSpec((tm, tk), lambda i, j, k: (i, jnp.minimum(k, n1 - 1)))]
        if a2 is not None:
            a_specs.append(pl.BlockSpec((tm, tk), lambda i, j, k: (i, jnp.maximum(k - n1, 0))))
    if mode == "nt":
        b_spec = pl.BlockSpec((tn, tk), lambda i, j, k: (j, k))
    else:
        b_spec = pl.BlockSpec((tk, tn), lambda i, j, k: (k, j))
    in_specs = a_specs + [b_spec]
    args = [a] + ([a2] if a2 is not None else []) + [b]
    if has_bias:
        in_specs.append(pl.BlockSpec((1, tn), lambda i, j, k: (0, j)))
        args.append(bias)
    return _call(body, name, (M // tm, N // tn, nk), in_specs, pl.BlockSpec((tm, tn), lambda i, j, k: (i, j)),
                 jax.ShapeDtypeStruct((M, N), out_dtype), args,
                 scratch=[pltpu.VMEM((tm, tn) if nk > 1 else (8, 128), F32)], side=side)


ROW_TILE = 512


def _vec_spec(n):
    return pl.BlockSpec((1, n), lambda t: (0, 0))


def _norm_mod_fwd(x, g, sc, sh, name, y=None, gate=None):
    T, D = x.shape
    tr = _tile(T, ROW_TILE, 16)
    has_y = y is not None

    def body(*refs):
        if has_y:
            x_ref, y_ref, gate_ref, g_ref, sc_ref, sh_ref, xn_ref, h_ref = refs
            xv = x_ref[...] + gate_ref[...] * y_ref[...]
            xn_ref[...] = xv
        else:
            x_ref, g_ref, sc_ref, sh_ref, h_ref = refs
            xv = x_ref[...]
        r = lax.rsqrt(jnp.mean(xv * xv, axis=-1, keepdims=True) + NORM_EPS)
        h = (xv * r * g_ref[...]) * (1.0 + sc_ref[...]) + sh_ref[...]
        h_ref[...] = h.astype(BF16)

    row = pl.BlockSpec((tr, D), lambda t: (t, 0))
    vec = _vec_spec(D)
    if has_y:
        in_specs, args = [row, row, vec, vec, vec, vec], [x, y, gate, g, sc, sh]
        out_shape = (jax.ShapeDtypeStruct((T, D), F32), jax.ShapeDtypeStruct((T, D), BF16))
        out_specs = (row, row)
    else:
        in_specs, args = [row, vec, vec, vec], [x, g, sc, sh]
        out_shape = jax.ShapeDtypeStruct((T, D), BF16)
        out_specs = row
    return _call(body, name, (T // tr,), in_specs, out_specs, out_shape, args)


def _gate_bwd(dxn, y, gate, name):
    T, D = dxn.shape
    tr = _tile(T, ROW_TILE, 16)

    def body(dxn_ref, y_ref, gate_ref, dy_ref, dgate_ref, dbias_ref):
        @pl.when(pl.program_id(0) == 0)
        def _():
            dgate_ref[...] = jnp.zeros_like(dgate_ref)
            dbias_ref[...] = jnp.zeros_like(dbias_ref)
        d = dxn_ref[...]
        dy = d * gate_ref[...]
        dy_ref[...] = dy.astype(BF16)
        dgate_ref[...] += jnp.sum(d * y_ref[...], axis=0, keepdims=True)
        dbias_ref[...] += jnp.sum(dy, axis=0, keepdims=True)

    row = pl.BlockSpec((tr, D), lambda t: (t, 0))
    vec = _vec_spec(D)
    return _call(body, name, (T // tr,), [row, row, vec], (row, vec, vec),
                 (jax.ShapeDtypeStruct((T, D), BF16), jax.ShapeDtypeStruct((1, D), F32),
                  jax.ShapeDtypeStruct((1, D), F32)), (dxn, y, gate))


def _norm_mod_bwd(dh, x, g, sc, sh, dxn, name, y_up=None, gate_up=None):
    T, D = x.shape
    tr = _tile(T, ROW_TILE, 16)
    fused = y_up is not None

    def body(*refs):
        if fused:
            (dh_ref, x_ref, g_ref, sc_ref, sh_ref, dxn_ref, y_ref, gate_ref,
             dx_ref, dg_ref, dsc_ref, dsh_ref, dy_ref, dgate_ref, dbias_ref) = refs
            sums = (dg_ref, dsc_ref, dsh_ref, dgate_ref, dbias_ref)
        else:
            dh_ref, x_ref, g_ref, sc_ref, sh_ref, dxn_ref, dx_ref, dg_ref, dsc_ref, dsh_ref = refs
            sums = (dg_ref, dsc_ref, dsh_ref)

        @pl.when(pl.program_id(0) == 0)
        def _():
            for r_ in sums:
                r_[...] = jnp.zeros_like(r_)
        dh = dh_ref[...].astype(F32)
        xv = x_ref[...]
        gv = g_ref[...]
        r = lax.rsqrt(jnp.mean(xv * xv, axis=-1, keepdims=True) + NORM_EPS)
        yv = xv * r
        dsh_ref[...] += jnp.sum(dh, axis=0, keepdims=True)
        dsc_ref[...] += jnp.sum(dh * (yv * gv), axis=0, keepdims=True)
        dn = dh * (1.0 + sc_ref[...])
        dg_ref[...] += jnp.sum(dn * yv, axis=0, keepdims=True)
        dy = dn * gv
        dx = dxn_ref[...] + r * (dy - yv * jnp.mean(dy * yv, axis=-1, keepdims=True))
        dx_ref[...] = dx
        if fused:
            dyu = dx * gate_ref[...]
            dy_ref[...] = dyu.astype(BF16)
            dgate_ref[...] += jnp.sum(dx * y_ref[...], axis=0, keepdims=True)
            dbias_ref[...] += jnp.sum(dyu, axis=0, keepdims=True)

    row = pl.BlockSpec((tr, D), lambda t: (t, 0))
    vec = _vec_spec(D)
    vshape = jax.ShapeDtypeStruct((1, D), F32)
    in_specs, args = [row, row, vec, vec, vec, row], [dh, x, g, sc, sh, dxn]
    out_specs, out_shape = [row, vec, vec, vec], [jax.ShapeDtypeStruct((T, D), F32), vshape, vshape, vshape]
    if fused:
        in_specs, args = in_specs + [row, vec], args + [y_up, gate_up]
        out_specs, out_shape = out_specs + [row, vec, vec], out_shape + [jax.ShapeDtypeStruct((T, D), BF16), vshape, vshape]
    return _call(body, name, (T // tr,), in_specs, tuple(out_specs), tuple(out_shape), args)


def _final_loss(x, y, gate, tgt, g, name):
    T, D = x.shape
    tr = _tile(T, ROW_TILE, 16)

    def body(x_ref, y_ref, gate_ref, tgt_ref, g_ref, dx_ref, dg_ref, loss_ref):
        @pl.when(pl.program_id(0) == 0)
        def _():
            dg_ref[...] = jnp.zeros_like(dg_ref)
            loss_ref[...] = jnp.zeros_like(loss_ref)
        xv = x_ref[...] + gate_ref[...] * y_ref[...]
        gv = g_ref[...]
        r = lax.rsqrt(jnp.mean(xv * xv, axis=-1, keepdims=True) + NORM_EPS)
        yv = xv * r
        e = yv * gv - tgt_ref[...]
        per_row = jnp.mean(e * e, axis=-1, keepdims=True)
        loss_ref[...] += 0.5 * jnp.sum(per_row, axis=0, keepdims=True)
        dout = e * (1.0 / D)
        dg_ref[...] += jnp.sum(dout * yv, axis=0, keepdims=True)
        dy = dout * gv
        dx_ref[...] = r * (dy - yv * jnp.mean(dy * yv, axis=-1, keepdims=True))

    row = pl.BlockSpec((tr, D), lambda t: (t, 0))
    vec = _vec_spec(D)
    return _call(body, name, (T // tr,), [row, row, vec, row, vec], (row, vec, _vec_spec(128)),
                 (jax.ShapeDtypeStruct((T, D), F32), jax.ShapeDtypeStruct((1, D), F32),
                  jax.ShapeDtypeStruct((1, 128), F32)), (x, y, gate, tgt, g))


FFN_HALO = 16
FFN_PAD = 8
FFN_CHUNK = 64
FFN_ROWS = 2048


def _lane_groups(cw):
    lw = 128 if cw % 128 == 0 else cw
    return lw, [pl.ds(g * lw, lw) for g in range(cw // lw)]


def _ffn_mid_fwd(z, dw, b, name, side=None):
    T, F2 = z.shape
    F = F2 // 2
    K = dw.shape[0]
    H, P = FFN_HALO, FFN_PAD
    tr = _tile(T, FFN_ROWS, H)
    cw = _tile(F, 512, 128)
    nc = F // cw
    rc = _tile(tr, FFN_CHUNK, 16)
    lw, groups = _lane_groups(cw)

    def body(gc_ref, gh_ref, uc_ref, uh_ref, wg_ref, wu_ref, bg_ref, bu_ref, a_ref, xg_s, xu_s):
        first = pl.program_id(1) == 0

        def fill(s, h_ref, c_ref):
            for g, ls in enumerate(groups):
                s[g, 0:P, :] = jnp.where(first, 0.0, h_ref[:, ls].astype(F32)[H - P:H])
                s[g, P:P + tr, :] = c_ref[:, ls].astype(F32)

        fill(xg_s, gh_ref, gc_ref)
        fill(xu_s, uh_ref, uc_ref)

        def chunk(ci, carry):
            r0 = pl.multiple_of(ci * rc, rc)
            for g, ls in enumerate(groups):
                def conv(s, w_ref, b_ref):
                    acc = b_ref[:, ls] + w_ref[K - 1:K, ls] * s[g, pl.ds(r0 + P, rc), :]
                    for k in range(K - 1):
                        acc = acc + w_ref[k:k + 1, ls] * s[g, pl.ds(r0 + P - (K - 1 - k), rc), :]
                    return acc

                gv = conv(xg_s, wg_ref, bg_ref)
                uv = conv(xu_s, wu_ref, bu_ref)
                a_ref[pl.ds(r0, rc), ls] = (gv * _sigmoid(gv) * uv).astype(BF16)
            return carry

        lax.fori_loop(0, tr // rc, chunk, 0, unroll=2)

    rb = tr // H
    in_specs = [
        pl.BlockSpec((tr, cw), lambda j, t: (t, j)),
        pl.BlockSpec((H, cw), lambda j, t: (jnp.maximum(t * rb - 1, 0), j)),
        pl.BlockSpec((tr, cw), lambda j, t: (t, j + nc)),
        pl.BlockSpec((H, cw), lambda j, t: (jnp.maximum(t * rb - 1, 0), j + nc)),
        pl.BlockSpec((K, cw), lambda j, t: (0, j)),
        pl.BlockSpec((K, cw), lambda j, t: (0, j + nc)),
        pl.BlockSpec((1, cw), lambda j, t: (0, j)),
        pl.BlockSpec((1, cw), lambda j, t: (0, j + nc)),
    ]
    return _call(body, name, (nc, T // tr), in_specs, pl.BlockSpec((tr, cw), lambda j, t: (t, j)),
                 jax.ShapeDtypeStruct((T, F), BF16), (z, z, z, z, dw, dw, b, b),
                 scratch=[pltpu.VMEM((len(groups), P + tr, lw), F32)] * 2, side=side)


def _ffn_mid_bwd(z, da, dw, b, name, side=None):
    T, F2 = z.shape
    F = F2 // 2
    K = dw.shape[0]
    H, P = FFN_HALO, FFN_PAD
    tr = _tile(T, FFN_ROWS, H)
    cw = _tile(F, 512, 128)
    nc = F // cw
    nt = T // tr
    rb = tr // H
    rc = _tile(tr, FFN_CHUNK, 16)
    ext = rc + P
    lw, groups = _lane_groups(cw)

    def body(gp_ref, gc_ref, gn_ref, up_ref, uc_ref, un_ref, dac_ref, dan_ref, wg_ref, wu_ref, bg_ref, bu_ref,
             dzg_ref, dzu_ref, dwg_ref, dwu_ref, dbg_ref, dbu_ref, xg_s, xu_s, da_s, dg_s, du_s, acc_s):
        t = pl.program_id(1)
        first = t == 0
        last = t == nt - 1

        @pl.when(first)
        def _():
            for r in (dwg_ref, dwu_ref, dbg_ref, dbu_ref):
                r[...] = jnp.zeros_like(r)

        def fill(s, p_ref, c_ref, n_ref):
            for g, ls in enumerate(groups):
                s[g, 0:P, :] = jnp.where(first, 0.0, p_ref[:, ls].astype(F32)[H - P:H])
                s[g, P:P + tr, :] = c_ref[:, ls].astype(F32)
                s[g, P + tr:2 * P + tr, :] = jnp.where(last, 0.0, n_ref[:, ls].astype(F32)[0:P])

        fill(xg_s, gp_ref, gc_ref, gn_ref)
        fill(xu_s, up_ref, uc_ref, un_ref)
        for g, ls in enumerate(groups):
            da_s[g, 0:tr, :] = dac_ref[:, ls].astype(F32)
            da_s[g, tr:tr + P, :] = jnp.where(last, 0.0, dan_ref[:, ls].astype(F32)[0:P])
        acc_s[...] = jnp.zeros_like(acc_s)

        def fold(v):
            return jnp.sum(v.reshape(rc // 8, 8, lw), axis=0)

        def chunk(ci, carry):
            r0 = pl.multiple_of(ci * rc, rc)
            for g, ls in enumerate(groups):
                def conv_ext(s, w_ref, b_ref):
                    acc = b_ref[:, ls] + w_ref[K - 1:K, ls] * s[g, pl.ds(r0 + P, ext), :]
                    for k in range(K - 1):
                        acc = acc + w_ref[k:k + 1, ls] * s[g, pl.ds(r0 + P - (K - 1 - k), ext), :]
                    return acc

                gv = conv_ext(xg_s, wg_ref, bg_ref)
                uv = conv_ext(xu_s, wu_ref, bu_ref)
                dav = da_s[g, pl.ds(r0, ext), :]
                sg = _sigmoid(gv)
                dg = dav * uv * (sg * (1.0 + gv * (1.0 - sg)))
                du = dav * (gv * sg)
                dg_s[g] = dg
                du_s[g] = du
                for idx, (d_s, dval, x_s, w_ref, dz_ref) in enumerate(
                        ((dg_s, dg, xg_s, wg_ref, dzg_ref), (du_s, du, xu_s, wu_ref, dzu_ref))):
                    cur = dval[:rc]
                    acc_s[idx, K, g] += fold(cur)
                    for k in range(K):
                        acc_s[idx, k, g] += fold(cur * x_s[g, pl.ds(r0 + P - (K - 1 - k), rc), :])
                    dz = w_ref[K - 1:K, ls] * cur
                    for k in range(K - 1):
                        dz = dz + w_ref[k:k + 1, ls] * d_s[g, pl.ds(K - 1 - k, rc), :]
                    dz_ref[pl.ds(r0, rc), ls] = dz.astype(BF16)
            return carry

        lax.fori_loop(0, tr // rc, chunk, 0, unroll=2)
        for idx, (dw_ref, db_ref) in enumerate(((dwg_ref, dbg_ref), (dwu_ref, dbu_ref))):
            for g, ls in enumerate(groups):
                db_ref[:, ls] += jnp.sum(acc_s[idx, K, g], axis=0, keepdims=True)
                dw_ref[:, ls] += jnp.concatenate(
                    [jnp.sum(acc_s[idx, k, g], axis=0, keepdims=True) for k in range(K)], axis=0)

    def prev(t):
        return jnp.maximum(t * rb - 1, 0)

    def nxt(t):
        return jnp.minimum((t + 1) * rb, T // H - 1)

    def zspecs(off):
        return [pl.BlockSpec((H, cw), lambda j, t: (prev(t), j + off)),
                pl.BlockSpec((tr, cw), lambda j, t: (t, j + off)),
                pl.BlockSpec((H, cw), lambda j, t: (nxt(t), j + off))]

    in_specs = zspecs(0) + zspecs(nc) + [
        pl.BlockSpec((tr, cw), lambda j, t: (t, j)),
        pl.BlockSpec((H, cw), lambda j, t: (nxt(t), j)),
        pl.BlockSpec((K, cw), lambda j, t: (0, j)),
        pl.BlockSpec((K, cw), lambda j, t: (0, j + nc)),
        pl.BlockSpec((1, cw), lambda j, t: (0, j)),
        pl.BlockSpec((1, cw), lambda j, t: (0, j + nc)),
    ]
    out_shape = (jax.ShapeDtypeStruct((T, F), BF16), jax.ShapeDtypeStruct((T, F), BF16),
                 jax.ShapeDtypeStruct((K, F), F32), jax.ShapeDtypeStruct((K, F), F32),
                 jax.ShapeDtypeStruct((1, F), F32), jax.ShapeDtypeStruct((1, F), F32))
    out_specs = (pl.BlockSpec((tr, cw), lambda j, t: (t, j)), pl.BlockSpec((tr, cw), lambda j, t: (t, j)),
                 pl.BlockSpec((K, cw), lambda j, t: (0, j)), pl.BlockSpec((K, cw), lambda j, t: (0, j)),
                 pl.BlockSpec((1, cw), lambda j, t: (0, j)), pl.BlockSpec((1, cw), lambda j, t: (0, j)))
    ng = len(groups)
    scratch = [pltpu.VMEM((ng, 2 * P + tr, lw), F32), pltpu.VMEM((ng, 2 * P + tr, lw), F32),
               pltpu.VMEM((ng, P + tr, lw), F32), pltpu.VMEM((ng, ext, lw), F32), pltpu.VMEM((ng, ext, lw), F32),
               pltpu.VMEM((2, K + 1, ng, 8, lw), F32)]
    res = _call(body, name, (nc, nt), in_specs, out_specs, out_shape, (z, z, z, z, z, z, da, da, dw, dw, b, b),
                scratch=scratch, side=side)
    dzg, dzu, dwg, dwu, dbg, dbu = res[:6]
    return (dzg, dzu, jnp.concatenate([dwg, dwu], axis=1), jnp.concatenate([dbg, dbu], axis=1)) + tuple(res[6:])


CONV_HALO = 32


def _convmix_fwd(p, dw, dwb, lng, lnb, name, side=None):
    T, D2 = p.shape
    D = D2 // 2
    K = dw.shape[0]
    H = CONV_HALO
    tr = _tile(T, 256, H)
    rb = tr // H

    def body(pc_ref, ph_ref, w_ref, wb_ref, lng_ref, lnb_ref, zc_ref, s_ref):
        first = pl.program_id(0) == 0
        pp = jnp.concatenate([jnp.where(first, 0.0, ph_ref[...].astype(F32)), pc_ref[...].astype(F32)], axis=0)
        zg = pp[:, :D] * _sigmoid(pp[:, D:])
        w = w_ref[...]
        acc = wb_ref[...] + w[K - 1:K, :] * _rows(zg, H, tr)
        for k in range(K - 1):
            acc = acc + w[k:k + 1, :] * _rows(zg, H - (K - 1 - k), tr)
        zc_ref[...] = acc
        mu = jnp.mean(acc, axis=-1, keepdims=True)
        xc = acc - mu
        rstd = lax.rsqrt(jnp.mean(xc * xc, axis=-1, keepdims=True) + NORM_EPS)
        ln = xc * rstd * lng_ref[...] + lnb_ref[...]
        s_ref[...] = (ln * _sigmoid(ln)).astype(BF16)

    in_specs = [pl.BlockSpec((tr, D2), lambda t: (t, 0)),
                pl.BlockSpec((H, D2), lambda t: (jnp.maximum(t * rb - 1, 0), 0)),
                pl.BlockSpec((K, D), lambda t: (0, 0)), _vec_spec(D), _vec_spec(D), _vec_spec(D)]
    row = pl.BlockSpec((tr, D), lambda t: (t, 0))
    return _call(body, name, (T // tr,), in_specs, (row, row),
                 (jax.ShapeDtypeStruct((T, D), F32), jax.ShapeDtypeStruct((T, D), BF16)), (p, p, dw, dwb, lng, lnb), side=side)


def _convmix_bwd(ds, zc, p, dw, lng, lnb, name, side=None):
    T, D = zc.shape
    D2 = 2 * D
    K = dw.shape[0]
    H = CONV_HALO
    tr = _tile(T, 256, H)
    rb = tr // H
    nt = T // tr
    ext = tr + H

    def body(dsc_ref, dsn_ref, zcc_ref, zcn_ref, pp_ref, pc_ref, w_ref, lng_ref, lnb_ref,
             dp_ref, dbin_ref, ddw_ref, ddwb_ref, dlng_ref, dlnb_ref):
        t = pl.program_id(0)
        first = t == 0
        last = t == nt - 1

        @pl.when(first)
        def _():
            for r in (dbin_ref, ddw_ref, ddwb_ref, dlng_ref, dlnb_ref):
                r[...] = jnp.zeros_like(r)

        dsv = jnp.concatenate([dsc_ref[...].astype(F32), jnp.where(last, 0.0, dsn_ref[...].astype(F32))], axis=0)
        zcv = jnp.concatenate([zcc_ref[...], zcn_ref[...]], axis=0)
        lg = lng_ref[...]
        mu = jnp.mean(zcv, axis=-1, keepdims=True)
        xc = zcv - mu
        rstd = lax.rsqrt(jnp.mean(xc * xc, axis=-1, keepdims=True) + NORM_EPS)
        xh = xc * rstd
        ln = xh * lg + lnb_ref[...]
        sg = _sigmoid(ln)
        dln = dsv * (sg * (1.0 + ln * (1.0 - sg)))
        dlnc = _rows(dln, 0, tr)
        dlnb_ref[...] += jnp.sum(dlnc, axis=0, keepdims=True)
        dlng_ref[...] += jnp.sum(dlnc * _rows(xh, 0, tr), axis=0, keepdims=True)
        dxh = dln * lg
        dzc = rstd * (dxh - jnp.mean(dxh, axis=-1, keepdims=True) - xh * jnp.mean(dxh * xh, axis=-1, keepdims=True))
        dzcc = _rows(dzc, 0, tr)
        ddwb_ref[...] += jnp.sum(dzcc, axis=0, keepdims=True)

        pv = jnp.concatenate([jnp.where(first, 0.0, pp_ref[...].astype(F32)), pc_ref[...].astype(F32)], axis=0)
        av, gv = pv[:, :D], pv[:, D:]
        sgg = _sigmoid(gv)
        zg = av * sgg
        w = w_ref[...]
        parts = []
        dzg = w[K - 1:K, :] * dzcc
        for k in range(K):
            parts.append(jnp.sum(dzcc * _rows(zg, H - (K - 1 - k), tr), axis=0, keepdims=True))
            if k < K - 1:
                dzg = dzg + w[k:k + 1, :] * _rows(dzc, K - 1 - k, tr)
        ddw_ref[...] += jnp.concatenate(parts, axis=0)
        ac, sc_ = _rows(av, H, tr), _rows(sgg, H, tr)
        dpa = dzg * sc_
        dpg = dzg * ac * sc_ * (1.0 - sc_)
        dpv = jnp.concatenate([dpa, dpg], axis=1)
        dp_ref[...] = dpv.astype(BF16)
        dbin_ref[...] += jnp.sum(dpv, axis=0, keepdims=True)

    def prev(t):
        return jnp.maximum(t * rb - 1, 0)

    def nxt(t):
        return jnp.minimum((t + 1) * rb, T // H - 1)

    in_specs = [pl.BlockSpec((tr, D), lambda t: (t, 0)), pl.BlockSpec((H, D), lambda t: (nxt(t), 0)),
                pl.BlockSpec((tr, D), lambda t: (t, 0)), pl.BlockSpec((H, D), lambda t: (nxt(t), 0)),
                pl.BlockSpec((H, D2), lambda t: (prev(t), 0)), pl.BlockSpec((tr, D2), lambda t: (t, 0)),
                pl.BlockSpec((K, D), lambda t: (0, 0)), _vec_spec(D), _vec_spec(D)]
    out_shape = (jax.ShapeDtypeStruct((T, D2), BF16), jax.ShapeDtypeStruct((1, D2), F32),
                 jax.ShapeDtypeStruct((K, D), F32), jax.ShapeDtypeStruct((1, D), F32),
                 jax.ShapeDtypeStruct((1, D), F32), jax.ShapeDtypeStruct((1, D), F32))
    out_specs = (pl.BlockSpec((tr, D2), lambda t: (t, 0)), _vec_spec(D2), pl.BlockSpec((K, D), lambda t: (0, 0)),
                 _vec_spec(D), _vec_spec(D), _vec_spec(D))
    return _call(body, name, (nt,), in_specs, out_specs, out_shape, (ds, ds, zc, zc, p, p, dw, lng, lnb), side=side)


_INV_SQRT2 = 1.0 / math.sqrt(2.0)
_INV_SQRT2PI = 1.0 / math.sqrt(2.0 * math.pi)


def _gelu(x):
    return 0.5 * x * (1.0 + lax.erf(x * _INV_SQRT2))


def _gelu_grad(x):
    return 0.5 * (1.0 + lax.erf(x * _INV_SQRT2)) + x * jnp.exp(-0.5 * x * x) * _INV_SQRT2PI


def _tril_mask(n, transposed=False):
    r = lax.broadcasted_iota(jnp.int32, (n, n), 0)
    c = lax.broadcasted_iota(jnp.int32, (n, n), 1)
    return (r <= c) if transposed else (r >= c)


def _sgu_fwd(p, lng, lnb, ws, bs_t, name):
    T, S2 = p.shape
    S = S2 // 2
    G, C, _ = ws.shape
    gd = S // G
    cpt = 2 if (T // C) % 2 == 0 else 1
    tr = C * cpt

    def body(p_ref, lng_ref, lnb_ref, ws_ref, bst_ref, m_ref):
        mask = _tril_mask(C)
        lg, lb = lng_ref[...], lnb_ref[...]
        bst = bst_ref[...]
        for ci in range(cpt):
            pc = p_ref[ci * C:(ci + 1) * C, :].astype(F32)
            z = _gelu(pc)
            u, v = z[:, :S], z[:, S:]
            mu = jnp.mean(v, axis=-1, keepdims=True)
            xc = v - mu
            rstd = lax.rsqrt(jnp.mean(xc * xc, axis=-1, keepdims=True) + NORM_EPS)
            vn = (xc * rstd * lg + lb).astype(BF16)
            outs = []
            for g in range(G):
                wm = jnp.where(mask, ws_ref[g], 0.0).astype(BF16)
                vs = _dot(wm, vn[:, g * gd:(g + 1) * gd], 1, 0) + bst[:, g:g + 1]
                outs.append(u[:, g * gd:(g + 1) * gd] * vs)
            m_ref[ci * C:(ci + 1) * C, :] = jnp.concatenate(outs, axis=1).astype(BF16)

    in_specs = [pl.BlockSpec((tr, S2), lambda t: (t, 0)), _vec_spec(S), _vec_spec(S),
                pl.BlockSpec((G, C, C), lambda t: (0, 0, 0)), pl.BlockSpec((C, G), lambda t: (0, 0))]
    return _call(body, name, (T // tr,), in_specs, pl.BlockSpec((tr, S), lambda t: (t, 0)),
                 jax.ShapeDtypeStruct((T, S), BF16), (p, lng, lnb, ws, bs_t))


def _sgu_bwd(p, dm, lng, lnb, ws, bs_t, name, side=None):
    T, S2 = p.shape
    S = S2 // 2
    G, C, _ = ws.shape
    gd = S // G
    nt = T // C

    def body(p_ref, dm_ref, lng_ref, lnb_ref, ws_ref, bst_ref, dp_ref, dbin_ref, dlng_ref, dlnb_ref, dws_ref, dbst_ref):
        @pl.when(pl.program_id(0) == 0)
        def _():
            for r in (dbin_ref, dlng_ref, dlnb_ref, dws_ref, dbst_ref):
                r[...] = jnp.zeros_like(r)

        mask = _tril_mask(C)
        lg, lb = lng_ref[...], lnb_ref[...]
        bst = bst_ref[...]
        pc = p_ref[...].astype(F32)
        dmv = dm_ref[...].astype(F32)
        z = _gelu(pc)
        u, v = z[:, :S], z[:, S:]
        mu = jnp.mean(v, axis=-1, keepdims=True)
        xc = v - mu
        rstd = lax.rsqrt(jnp.mean(xc * xc, axis=-1, keepdims=True) + NORM_EPS)
        vh = xc * rstd
        vn = (vh * lg + lb).astype(BF16)
        dus, dvns, dbcols = [], [], []
        for g in range(G):
            sl = slice(g * gd, (g + 1) * gd)
            wm = jnp.where(mask, ws_ref[g], 0.0).astype(BF16)
            vs = _dot(wm, vn[:, sl], 1, 0) + bst[:, g:g + 1]
            dmg = dmv[:, sl]
            dus.append(dmg * vs)
            dvs = dmg * u[:, sl]
            dbcols.append(jnp.sum(dvs, axis=-1, keepdims=True))
            dvsb = dvs.astype(BF16)
            dws_ref[g] += jnp.where(mask, _dot(dvsb, vn[:, sl], 1, 1), 0.0)
            dvns.append(_dot(wm, dvsb, 0, 0))
        dbst_ref[...] += jnp.concatenate(dbcols, axis=1)
        dvn = jnp.concatenate(dvns, axis=1)
        dlnb_ref[...] += jnp.sum(dvn, axis=0, keepdims=True)
        dlng_ref[...] += jnp.sum(dvn * vh, axis=0, keepdims=True)
        dvh = dvn * lg
        dv = rstd * (dvh - jnp.mean(dvh, axis=-1, keepdims=True) - vh * jnp.mean(dvh * vh, axis=-1, keepdims=True))
        dz = jnp.concatenate([jnp.concatenate(dus, axis=1), dv], axis=1)
        dpv = dz * _gelu_grad(pc)
        dp_ref[...] = dpv.astype(BF16)
        dbin_ref[...] += jnp.sum(dpv, axis=0, keepdims=True)

    in_specs = [pl.BlockSpec((C, S2), lambda t: (t, 0)), pl.BlockSpec((C, S), lambda t: (t, 0)),
                _vec_spec(S), _vec_spec(S), pl.BlockSpec((G, C, C), lambda t: (0, 0, 0)),
                pl.BlockSpec((C, G), lambda t: (0, 0))]
    out_shape = (jax.ShapeDtypeStruct((T, S2), BF16), jax.ShapeDtypeStruct((1, S2), F32),
                 jax.ShapeDtypeStruct((1, S), F32), jax.ShapeDtypeStruct((1, S), F32),
                 jax.ShapeDtypeStruct((G, C, C), F32), jax.ShapeDtypeStruct((C, G), F32))
    out_specs = (pl.BlockSpec((C, S2), lambda t: (t, 0)), _vec_spec(S2), _vec_spec(S), _vec_spec(S),
                 pl.BlockSpec((G, C, C), lambda t: (0, 0, 0)), pl.BlockSpec((C, G), lambda t: (0, 0)))
    return _call(body, name, (nt,), in_specs, out_specs, out_shape, (p, dm, lng, lnb, ws, bs_t), side=side)


def _alibi_slope(h, n_heads):
    return 2.0 ** (-8.0 * (h + 1) / n_heads)


def _attn_bias(n_heads):
    B = BLOCK
    qi = jnp.arange(B)[:, None]
    kj = jnp.arange(2 * B)[None, :]
    dist = qi + B - kj
    band = (dist >= 0) & (dist < B)
    slopes = jnp.array([_alibi_slope(h, n_heads) for h in range(n_heads)], F32)
    out = []
    for valid in (band & (kj >= B), band):
        b = jnp.where(valid[None], -slopes[:, None, None] * dist.astype(F32)[None], NEG_INF)
        b = jnp.where(kj[None] == 0, 0.0, b)
        out.append(b.reshape(n_heads // 2, 2, B, 2 * B).transpose(0, 2, 1, 3).reshape(n_heads // 2, B, 4 * B))
    return jnp.stack(out)


def _sink_rows(sinks):
    return jnp.pad(sinks.astype(F32)[:, None], ((0, 0), (0, BLOCK - 1)))


def _block_diag2(x):
    z = jnp.zeros_like(x)
    return jnp.concatenate([jnp.concatenate([x, z], axis=1), jnp.concatenate([z, x], axis=1)], axis=0)


def _attn_core(q_ref, kp_ref, kc_ref, vp_ref, vc_ref, sink_ref, bias_ref, n_heads):
    B, hd = BLOCK, HEAD_DIM
    ppg = n_heads // N_KV // 2
    pairs = range(n_heads // 2)
    row0 = lax.broadcasted_iota(jnp.int32, (2 * B, N_KV * hd), 0) == 0
    k2 = jnp.concatenate([kp_ref[...], kc_ref[...]], axis=0)
    v2 = jnp.concatenate([vp_ref[...], vc_ref[...]], axis=0)
    k2 = jnp.where(row0, jnp.zeros_like(k2), k2)
    v2 = jnp.where(row0, jnp.zeros_like(v2), v2)
    scale = hd ** -0.5
    kks = [_block_diag2(k2[:, kv * hd:(kv + 1) * hd] * scale) for kv in range(N_KV)]
    vvs = [_block_diag2(v2[:, kv * hd:(kv + 1) * hd]) for kv in range(N_KV)]
    first_rows = lax.broadcasted_iota(jnp.int32, (4 * B, 4 * hd), 0) < 2 * B
    first_lanes = lax.broadcasted_iota(jnp.int32, (4 * B, 4 * hd), 1) < 2 * hd
    ones = (first_rows == first_lanes).astype(BF16)
    vxs = [jnp.concatenate([vv, ones], axis=1) for vv in vvs]
    ss = [_dot(q_ref[:, pr * 2 * hd:(pr + 1) * 2 * hd], kks[pr // ppg], 1, 1) + bias_ref[pr] for pr in pairs]
    halves = []
    for h in range(n_heads):
        s = ss[h // 2][:, (h % 2) * 2 * B:(h % 2 + 1) * 2 * B]
        halves.append(jnp.concatenate([s[:, :B] + sink_ref[h:h + 1, :], s[:, B:]], axis=1))
    mxs = [jnp.max(s, axis=-1, keepdims=True) for s in halves]
    es = [jnp.exp(s - m) for s, m in zip(halves, mxs)]
    rs = [_dot(jnp.concatenate([es[2 * pr], es[2 * pr + 1]], axis=1).astype(BF16), vxs[pr // ppg], 1, 0) for pr in pairs]
    return kks, vvs, es, rs


def _attn_fwd(qkv, sink_rows, bias, n_heads, name, side=None):
    T = qkv.shape[0]
    B = BLOCK
    hd = HEAD_DIM
    HQ = n_heads * hd
    KVW = N_KV * hd
    assert (n_heads // N_KV) % 2 == 0, "heads are processed in pairs that share a key/value head"
    nb = T // B
    kcol = HQ // KVW

    def body(q_ref, kp_ref, kc_ref, vp_ref, vc_ref, sink_ref, bias_ref, o_ref):
        _, _, _, rs = _attn_core(q_ref, kp_ref, kc_ref, vp_ref, vc_ref, sink_ref, bias_ref, n_heads)
        low = lax.broadcasted_iota(jnp.int32, (B, 2 * hd), 1) < hd
        for pr, r in enumerate(rs):
            inv = jnp.where(low, 1.0 / r[:, 2 * hd:4 * hd], 1.0 / r[:, 4 * hd:])
            o_ref[:, pr * 2 * hd:(pr + 1) * 2 * hd] = (r[:, :2 * hd] * inv).astype(BF16)

    def prev(n):
        return jnp.maximum(n - 1, 0)

    in_specs = [pl.BlockSpec((B, HQ), lambda n: (n, 0)),
                pl.BlockSpec((B, KVW), lambda n: (prev(n), kcol)), pl.BlockSpec((B, KVW), lambda n: (n, kcol)),
                pl.BlockSpec((B, KVW), lambda n: (prev(n), kcol + 1)), pl.BlockSpec((B, KVW), lambda n: (n, kcol + 1)),
                pl.BlockSpec((n_heads, B), lambda n: (0, 0)),
                pl.BlockSpec((None, n_heads // 2, B, 4 * B), lambda n: (jnp.minimum(n, 1), 0, 0, 0))]
    return _call(body, name, (nb,), in_specs, pl.BlockSpec((B, HQ), lambda n: (n, 0)),
                 jax.ShapeDtypeStruct((T, HQ), BF16), (qkv, qkv, qkv, qkv, qkv, sink_rows, bias), side=side)


def _attn_bwd(qkv, do, sink_rows, bias, n_heads, name, side=None):
    T = qkv.shape[0]
    B = BLOCK
    hd = HEAD_DIM
    HQ = n_heads * hd
    KVW = N_KV * hd
    group = n_heads // N_KV
    assert group % 2 == 0, "heads are processed in pairs that share a key/value head"
    ppg = group // 2
    nb = T // B
    kcol = HQ // KVW
    scale = hd ** -0.5

    def body(q_ref, kp_ref, kc_ref, vp_ref, vc_ref, do_ref, sink_ref, bias_ref,
             dq_ref, dk_ref, dv_ref, dbq_ref, dbk_ref, dbv_ref, dsink_ref, ck_ref, cv_ref):
        n = pl.program_id(0)
        first = n == 0

        @pl.when(first)
        def _():
            for r in (dbq_ref, dbk_ref, dbv_ref, dsink_ref, ck_ref, cv_ref):
                r[...] = jnp.zeros_like(r)

        @pl.when(n < nb)
        def _():
            pairs = range(n_heads // 2)
            heads = range(n_heads)
            lanes = [slice(pr * 2 * hd, (pr + 1) * 2 * hd) for pr in pairs]
            kks, vvs, es, rs = _attn_core(q_ref, kp_ref, kc_ref, vp_ref, vc_ref, sink_ref, bias_ref, n_heads)
            low = lax.broadcasted_iota(jnp.int32, (B, 2 * hd), 1) < hd
            slot0 = lax.broadcasted_iota(jnp.int32, (B, B), 1) == 0
            lane_head = lax.broadcasted_iota(jnp.int32, (2 * hd, 4 * hd), 0) < hd
            out_head = lax.broadcasted_iota(jnp.int32, (2 * hd, 4 * hd), 1) < 2 * hd
            ones2 = (lane_head == out_head).astype(BF16)
            invs = [1.0 / rs[h // 2][:, (2 + 2 * (h % 2)) * hd:(4 + 2 * (h % 2)) * hd] for h in heads]
            dops = [do_ref[:, lanes[pr]] for pr in pairs]
            qps = [q_ref[:, lanes[pr]] for pr in pairs]
            dos = [dops[pr].astype(F32) * rs[pr][:, :2 * hd] * jnp.where(low, invs[2 * pr], invs[2 * pr + 1])
                   for pr in pairs]
            his = [t.astype(BF16) for t in dos]
            los = [(t - hi.astype(F32)).astype(BF16) for t, hi in zip(dos, his)]
            dsums = [_dot(hi, ones2, 1, 0) + _dot(lo, ones2, 1, 0) for hi, lo in zip(his, los)]
            dps = [_dot(dops[pr], vvs[pr // ppg], 1, 1) for pr in pairs]
            wide = lambda a: jnp.concatenate([a, a], axis=1)
            p_s = [es[h] * wide(invs[h]) for h in heads]
            ds_s = [p_s[h] * (dps[h // 2][:, (h % 2) * 2 * B:(h % 2 + 1) * 2 * B]
                              - wide(dsums[h // 2][:, (h % 2) * 2 * hd:(h % 2 + 1) * 2 * hd])) for h in heads]
            for h in heads:
                dsink_ref[h:h + 1, :] += jnp.sum(ds_s[h][:, :B], axis=0, keepdims=True)

            def without_slot0(a):
                return jnp.concatenate([jnp.where(slot0, 0.0, a[:, :B]), a[:, B:]], axis=1).astype(BF16)

            ds_b = [without_slot0(a) for a in ds_s]
            p_b = [without_slot0(a) for a in p_s]
            dqs = [_dot(jnp.concatenate([ds_b[2 * pr], ds_b[2 * pr + 1]], axis=1), kks[pr // ppg], 1, 0) for pr in pairs]
            for pr in pairs:
                dq_ref[:, lanes[pr]] = dqs[pr].astype(BF16)
                dbq_ref[:, lanes[pr]] += jnp.sum(dqs[pr], axis=0, keepdims=True)
            keeps = [low, jnp.logical_not(low)]
            qms = [jnp.where(keeps[h % 2], qps[h // 2], jnp.zeros_like(qps[0])) for h in heads]
            dms = [jnp.where(keeps[h % 2], dops[h // 2], jnp.zeros_like(dops[0])) for h in heads]
            dkh = [_dot(ds_b[h], qms[h], 0, 0) for h in heads]
            dvh = [_dot(p_b[h], dms[h], 0, 0) for h in heads]
            dks, dvs = [], []
            for kv in range(N_KV):
                tk = functools.reduce(jnp.add, dkh[kv * group:(kv + 1) * group])
                tv = functools.reduce(jnp.add, dvh[kv * group:(kv + 1) * group])
                dks.append((tk[:, :hd] + tk[:, hd:]) * scale)
                dvs.append(tv[:, :hd] + tv[:, hd:])
            dk2 = jnp.concatenate(dks, axis=1)
            dv2 = jnp.concatenate(dvs, axis=1)
            dbk_ref[...] += jnp.sum(dk2, axis=0, keepdims=True)
            dbv_ref[...] += jnp.sum(dv2, axis=0, keepdims=True)
            dk_ref[...] = (ck_ref[...] + dk2[:B]).astype(BF16)
            dv_ref[...] = (cv_ref[...] + dv2[:B]).astype(BF16)
            ck_ref[...] = dk2[B:]
            cv_ref[...] = dv2[B:]

        @pl.when(n == nb)
        def _():
            dk_ref[...] = ck_ref[...].astype(BF16)
            dv_ref[...] = cv_ref[...].astype(BF16)

    def cur(n):
        return jnp.minimum(n, nb - 1)

    def prev(n):
        return jnp.maximum(cur(n) - 1, 0)

    def outp(n):
        return jnp.maximum(n - 1, 0)

    in_specs = [pl.BlockSpec((B, HQ), lambda n: (cur(n), 0)),
                pl.BlockSpec((B, KVW), lambda n: (prev(n), kcol)), pl.BlockSpec((B, KVW), lambda n: (cur(n), kcol)),
                pl.BlockSpec((B, KVW), lambda n: (prev(n), kcol + 1)), pl.BlockSpec((B, KVW), lambda n: (cur(n), kcol + 1)),
                pl.BlockSpec((B, HQ), lambda n: (cur(n), 0)), pl.BlockSpec((n_heads, B), lambda n: (0, 0)),
                pl.BlockSpec((None, n_heads // 2, B, 4 * B), lambda n: (jnp.minimum(n, 1), 0, 0, 0))]
    out_shape = (jax.ShapeDtypeStruct((T, HQ), BF16), jax.ShapeDtypeStruct((T, KVW), BF16),
                 jax.ShapeDtypeStruct((T, KVW), BF16), jax.ShapeDtypeStruct((1, HQ), F32),
                 jax.ShapeDtypeStruct((1, KVW), F32), jax.ShapeDtypeStruct((1, KVW), F32),
                 jax.ShapeDtypeStruct((n_heads, B), F32))
    out_specs = (pl.BlockSpec((B, HQ), lambda n: (cur(n), 0)), pl.BlockSpec((B, KVW), lambda n: (outp(n), 0)),
                 pl.BlockSpec((B, KVW), lambda n: (outp(n), 0)), _vec_spec(HQ), _vec_spec(KVW), _vec_spec(KVW),
                 pl.BlockSpec((n_heads, B), lambda n: (0, 0)))
    return _call(body, name, (nb + 1,), in_specs, out_specs, out_shape, (qkv, qkv, qkv, qkv, qkv, do, sink_rows, bias),
                 scratch=[pltpu.VMEM((B, KVW), F32), pltpu.VMEM((B, KVW), F32)], side=side)


def _sum8(r, name):
    _, R, C = r.shape
    tr = _tile(R, 512, 16)

    def body(r_ref, o_ref):
        acc = r_ref[0].astype(F32)
        for d in range(1, N_DEV):
            acc = acc + r_ref[d].astype(F32)
        o_ref[...] = acc

    return _call(body, name, (R // tr,), [pl.BlockSpec((N_DEV, tr, C), lambda i: (0, i, 0))],
                 pl.BlockSpec((tr, C), lambda i: (i, 0)), jax.ShapeDtypeStruct((R, C), F32), (r,))


def _adamw(w, g, m, v, name):
    R, C = w.shape
    tr = _tile(R, 512, 8)
    c1 = 1.0 - ADAM_B1 ** ADAM_STEP
    c2 = 1.0 - ADAM_B2 ** ADAM_STEP

    def body(w_ref, g_ref, m_ref, v_ref, d_ref, nm_ref, nv_ref):
        gv = g_ref[...]
        nm = ADAM_B1 * m_ref[...] + (1.0 - ADAM_B1) * gv
        nv = ADAM_B2 * v_ref[...] + (1.0 - ADAM_B2) * (gv * gv)
        nm_ref[...] = nm
        nv_ref[...] = nv
        d_ref[...] = -ADAM_LR * ((nm / c1) / (jnp.sqrt(nv / c2) + ADAM_EPS) + ADAM_WD * w_ref[...])

    spec = pl.BlockSpec((tr, C), lambda i: (i, 0))
    shp = jax.ShapeDtypeStruct((R, C), F32)
    return _call(body, name, (R // tr,), [spec] * 4, (spec,) * 3, (shp, shp, shp), (w, g, m, v))


def _adamw_nd(w, g, m, v, name):
    shape = w.shape
    c = shape[-1]
    f = lambda a: a.reshape(-1, c)
    d, nm, nv = _adamw(f(w), f(g), f(m), f(v), name)
    return d.reshape(shape), nm.reshape(shape), nv.reshape(shape)


def _pack(arrs, width):
    flat = jnp.concatenate([a.reshape(-1).astype(F32) for a in arrs])
    n = flat.shape[0]
    quantum = 8 * width
    total = -(-n // quantum) * quantum
    return jnp.pad(flat, (0, total - n)).reshape(-1, width)


def _unpack(flat, shapes):
    out, off = [], 0
    for s in shapes:
        n = math.prod(s)
        out.append(flat[off:off + n].reshape(s))
        off += n
    return out


def kernel(x, c, norm1_g, norm2_g, ada_w, ada_b, attn_wqkv, attn_bqkv, attn_sinks, attn_wo, attn_bo, conv_w_in, conv_b_in, conv_dw, conv_dw_b, conv_ln_g, conv_ln_b, conv_w_out, conv_b_out, sgu_w_in, sgu_b_in, sgu_ln_g, sgu_ln_b, sgu_ws, sgu_bs, sgu_w_out, sgu_b_out, ffn_w_in, ffn_dw, ffn_dw_b, ffn_w_out, final_g, loss_target, m_norm1_g, m_norm2_g, m_ada_w, m_ada_b, m_attn_wqkv, m_attn_bqkv, m_attn_sinks, m_attn_wo, m_attn_bo, m_conv_w_in, m_conv_b_in, m_conv_dw, m_conv_dw_b, m_conv_ln_g, m_conv_ln_b, m_conv_w_out, m_conv_b_out, m_sgu_w_in, m_sgu_b_in, m_sgu_ln_g, m_sgu_ln_b, m_sgu_ws, m_sgu_bs, m_sgu_w_out, m_sgu_b_out, m_ffn_w_in, m_ffn_dw, m_ffn_dw_b, m_ffn_w_out, m_final_g, v_norm1_g, v_norm2_g, v_ada_w, v_ada_b, v_attn_wqkv, v_attn_bqkv, v_attn_sinks, v_attn_wo, v_attn_bo, v_conv_w_in, v_conv_b_in, v_conv_dw, v_conv_dw_b, v_conv_ln_g, v_conv_ln_b, v_conv_w_out, v_conv_b_out, v_sgu_w_in, v_sgu_b_in, v_sgu_ln_g, v_sgu_ln_b, v_sgu_ws, v_sgu_bs, v_sgu_w_out, v_sgu_b_out, v_ffn_w_in, v_ffn_dw, v_ffn_dw_b, v_ffn_w_out, v_final_g):
    W = dict(norm1_g=norm1_g, norm2_g=norm2_g, ada_w=ada_w, ada_b=ada_b, attn_wqkv=attn_wqkv, attn_bqkv=attn_bqkv, attn_sinks=attn_sinks, attn_wo=attn_wo, attn_bo=attn_bo, conv_w_in=conv_w_in, conv_b_in=conv_b_in, conv_dw=conv_dw, conv_dw_b=conv_dw_b, conv_ln_g=conv_ln_g, conv_ln_b=conv_ln_b, conv_w_out=conv_w_out, conv_b_out=conv_b_out, sgu_w_in=sgu_w_in, sgu_b_in=sgu_b_in, sgu_ln_g=sgu_ln_g, sgu_ln_b=sgu_ln_b, sgu_ws=sgu_ws, sgu_bs=sgu_bs, sgu_w_out=sgu_w_out, sgu_b_out=sgu_b_out, ffn_w_in=ffn_w_in, ffn_dw=ffn_dw, ffn_dw_b=ffn_dw_b, ffn_w_out=ffn_w_out, final_g=final_g)
    MOM = dict(norm1_g=m_norm1_g, norm2_g=m_norm2_g, ada_w=m_ada_w, ada_b=m_ada_b, attn_wqkv=m_attn_wqkv, attn_bqkv=m_attn_bqkv, attn_sinks=m_attn_sinks, attn_wo=m_attn_wo, attn_bo=m_attn_bo, conv_w_in=m_conv_w_in, conv_b_in=m_conv_b_in, conv_dw=m_conv_dw, conv_dw_b=m_conv_dw_b, conv_ln_g=m_conv_ln_g, conv_ln_b=m_conv_ln_b, conv_w_out=m_conv_w_out, conv_b_out=m_conv_b_out, sgu_w_in=m_sgu_w_in, sgu_b_in=m_sgu_b_in, sgu_ln_g=m_sgu_ln_g, sgu_ln_b=m_sgu_ln_b, sgu_ws=m_sgu_ws, sgu_bs=m_sgu_bs, sgu_w_out=m_sgu_w_out, sgu_b_out=m_sgu_b_out, ffn_w_in=m_ffn_w_in, ffn_dw=m_ffn_dw, ffn_dw_b=m_ffn_dw_b, ffn_w_out=m_ffn_w_out, final_g=m_final_g)
    VAR = dict(norm1_g=v_norm1_g, norm2_g=v_norm2_g, ada_w=v_ada_w, ada_b=v_ada_b, attn_wqkv=v_attn_wqkv, attn_bqkv=v_attn_bqkv, attn_sinks=v_attn_sinks, attn_wo=v_attn_wo, attn_bo=v_attn_bo, conv_w_in=v_conv_w_in, conv_b_in=v_conv_b_in, conv_dw=v_conv_dw, conv_dw_b=v_conv_dw_b, conv_ln_g=v_conv_ln_g, conv_ln_b=v_conv_ln_b, conv_w_out=v_conv_w_out, conv_b_out=v_conv_b_out, sgu_w_in=v_sgu_w_in, sgu_b_in=v_sgu_b_in, sgu_ln_g=v_sgu_ln_g, sgu_ln_b=v_sgu_ln_b, sgu_ws=v_sgu_ws, sgu_bs=v_sgu_bs, sgu_w_out=v_sgu_w_out, sgu_b_out=v_sgu_b_out, ffn_w_in=v_ffn_w_in, ffn_dw=v_ffn_dw, ffn_dw_b=v_ffn_dw_b, ffn_w_out=v_ffn_w_out, final_g=v_final_g)
    ORDER = list(W)

    _, T, D = x.shape
    depth = norm1_g.shape[0]
    n_heads = D // HEAD_DIM
    me = 4 * lax.axis_index("x") + 2 * lax.axis_index("y") + lax.axis_index("c")
    x0 = x.reshape(T, D)
    tgt = loss_target.reshape(T, D)

    small_sharded = ["attn_bqkv", "attn_bo", "conv_dw", "sgu_b_in", "sgu_ln_g", "sgu_ln_b", "sgu_b_out", "ffn_dw"]
    s_in = [c] + [W[n] for n in small_sharded]
    s_shapes = [a.shape for a in s_in]
    gathered_small = _all_gather(_pack(s_in, 128), "gather_small").reshape(N_DEV, -1)
    s_offs = [sum(math.prod(s) for s in s_shapes[:i]) for i in range(len(s_shapes))]

    def full(idx):
        shp = s_shapes[idx]
        a = gathered_small[:, s_offs[idx]:s_offs[idx] + math.prod(shp)].reshape((N_DEV,) + shp)
        return jnp.moveaxis(a, 0, -2).reshape(shp[:-1] + (N_DEV * shp[-1],))

    c_all = full(0).reshape(N_DEV, D)
    F_small = {n: full(1 + i) for i, n in enumerate(small_sharded)}
    attn_bias = _attn_bias(n_heads)

    c_act = c_all * jax.nn.sigmoid(c_all)
    c_pad = jnp.pad(c_act, ((0, 8), (0, 0))).astype(BF16)
    n_ada = ada_w.shape[-1]
    ada_cols = lax.dynamic_slice_in_dim(ada_b, me * n_ada, n_ada, axis=1)
    mod_loc = jnp.stack([_mm(c_pad, ada_w[i].astype(BF16), "nn", "ada_mod", bias=ada_cols[i:i + 1])
                         for i in range(depth)])
    mod_all = _all_gather(mod_loc, "gather_mod")
    mod_mine = lax.dynamic_index_in_dim(mod_all, me, axis=2, keepdims=False)
    mod = jnp.transpose(mod_mine, (1, 0, 2)).reshape(depth, 6, 1, D)

    mixer_names = {0: ("attn_wqkv", "attn_wo"), 1: ("conv_w_in", "conv_w_out"), 2: ("sgu_w_in", "sgu_w_out")}

    def piece(part, i, which):
        w_in, w_out = ("ffn_w_in", "ffn_w_out") if part == "ffn" else mixer_names[i % 3]
        l = i if part == "ffn" else i // 3
        return (w_in, l, W[w_in].shape[-1]) if which == "in" else (w_out, l, W[w_out].shape[1])

    def both(part, i):
        return [(part, i, "in"), (part, i, "out")] if 0 <= i < depth else []

    def local_weight(key):
        n, l, _ = piece(*key)
        return (W[n][l].T if key[2] == "in" else W[n][l]).astype(BF16)

    WF = {}

    def land_weights(keys, gathered):
        for key, g in zip(keys, gathered):
            n, l, r = piece(*key)
            WF[(n, l)] = g.reshape(N_DEV * r, D)

    def with_gather(keys, run):
        if not keys:
            return run(None)
        *outs, landed = run(("gather", [local_weight(k) for k in keys]))
        land_weights(keys, landed)
        return outs[0] if len(outs) == 1 else tuple(outs)

    def vec(a):
        return a.reshape(1, -1)

    land_weights(both("mix", 0), _exchange("gather", [local_weight(k) for k in both("mix", 0)], "gather_first"))

    saved = []
    xs, y_prev, gate_prev = x0, None, None
    for i in range(depth):
        sh1, sc1, g1, sh2, sc2, g2 = [mod[i, k] for k in range(6)]
        kind, j = i % 3, i // 3
        st = dict(kind=kind, j=j)
        if y_prev is None:
            h1 = _norm_mod_fwd(xs, vec(norm1_g[i]), sc1, sh1, "norm1_fwd")
        else:
            xs, h1 = _norm_mod_fwd(xs, vec(norm1_g[i]), sc1, sh1, "norm1_fwd", y=y_prev, gate=gate_prev)
        st.update(x_in=xs, h1=h1)
        first_ffn_in = [("ffn", 0, "in")] if i == 0 else []
        if kind == 0:
            qkv = _mm(h1, WF[("attn_wqkv", j)], "nt", "attn_qkv", out_dtype=BF16, bias=vec(F_small["attn_bqkv"][j]))
            o = with_gather(first_ffn_in, lambda side: _attn_fwd(qkv, _sink_rows(attn_sinks[j]), attn_bias, n_heads,
                                                                 "attn_fwd", side=side))
            y1 = _mm(o, WF[("attn_wo", j)], "nn", "attn_out", bias=vec(F_small["attn_bo"][j]))
            st.update(qkv=qkv, o=o)
        elif kind == 1:
            p = _mm(h1, WF[("conv_w_in", j)], "nt", "conv_in", out_dtype=BF16, bias=vec(conv_b_in[j]))
            zc, s = with_gather(first_ffn_in, lambda side: _convmix_fwd(
                p, F_small["conv_dw"][j], vec(conv_dw_b[j]), vec(conv_ln_g[j]), vec(conv_ln_b[j]), "convmix_fwd", side=side))
            y1 = _mm(s, WF[("conv_w_out", j)], "nn", "conv_out", bias=vec(conv_b_out[j]))
            st.update(p=p, zc=zc, s=s)
        else:
            p = _mm(h1, WF[("sgu_w_in", j)], "nt", "sgu_in", out_dtype=BF16, bias=vec(F_small["sgu_b_in"][j]))
            mm_ = _sgu_fwd(p, vec(F_small["sgu_ln_g"][j]), vec(F_small["sgu_ln_b"][j]), sgu_ws[j], sgu_bs[j].T, "sgu_fwd")
            y1 = _mm(mm_, WF[("sgu_w_out", j)], "nn", "sgu_out", bias=vec(F_small["sgu_b_out"][j]))
            st.update(p=p, m=mm_)
        xs, h2 = _norm_mod_fwd(xs, vec(norm2_g[i]), sc2, sh2, "norm2_fwd", y=y1, gate=g1)
        nxt = i + 1
        z = with_gather(both("mix", nxt) + ([("ffn", 0, "out")] if i == 0 else []),
                        lambda side: _mm(h2, WF[("ffn_w_in", i)], "nt", "ffn_in", out_dtype=BF16, side=side))
        a = with_gather(both("ffn", nxt)[:1],
                        lambda side: _ffn_mid_fwd(z, F_small["ffn_dw"][i], vec(ffn_dw_b[i]), "ffn_mid_fwd", side=side))
        y2 = with_gather(both("ffn", nxt)[1:], lambda side: _mm(a, WF[("ffn_w_out", i)], "nn", "ffn_out", side=side))
        st.update(y1=y1, x_mid=xs, h2=h2, z=z, a=a, y2=y2)
        saved.append(st)
        y_prev, gate_prev = y2, g2

    dx, d_final_g, loss_row = _final_loss(xs, y_prev, gate_prev, tgt, vec(final_g), "final_loss")
    loss = lax.psum(loss_row[0, 0], ("x", "y", "c"))

    G = {n: [None] * W[n].shape[0] for n in ORDER if n != "final_g"}
    GW = {}
    recv = {}
    dmod = [None] * depth

    def with_scatter(keys, run):
        if not keys:
            return run(None)
        sends = [GW[piece(*k)[:2]].reshape(N_DEV, piece(*k)[2], D) for k in keys]
        *outs, landed = run(("scatter", sends))
        recv.update(zip(keys, landed))
        return outs[0] if len(outs) == 1 else tuple(outs)

    for i in reversed(range(depth)):
        st = saved[i]
        sh1, sc1, g1, sh2, sc2, g2 = [mod[i, k] for k in range(6)]
        kind, j = st["kind"], st["j"]
        last_out = [("mix", 0, "out")] if i == 0 else []
        last_in = [("mix", 0, "in")] if i == 0 else []
        if i == depth - 1:
            dy2, dg2, _ = _gate_bwd(dx, st["y2"], g2, "gate_bwd")
        GW[("ffn_w_out", i)] = _mm(st["a"], dy2, "tn", "ffn_out_dw", out_dtype=BF16)
        da = _mm(dy2, WF[("ffn_w_out", i)], "nt", "ffn_out_dx", out_dtype=BF16)
        dzg, dzu, ddw, ddwb = with_scatter(both("mix", i + 1) + [("ffn", i, "out")], lambda side: _ffn_mid_bwd(
            st["z"], da, F_small["ffn_dw"][i], vec(ffn_dw_b[i]), "ffn_mid_bwd", side=side))
        G["ffn_dw"][i], G["ffn_dw_b"][i] = ddw, ddwb[0]
        GW[("ffn_w_in", i)] = _mm(dzg, st["h2"], "tn", "ffn_in_dw", out_dtype=BF16, a2=dzu)
        dh2 = with_scatter([("ffn", i, "in")],
                           lambda side: _mm(dzg, WF[("ffn_w_in", i)], "nn", "ffn_in_dx", a2=dzu, side=side))
        dx, dn2, dsc2, dsh2, dy1, dg1, dbo = _norm_mod_bwd(dh2, st["x_mid"], vec(norm2_g[i]), sc2, sh2, dx, "norm_bwd",
                                                           y_up=st["y1"], gate_up=g1)
        G["norm2_g"][i] = dn2[0]
        if kind == 0:
            G["attn_bo"][j] = dbo[0]
            GW[("attn_wo", j)] = _mm(st["o"], dy1, "tn", "attn_out_dw", out_dtype=BF16)
            do = _mm(dy1, WF[("attn_wo", j)], "nt", "attn_out_dx", out_dtype=BF16)
            dq, dk, dv, dbq, dbk, dbv, dsk = with_scatter(last_out, lambda side: _attn_bwd(
                st["qkv"], do, _sink_rows(attn_sinks[j]), attn_bias, n_heads, "attn_bwd", side=side))
            dqkv = jnp.concatenate([dq, dk, dv], axis=1)
            G["attn_bqkv"][j] = jnp.concatenate([dbq, dbk, dbv], axis=1)[0]
            G["attn_sinks"][j] = dsk[:, 0]
            GW[("attn_wqkv", j)] = _mm(dqkv, st["h1"], "tn", "attn_qkv_dw", out_dtype=BF16)
            dh1 = with_scatter(last_in, lambda side: _mm(dqkv, WF[("attn_wqkv", j)], "nn", "attn_qkv_dx", side=side))
        elif kind == 1:
            G["conv_b_out"][j] = dbo[0]
            GW[("conv_w_out", j)] = _mm(st["s"], dy1, "tn", "conv_out_dw", out_dtype=BF16)
            ds = _mm(dy1, WF[("conv_w_out", j)], "nt", "conv_out_dx")
            dp, dbin, ddw, ddwb, dlng, dlnb = with_scatter(last_out, lambda side: _convmix_bwd(
                ds, st["zc"], st["p"], F_small["conv_dw"][j], vec(conv_ln_g[j]), vec(conv_ln_b[j]), "convmix_bwd",
                side=side))
            G["conv_b_in"][j], G["conv_dw"][j], G["conv_dw_b"][j] = dbin[0], ddw, ddwb[0]
            G["conv_ln_g"][j], G["conv_ln_b"][j] = dlng[0], dlnb[0]
            GW[("conv_w_in", j)] = _mm(dp, st["h1"], "tn", "conv_in_dw", out_dtype=BF16)
            dh1 = with_scatter(last_in, lambda side: _mm(dp, WF[("conv_w_in", j)], "nn", "conv_in_dx", side=side))
        else:
            G["sgu_b_out"][j] = dbo[0]
            GW[("sgu_w_out", j)] = _mm(st["m"], dy1, "tn", "sgu_out_dw", out_dtype=BF16)
            dm = _mm(dy1, WF[("sgu_w_out", j)], "nt", "sgu_out_dx")
            dp, dbin, dlng, dlnb, dws, dbst = with_scatter(last_out, lambda side: _sgu_bwd(
                st["p"], dm, vec(F_small["sgu_ln_g"][j]), vec(F_small["sgu_ln_b"][j]), sgu_ws[j], sgu_bs[j].T, "sgu_bwd",
                side=side))
            G["sgu_b_in"][j], G["sgu_ln_g"][j], G["sgu_ln_b"][j] = dbin[0], dlng[0], dlnb[0]
            G["sgu_ws"][j], G["sgu_bs"][j] = dws, dbst.T
            GW[("sgu_w_in", j)] = _mm(dp, st["h1"], "tn", "sgu_in_dw", out_dtype=BF16)
            dh1 = with_scatter(last_in, lambda side: _mm(dp, WF[("sgu_w_in", j)], "nn", "sgu_in_dx", side=side))
        dmod_i = lambda: jnp.concatenate([dsh1, dsc1, dg1, dsh2, dsc2, dg2], axis=1)[0]
        if i > 0:
            dx, dn1, dsc1, dsh1, dy2_below, dg2_below, _ = _norm_mod_bwd(
                dh1, st["x_in"], vec(norm1_g[i]), sc1, sh1, dx, "norm_bwd", y_up=saved[i - 1]["y2"], gate_up=mod[i - 1, 5])
            dmod[i] = dmod_i()
            dy2, dg2 = dy2_below, dg2_below
        else:
            dx, dn1, dsc1, dsh1 = _norm_mod_bwd(dh1, st["x_in"], vec(norm1_g[i]), sc1, sh1, dx, "norm_bwd")
            dmod[i] = dmod_i()
        G["norm1_g"][i] = dn1[0]
    grad_x = dx.reshape(x.shape)

    small_names = [n for n in ORDER if n not in
                   ("ada_w", "ada_b", "attn_wqkv", "attn_wo", "conv_w_in", "conv_w_out", "sgu_w_in", "sgu_w_out",
                    "ffn_w_in", "ffn_w_out", "final_g")]
    g_small = {n: jnp.stack(G[n]) for n in small_names}
    g_small["final_g"] = d_final_g[0]
    g_small["ada_b"] = jnp.stack(dmod)
    names16 = ["ffn_dw", "conv_dw", "sgu_ws", "ffn_dw_b", "ada_b"]
    names32 = [n for n in g_small if n not in names16]
    shapes16, shapes32 = [g_small[n].shape for n in names16], [g_small[n].shape for n in names32]
    gathered32, gathered16 = _exchange("gather", [_pack([g_small[n] for n in names32], 1024),
                                                  _pack([g_small[n] for n in names16], 1024).astype(BF16)],
                                       "gather_small_grads")
    gsum = dict(zip(names32, _unpack(_sum8(gathered32, "sum_small_grads").reshape(-1), shapes32)))
    gsum.update(zip(names16, _unpack(_sum8(gathered16, "sum_small_grads").reshape(-1), shapes16)))

    def local_shard(n, a):
        if n in small_sharded:
            w = W[n].shape[-1]
            return lax.dynamic_slice_in_dim(a, me * w, w, axis=a.ndim - 1)
        return a

    gsum = {n: local_shard(n, a) for n, a in gsum.items()}

    off16 = sum(math.prod(s) for s in shapes16[:-1])
    dmod_all = gathered16.reshape(N_DEV, -1)[:, off16:off16 + math.prod(shapes16[-1])].reshape((N_DEV,) + shapes16[-1])
    dmod_cols = lax.dynamic_slice_in_dim(dmod_all, me * n_ada, n_ada, axis=2)
    dmod_pad = jnp.pad(dmod_cols, ((0, 8), (0, 0), (0, 0)))
    g_ada_w = jnp.stack([_mm(c_pad, dmod_pad[:, i, :], "tn", "ada_dw") for i in range(depth)])

    big = {}
    for i in range(depth):
        for key in both("mix", i) + both("ffn", i):
            n, l, _ = piece(*key)
            gp = _sum8(recv[key], "sum_weight_grads")
            big.setdefault(n, {})[l] = gp.T if key[2] == "in" else gp
    grads = {n: jnp.stack([v[l] for l in range(len(v))]) for n, v in big.items()}
    grads["ada_w"] = g_ada_w
    grads.update(gsum)

    delta, new_m, new_v = {}, {}, {}
    big_names = ["ada_w", "attn_wqkv", "attn_wo", "conv_w_in", "conv_w_out", "sgu_w_in", "sgu_w_out", "ffn_w_in",
                 "ffn_w_out"]
    for n in big_names:
        delta[n], new_m[n], new_v[n] = _adamw_nd(W[n], grads[n], MOM[n], VAR[n], "adamw_" + n)
    rest = [n for n in ORDER if n not in big_names]
    rest_shapes = [W[n].shape for n in rest]
    pk = lambda d: _pack([d[n] for n in rest], 128)
    d_s, m_s, v_s = _adamw(pk(W), pk(grads), pk(MOM), pk(VAR), "adamw_small")
    for n, a, b_, c_ in zip(rest, _unpack(d_s.reshape(-1), rest_shapes), _unpack(m_s.reshape(-1), rest_shapes),
                            _unpack(v_s.reshape(-1), rest_shapes)):
        delta[n], new_m[n], new_v[n] = a, b_, c_

    return (loss, grad_x, *[grads[n] for n in ORDER], *[delta[n] for n in ORDER],
            *[new_m[n] for n in ORDER], *[new_v[n] for n in ORDER])
```

```python
import functools
import math

import jax
import jax.numpy as jnp
from jax import lax
from jax.experimental import pallas as pl
from jax.experimental.pallas import tpu as pltpu

F32 = jnp.float32
BF16 = jnp.bfloat16

N_DEV = 8
HEAD_DIM = 64
N_KV = 4
BLOCK = 128
NORM_EPS = 1e-6
NEG_INF = -1e30
ADAM_LR = 0.001
ADAM_B1 = 0.9
ADAM_B2 = 0.999
ADAM_EPS = 1e-08
ADAM_WD = 0.01
ADAM_STEP = 10
V7X_VMEM_LIMIT = 56 * 1024 * 1024
MESH = pl.DeviceIdType.MESH


def _cparams():
    return pltpu.CompilerParams(vmem_limit_bytes=V7X_VMEM_LIMIT)


def _tile(n, cap, align):
    best = None
    for t in range(align, min(n, cap) + 1, align):
        if n % t == 0:
            best = t
    return best if best is not None else n


def _dot(a, b, ca, cb):
    return lax.dot_general(a, b, (((ca,), (cb,)), ((), ())), preferred_element_type=F32)


def _sigmoid(x):
    return 1.0 / (1.0 + jnp.exp(-x))


def _rows(x, start, n):
    return lax.slice_in_dim(x, start, start + n, axis=0)


def _my_coords():
    return lax.axis_index("x"), lax.axis_index("y"), lax.axis_index("c")


def _peer(m, mx, my, mc):
    px = (mx + ((m >> 2) & 1)) % 2
    py = (my + ((m >> 1) & 1)) % 2
    pc = (mc + (m & 1)) % 2
    return px, py, pc


def _exchange_copies(kind, x_ref, o_ref, send_sems, recv_sems, local_sem):
    mx, my, mc = _my_coords()
    me = 4 * mx + 2 * my + mc
    local = pltpu.make_async_copy(x_ref if kind == "gather" else x_ref.at[me], o_ref.at[me], local_sem)
    remote = []
    for m in range(1, N_DEV):
        px, py, pc = _peer(m, mx, my, mc)
        src = x_ref if kind == "gather" else x_ref.at[4 * px + 2 * py + pc]
        remote.append(pltpu.make_async_remote_copy(
            src_ref=src, dst_ref=o_ref.at[me], send_sem=send_sems.at[m - 1], recv_sem=recv_sems.at[m - 1],
            device_id=(px, py, pc), device_id_type=MESH))
    return local, remote


def _exchange_start(kind, *refs):
    local, remote = _exchange_copies(kind, *refs)
    local.start()
    for cp in remote:
        cp.start()


def _exchange_wait(kind, *refs):
    local, remote = _exchange_copies(kind, *refs)
    for cp in remote:
        cp.wait_recv()
    for cp in remote:
        cp.wait_send()
    local.wait()


_EXCHANGE_SEMS = [pltpu.SemaphoreType.DMA((N_DEV - 1,)), pltpu.SemaphoreType.DMA((N_DEV - 1,)), pltpu.SemaphoreType.DMA]


def _exchange_shape(kind, x):
    return jax.ShapeDtypeStruct(((N_DEV,) + x.shape) if kind == "gather" else x.shape, x.dtype)


def _exchange(kind, xs, name):
    n = len(xs)

    def body(*refs):
        x_refs, o_refs, sems = refs[:n], refs[n:2 * n], refs[2 * n:]
        for e in range(n):
            _exchange_start(kind, x_refs[e], o_refs[e], *sems[3 * e:3 * e + 3])
        for e in range(n):
            _exchange_wait(kind, x_refs[e], o_refs[e], *sems[3 * e:3 * e + 3])

    any_spec = pl.BlockSpec(memory_space=pl.ANY)
    return tuple(pl.pallas_call(
        body, name=name, out_shape=tuple(_exchange_shape(kind, x) for x in xs),
        in_specs=[any_spec] * n, out_specs=(any_spec,) * n, scratch_shapes=list(_EXCHANGE_SEMS) * n,
    )(*xs))


def _all_gather(x, name):
    return _exchange("gather", [x], name)[0]


def _call(body, name, grid, in_specs, out_specs, out_shape, args, scratch=(), side=None):
    single = not isinstance(out_shape, (tuple, list))
    if single:
        out_shape, out_specs = (out_shape,), (out_specs,)
    if side is None:
        res = pl.pallas_call(body, name=name, grid=grid, in_specs=list(in_specs), out_specs=tuple(out_specs),
                             out_shape=tuple(out_shape), scratch_shapes=list(scratch), compiler_params=_cparams())(*args)
        return res[0] if single else tuple(res)
    kind, xs = side
    n_in, n_out, n_scr, n_x = len(in_specs), len(out_shape), len(scratch), len(xs)

    def wrapped(*refs):
        ins, x_refs = refs[:n_in], refs[n_in:n_in + n_x]
        o0 = n_in + n_x
        outs, o_refs = refs[o0:o0 + n_out], refs[o0 + n_out:o0 + n_out + n_x]
        s0 = o0 + n_out + n_x
        scr, sems = refs[s0:s0 + n_scr], refs[s0 + n_scr:]
        ids = [pl.program_id(a) for a in range(len(grid))]
        first = functools.reduce(jnp.logical_and, [i == 0 for i in ids])
        last = functools.reduce(jnp.logical_and, [i == g - 1 for i, g in zip(ids, grid)])

        @pl.when(first)
        def _():
            for e in range(n_x):
                _exchange_start(kind, x_refs[e], o_refs[e], *sems[3 * e:3 * e + 3])

        body(*ins, *outs, *scr)

        @pl.when(last)
        def _():
            for e in range(n_x):
                _exchange_wait(kind, x_refs[e], o_refs[e], *sems[3 * e:3 * e + 3])

    any_spec = pl.BlockSpec(memory_space=pl.ANY)
    res = pl.pallas_call(
        wrapped, name=name, grid=grid, in_specs=list(in_specs) + [any_spec] * n_x,
        out_specs=tuple(out_specs) + (any_spec,) * n_x,
        out_shape=tuple(out_shape) + tuple(_exchange_shape(kind, x) for x in xs),
        scratch_shapes=list(scratch) + list(_EXCHANGE_SEMS) * n_x, compiler_params=_cparams())(*args, *xs)
    return tuple(res[:n_out]) + (tuple(res[n_out:]),)


def _mm(a, b, mode, name, out_dtype=F32, bias=None, side=None, a2=None, tm_cap=1024, tn_cap=1408, tk_cap=1408):
    w2 = 0 if a2 is None else a2.shape[1]
    assert a2 is None or (mode in ("nn", "tn") and a2.shape[0] == a.shape[0])
    if mode == "nn":
        (M, K), (K2, N) = a.shape, b.shape
        K += w2
    elif mode == "nt":
        (M, K), (N, K2) = a.shape, b.shape
    else:
        (K, M), (K2, N) = a.shape, b.shape
        M += w2
    assert K == K2, (a.shape, b.shape, mode)
    if mode == "tn":
        tm_cap, tk_cap = 256, 8192
    if mode == "nn":
        tk_cap = 2816
    tm = _tile(math.gcd(M, w2) if mode == "tn" else M, tm_cap, 128 if mode == "tn" else 16)
    tn = _tile(N, tn_cap, 128)
    tk = _tile(math.gcd(K, w2) if mode == "nn" else K, tk_cap, 128)
    nk = K // tk
    n1 = a.shape[1] // (tm if mode == "tn" else tk)
    ca, cb = {"nn": (1, 0), "nt": (1, 1), "tn": (0, 0)}[mode]
    has_bias = bias is not None

    def body(*refs):
        refs = list(refs)
        a_ref = refs.pop(0)
        a2_ref = refs.pop(0) if a2 is not None else None
        b_ref = refs.pop(0)
        bias_ref = refs.pop(0) if has_bias else None
        o_ref, acc_ref = refs
        k = pl.program_id(2)

        def finish(r):
            if has_bias:
                r = r + bias_ref[...]
            o_ref[...] = r.astype(out_dtype)

        def step(lhs_ref):
            part = _dot(lhs_ref[...].astype(BF16), b_ref[...].astype(BF16), ca, cb)
            if nk == 1:
                finish(part)
                return

            @pl.when(k == 0)
            def _():
                acc_ref[...] = part

            @pl.when(k > 0)
            def _():
                acc_ref[...] += part

        if a2 is None:
            step(a_ref)
        else:
            col = pl.program_id(0) if mode == "tn" else k
            pl.when(col < n1)(lambda: step(a_ref))
            pl.when(col >= n1)(lambda: step(a2_ref))

        if nk > 1:
            pl.when(k == nk - 1)(lambda: finish(acc_ref[...]))

    if mode == "tn":
        a_specs = [pl.BlockSpec((tk, tm), lambda i, j, k: (k, jnp.minimum(i, n1 - 1)))]
        if a2 is not None:
            a_specs.append(pl.BlockSpec((tk, tm), lambda i, j, k: (k, jnp.maximum(i - n1, 0))))
    else:
        a_specs = [pl.BlockSpec((tm, tk), lambda i, j, k: (i, jnp.minimum(k, n1 - 1)))]
        if a2 is not None:
            a_specs.append(pl.BlockSpec((tm, tk), lambda i, j, k: (i, jnp.maximum(k - n1, 0))))
    if mode == "nt":
        b_spec = pl.BlockSpec((tn, tk), lambda i, j, k: (j, k))
    else:
        b_spec = pl.BlockSpec((tk, tn), lambda i, j, k: (k, j))
    in_specs = a_specs + [b_spec]
    args = [a] + ([a2] if a2 is not None else []) + [b]
    if has_bias:
        in_specs.append(pl.BlockSpec((1, tn), lambda i, j, k: (0, j)))
        args.append(bias)
    return _call(body, name, (M // tm, N // tn, nk), in_specs, pl.BlockSpec((tm, tn), lambda i, j, k: (i, j)),
                 jax.ShapeDtypeStruct((M, N), out_dtype), args,
                 scratch=[pltpu.VMEM((tm, tn) if nk > 1 else (8, 128), F32)], side=side)


ROW_TILE = 512


def _vec_spec(n):
    return pl.BlockSpec((1, n), lambda t: (0, 0))


def _norm_mod_fwd(x, g, sc, sh, name, y=None, gate=None):
    T, D = x.shape
    tr = _tile(T, ROW_TILE, 16)
    has_y = y is not None

    def body(*refs):
        if has_y:
            x_ref, y_ref, gate_ref, g_ref, sc_ref, sh_ref, xn_ref, h_ref = refs
            xv = x_ref[...] + gate_ref[...] * y_ref[...]
            xn_ref[...] = xv
        else:
            x_ref, g_ref, sc_ref, sh_ref, h_ref = refs
            xv = x_ref[...]
        r = lax.rsqrt(jnp.mean(xv * xv, axis=-1, keepdims=True) + NORM_EPS)
        h = (xv * r * g_ref[...]) * (1.0 + sc_ref[...]) + sh_ref[...]
        h_ref[...] = h.astype(BF16)

    row = pl.BlockSpec((tr, D), lambda t: (t, 0))
    vec = _vec_spec(D)
    if has_y:
        in_specs, args = [row, row, vec, vec, vec, vec], [x, y, gate, g, sc, sh]
        out_shape = (jax.ShapeDtypeStruct((T, D), F32), jax.ShapeDtypeStruct((T, D), BF16))
        out_specs = (row, row)
    else:
        in_specs, args = [row, vec, vec, vec], [x, g, sc, sh]
        out_shape = jax.ShapeDtypeStruct((T, D), BF16)
        out_specs = row
    return _call(body, name, (T // tr,), in_specs, out_specs, out_shape, args)


def _norm_mod_bwd(dh, x, g, sc, sh, dxn, name, y_up=None, gate_up=None):
    T, D = x.shape
    tr = _tile(T, ROW_TILE, 16)
    fused = y_up is not None

    def body(*refs):
        if fused:
            (dh_ref, x_ref, g_ref, sc_ref, sh_ref, dxn_ref, y_ref, gate_ref,
             dx_ref, dg_ref, dsc_ref, dsh_ref, dy_ref, dgate_ref, dbias_ref) = refs
            sums = (dg_ref, dsc_ref, dsh_ref, dgate_ref, dbias_ref)
        else:
            dh_ref, x_ref, g_ref, sc_ref, sh_ref, dxn_ref, dx_ref, dg_ref, dsc_ref, dsh_ref = refs
            sums = (dg_ref, dsc_ref, dsh_ref)

        @pl.when(pl.program_id(0) == 0)
        def _():
            for r_ in sums:
                r_[...] = jnp.zeros_like(r_)
        dh = dh_ref[...].astype(F32)
        xv = x_ref[...]
        gv = g_ref[...]
        r = lax.rsqrt(jnp.mean(xv * xv, axis=-1, keepdims=True) + NORM_EPS)
        yv = xv * r
        dsh_ref[...] += jnp.sum(dh, axis=0, keepdims=True)
        dsc_ref[...] += jnp.sum(dh * (yv * gv), axis=0, keepdims=True)
        dn = dh * (1.0 + sc_ref[...])
        dg_ref[...] += jnp.sum(dn * yv, axis=0, keepdims=True)
        dy = dn * gv
        dx = dxn_ref[...] + r * (dy - yv * jnp.mean(dy * yv, axis=-1, keepdims=True))
        dx_ref[...] = dx
        if fused:
            dyu = dx * gate_ref[...]
            dy_ref[...] = dyu.astype(BF16)
            dgate_ref[...] += jnp.sum(dx * y_ref[...], axis=0, keepdims=True)
            dbias_ref[...] += jnp.sum(dyu, axis=0, keepdims=True)

    row = pl.BlockSpec((tr, D), lambda t: (t, 0))
    vec = _vec_spec(D)
    vshape = jax.ShapeDtypeStruct((1, D), F32)
    in_specs, args = [row, row, vec, vec, vec, row], [dh, x, g, sc, sh, dxn]
    out_specs, out_shape = [row, vec, vec, vec], [jax.ShapeDtypeStruct((T, D), F32), vshape, vshape, vshape]
    if fused:
        in_specs, args = in_specs + [row, vec], args + [y_up, gate_up]
        out_specs, out_shape = out_specs + [row, vec, vec], out_shape + [jax.ShapeDtypeStruct((T, D), BF16), vshape, vshape]
    return _call(body, name, (T // tr,), in_specs, tuple(out_specs), tuple(out_shape), args)


def _final_loss(x, y, gate, tgt, g, name):
    T, D = x.shape
    tr = _tile(T, ROW_TILE, 16)

    def body(x_ref, y_ref, gate_ref, tgt_ref, g_ref, dx_ref, dg_ref, loss_ref, dyu_ref, dgate_ref):
        @pl.when(pl.program_id(0) == 0)
        def _():
            for r_ in (dg_ref, loss_ref, dgate_ref):
                r_[...] = jnp.zeros_like(r_)
        yu = y_ref[...]
        gate = gate_ref[...]
        xv = x_ref[...] + gate * yu
        gv = g_ref[...]
        r = lax.rsqrt(jnp.mean(xv * xv, axis=-1, keepdims=True) + NORM_EPS)
        yv = xv * r
        e = yv * gv - tgt_ref[...]
        per_row = jnp.mean(e * e, axis=-1, keepdims=True)
        loss_ref[...] += 0.5 * jnp.sum(per_row, axis=0, keepdims=True)
        dout = e * (1.0 / D)
        dg_ref[...] += jnp.sum(dout * yv, axis=0, keepdims=True)
        dy = dout * gv
        dx = r * (dy - yv * jnp.mean(dy * yv, axis=-1, keepdims=True))
        dx_ref[...] = dx
        dyu_ref[...] = (dx * gate).astype(BF16)
        dgate_ref[...] += jnp.sum(dx * yu, axis=0, keepdims=True)

    row = pl.BlockSpec((tr, D), lambda t: (t, 0))
    vec = _vec_spec(D)
    return _call(body, name, (T // tr,), [row, row, vec, row, vec], (row, vec, _vec_spec(128), row, vec),
                 (jax.ShapeDtypeStruct((T, D), F32), jax.ShapeDtypeStruct((1, D), F32),
                  jax.ShapeDtypeStruct((1, 128), F32), jax.ShapeDtypeStruct((T, D), BF16),
                  jax.ShapeDtypeStruct((1, D), F32)), (x, y, gate, tgt, g))


FFN_HALO = 16
FFN_PAD = 8
FFN_CHUNK = 64
FFN_ROWS = 2048


def _lane_groups(cw):
    lw = 128 if cw % 128 == 0 else cw
    return lw, [pl.ds(g * lw, lw) for g in range(cw // lw)]


def _ffn_mid_fwd(z, dw, b, name, side=None):
    T, F2 = z.shape
    F = F2 // 2
    K = dw.shape[0]
    H, P = FFN_HALO, FFN_PAD
    tr = _tile(T, FFN_ROWS, H)
    cw = _tile(F, 512, 128)
    nc = F // cw
    rc = _tile(tr, FFN_CHUNK, 16)
    lw, groups = _lane_groups(cw)

    def body(gc_ref, gh_ref, uc_ref, uh_ref, wg_ref, wu_ref, bg_ref, bu_ref, a_ref, xg_s, xu_s):
        first = pl.program_id(1) == 0

        def fill(s, h_ref, c_ref):
            for g, ls in enumerate(groups):
                s[g, 0:P, :] = jnp.where(first, 0.0, h_ref[:, ls].astype(F32)[H - P:H])
                s[g, P:P + tr, :] = c_ref[:, ls].astype(F32)

        fill(xg_s, gh_ref, gc_ref)
        fill(xu_s, uh_ref, uc_ref)

        def chunk(ci, carry):
            r0 = pl.multiple_of(ci * rc, rc)
            for g, ls in enumerate(groups):
                def conv(s, w_ref, b_ref):
                    acc = b_ref[:, ls] + w_ref[K - 1:K, ls] * s[g, pl.ds(r0 + P, rc), :]
                    for k in range(K - 1):
                        acc = acc + w_ref[k:k + 1, ls] * s[g, pl.ds(r0 + P - (K - 1 - k), rc), :]
                    return acc

                gv = conv(xg_s, wg_ref, bg_ref)
                uv = conv(xu_s, wu_ref, bu_ref)
                a_ref[pl.ds(r0, rc), ls] = (gv * _sigmoid(gv) * uv).astype(BF16)
            return carry

        lax.fori_loop(0, tr // rc, chunk, 0, unroll=2)

    rb = tr // H
    in_specs = [
        pl.BlockSpec((tr, cw), lambda j, t: (t, j)),
        pl.BlockSpec((H, cw), lambda j, t: (jnp.maximum(t * rb - 1, 0), j)),
        pl.BlockSpec((tr, cw), lambda j, t: (t, j + nc)),
        pl.BlockSpec((H, cw), lambda j, t: (jnp.maximum(t * rb - 1, 0), j + nc)),
        pl.BlockSpec((K, cw), lambda j, t: (0, j)),
        pl.BlockSpec((K, cw), lambda j, t: (0, j + nc)),
        pl.BlockSpec((1, cw), lambda j, t: (0, j)),
        pl.BlockSpec((1, cw), lambda j, t: (0, j + nc)),
    ]
    return _call(body, name, (nc, T // tr), in_specs, pl.BlockSpec((tr, cw), lambda j, t: (t, j)),
                 jax.ShapeDtypeStruct((T, F), BF16), (z, z, z, z, dw, dw, b, b),
                 scratch=[pltpu.VMEM((len(groups), P + tr, lw), F32)] * 2, side=side)


def _ffn_mid_bwd(z, da, dw, b, name, side=None):
    T, F2 = z.shape
    F = F2 // 2
    K = dw.shape[0]
    H, P = FFN_HALO, FFN_PAD
    tr = _tile(T, FFN_ROWS, H)
    cw = _tile(F, 512, 128)
    nc = F // cw
    nt = T // tr
    rb = tr // H
    rc = _tile(tr, FFN_CHUNK, 16)
    ext = rc + P
    lw, groups = _lane_groups(cw)

    def body(gp_ref, gc_ref, gn_ref, up_ref, uc_ref, un_ref, dac_ref, dan_ref, wg_ref, wu_ref, bg_ref, bu_ref,
             dzg_ref, dzu_ref, dwg_ref, dwu_ref, dbg_ref, dbu_ref, xg_s, xu_s, da_s, dg_s, du_s, acc_s):
        t = pl.program_id(1)
        first = t == 0
        last = t == nt - 1

        @pl.when(first)
        def _():
            for r in (dwg_ref, dwu_ref, dbg_ref, dbu_ref):
                r[...] = jnp.zeros_like(r)

        def fill(s, p_ref, c_ref, n_ref):
            for g, ls in enumerate(groups):
                s[g, 0:P, :] = jnp.where(first, 0.0, p_ref[:, ls].astype(F32)[H - P:H])
                s[g, P:P + tr, :] = c_ref[:, ls].astype(F32)
                s[g, P + tr:2 * P + tr, :] = jnp.where(last, 0.0, n_ref[:, ls].astype(F32)[0:P])

        fill(xg_s, gp_ref, gc_ref, gn_ref)
        fill(xu_s, up_ref, uc_ref, un_ref)
        for g, ls in enumerate(groups):
            da_s[g, 0:tr, :] = dac_ref[:, ls].astype(F32)
            da_s[g, tr:tr + P, :] = jnp.where(last, 0.0, dan_ref[:, ls].astype(F32)[0:P])
        acc_s[...] = jnp.zeros_like(acc_s)

        def fold(v):
            return jnp.sum(v.reshape(rc // 8, 8, lw), axis=0)

        def chunk(ci, carry):
            r0 = pl.multiple_of(ci * rc, rc)
            for g, ls in enumerate(groups):
                def conv_ext(s, w_ref, b_ref):
                    acc = b_ref[:, ls] + w_ref[K - 1:K, ls] * s[g, pl.ds(r0 + P, ext), :]
                    for k in range(K - 1):
                        acc = acc + w_ref[k:k + 1, ls] * s[g, pl.ds(r0 + P - (K - 1 - k), ext), :]
                    return acc

                gv = conv_ext(xg_s, wg_ref, bg_ref)
                uv = conv_ext(xu_s, wu_ref, bu_ref)
                dav = da_s[g, pl.ds(r0, ext), :]
                sg = _sigmoid(gv)
                dg = dav * uv * (sg * (1.0 + gv * (1.0 - sg)))
                du = dav * (gv * sg)
                dg_s[g] = dg
                du_s[g] = du
                for idx, (d_s, dval, x_s, w_ref, dz_ref) in enumerate(
                        ((dg_s, dg, xg_s, wg_ref, dzg_ref), (du_s, du, xu_s, wu_ref, dzu_ref))):
                    cur = dval[:rc]
                    acc_s[idx, K, g] += fold(cur)
                    for k in range(K):
                        acc_s[idx, k, g] += fold(cur * x_s[g, pl.ds(r0 + P - (K - 1 - k), rc), :])
                    dz = w_ref[K - 1:K, ls] * cur
                    for k in range(K - 1):
                        dz = dz + w_ref[k:k + 1, ls] * d_s[g, pl.ds(K - 1 - k, rc), :]
                    dz_ref[pl.ds(r0, rc), ls] = dz.astype(BF16)
            return carry

        lax.fori_loop(0, tr // rc, chunk, 0, unroll=2)
        for idx, (dw_ref, db_ref) in enumerate(((dwg_ref, dbg_ref), (dwu_ref, dbu_ref))):
            for g, ls in enumerate(groups):
                db_ref[:, ls] += jnp.sum(acc_s[idx, K, g], axis=0, keepdims=True)
                dw_ref[:, ls] += jnp.concatenate(
                    [jnp.sum(acc_s[idx, k, g], axis=0, keepdims=True) for k in range(K)], axis=0)

    def prev(t):
        return jnp.maximum(t * rb - 1, 0)

    def nxt(t):
        return jnp.minimum((t + 1) * rb, T // H - 1)

    def zspecs(off):
        return [pl.BlockSpec((H, cw), lambda j, t: (prev(t), j + off)),
                pl.BlockSpec((tr, cw), lambda j, t: (t, j + off)),
                pl.BlockSpec((H, cw), lambda j, t: (nxt(t), j + off))]

    in_specs = zspecs(0) + zspecs(nc) + [
        pl.BlockSpec((tr, cw), lambda j, t: (t, j)),
        pl.BlockSpec((H, cw), lambda j, t: (nxt(t), j)),
        pl.BlockSpec((K, cw), lambda j, t: (0, j)),
        pl.BlockSpec((K, cw), lambda j, t: (0, j + nc)),
        pl.BlockSpec((1, cw), lambda j, t: (0, j)),
        pl.BlockSpec((1, cw), lambda j, t: (0, j + nc)),
    ]
    out_shape = (jax.ShapeDtypeStruct((T, F), BF16), jax.ShapeDtypeStruct((T, F), BF16),
                 jax.ShapeDtypeStruct((K, F), F32), jax.ShapeDtypeStruct((K, F), F32),
                 jax.ShapeDtypeStruct((1, F), F32), jax.ShapeDtypeStruct((1, F), F32))
    out_specs = (pl.BlockSpec((tr, cw), lambda j, t: (t, j)), pl.BlockSpec((tr, cw), lambda j, t: (t, j)),
                 pl.BlockSpec((K, cw), lambda j, t: (0, j)), pl.BlockSpec((K, cw), lambda j, t: (0, j)),
                 pl.BlockSpec((1, cw), lambda j, t: (0, j)), pl.BlockSpec((1, cw), lambda j, t: (0, j)))
    ng = len(groups)
    scratch = [pltpu.VMEM((ng, 2 * P + tr, lw), F32), pltpu.VMEM((ng, 2 * P + tr, lw), F32),
               pltpu.VMEM((ng, P + tr, lw), F32), pltpu.VMEM((ng, ext, lw), F32), pltpu.VMEM((ng, ext, lw), F32),
               pltpu.VMEM((2, K + 1, ng, 8, lw), F32)]
    res = _call(body, name, (nc, nt), in_specs, out_specs, out_shape, (z, z, z, z, z, z, da, da, dw, dw, b, b),
                scratch=scratch, side=side)
    dzg, dzu, dwg, dwu, dbg, dbu = res[:6]
    return (dzg, dzu, jnp.concatenate([dwg, dwu], axis=1), jnp.concatenate([dbg, dbu], axis=1)) + tuple(res[6:])


CONV_HALO = 32
CONV_ROWS = 512
CONV_CHUNK = 64


def _glu_window(s, p_ref, halo_ref, first, D, groups, lw, H, tr):
    for g, ls in enumerate(groups):
        gs = pl.ds(D + g * lw, lw)
        s[g, 0:H, :] = jnp.where(first, 0.0, halo_ref[:, ls].astype(F32) * _sigmoid(halo_ref[:, gs].astype(F32)))
        s[g, H:H + tr, :] = p_ref[:, ls].astype(F32) * _sigmoid(p_ref[:, gs].astype(F32))


def _convmix_fwd(p, dw, dwb, lng, lnb, name, side=None):
    T, D2 = p.shape
    D = D2 // 2
    K = dw.shape[0]
    H = CONV_HALO
    tr = _tile(T, CONV_ROWS, H)
    rb = tr // H
    rc = _tile(tr, CONV_CHUNK, 16)
    lw, groups = _lane_groups(D)

    def body(pc_ref, ph_ref, w_ref, wb_ref, lng_ref, lnb_ref, zc_ref, s_ref, zg_s):
        first = pl.program_id(0) == 0
        _glu_window(zg_s, pc_ref, ph_ref, first, D, groups, lw, H, tr)

        def chunk(ci, carry):
            r0 = pl.multiple_of(ci * rc, rc)
            for g, ls in enumerate(groups):
                acc = wb_ref[:, ls] + w_ref[K - 1:K, ls] * zg_s[g, pl.ds(pl.multiple_of(r0 + H, 8), rc), :]
                for k in range(K - 1):
                    acc = acc + w_ref[k:k + 1, ls] * zg_s[g, pl.ds(r0 + H - (K - 1 - k), rc), :]
                zc_ref[pl.ds(r0, rc), ls] = acc
            return carry

        lax.fori_loop(0, tr // rc, chunk, 0)
        acc = zc_ref[...]
        mu = jnp.mean(acc, axis=-1, keepdims=True)
        xc = acc - mu
        rstd = lax.rsqrt(jnp.mean(xc * xc, axis=-1, keepdims=True) + NORM_EPS)
        ln = xc * rstd * lng_ref[...] + lnb_ref[...]
        s_ref[...] = (ln * _sigmoid(ln)).astype(BF16)

    in_specs = [pl.BlockSpec((tr, D2), lambda t: (t, 0)),
                pl.BlockSpec((H, D2), lambda t: (jnp.maximum(t * rb - 1, 0), 0)),
                pl.BlockSpec((K, D), lambda t: (0, 0)), _vec_spec(D), _vec_spec(D), _vec_spec(D)]
    row = pl.BlockSpec((tr, D), lambda t: (t, 0))
    return _call(body, name, (T // tr,), in_specs, (row, row),
                 (jax.ShapeDtypeStruct((T, D), F32), jax.ShapeDtypeStruct((T, D), BF16)), (p, p, dw, dwb, lng, lnb),
                 scratch=[pltpu.VMEM((len(groups), H + tr, lw), F32)], side=side)


def _convmix_bwd(ds, zc, p, dw, lng, lnb, name, side=None):
    T, D = zc.shape
    D2 = 2 * D
    K = dw.shape[0]
    H = CONV_HALO
    tr = _tile(T, CONV_ROWS, H)
    rb = tr // H
    nt = T // tr
    ext = tr + H
    rc = _tile(tr, CONV_CHUNK, 16)
    lw, groups = _lane_groups(D)

    def body(dsc_ref, dsn_ref, zcc_ref, zcn_ref, pp_ref, pc_ref, w_ref, lng_ref, lnb_ref,
             dp_ref, dbin_ref, ddw_ref, ddwb_ref, dlng_ref, dlnb_ref, zg_s, dzc_s, acc_s):
        t = pl.program_id(0)
        first = t == 0
        last = t == nt - 1

        @pl.when(first)
        def _():
            for r in (dbin_ref, ddw_ref, ddwb_ref, dlng_ref, dlnb_ref):
                r[...] = jnp.zeros_like(r)

        dsv = jnp.concatenate([dsc_ref[...].astype(F32), jnp.where(last, 0.0, dsn_ref[...].astype(F32))], axis=0)
        zcv = jnp.concatenate([zcc_ref[...], zcn_ref[...]], axis=0)
        lg = lng_ref[...]
        mu = jnp.mean(zcv, axis=-1, keepdims=True)
        xc = zcv - mu
        rstd = lax.rsqrt(jnp.mean(xc * xc, axis=-1, keepdims=True) + NORM_EPS)
        xh = xc * rstd
        ln = xh * lg + lnb_ref[...]
        sg = _sigmoid(ln)
        dln = dsv * (sg * (1.0 + ln * (1.0 - sg)))
        dlnc = _rows(dln, 0, tr)
        dlnb_ref[...] += jnp.sum(dlnc, axis=0, keepdims=True)
        dlng_ref[...] += jnp.sum(dlnc * _rows(xh, 0, tr), axis=0, keepdims=True)
        dxh = dln * lg
        dzc = rstd * (dxh - jnp.mean(dxh, axis=-1, keepdims=True) - xh * jnp.mean(dxh * xh, axis=-1, keepdims=True))
        dzcc = _rows(dzc, 0, tr)
        ddwb_ref[...] += jnp.sum(dzcc, axis=0, keepdims=True)

        for g in range(len(groups)):
            dzc_s[g] = dzc[:, g * lw:(g + 1) * lw]
        _glu_window(zg_s, pc_ref, pp_ref, first, D, groups, lw, H, tr)
        acc_s[...] = jnp.zeros_like(acc_s)

        def fold(v):
            return jnp.sum(v.reshape(rc // 8, 8, lw), axis=0)

        def chunk(ci, carry):
            r0 = pl.multiple_of(ci * rc, rc)
            for g, ls in enumerate(groups):
                gs = pl.ds(D + g * lw, lw)
                cur = dzc_s[g, pl.ds(r0, rc), :]
                dzg = w_ref[K - 1:K, ls] * cur
                for k in range(K):
                    acc_s[k, g] += fold(cur * zg_s[g, pl.ds(r0 + H - (K - 1 - k), rc), :])
                    if k < K - 1:
                        dzg = dzg + w_ref[k:k + 1, ls] * dzc_s[g, pl.ds(r0 + (K - 1 - k), rc), :]
                ac = pc_ref[pl.ds(r0, rc), ls].astype(F32)
                sgg = _sigmoid(pc_ref[pl.ds(r0, rc), gs].astype(F32))
                dpa = dzg * sgg
                dpg = dzg * ac * sgg * (1.0 - sgg)
                dp_ref[pl.ds(r0, rc), ls] = dpa.astype(BF16)
                dp_ref[pl.ds(r0, rc), gs] = dpg.astype(BF16)
                acc_s[K, g] += fold(dpa)
                acc_s[K + 1, g] += fold(dpg)
            return carry

        lax.fori_loop(0, tr // rc, chunk, 0)
        for g, ls in enumerate(groups):
            ddw_ref[:, ls] += jnp.concatenate([jnp.sum(acc_s[k, g], axis=0, keepdims=True) for k in range(K)], axis=0)
            dbin_ref[:, ls] += jnp.sum(acc_s[K, g], axis=0, keepdims=True)
            dbin_ref[:, pl.ds(D + g * lw, lw)] += jnp.sum(acc_s[K + 1, g], axis=0, keepdims=True)

    def prev(t):
        return jnp.maximum(t * rb - 1, 0)

    def nxt(t):
        return jnp.minimum((t + 1) * rb, T // H - 1)

    in_specs = [pl.BlockSpec((tr, D), lambda t: (t, 0)), pl.BlockSpec((H, D), lambda t: (nxt(t), 0)),
                pl.BlockSpec((tr, D), lambda t: (t, 0)), pl.BlockSpec((H, D), lambda t: (nxt(t), 0)),
                pl.BlockSpec((H, D2), lambda t: (prev(t), 0)), pl.BlockSpec((tr, D2), lambda t: (t, 0)),
                pl.BlockSpec((K, D), lambda t: (0, 0)), _vec_spec(D), _vec_spec(D)]
    out_shape = (jax.ShapeDtypeStruct((T, D2), BF16), jax.ShapeDtypeStruct((1, D2), F32),
                 jax.ShapeDtypeStruct((K, D), F32), jax.ShapeDtypeStruct((1, D), F32),
                 jax.ShapeDtypeStruct((1, D), F32), jax.ShapeDtypeStruct((1, D), F32))
    out_specs = (pl.BlockSpec((tr, D2), lambda t: (t, 0)), _vec_spec(D2), pl.BlockSpec((K, D), lambda t: (0, 0)),
                 _vec_spec(D), _vec_spec(D), _vec_spec(D))
    ng = len(groups)
    scratch = [pltpu.VMEM((ng, H + tr, lw), F32), pltpu.VMEM((ng, ext, lw), F32), pltpu.VMEM((K + 2, ng, 8, lw), F32)]
    return _call(body, name, (nt,), in_specs, out_specs, out_shape, (ds, ds, zc, zc, p, p, dw, lng, lnb),
                 scratch=scratch, side=side)


_INV_SQRT2 = 1.0 / math.sqrt(2.0)
_INV_SQRT2PI = 1.0 / math.sqrt(2.0 * math.pi)


def _gelu(x):
    return 0.5 * x * (1.0 + lax.erf(x * _INV_SQRT2))


def _gelu_grad(x):
    return 0.5 * (1.0 + lax.erf(x * _INV_SQRT2)) + x * jnp.exp(-0.5 * x * x) * _INV_SQRT2PI


def _tril_mask(n, transposed=False):
    r = lax.broadcasted_iota(jnp.int32, (n, n), 0)
    c = lax.broadcasted_iota(jnp.int32, (n, n), 1)
    return (r <= c) if transposed else (r >= c)


def _sgu_fwd(p, lng, lnb, ws, bs_t, name):
    T, S2 = p.shape
    S = S2 // 2
    G, C, _ = ws.shape
    gd = S // G
    cpt = 2 if (T // C) % 2 == 0 else 1
    tr = C * cpt

    def body(p_ref, lng_ref, lnb_ref, ws_ref, bst_ref, m_ref):
        mask = _tril_mask(C)
        lg, lb = lng_ref[...], lnb_ref[...]
        bst = bst_ref[...]
        for ci in range(cpt):
            pc = p_ref[ci * C:(ci + 1) * C, :].astype(F32)
            z = _gelu(pc)
            u, v = z[:, :S], z[:, S:]
            mu = jnp.mean(v, axis=-1, keepdims=True)
            xc = v - mu
            rstd = lax.rsqrt(jnp.mean(xc * xc, axis=-1, keepdims=True) + NORM_EPS)
            vn = (xc * rstd * lg + lb).astype(BF16)
            outs = []
            for g in range(G):
                wm = jnp.where(mask, ws_ref[g], 0.0).astype(BF16)
                vs = _dot(wm, vn[:, g * gd:(g + 1) * gd], 1, 0) + bst[:, g:g + 1]
                outs.append(u[:, g * gd:(g + 1) * gd] * vs)
            m_ref[ci * C:(ci + 1) * C, :] = jnp.concatenate(outs, axis=1).astype(BF16)

    in_specs = [pl.BlockSpec((tr, S2), lambda t: (t, 0)), _vec_spec(S), _vec_spec(S),
                pl.BlockSpec((G, C, C), lambda t: (0, 0, 0)), pl.BlockSpec((C, G), lambda t: (0, 0))]
    return _call(body, name, (T // tr,), in_specs, pl.BlockSpec((tr, S), lambda t: (t, 0)),
                 jax.ShapeDtypeStruct((T, S), BF16), (p, lng, lnb, ws, bs_t))


def _sgu_bwd(p, dm, lng, lnb, ws, bs_t, name, side=None):
    T, S2 = p.shape
    S = S2 // 2
    G, C, _ = ws.shape
    gd = S // G
    nt = T // C

    def body(p_ref, dm_ref, lng_ref, lnb_ref, ws_ref, bst_ref, dp_ref, dbin_ref, dlng_ref, dlnb_ref, dws_ref, dbst_ref):
        @pl.when(pl.program_id(0) == 0)
        def _():
            for r in (dbin_ref, dlng_ref, dlnb_ref, dws_ref, dbst_ref):
                r[...] = jnp.zeros_like(r)

        mask = _tril_mask(C)
        lg, lb = lng_ref[...], lnb_ref[...]
        bst = bst_ref[...]
        pc = p_ref[...].astype(F32)
        dmv = dm_ref[...].astype(F32)
        z = _gelu(pc)
        u, v = z[:, :S], z[:, S:]
        mu = jnp.mean(v, axis=-1, keepdims=True)
        xc = v - mu
        rstd = lax.rsqrt(jnp.mean(xc * xc, axis=-1, keepdims=True) + NORM_EPS)
        vh = xc * rstd
        vn = (vh * lg + lb).astype(BF16)
        dus, dvns, dbcols = [], [], []
        for g in range(G):
            sl = slice(g * gd, (g + 1) * gd)
            wm = jnp.where(mask, ws_ref[g], 0.0).astype(BF16)
            vs = _dot(wm, vn[:, sl], 1, 0) + bst[:, g:g + 1]
            dmg = dmv[:, sl]
            dus.append(dmg * vs)
            dvs = dmg * u[:, sl]
            dbcols.append(jnp.sum(dvs, axis=-1, keepdims=True))
            dvsb = dvs.astype(BF16)
            dws_ref[g] += jnp.where(mask, _dot(dvsb, vn[:, sl], 1, 1), 0.0)
            dvns.append(_dot(wm, dvsb, 0, 0))
        dbst_ref[...] += jnp.concatenate(dbcols, axis=1)
        dvn = jnp.concatenate(dvns, axis=1)
        dlnb_ref[...] += jnp.sum(dvn, axis=0, keepdims=True)
        dlng_ref[...] += jnp.sum(dvn * vh, axis=0, keepdims=True)
        dvh = dvn * lg
        dv = rstd * (dvh - jnp.mean(dvh, axis=-1, keepdims=True) - vh * jnp.mean(dvh * vh, axis=-1, keepdims=True))
        dz = jnp.concatenate([jnp.concatenate(dus, axis=1), dv], axis=1)
        dpv = dz * _gelu_grad(pc)
        dp_ref[...] = dpv.astype(BF16)
        dbin_ref[...] += jnp.sum(dpv, axis=0, keepdims=True)

    in_specs = [pl.BlockSpec((C, S2), lambda t: (t, 0)), pl.BlockSpec((C, S), lambda t: (t, 0)),
                _vec_spec(S), _vec_spec(S), pl.BlockSpec((G, C, C), lambda t: (0, 0, 0)),
                pl.BlockSpec((C, G), lambda t: (0, 0))]
    out_shape = (jax.ShapeDtypeStruct((T, S2), BF16), jax.ShapeDtypeStruct((1, S2), F32),
                 jax.ShapeDtypeStruct((1, S), F32), jax.ShapeDtypeStruct((1, S), F32),
                 jax.ShapeDtypeStruct((G, C, C), F32), jax.ShapeDtypeStruct((C, G), F32))
    out_specs = (pl.BlockSpec((C, S2), lambda t: (t, 0)), _vec_spec(S2), _vec_spec(S), _vec_spec(S),
                 pl.BlockSpec((G, C, C), lambda t: (0, 0, 0)), pl.BlockSpec((C, G), lambda t: (0, 0)))
    return _call(body, name, (nt,), in_specs, out_specs, out_shape, (p, dm, lng, lnb, ws, bs_t), side=side)


def _alibi_slope(h, n_heads):
    return 2.0 ** (-8.0 * (h + 1) / n_heads)


def _attn_bias(n_heads):
    B = BLOCK
    qi = jnp.arange(B)[:, None]
    kj = jnp.arange(2 * B)[None, :]
    dist = qi + B - kj
    band = (dist >= 0) & (dist < B)
    slopes = jnp.array([_alibi_slope(h, n_heads) for h in range(n_heads)], F32)
    out = []
    for valid in (band & (kj >= B), band):
        b = jnp.where(valid[None], -slopes[:, None, None] * dist.astype(F32)[None], NEG_INF)
        b = jnp.where(kj[None] == 0, 0.0, b)
        out.append(b.reshape(n_heads // 2, 2, B, 2 * B).transpose(0, 2, 1, 3).reshape(n_heads // 2, B, 4 * B))
    return jnp.stack(out)


def _sink_rows(sinks):
    return jnp.pad(sinks.astype(F32)[:, None], ((0, 0), (0, BLOCK - 1)))


def _block_diag2(x):
    z = jnp.zeros_like(x)
    return jnp.concatenate([jnp.concatenate([x, z], axis=1), jnp.concatenate([z, x], axis=1)], axis=0)


def _attn_core(q_ref, kp_ref, kc_ref, vp_ref, vc_ref, sink_ref, bias_ref, n_heads):
    B, hd = BLOCK, HEAD_DIM
    ppg = n_heads // N_KV // 2
    pairs = range(n_heads // 2)
    row0 = lax.broadcasted_iota(jnp.int32, (2 * B, N_KV * hd), 0) == 0
    k2 = jnp.concatenate([kp_ref[...], kc_ref[...]], axis=0)
    v2 = jnp.concatenate([vp_ref[...], vc_ref[...]], axis=0)
    k2 = jnp.where(row0, jnp.zeros_like(k2), k2)
    v2 = jnp.where(row0, jnp.zeros_like(v2), v2)
    scale = hd ** -0.5
    kks = [_block_diag2(k2[:, kv * hd:(kv + 1) * hd] * scale) for kv in range(N_KV)]
    vvs = [_block_diag2(v2[:, kv * hd:(kv + 1) * hd]) for kv in range(N_KV)]
    first_rows = lax.broadcasted_iota(jnp.int32, (4 * B, 4 * hd), 0) < 2 * B
    first_lanes = lax.broadcasted_iota(jnp.int32, (4 * B, 4 * hd), 1) < 2 * hd
    ones = (first_rows == first_lanes).astype(BF16)
    vxs = [jnp.concatenate([vv, ones], axis=1) for vv in vvs]
    ss = [_dot(q_ref[:, pr * 2 * hd:(pr + 1) * 2 * hd], kks[pr // ppg], 1, 1) + bias_ref[pr] for pr in pairs]
    halves = []
    for h in range(n_heads):
        s = ss[h // 2][:, (h % 2) * 2 * B:(h % 2 + 1) * 2 * B]
        halves.append(jnp.concatenate([s[:, :B] + sink_ref[h:h + 1, :], s[:, B:]], axis=1))
    mxs = [jnp.max(s, axis=-1, keepdims=True) for s in halves]
    es = [jnp.exp(s - m) for s, m in zip(halves, mxs)]
    rs = [_dot(jnp.concatenate([es[2 * pr], es[2 * pr + 1]], axis=1).astype(BF16), vxs[pr // ppg], 1, 0) for pr in pairs]
    return kks, vvs, es, rs


def _attn_fwd(qkv, sink_rows, bias, n_heads, name, side=None):
    T = qkv.shape[0]
    B = BLOCK
    hd = HEAD_DIM
    HQ = n_heads * hd
    KVW = N_KV * hd
    assert (n_heads // N_KV) % 2 == 0, "heads are processed in pairs that share a key/value head"
    nb = T // B
    kcol = HQ // KVW

    def body(q_ref, kp_ref, kc_ref, vp_ref, vc_ref, sink_ref, bias_ref, o_ref):
        _, _, _, rs = _attn_core(q_ref, kp_ref, kc_ref, vp_ref, vc_ref, sink_ref, bias_ref, n_heads)
        low = lax.broadcasted_iota(jnp.int32, (B, 2 * hd), 1) < hd
        for pr, r in enumerate(rs):
            inv = jnp.where(low, 1.0 / r[:, 2 * hd:4 * hd], 1.0 / r[:, 4 * hd:])
            o_ref[:, pr * 2 * hd:(pr + 1) * 2 * hd] = (r[:, :2 * hd] * inv).astype(BF16)

    def prev(n):
        return jnp.maximum(n - 1, 0)

    in_specs = [pl.BlockSpec((B, HQ), lambda n: (n, 0)),
                pl.BlockSpec((B, KVW), lambda n: (prev(n), kcol)), pl.BlockSpec((B, KVW), lambda n: (n, kcol)),
                pl.BlockSpec((B, KVW), lambda n: (prev(n), kcol + 1)), pl.BlockSpec((B, KVW), lambda n: (n, kcol + 1)),
                pl.BlockSpec((n_heads, B), lambda n: (0, 0)),
                pl.BlockSpec((None, n_heads // 2, B, 4 * B), lambda n: (jnp.minimum(n, 1), 0, 0, 0))]
    return _call(body, name, (nb,), in_specs, pl.BlockSpec((B, HQ), lambda n: (n, 0)),
                 jax.ShapeDtypeStruct((T, HQ), BF16), (qkv, qkv, qkv, qkv, qkv, sink_rows, bias), side=side)


def _attn_bwd(qkv, do, sink_rows, bias, n_heads, name, side=None):
    T = qkv.shape[0]
    B = BLOCK
    hd = HEAD_DIM
    HQ = n_heads * hd
    KVW = N_KV * hd
    group = n_heads // N_KV
    assert group % 2 == 0, "heads are processed in pairs that share a key/value head"
    ppg = group // 2
    nb = T // B
    kcol = HQ // KVW
    scale = hd ** -0.5

    def body(q_ref, kp_ref, kc_ref, vp_ref, vc_ref, do_ref, sink_ref, bias_ref,
             dq_ref, dk_ref, dv_ref, dbq_ref, dbk_ref, dbv_ref, dsink_ref, ck_ref, cv_ref):
        n = pl.program_id(0)
        first = n == 0

        @pl.when(first)
        def _():
            for r in (dbq_ref, dbk_ref, dbv_ref, dsink_ref, ck_ref, cv_ref):
                r[...] = jnp.zeros_like(r)

        @pl.when(n < nb)
        def _():
            pairs = range(n_heads // 2)
            heads = range(n_heads)
            lanes = [slice(pr * 2 * hd, (pr + 1) * 2 * hd) for pr in pairs]
            kks, vvs, es, rs = _attn_core(q_ref, kp_ref, kc_ref, vp_ref, vc_ref, sink_ref, bias_ref, n_heads)
            low = lax.broadcasted_iota(jnp.int32, (B, 2 * hd), 1) < hd
            slot0 = lax.broadcasted_iota(jnp.int32, (B, B), 1) == 0
            lane_head = lax.broadcasted_iota(jnp.int32, (2 * hd, 4 * hd), 0) < hd
            out_head = lax.broadcasted_iota(jnp.int32, (2 * hd, 4 * hd), 1) < 2 * hd
            ones2 = (lane_head == out_head).astype(BF16)
            invs = [1.0 / rs[h // 2][:, (2 + 2 * (h % 2)) * hd:(4 + 2 * (h % 2)) * hd] for h in heads]
            dops = [do_ref[:, lanes[pr]] for pr in pairs]
            qps = [q_ref[:, lanes[pr]] for pr in pairs]
            dos = [dops[pr].astype(F32) * rs[pr][:, :2 * hd] * jnp.where(low, invs[2 * pr], invs[2 * pr + 1])
                   for pr in pairs]
            his = [t.astype(BF16) for t in dos]
            los = [(t - hi.astype(F32)).astype(BF16) for t, hi in zip(dos, his)]
            dsums = [_dot(hi, ones2, 1, 0) + _dot(lo, ones2, 1, 0) for hi, lo in zip(his, los)]
            dps = [_dot(dops[pr], vvs[pr // ppg], 1, 1) for pr in pairs]
            wide = lambda a: jnp.concatenate([a, a], axis=1)
            p_s = [es[h] * wide(invs[h]) for h in heads]
            ds_s = [p_s[h] * (dps[h // 2][:, (h % 2) * 2 * B:(h % 2 + 1) * 2 * B]
                              - wide(dsums[h // 2][:, (h % 2) * 2 * hd:(h % 2 + 1) * 2 * hd])) for h in heads]
            for h in heads:
                dsink_ref[h:h + 1, :] += jnp.sum(ds_s[h][:, :B], axis=0, keepdims=True)

            def without_slot0(a):
                return jnp.concatenate([jnp.where(slot0, 0.0, a[:, :B]), a[:, B:]], axis=1).astype(BF16)

            ds_b = [without_slot0(a) for a in ds_s]
            p_b = [without_slot0(a) for a in p_s]
            dqs = [_dot(jnp.concatenate([ds_b[2 * pr], ds_b[2 * pr + 1]], axis=1), kks[pr // ppg], 1, 0) for pr in pairs]
            for pr in pairs:
                dq_ref[:, lanes[pr]] = dqs[pr].astype(BF16)
                dbq_ref[:, lanes[pr]] += jnp.sum(dqs[pr], axis=0, keepdims=True)
            keeps = [low, jnp.logical_not(low)]
            qms = [jnp.where(keeps[h % 2], qps[h // 2], jnp.zeros_like(qps[0])) for h in heads]
            dms = [jnp.where(keeps[h % 2], dops[h // 2], jnp.zeros_like(dops[0])) for h in heads]
            dkh = [_dot(ds_b[h], qms[h], 0, 0) for h in heads]
            dvh = [_dot(p_b[h], dms[h], 0, 0) for h in heads]
            dks, dvs = [], []
            for kv in range(N_KV):
                tk = functools.reduce(jnp.add, dkh[kv * group:(kv + 1) * group])
                tv = functools.reduce(jnp.add, dvh[kv * group:(kv + 1) * group])
                dks.append((tk[:, :hd] + tk[:, hd:]) * scale)
                dvs.append(tv[:, :hd] + tv[:, hd:])
            dk2 = jnp.concatenate(dks, axis=1)
            dv2 = jnp.concatenate(dvs, axis=1)
            dbk_ref[...] += jnp.sum(dk2, axis=0, keepdims=True)
            dbv_ref[...] += jnp.sum(dv2, axis=0, keepdims=True)
            dk_ref[...] = (ck_ref[...] + dk2[:B]).astype(BF16)
            dv_ref[...] = (cv_ref[...] + dv2[:B]).astype(BF16)
            ck_ref[...] = dk2[B:]
            cv_ref[...] = dv2[B:]

        @pl.when(n == nb)
        def _():
            dk_ref[...] = ck_ref[...].astype(BF16)
            dv_ref[...] = cv_ref[...].astype(BF16)

    def cur(n):
        return jnp.minimum(n, nb - 1)

    def prev(n):
        return jnp.maximum(cur(n) - 1, 0)

    def outp(n):
        return jnp.maximum(n - 1, 0)

    in_specs = [pl.BlockSpec((B, HQ), lambda n: (cur(n), 0)),
                pl.BlockSpec((B, KVW), lambda n: (prev(n), kcol)), pl.BlockSpec((B, KVW), lambda n: (cur(n), kcol)),
                pl.BlockSpec((B, KVW), lambda n: (prev(n), kcol + 1)), pl.BlockSpec((B, KVW), lambda n: (cur(n), kcol + 1)),
                pl.BlockSpec((B, HQ), lambda n: (cur(n), 0)), pl.BlockSpec((n_heads, B), lambda n: (0, 0)),
                pl.BlockSpec((None, n_heads // 2, B, 4 * B), lambda n: (jnp.minimum(n, 1), 0, 0, 0))]
    out_shape = (jax.ShapeDtypeStruct((T, HQ), BF16), jax.ShapeDtypeStruct((T, KVW), BF16),
                 jax.ShapeDtypeStruct((T, KVW), BF16), jax.ShapeDtypeStruct((1, HQ), F32),
                 jax.ShapeDtypeStruct((1, KVW), F32), jax.ShapeDtypeStruct((1, KVW), F32),
                 jax.ShapeDtypeStruct((n_heads, B), F32))
    out_specs = (pl.BlockSpec((B, HQ), lambda n: (cur(n), 0)), pl.BlockSpec((B, KVW), lambda n: (outp(n), 0)),
                 pl.BlockSpec((B, KVW), lambda n: (outp(n), 0)), _vec_spec(HQ), _vec_spec(KVW), _vec_spec(KVW),
                 pl.BlockSpec((n_heads, B), lambda n: (0, 0)))
    return _call(body, name, (nb + 1,), in_specs, out_specs, out_shape, (qkv, qkv, qkv, qkv, qkv, do, sink_rows, bias),
                 scratch=[pltpu.VMEM((B, KVW), F32), pltpu.VMEM((B, KVW), F32)], side=side)


def _sum8(r, name):
    _, R, C = r.shape
    tr = _tile(R, 512, 16)

    def body(r_ref, o_ref):
        acc = r_ref[0].astype(F32)
        for d in range(1, N_DEV):
            acc = acc + r_ref[d].astype(F32)
        o_ref[...] = acc

    return _call(body, name, (R // tr,), [pl.BlockSpec((N_DEV, tr, C), lambda i: (0, i, 0))],
                 pl.BlockSpec((tr, C), lambda i: (i, 0)), jax.ShapeDtypeStruct((R, C), F32), (r,))


def _adamw(w, g, m, v, name):
    R, C = w.shape
    tr = _tile(R, 512, 8)
    c1 = 1.0 - ADAM_B1 ** ADAM_STEP
    c2 = 1.0 - ADAM_B2 ** ADAM_STEP

    def body(w_ref, g_ref, m_ref, v_ref, d_ref, nm_ref, nv_ref):
        gv = g_ref[...]
        nm = ADAM_B1 * m_ref[...] + (1.0 - ADAM_B1) * gv
        nv = ADAM_B2 * v_ref[...] + (1.0 - ADAM_B2) * (gv * gv)
        nm_ref[...] = nm
        nv_ref[...] = nv
        d_ref[...] = -ADAM_LR * ((nm / c1) / (jnp.sqrt(nv / c2) + ADAM_EPS) + ADAM_WD * w_ref[...])

    spec = pl.BlockSpec((tr, C), lambda i: (i, 0))
    shp = jax.ShapeDtypeStruct((R, C), F32)
    return _call(body, name, (R // tr,), [spec] * 4, (spec,) * 3, (shp, shp, shp), (w, g, m, v))


def _adamw_nd(w, g, m, v, name):
    shape = w.shape
    c = shape[-1]
    f = lambda a: a.reshape(-1, c)
    d, nm, nv = _adamw(f(w), f(g), f(m), f(v), name)
    return d.reshape(shape), nm.reshape(shape), nv.reshape(shape)


def _pack(arrs, width):
    flat = jnp.concatenate([a.reshape(-1).astype(F32) for a in arrs])
    n = flat.shape[0]
    quantum = 8 * width
    total = -(-n // quantum) * quantum
    return jnp.pad(flat, (0, total - n)).reshape(-1, width)


def _unpack(flat, shapes):
    out, off = [], 0
    for s in shapes:
        n = math.prod(s)
        out.append(flat[off:off + n].reshape(s))
        off += n
    return out


def kernel(x, c, norm1_g, norm2_g, ada_w, ada_b, attn_wqkv, attn_bqkv, attn_sinks, attn_wo, attn_bo, conv_w_in, conv_b_in, conv_dw, conv_dw_b, conv_ln_g, conv_ln_b, conv_w_out, conv_b_out, sgu_w_in, sgu_b_in, sgu_ln_g, sgu_ln_b, sgu_ws, sgu_bs, sgu_w_out, sgu_b_out, ffn_w_in, ffn_dw, ffn_dw_b, ffn_w_out, final_g, loss_target, m_norm1_g, m_norm2_g, m_ada_w, m_ada_b, m_attn_wqkv, m_attn_bqkv, m_attn_sinks, m_attn_wo, m_attn_bo, m_conv_w_in, m_conv_b_in, m_conv_dw, m_conv_dw_b, m_conv_ln_g, m_conv_ln_b, m_conv_w_out, m_conv_b_out, m_sgu_w_in, m_sgu_b_in, m_sgu_ln_g, m_sgu_ln_b, m_sgu_ws, m_sgu_bs, m_sgu_w_out, m_sgu_b_out, m_ffn_w_in, m_ffn_dw, m_ffn_dw_b, m_ffn_w_out, m_final_g, v_norm1_g, v_norm2_g, v_ada_w, v_ada_b, v_attn_wqkv, v_attn_bqkv, v_attn_sinks, v_attn_wo, v_attn_bo, v_conv_w_in, v_conv_b_in, v_conv_dw, v_conv_dw_b, v_conv_ln_g, v_conv_ln_b, v_conv_w_out, v_conv_b_out, v_sgu_w_in, v_sgu_b_in, v_sgu_ln_g, v_sgu_ln_b, v_sgu_ws, v_sgu_bs, v_sgu_w_out, v_sgu_b_out, v_ffn_w_in, v_ffn_dw, v_ffn_dw_b, v_ffn_w_out, v_final_g):
    W = dict(norm1_g=norm1_g, norm2_g=norm2_g, ada_w=ada_w, ada_b=ada_b, attn_wqkv=attn_wqkv, attn_bqkv=attn_bqkv, attn_sinks=attn_sinks, attn_wo=attn_wo, attn_bo=attn_bo, conv_w_in=conv_w_in, conv_b_in=conv_b_in, conv_dw=conv_dw, conv_dw_b=conv_dw_b, conv_ln_g=conv_ln_g, conv_ln_b=conv_ln_b, conv_w_out=conv_w_out, conv_b_out=conv_b_out, sgu_w_in=sgu_w_in, sgu_b_in=sgu_b_in, sgu_ln_g=sgu_ln_g, sgu_ln_b=sgu_ln_b, sgu_ws=sgu_ws, sgu_bs=sgu_bs, sgu_w_out=sgu_w_out, sgu_b_out=sgu_b_out, ffn_w_in=ffn_w_in, ffn_dw=ffn_dw, ffn_dw_b=ffn_dw_b, ffn_w_out=ffn_w_out, final_g=final_g)
    MOM = dict(norm1_g=m_norm1_g, norm2_g=m_norm2_g, ada_w=m_ada_w, ada_b=m_ada_b, attn_wqkv=m_attn_wqkv, attn_bqkv=m_attn_bqkv, attn_sinks=m_attn_sinks, attn_wo=m_attn_wo, attn_bo=m_attn_bo, conv_w_in=m_conv_w_in, conv_b_in=m_conv_b_in, conv_dw=m_conv_dw, conv_dw_b=m_conv_dw_b, conv_ln_g=m_conv_ln_g, conv_ln_b=m_conv_ln_b, conv_w_out=m_conv_w_out, conv_b_out=m_conv_b_out, sgu_w_in=m_sgu_w_in, sgu_b_in=m_sgu_b_in, sgu_ln_g=m_sgu_ln_g, sgu_ln_b=m_sgu_ln_b, sgu_ws=m_sgu_ws, sgu_bs=m_sgu_bs, sgu_w_out=m_sgu_w_out, sgu_b_out=m_sgu_b_out, ffn_w_in=m_ffn_w_in, ffn_dw=m_ffn_dw, ffn_dw_b=m_ffn_dw_b, ffn_w_out=m_ffn_w_out, final_g=m_final_g)
    VAR = dict(norm1_g=v_norm1_g, norm2_g=v_norm2_g, ada_w=v_ada_w, ada_b=v_ada_b, attn_wqkv=v_attn_wqkv, attn_bqkv=v_attn_bqkv, attn_sinks=v_attn_sinks, attn_wo=v_attn_wo, attn_bo=v_attn_bo, conv_w_in=v_conv_w_in, conv_b_in=v_conv_b_in, conv_dw=v_conv_dw, conv_dw_b=v_conv_dw_b, conv_ln_g=v_conv_ln_g, conv_ln_b=v_conv_ln_b, conv_w_out=v_conv_w_out, conv_b_out=v_conv_b_out, sgu_w_in=v_sgu_w_in, sgu_b_in=v_sgu_b_in, sgu_ln_g=v_sgu_ln_g, sgu_ln_b=v_sgu_ln_b, sgu_ws=v_sgu_ws, sgu_bs=v_sgu_bs, sgu_w_out=v_sgu_w_out, sgu_b_out=v_sgu_b_out, ffn_w_in=v_ffn_w_in, ffn_dw=v_ffn_dw, ffn_dw_b=v_ffn_dw_b, ffn_w_out=v_ffn_w_out, final_g=v_final_g)
    ORDER = list(W)

    _, T, D = x.shape
    depth = norm1_g.shape[0]
    n_heads = D // HEAD_DIM
    me = 4 * lax.axis_index("x") + 2 * lax.axis_index("y") + lax.axis_index("c")
    x0 = x.reshape(T, D)
    tgt = loss_target.reshape(T, D)

    small_sharded = ["attn_bqkv", "attn_bo", "conv_dw", "sgu_b_in", "sgu_ln_g", "sgu_ln_b", "sgu_b_out", "ffn_dw"]
    s_in = [c] + [W[n] for n in small_sharded]
    s_shapes = [a.shape for a in s_in]
    gathered_small = _all_gather(_pack(s_in, 128), "gather_small").reshape(N_DEV, -1)
    s_offs = [sum(math.prod(s) for s in s_shapes[:i]) for i in range(len(s_shapes))]

    def full(idx):
        shp = s_shapes[idx]
        a = gathered_small[:, s_offs[idx]:s_offs[idx] + math.prod(shp)].reshape((N_DEV,) + shp)
        return jnp.moveaxis(a, 0, -2).reshape(shp[:-1] + (N_DEV * shp[-1],))

    c_all = full(0).reshape(N_DEV, D)
    F_small = {n: full(1 + i) for i, n in enumerate(small_sharded)}
    attn_bias = _attn_bias(n_heads)

    c_act = c_all * jax.nn.sigmoid(c_all)
    c_pad = jnp.pad(c_act, ((0, 8), (0, 0))).astype(BF16)
    n_ada = ada_w.shape[-1]
    ada_cols = lax.dynamic_slice_in_dim(ada_b, me * n_ada, n_ada, axis=1)
    mod_loc = jnp.stack([_mm(c_pad, ada_w[i].astype(BF16), "nn", "ada_mod", bias=ada_cols[i:i + 1])
                         for i in range(depth)])
    mod_all = _all_gather(mod_loc, "gather_mod")
    mod_mine = lax.dynamic_index_in_dim(mod_all, me, axis=2, keepdims=False)
    mod = jnp.transpose(mod_mine, (1, 0, 2)).reshape(depth, 6, 1, D)

    mixer_names = {0: ("attn_wqkv", "attn_wo"), 1: ("conv_w_in", "conv_w_out"), 2: ("sgu_w_in", "sgu_w_out")}

    def piece(part, i, which):
        w_in, w_out = ("ffn_w_in", "ffn_w_out") if part == "ffn" else mixer_names[i % 3]
        l = i if part == "ffn" else i // 3
        return (w_in, l, W[w_in].shape[-1]) if which == "in" else (w_out, l, W[w_out].shape[1])

    def both(part, i):
        return [(part, i, "in"), (part, i, "out")] if 0 <= i < depth else []

    def local_weight(key):
        n, l, _ = piece(*key)
        return (W[n][l].T if key[2] == "in" else W[n][l]).astype(BF16)

    WF = {}

    def land_weights(keys, gathered):
        for key, g in zip(keys, gathered):
            n, l, r = piece(*key)
            WF[(n, l)] = g.reshape(N_DEV * r, D)

    def with_gather(keys, run):
        if not keys:
            return run(None)
        *outs, landed = run(("gather", [local_weight(k) for k in keys]))
        land_weights(keys, landed)
        return outs[0] if len(outs) == 1 else tuple(outs)

    def vec(a):
        return a.reshape(1, -1)

    land_weights(both("mix", 0)[:1], _exchange("gather", [local_weight(k) for k in both("mix", 0)[:1]], "gather_first"))

    saved = []
    xs, y_prev, gate_prev = x0, None, None
    for i in range(depth):
        sh1, sc1, g1, sh2, sc2, g2 = [mod[i, k] for k in range(6)]
        kind, j = i % 3, i // 3
        st = dict(kind=kind, j=j)
        if y_prev is None:
            h1 = _norm_mod_fwd(xs, vec(norm1_g[i]), sc1, sh1, "norm1_fwd")
        else:
            xs, h1 = _norm_mod_fwd(xs, vec(norm1_g[i]), sc1, sh1, "norm1_fwd", y=y_prev, gate=gate_prev)
        st.update(x_in=xs, h1=h1)
        first_ffn_in = [("ffn", 0, "in")] if i == 0 else []
        if kind == 0:
            qkv = with_gather(both("mix", 0)[1:] if i == 0 else [], lambda side: _mm(
                h1, WF[("attn_wqkv", j)], "nt", "attn_qkv", out_dtype=BF16, bias=vec(F_small["attn_bqkv"][j]), side=side))
            o = with_gather(first_ffn_in, lambda side: _attn_fwd(qkv, _sink_rows(attn_sinks[j]), attn_bias, n_heads,
                                                                 "attn_fwd", side=side))
            y1 = _mm(o, WF[("attn_wo", j)], "nn", "attn_out", bias=vec(F_small["attn_bo"][j]))
            st.update(qkv=qkv, o=o)
        elif kind == 1:
            p = _mm(h1, WF[("conv_w_in", j)], "nt", "conv_in", out_dtype=BF16, bias=vec(conv_b_in[j]))
            zc, s = with_gather(first_ffn_in, lambda side: _convmix_fwd(
                p, F_small["conv_dw"][j], vec(conv_dw_b[j]), vec(conv_ln_g[j]), vec(conv_ln_b[j]), "convmix_fwd", side=side))
            y1 = _mm(s, WF[("conv_w_out", j)], "nn", "conv_out", bias=vec(conv_b_out[j]))
            st.update(p=p, zc=zc, s=s)
        else:
            p = _mm(h1, WF[("sgu_w_in", j)], "nt", "sgu_in", out_dtype=BF16, bias=vec(F_small["sgu_b_in"][j]))
            mm_ = _sgu_fwd(p, vec(F_small["sgu_ln_g"][j]), vec(F_small["sgu_ln_b"][j]), sgu_ws[j], sgu_bs[j].T, "sgu_fwd")
            y1 = _mm(mm_, WF[("sgu_w_out", j)], "nn", "sgu_out", bias=vec(F_small["sgu_b_out"][j]))
            st.update(p=p, m=mm_)
        xs, h2 = _norm_mod_fwd(xs, vec(norm2_g[i]), sc2, sh2, "norm2_fwd", y=y1, gate=g1)
        nxt = i + 1
        z = with_gather(both("mix", nxt) + ([("ffn", 0, "out")] if i == 0 else []),
                        lambda side: _mm(h2, WF[("ffn_w_in", i)], "nt", "ffn_in", out_dtype=BF16, side=side))
        a = with_gather(both("ffn", nxt)[:1],
                        lambda side: _ffn_mid_fwd(z, F_small["ffn_dw"][i], vec(ffn_dw_b[i]), "ffn_mid_fwd", side=side))
        y2 = with_gather(both("ffn", nxt)[1:], lambda side: _mm(a, WF[("ffn_w_out", i)], "nn", "ffn_out", side=side))
        st.update(y1=y1, x_mid=xs, h2=h2, z=z, a=a, y2=y2)
        saved.append(st)
        y_prev, gate_prev = y2, g2

    dx, d_final_g, loss_row, dy2, dg2 = _final_loss(xs, y_prev, gate_prev, tgt, vec(final_g), "final_loss")
    loss = lax.psum(loss_row[0, 0], ("x", "y", "c"))

    G = {n: [None] * W[n].shape[0] for n in ORDER if n != "final_g"}
    GW = {}
    recv = {}
    dmod = [None] * depth

    def with_scatter(keys, run):
        if not keys:
            return run(None)
        sends = [GW[piece(*k)[:2]].reshape(N_DEV, piece(*k)[2], D) for k in keys]
        *outs, landed = run(("scatter", sends))
        recv.update(zip(keys, landed))
        return outs[0] if len(outs) == 1 else tuple(outs)

    for i in reversed(range(depth)):
        st = saved[i]
        sh1, sc1, g1, sh2, sc2, g2 = [mod[i, k] for k in range(6)]
        kind, j = st["kind"], st["j"]
        last_out = [("mix", 0, "out")] if i == 0 else []
        last_in = [("mix", 0, "in")] if i == 0 else []
        GW[("ffn_w_out", i)] = _mm(st["a"], dy2, "tn", "ffn_out_dw", out_dtype=BF16)
        da = _mm(dy2, WF[("ffn_w_out", i)], "nt", "ffn_out_dx", out_dtype=BF16)
        dzg, dzu, ddw, ddwb = with_scatter(both("mix", i + 1) + [("ffn", i, "out")], lambda side: _ffn_mid_bwd(
            st["z"], da, F_small["ffn_dw"][i], vec(ffn_dw_b[i]), "ffn_mid_bwd", side=side))
        G["ffn_dw"][i], G["ffn_dw_b"][i] = ddw, ddwb[0]
        GW[("ffn_w_in", i)] = _mm(dzg, st["h2"], "tn", "ffn_in_dw", out_dtype=BF16, a2=dzu)
        dh2 = with_scatter([("ffn", i, "in")],
                           lambda side: _mm(dzg, WF[("ffn_w_in", i)], "nn", "ffn_in_dx", a2=dzu, side=side))
        dx, dn2, dsc2, dsh2, dy1, dg1, dbo = _norm_mod_bwd(dh2, st["x_mid"], vec(norm2_g[i]), sc2, sh2, dx, "norm_bwd",
                                                           y_up=st["y1"], gate_up=g1)
        G["norm2_g"][i] = dn2[0]
        if kind == 0:
            G["attn_bo"][j] = dbo[0]
            GW[("attn_wo", j)] = _mm(st["o"], dy1, "tn", "attn_out_dw", out_dtype=BF16)
            do = _mm(dy1, WF[("attn_wo", j)], "nt", "attn_out_dx", out_dtype=BF16)
            dq, dk, dv, dbq, dbk, dbv, dsk = with_scatter(last_out, lambda side: _attn_bwd(
                st["qkv"], do, _sink_rows(attn_sinks[j]), attn_bias, n_heads, "attn_bwd", side=side))
            dqkv = jnp.concatenate([dq, dk, dv], axis=1)
            G["attn_bqkv"][j] = jnp.concatenate([dbq, dbk, dbv], axis=1)[0]
            G["attn_sinks"][j] = dsk[:, 0]
            GW[("attn_wqkv", j)] = _mm(dqkv, st["h1"], "tn", "attn_qkv_dw", out_dtype=BF16)
            dh1 = with_scatter(last_in, lambda side: _mm(dqkv, WF[("attn_wqkv", j)], "nn", "attn_qkv_dx", side=side))
        elif kind == 1:
            G["conv_b_out"][j] = dbo[0]
            GW[("conv_w_out", j)] = _mm(st["s"], dy1, "tn", "conv_out_dw", out_dtype=BF16)
            ds = _mm(dy1, WF[("conv_w_out", j)], "nt", "conv_out_dx")
            dp, dbin, ddw, ddwb, dlng, dlnb = with_scatter(last_out, lambda side: _convmix_bwd(
                ds, st["zc"], st["p"], F_small["conv_dw"][j], vec(conv_ln_g[j]), vec(conv_ln_b[j]), "convmix_bwd",
                side=side))
            G["conv_b_in"][j], G["conv_dw"][j], G["conv_dw_b"][j] = dbin[0], ddw, ddwb[0]
            G["conv_ln_g"][j], G["conv_ln_b"][j] = dlng[0], dlnb[0]
            GW[("conv_w_in", j)] = _mm(dp, st["h1"], "tn", "conv_in_dw", out_dtype=BF16)
            dh1 = with_scatter(last_in, lambda side: _mm(dp, WF[("conv_w_in", j)], "nn", "conv_in_dx", side=side))
        else:
            G["sgu_b_out"][j] = dbo[0]
            GW[("sgu_w_out", j)] = _mm(st["m"], dy1, "tn", "sgu_out_dw", out_dtype=BF16)
            dm = _mm(dy1, WF[("sgu_w_out", j)], "nt", "sgu_out_dx")
            dp, dbin, dlng, dlnb, dws, dbst = with_scatter(last_out, lambda side: _sgu_bwd(
                st["p"], dm, vec(F_small["sgu_ln_g"][j]), vec(F_small["sgu_ln_b"][j]), sgu_ws[j], sgu_bs[j].T, "sgu_bwd",
                side=side))
            G["sgu_b_in"][j], G["sgu_ln_g"][j], G["sgu_ln_b"][j] = dbin[0], dlng[0], dlnb[0]
            G["sgu_ws"][j], G["sgu_bs"][j] = dws, dbst.T
            GW[("sgu_w_in", j)] = _mm(dp, st["h1"], "tn", "sgu_in_dw", out_dtype=BF16)
            dh1 = with_scatter(last_in, lambda side: _mm(dp, WF[("sgu_w_in", j)], "nn", "sgu_in_dx", side=side))
        dmod_i = lambda: jnp.concatenate([dsh1, dsc1, dg1, dsh2, dsc2, dg2], axis=1)[0]
        if i > 0:
            dx, dn1, dsc1, dsh1, dy2_below, dg2_below, _ = _norm_mod_bwd(
                dh1, st["x_in"], vec(norm1_g[i]), sc1, sh1, dx, "norm_bwd", y_up=saved[i - 1]["y2"], gate_up=mod[i - 1, 5])
            dmod[i] = dmod_i()
            dy2, dg2 = dy2_below, dg2_below
        else:
            dx, dn1, dsc1, dsh1 = _norm_mod_bwd(dh1, st["x_in"], vec(norm1_g[i]), sc1, sh1, dx, "norm_bwd")
            dmod[i] = dmod_i()
        G["norm1_g"][i] = dn1[0]
    grad_x = dx.reshape(x.shape)

    small_names = [n for n in ORDER if n not in
                   ("ada_w", "ada_b", "attn_wqkv", "attn_wo", "conv_w_in", "conv_w_out", "sgu_w_in", "sgu_w_out",
                    "ffn_w_in", "ffn_w_out", "final_g")]
    g_small = {n: jnp.stack(G[n]) for n in small_names}
    g_small["final_g"] = d_final_g[0]
    g_small["ada_b"] = jnp.stack(dmod)
    names16 = ["ffn_dw", "conv_dw", "sgu_ws", "ffn_dw_b", "ada_b"]
    names32 = [n for n in g_small if n not in names16]
    shapes16, shapes32 = [g_small[n].shape for n in names16], [g_small[n].shape for n in names32]
    gathered32, gathered16 = _exchange("gather", [_pack([g_small[n] for n in names32], 1024),
                                                  _pack([g_small[n] for n in names16], 1024).astype(BF16)],
                                       "gather_small_grads")
    gsum = dict(zip(names32, _unpack(_sum8(gathered32, "sum_small_grads").reshape(-1), shapes32)))
    gsum.update(zip(names16, _unpack(_sum8(gathered16, "sum_small_grads").reshape(-1), shapes16)))

    def local_shard(n, a):
        if n in small_sharded:
            w = W[n].shape[-1]
            return lax.dynamic_slice_in_dim(a, me * w, w, axis=a.ndim - 1)
        return a

    gsum = {n: local_shard(n, a) for n, a in gsum.items()}

    off16 = sum(math.prod(s) for s in shapes16[:-1])
    dmod_all = gathered16.reshape(N_DEV, -1)[:, off16:off16 + math.prod(shapes16[-1])].reshape((N_DEV,) + shapes16[-1])
    dmod_cols = lax.dynamic_slice_in_dim(dmod_all, me * n_ada, n_ada, axis=2)
    dmod_pad = jnp.pad(dmod_cols, ((0, 8), (0, 0), (0, 0)))
    g_ada_w = jnp.stack([_mm(c_pad, dmod_pad[:, i, :], "tn", "ada_dw") for i in range(depth)])

    big = {}
    for i in range(depth):
        for key in both("mix", i) + both("ffn", i):
            n, l, _ = piece(*key)
            gp = _sum8(recv[key], "sum_weight_grads")
            big.setdefault(n, {})[l] = gp.T if key[2] == "in" else gp
    grads = {n: jnp.stack([v[l] for l in range(len(v))]) for n, v in big.items()}
    grads["ada_w"] = g_ada_w
    grads.update(gsum)

    delta, new_m, new_v = {}, {}, {}
    big_names = ["ada_w", "attn_wqkv", "attn_wo", "conv_w_in", "conv_w_out", "sgu_w_in", "sgu_w_out", "ffn_w_in",
                 "ffn_w_out"]
    for n in big_names:
        delta[n], new_m[n], new_v[n] = _adamw_nd(W[n], grads[n], MOM[n], VAR[n], "adamw_" + n)
    rest = [n for n in ORDER if n not in big_names]
    rest_shapes = [W[n].shape for n in rest]
    pk = lambda d: _pack([d[n] for n in rest], 128)
    d_s, m_s, v_s = _adamw(pk(W), pk(grads), pk(MOM), pk(VAR), "adamw_small")
    for n, a, b_, c_ in zip(rest, _unpack(d_s.reshape(-1), rest_shapes), _unpack(m_s.reshape(-1), rest_shapes),
                            _unpack(v_s.reshape(-1), rest_shapes)):
        delta[n], new_m[n], new_v[n] = a, b_, c_

    return (loss, grad_x, *[grads[n] for n in ORDER], *[delta[n] for n in ORDER],
            *[new_m[n] for n in ORDER], *[new_v[n] for n in ORDER])
```

```python
import functools
import math

import jax
import jax.numpy as jnp
from jax import lax
from jax.experimental import pallas as pl
from jax.experimental.pallas import tpu as pltpu

F32 = jnp.float32
BF16 = jnp.bfloat16

N_DEV = 8
HEAD_DIM = 64
N_KV = 4
BLOCK = 128
NORM_EPS = 1e-6
NEG_INF = -1e30
ADAM_LR = 0.001
ADAM_B1 = 0.9
ADAM_B2 = 0.999
ADAM_EPS = 1e-08
ADAM_WD = 0.01
ADAM_STEP = 10
V7X_VMEM_LIMIT = 56 * 1024 * 1024
MESH = pl.DeviceIdType.MESH


def _cparams():
    return pltpu.CompilerParams(vmem_limit_bytes=V7X_VMEM_LIMIT)


def _tile(n, cap, align):
    best = None
    for t in range(align, min(n, cap) + 1, align):
        if n % t == 0:
            best = t
    return best if best is not None else n


def _dot(a, b, ca, cb):
    return lax.dot_general(a, b, (((ca,), (cb,)), ((), ())), preferred_element_type=F32)


def _sigmoid(x):
    return 1.0 / (1.0 + jnp.exp(-x))


def _rows(x, start, n):
    return lax.slice_in_dim(x, start, start + n, axis=0)


def _my_coords():
    return lax.axis_index("x"), lax.axis_index("y"), lax.axis_index("c")


def _peer(m, mx, my, mc):
    px = (mx + ((m >> 2) & 1)) % 2
    py = (my + ((m >> 1) & 1)) % 2
    pc = (mc + (m & 1)) % 2
    return px, py, pc


def _exchange_copies(kind, x_ref, o_ref, send_sems, recv_sems, local_sem):
    mx, my, mc = _my_coords()
    me = 4 * mx + 2 * my + mc
    local = pltpu.make_async_copy(x_ref if kind == "gather" else x_ref.at[me], o_ref.at[me], local_sem)
    remote = []
    for m in range(1, N_DEV):
        px, py, pc = _peer(m, mx, my, mc)
        src = x_ref if kind == "gather" else x_ref.at[4 * px + 2 * py + pc]
        remote.append(pltpu.make_async_remote_copy(
            src_ref=src, dst_ref=o_ref.at[me], send_sem=send_sems.at[m - 1], recv_sem=recv_sems.at[m - 1],
            device_id=(px, py, pc), device_id_type=MESH))
    return local, remote


def _exchange_start(kind, *refs):
    local, remote = _exchange_copies(kind, *refs)
    local.start()
    for cp in remote:
        cp.start()


def _exchange_wait(kind, *refs):
    local, remote = _exchange_copies(kind, *refs)
    for cp in remote:
        cp.wait_recv()
    for cp in remote:
        cp.wait_send()
    local.wait()


_EXCHANGE_SEMS = [pltpu.SemaphoreType.DMA((N_DEV - 1,)), pltpu.SemaphoreType.DMA((N_DEV - 1,)), pltpu.SemaphoreType.DMA]


def _exchange_shape(kind, x):
    return jax.ShapeDtypeStruct(((N_DEV,) + x.shape) if kind == "gather" else x.shape, x.dtype)


def _exchange(kind, xs, name):
    n = len(xs)

    def body(*refs):
        x_refs, o_refs, sems = refs[:n], refs[n:2 * n], refs[2 * n:]
        for e in range(n):
            _exchange_start(kind, x_refs[e], o_refs[e], *sems[3 * e:3 * e + 3])
        for e in range(n):
            _exchange_wait(kind, x_refs[e], o_refs[e], *sems[3 * e:3 * e + 3])

    any_spec = pl.BlockSpec(memory_space=pl.ANY)
    return tuple(pl.pallas_call(
        body, name=name, out_shape=tuple(_exchange_shape(kind, x) for x in xs),
        in_specs=[any_spec] * n, out_specs=(any_spec,) * n, scratch_shapes=list(_EXCHANGE_SEMS) * n,
    )(*xs))


def _all_gather(x, name):
    return _exchange("gather", [x], name)[0]


def _call(body, name, grid, in_specs, out_specs, out_shape, args, scratch=(), side=None):
    single = not isinstance(out_shape, (tuple, list))
    if single:
        out_shape, out_specs = (out_shape,), (out_specs,)
    if side is None:
        res = pl.pallas_call(body, name=name, grid=grid, in_specs=list(in_specs), out_specs=tuple(out_specs),
                             out_shape=tuple(out_shape), scratch_shapes=list(scratch), compiler_params=_cparams())(*args)
        return res[0] if single else tuple(res)
    kind, xs = side
    n_in, n_out, n_scr, n_x = len(in_specs), len(out_shape), len(scratch), len(xs)

    def wrapped(*refs):
        ins, x_refs = refs[:n_in], refs[n_in:n_in + n_x]
        o0 = n_in + n_x
        outs, o_refs = refs[o0:o0 + n_out], refs[o0 + n_out:o0 + n_out + n_x]
        s0 = o0 + n_out + n_x
        scr, sems = refs[s0:s0 + n_scr], refs[s0 + n_scr:]
        ids = [pl.program_id(a) for a in range(len(grid))]
        first = functools.reduce(jnp.logical_and, [i == 0 for i in ids])
        last = functools.reduce(jnp.logical_and, [i == g - 1 for i, g in zip(ids, grid)])

        @pl.when(first)
        def _():
            for e in range(n_x):
                _exchange_start(kind, x_refs[e], o_refs[e], *sems[3 * e:3 * e + 3])

        body(*ins, *outs, *scr)

        @pl.when(last)
        def _():
            for e in range(n_x):
                _exchange_wait(kind, x_refs[e], o_refs[e], *sems[3 * e:3 * e + 3])

    any_spec = pl.BlockSpec(memory_space=pl.ANY)
    res = pl.pallas_call(
        wrapped, name=name, grid=grid, in_specs=list(in_specs) + [any_spec] * n_x,
        out_specs=tuple(out_specs) + (any_spec,) * n_x,
        out_shape=tuple(out_shape) + tuple(_exchange_shape(kind, x) for x in xs),
        scratch_shapes=list(scratch) + list(_EXCHANGE_SEMS) * n_x, compiler_params=_cparams())(*args, *xs)
    return tuple(res[:n_out]) + (tuple(res[n_out:]),)


def _mm(a, b, mode, name, out_dtype=F32, bias=None, side=None, a2=None, tm_cap=1024, tn_cap=1408, tk_cap=1408):
    w2 = 0 if a2 is None else a2.shape[1]
    assert a2 is None or (mode in ("nn", "tn") and a2.shape[0] == a.shape[0])
    if mode == "nn":
        (M, K), (K2, N) = a.shape, b.shape
        K += w2
    elif mode == "nt":
        (M, K), (N, K2) = a.shape, b.shape
    else:
        (K, M), (K2, N) = a.shape, b.shape
        M += w2
    assert K == K2, (a.shape, b.shape, mode)
    if mode == "tn":
        tm_cap, tk_cap = 256, 8192
    if mode == "nn":
        tk_cap = 2816
    tm = _tile(math.gcd(M, w2) if mode == "tn" else M, tm_cap, 128 if mode == "tn" else 16)
    tn = _tile(N, tn_cap, 128)
    tk = _tile(math.gcd(K, w2) if mode == "nn" else K, tk_cap, 128)
    nk = K // tk
    n1 = a.shape[1] // (tm if mode == "tn" else tk)
    ca, cb = {"nn": (1, 0), "nt": (1, 1), "tn": (0, 0)}[mode]
    has_bias = bias is not None

    def body(*refs):
        refs = list(refs)
        a_ref = refs.pop(0)
        a2_ref = refs.pop(0) if a2 is not None else None
        b_ref = refs.pop(0)
        bias_ref = refs.pop(0) if has_bias else None
        o_ref, acc_ref = refs
        k = pl.program_id(2)

        def finish(r):
            if has_bias:
                r = r + bias_ref[...]
            o_ref[...] = r.astype(out_dtype)

        def step(lhs_ref):
            part = _dot(lhs_ref[...].astype(BF16), b_ref[...].astype(BF16), ca, cb)
            if nk == 1:
                finish(part)
                return

            @pl.when(k == 0)
            def _():
                acc_ref[...] = part

            @pl.when(k > 0)
            def _():
                acc_ref[...] += part

        if a2 is None:
            step(a_ref)
        else:
            col = pl.program_id(0) if mode == "tn" else k
            pl.when(col < n1)(lambda: step(a_ref))
            pl.when(col >= n1)(lambda: step(a2_ref))

        if nk > 1:
            pl.when(k == nk - 1)(lambda: finish(acc_ref[...]))

    if mode == "tn":
        a_specs = [pl.BlockSpec((tk, tm), lambda i, j, k: (k, jnp.minimum(i, n1 - 1)))]
        if a2 is not None:
            a_specs.append(pl.BlockSpec((tk, tm), lambda i, j, k: (k, jnp.maximum(i - n1, 0))))
    else:
        a_specs = [pl.BlockSpec((tm, tk), lambda i, j, k: (i, jnp.minimum(k, n1 - 1)))]
        if a2 is not None:
            a_specs.append(pl.BlockSpec((tm, tk), lambda i, j, k: (i, jnp.maximum(k - n1, 0))))
    if mode == "nt":
        b_spec = pl.BlockSpec((tn, tk), lambda i, j, k: (j, k))
    else:
        b_spec = pl.BlockSpec((tk, tn), lambda i, j, k: (k, j))
    in_specs = a_specs + [b_spec]
    args = [a] + ([a2] if a2 is not None else []) + [b]
    if has_bias:
        in_specs.append(pl.BlockSpec((1, tn), lambda i, j, k: (0, j)))
        args.append(bias)
    return _call(body, name, (M // tm, N // tn, nk), in_specs, pl.BlockSpec((tm, tn), lambda i, j, k: (i, j)),
                 jax.ShapeDtypeStruct((M, N), out_dtype), args,
                 scratch=[pltpu.VMEM((tm, tn) if nk > 1 else (8, 128), F32)], side=side)


ROW_TILE = 512


def _vec_spec(n):
    return pl.BlockSpec((1, n), lambda t: (0, 0))


def _norm_mod_fwd(x, g, sc, sh, name, y=None, gate=None):
    T, D = x.shape
    tr = _tile(T, ROW_TILE, 16)
    has_y = y is not None

    def body(*refs):
        if has_y:
            x_ref, y_ref, gate_ref, g_ref, sc_ref, sh_ref, xn_ref, h_ref = refs
            xv = x_ref[...] + gate_ref[...] * y_ref[...]
            xn_ref[...] = xv
        else:
            x_ref, g_ref, sc_ref, sh_ref, h_ref = refs
            xv = x_ref[...]
        r = lax.rsqrt(jnp.mean(xv * xv, axis=-1, keepdims=True) + NORM_EPS)
        h = (xv * r * g_ref[...]) * (1.0 + sc_ref[...]) + sh_ref[...]
        h_ref[...] = h.astype(BF16)

    row = pl.BlockSpec((tr, D), lambda t: (t, 0))
    vec = _vec_spec(D)
    if has_y:
        in_specs, args = [row, row, vec, vec, vec, vec], [x, y, gate, g, sc, sh]
        out_shape = (jax.ShapeDtypeStruct((T, D), F32), jax.ShapeDtypeStruct((T, D), BF16))
        out_specs = (row, row)
    else:
        in_specs, args = [row, vec, vec, vec], [x, g, sc, sh]
        out_shape = jax.ShapeDtypeStruct((T, D), BF16)
        out_specs = row
    return _call(body, name, (T // tr,), in_specs, out_specs, out_shape, args)


def _norm_mod_bwd(dh, x, g, sc, sh, dxn, name, y_up=None, gate_up=None):
    T, D = x.shape
    tr = _tile(T, ROW_TILE, 16)
    fused = y_up is not None

    def body(*refs):
        if fused:
            (dh_ref, x_ref, g_ref, sc_ref, sh_ref, dxn_ref, y_ref, gate_ref,
             dx_ref, dg_ref, dsc_ref, dsh_ref, dy_ref, dgate_ref, dbias_ref) = refs
            sums = (dg_ref, dsc_ref, dsh_ref, dgate_ref, dbias_ref)
        else:
            dh_ref, x_ref, g_ref, sc_ref, sh_ref, dxn_ref, dx_ref, dg_ref, dsc_ref, dsh_ref = refs
            sums = (dg_ref, dsc_ref, dsh_ref)

        @pl.when(pl.program_id(0) == 0)
        def _():
            for r_ in sums:
                r_[...] = jnp.zeros_like(r_)
        dh = dh_ref[...].astype(F32)
        xv = x_ref[...]
        gv = g_ref[...]
        r = lax.rsqrt(jnp.mean(xv * xv, axis=-1, keepdims=True) + NORM_EPS)
        yv = xv * r
        dsh_ref[...] += jnp.sum(dh, axis=0, keepdims=True)
        dsc_ref[...] += jnp.sum(dh * (yv * gv), axis=0, keepdims=True)
        dn = dh * (1.0 + sc_ref[...])
        dg_ref[...] += jnp.sum(dn * yv, axis=0, keepdims=True)
        dy = dn * gv
        dx = dxn_ref[...] + r * (dy - yv * jnp.mean(dy * yv, axis=-1, keepdims=True))
        dx_ref[...] = dx
        if fused:
            dyu = dx * gate_ref[...]
            dy_ref[...] = dyu.astype(BF16)
            dgate_ref[...] += jnp.sum(dx * y_ref[...], axis=0, keepdims=True)
            dbias_ref[...] += jnp.sum(dyu, axis=0, keepdims=True)

    row = pl.BlockSpec((tr, D), lambda t: (t, 0))
    vec = _vec_spec(D)
    vshape = jax.ShapeDtypeStruct((1, D), F32)
    in_specs, args = [row, row, vec, vec, vec, row], [dh, x, g, sc, sh, dxn]
    out_specs, out_shape = [row, vec, vec, vec], [jax.ShapeDtypeStruct((T, D), F32), vshape, vshape, vshape]
    if fused:
        in_specs, args = in_specs + [row, vec], args + [y_up, gate_up]
        out_specs, out_shape = out_specs + [row, vec, vec], out_shape + [jax.ShapeDtypeStruct((T, D), BF16), vshape, vshape]
    return _call(body, name, (T // tr,), in_specs, tuple(out_specs), tuple(out_shape), args)


def _final_loss(x, y, gate, tgt, g, name):
    T, D = x.shape
    tr = _tile(T, ROW_TILE, 16)

    def body(x_ref, y_ref, gate_ref, tgt_ref, g_ref, dx_ref, dg_ref, loss_ref, dyu_ref, dgate_ref):
        @pl.when(pl.program_id(0) == 0)
        def _():
            for r_ in (dg_ref, loss_ref, dgate_ref):
                r_[...] = jnp.zeros_like(r_)
        yu = y_ref[...]
        gate = gate_ref[...]
        xv = x_ref[...] + gate * yu
        gv = g_ref[...]
        r = lax.rsqrt(jnp.mean(xv * xv, axis=-1, keepdims=True) + NORM_EPS)
        yv = xv * r
        e = yv * gv - tgt_ref[...]
        per_row = jnp.mean(e * e, axis=-1, keepdims=True)
        loss_ref[...] += 0.5 * jnp.sum(per_row, axis=0, keepdims=True)
        dout = e * (1.0 / D)
        dg_ref[...] += jnp.sum(dout * yv, axis=0, keepdims=True)
        dy = dout * gv
        dx = r * (dy - yv * jnp.mean(dy * yv, axis=-1, keepdims=True))
        dx_ref[...] = dx
        dyu_ref[...] = (dx * gate).astype(BF16)
        dgate_ref[...] += jnp.sum(dx * yu, axis=0, keepdims=True)

    row = pl.BlockSpec((tr, D), lambda t: (t, 0))
    vec = _vec_spec(D)
    return _call(body, name, (T // tr,), [row, row, vec, row, vec], (row, vec, _vec_spec(128), row, vec),
                 (jax.ShapeDtypeStruct((T, D), F32), jax.ShapeDtypeStruct((1, D), F32),
                  jax.ShapeDtypeStruct((1, 128), F32), jax.ShapeDtypeStruct((T, D), BF16),
                  jax.ShapeDtypeStruct((1, D), F32)), (x, y, gate, tgt, g))


FFN_HALO = 16
FFN_PAD = 8
FFN_CHUNK = 64
FFN_ROWS_FWD = 8192
FFN_ROWS_BWD = 4096


def _lane_groups(cw):
    lw = 128 if cw % 128 == 0 else cw
    return lw, [pl.ds(g * lw, lw) for g in range(cw // lw)]


def _ffn_mid_fwd(z, dw, b, name, side=None):
    T, F2 = z.shape
    F = F2 // 2
    K = dw.shape[0]
    H, P = FFN_HALO, FFN_PAD
    tr = _tile(T, FFN_ROWS_FWD, H)
    cw = _tile(F, 512, 128)
    nc = F // cw
    rc = _tile(tr, FFN_CHUNK, 16)
    lw, groups = _lane_groups(cw)

    def body(gc_ref, gh_ref, uc_ref, uh_ref, wg_ref, wu_ref, bg_ref, bu_ref, a_ref, xg_s, xu_s):
        first = pl.program_id(1) == 0

        def fill(s, h_ref, c_ref):
            for g, ls in enumerate(groups):
                s[g, 0:P, :] = jnp.where(first, 0.0, h_ref[:, ls].astype(F32)[H - P:H])
                s[g, P:P + tr, :] = c_ref[:, ls].astype(F32)

        fill(xg_s, gh_ref, gc_ref)
        fill(xu_s, uh_ref, uc_ref)

        def chunk(ci, carry):
            r0 = pl.multiple_of(ci * rc, rc)
            for g, ls in enumerate(groups):
                def conv(s, w_ref, b_ref):
                    acc = b_ref[:, ls] + w_ref[K - 1:K, ls] * s[g, pl.ds(r0 + P, rc), :]
                    for k in range(K - 1):
                        acc = acc + w_ref[k:k + 1, ls] * s[g, pl.ds(r0 + P - (K - 1 - k), rc), :]
                    return acc

                gv = conv(xg_s, wg_ref, bg_ref)
                uv = conv(xu_s, wu_ref, bu_ref)
                a_ref[pl.ds(r0, rc), ls] = (gv * _sigmoid(gv) * uv).astype(BF16)
            return carry

        lax.fori_loop(0, tr // rc, chunk, 0, unroll=2)

    rb = tr // H
    in_specs = [
        pl.BlockSpec((tr, cw), lambda j, t: (t, j)),
        pl.BlockSpec((H, cw), lambda j, t: (jnp.maximum(t * rb - 1, 0), j)),
        pl.BlockSpec((tr, cw), lambda j, t: (t, j + nc)),
        pl.BlockSpec((H, cw), lambda j, t: (jnp.maximum(t * rb - 1, 0), j + nc)),
        pl.BlockSpec((K, cw), lambda j, t: (0, j)),
        pl.BlockSpec((K, cw), lambda j, t: (0, j + nc)),
        pl.BlockSpec((1, cw), lambda j, t: (0, j)),
        pl.BlockSpec((1, cw), lambda j, t: (0, j + nc)),
    ]
    return _call(body, name, (nc, T // tr), in_specs, pl.BlockSpec((tr, cw), lambda j, t: (t, j)),
                 jax.ShapeDtypeStruct((T, F), BF16), (z, z, z, z, dw, dw, b, b),
                 scratch=[pltpu.VMEM((len(groups), P + tr, lw), F32)] * 2, side=side)


def _ffn_mid_bwd(z, da, dw, b, name, side=None):
    T, F2 = z.shape
    F = F2 // 2
    K = dw.shape[0]
    H, P = FFN_HALO, FFN_PAD
    tr = _tile(T, FFN_ROWS_BWD, H)
    cw = _tile(F, 512, 128)
    nc = F // cw
    nt = T // tr
    rb = tr // H
    rc = _tile(tr, FFN_CHUNK, 16)
    ext = rc + P
    lw, groups = _lane_groups(cw)

    def body(gp_ref, gc_ref, gn_ref, up_ref, uc_ref, un_ref, dac_ref, dan_ref, wg_ref, wu_ref, bg_ref, bu_ref,
             dzg_ref, dzu_ref, dwg_ref, dwu_ref, dbg_ref, dbu_ref, xg_s, xu_s, da_s, dg_s, du_s, acc_s):
        t = pl.program_id(1)
        first = t == 0
        last = t == nt - 1

        @pl.when(first)
        def _():
            for r in (dwg_ref, dwu_ref, dbg_ref, dbu_ref):
                r[...] = jnp.zeros_like(r)

        def fill(s, p_ref, c_ref, n_ref):
            for g, ls in enumerate(groups):
                s[g, 0:P, :] = jnp.where(first, 0.0, p_ref[:, ls].astype(F32)[H - P:H])
                s[g, P:P + tr, :] = c_ref[:, ls].astype(F32)
                s[g, P + tr:2 * P + tr, :] = jnp.where(last, 0.0, n_ref[:, ls].astype(F32)[0:P])

        fill(xg_s, gp_ref, gc_ref, gn_ref)
        fill(xu_s, up_ref, uc_ref, un_ref)
        for g, ls in enumerate(groups):
            da_s[g, 0:tr, :] = dac_ref[:, ls].astype(F32)
            da_s[g, tr:tr + P, :] = jnp.where(last, 0.0, dan_ref[:, ls].astype(F32)[0:P])
        acc_s[...] = jnp.zeros_like(acc_s)

        def fold(v):
            return jnp.sum(v.reshape(rc // 8, 8, lw), axis=0)

        def chunk(ci, carry):
            r0 = pl.multiple_of(ci * rc, rc)
            for g, ls in enumerate(groups):
                def conv_ext(s, w_ref, b_ref):
                    acc = b_ref[:, ls] + w_ref[K - 1:K, ls] * s[g, pl.ds(r0 + P, ext), :]
                    for k in range(K - 1):
                        acc = acc + w_ref[k:k + 1, ls] * s[g, pl.ds(r0 + P - (K - 1 - k), ext), :]
                    return acc

                gv = conv_ext(xg_s, wg_ref, bg_ref)
                uv = conv_ext(xu_s, wu_ref, bu_ref)
                dav = da_s[g, pl.ds(r0, ext), :]
                sg = _sigmoid(gv)
                dg = dav * uv * (sg * (1.0 + gv * (1.0 - sg)))
                du = dav * (gv * sg)
                dg_s[g] = dg
                du_s[g] = du
                for idx, (d_s, dval, x_s, w_ref, dz_ref) in enumerate(
                        ((dg_s, dg, xg_s, wg_ref, dzg_ref), (du_s, du, xu_s, wu_ref, dzu_ref))):
                    cur = dval[:rc]
                    acc_s[idx, K, g] += fold(cur)
                    for k in range(K):
                        acc_s[idx, k, g] += fold(cur * x_s[g, pl.ds(r0 + P - (K - 1 - k), rc), :])
                    dz = w_ref[K - 1:K, ls] * cur
                    for k in range(K - 1):
                        dz = dz + w_ref[k:k + 1, ls] * d_s[g, pl.ds(K - 1 - k, rc), :]
                    dz_ref[pl.ds(r0, rc), ls] = dz.astype(BF16)
            return carry

        lax.fori_loop(0, tr // rc, chunk, 0, unroll=2)
        for idx, (dw_ref, db_ref) in enumerate(((dwg_ref, dbg_ref), (dwu_ref, dbu_ref))):
            for g, ls in enumerate(groups):
                db_ref[:, ls] += jnp.sum(acc_s[idx, K, g], axis=0, keepdims=True)
                dw_ref[:, ls] += jnp.concatenate(
                    [jnp.sum(acc_s[idx, k, g], axis=0, keepdims=True) for k in range(K)], axis=0)

    def prev(t):
        return jnp.maximum(t * rb - 1, 0)

    def nxt(t):
        return jnp.minimum((t + 1) * rb, T // H - 1)

    def zspecs(off):
        return [pl.BlockSpec((H, cw), lambda j, t: (prev(t), j + off)),
                pl.BlockSpec((tr, cw), lambda j, t: (t, j + off)),
                pl.BlockSpec((H, cw), lambda j, t: (nxt(t), j + off))]

    in_specs = zspecs(0) + zspecs(nc) + [
        pl.BlockSpec((tr, cw), lambda j, t: (t, j)),
        pl.BlockSpec((H, cw), lambda j, t: (nxt(t), j)),
        pl.BlockSpec((K, cw), lambda j, t: (0, j)),
        pl.BlockSpec((K, cw), lambda j, t: (0, j + nc)),
        pl.BlockSpec((1, cw), lambda j, t: (0, j)),
        pl.BlockSpec((1, cw), lambda j, t: (0, j + nc)),
    ]
    out_shape = (jax.ShapeDtypeStruct((T, F), BF16), jax.ShapeDtypeStruct((T, F), BF16),
                 jax.ShapeDtypeStruct((K, F), F32), jax.ShapeDtypeStruct((K, F), F32),
                 jax.ShapeDtypeStruct((1, F), F32), jax.ShapeDtypeStruct((1, F), F32))
    out_specs = (pl.BlockSpec((tr, cw), lambda j, t: (t, j)), pl.BlockSpec((tr, cw), lambda j, t: (t, j)),
                 pl.BlockSpec((K, cw), lambda j, t: (0, j)), pl.BlockSpec((K, cw), lambda j, t: (0, j)),
                 pl.BlockSpec((1, cw), lambda j, t: (0, j)), pl.BlockSpec((1, cw), lambda j, t: (0, j)))
    ng = len(groups)
    scratch = [pltpu.VMEM((ng, 2 * P + tr, lw), F32), pltpu.VMEM((ng, 2 * P + tr, lw), F32),
               pltpu.VMEM((ng, P + tr, lw), F32), pltpu.VMEM((ng, ext, lw), F32), pltpu.VMEM((ng, ext, lw), F32),
               pltpu.VMEM((2, K + 1, ng, 8, lw), F32)]
    res = _call(body, name, (nc, nt), in_specs, out_specs, out_shape, (z, z, z, z, z, z, da, da, dw, dw, b, b),
                scratch=scratch, side=side)
    dzg, dzu, dwg, dwu, dbg, dbu = res[:6]
    return (dzg, dzu, jnp.concatenate([dwg, dwu], axis=1), jnp.concatenate([dbg, dbu], axis=1)) + tuple(res[6:])


CONV_HALO = 32
CONV_ROWS = 512
CONV_CHUNK = 64


def _glu_window(s, p_ref, halo_ref, first, D, groups, lw, H, tr):
    for g, ls in enumerate(groups):
        gs = pl.ds(D + g * lw, lw)
        s[g, 0:H, :] = jnp.where(first, 0.0, halo_ref[:, ls].astype(F32) * _sigmoid(halo_ref[:, gs].astype(F32)))
        s[g, H:H + tr, :] = p_ref[:, ls].astype(F32) * _sigmoid(p_ref[:, gs].astype(F32))


def _convmix_fwd(p, dw, dwb, lng, lnb, name, side=None):
    T, D2 = p.shape
    D = D2 // 2
    K = dw.shape[0]
    H = CONV_HALO
    tr = _tile(T, CONV_ROWS, H)
    rb = tr // H
    rc = _tile(tr, CONV_CHUNK, 16)
    lw, groups = _lane_groups(D)

    def body(pc_ref, ph_ref, w_ref, wb_ref, lng_ref, lnb_ref, zc_ref, s_ref, zg_s):
        first = pl.program_id(0) == 0
        _glu_window(zg_s, pc_ref, ph_ref, first, D, groups, lw, H, tr)

        def chunk(ci, carry):
            r0 = pl.multiple_of(ci * rc, rc)
            for g, ls in enumerate(groups):
                acc = wb_ref[:, ls] + w_ref[K - 1:K, ls] * zg_s[g, pl.ds(pl.multiple_of(r0 + H, 8), rc), :]
                for k in range(K - 1):
                    acc = acc + w_ref[k:k + 1, ls] * zg_s[g, pl.ds(r0 + H - (K - 1 - k), rc), :]
                zc_ref[pl.ds(r0, rc), ls] = acc
            return carry

        lax.fori_loop(0, tr // rc, chunk, 0)
        acc = zc_ref[...]
        mu = jnp.mean(acc, axis=-1, keepdims=True)
        xc = acc - mu
        rstd = lax.rsqrt(jnp.mean(xc * xc, axis=-1, keepdims=True) + NORM_EPS)
        ln = xc * rstd * lng_ref[...] + lnb_ref[...]
        s_ref[...] = (ln * _sigmoid(ln)).astype(BF16)

    in_specs = [pl.BlockSpec((tr, D2), lambda t: (t, 0)),
                pl.BlockSpec((H, D2), lambda t: (jnp.maximum(t * rb - 1, 0), 0)),
                pl.BlockSpec((K, D), lambda t: (0, 0)), _vec_spec(D), _vec_spec(D), _vec_spec(D)]
    row = pl.BlockSpec((tr, D), lambda t: (t, 0))
    return _call(body, name, (T // tr,), in_specs, (row, row),
                 (jax.ShapeDtypeStruct((T, D), F32), jax.ShapeDtypeStruct((T, D), BF16)), (p, p, dw, dwb, lng, lnb),
                 scratch=[pltpu.VMEM((len(groups), H + tr, lw), F32)], side=side)


def _convmix_bwd(ds, zc, p, dw, lng, lnb, name, side=None):
    T, D = zc.shape
    D2 = 2 * D
    K = dw.shape[0]
    H = CONV_HALO
    tr = _tile(T, CONV_ROWS, H)
    rb = tr // H
    nt = T // tr
    ext = tr + H
    rc = _tile(tr, CONV_CHUNK, 16)
    lw, groups = _lane_groups(D)

    def body(dsc_ref, dsn_ref, zcc_ref, zcn_ref, pp_ref, pc_ref, w_ref, lng_ref, lnb_ref,
             dp_ref, dbin_ref, ddw_ref, ddwb_ref, dlng_ref, dlnb_ref, zg_s, dzc_s, acc_s):
        t = pl.program_id(0)
        first = t == 0
        last = t == nt - 1

        @pl.when(first)
        def _():
            for r in (dbin_ref, ddw_ref, ddwb_ref, dlng_ref, dlnb_ref):
                r[...] = jnp.zeros_like(r)

        dsv = jnp.concatenate([dsc_ref[...].astype(F32), jnp.where(last, 0.0, dsn_ref[...].astype(F32))], axis=0)
        zcv = jnp.concatenate([zcc_ref[...], zcn_ref[...]], axis=0)
        lg = lng_ref[...]
        mu = jnp.mean(zcv, axis=-1, keepdims=True)
        xc = zcv - mu
        rstd = lax.rsqrt(jnp.mean(xc * xc, axis=-1, keepdims=True) + NORM_EPS)
        xh = xc * rstd
        ln = xh * lg + lnb_ref[...]
        sg = _sigmoid(ln)
        dln = dsv * (sg * (1.0 + ln * (1.0 - sg)))
        dlnc = _rows(dln, 0, tr)
        dlnb_ref[...] += jnp.sum(dlnc, axis=0, keepdims=True)
        dlng_ref[...] += jnp.sum(dlnc * _rows(xh, 0, tr), axis=0, keepdims=True)
        dxh = dln * lg
        dzc = rstd * (dxh - jnp.mean(dxh, axis=-1, keepdims=True) - xh * jnp.mean(dxh * xh, axis=-1, keepdims=True))
        dzcc = _rows(dzc, 0, tr)
        ddwb_ref[...] += jnp.sum(dzcc, axis=0, keepdims=True)

        for g in range(len(groups)):
            dzc_s[g] = dzc[:, g * lw:(g + 1) * lw]
        _glu_window(zg_s, pc_ref, pp_ref, first, D, groups, lw, H, tr)
        acc_s[...] = jnp.zeros_like(acc_s)

        def fold(v):
            return jnp.sum(v.reshape(rc // 8, 8, lw), axis=0)

        def chunk(ci, carry):
            r0 = pl.multiple_of(ci * rc, rc)
            for g, ls in enumerate(groups):
                gs = pl.ds(D + g * lw, lw)
                cur = dzc_s[g, pl.ds(r0, rc), :]
                dzg = w_ref[K - 1:K, ls] * cur
                for k in range(K):
                    acc_s[k, g] += fold(cur * zg_s[g, pl.ds(r0 + H - (K - 1 - k), rc), :])
                    if k < K - 1:
                        dzg = dzg + w_ref[k:k + 1, ls] * dzc_s[g, pl.ds(r0 + (K - 1 - k), rc), :]
                ac = pc_ref[pl.ds(r0, rc), ls].astype(F32)
                sgg = _sigmoid(pc_ref[pl.ds(r0, rc), gs].astype(F32))
                dpa = dzg * sgg
                dpg = dzg * ac * sgg * (1.0 - sgg)
                dp_ref[pl.ds(r0, rc), ls] = dpa.astype(BF16)
                dp_ref[pl.ds(r0, rc), gs] = dpg.astype(BF16)
                acc_s[K, g] += fold(dpa)
                acc_s[K + 1, g] += fold(dpg)
            return carry

        lax.fori_loop(0, tr // rc, chunk, 0)
        for g, ls in enumerate(groups):
            ddw_ref[:, ls] += jnp.concatenate([jnp.sum(acc_s[k, g], axis=0, keepdims=True) for k in range(K)], axis=0)
            dbin_ref[:, ls] += jnp.sum(acc_s[K, g], axis=0, keepdims=True)
            dbin_ref[:, pl.ds(D + g * lw, lw)] += jnp.sum(acc_s[K + 1, g], axis=0, keepdims=True)

    def prev(t):
        return jnp.maximum(t * rb - 1, 0)

    def nxt(t):
        return jnp.minimum((t + 1) * rb, T // H - 1)

    in_specs = [pl.BlockSpec((tr, D), lambda t: (t, 0)), pl.BlockSpec((H, D), lambda t: (nxt(t), 0)),
                pl.BlockSpec((tr, D), lambda t: (t, 0)), pl.BlockSpec((H, D), lambda t: (nxt(t), 0)),
                pl.BlockSpec((H, D2), lambda t: (prev(t), 0)), pl.BlockSpec((tr, D2), lambda t: (t, 0)),
                pl.BlockSpec((K, D), lambda t: (0, 0)), _vec_spec(D), _vec_spec(D)]
    out_shape = (jax.ShapeDtypeStruct((T, D2), BF16), jax.ShapeDtypeStruct((1, D2), F32),
                 jax.ShapeDtypeStruct((K, D), F32), jax.ShapeDtypeStruct((1, D), F32),
                 jax.ShapeDtypeStruct((1, D), F32), jax.ShapeDtypeStruct((1, D), F32))
    out_specs = (pl.BlockSpec((tr, D2), lambda t: (t, 0)), _vec_spec(D2), pl.BlockSpec((K, D), lambda t: (0, 0)),
                 _vec_spec(D), _vec_spec(D), _vec_spec(D))
    ng = len(groups)
    scratch = [pltpu.VMEM((ng, H + tr, lw), F32), pltpu.VMEM((ng, ext, lw), F32), pltpu.VMEM((K + 2, ng, 8, lw), F32)]
    return _call(body, name, (nt,), in_specs, out_specs, out_shape, (ds, ds, zc, zc, p, p, dw, lng, lnb),
                 scratch=scratch, side=side)


_INV_SQRT2 = 1.0 / math.sqrt(2.0)
_INV_SQRT2PI = 1.0 / math.sqrt(2.0 * math.pi)


def _gelu(x):
    return 0.5 * x * (1.0 + lax.erf(x * _INV_SQRT2))


def _gelu_grad(x):
    return 0.5 * (1.0 + lax.erf(x * _INV_SQRT2)) + x * jnp.exp(-0.5 * x * x) * _INV_SQRT2PI


def _tril_mask(n, transposed=False):
    r = lax.broadcasted_iota(jnp.int32, (n, n), 0)
    c = lax.broadcasted_iota(jnp.int32, (n, n), 1)
    return (r <= c) if transposed else (r >= c)


def _sgu_fwd(p, lng, lnb, ws, bs_t, name):
    T, S2 = p.shape
    S = S2 // 2
    G, C, _ = ws.shape
    gd = S // G
    cpt = 2 if (T // C) % 2 == 0 else 1
    tr = C * cpt

    def body(p_ref, lng_ref, lnb_ref, ws_ref, bst_ref, m_ref):
        mask = _tril_mask(C)
        lg, lb = lng_ref[...], lnb_ref[...]
        bst = bst_ref[...]
        for ci in range(cpt):
            pc = p_ref[ci * C:(ci + 1) * C, :].astype(F32)
            z = _gelu(pc)
            u, v = z[:, :S], z[:, S:]
            mu = jnp.mean(v, axis=-1, keepdims=True)
            xc = v - mu
            rstd = lax.rsqrt(jnp.mean(xc * xc, axis=-1, keepdims=True) + NORM_EPS)
            vn = (xc * rstd * lg + lb).astype(BF16)
            outs = []
            for g in range(G):
                wm = jnp.where(mask, ws_ref[g], 0.0).astype(BF16)
                vs = _dot(wm, vn[:, g * gd:(g + 1) * gd], 1, 0) + bst[:, g:g + 1]
                outs.append(u[:, g * gd:(g + 1) * gd] * vs)
            m_ref[ci * C:(ci + 1) * C, :] = jnp.concatenate(outs, axis=1).astype(BF16)

    in_specs = [pl.BlockSpec((tr, S2), lambda t: (t, 0)), _vec_spec(S), _vec_spec(S),
                pl.BlockSpec((G, C, C), lambda t: (0, 0, 0)), pl.BlockSpec((C, G), lambda t: (0, 0))]
    return _call(body, name, (T // tr,), in_specs, pl.BlockSpec((tr, S), lambda t: (t, 0)),
                 jax.ShapeDtypeStruct((T, S), BF16), (p, lng, lnb, ws, bs_t))


def _sgu_bwd(p, dm, lng, lnb, ws, bs_t, name, side=None):
    T, S2 = p.shape
    S = S2 // 2
    G, C, _ = ws.shape
    gd = S // G
    nt = T // C

    def body(p_ref, dm_ref, lng_ref, lnb_ref, ws_ref, bst_ref, dp_ref, dbin_ref, dlng_ref, dlnb_ref, dws_ref, dbst_ref):
        @pl.when(pl.program_id(0) == 0)
        def _():
            for r in (dbin_ref, dlng_ref, dlnb_ref, dws_ref, dbst_ref):
                r[...] = jnp.zeros_like(r)

        mask = _tril_mask(C)
        lg, lb = lng_ref[...], lnb_ref[...]
        bst = bst_ref[...]
        pc = p_ref[...].astype(F32)
        dmv = dm_ref[...].astype(F32)
        z = _gelu(pc)
        u, v = z[:, :S], z[:, S:]
        mu = jnp.mean(v, axis=-1, keepdims=True)
        xc = v - mu
        rstd = lax.rsqrt(jnp.mean(xc * xc, axis=-1, keepdims=True) + NORM_EPS)
        vh = xc * rstd
        vn = (vh * lg + lb).astype(BF16)
        dus, dvns, dbcols = [], [], []
        for g in range(G):
            sl = slice(g * gd, (g + 1) * gd)
            wm = jnp.where(mask, ws_ref[g], 0.0).astype(BF16)
            vs = _dot(wm, vn[:, sl], 1, 0) + bst[:, g:g + 1]
            dmg = dmv[:, sl]
            dus.append(dmg * vs)
            dvs = dmg * u[:, sl]
            dbcols.append(jnp.sum(dvs, axis=-1, keepdims=True))
            dvsb = dvs.astype(BF16)
            dws_ref[g] += jnp.where(mask, _dot(dvsb, vn[:, sl], 1, 1), 0.0)
            dvns.append(_dot(wm, dvsb, 0, 0))
        dbst_ref[...] += jnp.concatenate(dbcols, axis=1)
        dvn = jnp.concatenate(dvns, axis=1)
        dlnb_ref[...] += jnp.sum(dvn, axis=0, keepdims=True)
        dlng_ref[...] += jnp.sum(dvn * vh, axis=0, keepdims=True)
        dvh = dvn * lg
        dv = rstd * (dvh - jnp.mean(dvh, axis=-1, keepdims=True) - vh * jnp.mean(dvh * vh, axis=-1, keepdims=True))
        dz = jnp.concatenate([jnp.concatenate(dus, axis=1), dv], axis=1)
        dpv = dz * _gelu_grad(pc)
        dp_ref[...] = dpv.astype(BF16)
        dbin_ref[...] += jnp.sum(dpv, axis=0, keepdims=True)

    in_specs = [pl.BlockSpec((C, S2), lambda t: (t, 0)), pl.BlockSpec((C, S), lambda t: (t, 0)),
                _vec_spec(S), _vec_spec(S), pl.BlockSpec((G, C, C), lambda t: (0, 0, 0)),
                pl.BlockSpec((C, G), lambda t: (0, 0))]
    out_shape = (jax.ShapeDtypeStruct((T, S2), BF16), jax.ShapeDtypeStruct((1, S2), F32),
                 jax.ShapeDtypeStruct((1, S), F32), jax.ShapeDtypeStruct((1, S), F32),
                 jax.ShapeDtypeStruct((G, C, C), F32), jax.ShapeDtypeStruct((C, G), F32))
    out_specs = (pl.BlockSpec((C, S2), lambda t: (t, 0)), _vec_spec(S2), _vec_spec(S), _vec_spec(S),
                 pl.BlockSpec((G, C, C), lambda t: (0, 0, 0)), pl.BlockSpec((C, G), lambda t: (0, 0)))
    return _call(body, name, (nt,), in_specs, out_specs, out_shape, (p, dm, lng, lnb, ws, bs_t), side=side)


def _alibi_slope(h, n_heads):
    return 2.0 ** (-8.0 * (h + 1) / n_heads)


def _attn_bias(n_heads):
    B = BLOCK
    qi = jnp.arange(B)[:, None]
    kj = jnp.arange(2 * B)[None, :]
    dist = qi + B - kj
    band = (dist >= 0) & (dist < B)
    slopes = jnp.array([_alibi_slope(h, n_heads) for h in range(n_heads)], F32)
    out = []
    for valid in (band & (kj >= B), band):
        b = jnp.where(valid[None], -slopes[:, None, None] * dist.astype(F32)[None], NEG_INF)
        b = jnp.where(kj[None] == 0, 0.0, b)
        out.append(b.reshape(n_heads // 2, 2, B, 2 * B).transpose(0, 2, 1, 3).reshape(n_heads // 2, B, 4 * B))
    return jnp.stack(out)


def _sink_rows(sinks):
    return jnp.pad(sinks.astype(F32)[:, None], ((0, 0), (0, BLOCK - 1)))


def _block_diag2(x):
    z = jnp.zeros_like(x)
    return jnp.concatenate([jnp.concatenate([x, z], axis=1), jnp.concatenate([z, x], axis=1)], axis=0)


def _attn_core(q_ref, kp_ref, kc_ref, vp_ref, vc_ref, sink_ref, bias_ref, n_heads):
    B, hd = BLOCK, HEAD_DIM
    ppg = n_heads // N_KV // 2
    pairs = range(n_heads // 2)
    row0 = lax.broadcasted_iota(jnp.int32, (2 * B, N_KV * hd), 0) == 0
    k2 = jnp.concatenate([kp_ref[...], kc_ref[...]], axis=0)
    v2 = jnp.concatenate([vp_ref[...], vc_ref[...]], axis=0)
    k2 = jnp.where(row0, jnp.zeros_like(k2), k2)
    v2 = jnp.where(row0, jnp.zeros_like(v2), v2)
    scale = hd ** -0.5
    kks = [_block_diag2(k2[:, kv * hd:(kv + 1) * hd] * scale) for kv in range(N_KV)]
    vvs = [_block_diag2(v2[:, kv * hd:(kv + 1) * hd]) for kv in range(N_KV)]
    first_rows = lax.broadcasted_iota(jnp.int32, (4 * B, 4 * hd), 0) < 2 * B
    first_lanes = lax.broadcasted_iota(jnp.int32, (4 * B, 4 * hd), 1) < 2 * hd
    ones = (first_rows == first_lanes).astype(BF16)
    vxs = [jnp.concatenate([vv, ones], axis=1) for vv in vvs]
    ss = [_dot(q_ref[:, pr * 2 * hd:(pr + 1) * 2 * hd], kks[pr // ppg], 1, 1) + bias_ref[pr] for pr in pairs]
    halves = []
    for h in range(n_heads):
        s = ss[h // 2][:, (h % 2) * 2 * B:(h % 2 + 1) * 2 * B]
        halves.append(jnp.concatenate([s[:, :B] + sink_ref[h:h + 1, :], s[:, B:]], axis=1))
    mxs = [jnp.max(s, axis=-1, keepdims=True) for s in halves]
    es = [jnp.exp(s - m) for s, m in zip(halves, mxs)]
    rs = [_dot(jnp.concatenate([es[2 * pr], es[2 * pr + 1]], axis=1).astype(BF16), vxs[pr // ppg], 1, 0) for pr in pairs]
    return kks, vvs, es, rs


def _attn_fwd(qkv, sink_rows, bias, n_heads, name, side=None):
    T = qkv.shape[0]
    B = BLOCK
    hd = HEAD_DIM
    HQ = n_heads * hd
    KVW = N_KV * hd
    assert (n_heads // N_KV) % 2 == 0, "heads are processed in pairs that share a key/value head"
    nb = T // B
    kcol = HQ // KVW

    def body(q_ref, kp_ref, kc_ref, vp_ref, vc_ref, sink_ref, bias_ref, o_ref):
        _, _, _, rs = _attn_core(q_ref, kp_ref, kc_ref, vp_ref, vc_ref, sink_ref, bias_ref, n_heads)
        low = lax.broadcasted_iota(jnp.int32, (B, 2 * hd), 1) < hd
        for pr, r in enumerate(rs):
            inv = jnp.where(low, 1.0 / r[:, 2 * hd:4 * hd], 1.0 / r[:, 4 * hd:])
            o_ref[:, pr * 2 * hd:(pr + 1) * 2 * hd] = (r[:, :2 * hd] * inv).astype(BF16)

    def prev(n):
        return jnp.maximum(n - 1, 0)

    in_specs = [pl.BlockSpec((B, HQ), lambda n: (n, 0)),
                pl.BlockSpec((B, KVW), lambda n: (prev(n), kcol)), pl.BlockSpec((B, KVW), lambda n: (n, kcol)),
                pl.BlockSpec((B, KVW), lambda n: (prev(n), kcol + 1)), pl.BlockSpec((B, KVW), lambda n: (n, kcol + 1)),
                pl.BlockSpec((n_heads, B), lambda n: (0, 0)),
                pl.BlockSpec((None, n_heads // 2, B, 4 * B), lambda n: (jnp.minimum(n, 1), 0, 0, 0))]
    return _call(body, name, (nb,), in_specs, pl.BlockSpec((B, HQ), lambda n: (n, 0)),
                 jax.ShapeDtypeStruct((T, HQ), BF16), (qkv, qkv, qkv, qkv, qkv, sink_rows, bias), side=side)


def _attn_bwd(qkv, do, sink_rows, bias, n_heads, name, side=None):
    T = qkv.shape[0]
    B = BLOCK
    hd = HEAD_DIM
    HQ = n_heads * hd
    KVW = N_KV * hd
    group = n_heads // N_KV
    assert group % 2 == 0, "heads are processed in pairs that share a key/value head"
    ppg = group // 2
    nb = T // B
    kcol = HQ // KVW
    scale = hd ** -0.5

    def body(q_ref, kp_ref, kc_ref, vp_ref, vc_ref, do_ref, sink_ref, bias_ref,
             dq_ref, dkv_ref, dbq_ref, dbk_ref, dbv_ref, dsink_ref, ck_ref, cv_ref):
        n = pl.program_id(0)
        first = n == 0

        @pl.when(first)
        def _():
            for r in (dbq_ref, dbk_ref, dbv_ref, dsink_ref, ck_ref, cv_ref):
                r[...] = jnp.zeros_like(r)

        @pl.when(n < nb)
        def _():
            pairs = range(n_heads // 2)
            heads = range(n_heads)
            lanes = [slice(pr * 2 * hd, (pr + 1) * 2 * hd) for pr in pairs]
            kks, vvs, es, rs = _attn_core(q_ref, kp_ref, kc_ref, vp_ref, vc_ref, sink_ref, bias_ref, n_heads)
            low = lax.broadcasted_iota(jnp.int32, (B, 2 * hd), 1) < hd
            slot0 = lax.broadcasted_iota(jnp.int32, (B, B), 1) == 0
            lane_head = lax.broadcasted_iota(jnp.int32, (2 * hd, 4 * hd), 0) < hd
            out_head = lax.broadcasted_iota(jnp.int32, (2 * hd, 4 * hd), 1) < 2 * hd
            ones2 = (lane_head == out_head).astype(BF16)
            invs = [1.0 / rs[h // 2][:, (2 + 2 * (h % 2)) * hd:(4 + 2 * (h % 2)) * hd] for h in heads]
            dops = [do_ref[:, lanes[pr]] for pr in pairs]
            qps = [q_ref[:, lanes[pr]] for pr in pairs]
            dos = [dops[pr].astype(F32) * rs[pr][:, :2 * hd] * jnp.where(low, invs[2 * pr], invs[2 * pr + 1])
                   for pr in pairs]
            his = [t.astype(BF16) for t in dos]
            los = [(t - hi.astype(F32)).astype(BF16) for t, hi in zip(dos, his)]
            dsums = [_dot(hi, ones2, 1, 0) + _dot(lo, ones2, 1, 0) for hi, lo in zip(his, los)]
            dps = [_dot(dops[pr], vvs[pr // ppg], 1, 1) for pr in pairs]
            wide = lambda a: jnp.concatenate([a, a], axis=1)
            p_s = [es[h] * wide(invs[h]) for h in heads]
            ds_s = [p_s[h] * (dps[h // 2][:, (h % 2) * 2 * B:(h % 2 + 1) * 2 * B]
                              - wide(dsums[h // 2][:, (h % 2) * 2 * hd:(h % 2 + 1) * 2 * hd])) for h in heads]
            for h in heads:
                dsink_ref[h:h + 1, :] += jnp.sum(ds_s[h][:, :B], axis=0, keepdims=True)

            def without_slot0(a):
                return jnp.concatenate([jnp.where(slot0, 0.0, a[:, :B]), a[:, B:]], axis=1).astype(BF16)

            ds_b = [without_slot0(a) for a in ds_s]
            p_b = [without_slot0(a) for a in p_s]
            dqs = [_dot(jnp.concatenate([ds_b[2 * pr], ds_b[2 * pr + 1]], axis=1), kks[pr // ppg], 1, 0) for pr in pairs]
            for pr in pairs:
                dq_ref[:, lanes[pr]] = dqs[pr].astype(BF16)
                dbq_ref[:, lanes[pr]] += jnp.sum(dqs[pr], axis=0, keepdims=True)
            keeps = [low, jnp.logical_not(low)]
            qms = [jnp.where(keeps[h % 2], qps[h // 2], jnp.zeros_like(qps[0])) for h in heads]
            dms = [jnp.where(keeps[h % 2], dops[h // 2], jnp.zeros_like(dops[0])) for h in heads]
            dkh = [_dot(ds_b[h], qms[h], 0, 0) for h in heads]
            dvh = [_dot(p_b[h], dms[h], 0, 0) for h in heads]
            dks, dvs = [], []
            for kv in range(N_KV):
                tk = functools.reduce(jnp.add, dkh[kv * group:(kv + 1) * group])
                tv = functools.reduce(jnp.add, dvh[kv * group:(kv + 1) * group])
                dks.append((tk[:, :hd] + tk[:, hd:]) * scale)
                dvs.append(tv[:, :hd] + tv[:, hd:])
            dk2 = jnp.concatenate(dks, axis=1)
            dv2 = jnp.concatenate(dvs, axis=1)
            dbk_ref[...] += jnp.sum(dk2, axis=0, keepdims=True)
            dbv_ref[...] += jnp.sum(dv2, axis=0, keepdims=True)
            dkv_ref[:, :KVW] = (ck_ref[...] + dk2[:B]).astype(BF16)
            dkv_ref[:, KVW:] = (cv_ref[...] + dv2[:B]).astype(BF16)
            ck_ref[...] = dk2[B:]
            cv_ref[...] = dv2[B:]

        @pl.when(n == nb)
        def _():
            dkv_ref[:, :KVW] = ck_ref[...].astype(BF16)
            dkv_ref[:, KVW:] = cv_ref[...].astype(BF16)

    def cur(n):
        return jnp.minimum(n, nb - 1)

    def prev(n):
        return jnp.maximum(cur(n) - 1, 0)

    def outp(n):
        return jnp.maximum(n - 1, 0)

    in_specs = [pl.BlockSpec((B, HQ), lambda n: (cur(n), 0)),
                pl.BlockSpec((B, KVW), lambda n: (prev(n), kcol)), pl.BlockSpec((B, KVW), lambda n: (cur(n), kcol)),
                pl.BlockSpec((B, KVW), lambda n: (prev(n), kcol + 1)), pl.BlockSpec((B, KVW), lambda n: (cur(n), kcol + 1)),
                pl.BlockSpec((B, HQ), lambda n: (cur(n), 0)), pl.BlockSpec((n_heads, B), lambda n: (0, 0)),
                pl.BlockSpec((None, n_heads // 2, B, 4 * B), lambda n: (jnp.minimum(n, 1), 0, 0, 0))]
    out_shape = (jax.ShapeDtypeStruct((T, HQ), BF16), jax.ShapeDtypeStruct((T, 2 * KVW), BF16),
                 jax.ShapeDtypeStruct((1, HQ), F32), jax.ShapeDtypeStruct((1, KVW), F32),
                 jax.ShapeDtypeStruct((1, KVW), F32), jax.ShapeDtypeStruct((n_heads, B), F32))
    out_specs = (pl.BlockSpec((B, HQ), lambda n: (cur(n), 0)), pl.BlockSpec((B, 2 * KVW), lambda n: (outp(n), 0)),
                 _vec_spec(HQ), _vec_spec(KVW), _vec_spec(KVW), pl.BlockSpec((n_heads, B), lambda n: (0, 0)))
    return _call(body, name, (nb + 1,), in_specs, out_specs, out_shape, (qkv, qkv, qkv, qkv, qkv, do, sink_rows, bias),
                 scratch=[pltpu.VMEM((B, KVW), F32), pltpu.VMEM((B, KVW), F32)], side=side)


def _sum8(r, name):
    _, R, C = r.shape
    tr = _tile(R, 512, 16)

    def body(r_ref, o_ref):
        acc = r_ref[0].astype(F32)
        for d in range(1, N_DEV):
            acc = acc + r_ref[d].astype(F32)
        o_ref[...] = acc

    return _call(body, name, (R // tr,), [pl.BlockSpec((N_DEV, tr, C), lambda i: (0, i, 0))],
                 pl.BlockSpec((tr, C), lambda i: (i, 0)), jax.ShapeDtypeStruct((R, C), F32), (r,))


def _adamw(w, g, m, v, name):
    R, C = w.shape
    tr = _tile(R, 512, 8)
    c1 = 1.0 - ADAM_B1 ** ADAM_STEP
    c2 = 1.0 - ADAM_B2 ** ADAM_STEP

    def body(w_ref, g_ref, m_ref, v_ref, d_ref, nm_ref, nv_ref):
        gv = g_ref[...]
        nm = ADAM_B1 * m_ref[...] + (1.0 - ADAM_B1) * gv
        nv = ADAM_B2 * v_ref[...] + (1.0 - ADAM_B2) * (gv * gv)
        nm_ref[...] = nm
        nv_ref[...] = nv
        d_ref[...] = -ADAM_LR * ((nm / c1) / (jnp.sqrt(nv / c2) + ADAM_EPS) + ADAM_WD * w_ref[...])

    spec = pl.BlockSpec((tr, C), lambda i: (i, 0))
    shp = jax.ShapeDtypeStruct((R, C), F32)
    return _call(body, name, (R // tr,), [spec] * 4, (spec,) * 3, (shp, shp, shp), (w, g, m, v))


def _adamw_nd(w, g, m, v, name):
    shape = w.shape
    c = shape[-1]
    f = lambda a: a.reshape(-1, c)
    d, nm, nv = _adamw(f(w), f(g), f(m), f(v), name)
    return d.reshape(shape), nm.reshape(shape), nv.reshape(shape)


def _pack(arrs, width):
    flat = jnp.concatenate([a.reshape(-1).astype(F32) for a in arrs])
    n = flat.shape[0]
    quantum = 8 * width
    total = -(-n // quantum) * quantum
    return jnp.pad(flat, (0, total - n)).reshape(-1, width)


def _unpack(flat, shapes):
    out, off = [], 0
    for s in shapes:
        n = math.prod(s)
        out.append(flat[off:off + n].reshape(s))
        off += n
    return out


def kernel(x, c, norm1_g, norm2_g, ada_w, ada_b, attn_wqkv, attn_bqkv, attn_sinks, attn_wo, attn_bo, conv_w_in, conv_b_in, conv_dw, conv_dw_b, conv_ln_g, conv_ln_b, conv_w_out, conv_b_out, sgu_w_in, sgu_b_in, sgu_ln_g, sgu_ln_b, sgu_ws, sgu_bs, sgu_w_out, sgu_b_out, ffn_w_in, ffn_dw, ffn_dw_b, ffn_w_out, final_g, loss_target, m_norm1_g, m_norm2_g, m_ada_w, m_ada_b, m_attn_wqkv, m_attn_bqkv, m_attn_sinks, m_attn_wo, m_attn_bo, m_conv_w_in, m_conv_b_in, m_conv_dw, m_conv_dw_b, m_conv_ln_g, m_conv_ln_b, m_conv_w_out, m_conv_b_out, m_sgu_w_in, m_sgu_b_in, m_sgu_ln_g, m_sgu_ln_b, m_sgu_ws, m_sgu_bs, m_sgu_w_out, m_sgu_b_out, m_ffn_w_in, m_ffn_dw, m_ffn_dw_b, m_ffn_w_out, m_final_g, v_norm1_g, v_norm2_g, v_ada_w, v_ada_b, v_attn_wqkv, v_attn_bqkv, v_attn_sinks, v_attn_wo, v_attn_bo, v_conv_w_in, v_conv_b_in, v_conv_dw, v_conv_dw_b, v_conv_ln_g, v_conv_ln_b, v_conv_w_out, v_conv_b_out, v_sgu_w_in, v_sgu_b_in, v_sgu_ln_g, v_sgu_ln_b, v_sgu_ws, v_sgu_bs, v_sgu_w_out, v_sgu_b_out, v_ffn_w_in, v_ffn_dw, v_ffn_dw_b, v_ffn_w_out, v_final_g):
    W = dict(norm1_g=norm1_g, norm2_g=norm2_g, ada_w=ada_w, ada_b=ada_b, attn_wqkv=attn_wqkv, attn_bqkv=attn_bqkv, attn_sinks=attn_sinks, attn_wo=attn_wo, attn_bo=attn_bo, conv_w_in=conv_w_in, conv_b_in=conv_b_in, conv_dw=conv_dw, conv_dw_b=conv_dw_b, conv_ln_g=conv_ln_g, conv_ln_b=conv_ln_b, conv_w_out=conv_w_out, conv_b_out=conv_b_out, sgu_w_in=sgu_w_in, sgu_b_in=sgu_b_in, sgu_ln_g=sgu_ln_g, sgu_ln_b=sgu_ln_b, sgu_ws=sgu_ws, sgu_bs=sgu_bs, sgu_w_out=sgu_w_out, sgu_b_out=sgu_b_out, ffn_w_in=ffn_w_in, ffn_dw=ffn_dw, ffn_dw_b=ffn_dw_b, ffn_w_out=ffn_w_out, final_g=final_g)
    MOM = dict(norm1_g=m_norm1_g, norm2_g=m_norm2_g, ada_w=m_ada_w, ada_b=m_ada_b, attn_wqkv=m_attn_wqkv, attn_bqkv=m_attn_bqkv, attn_sinks=m_attn_sinks, attn_wo=m_attn_wo, attn_bo=m_attn_bo, conv_w_in=m_conv_w_in, conv_b_in=m_conv_b_in, conv_dw=m_conv_dw, conv_dw_b=m_conv_dw_b, conv_ln_g=m_conv_ln_g, conv_ln_b=m_conv_ln_b, conv_w_out=m_conv_w_out, conv_b_out=m_conv_b_out, sgu_w_in=m_sgu_w_in, sgu_b_in=m_sgu_b_in, sgu_ln_g=m_sgu_ln_g, sgu_ln_b=m_sgu_ln_b, sgu_ws=m_sgu_ws, sgu_bs=m_sgu_bs, sgu_w_out=m_sgu_w_out, sgu_b_out=m_sgu_b_out, ffn_w_in=m_ffn_w_in, ffn_dw=m_ffn_dw, ffn_dw_b=m_ffn_dw_b, ffn_w_out=m_ffn_w_out, final_g=m_final_g)
    VAR = dict(norm1_g=v_norm1_g, norm2_g=v_norm2_g, ada_w=v_ada_w, ada_b=v_ada_b, attn_wqkv=v_attn_wqkv, attn_bqkv=v_attn_bqkv, attn_sinks=v_attn_sinks, attn_wo=v_attn_wo, attn_bo=v_attn_bo, conv_w_in=v_conv_w_in, conv_b_in=v_conv_b_in, conv_dw=v_conv_dw, conv_dw_b=v_conv_dw_b, conv_ln_g=v_conv_ln_g, conv_ln_b=v_conv_ln_b, conv_w_out=v_conv_w_out, conv_b_out=v_conv_b_out, sgu_w_in=v_sgu_w_in, sgu_b_in=v_sgu_b_in, sgu_ln_g=v_sgu_ln_g, sgu_ln_b=v_sgu_ln_b, sgu_ws=v_sgu_ws, sgu_bs=v_sgu_bs, sgu_w_out=v_sgu_w_out, sgu_b_out=v_sgu_b_out, ffn_w_in=v_ffn_w_in, ffn_dw=v_ffn_dw, ffn_dw_b=v_ffn_dw_b, ffn_w_out=v_ffn_w_out, final_g=v_final_g)
    ORDER = list(W)

    _, T, D = x.shape
    depth = norm1_g.shape[0]
    n_heads = D // HEAD_DIM
    me = 4 * lax.axis_index("x") + 2 * lax.axis_index("y") + lax.axis_index("c")
    x0 = x.reshape(T, D)
    tgt = loss_target.reshape(T, D)

    small_sharded = ["attn_bqkv", "attn_bo", "conv_dw", "sgu_b_in", "sgu_ln_g", "sgu_ln_b", "sgu_b_out", "ffn_dw"]
    s_in = [c] + [W[n] for n in small_sharded]
    s_shapes = [a.shape for a in s_in]
    gathered_small = _all_gather(_pack(s_in, 128), "gather_small").reshape(N_DEV, -1)
    s_offs = [sum(math.prod(s) for s in s_shapes[:i]) for i in range(len(s_shapes))]

    def full(idx):
        shp = s_shapes[idx]
        a = gathered_small[:, s_offs[idx]:s_offs[idx] + math.prod(shp)].reshape((N_DEV,) + shp)
        return jnp.moveaxis(a, 0, -2).reshape(shp[:-1] + (N_DEV * shp[-1],))

    c_all = full(0).reshape(N_DEV, D)
    F_small = {n: full(1 + i) for i, n in enumerate(small_sharded)}
    attn_bias = _attn_bias(n_heads)

    c_act = c_all * jax.nn.sigmoid(c_all)
    c_pad = jnp.pad(c_act, ((0, 8), (0, 0))).astype(BF16)
    n_ada = ada_w.shape[-1]
    ada_cols = lax.dynamic_slice_in_dim(ada_b, me * n_ada, n_ada, axis=1)
    mod_loc = jnp.stack([_mm(c_pad, ada_w[i].astype(BF16), "nn", "ada_mod", bias=ada_cols[i:i + 1])
                         for i in range(depth)])
    mod_all = _all_gather(mod_loc, "gather_mod")
    mod_mine = lax.dynamic_index_in_dim(mod_all, me, axis=2, keepdims=False)
    mod = jnp.transpose(mod_mine, (1, 0, 2)).reshape(depth, 6, 1, D)

    mixer_names = {0: ("attn_wqkv", "attn_wo"), 1: ("conv_w_in", "conv_w_out"), 2: ("sgu_w_in", "sgu_w_out")}

    def piece(part, i, which):
        w_in, w_out = ("ffn_w_in", "ffn_w_out") if part == "ffn" else mixer_names[i % 3]
        l = i if part == "ffn" else i // 3
        return (w_in, l, W[w_in].shape[-1]) if which == "in" else (w_out, l, W[w_out].shape[1])

    def both(part, i):
        return [(part, i, "in"), (part, i, "out")] if 0 <= i < depth else []

    def local_weight(key):
        n, l, _ = piece(*key)
        return (W[n][l].T if key[2] == "in" else W[n][l]).astype(BF16)

    WF = {}

    def land_weights(keys, gathered):
        for key, g in zip(keys, gathered):
            n, l, r = piece(*key)
            WF[(n, l)] = g.reshape(N_DEV * r, D)

    def with_gather(keys, run):
        if not keys:
            return run(None)
        *outs, landed = run(("gather", [local_weight(k) for k in keys]))
        land_weights(keys, landed)
        return outs[0] if len(outs) == 1 else tuple(outs)

    def vec(a):
        return a.reshape(1, -1)

    land_weights(both("mix", 0)[:1], _exchange("gather", [local_weight(k) for k in both("mix", 0)[:1]], "gather_first"))

    saved = []
    xs, y_prev, gate_prev = x0, None, None
    for i in range(depth):
        sh1, sc1, g1, sh2, sc2, g2 = [mod[i, k] for k in range(6)]
        kind, j = i % 3, i // 3
        st = dict(kind=kind, j=j)
        if y_prev is None:
            h1 = _norm_mod_fwd(xs, vec(norm1_g[i]), sc1, sh1, "norm1_fwd")
        else:
            xs, h1 = _norm_mod_fwd(xs, vec(norm1_g[i]), sc1, sh1, "norm1_fwd", y=y_prev, gate=gate_prev)
        st.update(x_in=xs, h1=h1)
        first_ffn_in = [("ffn", 0, "in")] if i == 0 else []
        if kind == 0:
            qkv = with_gather(both("mix", 0)[1:] if i == 0 else [], lambda side: _mm(
                h1, WF[("attn_wqkv", j)], "nt", "attn_qkv", out_dtype=BF16, bias=vec(F_small["attn_bqkv"][j]), side=side))
            o = with_gather(first_ffn_in, lambda side: _attn_fwd(qkv, _sink_rows(attn_sinks[j]), attn_bias, n_heads,
                                                                 "attn_fwd", side=side))
            y1 = _mm(o, WF[("attn_wo", j)], "nn", "attn_out", bias=vec(F_small["attn_bo"][j]))
            st.update(qkv=qkv, o=o)
        elif kind == 1:
            p = _mm(h1, WF[("conv_w_in", j)], "nt", "conv_in", out_dtype=BF16, bias=vec(conv_b_in[j]))
            zc, s = with_gather(first_ffn_in, lambda side: _convmix_fwd(
                p, F_small["conv_dw"][j], vec(conv_dw_b[j]), vec(conv_ln_g[j]), vec(conv_ln_b[j]), "convmix_fwd", side=side))
            y1 = _mm(s, WF[("conv_w_out", j)], "nn", "conv_out", bias=vec(conv_b_out[j]))
            st.update(p=p, zc=zc, s=s)
        else:
            p = _mm(h1, WF[("sgu_w_in", j)], "nt", "sgu_in", out_dtype=BF16, bias=vec(F_small["sgu_b_in"][j]))
            mm_ = _sgu_fwd(p, vec(F_small["sgu_ln_g"][j]), vec(F_small["sgu_ln_b"][j]), sgu_ws[j], sgu_bs[j].T, "sgu_fwd")
            y1 = _mm(mm_, WF[("sgu_w_out", j)], "nn", "sgu_out", bias=vec(F_small["sgu_b_out"][j]))
            st.update(p=p, m=mm_)
        xs, h2 = _norm_mod_fwd(xs, vec(norm2_g[i]), sc2, sh2, "norm2_fwd", y=y1, gate=g1)
        nxt = i + 1
        z = with_gather(both("mix", nxt) + ([("ffn", 0, "out")] if i == 0 else []),
                        lambda side: _mm(h2, WF[("ffn_w_in", i)], "nt", "ffn_in", out_dtype=BF16, side=side))
        a = with_gather(both("ffn", nxt)[:1],
                        lambda side: _ffn_mid_fwd(z, F_small["ffn_dw"][i], vec(ffn_dw_b[i]), "ffn_mid_fwd", side=side))
        y2 = with_gather(both("ffn", nxt)[1:], lambda side: _mm(a, WF[("ffn_w_out", i)], "nn", "ffn_out", side=side))
        st.update(y1=y1, x_mid=xs, h2=h2, z=z, a=a, y2=y2)
        saved.append(st)
        y_prev, gate_prev = y2, g2

    dx, d_final_g, loss_row, dy2, dg2 = _final_loss(xs, y_prev, gate_prev, tgt, vec(final_g), "final_loss")
    loss = lax.psum(loss_row[0, 0], ("x", "y", "c"))

    G = {n: [None] * W[n].shape[0] for n in ORDER if n != "final_g"}
    GW = {}
    recv = {}
    dmod = [None] * depth

    def with_scatter(keys, run):
        if not keys:
            return run(None)
        sends = [GW[piece(*k)[:2]].reshape(N_DEV, piece(*k)[2], D) for k in keys]
        *outs, landed = run(("scatter", sends))
        recv.update(zip(keys, landed))
        return outs[0] if len(outs) == 1 else tuple(outs)

    for i in reversed(range(depth)):
        st = saved[i]
        sh1, sc1, g1, sh2, sc2, g2 = [mod[i, k] for k in range(6)]
        kind, j = st["kind"], st["j"]
        last_out = [("mix", 0, "out")] if i == 0 else []
        last_in = [("mix", 0, "in")] if i == 0 else []
        GW[("ffn_w_out", i)] = _mm(st["a"], dy2, "tn", "ffn_out_dw", out_dtype=BF16)
        da = _mm(dy2, WF[("ffn_w_out", i)], "nt", "ffn_out_dx", out_dtype=BF16)
        dzg, dzu, ddw, ddwb = with_scatter(both("mix", i + 1) + [("ffn", i, "out")], lambda side: _ffn_mid_bwd(
            st["z"], da, F_small["ffn_dw"][i], vec(ffn_dw_b[i]), "ffn_mid_bwd", side=side))
        G["ffn_dw"][i], G["ffn_dw_b"][i] = ddw, ddwb[0]
        GW[("ffn_w_in", i)] = _mm(dzg, st["h2"], "tn", "ffn_in_dw", out_dtype=BF16, a2=dzu)
        dh2 = with_scatter([("ffn", i, "in")],
                           lambda side: _mm(dzg, WF[("ffn_w_in", i)], "nn", "ffn_in_dx", out_dtype=BF16, a2=dzu, side=side))
        dx, dn2, dsc2, dsh2, dy1, dg1, dbo = _norm_mod_bwd(dh2, st["x_mid"], vec(norm2_g[i]), sc2, sh2, dx, "norm_bwd",
                                                           y_up=st["y1"], gate_up=g1)
        G["norm2_g"][i] = dn2[0]
        if kind == 0:
            G["attn_bo"][j] = dbo[0]
            GW[("attn_wo", j)] = _mm(st["o"], dy1, "tn", "attn_out_dw", out_dtype=BF16)
            do = _mm(dy1, WF[("attn_wo", j)], "nt", "attn_out_dx", out_dtype=BF16)
            dq, dkv, dbq, dbk, dbv, dsk = with_scatter(last_out, lambda side: _attn_bwd(
                st["qkv"], do, _sink_rows(attn_sinks[j]), attn_bias, n_heads, "attn_bwd", side=side))
            G["attn_bqkv"][j] = jnp.concatenate([dbq, dbk, dbv], axis=1)[0]
            G["attn_sinks"][j] = dsk[:, 0]
            GW[("attn_wqkv", j)] = _mm(dq, st["h1"], "tn", "attn_qkv_dw", out_dtype=BF16, a2=dkv)
            dh1 = with_scatter(last_in, lambda side: _mm(dq, WF[("attn_wqkv", j)], "nn", "attn_qkv_dx", out_dtype=BF16,
                                                         a2=dkv, side=side))
        elif kind == 1:
            G["conv_b_out"][j] = dbo[0]
            GW[("conv_w_out", j)] = _mm(st["s"], dy1, "tn", "conv_out_dw", out_dtype=BF16)
            ds = _mm(dy1, WF[("conv_w_out", j)], "nt", "conv_out_dx", out_dtype=BF16)
            dp, dbin, ddw, ddwb, dlng, dlnb = with_scatter(last_out, lambda side: _convmix_bwd(
                ds, st["zc"], st["p"], F_small["conv_dw"][j], vec(conv_ln_g[j]), vec(conv_ln_b[j]), "convmix_bwd",
                side=side))
            G["conv_b_in"][j], G["conv_dw"][j], G["conv_dw_b"][j] = dbin[0], ddw, ddwb[0]
            G["conv_ln_g"][j], G["conv_ln_b"][j] = dlng[0], dlnb[0]
            GW[("conv_w_in", j)] = _mm(dp, st["h1"], "tn", "conv_in_dw", out_dtype=BF16)
            dh1 = with_scatter(last_in, lambda side: _mm(dp, WF[("conv_w_in", j)], "nn", "conv_in_dx", out_dtype=BF16,
                                                         side=side))
        else:
            G["sgu_b_out"][j] = dbo[0]
            GW[("sgu_w_out", j)] = _mm(st["m"], dy1, "tn", "sgu_out_dw", out_dtype=BF16)
            dm = _mm(dy1, WF[("sgu_w_out", j)], "nt", "sgu_out_dx", out_dtype=BF16)
            dp, dbin, dlng, dlnb, dws, dbst = with_scatter(last_out, lambda side: _sgu_bwd(
                st["p"], dm, vec(F_small["sgu_ln_g"][j]), vec(F_small["sgu_ln_b"][j]), sgu_ws[j], sgu_bs[j].T, "sgu_bwd",
                side=side))
            G["sgu_b_in"][j], G["sgu_ln_g"][j], G["sgu_ln_b"][j] = dbin[0], dlng[0], dlnb[0]
            G["sgu_ws"][j], G["sgu_bs"][j] = dws, dbst.T
            GW[("sgu_w_in", j)] = _mm(dp, st["h1"], "tn", "sgu_in_dw", out_dtype=BF16)
            dh1 = with_scatter(last_in, lambda side: _mm(dp, WF[("sgu_w_in", j)], "nn", "sgu_in_dx", out_dtype=BF16,
                                                         side=side))
        dmod_i = lambda: jnp.concatenate([dsh1, dsc1, dg1, dsh2, dsc2, dg2], axis=1)[0]
        if i > 0:
            dx, dn1, dsc1, dsh1, dy2_below, dg2_below, _ = _norm_mod_bwd(
                dh1, st["x_in"], vec(norm1_g[i]), sc1, sh1, dx, "norm_bwd", y_up=saved[i - 1]["y2"], gate_up=mod[i - 1, 5])
            dmod[i] = dmod_i()
            dy2, dg2 = dy2_below, dg2_below
        else:
            dx, dn1, dsc1, dsh1 = _norm_mod_bwd(dh1, st["x_in"], vec(norm1_g[i]), sc1, sh1, dx, "norm_bwd")
            dmod[i] = dmod_i()
        G["norm1_g"][i] = dn1[0]
    grad_x = dx.reshape(x.shape)

    small_names = [n for n in ORDER if n not in
                   ("ada_w", "ada_b", "attn_wqkv", "attn_wo", "conv_w_in", "conv_w_out", "sgu_w_in", "sgu_w_out",
                    "ffn_w_in", "ffn_w_out", "final_g")]
    g_small = {n: jnp.stack(G[n]) for n in small_names}
    g_small["final_g"] = d_final_g[0]
    g_small["ada_b"] = jnp.stack(dmod)
    names16 = ["ffn_dw", "conv_dw", "sgu_ws", "ffn_dw_b", "ada_b"]
    names32 = [n for n in g_small if n not in names16]
    shapes16, shapes32 = [g_small[n].shape for n in names16], [g_small[n].shape for n in names32]
    gathered32, gathered16 = _exchange("gather", [_pack([g_small[n] for n in names32], 1024),
                                                  _pack([g_small[n] for n in names16], 1024).astype(BF16)],
                                       "gather_small_grads")
    gsum = dict(zip(names32, _unpack(_sum8(gathered32, "sum_small_grads").reshape(-1), shapes32)))
    gsum.update(zip(names16, _unpack(_sum8(gathered16, "sum_small_grads").reshape(-1), shapes16)))

    def local_shard(n, a):
        if n in small_sharded:
            w = W[n].shape[-1]
            return lax.dynamic_slice_in_dim(a, me * w, w, axis=a.ndim - 1)
        return a

    gsum = {n: local_shard(n, a) for n, a in gsum.items()}

    off16 = sum(math.prod(s) for s in shapes16[:-1])
    dmod_all = gathered16.reshape(N_DEV, -1)[:, off16:off16 + math.prod(shapes16[-1])].reshape((N_DEV,) + shapes16[-1])
    dmod_cols = lax.dynamic_slice_in_dim(dmod_all, me * n_ada, n_ada, axis=2)
    dmod_pad = jnp.pad(dmod_cols, ((0, 8), (0, 0), (0, 0)))
    g_ada_w = jnp.stack([_mm(c_pad, dmod_pad[:, i, :], "tn", "ada_dw") for i in range(depth)])

    big = {}
    for i in range(depth):
        for key in both("mix", i) + both("ffn", i):
            n, l, _ = piece(*key)
            gp = _sum8(recv[key], "sum_weight_grads")
            big.setdefault(n, {})[l] = gp.T if key[2] == "in" else gp
    grads = {n: jnp.stack([v[l] for l in range(len(v))]) for n, v in big.items()}
    grads["ada_w"] = g_ada_w
    grads.update(gsum)

    delta, new_m, new_v = {}, {}, {}
    big_names = ["ada_w", "attn_wqkv", "attn_wo", "conv_w_in", "conv_w_out", "sgu_w_in", "sgu_w_out", "ffn_w_in",
                 "ffn_w_out"]
    for n in big_names:
        delta[n], new_m[n], new_v[n] = _adamw_nd(W[n], grads[n], MOM[n], VAR[n], "adamw_" + n)
    rest = [n for n in ORDER if n not in big_names]
    rest_shapes = [W[n].shape for n in rest]
    pk = lambda d: _pack([d[n] for n in rest], 128)
    d_s, m_s, v_s = _adamw(pk(W), pk(grads), pk(MOM), pk(VAR), "adamw_small")
    for n, a, b_, c_ in zip(rest, _unpack(d_s.reshape(-1), rest_shapes), _unpack(m_s.reshape(-1), rest_shapes),
                            _unpack(v_s.reshape(-1), rest_shapes)):
        delta[n], new_m[n], new_v[n] = a, b_, c_

    return (loss, grad_x, *[grads[n] for n in ORDER], *[delta[n] for n in ORDER],
            *[new_m[n] for n in ORDER], *[new_v[n] for n in ORDER])
```

```python
import functools
import math

import jax
import jax.numpy as jnp
from jax import lax
from jax.experimental import pallas as pl
from jax.experimental.pallas import tpu as pltpu

F32 = jnp.float32
BF16 = jnp.bfloat16

N_DEV = 8
HEAD_DIM = 64
N_KV = 4
BLOCK = 128
NORM_EPS = 1e-6
NEG_INF = -1e30
ADAM_LR = 0.001
ADAM_B1 = 0.9
ADAM_B2 = 0.999
ADAM_EPS = 1e-08
ADAM_WD = 0.01
ADAM_STEP = 10
V7X_VMEM_LIMIT = 56 * 1024 * 1024
MESH = pl.DeviceIdType.MESH


def _cparams():
    return pltpu.CompilerParams(vmem_limit_bytes=V7X_VMEM_LIMIT)


def _tile(n, cap, align):
    best = None
    for t in range(align, min(n, cap) + 1, align):
        if n % t == 0:
            best = t
    return best if best is not None else n


def _dot(a, b, ca, cb):
    return lax.dot_general(a, b, (((ca,), (cb,)), ((), ())), preferred_element_type=F32)


def _sigmoid(x):
    return 1.0 / (1.0 + jnp.exp(-x))


def _rows(x, start, n):
    return lax.slice_in_dim(x, start, start + n, axis=0)


def _my_coords():
    return lax.axis_index("x"), lax.axis_index("y"), lax.axis_index("c")


def _peer(m, mx, my, mc):
    px = (mx + ((m >> 2) & 1)) % 2
    py = (my + ((m >> 1) & 1)) % 2
    pc = (mc + (m & 1)) % 2
    return px, py, pc


def _exchange_copies(kind, x_ref, o_ref, send_sems, recv_sems, local_sem):
    mx, my, mc = _my_coords()
    me = 4 * mx + 2 * my + mc
    local = pltpu.make_async_copy(x_ref if kind == "gather" else x_ref.at[me], o_ref.at[me], local_sem)
    remote = []
    for m in range(1, N_DEV):
        px, py, pc = _peer(m, mx, my, mc)
        src = x_ref if kind == "gather" else x_ref.at[4 * px + 2 * py + pc]
        remote.append(pltpu.make_async_remote_copy(
            src_ref=src, dst_ref=o_ref.at[me], send_sem=send_sems.at[m - 1], recv_sem=recv_sems.at[m - 1],
            device_id=(px, py, pc), device_id_type=MESH))
    return local, remote


def _exchange_start(kind, *refs):
    local, remote = _exchange_copies(kind, *refs)
    local.start()
    for cp in remote:
        cp.start()


def _exchange_wait(kind, *refs):
    local, remote = _exchange_copies(kind, *refs)
    for cp in remote:
        cp.wait_recv()
    for cp in remote:
        cp.wait_send()
    local.wait()


_EXCHANGE_SEMS = [pltpu.SemaphoreType.DMA((N_DEV - 1,)), pltpu.SemaphoreType.DMA((N_DEV - 1,)), pltpu.SemaphoreType.DMA]


def _exchange_shape(kind, x):
    return jax.ShapeDtypeStruct(((N_DEV,) + x.shape) if kind == "gather" else x.shape, x.dtype)


def _exchange(kind, xs, name):
    n = len(xs)

    def body(*refs):
        x_refs, o_refs, sems = refs[:n], refs[n:2 * n], refs[2 * n:]
        for e in range(n):
            _exchange_start(kind, x_refs[e], o_refs[e], *sems[3 * e:3 * e + 3])
        for e in range(n):
            _exchange_wait(kind, x_refs[e], o_refs[e], *sems[3 * e:3 * e + 3])

    any_spec = pl.BlockSpec(memory_space=pl.ANY)
    return tuple(pl.pallas_call(
        body, name=name, out_shape=tuple(_exchange_shape(kind, x) for x in xs),
        in_specs=[any_spec] * n, out_specs=(any_spec,) * n, scratch_shapes=list(_EXCHANGE_SEMS) * n,
    )(*xs))


def _all_gather(x, name):
    return _exchange("gather", [x], name)[0]


def _call(body, name, grid, in_specs, out_specs, out_shape, args, scratch=(), side=None):
    single = not isinstance(out_shape, (tuple, list))
    if single:
        out_shape, out_specs = (out_shape,), (out_specs,)
    if side is None:
        res = pl.pallas_call(body, name=name, grid=grid, in_specs=list(in_specs), out_specs=tuple(out_specs),
                             out_shape=tuple(out_shape), scratch_shapes=list(scratch), compiler_params=_cparams())(*args)
        return res[0] if single else tuple(res)
    kind, xs = side
    n_in, n_out, n_scr, n_x = len(in_specs), len(out_shape), len(scratch), len(xs)

    def wrapped(*refs):
        ins, x_refs = refs[:n_in], refs[n_in:n_in + n_x]
        o0 = n_in + n_x
        outs, o_refs = refs[o0:o0 + n_out], refs[o0 + n_out:o0 + n_out + n_x]
        s0 = o0 + n_out + n_x
        scr, sems = refs[s0:s0 + n_scr], refs[s0 + n_scr:]
        ids = [pl.program_id(a) for a in range(len(grid))]
        first = functools.reduce(jnp.logical_and, [i == 0 for i in ids])
        last = functools.reduce(jnp.logical_and, [i == g - 1 for i, g in zip(ids, grid)])

        @pl.when(first)
        def _():
            for e in range(n_x):
                _exchange_start(kind, x_refs[e], o_refs[e], *sems[3 * e:3 * e + 3])

        body(*ins, *outs, *scr)

        @pl.when(last)
        def _():
            for e in range(n_x):
                _exchange_wait(kind, x_refs[e], o_refs[e], *sems[3 * e:3 * e + 3])

    any_spec = pl.BlockSpec(memory_space=pl.ANY)
    res = pl.pallas_call(
        wrapped, name=name, grid=grid, in_specs=list(in_specs) + [any_spec] * n_x,
        out_specs=tuple(out_specs) + (any_spec,) * n_x,
        out_shape=tuple(out_shape) + tuple(_exchange_shape(kind, x) for x in xs),
        scratch_shapes=list(scratch) + list(_EXCHANGE_SEMS) * n_x, compiler_params=_cparams())(*args, *xs)
    return tuple(res[:n_out]) + (tuple(res[n_out:]),)


def _mm(a, b, mode, name, out_dtype=F32, bias=None, side=None, a2=None, tm_cap=1024, tn_cap=1408, tk_cap=1408):
    w2 = 0 if a2 is None else a2.shape[1]
    assert a2 is None or (mode in ("nn", "tn") and a2.shape[0] == a.shape[0])
    if mode == "nn":
        (M, K), (K2, N) = a.shape, b.shape
        K += w2
    elif mode == "nt":
        (M, K), (N, K2) = a.shape, b.shape
    else:
        (K, M), (K2, N) = a.shape, b.shape
        M += w2
    assert K == K2, (a.shape, b.shape, mode)
    if mode == "tn":
        tm_cap, tk_cap = 256, 8192
    if mode == "nn":
        tk_cap = 2816
    tm = _tile(math.gcd(M, w2) if mode == "tn" else M, tm_cap, 128 if mode == "tn" else 16)
    tn = _tile(N, tn_cap, 128)
    tk = _tile(math.gcd(K, w2) if mode == "nn" else K, tk_cap, 128)
    nk = K // tk
    n1 = a.shape[1] // (tm if mode == "tn" else tk)
    ca, cb = {"nn": (1, 0), "nt": (1, 1), "tn": (0, 0)}[mode]
    has_bias = bias is not None

    def body(*refs):
        refs = list(refs)
        a_ref = refs.pop(0)
        a2_ref = refs.pop(0) if a2 is not None else None
        b_ref = refs.pop(0)
        bias_ref = refs.pop(0) if has_bias else None
        o_ref, acc_ref = refs
        k = pl.program_id(2)

        def finish(r):
            if has_bias:
                r = r + bias_ref[...]
            o_ref[...] = r.astype(out_dtype)

        def step(lhs_ref):
            part = _dot(lhs_ref[...].astype(BF16), b_ref[...].astype(BF16), ca, cb)
            if nk == 1:
                finish(part)
                return

            @pl.when(k == 0)
            def _():
                acc_ref[...] = part

            @pl.when(k > 0)
            def _():
                acc_ref[...] += part

        if a2 is None:
            step(a_ref)
        else:
            col = pl.program_id(0) if mode == "tn" else k
            pl.when(col < n1)(lambda: step(a_ref))
            pl.when(col >= n1)(lambda: step(a2_ref))

        if nk > 1:
            pl.when(k == nk - 1)(lambda: finish(acc_ref[...]))

    if mode == "tn":
        a_specs = [pl.BlockSpec((tk, tm), lambda i, j, k: (k, jnp.minimum(i, n1 - 1)))]
        if a2 is not None:
            a_specs.append(pl.BlockSpec((tk, tm), lambda i, j, k: (k, jnp.maximum(i - n1, 0))))
    else:
        a_specs = [pl.BlockSpec((tm, tk), lambda i, j, k: (i, jnp.minimum(k, n1 - 1)))]
        if a2 is not None:
            a_specs.append(pl.BlockSpec((tm, tk), lambda i, j, k: (i, jnp.maximum(k - n1, 0))))
    if mode == "nt":
        b_spec = pl.BlockSpec((tn, tk), lambda i, j, k: (j, k))
    else:
        b_spec = pl.BlockSpec((tk, tn), lambda i, j, k: (k, j))
    in_specs = a_specs + [b_spec]
    args = [a] + ([a2] if a2 is not None else []) + [b]
    if has_bias:
        in_specs.append(pl.BlockSpec((1, tn), lambda i, j, k: (0, j)))
        args.append(bias)
    return _call(body, name, (M // tm, N // tn, nk), in_specs, pl.BlockSpec((tm, tn), lambda i, j, k: (i, j)),
                 jax.ShapeDtypeStruct((M, N), out_dtype), args,
                 scratch=[pltpu.VMEM((tm, tn) if nk > 1 else (8, 128), F32)], side=side)


ROW_TILE = 512


def _vec_spec(n):
    return pl.BlockSpec((1, n), lambda t: (0, 0))


def _norm_mod_fwd(x, g, sc, sh, name, y=None, gate=None):
    T, D = x.shape
    tr = _tile(T, ROW_TILE, 16)
    has_y = y is not None

    def body(*refs):
        if has_y:
            x_ref, y_ref, gate_ref, g_ref, sc_ref, sh_ref, xn_ref, h_ref = refs
            xv = x_ref[...] + gate_ref[...] * y_ref[...].astype(F32)
            xn_ref[...] = xv
        else:
            x_ref, g_ref, sc_ref, sh_ref, h_ref = refs
            xv = x_ref[...]
        r = lax.rsqrt(jnp.mean(xv * xv, axis=-1, keepdims=True) + NORM_EPS)
        h = (xv * r * g_ref[...]) * (1.0 + sc_ref[...]) + sh_ref[...]
        h_ref[...] = h.astype(BF16)

    row = pl.BlockSpec((tr, D), lambda t: (t, 0))
    vec = _vec_spec(D)
    if has_y:
        in_specs, args = [row, row, vec, vec, vec, vec], [x, y, gate, g, sc, sh]
        out_shape = (jax.ShapeDtypeStruct((T, D), F32), jax.ShapeDtypeStruct((T, D), BF16))
        out_specs = (row, row)
    else:
        in_specs, args = [row, vec, vec, vec], [x, g, sc, sh]
        out_shape = jax.ShapeDtypeStruct((T, D), BF16)
        out_specs = row
    return _call(body, name, (T // tr,), in_specs, out_specs, out_shape, args)


def _norm_mod_bwd(dh, x, g, sc, sh, dxn, name, y_up=None, gate_up=None):
    T, D = x.shape
    tr = _tile(T, ROW_TILE, 16)
    fused = y_up is not None

    def body(*refs):
        if fused:
            (dh_ref, x_ref, g_ref, sc_ref, sh_ref, dxn_ref, y_ref, gate_ref,
             dx_ref, dg_ref, dsc_ref, dsh_ref, dy_ref, dgate_ref, dbias_ref) = refs
            sums = (dg_ref, dsc_ref, dsh_ref, dgate_ref, dbias_ref)
        else:
            dh_ref, x_ref, g_ref, sc_ref, sh_ref, dxn_ref, dx_ref, dg_ref, dsc_ref, dsh_ref = refs
            sums = (dg_ref, dsc_ref, dsh_ref)

        @pl.when(pl.program_id(0) == 0)
        def _():
            for r_ in sums:
                r_[...] = jnp.zeros_like(r_)
        dh = dh_ref[...].astype(F32)
        xv = x_ref[...]
        gv = g_ref[...]
        r = lax.rsqrt(jnp.mean(xv * xv, axis=-1, keepdims=True) + NORM_EPS)
        yv = xv * r
        dsh_ref[...] += jnp.sum(dh, axis=0, keepdims=True)
        dsc_ref[...] += jnp.sum(dh * (yv * gv), axis=0, keepdims=True)
        dn = dh * (1.0 + sc_ref[...])
        dg_ref[...] += jnp.sum(dn * yv, axis=0, keepdims=True)
        dy = dn * gv
        dx = dxn_ref[...] + r * (dy - yv * jnp.mean(dy * yv, axis=-1, keepdims=True))
        dx_ref[...] = dx
        if fused:
            dyu = dx * gate_ref[...]
            dy_ref[...] = dyu.astype(BF16)
            dgate_ref[...] += jnp.sum(dx * y_ref[...].astype(F32), axis=0, keepdims=True)
            dbias_ref[...] += jnp.sum(dyu, axis=0, keepdims=True)

    row = pl.BlockSpec((tr, D), lambda t: (t, 0))
    vec = _vec_spec(D)
    vshape = jax.ShapeDtypeStruct((1, D), F32)
    in_specs, args = [row, row, vec, vec, vec, row], [dh, x, g, sc, sh, dxn]
    out_specs, out_shape = [row, vec, vec, vec], [jax.ShapeDtypeStruct((T, D), F32), vshape, vshape, vshape]
    if fused:
        in_specs, args = in_specs + [row, vec], args + [y_up, gate_up]
        out_specs, out_shape = out_specs + [row, vec, vec], out_shape + [jax.ShapeDtypeStruct((T, D), BF16), vshape, vshape]
    return _call(body, name, (T // tr,), in_specs, tuple(out_specs), tuple(out_shape), args)


def _final_loss(x, y, gate, tgt, g, name):
    T, D = x.shape
    tr = _tile(T, ROW_TILE, 16)

    def body(x_ref, y_ref, gate_ref, tgt_ref, g_ref, dx_ref, dg_ref, loss_ref, dyu_ref, dgate_ref):
        @pl.when(pl.program_id(0) == 0)
        def _():
            for r_ in (dg_ref, loss_ref, dgate_ref):
                r_[...] = jnp.zeros_like(r_)
        yu = y_ref[...].astype(F32)
        gate = gate_ref[...]
        xv = x_ref[...] + gate * yu
        gv = g_ref[...]
        r = lax.rsqrt(jnp.mean(xv * xv, axis=-1, keepdims=True) + NORM_EPS)
        yv = xv * r
        e = yv * gv - tgt_ref[...]
        per_row = jnp.mean(e * e, axis=-1, keepdims=True)
        loss_ref[...] += 0.5 * jnp.sum(per_row, axis=0, keepdims=True)
        dout = e * (1.0 / D)
        dg_ref[...] += jnp.sum(dout * yv, axis=0, keepdims=True)
        dy = dout * gv
        dx = r * (dy - yv * jnp.mean(dy * yv, axis=-1, keepdims=True))
        dx_ref[...] = dx
        dyu_ref[...] = (dx * gate).astype(BF16)
        dgate_ref[...] += jnp.sum(dx * yu, axis=0, keepdims=True)

    row = pl.BlockSpec((tr, D), lambda t: (t, 0))
    vec = _vec_spec(D)
    return _call(body, name, (T // tr,), [row, row, vec, row, vec], (row, vec, _vec_spec(128), row, vec),
                 (jax.ShapeDtypeStruct((T, D), F32), jax.ShapeDtypeStruct((1, D), F32),
                  jax.ShapeDtypeStruct((1, 128), F32), jax.ShapeDtypeStruct((T, D), BF16),
                  jax.ShapeDtypeStruct((1, D), F32)), (x, y, gate, tgt, g))


FFN_HALO = 16
FFN_PAD = 8
FFN_CHUNK = 64
FFN_ROWS_FWD = 8192
FFN_ROWS_BWD = 4096


def _lane_groups(cw):
    lw = 128 if cw % 128 == 0 else cw
    return lw, [pl.ds(g * lw, lw) for g in range(cw // lw)]


def _ffn_mid_fwd(z, dw, b, name, side=None):
    T, F2 = z.shape
    F = F2 // 2
    K = dw.shape[0]
    H, P = FFN_HALO, FFN_PAD
    tr = _tile(T, FFN_ROWS_FWD, H)
    cw = _tile(F, 512, 128)
    nc = F // cw
    rc = _tile(tr, FFN_CHUNK, 16)
    lw, groups = _lane_groups(cw)

    def body(gc_ref, gh_ref, uc_ref, uh_ref, wg_ref, wu_ref, bg_ref, bu_ref, a_ref, xg_s, xu_s):
        first = pl.program_id(1) == 0

        def fill(s, h_ref, c_ref):
            for g, ls in enumerate(groups):
                s[g, 0:P, :] = jnp.where(first, 0.0, h_ref[:, ls].astype(F32)[H - P:H])
                s[g, P:P + tr, :] = c_ref[:, ls].astype(F32)

        fill(xg_s, gh_ref, gc_ref)
        fill(xu_s, uh_ref, uc_ref)

        def chunk(ci, carry):
            r0 = pl.multiple_of(ci * rc, rc)
            for g, ls in enumerate(groups):
                def conv(s, w_ref, b_ref):
                    acc = b_ref[:, ls] + w_ref[K - 1:K, ls] * s[g, pl.ds(r0 + P, rc), :]
                    for k in range(K - 1):
                        acc = acc + w_ref[k:k + 1, ls] * s[g, pl.ds(r0 + P - (K - 1 - k), rc), :]
                    return acc

                gv = conv(xg_s, wg_ref, bg_ref)
                uv = conv(xu_s, wu_ref, bu_ref)
                a_ref[pl.ds(r0, rc), ls] = (gv * _sigmoid(gv) * uv).astype(BF16)
            return carry

        lax.fori_loop(0, tr // rc, chunk, 0, unroll=2)

    rb = tr // H
    in_specs = [
        pl.BlockSpec((tr, cw), lambda j, t: (t, j)),
        pl.BlockSpec((H, cw), lambda j, t: (jnp.maximum(t * rb - 1, 0), j)),
        pl.BlockSpec((tr, cw), lambda j, t: (t, j + nc)),
        pl.BlockSpec((H, cw), lambda j, t: (jnp.maximum(t * rb - 1, 0), j + nc)),
        pl.BlockSpec((K, cw), lambda j, t: (0, j)),
        pl.BlockSpec((K, cw), lambda j, t: (0, j + nc)),
        pl.BlockSpec((1, cw), lambda j, t: (0, j)),
        pl.BlockSpec((1, cw), lambda j, t: (0, j + nc)),
    ]
    return _call(body, name, (nc, T // tr), in_specs, pl.BlockSpec((tr, cw), lambda j, t: (t, j)),
                 jax.ShapeDtypeStruct((T, F), BF16), (z, z, z, z, dw, dw, b, b),
                 scratch=[pltpu.VMEM((len(groups), P + tr, lw), F32)] * 2, side=side)


def _ffn_mid_bwd(z, da, dw, b, name, side=None):
    T, F2 = z.shape
    F = F2 // 2
    K = dw.shape[0]
    H, P = FFN_HALO, FFN_PAD
    tr = _tile(T, FFN_ROWS_BWD, H)
    cw = _tile(F, 512, 128)
    nc = F // cw
    nt = T // tr
    rb = tr // H
    rc = _tile(tr, FFN_CHUNK, 16)
    ext = rc + P
    lw, groups = _lane_groups(cw)

    def body(gp_ref, gc_ref, gn_ref, up_ref, uc_ref, un_ref, dac_ref, dan_ref, wg_ref, wu_ref, bg_ref, bu_ref,
             dzg_ref, dzu_ref, dwg_ref, dwu_ref, dbg_ref, dbu_ref, xg_s, xu_s, da_s, dg_s, du_s, acc_s):
        t = pl.program_id(1)
        first = t == 0
        last = t == nt - 1

        @pl.when(first)
        def _():
            for r in (dwg_ref, dwu_ref, dbg_ref, dbu_ref):
                r[...] = jnp.zeros_like(r)

        def fill(s, p_ref, c_ref, n_ref):
            for g, ls in enumerate(groups):
                s[g, 0:P, :] = jnp.where(first, 0.0, p_ref[:, ls].astype(F32)[H - P:H])
                s[g, P:P + tr, :] = c_ref[:, ls].astype(F32)
                s[g, P + tr:2 * P + tr, :] = jnp.where(last, 0.0, n_ref[:, ls].astype(F32)[0:P])

        fill(xg_s, gp_ref, gc_ref, gn_ref)
        fill(xu_s, up_ref, uc_ref, un_ref)
        for g, ls in enumerate(groups):
            da_s[g, 0:tr, :] = dac_ref[:, ls].astype(F32)
            da_s[g, tr:tr + P, :] = jnp.where(last, 0.0, dan_ref[:, ls].astype(F32)[0:P])
        acc_s[...] = jnp.zeros_like(acc_s)

        def fold(v):
            return jnp.sum(v.reshape(rc // 8, 8, lw), axis=0)

        def chunk(ci, carry):
            r0 = pl.multiple_of(ci * rc, rc)
            for g, ls in enumerate(groups):
                def conv_ext(s, w_ref, b_ref):
                    acc = b_ref[:, ls] + w_ref[K - 1:K, ls] * s[g, pl.ds(r0 + P, ext), :]
                    for k in range(K - 1):
                        acc = acc + w_ref[k:k + 1, ls] * s[g, pl.ds(r0 + P - (K - 1 - k), ext), :]
                    return acc

                gv = conv_ext(xg_s, wg_ref, bg_ref)
                uv = conv_ext(xu_s, wu_ref, bu_ref)
                dav = da_s[g, pl.ds(r0, ext), :]
                sg = _sigmoid(gv)
                dg = dav * uv * (sg * (1.0 + gv * (1.0 - sg)))
                du = dav * (gv * sg)
                dg_s[g] = dg
                du_s[g] = du
                for idx, (d_s, dval, x_s, w_ref, dz_ref) in enumerate(
                        ((dg_s, dg, xg_s, wg_ref, dzg_ref), (du_s, du, xu_s, wu_ref, dzu_ref))):
                    cur = dval[:rc]
                    acc_s[idx, K, g] += fold(cur)
                    for k in range(K):
                        acc_s[idx, k, g] += fold(cur * x_s[g, pl.ds(r0 + P - (K - 1 - k), rc), :])
                    dz = w_ref[K - 1:K, ls] * cur
                    for k in range(K - 1):
                        dz = dz + w_ref[k:k + 1, ls] * d_s[g, pl.ds(K - 1 - k, rc), :]
                    dz_ref[pl.ds(r0, rc), ls] = dz.astype(BF16)
            return carry

        lax.fori_loop(0, tr // rc, chunk, 0, unroll=2)
        for idx, (dw_ref, db_ref) in enumerate(((dwg_ref, dbg_ref), (dwu_ref, dbu_ref))):
            for g, ls in enumerate(groups):
                db_ref[:, ls] += jnp.sum(acc_s[idx, K, g], axis=0, keepdims=True)
                dw_ref[:, ls] += jnp.concatenate(
                    [jnp.sum(acc_s[idx, k, g], axis=0, keepdims=True) for k in range(K)], axis=0)

    def prev(t):
        return jnp.maximum(t * rb - 1, 0)

    def nxt(t):
        return jnp.minimum((t + 1) * rb, T // H - 1)

    def zspecs(off):
        return [pl.BlockSpec((H, cw), lambda j, t: (prev(t), j + off)),
                pl.BlockSpec((tr, cw), lambda j, t: (t, j + off)),
                pl.BlockSpec((H, cw), lambda j, t: (nxt(t), j + off))]

    in_specs = zspecs(0) + zspecs(nc) + [
        pl.BlockSpec((tr, cw), lambda j, t: (t, j)),
        pl.BlockSpec((H, cw), lambda j, t: (nxt(t), j)),
        pl.BlockSpec((K, cw), lambda j, t: (0, j)),
        pl.BlockSpec((K, cw), lambda j, t: (0, j + nc)),
        pl.BlockSpec((1, cw), lambda j, t: (0, j)),
        pl.BlockSpec((1, cw), lambda j, t: (0, j + nc)),
    ]
    out_shape = (jax.ShapeDtypeStruct((T, F), BF16), jax.ShapeDtypeStruct((T, F), BF16),
                 jax.ShapeDtypeStruct((K, F), F32), jax.ShapeDtypeStruct((K, F), F32),
                 jax.ShapeDtypeStruct((1, F), F32), jax.ShapeDtypeStruct((1, F), F32))
    out_specs = (pl.BlockSpec((tr, cw), lambda j, t: (t, j)), pl.BlockSpec((tr, cw), lambda j, t: (t, j)),
                 pl.BlockSpec((K, cw), lambda j, t: (0, j)), pl.BlockSpec((K, cw), lambda j, t: (0, j)),
                 pl.BlockSpec((1, cw), lambda j, t: (0, j)), pl.BlockSpec((1, cw), lambda j, t: (0, j)))
    ng = len(groups)
    scratch = [pltpu.VMEM((ng, 2 * P + tr, lw), F32), pltpu.VMEM((ng, 2 * P + tr, lw), F32),
               pltpu.VMEM((ng, P + tr, lw), F32), pltpu.VMEM((ng, ext, lw), F32), pltpu.VMEM((ng, ext, lw), F32),
               pltpu.VMEM((2, K + 1, ng, 8, lw), F32)]
    res = _call(body, name, (nc, nt), in_specs, out_specs, out_shape, (z, z, z, z, z, z, da, da, dw, dw, b, b),
                scratch=scratch, side=side)
    dzg, dzu, dwg, dwu, dbg, dbu = res[:6]
    return (dzg, dzu, jnp.concatenate([dwg, dwu], axis=1), jnp.concatenate([dbg, dbu], axis=1)) + tuple(res[6:])


CONV_HALO = 32
CONV_ROWS = 512
CONV_CHUNK = 64


def _glu_window(s, p_ref, halo_ref, first, D, groups, lw, H, tr):
    for g, ls in enumerate(groups):
        gs = pl.ds(D + g * lw, lw)
        s[g, 0:H, :] = jnp.where(first, 0.0, halo_ref[:, ls].astype(F32) * _sigmoid(halo_ref[:, gs].astype(F32)))
        s[g, H:H + tr, :] = p_ref[:, ls].astype(F32) * _sigmoid(p_ref[:, gs].astype(F32))


def _convmix_fwd(p, dw, dwb, lng, lnb, name, side=None):
    T, D2 = p.shape
    D = D2 // 2
    K = dw.shape[0]
    H = CONV_HALO
    tr = _tile(T, CONV_ROWS, H)
    rb = tr // H
    rc = _tile(tr, CONV_CHUNK, 16)
    lw, groups = _lane_groups(D)

    def body(pc_ref, ph_ref, w_ref, wb_ref, lng_ref, lnb_ref, zc_ref, s_ref, zg_s):
        first = pl.program_id(0) == 0
        _glu_window(zg_s, pc_ref, ph_ref, first, D, groups, lw, H, tr)

        def chunk(ci, carry):
            r0 = pl.multiple_of(ci * rc, rc)
            for g, ls in enumerate(groups):
                acc = wb_ref[:, ls] + w_ref[K - 1:K, ls] * zg_s[g, pl.ds(pl.multiple_of(r0 + H, 8), rc), :]
                for k in range(K - 1):
                    acc = acc + w_ref[k:k + 1, ls] * zg_s[g, pl.ds(r0 + H - (K - 1 - k), rc), :]
                zc_ref[pl.ds(r0, rc), ls] = acc
            return carry

        lax.fori_loop(0, tr // rc, chunk, 0)
        acc = zc_ref[...]
        mu = jnp.mean(acc, axis=-1, keepdims=True)
        xc = acc - mu
        rstd = lax.rsqrt(jnp.mean(xc * xc, axis=-1, keepdims=True) + NORM_EPS)
        ln = xc * rstd * lng_ref[...] + lnb_ref[...]
        s_ref[...] = (ln * _sigmoid(ln)).astype(BF16)

    in_specs = [pl.BlockSpec((tr, D2), lambda t: (t, 0)),
                pl.BlockSpec((H, D2), lambda t: (jnp.maximum(t * rb - 1, 0), 0)),
                pl.BlockSpec((K, D), lambda t: (0, 0)), _vec_spec(D), _vec_spec(D), _vec_spec(D)]
    row = pl.BlockSpec((tr, D), lambda t: (t, 0))
    return _call(body, name, (T // tr,), in_specs, (row, row),
                 (jax.ShapeDtypeStruct((T, D), F32), jax.ShapeDtypeStruct((T, D), BF16)), (p, p, dw, dwb, lng, lnb),
                 scratch=[pltpu.VMEM((len(groups), H + tr, lw), F32)], side=side)


def _convmix_bwd(ds, zc, p, dw, lng, lnb, name, side=None):
    T, D = zc.shape
    D2 = 2 * D
    K = dw.shape[0]
    H = CONV_HALO
    tr = _tile(T, CONV_ROWS, H)
    rb = tr // H
    nt = T // tr
    ext = tr + H
    rc = _tile(tr, CONV_CHUNK, 16)
    lw, groups = _lane_groups(D)

    def body(dsc_ref, dsn_ref, zcc_ref, zcn_ref, pp_ref, pc_ref, w_ref, lng_ref, lnb_ref,
             dp_ref, dbin_ref, ddw_ref, ddwb_ref, dlng_ref, dlnb_ref, zg_s, dzc_s, acc_s):
        t = pl.program_id(0)
        first = t == 0
        last = t == nt - 1

        @pl.when(first)
        def _():
            for r in (dbin_ref, ddw_ref, ddwb_ref, dlng_ref, dlnb_ref):
                r[...] = jnp.zeros_like(r)

        dsv = jnp.concatenate([dsc_ref[...].astype(F32), jnp.where(last, 0.0, dsn_ref[...].astype(F32))], axis=0)
        zcv = jnp.concatenate([zcc_ref[...], zcn_ref[...]], axis=0)
        lg = lng_ref[...]
        mu = jnp.mean(zcv, axis=-1, keepdims=True)
        xc = zcv - mu
        rstd = lax.rsqrt(jnp.mean(xc * xc, axis=-1, keepdims=True) + NORM_EPS)
        xh = xc * rstd
        ln = xh * lg + lnb_ref[...]
        sg = _sigmoid(ln)
        dln = dsv * (sg * (1.0 + ln * (1.0 - sg)))
        dlnc = _rows(dln, 0, tr)
        dlnb_ref[...] += jnp.sum(dlnc, axis=0, keepdims=True)
        dlng_ref[...] += jnp.sum(dlnc * _rows(xh, 0, tr), axis=0, keepdims=True)
        dxh = dln * lg
        dzc = rstd * (dxh - jnp.mean(dxh, axis=-1, keepdims=True) - xh * jnp.mean(dxh * xh, axis=-1, keepdims=True))
        dzcc = _rows(dzc, 0, tr)
        ddwb_ref[...] += jnp.sum(dzcc, axis=0, keepdims=True)

        for g in range(len(groups)):
            dzc_s[g] = dzc[:, g * lw:(g + 1) * lw]
        _glu_window(zg_s, pc_ref, pp_ref, first, D, groups, lw, H, tr)
        acc_s[...] = jnp.zeros_like(acc_s)

        def fold(v):
            return jnp.sum(v.reshape(rc // 8, 8, lw), axis=0)

        def chunk(ci, carry):
            r0 = pl.multiple_of(ci * rc, rc)
            for g, ls in enumerate(groups):
                gs = pl.ds(D + g * lw, lw)
                cur = dzc_s[g, pl.ds(r0, rc), :]
                dzg = w_ref[K - 1:K, ls] * cur
                for k in range(K):
                    acc_s[k, g] += fold(cur * zg_s[g, pl.ds(r0 + H - (K - 1 - k), rc), :])
                    if k < K - 1:
                        dzg = dzg + w_ref[k:k + 1, ls] * dzc_s[g, pl.ds(r0 + (K - 1 - k), rc), :]
                ac = pc_ref[pl.ds(r0, rc), ls].astype(F32)
                sgg = _sigmoid(pc_ref[pl.ds(r0, rc), gs].astype(F32))
                dpa = dzg * sgg
                dpg = dzg * ac * sgg * (1.0 - sgg)
                dp_ref[pl.ds(r0, rc), ls] = dpa.astype(BF16)
                dp_ref[pl.ds(r0, rc), gs] = dpg.astype(BF16)
                acc_s[K, g] += fold(dpa)
                acc_s[K + 1, g] += fold(dpg)
            return carry

        lax.fori_loop(0, tr // rc, chunk, 0)
        for g, ls in enumerate(groups):
            ddw_ref[:, ls] += jnp.concatenate([jnp.sum(acc_s[k, g], axis=0, keepdims=True) for k in range(K)], axis=0)
            dbin_ref[:, ls] += jnp.sum(acc_s[K, g], axis=0, keepdims=True)
            dbin_ref[:, pl.ds(D + g * lw, lw)] += jnp.sum(acc_s[K + 1, g], axis=0, keepdims=True)

    def prev(t):
        return jnp.maximum(t * rb - 1, 0)

    def nxt(t):
        return jnp.minimum((t + 1) * rb, T // H - 1)

    in_specs = [pl.BlockSpec((tr, D), lambda t: (t, 0)), pl.BlockSpec((H, D), lambda t: (nxt(t), 0)),
                pl.BlockSpec((tr, D), lambda t: (t, 0)), pl.BlockSpec((H, D), lambda t: (nxt(t), 0)),
                pl.BlockSpec((H, D2), lambda t: (prev(t), 0)), pl.BlockSpec((tr, D2), lambda t: (t, 0)),
                pl.BlockSpec((K, D), lambda t: (0, 0)), _vec_spec(D), _vec_spec(D)]
    out_shape = (jax.ShapeDtypeStruct((T, D2), BF16), jax.ShapeDtypeStruct((1, D2), F32),
                 jax.ShapeDtypeStruct((K, D), F32), jax.ShapeDtypeStruct((1, D), F32),
                 jax.ShapeDtypeStruct((1, D), F32), jax.ShapeDtypeStruct((1, D), F32))
    out_specs = (pl.BlockSpec((tr, D2), lambda t: (t, 0)), _vec_spec(D2), pl.BlockSpec((K, D), lambda t: (0, 0)),
                 _vec_spec(D), _vec_spec(D), _vec_spec(D))
    ng = len(groups)
    scratch = [pltpu.VMEM((ng, H + tr, lw), F32), pltpu.VMEM((ng, ext, lw), F32), pltpu.VMEM((K + 2, ng, 8, lw), F32)]
    return _call(body, name, (nt,), in_specs, out_specs, out_shape, (ds, ds, zc, zc, p, p, dw, lng, lnb),
                 scratch=scratch, side=side)


_INV_SQRT2 = 1.0 / math.sqrt(2.0)
_INV_SQRT2PI = 1.0 / math.sqrt(2.0 * math.pi)


def _gelu(x):
    return 0.5 * x * (1.0 + lax.erf(x * _INV_SQRT2))


def _gelu_grad(x):
    return 0.5 * (1.0 + lax.erf(x * _INV_SQRT2)) + x * jnp.exp(-0.5 * x * x) * _INV_SQRT2PI


def _tril_mask(n, transposed=False):
    r = lax.broadcasted_iota(jnp.int32, (n, n), 0)
    c = lax.broadcasted_iota(jnp.int32, (n, n), 1)
    return (r <= c) if transposed else (r >= c)


def _sgu_fwd(p, lng, lnb, ws, bs_t, name):
    T, S2 = p.shape
    S = S2 // 2
    G, C, _ = ws.shape
    gd = S // G
    cpt = 2 if (T // C) % 2 == 0 else 1
    tr = C * cpt

    def body(p_ref, lng_ref, lnb_ref, ws_ref, bst_ref, m_ref):
        mask = _tril_mask(C)
        lg, lb = lng_ref[...], lnb_ref[...]
        bst = bst_ref[...]
        for ci in range(cpt):
            pc = p_ref[ci * C:(ci + 1) * C, :].astype(F32)
            z = _gelu(pc)
            u, v = z[:, :S], z[:, S:]
            mu = jnp.mean(v, axis=-1, keepdims=True)
            xc = v - mu
            rstd = lax.rsqrt(jnp.mean(xc * xc, axis=-1, keepdims=True) + NORM_EPS)
            vn = (xc * rstd * lg + lb).astype(BF16)
            outs = []
            for g in range(G):
                wm = jnp.where(mask, ws_ref[g], 0.0).astype(BF16)
                vs = _dot(wm, vn[:, g * gd:(g + 1) * gd], 1, 0) + bst[:, g:g + 1]
                outs.append(u[:, g * gd:(g + 1) * gd] * vs)
            m_ref[ci * C:(ci + 1) * C, :] = jnp.concatenate(outs, axis=1).astype(BF16)

    in_specs = [pl.BlockSpec((tr, S2), lambda t: (t, 0)), _vec_spec(S), _vec_spec(S),
                pl.BlockSpec((G, C, C), lambda t: (0, 0, 0)), pl.BlockSpec((C, G), lambda t: (0, 0))]
    return _call(body, name, (T // tr,), in_specs, pl.BlockSpec((tr, S), lambda t: (t, 0)),
                 jax.ShapeDtypeStruct((T, S), BF16), (p, lng, lnb, ws, bs_t))


def _sgu_bwd(p, dm, lng, lnb, ws, bs_t, name, side=None):
    T, S2 = p.shape
    S = S2 // 2
    G, C, _ = ws.shape
    gd = S // G
    nt = T // C

    def body(p_ref, dm_ref, lng_ref, lnb_ref, ws_ref, bst_ref, dp_ref, dbin_ref, dlng_ref, dlnb_ref, dws_ref, dbst_ref):
        @pl.when(pl.program_id(0) == 0)
        def _():
            for r in (dbin_ref, dlng_ref, dlnb_ref, dws_ref, dbst_ref):
                r[...] = jnp.zeros_like(r)

        mask = _tril_mask(C)
        lg, lb = lng_ref[...], lnb_ref[...]
        bst = bst_ref[...]
        pc = p_ref[...].astype(F32)
        dmv = dm_ref[...].astype(F32)
        z = _gelu(pc)
        u, v = z[:, :S], z[:, S:]
        mu = jnp.mean(v, axis=-1, keepdims=True)
        xc = v - mu
        rstd = lax.rsqrt(jnp.mean(xc * xc, axis=-1, keepdims=True) + NORM_EPS)
        vh = xc * rstd
        vn = (vh * lg + lb).astype(BF16)
        dus, dvns, dbcols = [], [], []
        for g in range(G):
            sl = slice(g * gd, (g + 1) * gd)
            wm = jnp.where(mask, ws_ref[g], 0.0).astype(BF16)
            vs = _dot(wm, vn[:, sl], 1, 0) + bst[:, g:g + 1]
            dmg = dmv[:, sl]
            dus.append(dmg * vs)
            dvs = dmg * u[:, sl]
            dbcols.append(jnp.sum(dvs, axis=-1, keepdims=True))
            dvsb = dvs.astype(BF16)
            dws_ref[g] += jnp.where(mask, _dot(dvsb, vn[:, sl], 1, 1), 0.0)
            dvns.append(_dot(wm, dvsb, 0, 0))
        dbst_ref[...] += jnp.concatenate(dbcols, axis=1)
        dvn = jnp.concatenate(dvns, axis=1)
        dlnb_ref[...] += jnp.sum(dvn, axis=0, keepdims=True)
        dlng_ref[...] += jnp.sum(dvn * vh, axis=0, keepdims=True)
        dvh = dvn * lg
        dv = rstd * (dvh - jnp.mean(dvh, axis=-1, keepdims=True) - vh * jnp.mean(dvh * vh, axis=-1, keepdims=True))
        dz = jnp.concatenate([jnp.concatenate(dus, axis=1), dv], axis=1)
        dpv = dz * _gelu_grad(pc)
        dp_ref[...] = dpv.astype(BF16)
        dbin_ref[...] += jnp.sum(dpv, axis=0, keepdims=True)

    in_specs = [pl.BlockSpec((C, S2), lambda t: (t, 0)), pl.BlockSpec((C, S), lambda t: (t, 0)),
                _vec_spec(S), _vec_spec(S), pl.BlockSpec((G, C, C), lambda t: (0, 0, 0)),
                pl.BlockSpec((C, G), lambda t: (0, 0))]
    out_shape = (jax.ShapeDtypeStruct((T, S2), BF16), jax.ShapeDtypeStruct((1, S2), F32),
                 jax.ShapeDtypeStruct((1, S), F32), jax.ShapeDtypeStruct((1, S), F32),
                 jax.ShapeDtypeStruct((G, C, C), F32), jax.ShapeDtypeStruct((C, G), F32))
    out_specs = (pl.BlockSpec((C, S2), lambda t: (t, 0)), _vec_spec(S2), _vec_spec(S), _vec_spec(S),
                 pl.BlockSpec((G, C, C), lambda t: (0, 0, 0)), pl.BlockSpec((C, G), lambda t: (0, 0)))
    return _call(body, name, (nt,), in_specs, out_specs, out_shape, (p, dm, lng, lnb, ws, bs_t), side=side)


def _alibi_slope(h, n_heads):
    return 2.0 ** (-8.0 * (h + 1) / n_heads)


def _attn_bias(n_heads):
    B = BLOCK
    qi = jnp.arange(B)[:, None]
    kj = jnp.arange(2 * B)[None, :]
    dist = qi + B - kj
    band = (dist >= 0) & (dist < B)
    slopes = jnp.array([_alibi_slope(h, n_heads) for h in range(n_heads)], F32)
    out = []
    for valid in (band & (kj >= B), band):
        b = jnp.where(valid[None], -slopes[:, None, None] * dist.astype(F32)[None], NEG_INF)
        b = jnp.where(kj[None] == 0, 0.0, b)
        out.append(b.reshape(n_heads // 2, 2, B, 2 * B).transpose(0, 2, 1, 3).reshape(n_heads // 2, B, 4 * B))
    return jnp.stack(out)


def _sink_rows(sinks):
    return jnp.pad(sinks.astype(F32)[:, None], ((0, 0), (0, BLOCK - 1)))


def _block_diag2(x):
    z = jnp.zeros_like(x)
    return jnp.concatenate([jnp.concatenate([x, z], axis=1), jnp.concatenate([z, x], axis=1)], axis=0)


def _attn_core(q_ref, kp_ref, kc_ref, vp_ref, vc_ref, sink_ref, bias_ref, n_heads):
    B, hd = BLOCK, HEAD_DIM
    ppg = n_heads // N_KV // 2
    pairs = range(n_heads // 2)
    row0 = lax.broadcasted_iota(jnp.int32, (2 * B, N_KV * hd), 0) == 0
    k2 = jnp.concatenate([kp_ref[...], kc_ref[...]], axis=0)
    v2 = jnp.concatenate([vp_ref[...], vc_ref[...]], axis=0)
    k2 = jnp.where(row0, jnp.zeros_like(k2), k2)
    v2 = jnp.where(row0, jnp.zeros_like(v2), v2)
    scale = hd ** -0.5
    kks = [_block_diag2(k2[:, kv * hd:(kv + 1) * hd] * scale) for kv in range(N_KV)]
    vvs = [_block_diag2(v2[:, kv * hd:(kv + 1) * hd]) for kv in range(N_KV)]
    first_rows = lax.broadcasted_iota(jnp.int32, (4 * B, 4 * hd), 0) < 2 * B
    first_lanes = lax.broadcasted_iota(jnp.int32, (4 * B, 4 * hd), 1) < 2 * hd
    ones = (first_rows == first_lanes).astype(BF16)
    vxs = [jnp.concatenate([vv, ones], axis=1) for vv in vvs]
    ss = [_dot(q_ref[:, pr * 2 * hd:(pr + 1) * 2 * hd], kks[pr // ppg], 1, 1) + bias_ref[pr] for pr in pairs]
    halves = []
    for h in range(n_heads):
        s = ss[h // 2][:, (h % 2) * 2 * B:(h % 2 + 1) * 2 * B]
        halves.append(jnp.concatenate([s[:, :B] + sink_ref[h:h + 1, :], s[:, B:]], axis=1))
    mxs = [jnp.max(s, axis=-1, keepdims=True) for s in halves]
    es = [jnp.exp(s - m) for s, m in zip(halves, mxs)]
    rs = [_dot(jnp.concatenate([es[2 * pr], es[2 * pr + 1]], axis=1).astype(BF16), vxs[pr // ppg], 1, 0) for pr in pairs]
    return kks, vvs, es, rs


def _attn_fwd(qkv, sink_rows, bias, n_heads, name, side=None):
    T = qkv.shape[0]
    B = BLOCK
    hd = HEAD_DIM
    HQ = n_heads * hd
    KVW = N_KV * hd
    assert (n_heads // N_KV) % 2 == 0, "heads are processed in pairs that share a key/value head"
    nb = T // B
    kcol = HQ // KVW

    def body(q_ref, kp_ref, kc_ref, vp_ref, vc_ref, sink_ref, bias_ref, o_ref):
        _, _, _, rs = _attn_core(q_ref, kp_ref, kc_ref, vp_ref, vc_ref, sink_ref, bias_ref, n_heads)
        low = lax.broadcasted_iota(jnp.int32, (B, 2 * hd), 1) < hd
        for pr, r in enumerate(rs):
            inv = jnp.where(low, 1.0 / r[:, 2 * hd:4 * hd], 1.0 / r[:, 4 * hd:])
            o_ref[:, pr * 2 * hd:(pr + 1) * 2 * hd] = (r[:, :2 * hd] * inv).astype(BF16)

    def prev(n):
        return jnp.maximum(n - 1, 0)

    in_specs = [pl.BlockSpec((B, HQ), lambda n: (n, 0)),
                pl.BlockSpec((B, KVW), lambda n: (prev(n), kcol)), pl.BlockSpec((B, KVW), lambda n: (n, kcol)),
                pl.BlockSpec((B, KVW), lambda n: (prev(n), kcol + 1)), pl.BlockSpec((B, KVW), lambda n: (n, kcol + 1)),
                pl.BlockSpec((n_heads, B), lambda n: (0, 0)),
                pl.BlockSpec((None, n_heads // 2, B, 4 * B), lambda n: (jnp.minimum(n, 1), 0, 0, 0))]
    return _call(body, name, (nb,), in_specs, pl.BlockSpec((B, HQ), lambda n: (n, 0)),
                 jax.ShapeDtypeStruct((T, HQ), BF16), (qkv, qkv, qkv, qkv, qkv, sink_rows, bias), side=side)


def _attn_bwd(qkv, do, sink_rows, bias, n_heads, name, side=None):
    T = qkv.shape[0]
    B = BLOCK
    hd = HEAD_DIM
    HQ = n_heads * hd
    KVW = N_KV * hd
    group = n_heads // N_KV
    assert group % 2 == 0, "heads are processed in pairs that share a key/value head"
    ppg = group // 2
    nb = T // B
    kcol = HQ // KVW
    scale = hd ** -0.5

    def body(q_ref, kp_ref, kc_ref, vp_ref, vc_ref, do_ref, sink_ref, bias_ref,
             dq_ref, dkv_ref, dbq_ref, dbk_ref, dbv_ref, dsink_ref, ck_ref, cv_ref):
        n = pl.program_id(0)
        first = n == 0

        @pl.when(first)
        def _():
            for r in (dbq_ref, dbk_ref, dbv_ref, dsink_ref, ck_ref, cv_ref):
                r[...] = jnp.zeros_like(r)

        @pl.when(n < nb)
        def _():
            pairs = range(n_heads // 2)
            heads = range(n_heads)
            lanes = [slice(pr * 2 * hd, (pr + 1) * 2 * hd) for pr in pairs]
            kks, vvs, es, rs = _attn_core(q_ref, kp_ref, kc_ref, vp_ref, vc_ref, sink_ref, bias_ref, n_heads)
            low = lax.broadcasted_iota(jnp.int32, (B, 2 * hd), 1) < hd
            slot0 = lax.broadcasted_iota(jnp.int32, (B, B), 1) == 0
            lane_head = lax.broadcasted_iota(jnp.int32, (2 * hd, 4 * hd), 0) < hd
            out_head = lax.broadcasted_iota(jnp.int32, (2 * hd, 4 * hd), 1) < 2 * hd
            ones2 = (lane_head == out_head).astype(BF16)
            invs = [1.0 / rs[h // 2][:, (2 + 2 * (h % 2)) * hd:(4 + 2 * (h % 2)) * hd] for h in heads]
            dops = [do_ref[:, lanes[pr]] for pr in pairs]
            qps = [q_ref[:, lanes[pr]] for pr in pairs]
            dos = [dops[pr].astype(F32) * rs[pr][:, :2 * hd] * jnp.where(low, invs[2 * pr], invs[2 * pr + 1])
                   for pr in pairs]
            his = [t.astype(BF16) for t in dos]
            los = [(t - hi.astype(F32)).astype(BF16) for t, hi in zip(dos, his)]
            dsums = [_dot(hi, ones2, 1, 0) + _dot(lo, ones2, 1, 0) for hi, lo in zip(his, los)]
            dps = [_dot(dops[pr], vvs[pr // ppg], 1, 1) for pr in pairs]
            wide = lambda a: jnp.concatenate([a, a], axis=1)
            p_s = [es[h] * wide(invs[h]) for h in heads]
            ds_s = [p_s[h] * (dps[h // 2][:, (h % 2) * 2 * B:(h % 2 + 1) * 2 * B]
                              - wide(dsums[h // 2][:, (h % 2) * 2 * hd:(h % 2 + 1) * 2 * hd])) for h in heads]
            for h in heads:
                dsink_ref[h:h + 1, :] += jnp.sum(ds_s[h][:, :B], axis=0, keepdims=True)

            def without_slot0(a):
                return jnp.concatenate([jnp.where(slot0, 0.0, a[:, :B]), a[:, B:]], axis=1).astype(BF16)

            ds_b = [without_slot0(a) for a in ds_s]
            p_b = [without_slot0(a) for a in p_s]
            dqs = [_dot(jnp.concatenate([ds_b[2 * pr], ds_b[2 * pr + 1]], axis=1), kks[pr // ppg], 1, 0) for pr in pairs]
            for pr in pairs:
                dq_ref[:, lanes[pr]] = dqs[pr].astype(BF16)
                dbq_ref[:, lanes[pr]] += jnp.sum(dqs[pr], axis=0, keepdims=True)
            keeps = [low, jnp.logical_not(low)]
            qms = [jnp.where(keeps[h % 2], qps[h // 2], jnp.zeros_like(qps[0])) for h in heads]
            dms = [jnp.where(keeps[h % 2], dops[h // 2], jnp.zeros_like(dops[0])) for h in heads]
            dkh = [_dot(ds_b[h], qms[h], 0, 0) for h in heads]
            dvh = [_dot(p_b[h], dms[h], 0, 0) for h in heads]
            dks, dvs = [], []
            for kv in range(N_KV):
                tk = functools.reduce(jnp.add, dkh[kv * group:(kv + 1) * group])
                tv = functools.reduce(jnp.add, dvh[kv * group:(kv + 1) * group])
                dks.append((tk[:, :hd] + tk[:, hd:]) * scale)
                dvs.append(tv[:, :hd] + tv[:, hd:])
            dk2 = jnp.concatenate(dks, axis=1)
            dv2 = jnp.concatenate(dvs, axis=1)
            dbk_ref[...] += jnp.sum(dk2, axis=0, keepdims=True)
            dbv_ref[...] += jnp.sum(dv2, axis=0, keepdims=True)
            dkv_ref[:, :KVW] = (ck_ref[...] + dk2[:B]).astype(BF16)
            dkv_ref[:, KVW:] = (cv_ref[...] + dv2[:B]).astype(BF16)
            ck_ref[...] = dk2[B:]
            cv_ref[...] = dv2[B:]

        @pl.when(n == nb)
        def _():
            dkv_ref[:, :KVW] = ck_ref[...].astype(BF16)
            dkv_ref[:, KVW:] = cv_ref[...].astype(BF16)

    def cur(n):
        return jnp.minimum(n, nb - 1)

    def prev(n):
        return jnp.maximum(cur(n) - 1, 0)

    def outp(n):
        return jnp.maximum(n - 1, 0)

    in_specs = [pl.BlockSpec((B, HQ), lambda n: (cur(n), 0)),
                pl.BlockSpec((B, KVW), lambda n: (prev(n), kcol)), pl.BlockSpec((B, KVW), lambda n: (cur(n), kcol)),
                pl.BlockSpec((B, KVW), lambda n: (prev(n), kcol + 1)), pl.BlockSpec((B, KVW), lambda n: (cur(n), kcol + 1)),
                pl.BlockSpec((B, HQ), lambda n: (cur(n), 0)), pl.BlockSpec((n_heads, B), lambda n: (0, 0)),
                pl.BlockSpec((None, n_heads // 2, B, 4 * B), lambda n: (jnp.minimum(n, 1), 0, 0, 0))]
    out_shape = (jax.ShapeDtypeStruct((T, HQ), BF16), jax.ShapeDtypeStruct((T, 2 * KVW), BF16),
                 jax.ShapeDtypeStruct((1, HQ), F32), jax.ShapeDtypeStruct((1, KVW), F32),
                 jax.ShapeDtypeStruct((1, KVW), F32), jax.ShapeDtypeStruct((n_heads, B), F32))
    out_specs = (pl.BlockSpec((B, HQ), lambda n: (cur(n), 0)), pl.BlockSpec((B, 2 * KVW), lambda n: (outp(n), 0)),
                 _vec_spec(HQ), _vec_spec(KVW), _vec_spec(KVW), pl.BlockSpec((n_heads, B), lambda n: (0, 0)))
    return _call(body, name, (nb + 1,), in_specs, out_specs, out_shape, (qkv, qkv, qkv, qkv, qkv, do, sink_rows, bias),
                 scratch=[pltpu.VMEM((B, KVW), F32), pltpu.VMEM((B, KVW), F32)], side=side)


def _sum8(r, name):
    _, R, C = r.shape
    tr = _tile(R, 512, 16)

    def body(r_ref, o_ref):
        acc = r_ref[0].astype(F32)
        for d in range(1, N_DEV):
            acc = acc + r_ref[d].astype(F32)
        o_ref[...] = acc

    return _call(body, name, (R // tr,), [pl.BlockSpec((N_DEV, tr, C), lambda i: (0, i, 0))],
                 pl.BlockSpec((tr, C), lambda i: (i, 0)), jax.ShapeDtypeStruct((R, C), F32), (r,))


def _adamw(w, g, m, v, name):
    R, C = w.shape
    tr = _tile(R, 512, 8)
    c1 = 1.0 - ADAM_B1 ** ADAM_STEP
    c2 = 1.0 - ADAM_B2 ** ADAM_STEP

    def body(w_ref, g_ref, m_ref, v_ref, d_ref, nm_ref, nv_ref):
        gv = g_ref[...]
        nm = ADAM_B1 * m_ref[...] + (1.0 - ADAM_B1) * gv
        nv = ADAM_B2 * v_ref[...] + (1.0 - ADAM_B2) * (gv * gv)
        nm_ref[...] = nm
        nv_ref[...] = nv
        d_ref[...] = -ADAM_LR * ((nm / c1) / (jnp.sqrt(nv / c2) + ADAM_EPS) + ADAM_WD * w_ref[...])

    spec = pl.BlockSpec((tr, C), lambda i: (i, 0))
    shp = jax.ShapeDtypeStruct((R, C), F32)
    return _call(body, name, (R // tr,), [spec] * 4, (spec,) * 3, (shp, shp, shp), (w, g, m, v))


def _adamw_nd(w, g, m, v, name):
    shape = w.shape
    c = shape[-1]
    f = lambda a: a.reshape(-1, c)
    d, nm, nv = _adamw(f(w), f(g), f(m), f(v), name)
    return d.reshape(shape), nm.reshape(shape), nv.reshape(shape)


def _pack(arrs, width):
    flat = jnp.concatenate([a.reshape(-1).astype(F32) for a in arrs])
    n = flat.shape[0]
    quantum = 8 * width
    total = -(-n // quantum) * quantum
    return jnp.pad(flat, (0, total - n)).reshape(-1, width)


def _unpack(flat, shapes):
    out, off = [], 0
    for s in shapes:
        n = math.prod(s)
        out.append(flat[off:off + n].reshape(s))
        off += n
    return out


def kernel(x, c, norm1_g, norm2_g, ada_w, ada_b, attn_wqkv, attn_bqkv, attn_sinks, attn_wo, attn_bo, conv_w_in, conv_b_in, conv_dw, conv_dw_b, conv_ln_g, conv_ln_b, conv_w_out, conv_b_out, sgu_w_in, sgu_b_in, sgu_ln_g, sgu_ln_b, sgu_ws, sgu_bs, sgu_w_out, sgu_b_out, ffn_w_in, ffn_dw, ffn_dw_b, ffn_w_out, final_g, loss_target, m_norm1_g, m_norm2_g, m_ada_w, m_ada_b, m_attn_wqkv, m_attn_bqkv, m_attn_sinks, m_attn_wo, m_attn_bo, m_conv_w_in, m_conv_b_in, m_conv_dw, m_conv_dw_b, m_conv_ln_g, m_conv_ln_b, m_conv_w_out, m_conv_b_out, m_sgu_w_in, m_sgu_b_in, m_sgu_ln_g, m_sgu_ln_b, m_sgu_ws, m_sgu_bs, m_sgu_w_out, m_sgu_b_out, m_ffn_w_in, m_ffn_dw, m_ffn_dw_b, m_ffn_w_out, m_final_g, v_norm1_g, v_norm2_g, v_ada_w, v_ada_b, v_attn_wqkv, v_attn_bqkv, v_attn_sinks, v_attn_wo, v_attn_bo, v_conv_w_in, v_conv_b_in, v_conv_dw, v_conv_dw_b, v_conv_ln_g, v_conv_ln_b, v_conv_w_out, v_conv_b_out, v_sgu_w_in, v_sgu_b_in, v_sgu_ln_g, v_sgu_ln_b, v_sgu_ws, v_sgu_bs, v_sgu_w_out, v_sgu_b_out, v_ffn_w_in, v_ffn_dw, v_ffn_dw_b, v_ffn_w_out, v_final_g):
    W = dict(norm1_g=norm1_g, norm2_g=norm2_g, ada_w=ada_w, ada_b=ada_b, attn_wqkv=attn_wqkv, attn_bqkv=attn_bqkv, attn_sinks=attn_sinks, attn_wo=attn_wo, attn_bo=attn_bo, conv_w_in=conv_w_in, conv_b_in=conv_b_in, conv_dw=conv_dw, conv_dw_b=conv_dw_b, conv_ln_g=conv_ln_g, conv_ln_b=conv_ln_b, conv_w_out=conv_w_out, conv_b_out=conv_b_out, sgu_w_in=sgu_w_in, sgu_b_in=sgu_b_in, sgu_ln_g=sgu_ln_g, sgu_ln_b=sgu_ln_b, sgu_ws=sgu_ws, sgu_bs=sgu_bs, sgu_w_out=sgu_w_out, sgu_b_out=sgu_b_out, ffn_w_in=ffn_w_in, ffn_dw=ffn_dw, ffn_dw_b=ffn_dw_b, ffn_w_out=ffn_w_out, final_g=final_g)
    MOM = dict(norm1_g=m_norm1_g, norm2_g=m_norm2_g, ada_w=m_ada_w, ada_b=m_ada_b, attn_wqkv=m_attn_wqkv, attn_bqkv=m_attn_bqkv, attn_sinks=m_attn_sinks, attn_wo=m_attn_wo, attn_bo=m_attn_bo, conv_w_in=m_conv_w_in, conv_b_in=m_conv_b_in, conv_dw=m_conv_dw, conv_dw_b=m_conv_dw_b, conv_ln_g=m_conv_ln_g, conv_ln_b=m_conv_ln_b, conv_w_out=m_conv_w_out, conv_b_out=m_conv_b_out, sgu_w_in=m_sgu_w_in, sgu_b_in=m_sgu_b_in, sgu_ln_g=m_sgu_ln_g, sgu_ln_b=m_sgu_ln_b, sgu_ws=m_sgu_ws, sgu_bs=m_sgu_bs, sgu_w_out=m_sgu_w_out, sgu_b_out=m_sgu_b_out, ffn_w_in=m_ffn_w_in, ffn_dw=m_ffn_dw, ffn_dw_b=m_ffn_dw_b, ffn_w_out=m_ffn_w_out, final_g=m_final_g)
    VAR = dict(norm1_g=v_norm1_g, norm2_g=v_norm2_g, ada_w=v_ada_w, ada_b=v_ada_b, attn_wqkv=v_attn_wqkv, attn_bqkv=v_attn_bqkv, attn_sinks=v_attn_sinks, attn_wo=v_attn_wo, attn_bo=v_attn_bo, conv_w_in=v_conv_w_in, conv_b_in=v_conv_b_in, conv_dw=v_conv_dw, conv_dw_b=v_conv_dw_b, conv_ln_g=v_conv_ln_g, conv_ln_b=v_conv_ln_b, conv_w_out=v_conv_w_out, conv_b_out=v_conv_b_out, sgu_w_in=v_sgu_w_in, sgu_b_in=v_sgu_b_in, sgu_ln_g=v_sgu_ln_g, sgu_ln_b=v_sgu_ln_b, sgu_ws=v_sgu_ws, sgu_bs=v_sgu_bs, sgu_w_out=v_sgu_w_out, sgu_b_out=v_sgu_b_out, ffn_w_in=v_ffn_w_in, ffn_dw=v_ffn_dw, ffn_dw_b=v_ffn_dw_b, ffn_w_out=v_ffn_w_out, final_g=v_final_g)
    ORDER = list(W)

    _, T, D = x.shape
    depth = norm1_g.shape[0]
    n_heads = D // HEAD_DIM
    me = 4 * lax.axis_index("x") + 2 * lax.axis_index("y") + lax.axis_index("c")
    x0 = x.reshape(T, D)
    tgt = loss_target.reshape(T, D)

    small_sharded = ["attn_bqkv", "attn_bo", "conv_dw", "sgu_b_in", "sgu_ln_g", "sgu_ln_b", "sgu_b_out", "ffn_dw"]
    s_in = [c] + [W[n] for n in small_sharded]
    s_shapes = [a.shape for a in s_in]
    gathered_small = _all_gather(_pack(s_in, 128), "gather_small").reshape(N_DEV, -1)
    s_offs = [sum(math.prod(s) for s in s_shapes[:i]) for i in range(len(s_shapes))]

    def full(idx):
        shp = s_shapes[idx]
        a = gathered_small[:, s_offs[idx]:s_offs[idx] + math.prod(shp)].reshape((N_DEV,) + shp)
        return jnp.moveaxis(a, 0, -2).reshape(shp[:-1] + (N_DEV * shp[-1],))

    c_all = full(0).reshape(N_DEV, D)
    F_small = {n: full(1 + i) for i, n in enumerate(small_sharded)}
    attn_bias = _attn_bias(n_heads)

    c_act = c_all * jax.nn.sigmoid(c_all)
    c_pad = jnp.pad(c_act, ((0, 8), (0, 0))).astype(BF16)
    n_ada = ada_w.shape[-1]
    ada_cols = lax.dynamic_slice_in_dim(ada_b, me * n_ada, n_ada, axis=1)
    mod_loc = jnp.stack([_mm(c_pad, ada_w[i].astype(BF16), "nn", "ada_mod", bias=ada_cols[i:i + 1])
                         for i in range(depth)])
    mod_all = _all_gather(mod_loc, "gather_mod")
    mod_mine = lax.dynamic_index_in_dim(mod_all, me, axis=2, keepdims=False)
    mod = jnp.transpose(mod_mine, (1, 0, 2)).reshape(depth, 6, 1, D)

    mixer_names = {0: ("attn_wqkv", "attn_wo"), 1: ("conv_w_in", "conv_w_out"), 2: ("sgu_w_in", "sgu_w_out")}

    def piece(part, i, which):
        w_in, w_out = ("ffn_w_in", "ffn_w_out") if part == "ffn" else mixer_names[i % 3]
        l = i if part == "ffn" else i // 3
        return (w_in, l, W[w_in].shape[-1]) if which == "in" else (w_out, l, W[w_out].shape[1])

    def both(part, i):
        return [(part, i, "in"), (part, i, "out")] if 0 <= i < depth else []

    def local_weight(key):
        n, l, _ = piece(*key)
        return (W[n][l].T if key[2] == "in" else W[n][l]).astype(BF16)

    WF = {}

    def land_weights(keys, gathered):
        for key, g in zip(keys, gathered):
            n, l, r = piece(*key)
            WF[(n, l)] = g.reshape(N_DEV * r, D)

    def with_gather(keys, run):
        if not keys:
            return run(None)
        *outs, landed = run(("gather", [local_weight(k) for k in keys]))
        land_weights(keys, landed)
        return outs[0] if len(outs) == 1 else tuple(outs)

    def vec(a):
        return a.reshape(1, -1)

    land_weights(both("mix", 0)[:1], _exchange("gather", [local_weight(k) for k in both("mix", 0)[:1]], "gather_first"))

    saved = []
    xs, y_prev, gate_prev = x0, None, None
    for i in range(depth):
        sh1, sc1, g1, sh2, sc2, g2 = [mod[i, k] for k in range(6)]
        kind, j = i % 3, i // 3
        st = dict(kind=kind, j=j)
        if y_prev is None:
            h1 = _norm_mod_fwd(xs, vec(norm1_g[i]), sc1, sh1, "norm1_fwd")
        else:
            xs, h1 = _norm_mod_fwd(xs, vec(norm1_g[i]), sc1, sh1, "norm1_fwd", y=y_prev, gate=gate_prev)
        st.update(x_in=xs, h1=h1)
        first_ffn_in = [("ffn", 0, "in")] if i == 0 else []
        if kind == 0:
            qkv = with_gather(both("mix", 0)[1:] if i == 0 else [], lambda side: _mm(
                h1, WF[("attn_wqkv", j)], "nt", "attn_qkv", out_dtype=BF16, bias=vec(F_small["attn_bqkv"][j]), side=side))
            o = with_gather(first_ffn_in, lambda side: _attn_fwd(qkv, _sink_rows(attn_sinks[j]), attn_bias, n_heads,
                                                                 "attn_fwd", side=side))
            y1 = _mm(o, WF[("attn_wo", j)], "nn", "attn_out", out_dtype=BF16, bias=vec(F_small["attn_bo"][j]))
            st.update(qkv=qkv, o=o)
        elif kind == 1:
            p = _mm(h1, WF[("conv_w_in", j)], "nt", "conv_in", out_dtype=BF16, bias=vec(conv_b_in[j]))
            zc, s = with_gather(first_ffn_in, lambda side: _convmix_fwd(
                p, F_small["conv_dw"][j], vec(conv_dw_b[j]), vec(conv_ln_g[j]), vec(conv_ln_b[j]), "convmix_fwd", side=side))
            y1 = _mm(s, WF[("conv_w_out", j)], "nn", "conv_out", out_dtype=BF16, bias=vec(conv_b_out[j]))
            st.update(p=p, zc=zc, s=s)
        else:
            p = _mm(h1, WF[("sgu_w_in", j)], "nt", "sgu_in", out_dtype=BF16, bias=vec(F_small["sgu_b_in"][j]))
            mm_ = _sgu_fwd(p, vec(F_small["sgu_ln_g"][j]), vec(F_small["sgu_ln_b"][j]), sgu_ws[j], sgu_bs[j].T, "sgu_fwd")
            y1 = _mm(mm_, WF[("sgu_w_out", j)], "nn", "sgu_out", out_dtype=BF16, bias=vec(F_small["sgu_b_out"][j]))
            st.update(p=p, m=mm_)
        xs, h2 = _norm_mod_fwd(xs, vec(norm2_g[i]), sc2, sh2, "norm2_fwd", y=y1, gate=g1)
        nxt = i + 1
        z = with_gather(both("mix", nxt) + ([("ffn", 0, "out")] if i == 0 else []),
                        lambda side: _mm(h2, WF[("ffn_w_in", i)], "nt", "ffn_in", out_dtype=BF16, side=side))
        a = with_gather(both("ffn", nxt)[:1],
                        lambda side: _ffn_mid_fwd(z, F_small["ffn_dw"][i], vec(ffn_dw_b[i]), "ffn_mid_fwd", side=side))
        y2 = with_gather(both("ffn", nxt)[1:], lambda side: _mm(a, WF[("ffn_w_out", i)], "nn", "ffn_out",
                                                                      out_dtype=BF16, side=side))
        st.update(y1=y1, x_mid=xs, h2=h2, z=z, a=a, y2=y2)
        saved.append(st)
        y_prev, gate_prev = y2, g2

    dx, d_final_g, loss_row, dy2, dg2 = _final_loss(xs, y_prev, gate_prev, tgt, vec(final_g), "final_loss")
    loss = lax.psum(loss_row[0, 0], ("x", "y", "c"))

    G = {n: [None] * W[n].shape[0] for n in ORDER if n != "final_g"}
    GW = {}
    recv = {}
    dmod = [None] * depth

    def with_scatter(keys, run):
        if not keys:
            return run(None)
        sends = [GW[piece(*k)[:2]].reshape(N_DEV, piece(*k)[2], D) for k in keys]
        *outs, landed = run(("scatter", sends))
        recv.update(zip(keys, landed))
        return outs[0] if len(outs) == 1 else tuple(outs)

    for i in reversed(range(depth)):
        st = saved[i]
        sh1, sc1, g1, sh2, sc2, g2 = [mod[i, k] for k in range(6)]
        kind, j = st["kind"], st["j"]
        last_out = [("mix", 0, "out")] if i == 0 else []
        last_in = [("mix", 0, "in")] if i == 0 else []
        GW[("ffn_w_out", i)] = _mm(st["a"], dy2, "tn", "ffn_out_dw", out_dtype=BF16)
        da = _mm(dy2, WF[("ffn_w_out", i)], "nt", "ffn_out_dx", out_dtype=BF16)
        dzg, dzu, ddw, ddwb = with_scatter(both("mix", i + 1) + [("ffn", i, "out")], lambda side: _ffn_mid_bwd(
            st["z"], da, F_small["ffn_dw"][i], vec(ffn_dw_b[i]), "ffn_mid_bwd", side=side))
        G["ffn_dw"][i], G["ffn_dw_b"][i] = ddw, ddwb[0]
        GW[("ffn_w_in", i)] = _mm(dzg, st["h2"], "tn", "ffn_in_dw", out_dtype=BF16, a2=dzu)
        dh2 = with_scatter([("ffn", i, "in")],
                           lambda side: _mm(dzg, WF[("ffn_w_in", i)], "nn", "ffn_in_dx", out_dtype=BF16, a2=dzu, side=side))
        dx, dn2, dsc2, dsh2, dy1, dg1, dbo = _norm_mod_bwd(dh2, st["x_mid"], vec(norm2_g[i]), sc2, sh2, dx, "norm_bwd",
                                                           y_up=st["y1"], gate_up=g1)
        G["norm2_g"][i] = dn2[0]
        if kind == 0:
            G["attn_bo"][j] = dbo[0]
            GW[("attn_wo", j)] = _mm(st["o"], dy1, "tn", "attn_out_dw", out_dtype=BF16)
            do = _mm(dy1, WF[("attn_wo", j)], "nt", "attn_out_dx", out_dtype=BF16)
            dq, dkv, dbq, dbk, dbv, dsk = with_scatter(last_out, lambda side: _attn_bwd(
                st["qkv"], do, _sink_rows(attn_sinks[j]), attn_bias, n_heads, "attn_bwd", side=side))
            G["attn_bqkv"][j] = jnp.concatenate([dbq, dbk, dbv], axis=1)[0]
            G["attn_sinks"][j] = dsk[:, 0]
            GW[("attn_wqkv", j)] = _mm(dq, st["h1"], "tn", "attn_qkv_dw", out_dtype=BF16, a2=dkv)
            dh1 = with_scatter(last_in, lambda side: _mm(dq, WF[("attn_wqkv", j)], "nn", "attn_qkv_dx", out_dtype=BF16,
                                                         a2=dkv, side=side))
        elif kind == 1:
            G["conv_b_out"][j] = dbo[0]
            GW[("conv_w_out", j)] = _mm(st["s"], dy1, "tn", "conv_out_dw", out_dtype=BF16)
            ds = _mm(dy1, WF[("conv_w_out", j)], "nt", "conv_out_dx", out_dtype=BF16)
            dp, dbin, ddw, ddwb, dlng, dlnb = with_scatter(last_out, lambda side: _convmix_bwd(
                ds, st["zc"], st["p"], F_small["conv_dw"][j], vec(conv_ln_g[j]), vec(conv_ln_b[j]), "convmix_bwd",
                side=side))
            G["conv_b_in"][j], G["conv_dw"][j], G["conv_dw_b"][j] = dbin[0], ddw, ddwb[0]
            G["conv_ln_g"][j], G["conv_ln_b"][j] = dlng[0], dlnb[0]
            GW[("conv_w_in", j)] = _mm(dp, st["h1"], "tn", "conv_in_dw", out_dtype=BF16)
            dh1 = with_scatter(last_in, lambda side: _mm(dp, WF[("conv_w_in", j)], "nn", "conv_in_dx", out_dtype=BF16,
                                                         side=side))
        else:
            G["sgu_b_out"][j] = dbo[0]
            GW[("sgu_w_out", j)] = _mm(st["m"], dy1, "tn", "sgu_out_dw", out_dtype=BF16)
            dm = _mm(dy1, WF[("sgu_w_out", j)], "nt", "sgu_out_dx", out_dtype=BF16)
            dp, dbin, dlng, dlnb, dws, dbst = with_scatter(last_out, lambda side: _sgu_bwd(
                st["p"], dm, vec(F_small["sgu_ln_g"][j]), vec(F_small["sgu_ln_b"][j]), sgu_ws[j], sgu_bs[j].T, "sgu_bwd",
                side=side))
            G["sgu_b_in"][j], G["sgu_ln_g"][j], G["sgu_ln_b"][j] = dbin[0], dlng[0], dlnb[0]
            G["sgu_ws"][j], G["sgu_bs"][j] = dws, dbst.T
            GW[("sgu_w_in", j)] = _mm(dp, st["h1"], "tn", "sgu_in_dw", out_dtype=BF16)
            dh1 = with_scatter(last_in, lambda side: _mm(dp, WF[("sgu_w_in", j)], "nn", "sgu_in_dx", out_dtype=BF16,
                                                         side=side))
        dmod_i = lambda: jnp.concatenate([dsh1, dsc1, dg1, dsh2, dsc2, dg2], axis=1)[0]
        if i > 0:
            dx, dn1, dsc1, dsh1, dy2_below, dg2_below, _ = _norm_mod_bwd(
                dh1, st["x_in"], vec(norm1_g[i]), sc1, sh1, dx, "norm_bwd", y_up=saved[i - 1]["y2"], gate_up=mod[i - 1, 5])
            dmod[i] = dmod_i()
            dy2, dg2 = dy2_below, dg2_below
        else:
            dx, dn1, dsc1, dsh1 = _norm_mod_bwd(dh1, st["x_in"], vec(norm1_g[i]), sc1, sh1, dx, "norm_bwd")
            dmod[i] = dmod_i()
        G["norm1_g"][i] = dn1[0]
    grad_x = dx.reshape(x.shape)

    small_names = [n for n in ORDER if n not in
                   ("ada_w", "ada_b", "attn_wqkv", "attn_wo", "conv_w_in", "conv_w_out", "sgu_w_in", "sgu_w_out",
                    "ffn_w_in", "ffn_w_out", "final_g")]
    g_small = {n: jnp.stack(G[n]) for n in small_names}
    g_small["final_g"] = d_final_g[0]
    g_small["ada_b"] = jnp.stack(dmod)
    names16 = ["ffn_dw", "conv_dw", "sgu_ws", "ffn_dw_b", "ada_b"]
    names32 = [n for n in g_small if n not in names16]
    shapes16, shapes32 = [g_small[n].shape for n in names16], [g_small[n].shape for n in names32]
    gathered32, gathered16 = _exchange("gather", [_pack([g_small[n] for n in names32], 1024),
                                                  _pack([g_small[n] for n in names16], 1024).astype(BF16)],
                                       "gather_small_grads")
    gsum = dict(zip(names32, _unpack(_sum8(gathered32, "sum_small_grads").reshape(-1), shapes32)))
    gsum.update(zip(names16, _unpack(_sum8(gathered16, "sum_small_grads").reshape(-1), shapes16)))

    def local_shard(n, a):
        if n in small_sharded:
            w = W[n].shape[-1]
            return lax.dynamic_slice_in_dim(a, me * w, w, axis=a.ndim - 1)
        return a

    gsum = {n: local_shard(n, a) for n, a in gsum.items()}

    off16 = sum(math.prod(s) for s in shapes16[:-1])
    dmod_all = gathered16.reshape(N_DEV, -1)[:, off16:off16 + math.prod(shapes16[-1])].reshape((N_DEV,) + shapes16[-1])
    dmod_cols = lax.dynamic_slice_in_dim(dmod_all, me * n_ada, n_ada, axis=2)
    dmod_pad = jnp.pad(dmod_cols, ((0, 8), (0, 0), (0, 0)))
    g_ada_w = jnp.stack([_mm(c_pad, dmod_pad[:, i, :], "tn", "ada_dw") for i in range(depth)])

    big = {}
    for i in range(depth):
        for key in both("mix", i) + both("ffn", i):
            n, l, _ = piece(*key)
            gp = _sum8(recv[key], "sum_weight_grads")
            big.setdefault(n, {})[l] = gp.T if key[2] == "in" else gp
    grads = {n: jnp.stack([v[l] for l in range(len(v))]) for n, v in big.items()}
    grads["ada_w"] = g_ada_w
    grads.update(gsum)

    delta, new_m, new_v = {}, {}, {}
    big_names = ["ada_w", "attn_wqkv", "attn_wo", "conv_w_in", "conv_w_out", "sgu_w_in", "sgu_w_out", "ffn_w_in",
                 "ffn_w_out"]
    for n in big_names:
        delta[n], new_m[n], new_v[n] = _adamw_nd(W[n], grads[n], MOM[n], VAR[n], "adamw_" + n)
    rest = [n for n in ORDER if n not in big_names]
    rest_shapes = [W[n].shape for n in rest]
    pk = lambda d: _pack([d[n] for n in rest], 128)
    d_s, m_s, v_s = _adamw(pk(W), pk(grads), pk(MOM), pk(VAR), "adamw_small")
    for n, a, b_, c_ in zip(rest, _unpack(d_s.reshape(-1), rest_shapes), _unpack(m_s.reshape(-1), rest_shapes),
                            _unpack(v_s.reshape(-1), rest_shapes)):
        delta[n], new_m[n], new_v[n] = a, b_, c_

    return (loss, grad_x, *[grads[n] for n in ORDER], *[delta[n] for n in ORDER],
            *[new_m[n] for n in ORDER], *[new_v[n] for n in ORDER])
```

```python
import functools
import math

import jax
import jax.numpy as jnp
from jax import lax
from jax.experimental import pallas as pl
from jax.experimental.pallas import tpu as pltpu

F32 = jnp.float32
BF16 = jnp.bfloat16

N_DEV = 8
HEAD_DIM = 64
N_KV = 4
BLOCK = 128
NORM_EPS = 1e-6
NEG_INF = -1e30
ADAM_LR = 0.001
ADAM_B1 = 0.9
ADAM_B2 = 0.999
ADAM_EPS = 1e-08
ADAM_WD = 0.01
ADAM_STEP = 10
V7X_VMEM_LIMIT = 56 * 1024 * 1024
MESH = pl.DeviceIdType.MESH


def _cparams():
    return pltpu.CompilerParams(vmem_limit_bytes=V7X_VMEM_LIMIT)


def _tile(n, cap, align):
    best = None
    for t in range(align, min(n, cap) + 1, align):
        if n % t == 0:
            best = t
    return best if best is not None else n


def _dot(a, b, ca, cb):
    return lax.dot_general(a, b, (((ca,), (cb,)), ((), ())), preferred_element_type=F32)


def _sigmoid(x):
    return 1.0 / (1.0 + jnp.exp(-x))


def _rows(x, start, n):
    return lax.slice_in_dim(x, start, start + n, axis=0)


def _my_coords():
    return lax.axis_index("x"), lax.axis_index("y"), lax.axis_index("c")


def _peer(m, mx, my, mc):
    px = (mx + ((m >> 2) & 1)) % 2
    py = (my + ((m >> 1) & 1)) % 2
    pc = (mc + (m & 1)) % 2
    return px, py, pc


def _exchange_copies(kind, x_ref, o_ref, send_sems, recv_sems, local_sem):
    mx, my, mc = _my_coords()
    me = 4 * mx + 2 * my + mc
    local = pltpu.make_async_copy(x_ref if kind == "gather" else x_ref.at[me], o_ref.at[me], local_sem)
    remote = []
    for m in range(1, N_DEV):
        px, py, pc = _peer(m, mx, my, mc)
        src = x_ref if kind == "gather" else x_ref.at[4 * px + 2 * py + pc]
        remote.append(pltpu.make_async_remote_copy(
            src_ref=src, dst_ref=o_ref.at[me], send_sem=send_sems.at[m - 1], recv_sem=recv_sems.at[m - 1],
            device_id=(px, py, pc), device_id_type=MESH))
    return local, remote


def _exchange_start(kind, *refs):
    local, remote = _exchange_copies(kind, *refs)
    local.start()
    for cp in remote:
        cp.start()


def _exchange_wait(kind, *refs):
    local, remote = _exchange_copies(kind, *refs)
    for cp in remote:
        cp.wait_recv()
    for cp in remote:
        cp.wait_send()
    local.wait()


_EXCHANGE_SEMS = [pltpu.SemaphoreType.DMA((N_DEV - 1,)), pltpu.SemaphoreType.DMA((N_DEV - 1,)), pltpu.SemaphoreType.DMA]


def _exchange_shape(kind, x):
    return jax.ShapeDtypeStruct(((N_DEV,) + x.shape) if kind == "gather" else x.shape, x.dtype)


def _exchange(kind, xs, name):
    n = len(xs)

    def body(*refs):
        x_refs, o_refs, sems = refs[:n], refs[n:2 * n], refs[2 * n:]
        for e in range(n):
            _exchange_start(kind, x_refs[e], o_refs[e], *sems[3 * e:3 * e + 3])
        for e in range(n):
            _exchange_wait(kind, x_refs[e], o_refs[e], *sems[3 * e:3 * e + 3])

    any_spec = pl.BlockSpec(memory_space=pl.ANY)
    return tuple(pl.pallas_call(
        body, name=name, out_shape=tuple(_exchange_shape(kind, x) for x in xs),
        in_specs=[any_spec] * n, out_specs=(any_spec,) * n, scratch_shapes=list(_EXCHANGE_SEMS) * n,
    )(*xs))


def _all_gather(x, name):
    return _exchange("gather", [x], name)[0]


def _call(body, name, grid, in_specs, out_specs, out_shape, args, scratch=(), side=None):
    single = not isinstance(out_shape, (tuple, list))
    if single:
        out_shape, out_specs = (out_shape,), (out_specs,)
    if side is None:
        res = pl.pallas_call(body, name=name, grid=grid, in_specs=list(in_specs), out_specs=tuple(out_specs),
                             out_shape=tuple(out_shape), scratch_shapes=list(scratch), compiler_params=_cparams())(*args)
        return res[0] if single else tuple(res)
    kind, xs = side
    n_in, n_out, n_scr, n_x = len(in_specs), len(out_shape), len(scratch), len(xs)

    def wrapped(*refs):
        ins, x_refs = refs[:n_in], refs[n_in:n_in + n_x]
        o0 = n_in + n_x
        outs, o_refs = refs[o0:o0 + n_out], refs[o0 + n_out:o0 + n_out + n_x]
        s0 = o0 + n_out + n_x
        scr, sems = refs[s0:s0 + n_scr], refs[s0 + n_scr:]
        ids = [pl.program_id(a) for a in range(len(grid))]
        first = functools.reduce(jnp.logical_and, [i == 0 for i in ids])
        last = functools.reduce(jnp.logical_and, [i == g - 1 for i, g in zip(ids, grid)])

        @pl.when(first)
        def _():
            for e in range(n_x):
                _exchange_start(kind, x_refs[e], o_refs[e], *sems[3 * e:3 * e + 3])

        body(*ins, *outs, *scr)

        @pl.when(last)
        def _():
            for e in range(n_x):
                _exchange_wait(kind, x_refs[e], o_refs[e], *sems[3 * e:3 * e + 3])

    any_spec = pl.BlockSpec(memory_space=pl.ANY)
    res = pl.pallas_call(
        wrapped, name=name, grid=grid, in_specs=list(in_specs) + [any_spec] * n_x,
        out_specs=tuple(out_specs) + (any_spec,) * n_x,
        out_shape=tuple(out_shape) + tuple(_exchange_shape(kind, x) for x in xs),
        scratch_shapes=list(scratch) + list(_EXCHANGE_SEMS) * n_x, compiler_params=_cparams())(*args, *xs)
    return tuple(res[:n_out]) + (tuple(res[n_out:]),)


def _mm(a, b, mode, name, out_dtype=F32, bias=None, side=None, a2=None, tm_cap=1024, tn_cap=1408, tk_cap=1408):
    w2 = 0 if a2 is None else a2.shape[1]
    assert a2 is None or (mode in ("nn", "tn") and a2.shape[0] == a.shape[0])
    if mode == "nn":
        (M, K), (K2, N) = a.shape, b.shape
        K += w2
    elif mode == "nt":
        (M, K), (N, K2) = a.shape, b.shape
    else:
        (K, M), (K2, N) = a.shape, b.shape
        M += w2
    assert K == K2, (a.shape, b.shape, mode)
    if mode == "tn":
        tm_cap, tk_cap = 256, 8192
    if mode == "nn":
        tk_cap = 2816
    tm = _tile(math.gcd(M, w2) if mode == "tn" else M, tm_cap, 128 if mode == "tn" else 16)
    tn = _tile(N, tn_cap, 128)
    tk = _tile(math.gcd(K, w2) if mode == "nn" else K, tk_cap, 128)
    nk = K // tk
    n1 = a.shape[1] // (tm if mode == "tn" else tk)
    ca, cb = {"nn": (1, 0), "nt": (1, 1), "tn": (0, 0)}[mode]
    has_bias = bias is not None

    def body(*refs):
        refs = list(refs)
        a_ref = refs.pop(0)
        a2_ref = refs.pop(0) if a2 is not None else None
        b_ref = refs.pop(0)
        bias_ref = refs.pop(0) if has_bias else None
        o_ref, acc_ref = refs
        k = pl.program_id(2)

        def finish(r):
            if has_bias:
                r = r + bias_ref[...]
            o_ref[...] = r.astype(out_dtype)

        def step(lhs_ref):
            part = _dot(lhs_ref[...].astype(BF16), b_ref[...].astype(BF16), ca, cb)
            if nk == 1:
                finish(part)
                return

            @pl.when(k == 0)
            def _():
                acc_ref[...] = part

            @pl.when(k > 0)
            def _():
                acc_ref[...] += part

        if a2 is None:
            step(a_ref)
        else:
            col = pl.program_id(0) if mode == "tn" else k
            pl.when(col < n1)(lambda: step(a_ref))
            pl.when(col >= n1)(lambda: step(a2_ref))

        if nk > 1:
            pl.when(k == nk - 1)(lambda: finish(acc_ref[...]))

    if mode == "tn":
        a_specs = [pl.BlockSpec((tk, tm), lambda i, j, k: (k, jnp.minimum(i, n1 - 1)))]
        if a2 is not None:
            a_specs.append(pl.BlockSpec((tk, tm), lambda i, j, k: (k, jnp.maximum(i - n1, 0))))
    else:
        a_specs = [pl.BlockSpec((tm, tk), lambda i, j, k: (i, jnp.minimum(k, n1 - 1)))]
        if a2 is not None:
            a_specs.append(pl.BlockSpec((tm, tk), lambda i, j, k: (i, jnp.maximum(k - n1, 0))))
    if mode == "nt":
        b_spec = pl.BlockSpec((tn, tk), lambda i, j, k: (j, k))
    else:
        b_spec = pl.BlockSpec((tk, tn), lambda i, j, k: (k, j))
    in_specs = a_specs + [b_spec]
    args = [a] + ([a2] if a2 is not None else []) + [b]
    if has_bias:
        in_specs.append(pl.BlockSpec((1, tn), lambda i, j, k: (0, j)))
        args.append(bias)
    return _call(body, name, (M // tm, N // tn, nk), in_specs, pl.BlockSpec((tm, tn), lambda i, j, k: (i, j)),
                 jax.ShapeDtypeStruct((M, N), out_dtype), args,
                 scratch=[pltpu.VMEM((tm, tn) if nk > 1 else (8, 128), F32)], side=side)


ROW_TILE = 512


def _vec_spec(n):
    return pl.BlockSpec((1, n), lambda t: (0, 0))


def _norm_mod_fwd(x, g, sc, sh, name, y=None, gate=None):
    T, D = x.shape
    tr = _tile(T, ROW_TILE, 16)
    has_y = y is not None

    def body(*refs):
        if has_y:
            x_ref, y_ref, gate_ref, g_ref, sc_ref, sh_ref, xn_ref, h_ref = refs
            xv = x_ref[...] + gate_ref[...] * y_ref[...].astype(F32)
            xn_ref[...] = xv
        else:
            x_ref, g_ref, sc_ref, sh_ref, h_ref = refs
            xv = x_ref[...]
        r = lax.rsqrt(jnp.mean(xv * xv, axis=-1, keepdims=True) + NORM_EPS)
        h = (xv * r * g_ref[...]) * (1.0 + sc_ref[...]) + sh_ref[...]
        h_ref[...] = h.astype(BF16)

    row = pl.BlockSpec((tr, D), lambda t: (t, 0))
    vec = _vec_spec(D)
    if has_y:
        in_specs, args = [row, row, vec, vec, vec, vec], [x, y, gate, g, sc, sh]
        out_shape = (jax.ShapeDtypeStruct((T, D), F32), jax.ShapeDtypeStruct((T, D), BF16))
        out_specs = (row, row)
    else:
        in_specs, args = [row, vec, vec, vec], [x, g, sc, sh]
        out_shape = jax.ShapeDtypeStruct((T, D), BF16)
        out_specs = row
    return _call(body, name, (T // tr,), in_specs, out_specs, out_shape, args)


def _norm_mod_bwd(dh, x, g, sc, sh, dxn, name, y_up=None, gate_up=None):
    T, D = x.shape
    tr = _tile(T, ROW_TILE, 16)
    fused = y_up is not None

    def body(*refs):
        if fused:
            (dh_ref, x_ref, g_ref, sc_ref, sh_ref, dxn_ref, y_ref, gate_ref,
             dx_ref, dg_ref, dsc_ref, dsh_ref, dy_ref, dgate_ref, dbias_ref) = refs
            sums = (dg_ref, dsc_ref, dsh_ref, dgate_ref, dbias_ref)
        else:
            dh_ref, x_ref, g_ref, sc_ref, sh_ref, dxn_ref, dx_ref, dg_ref, dsc_ref, dsh_ref = refs
            sums = (dg_ref, dsc_ref, dsh_ref)

        @pl.when(pl.program_id(0) == 0)
        def _():
            for r_ in sums:
                r_[...] = jnp.zeros_like(r_)
        dh = dh_ref[...].astype(F32)
        xv = x_ref[...]
        gv = g_ref[...]
        r = lax.rsqrt(jnp.mean(xv * xv, axis=-1, keepdims=True) + NORM_EPS)
        yv = xv * r
        dsh_ref[...] += jnp.sum(dh, axis=0, keepdims=True)
        dsc_ref[...] += jnp.sum(dh * (yv * gv), axis=0, keepdims=True)
        dn = dh * (1.0 + sc_ref[...])
        dg_ref[...] += jnp.sum(dn * yv, axis=0, keepdims=True)
        dy = dn * gv
        dx = dxn_ref[...] + r * (dy - yv * jnp.mean(dy * yv, axis=-1, keepdims=True))
        dx_ref[...] = dx
        if fused:
            dyu = dx * gate_ref[...]
            dy_ref[...] = dyu.astype(BF16)
            dgate_ref[...] += jnp.sum(dx * y_ref[...].astype(F32), axis=0, keepdims=True)
            dbias_ref[...] += jnp.sum(dyu, axis=0, keepdims=True)

    row = pl.BlockSpec((tr, D), lambda t: (t, 0))
    vec = _vec_spec(D)
    vshape = jax.ShapeDtypeStruct((1, D), F32)
    in_specs, args = [row, row, vec, vec, vec, row], [dh, x, g, sc, sh, dxn]
    out_specs, out_shape = [row, vec, vec, vec], [jax.ShapeDtypeStruct((T, D), F32), vshape, vshape, vshape]
    if fused:
        in_specs, args = in_specs + [row, vec], args + [y_up, gate_up]
        out_specs, out_shape = out_specs + [row, vec, vec], out_shape + [jax.ShapeDtypeStruct((T, D), BF16), vshape, vshape]
    return _call(body, name, (T // tr,), in_specs, tuple(out_specs), tuple(out_shape), args)


def _final_loss(x, y, gate, tgt, g, name):
    T, D = x.shape
    tr = _tile(T, ROW_TILE, 16)

    def body(x_ref, y_ref, gate_ref, tgt_ref, g_ref, dx_ref, dg_ref, loss_ref, dyu_ref, dgate_ref):
        @pl.when(pl.program_id(0) == 0)
        def _():
            for r_ in (dg_ref, loss_ref, dgate_ref):
                r_[...] = jnp.zeros_like(r_)
        yu = y_ref[...].astype(F32)
        gate = gate_ref[...]
        xv = x_ref[...] + gate * yu
        gv = g_ref[...]
        r = lax.rsqrt(jnp.mean(xv * xv, axis=-1, keepdims=True) + NORM_EPS)
        yv = xv * r
        e = yv * gv - tgt_ref[...]
        per_row = jnp.mean(e * e, axis=-1, keepdims=True)
        loss_ref[...] += 0.5 * jnp.sum(per_row, axis=0, keepdims=True)
        dout = e * (1.0 / D)
        dg_ref[...] += jnp.sum(dout * yv, axis=0, keepdims=True)
        dy = dout * gv
        dx = r * (dy - yv * jnp.mean(dy * yv, axis=-1, keepdims=True))
        dx_ref[...] = dx
        dyu_ref[...] = (dx * gate).astype(BF16)
        dgate_ref[...] += jnp.sum(dx * yu, axis=0, keepdims=True)

    row = pl.BlockSpec((tr, D), lambda t: (t, 0))
    vec = _vec_spec(D)
    return _call(body, name, (T // tr,), [row, row, vec, row, vec], (row, vec, _vec_spec(128), row, vec),
                 (jax.ShapeDtypeStruct((T, D), F32), jax.ShapeDtypeStruct((1, D), F32),
                  jax.ShapeDtypeStruct((1, 128), F32), jax.ShapeDtypeStruct((T, D), BF16),
                  jax.ShapeDtypeStruct((1, D), F32)), (x, y, gate, tgt, g))


FFN_HALO = 16
FFN_PAD = 8
FFN_CHUNK = 64
FFN_ROWS_FWD = 8192
FFN_ROWS_BWD = 8192


def _lane_groups(cw):
    lw = 128 if cw % 128 == 0 else cw
    return lw, [pl.ds(g * lw, lw) for g in range(cw // lw)]


def _ffn_mid_fwd(z, dw, b, name, side=None):
    T, F2 = z.shape
    F = F2 // 2
    K = dw.shape[0]
    H, P = FFN_HALO, FFN_PAD
    tr = _tile(T, FFN_ROWS_FWD, H)
    cw = _tile(F, 512, 128)
    nc = F // cw
    rc = _tile(tr, FFN_CHUNK, 16)
    n_chunks = tr // rc
    assert n_chunks % 2 == 0 and rc >= H
    lw, groups = _lane_groups(cw)

    def body(gc_ref, gh_ref, uc_ref, uh_ref, wg_ref, wu_ref, bg_ref, bu_ref, a_ref, xg_s, xu_s):
        first = pl.program_id(1) == 0
        halves = ((xg_s, gc_ref, gh_ref, wg_ref, bg_ref), (xu_s, uc_ref, uh_ref, wu_ref, bu_ref))

        for s, c_ref, h_ref, _, _ in halves:
            for g, ls in enumerate(groups):
                s[0, g, 0:P, :] = jnp.where(first, 0.0, h_ref[:, ls].astype(F32)[H - P:H])
                s[0, g, P:P + rc, :] = c_ref[0:rc, ls].astype(F32)

        def stage(slot, c):
            start = pl.multiple_of(c * rc - H, H)
            for s, c_ref, _, _, _ in halves:
                for g, ls in enumerate(groups):
                    s[slot, g] = c_ref[pl.ds(start, rc + H), ls].astype(F32)[H - P:]

        def compute(slot, c):
            r0 = pl.multiple_of(c * rc, rc)
            for g, ls in enumerate(groups):
                vals = []
                for s, _, _, w_ref, b_ref in halves:
                    acc = b_ref[:, ls] + w_ref[K - 1:K, ls] * s[slot, g, P:P + rc, :]
                    for k in range(K - 1):
                        o = P - (K - 1 - k)
                        acc = acc + w_ref[k:k + 1, ls] * s[slot, g, o:o + rc, :]
                    vals.append(acc)
                gv, uv = vals
                a_ref[pl.ds(r0, rc), ls] = (gv * _sigmoid(gv) * uv).astype(BF16)

        def pair(i, carry):
            c0 = 2 * i
            compute(0, c0)
            stage(1, c0 + 1)
            compute(1, c0 + 1)
            stage(0, jnp.minimum(c0 + 2, n_chunks - 1))
            return carry

        lax.fori_loop(0, n_chunks // 2, pair, 0)

    rb = tr // H
    in_specs = [
        pl.BlockSpec((tr, cw), lambda j, t: (t, j)),
        pl.BlockSpec((H, cw), lambda j, t: (jnp.maximum(t * rb - 1, 0), j)),
        pl.BlockSpec((tr, cw), lambda j, t: (t, j + nc)),
        pl.BlockSpec((H, cw), lambda j, t: (jnp.maximum(t * rb - 1, 0), j + nc)),
        pl.BlockSpec((K, cw), lambda j, t: (0, j)),
        pl.BlockSpec((K, cw), lambda j, t: (0, j + nc)),
        pl.BlockSpec((1, cw), lambda j, t: (0, j)),
        pl.BlockSpec((1, cw), lambda j, t: (0, j + nc)),
    ]
    return _call(body, name, (nc, T // tr), in_specs, pl.BlockSpec((tr, cw), lambda j, t: (t, j)),
                 jax.ShapeDtypeStruct((T, F), BF16), (z, z, z, z, dw, dw, b, b),
                 scratch=[pltpu.VMEM((2, len(groups), P + rc, lw), F32)] * 2, side=side)


def _ffn_mid_bwd(z, da, dw, b, name, side=None):
    T, F2 = z.shape
    F = F2 // 2
    K = dw.shape[0]
    H, P = FFN_HALO, FFN_PAD
    tr = _tile(T, FFN_ROWS_BWD, H)
    cw = _tile(F, 512, 128)
    nc = F // cw
    nt = T // tr
    rb = tr // H
    rc = _tile(tr, FFN_CHUNK, 16)
    n_chunks = tr // rc
    assert n_chunks % 2 == 0 and rc >= H
    ext = rc + P
    lw, groups = _lane_groups(cw)

    def body(gp_ref, gc_ref, gn_ref, up_ref, uc_ref, un_ref, dac_ref, dan_ref, wg_ref, wu_ref, bg_ref, bu_ref,
             dzg_ref, dzu_ref, dwg_ref, dwu_ref, dbg_ref, dbu_ref, xg_s, xu_s, da_s, dg_s, du_s, acc_s):
        t = pl.program_id(1)
        first = t == 0
        last = t == nt - 1

        @pl.when(first)
        def _():
            for r in (dwg_ref, dwu_ref, dbg_ref, dbu_ref):
                r[...] = jnp.zeros_like(r)

        xs = ((xg_s, gp_ref, gc_ref, gn_ref), (xu_s, up_ref, uc_ref, un_ref))
        acc_s[...] = jnp.zeros_like(acc_s)

        def stage_first():
            for s, p_ref, c_ref, _ in xs:
                for g, ls in enumerate(groups):
                    s[0, g, 0:P, :] = jnp.where(first, 0.0, p_ref[:, ls].astype(F32)[H - P:H])
                    s[0, g, P:2 * P + rc, :] = c_ref[0:rc + H, ls].astype(F32)[:rc + P]
            for g, ls in enumerate(groups):
                da_s[0, g] = dac_ref[0:rc + H, ls].astype(F32)[:ext]

        def stage(slot, c):
            start = pl.multiple_of(c * rc - H, H)
            for s, _, c_ref, _ in xs:
                for g, ls in enumerate(groups):
                    s[slot, g] = c_ref[pl.ds(start, rc + 2 * H), ls].astype(F32)[H - P:H + rc + P]
            for g, ls in enumerate(groups):
                da_s[slot, g] = dac_ref[pl.ds(pl.multiple_of(c * rc, rc), rc + H), ls].astype(F32)[:ext]

        def stage_last(slot):
            for s, _, c_ref, n_ref in xs:
                for g, ls in enumerate(groups):
                    s[slot, g, 0:P + rc, :] = c_ref[tr - rc - H:tr, ls].astype(F32)[H - P:]
                    s[slot, g, P + rc:2 * P + rc, :] = jnp.where(last, 0.0, n_ref[:, ls].astype(F32)[0:P])
            for g, ls in enumerate(groups):
                da_s[slot, g, 0:rc, :] = dac_ref[tr - rc:tr, ls].astype(F32)
                da_s[slot, g, rc:ext, :] = jnp.where(last, 0.0, dan_ref[:, ls].astype(F32)[0:P])

        def fold(v):
            return jnp.sum(v.reshape(rc // 8, 8, lw), axis=0)

        def compute(slot, c):
            r0 = c * rc if isinstance(c, int) else pl.multiple_of(c * rc, rc)
            for g, ls in enumerate(groups):
                def conv_ext(s, w_ref, b_ref):
                    acc = b_ref[:, ls] + w_ref[K - 1:K, ls] * s[slot, g, P:P + ext, :]
                    for k in range(K - 1):
                        o = P - (K - 1 - k)
                        acc = acc + w_ref[k:k + 1, ls] * s[slot, g, o:o + ext, :]
                    return acc

                gv = conv_ext(xg_s, wg_ref, bg_ref)
                uv = conv_ext(xu_s, wu_ref, bu_ref)
                dav = da_s[slot, g]
                sg = _sigmoid(gv)
                dg = dav * uv * (sg * (1.0 + gv * (1.0 - sg)))
                du = dav * (gv * sg)
                dg_s[g] = dg
                du_s[g] = du
                for idx, (d_s, dval, x_s, w_ref, dz_ref) in enumerate(
                        ((dg_s, dg, xg_s, wg_ref, dzg_ref), (du_s, du, xu_s, wu_ref, dzu_ref))):
                    cur = dval[:rc]
                    acc_s[idx, K, g] += fold(cur)
                    for k in range(K):
                        o = P - (K - 1 - k)
                        acc_s[idx, k, g] += fold(cur * x_s[slot, g, o:o + rc, :])
                    dz = w_ref[K - 1:K, ls] * cur
                    for k in range(K - 1):
                        dz = dz + w_ref[k:k + 1, ls] * d_s[g, K - 1 - k:K - 1 - k + rc, :]
                    dz_ref[pl.ds(r0, rc), ls] = dz.astype(BF16)

        def pair(i, carry):
            c0 = 2 * i
            compute(0, c0)
            stage(1, c0 + 1)
            compute(1, c0 + 1)
            stage(0, c0 + 2)
            return carry

        stage_first()
        lax.fori_loop(0, n_chunks // 2 - 1, pair, 0)
        compute(0, n_chunks - 2)
        stage_last(1)
        compute(1, n_chunks - 1)
        for idx, (dw_ref, db_ref) in enumerate(((dwg_ref, dbg_ref), (dwu_ref, dbu_ref))):
            for g, ls in enumerate(groups):
                db_ref[:, ls] += jnp.sum(acc_s[idx, K, g], axis=0, keepdims=True)
                dw_ref[:, ls] += jnp.concatenate(
                    [jnp.sum(acc_s[idx, k, g], axis=0, keepdims=True) for k in range(K)], axis=0)

    def prev(t):
        return jnp.maximum(t * rb - 1, 0)

    def nxt(t):
        return jnp.minimum((t + 1) * rb, T // H - 1)

    def zspecs(off):
        return [pl.BlockSpec((H, cw), lambda j, t: (prev(t), j + off)),
                pl.BlockSpec((tr, cw), lambda j, t: (t, j + off)),
                pl.BlockSpec((H, cw), lambda j, t: (nxt(t), j + off))]

    in_specs = zspecs(0) + zspecs(nc) + [
        pl.BlockSpec((tr, cw), lambda j, t: (t, j)),
        pl.BlockSpec((H, cw), lambda j, t: (nxt(t), j)),
        pl.BlockSpec((K, cw), lambda j, t: (0, j)),
        pl.BlockSpec((K, cw), lambda j, t: (0, j + nc)),
        pl.BlockSpec((1, cw), lambda j, t: (0, j)),
        pl.BlockSpec((1, cw), lambda j, t: (0, j + nc)),
    ]
    out_shape = (jax.ShapeDtypeStruct((T, F), BF16), jax.ShapeDtypeStruct((T, F), BF16),
                 jax.ShapeDtypeStruct((K, F), F32), jax.ShapeDtypeStruct((K, F), F32),
                 jax.ShapeDtypeStruct((1, F), F32), jax.ShapeDtypeStruct((1, F), F32))
    out_specs = (pl.BlockSpec((tr, cw), lambda j, t: (t, j)), pl.BlockSpec((tr, cw), lambda j, t: (t, j)),
                 pl.BlockSpec((K, cw), lambda j, t: (0, j)), pl.BlockSpec((K, cw), lambda j, t: (0, j)),
                 pl.BlockSpec((1, cw), lambda j, t: (0, j)), pl.BlockSpec((1, cw), lambda j, t: (0, j)))
    ng = len(groups)
    scratch = [pltpu.VMEM((2, ng, 2 * P + rc, lw), F32), pltpu.VMEM((2, ng, 2 * P + rc, lw), F32),
               pltpu.VMEM((2, ng, ext, lw), F32), pltpu.VMEM((ng, ext, lw), F32), pltpu.VMEM((ng, ext, lw), F32),
               pltpu.VMEM((2, K + 1, ng, 8, lw), F32)]
    res = _call(body, name, (nc, nt), in_specs, out_specs, out_shape, (z, z, z, z, z, z, da, da, dw, dw, b, b),
                scratch=scratch, side=side)
    dzg, dzu, dwg, dwu, dbg, dbu = res[:6]
    return (dzg, dzu, jnp.concatenate([dwg, dwu], axis=1), jnp.concatenate([dbg, dbu], axis=1)) + tuple(res[6:])


CONV_HALO = 32
CONV_ROWS = 512
CONV_CHUNK = 64


def _glu_window(s, p_ref, halo_ref, first, D, groups, lw, H, tr):
    for g, ls in enumerate(groups):
        gs = pl.ds(D + g * lw, lw)
        s[g, 0:H, :] = jnp.where(first, 0.0, halo_ref[:, ls].astype(F32) * _sigmoid(halo_ref[:, gs].astype(F32)))
        s[g, H:H + tr, :] = p_ref[:, ls].astype(F32) * _sigmoid(p_ref[:, gs].astype(F32))


def _convmix_fwd(p, dw, dwb, lng, lnb, name, side=None):
    T, D2 = p.shape
    D = D2 // 2
    K = dw.shape[0]
    H = CONV_HALO
    tr = _tile(T, CONV_ROWS, H)
    rb = tr // H
    rc = _tile(tr, CONV_CHUNK, 16)
    lw, groups = _lane_groups(D)

    def body(pc_ref, ph_ref, w_ref, wb_ref, lng_ref, lnb_ref, zc_ref, s_ref, zg_s):
        first = pl.program_id(0) == 0
        _glu_window(zg_s, pc_ref, ph_ref, first, D, groups, lw, H, tr)

        def chunk(ci, carry):
            r0 = pl.multiple_of(ci * rc, rc)
            for g, ls in enumerate(groups):
                acc = wb_ref[:, ls] + w_ref[K - 1:K, ls] * zg_s[g, pl.ds(pl.multiple_of(r0 + H, 8), rc), :]
                for k in range(K - 1):
                    acc = acc + w_ref[k:k + 1, ls] * zg_s[g, pl.ds(r0 + H - (K - 1 - k), rc), :]
                zc_ref[pl.ds(r0, rc), ls] = acc
            return carry

        lax.fori_loop(0, tr // rc, chunk, 0)
        acc = zc_ref[...]
        mu = jnp.mean(acc, axis=-1, keepdims=True)
        xc = acc - mu
        rstd = lax.rsqrt(jnp.mean(xc * xc, axis=-1, keepdims=True) + NORM_EPS)
        ln = xc * rstd * lng_ref[...] + lnb_ref[...]
        s_ref[...] = (ln * _sigmoid(ln)).astype(BF16)

    in_specs = [pl.BlockSpec((tr, D2), lambda t: (t, 0)),
                pl.BlockSpec((H, D2), lambda t: (jnp.maximum(t * rb - 1, 0), 0)),
                pl.BlockSpec((K, D), lambda t: (0, 0)), _vec_spec(D), _vec_spec(D), _vec_spec(D)]
    row = pl.BlockSpec((tr, D), lambda t: (t, 0))
    return _call(body, name, (T // tr,), in_specs, (row, row),
                 (jax.ShapeDtypeStruct((T, D), F32), jax.ShapeDtypeStruct((T, D), BF16)), (p, p, dw, dwb, lng, lnb),
                 scratch=[pltpu.VMEM((len(groups), H + tr, lw), F32)], side=side)


def _convmix_bwd(ds, zc, p, dw, lng, lnb, name, side=None):
    T, D = zc.shape
    D2 = 2 * D
    K = dw.shape[0]
    H = CONV_HALO
    tr = _tile(T, CONV_ROWS, H)
    rb = tr // H
    nt = T // tr
    ext = tr + H
    rc = _tile(tr, CONV_CHUNK, 16)
    lw, groups = _lane_groups(D)

    def body(dsc_ref, dsn_ref, zcc_ref, zcn_ref, pp_ref, pc_ref, w_ref, lng_ref, lnb_ref,
             dp_ref, dbin_ref, ddw_ref, ddwb_ref, dlng_ref, dlnb_ref, zg_s, dzc_s, acc_s):
        t = pl.program_id(0)
        first = t == 0
        last = t == nt - 1

        @pl.when(first)
        def _():
            for r in (dbin_ref, ddw_ref, ddwb_ref, dlng_ref, dlnb_ref):
                r[...] = jnp.zeros_like(r)

        dsv = jnp.concatenate([dsc_ref[...].astype(F32), jnp.where(last, 0.0, dsn_ref[...].astype(F32))], axis=0)
        zcv = jnp.concatenate([zcc_ref[...], zcn_ref[...]], axis=0)
        lg = lng_ref[...]
        mu = jnp.mean(zcv, axis=-1, keepdims=True)
        xc = zcv - mu
        rstd = lax.rsqrt(jnp.mean(xc * xc, axis=-1, keepdims=True) + NORM_EPS)
        xh = xc * rstd
        ln = xh * lg + lnb_ref[...]
        sg = _sigmoid(ln)
        dln = dsv * (sg * (1.0 + ln * (1.0 - sg)))
        dlnc = _rows(dln, 0, tr)
        dlnb_ref[...] += jnp.sum(dlnc, axis=0, keepdims=True)
        dlng_ref[...] += jnp.sum(dlnc * _rows(xh, 0, tr), axis=0, keepdims=True)
        dxh = dln * lg
        dzc = rstd * (dxh - jnp.mean(dxh, axis=-1, keepdims=True) - xh * jnp.mean(dxh * xh, axis=-1, keepdims=True))
        dzcc = _rows(dzc, 0, tr)
        ddwb_ref[...] += jnp.sum(dzcc, axis=0, keepdims=True)

        for g in range(len(groups)):
            dzc_s[g] = dzc[:, g * lw:(g + 1) * lw]
        _glu_window(zg_s, pc_ref, pp_ref, first, D, groups, lw, H, tr)
        acc_s[...] = jnp.zeros_like(acc_s)

        def fold(v):
            return jnp.sum(v.reshape(rc // 8, 8, lw), axis=0)

        def chunk(ci, carry):
            r0 = pl.multiple_of(ci * rc, rc)
            for g, ls in enumerate(groups):
                gs = pl.ds(D + g * lw, lw)
                cur = dzc_s[g, pl.ds(r0, rc), :]
                dzg = w_ref[K - 1:K, ls] * cur
                for k in range(K):
                    acc_s[k, g] += fold(cur * zg_s[g, pl.ds(r0 + H - (K - 1 - k), rc), :])
                    if k < K - 1:
                        dzg = dzg + w_ref[k:k + 1, ls] * dzc_s[g, pl.ds(r0 + (K - 1 - k), rc), :]
                ac = pc_ref[pl.ds(r0, rc), ls].astype(F32)
                sgg = _sigmoid(pc_ref[pl.ds(r0, rc), gs].astype(F32))
                dpa = dzg * sgg
                dpg = dzg * ac * sgg * (1.0 - sgg)
                dp_ref[pl.ds(r0, rc), ls] = dpa.astype(BF16)
                dp_ref[pl.ds(r0, rc), gs] = dpg.astype(BF16)
                acc_s[K, g] += fold(dpa)
                acc_s[K + 1, g] += fold(dpg)
            return carry

        lax.fori_loop(0, tr // rc, chunk, 0)
        for g, ls in enumerate(groups):
            ddw_ref[:, ls] += jnp.concatenate([jnp.sum(acc_s[k, g], axis=0, keepdims=True) for k in range(K)], axis=0)
            dbin_ref[:, ls] += jnp.sum(acc_s[K, g], axis=0, keepdims=True)
            dbin_ref[:, pl.ds(D + g * lw, lw)] += jnp.sum(acc_s[K + 1, g], axis=0, keepdims=True)

    def prev(t):
        return jnp.maximum(t * rb - 1, 0)

    def nxt(t):
        return jnp.minimum((t + 1) * rb, T // H - 1)

    in_specs = [pl.BlockSpec((tr, D), lambda t: (t, 0)), pl.BlockSpec((H, D), lambda t: (nxt(t), 0)),
                pl.BlockSpec((tr, D), lambda t: (t, 0)), pl.BlockSpec((H, D), lambda t: (nxt(t), 0)),
                pl.BlockSpec((H, D2), lambda t: (prev(t), 0)), pl.BlockSpec((tr, D2), lambda t: (t, 0)),
                pl.BlockSpec((K, D), lambda t: (0, 0)), _vec_spec(D), _vec_spec(D)]
    out_shape = (jax.ShapeDtypeStruct((T, D2), BF16), jax.ShapeDtypeStruct((1, D2), F32),
                 jax.ShapeDtypeStruct((K, D), F32), jax.ShapeDtypeStruct((1, D), F32),
                 jax.ShapeDtypeStruct((1, D), F32), jax.ShapeDtypeStruct((1, D), F32))
    out_specs = (pl.BlockSpec((tr, D2), lambda t: (t, 0)), _vec_spec(D2), pl.BlockSpec((K, D), lambda t: (0, 0)),
                 _vec_spec(D), _vec_spec(D), _vec_spec(D))
    ng = len(groups)
    scratch = [pltpu.VMEM((ng, H + tr, lw), F32), pltpu.VMEM((ng, ext, lw), F32), pltpu.VMEM((K + 2, ng, 8, lw), F32)]
    return _call(body, name, (nt,), in_specs, out_specs, out_shape, (ds, ds, zc, zc, p, p, dw, lng, lnb),
                 scratch=scratch, side=side)


_INV_SQRT2 = 1.0 / math.sqrt(2.0)
_INV_SQRT2PI = 1.0 / math.sqrt(2.0 * math.pi)


def _gelu(x):
    return 0.5 * x * (1.0 + lax.erf(x * _INV_SQRT2))


def _gelu_grad(x):
    return 0.5 * (1.0 + lax.erf(x * _INV_SQRT2)) + x * jnp.exp(-0.5 * x * x) * _INV_SQRT2PI


def _tril_mask(n, transposed=False):
    r = lax.broadcasted_iota(jnp.int32, (n, n), 0)
    c = lax.broadcasted_iota(jnp.int32, (n, n), 1)
    return (r <= c) if transposed else (r >= c)


def _sgu_fwd(p, lng, lnb, ws, bs_t, name):
    T, S2 = p.shape
    S = S2 // 2
    G, C, _ = ws.shape
    gd = S // G
    cpt = 2 if (T // C) % 2 == 0 else 1
    tr = C * cpt

    def body(p_ref, lng_ref, lnb_ref, ws_ref, bst_ref, m_ref):
        mask = _tril_mask(C)
        lg, lb = lng_ref[...], lnb_ref[...]
        bst = bst_ref[...]
        for ci in range(cpt):
            pc = p_ref[ci * C:(ci + 1) * C, :].astype(F32)
            z = _gelu(pc)
            u, v = z[:, :S], z[:, S:]
            mu = jnp.mean(v, axis=-1, keepdims=True)
            xc = v - mu
            rstd = lax.rsqrt(jnp.mean(xc * xc, axis=-1, keepdims=True) + NORM_EPS)
            vn = (xc * rstd * lg + lb).astype(BF16)
            outs = []
            for g in range(G):
                wm = jnp.where(mask, ws_ref[g], 0.0).astype(BF16)
                vs = _dot(wm, vn[:, g * gd:(g + 1) * gd], 1, 0) + bst[:, g:g + 1]
                outs.append(u[:, g * gd:(g + 1) * gd] * vs)
            m_ref[ci * C:(ci + 1) * C, :] = jnp.concatenate(outs, axis=1).astype(BF16)

    in_specs = [pl.BlockSpec((tr, S2), lambda t: (t, 0)), _vec_spec(S), _vec_spec(S),
                pl.BlockSpec((G, C, C), lambda t: (0, 0, 0)), pl.BlockSpec((C, G), lambda t: (0, 0))]
    return _call(body, name, (T // tr,), in_specs, pl.BlockSpec((tr, S), lambda t: (t, 0)),
                 jax.ShapeDtypeStruct((T, S), BF16), (p, lng, lnb, ws, bs_t))


def _sgu_bwd(p, dm, lng, lnb, ws, bs_t, name, side=None):
    T, S2 = p.shape
    S = S2 // 2
    G, C, _ = ws.shape
    gd = S // G
    nt = T // C

    def body(p_ref, dm_ref, lng_ref, lnb_ref, ws_ref, bst_ref, dp_ref, dbin_ref, dlng_ref, dlnb_ref, dws_ref, dbst_ref):
        @pl.when(pl.program_id(0) == 0)
        def _():
            for r in (dbin_ref, dlng_ref, dlnb_ref, dws_ref, dbst_ref):
                r[...] = jnp.zeros_like(r)

        mask = _tril_mask(C)
        lg, lb = lng_ref[...], lnb_ref[...]
        bst = bst_ref[...]
        pc = p_ref[...].astype(F32)
        dmv = dm_ref[...].astype(F32)
        z = _gelu(pc)
        u, v = z[:, :S], z[:, S:]
        mu = jnp.mean(v, axis=-1, keepdims=True)
        xc = v - mu
        rstd = lax.rsqrt(jnp.mean(xc * xc, axis=-1, keepdims=True) + NORM_EPS)
        vh = xc * rstd
        vn = (vh * lg + lb).astype(BF16)
        dus, dvns, dbcols = [], [], []
        for g in range(G):
            sl = slice(g * gd, (g + 1) * gd)
            wm = jnp.where(mask, ws_ref[g], 0.0).astype(BF16)
            vs = _dot(wm, vn[:, sl], 1, 0) + bst[:, g:g + 1]
            dmg = dmv[:, sl]
            dus.append(dmg * vs)
            dvs = dmg * u[:, sl]
            dbcols.append(jnp.sum(dvs, axis=-1, keepdims=True))
            dvsb = dvs.astype(BF16)
            dws_ref[g] += jnp.where(mask, _dot(dvsb, vn[:, sl], 1, 1), 0.0)
            dvns.append(_dot(wm, dvsb, 0, 0))
        dbst_ref[...] += jnp.concatenate(dbcols, axis=1)
        dvn = jnp.concatenate(dvns, axis=1)
        dlnb_ref[...] += jnp.sum(dvn, axis=0, keepdims=True)
        dlng_ref[...] += jnp.sum(dvn * vh, axis=0, keepdims=True)
        dvh = dvn * lg
        dv = rstd * (dvh - jnp.mean(dvh, axis=-1, keepdims=True) - vh * jnp.mean(dvh * vh, axis=-1, keepdims=True))
        dz = jnp.concatenate([jnp.concatenate(dus, axis=1), dv], axis=1)
        dpv = dz * _gelu_grad(pc)
        dp_ref[...] = dpv.astype(BF16)
        dbin_ref[...] += jnp.sum(dpv, axis=0, keepdims=True)

    in_specs = [pl.BlockSpec((C, S2), lambda t: (t, 0)), pl.BlockSpec((C, S), lambda t: (t, 0)),
                _vec_spec(S), _vec_spec(S), pl.BlockSpec((G, C, C), lambda t: (0, 0, 0)),
                pl.BlockSpec((C, G), lambda t: (0, 0))]
    out_shape = (jax.ShapeDtypeStruct((T, S2), BF16), jax.ShapeDtypeStruct((1, S2), F32),
                 jax.ShapeDtypeStruct((1, S), F32), jax.ShapeDtypeStruct((1, S), F32),
                 jax.ShapeDtypeStruct((G, C, C), F32), jax.ShapeDtypeStruct((C, G), F32))
    out_specs = (pl.BlockSpec((C, S2), lambda t: (t, 0)), _vec_spec(S2), _vec_spec(S), _vec_spec(S),
                 pl.BlockSpec((G, C, C), lambda t: (0, 0, 0)), pl.BlockSpec((C, G), lambda t: (0, 0)))
    return _call(body, name, (nt,), in_specs, out_specs, out_shape, (p, dm, lng, lnb, ws, bs_t), side=side)


def _alibi_slope(h, n_heads):
    return 2.0 ** (-8.0 * (h + 1) / n_heads)


def _attn_bias(n_heads):
    B = BLOCK
    qi = jnp.arange(B)[:, None]
    kj = jnp.arange(2 * B)[None, :]
    dist = qi + B - kj
    band = (dist >= 0) & (dist < B)
    slopes = jnp.array([_alibi_slope(h, n_heads) for h in range(n_heads)], F32)
    out = []
    for valid in (band & (kj >= B), band):
        b = jnp.where(valid[None], -slopes[:, None, None] * dist.astype(F32)[None], NEG_INF)
        b = jnp.where(kj[None] == 0, 0.0, b)
        out.append(b.reshape(n_heads // 2, 2, B, 2 * B).transpose(0, 2, 1, 3).reshape(n_heads // 2, B, 4 * B))
    return jnp.stack(out)


def _sink_rows(sinks):
    return jnp.pad(sinks.astype(F32)[:, None], ((0, 0), (0, BLOCK - 1)))


def _block_diag2(x):
    z = jnp.zeros_like(x)
    return jnp.concatenate([jnp.concatenate([x, z], axis=1), jnp.concatenate([z, x], axis=1)], axis=0)


def _attn_core(q_ref, kp_ref, kc_ref, vp_ref, vc_ref, sink_ref, bias_ref, n_heads):
    B, hd = BLOCK, HEAD_DIM
    ppg = n_heads // N_KV // 2
    pairs = range(n_heads // 2)
    row0 = lax.broadcasted_iota(jnp.int32, (2 * B, N_KV * hd), 0) == 0
    k2 = jnp.concatenate([kp_ref[...], kc_ref[...]], axis=0)
    v2 = jnp.concatenate([vp_ref[...], vc_ref[...]], axis=0)
    k2 = jnp.where(row0, jnp.zeros_like(k2), k2)
    v2 = jnp.where(row0, jnp.zeros_like(v2), v2)
    scale = hd ** -0.5
    kks = [_block_diag2(k2[:, kv * hd:(kv + 1) * hd] * scale) for kv in range(N_KV)]
    vvs = [_block_diag2(v2[:, kv * hd:(kv + 1) * hd]) for kv in range(N_KV)]
    first_rows = lax.broadcasted_iota(jnp.int32, (4 * B, 4 * hd), 0) < 2 * B
    first_lanes = lax.broadcasted_iota(jnp.int32, (4 * B, 4 * hd), 1) < 2 * hd
    ones = (first_rows == first_lanes).astype(BF16)
    vxs = [jnp.concatenate([vv, ones], axis=1) for vv in vvs]
    ss = [_dot(q_ref[:, pr * 2 * hd:(pr + 1) * 2 * hd], kks[pr // ppg], 1, 1) + bias_ref[pr] for pr in pairs]
    halves = []
    for h in range(n_heads):
        s = ss[h // 2][:, (h % 2) * 2 * B:(h % 2 + 1) * 2 * B]
        halves.append(jnp.concatenate([s[:, :B] + sink_ref[h:h + 1, :], s[:, B:]], axis=1))
    mxs = [jnp.max(s, axis=-1, keepdims=True) for s in halves]
    es = [jnp.exp(s - m) for s, m in zip(halves, mxs)]
    rs = [_dot(jnp.concatenate([es[2 * pr], es[2 * pr + 1]], axis=1).astype(BF16), vxs[pr // ppg], 1, 0) for pr in pairs]
    return kks, vvs, es, rs


def _attn_fwd(qkv, sink_rows, bias, n_heads, name, side=None):
    T = qkv.shape[0]
    B = BLOCK
    hd = HEAD_DIM
    HQ = n_heads * hd
    KVW = N_KV * hd
    assert (n_heads // N_KV) % 2 == 0, "heads are processed in pairs that share a key/value head"
    nb = T // B
    kcol = HQ // KVW

    def body(q_ref, kp_ref, kc_ref, vp_ref, vc_ref, sink_ref, bias_ref, o_ref):
        _, _, _, rs = _attn_core(q_ref, kp_ref, kc_ref, vp_ref, vc_ref, sink_ref, bias_ref, n_heads)
        low = lax.broadcasted_iota(jnp.int32, (B, 2 * hd), 1) < hd
        for pr, r in enumerate(rs):
            inv = jnp.where(low, 1.0 / r[:, 2 * hd:4 * hd], 1.0 / r[:, 4 * hd:])
            o_ref[:, pr * 2 * hd:(pr + 1) * 2 * hd] = (r[:, :2 * hd] * inv).astype(BF16)

    def prev(n):
        return jnp.maximum(n - 1, 0)

    in_specs = [pl.BlockSpec((B, HQ), lambda n: (n, 0)),
                pl.BlockSpec((B, KVW), lambda n: (prev(n), kcol)), pl.BlockSpec((B, KVW), lambda n: (n, kcol)),
                pl.BlockSpec((B, KVW), lambda n: (prev(n), kcol + 1)), pl.BlockSpec((B, KVW), lambda n: (n, kcol + 1)),
                pl.BlockSpec((n_heads, B), lambda n: (0, 0)),
                pl.BlockSpec((None, n_heads // 2, B, 4 * B), lambda n: (jnp.minimum(n, 1), 0, 0, 0))]
    return _call(body, name, (nb,), in_specs, pl.BlockSpec((B, HQ), lambda n: (n, 0)),
                 jax.ShapeDtypeStruct((T, HQ), BF16), (qkv, qkv, qkv, qkv, qkv, sink_rows, bias), side=side)


def _attn_bwd(qkv, do, sink_rows, bias, n_heads, name, side=None):
    T = qkv.shape[0]
    B = BLOCK
    hd = HEAD_DIM
    HQ = n_heads * hd
    KVW = N_KV * hd
    group = n_heads // N_KV
    assert group % 2 == 0, "heads are processed in pairs that share a key/value head"
    ppg = group // 2
    nb = T // B
    kcol = HQ // KVW
    scale = hd ** -0.5

    def body(q_ref, kp_ref, kc_ref, vp_ref, vc_ref, do_ref, sink_ref, bias_ref,
             dq_ref, dkv_ref, dbq_ref, dbk_ref, dbv_ref, dsink_ref, ck_ref, cv_ref):
        n = pl.program_id(0)
        first = n == 0

        @pl.when(first)
        def _():
            for r in (dbq_ref, dbk_ref, dbv_ref, dsink_ref, ck_ref, cv_ref):
                r[...] = jnp.zeros_like(r)

        @pl.when(n < nb)
        def _():
            pairs = range(n_heads // 2)
            heads = range(n_heads)
            lanes = [slice(pr * 2 * hd, (pr + 1) * 2 * hd) for pr in pairs]
            kks, vvs, es, rs = _attn_core(q_ref, kp_ref, kc_ref, vp_ref, vc_ref, sink_ref, bias_ref, n_heads)
            low = lax.broadcasted_iota(jnp.int32, (B, 2 * hd), 1) < hd
            slot0 = lax.broadcasted_iota(jnp.int32, (B, B), 1) == 0
            lane_head = lax.broadcasted_iota(jnp.int32, (2 * hd, 4 * hd), 0) < hd
            out_head = lax.broadcasted_iota(jnp.int32, (2 * hd, 4 * hd), 1) < 2 * hd
            ones2 = (lane_head == out_head).astype(BF16)
            invs = [1.0 / rs[h // 2][:, (2 + 2 * (h % 2)) * hd:(4 + 2 * (h % 2)) * hd] for h in heads]
            dops = [do_ref[:, lanes[pr]] for pr in pairs]
            qps = [q_ref[:, lanes[pr]] for pr in pairs]
            dos = [dops[pr].astype(F32) * rs[pr][:, :2 * hd] * jnp.where(low, invs[2 * pr], invs[2 * pr + 1])
                   for pr in pairs]
            his = [t.astype(BF16) for t in dos]
            los = [(t - hi.astype(F32)).astype(BF16) for t, hi in zip(dos, his)]
            dsums = [_dot(hi, ones2, 1, 0) + _dot(lo, ones2, 1, 0) for hi, lo in zip(his, los)]
            dps = [_dot(dops[pr], vvs[pr // ppg], 1, 1) for pr in pairs]
            wide = lambda a: jnp.concatenate([a, a], axis=1)
            p_s = [es[h] * wide(invs[h]) for h in heads]
            ds_s = [p_s[h] * (dps[h // 2][:, (h % 2) * 2 * B:(h % 2 + 1) * 2 * B]
                              - wide(dsums[h // 2][:, (h % 2) * 2 * hd:(h % 2 + 1) * 2 * hd])) for h in heads]
            for h in heads:
                dsink_ref[h:h + 1, :] += jnp.sum(ds_s[h][:, :B], axis=0, keepdims=True)

            def without_slot0(a):
                return jnp.concatenate([jnp.where(slot0, 0.0, a[:, :B]), a[:, B:]], axis=1).astype(BF16)

            ds_b = [without_slot0(a) for a in ds_s]
            p_b = [without_slot0(a) for a in p_s]
            dqs = [_dot(jnp.concatenate([ds_b[2 * pr], ds_b[2 * pr + 1]], axis=1), kks[pr // ppg], 1, 0) for pr in pairs]
            for pr in pairs:
                dq_ref[:, lanes[pr]] = dqs[pr].astype(BF16)
                dbq_ref[:, lanes[pr]] += jnp.sum(dqs[pr], axis=0, keepdims=True)
            keeps = [low, jnp.logical_not(low)]
            qms = [jnp.where(keeps[h % 2], qps[h // 2], jnp.zeros_like(qps[0])) for h in heads]
            dms = [jnp.where(keeps[h % 2], dops[h // 2], jnp.zeros_like(dops[0])) for h in heads]
            dkh = [_dot(ds_b[h], qms[h], 0, 0) for h in heads]
            dvh = [_dot(p_b[h], dms[h], 0, 0) for h in heads]
            dks, dvs = [], []
            for kv in range(N_KV):
                tk = functools.reduce(jnp.add, dkh[kv * group:(kv + 1) * group])
                tv = functools.reduce(jnp.add, dvh[kv * group:(kv + 1) * group])
                dks.append((tk[:, :hd] + tk[:, hd:]) * scale)
                dvs.append(tv[:, :hd] + tv[:, hd:])
            dk2 = jnp.concatenate(dks, axis=1)
            dv2 = jnp.concatenate(dvs, axis=1)
            dbk_ref[...] += jnp.sum(dk2, axis=0, keepdims=True)
            dbv_ref[...] += jnp.sum(dv2, axis=0, keepdims=True)
            dkv_ref[:, :KVW] = (ck_ref[...] + dk2[:B]).astype(BF16)
            dkv_ref[:, KVW:] = (cv_ref[...] + dv2[:B]).astype(BF16)
            ck_ref[...] = dk2[B:]
            cv_ref[...] = dv2[B:]

        @pl.when(n == nb)
        def _():
            dkv_ref[:, :KVW] = ck_ref[...].astype(BF16)
            dkv_ref[:, KVW:] = cv_ref[...].astype(BF16)

    def cur(n):
        return jnp.minimum(n, nb - 1)

    def prev(n):
        return jnp.maximum(cur(n) - 1, 0)

    def outp(n):
        return jnp.maximum(n - 1, 0)

    in_specs = [pl.BlockSpec((B, HQ), lambda n: (cur(n), 0)),
                pl.BlockSpec((B, KVW), lambda n: (prev(n), kcol)), pl.BlockSpec((B, KVW), lambda n: (cur(n), kcol)),
                pl.BlockSpec((B, KVW), lambda n: (prev(n), kcol + 1)), pl.BlockSpec((B, KVW), lambda n: (cur(n), kcol + 1)),
                pl.BlockSpec((B, HQ), lambda n: (cur(n), 0)), pl.BlockSpec((n_heads, B), lambda n: (0, 0)),
                pl.BlockSpec((None, n_heads // 2, B, 4 * B), lambda n: (jnp.minimum(n, 1), 0, 0, 0))]
    out_shape = (jax.ShapeDtypeStruct((T, HQ), BF16), jax.ShapeDtypeStruct((T, 2 * KVW), BF16),
                 jax.ShapeDtypeStruct((1, HQ), F32), jax.ShapeDtypeStruct((1, KVW), F32),
                 jax.ShapeDtypeStruct((1, KVW), F32), jax.ShapeDtypeStruct((n_heads, B), F32))
    out_specs = (pl.BlockSpec((B, HQ), lambda n: (cur(n), 0)), pl.BlockSpec((B, 2 * KVW), lambda n: (outp(n), 0)),
                 _vec_spec(HQ), _vec_spec(KVW), _vec_spec(KVW), pl.BlockSpec((n_heads, B), lambda n: (0, 0)))
    return _call(body, name, (nb + 1,), in_specs, out_specs, out_shape, (qkv, qkv, qkv, qkv, qkv, do, sink_rows, bias),
                 scratch=[pltpu.VMEM((B, KVW), F32), pltpu.VMEM((B, KVW), F32)], side=side)


def _sum8(r, name):
    _, R, C = r.shape
    tr = _tile(R, 512, 16)

    def body(r_ref, o_ref):
        acc = r_ref[0].astype(F32)
        for d in range(1, N_DEV):
            acc = acc + r_ref[d].astype(F32)
        o_ref[...] = acc

    return _call(body, name, (R // tr,), [pl.BlockSpec((N_DEV, tr, C), lambda i: (0, i, 0))],
                 pl.BlockSpec((tr, C), lambda i: (i, 0)), jax.ShapeDtypeStruct((R, C), F32), (r,))


def _adamw(w, g, m, v, name):
    R, C = w.shape
    tr = _tile(R, 512, 8)
    c1 = 1.0 - ADAM_B1 ** ADAM_STEP
    c2 = 1.0 - ADAM_B2 ** ADAM_STEP

    def body(w_ref, g_ref, m_ref, v_ref, d_ref, nm_ref, nv_ref):
        gv = g_ref[...]
        nm = ADAM_B1 * m_ref[...] + (1.0 - ADAM_B1) * gv
        nv = ADAM_B2 * v_ref[...] + (1.0 - ADAM_B2) * (gv * gv)
        nm_ref[...] = nm
        nv_ref[...] = nv
        d_ref[...] = -ADAM_LR * ((nm / c1) / (jnp.sqrt(nv / c2) + ADAM_EPS) + ADAM_WD * w_ref[...])

    spec = pl.BlockSpec((tr, C), lambda i: (i, 0))
    shp = jax.ShapeDtypeStruct((R, C), F32)
    return _call(body, name, (R // tr,), [spec] * 4, (spec,) * 3, (shp, shp, shp), (w, g, m, v))


def _adamw_nd(w, g, m, v, name):
    shape = w.shape
    c = shape[-1]
    f = lambda a: a.reshape(-1, c)
    d, nm, nv = _adamw(f(w), f(g), f(m), f(v), name)
    return d.reshape(shape), nm.reshape(shape), nv.reshape(shape)


def _pack(arrs, width):
    flat = jnp.concatenate([a.reshape(-1).astype(F32) for a in arrs])
    n = flat.shape[0]
    quantum = 8 * width
    total = -(-n // quantum) * quantum
    return jnp.pad(flat, (0, total - n)).reshape(-1, width)


def _unpack(flat, shapes):
    out, off = [], 0
    for s in shapes:
        n = math.prod(s)
        out.append(flat[off:off + n].reshape(s))
        off += n
    return out


def kernel(x, c, norm1_g, norm2_g, ada_w, ada_b, attn_wqkv, attn_bqkv, attn_sinks, attn_wo, attn_bo, conv_w_in, conv_b_in, conv_dw, conv_dw_b, conv_ln_g, conv_ln_b, conv_w_out, conv_b_out, sgu_w_in, sgu_b_in, sgu_ln_g, sgu_ln_b, sgu_ws, sgu_bs, sgu_w_out, sgu_b_out, ffn_w_in, ffn_dw, ffn_dw_b, ffn_w_out, final_g, loss_target, m_norm1_g, m_norm2_g, m_ada_w, m_ada_b, m_attn_wqkv, m_attn_bqkv, m_attn_sinks, m_attn_wo, m_attn_bo, m_conv_w_in, m_conv_b_in, m_conv_dw, m_conv_dw_b, m_conv_ln_g, m_conv_ln_b, m_conv_w_out, m_conv_b_out, m_sgu_w_in, m_sgu_b_in, m_sgu_ln_g, m_sgu_ln_b, m_sgu_ws, m_sgu_bs, m_sgu_w_out, m_sgu_b_out, m_ffn_w_in, m_ffn_dw, m_ffn_dw_b, m_ffn_w_out, m_final_g, v_norm1_g, v_norm2_g, v_ada_w, v_ada_b, v_attn_wqkv, v_attn_bqkv, v_attn_sinks, v_attn_wo, v_attn_bo, v_conv_w_in, v_conv_b_in, v_conv_dw, v_conv_dw_b, v_conv_ln_g, v_conv_ln_b, v_conv_w_out, v_conv_b_out, v_sgu_w_in, v_sgu_b_in, v_sgu_ln_g, v_sgu_ln_b, v_sgu_ws, v_sgu_bs, v_sgu_w_out, v_sgu_b_out, v_ffn_w_in, v_ffn_dw, v_ffn_dw_b, v_ffn_w_out, v_final_g):
    W = dict(norm1_g=norm1_g, norm2_g=norm2_g, ada_w=ada_w, ada_b=ada_b, attn_wqkv=attn_wqkv, attn_bqkv=attn_bqkv, attn_sinks=attn_sinks, attn_wo=attn_wo, attn_bo=attn_bo, conv_w_in=conv_w_in, conv_b_in=conv_b_in, conv_dw=conv_dw, conv_dw_b=conv_dw_b, conv_ln_g=conv_ln_g, conv_ln_b=conv_ln_b, conv_w_out=conv_w_out, conv_b_out=conv_b_out, sgu_w_in=sgu_w_in, sgu_b_in=sgu_b_in, sgu_ln_g=sgu_ln_g, sgu_ln_b=sgu_ln_b, sgu_ws=sgu_ws, sgu_bs=sgu_bs, sgu_w_out=sgu_w_out, sgu_b_out=sgu_b_out, ffn_w_in=ffn_w_in, ffn_dw=ffn_dw, ffn_dw_b=ffn_dw_b, ffn_w_out=ffn_w_out, final_g=final_g)
    MOM = dict(norm1_g=m_norm1_g, norm2_g=m_norm2_g, ada_w=m_ada_w, ada_b=m_ada_b, attn_wqkv=m_attn_wqkv, attn_bqkv=m_attn_bqkv, attn_sinks=m_attn_sinks, attn_wo=m_attn_wo, attn_bo=m_attn_bo, conv_w_in=m_conv_w_in, conv_b_in=m_conv_b_in, conv_dw=m_conv_dw, conv_dw_b=m_conv_dw_b, conv_ln_g=m_conv_ln_g, conv_ln_b=m_conv_ln_b, conv_w_out=m_conv_w_out, conv_b_out=m_conv_b_out, sgu_w_in=m_sgu_w_in, sgu_b_in=m_sgu_b_in, sgu_ln_g=m_sgu_ln_g, sgu_ln_b=m_sgu_ln_b, sgu_ws=m_sgu_ws, sgu_bs=m_sgu_bs, sgu_w_out=m_sgu_w_out, sgu_b_out=m_sgu_b_out, ffn_w_in=m_ffn_w_in, ffn_dw=m_ffn_dw, ffn_dw_b=m_ffn_dw_b, ffn_w_out=m_ffn_w_out, final_g=m_final_g)
    VAR = dict(norm1_g=v_norm1_g, norm2_g=v_norm2_g, ada_w=v_ada_w, ada_b=v_ada_b, attn_wqkv=v_attn_wqkv, attn_bqkv=v_attn_bqkv, attn_sinks=v_attn_sinks, attn_wo=v_attn_wo, attn_bo=v_attn_bo, conv_w_in=v_conv_w_in, conv_b_in=v_conv_b_in, conv_dw=v_conv_dw, conv_dw_b=v_conv_dw_b, conv_ln_g=v_conv_ln_g, conv_ln_b=v_conv_ln_b, conv_w_out=v_conv_w_out, conv_b_out=v_conv_b_out, sgu_w_in=v_sgu_w_in, sgu_b_in=v_sgu_b_in, sgu_ln_g=v_sgu_ln_g, sgu_ln_b=v_sgu_ln_b, sgu_ws=v_sgu_ws, sgu_bs=v_sgu_bs, sgu_w_out=v_sgu_w_out, sgu_b_out=v_sgu_b_out, ffn_w_in=v_ffn_w_in, ffn_dw=v_ffn_dw, ffn_dw_b=v_ffn_dw_b, ffn_w_out=v_ffn_w_out, final_g=v_final_g)
    ORDER = list(W)

    _, T, D = x.shape
    depth = norm1_g.shape[0]
    n_heads = D // HEAD_DIM
    me = 4 * lax.axis_index("x") + 2 * lax.axis_index("y") + lax.axis_index("c")
    x0 = x.reshape(T, D)
    tgt = loss_target.reshape(T, D)

    small_sharded = ["attn_bqkv", "attn_bo", "conv_dw", "sgu_b_in", "sgu_ln_g", "sgu_ln_b", "sgu_b_out", "ffn_dw"]
    s_in = [c] + [W[n] for n in small_sharded]
    s_shapes = [a.shape for a in s_in]
    gathered_small = _all_gather(_pack(s_in, 128), "gather_small").reshape(N_DEV, -1)
    s_offs = [sum(math.prod(s) for s in s_shapes[:i]) for i in range(len(s_shapes))]

    def full(idx):
        shp = s_shapes[idx]
        a = gathered_small[:, s_offs[idx]:s_offs[idx] + math.prod(shp)].reshape((N_DEV,) + shp)
        return jnp.moveaxis(a, 0, -2).reshape(shp[:-1] + (N_DEV * shp[-1],))

    c_all = full(0).reshape(N_DEV, D)
    F_small = {n: full(1 + i) for i, n in enumerate(small_sharded)}
    attn_bias = _attn_bias(n_heads)

    c_act = c_all * jax.nn.sigmoid(c_all)
    c_pad = jnp.pad(c_act, ((0, 8), (0, 0))).astype(BF16)
    n_ada = ada_w.shape[-1]
    ada_cols = lax.dynamic_slice_in_dim(ada_b, me * n_ada, n_ada, axis=1)
    mod_loc = jnp.stack([_mm(c_pad, ada_w[i].astype(BF16), "nn", "ada_mod", bias=ada_cols[i:i + 1])
                         for i in range(depth)])
    mod_all = _all_gather(mod_loc, "gather_mod")
    mod_mine = lax.dynamic_index_in_dim(mod_all, me, axis=2, keepdims=False)
    mod = jnp.transpose(mod_mine, (1, 0, 2)).reshape(depth, 6, 1, D)

    mixer_names = {0: ("attn_wqkv", "attn_wo"), 1: ("conv_w_in", "conv_w_out"), 2: ("sgu_w_in", "sgu_w_out")}

    def piece(part, i, which):
        w_in, w_out = ("ffn_w_in", "ffn_w_out") if part == "ffn" else mixer_names[i % 3]
        l = i if part == "ffn" else i // 3
        return (w_in, l, W[w_in].shape[-1]) if which == "in" else (w_out, l, W[w_out].shape[1])

    def both(part, i):
        return [(part, i, "in"), (part, i, "out")] if 0 <= i < depth else []

    def local_weight(key):
        n, l, _ = piece(*key)
        return (W[n][l].T if key[2] == "in" else W[n][l]).astype(BF16)

    WF = {}

    def land_weights(keys, gathered):
        for key, g in zip(keys, gathered):
            n, l, r = piece(*key)
            WF[(n, l)] = g.reshape(N_DEV * r, D)

    def with_gather(keys, run):
        if not keys:
            return run(None)
        *outs, landed = run(("gather", [local_weight(k) for k in keys]))
        land_weights(keys, landed)
        return outs[0] if len(outs) == 1 else tuple(outs)

    def vec(a):
        return a.reshape(1, -1)

    land_weights(both("mix", 0)[:1], _exchange("gather", [local_weight(k) for k in both("mix", 0)[:1]], "gather_first"))

    saved = []
    xs, y_prev, gate_prev = x0, None, None
    for i in range(depth):
        sh1, sc1, g1, sh2, sc2, g2 = [mod[i, k] for k in range(6)]
        kind, j = i % 3, i // 3
        st = dict(kind=kind, j=j)
        if y_prev is None:
            h1 = _norm_mod_fwd(xs, vec(norm1_g[i]), sc1, sh1, "norm1_fwd")
        else:
            xs, h1 = _norm_mod_fwd(xs, vec(norm1_g[i]), sc1, sh1, "norm1_fwd", y=y_prev, gate=gate_prev)
        st.update(x_in=xs, h1=h1)
        first_ffn_in = [("ffn", 0, "in")] if i == 0 else []
        if kind == 0:
            qkv = with_gather(both("mix", 0)[1:] if i == 0 else [], lambda side: _mm(
                h1, WF[("attn_wqkv", j)], "nt", "attn_qkv", out_dtype=BF16, bias=vec(F_small["attn_bqkv"][j]), side=side))
            o = with_gather(first_ffn_in, lambda side: _attn_fwd(qkv, _sink_rows(attn_sinks[j]), attn_bias, n_heads,
                                                                 "attn_fwd", side=side))
            y1 = _mm(o, WF[("attn_wo", j)], "nn", "attn_out", out_dtype=BF16, bias=vec(F_small["attn_bo"][j]))
            st.update(qkv=qkv, o=o)
        elif kind == 1:
            p = _mm(h1, WF[("conv_w_in", j)], "nt", "conv_in", out_dtype=BF16, bias=vec(conv_b_in[j]))
            zc, s = with_gather(first_ffn_in, lambda side: _convmix_fwd(
                p, F_small["conv_dw"][j], vec(conv_dw_b[j]), vec(conv_ln_g[j]), vec(conv_ln_b[j]), "convmix_fwd", side=side))
            y1 = _mm(s, WF[("conv_w_out", j)], "nn", "conv_out", out_dtype=BF16, bias=vec(conv_b_out[j]))
            st.update(p=p, zc=zc, s=s)
        else:
            p = _mm(h1, WF[("sgu_w_in", j)], "nt", "sgu_in", out_dtype=BF16, bias=vec(F_small["sgu_b_in"][j]))
            mm_ = _sgu_fwd(p, vec(F_small["sgu_ln_g"][j]), vec(F_small["sgu_ln_b"][j]), sgu_ws[j], sgu_bs[j].T, "sgu_fwd")
            y1 = _mm(mm_, WF[("sgu_w_out", j)], "nn", "sgu_out", out_dtype=BF16, bias=vec(F_small["sgu_b_out"][j]))
            st.update(p=p, m=mm_)
        xs, h2 = _norm_mod_fwd(xs, vec(norm2_g[i]), sc2, sh2, "norm2_fwd", y=y1, gate=g1)
        nxt = i + 1
        z = with_gather(both("mix", nxt) + ([("ffn", 0, "out")] if i == 0 else []),
                        lambda side: _mm(h2, WF[("ffn_w_in", i)], "nt", "ffn_in", out_dtype=BF16, side=side))
        a = with_gather(both("ffn", nxt)[:1],
                        lambda side: _ffn_mid_fwd(z, F_small["ffn_dw"][i], vec(ffn_dw_b[i]), "ffn_mid_fwd", side=side))
        y2 = with_gather(both("ffn", nxt)[1:], lambda side: _mm(a, WF[("ffn_w_out", i)], "nn", "ffn_out",
                                                                      out_dtype=BF16, side=side))
        st.update(y1=y1, x_mid=xs, h2=h2, z=z, a=a, y2=y2)
        saved.append(st)
        y_prev, gate_prev = y2, g2

    dx, d_final_g, loss_row, dy2, dg2 = _final_loss(xs, y_prev, gate_prev, tgt, vec(final_g), "final_loss")
    loss = lax.psum(loss_row[0, 0], ("x", "y", "c"))

    G = {n: [None] * W[n].shape[0] for n in ORDER if n != "final_g"}
    GW = {}
    recv = {}
    dmod = [None] * depth

    def with_scatter(keys, run):
        if not keys:
            return run(None)
        sends = [GW[piece(*k)[:2]].reshape(N_DEV, piece(*k)[2], D) for k in keys]
        *outs, landed = run(("scatter", sends))
        recv.update(zip(keys, landed))
        return outs[0] if len(outs) == 1 else tuple(outs)

    for i in reversed(range(depth)):
        st = saved[i]
        sh1, sc1, g1, sh2, sc2, g2 = [mod[i, k] for k in range(6)]
        kind, j = st["kind"], st["j"]
        last_out = [("mix", 0, "out")] if i == 0 else []
        last_in = [("mix", 0, "in")] if i == 0 else []
        GW[("ffn_w_out", i)] = _mm(st["a"], dy2, "tn", "ffn_out_dw", out_dtype=BF16)
        da = _mm(dy2, WF[("ffn_w_out", i)], "nt", "ffn_out_dx", out_dtype=BF16)
        dzg, dzu, ddw, ddwb = with_scatter(both("mix", i + 1) + [("ffn", i, "out")], lambda side: _ffn_mid_bwd(
            st["z"], da, F_small["ffn_dw"][i], vec(ffn_dw_b[i]), "ffn_mid_bwd", side=side))
        G["ffn_dw"][i], G["ffn_dw_b"][i] = ddw, ddwb[0]
        GW[("ffn_w_in", i)] = _mm(dzg, st["h2"], "tn", "ffn_in_dw", out_dtype=BF16, a2=dzu)
        dh2 = with_scatter([("ffn", i, "in")],
                           lambda side: _mm(dzg, WF[("ffn_w_in", i)], "nn", "ffn_in_dx", out_dtype=BF16, a2=dzu, side=side))
        dx, dn2, dsc2, dsh2, dy1, dg1, dbo = _norm_mod_bwd(dh2, st["x_mid"], vec(norm2_g[i]), sc2, sh2, dx, "norm_bwd",
                                                           y_up=st["y1"], gate_up=g1)
        G["norm2_g"][i] = dn2[0]
        if kind == 0:
            G["attn_bo"][j] = dbo[0]
            GW[("attn_wo", j)] = _mm(st["o"], dy1, "tn", "attn_out_dw", out_dtype=BF16)
            do = _mm(dy1, WF[("attn_wo", j)], "nt", "attn_out_dx", out_dtype=BF16)
            dq, dkv, dbq, dbk, dbv, dsk = with_scatter(last_out, lambda side: _attn_bwd(
                st["qkv"], do, _sink_rows(attn_sinks[j]), attn_bias, n_heads, "attn_bwd", side=side))
            G["attn_bqkv"][j] = jnp.concatenate([dbq, dbk, dbv], axis=1)[0]
            G["attn_sinks"][j] = dsk[:, 0]
            GW[("attn_wqkv", j)] = _mm(dq, st["h1"], "tn", "attn_qkv_dw", out_dtype=BF16, a2=dkv)
            dh1 = with_scatter(last_in, lambda side: _mm(dq, WF[("attn_wqkv", j)], "nn", "attn_qkv_dx", out_dtype=BF16,
                                                         a2=dkv, side=side))
        elif kind == 1:
            G["conv_b_out"][j] = dbo[0]
            GW[("conv_w_out", j)] = _mm(st["s"], dy1, "tn", "conv_out_dw", out_dtype=BF16)
            ds = _mm(dy1, WF[("conv_w_out", j)], "nt", "conv_out_dx", out_dtype=BF16)
            dp, dbin, ddw, ddwb, dlng, dlnb = with_scatter(last_out, lambda side: _convmix_bwd(
                ds, st["zc"], st["p"], F_small["conv_dw"][j], vec(conv_ln_g[j]), vec(conv_ln_b[j]), "convmix_bwd",
                side=side))
            G["conv_b_in"][j], G["conv_dw"][j], G["conv_dw_b"][j] = dbin[0], ddw, ddwb[0]
            G["conv_ln_g"][j], G["conv_ln_b"][j] = dlng[0], dlnb[0]
            GW[("conv_w_in", j)] = _mm(dp, st["h1"], "tn", "conv_in_dw", out_dtype=BF16)
            dh1 = with_scatter(last_in, lambda side: _mm(dp, WF[("conv_w_in", j)], "nn", "conv_in_dx", out_dtype=BF16,
                                                         side=side))
        else:
            G["sgu_b_out"][j] = dbo[0]
            GW[("sgu_w_out", j)] = _mm(st["m"], dy1, "tn", "sgu_out_dw", out_dtype=BF16)
            dm = _mm(dy1, WF[("sgu_w_out", j)], "nt", "sgu_out_dx", out_dtype=BF16)
            dp, dbin, dlng, dlnb, dws, dbst = with_scatter(last_out, lambda side: _sgu_bwd(
                st["p"], dm, vec(F_small["sgu_ln_g"][j]), vec(F_small["sgu_ln_b"][j]), sgu_ws[j], sgu_bs[j].T, "sgu_bwd",
                side=side))
            G["sgu_b_in"][j], G["sgu_ln_g"][j], G["sgu_ln_b"][j] = dbin[0], dlng[0], dlnb[0]
            G["sgu_ws"][j], G["sgu_bs"][j] = dws, dbst.T
            GW[("sgu_w_in", j)] = _mm(dp, st["h1"], "tn", "sgu_in_dw", out_dtype=BF16)
            dh1 = with_scatter(last_in, lambda side: _mm(dp, WF[("sgu_w_in", j)], "nn", "sgu_in_dx", out_dtype=BF16,
                                                         side=side))
        dmod_i = lambda: jnp.concatenate([dsh1, dsc1, dg1, dsh2, dsc2, dg2], axis=1)[0]
        if i > 0:
            dx, dn1, dsc1, dsh1, dy2_below, dg2_below, _ = _norm_mod_bwd(
                dh1, st["x_in"], vec(norm1_g[i]), sc1, sh1, dx, "norm_bwd", y_up=saved[i - 1]["y2"], gate_up=mod[i - 1, 5])
            dmod[i] = dmod_i()
            dy2, dg2 = dy2_below, dg2_below
        else:
            dx, dn1, dsc1, dsh1 = _norm_mod_bwd(dh1, st["x_in"], vec(norm1_g[i]), sc1, sh1, dx, "norm_bwd")
            dmod[i] = dmod_i()
        G["norm1_g"][i] = dn1[0]
    grad_x = dx.reshape(x.shape)

    small_names = [n for n in ORDER if n not in
                   ("ada_w", "ada_b", "attn_wqkv", "attn_wo", "conv_w_in", "conv_w_out", "sgu_w_in", "sgu_w_out",
                    "ffn_w_in", "ffn_w_out", "final_g")]
    g_small = {n: jnp.stack(G[n]) for n in small_names}
    g_small["final_g"] = d_final_g[0]
    g_small["ada_b"] = jnp.stack(dmod)
    names16 = ["ffn_dw", "conv_dw", "sgu_ws", "ffn_dw_b", "ada_b"]
    names32 = [n for n in g_small if n not in names16]
    shapes16, shapes32 = [g_small[n].shape for n in names16], [g_small[n].shape for n in names32]
    gathered32, gathered16 = _exchange("gather", [_pack([g_small[n] for n in names32], 1024),
                                                  _pack([g_small[n] for n in names16], 1024).astype(BF16)],
                                       "gather_small_grads")
    gsum = dict(zip(names32, _unpack(_sum8(gathered32, "sum_small_grads").reshape(-1), shapes32)))
    gsum.update(zip(names16, _unpack(_sum8(gathered16, "sum_small_grads").reshape(-1), shapes16)))

    def local_shard(n, a):
        if n in small_sharded:
            w = W[n].shape[-1]
            return lax.dynamic_slice_in_dim(a, me * w, w, axis=a.ndim - 1)
        return a

    gsum = {n: local_shard(n, a) for n, a in gsum.items()}

    off16 = sum(math.prod(s) for s in shapes16[:-1])
    dmod_all = gathered16.reshape(N_DEV, -1)[:, off16:off16 + math.prod(shapes16[-1])].reshape((N_DEV,) + shapes16[-1])
    dmod_cols = lax.dynamic_slice_in_dim(dmod_all, me * n_ada, n_ada, axis=2)
    dmod_pad = jnp.pad(dmod_cols, ((0, 8), (0, 0), (0, 0)))
    g_ada_w = jnp.stack([_mm(c_pad, dmod_pad[:, i, :], "tn", "ada_dw") for i in range(depth)])

    big = {}
    for i in range(depth):
        for key in both("mix", i) + both("ffn", i):
            n, l, _ = piece(*key)
            gp = _sum8(recv[key], "sum_weight_grads")
            big.setdefault(n, {})[l] = gp.T if key[2] == "in" else gp
    grads = {n: jnp.stack([v[l] for l in range(len(v))]) for n, v in big.items()}
    grads["ada_w"] = g_ada_w
    grads.update(gsum)

    delta, new_m, new_v = {}, {}, {}
    big_names = ["ada_w", "attn_wqkv", "attn_wo", "conv_w_in", "conv_w_out", "sgu_w_in", "sgu_w_out", "ffn_w_in",
                 "ffn_w_out"]
    for n in big_names:
        delta[n], new_m[n], new_v[n] = _adamw_nd(W[n], grads[n], MOM[n], VAR[n], "adamw_" + n)
    rest = [n for n in ORDER if n not in big_names]
    rest_shapes = [W[n].shape for n in rest]
    pk = lambda d: _pack([d[n] for n in rest], 128)
    d_s, m_s, v_s = _adamw(pk(W), pk(grads), pk(MOM), pk(VAR), "adamw_small")
    for n, a, b_, c_ in zip(rest, _unpack(d_s.reshape(-1), rest_shapes), _unpack(m_s.reshape(-1), rest_shapes),
                            _unpack(v_s.reshape(-1), rest_shapes)):
        delta[n], new_m[n], new_v[n] = a, b_, c_

    return (loss, grad_x, *[grads[n] for n in ORDER], *[delta[n] for n in ORDER],
            *[new_m[n] for n in ORDER], *[new_v[n] for n in ORDER])
```

```python
import functools
import math

import jax
import jax.numpy as jnp
from jax import lax
from jax.experimental import pallas as pl
from jax.experimental.pallas import tpu as pltpu

F32 = jnp.float32
BF16 = jnp.bfloat16

N_DEV = 8
HEAD_DIM = 64
N_KV = 4
BLOCK = 128
NORM_EPS = 1e-6
NEG_INF = -1e30
ADAM_LR = 0.001
ADAM_B1 = 0.9
ADAM_B2 = 0.999
ADAM_EPS = 1e-08
ADAM_WD = 0.01
ADAM_STEP = 10
V7X_VMEM_LIMIT = 56 * 1024 * 1024
MESH = pl.DeviceIdType.MESH


def _cparams():
    return pltpu.CompilerParams(vmem_limit_bytes=V7X_VMEM_LIMIT)


def _tile(n, cap, align):
    best = None
    for t in range(align, min(n, cap) + 1, align):
        if n % t == 0:
            best = t
    return best if best is not None else n


def _dot(a, b, ca, cb):
    return lax.dot_general(a, b, (((ca,), (cb,)), ((), ())), preferred_element_type=F32)


def _sigmoid(x):
    return 1.0 / (1.0 + jnp.exp(-x))


def _rows(x, start, n):
    return lax.slice_in_dim(x, start, start + n, axis=0)


def _my_coords():
    return lax.axis_index("x"), lax.axis_index("y"), lax.axis_index("c")


def _peer(m, mx, my, mc):
    px = (mx + ((m >> 2) & 1)) % 2
    py = (my + ((m >> 1) & 1)) % 2
    pc = (mc + (m & 1)) % 2
    return px, py, pc


def _exchange_copies(kind, x_ref, o_ref, send_sems, recv_sems, local_sem):
    mx, my, mc = _my_coords()
    me = 4 * mx + 2 * my + mc
    local = pltpu.make_async_copy(x_ref if kind == "gather" else x_ref.at[me], o_ref.at[me], local_sem)
    remote = []
    for m in range(1, N_DEV):
        px, py, pc = _peer(m, mx, my, mc)
        src = x_ref if kind == "gather" else x_ref.at[4 * px + 2 * py + pc]
        remote.append(pltpu.make_async_remote_copy(
            src_ref=src, dst_ref=o_ref.at[me], send_sem=send_sems.at[m - 1], recv_sem=recv_sems.at[m - 1],
            device_id=(px, py, pc), device_id_type=MESH))
    return local, remote


def _exchange_start(kind, *refs):
    local, remote = _exchange_copies(kind, *refs)
    local.start()
    for cp in remote:
        cp.start()


def _exchange_wait(kind, *refs):
    local, remote = _exchange_copies(kind, *refs)
    for cp in remote:
        cp.wait_recv()
    for cp in remote:
        cp.wait_send()
    local.wait()


_EXCHANGE_SEMS = [pltpu.SemaphoreType.DMA((N_DEV - 1,)), pltpu.SemaphoreType.DMA((N_DEV - 1,)), pltpu.SemaphoreType.DMA]


def _exchange_shape(kind, x):
    return jax.ShapeDtypeStruct(((N_DEV,) + x.shape) if kind == "gather" else x.shape, x.dtype)


def _exchange(kind, xs, name):
    n = len(xs)

    def body(*refs):
        x_refs, o_refs, sems = refs[:n], refs[n:2 * n], refs[2 * n:]
        for e in range(n):
            _exchange_start(kind, x_refs[e], o_refs[e], *sems[3 * e:3 * e + 3])
        for e in range(n):
            _exchange_wait(kind, x_refs[e], o_refs[e], *sems[3 * e:3 * e + 3])

    any_spec = pl.BlockSpec(memory_space=pl.ANY)
    return tuple(pl.pallas_call(
        body, name=name, out_shape=tuple(_exchange_shape(kind, x) for x in xs),
        in_specs=[any_spec] * n, out_specs=(any_spec,) * n, scratch_shapes=list(_EXCHANGE_SEMS) * n,
    )(*xs))


def _all_gather(x, name):
    return _exchange("gather", [x], name)[0]


def _call(body, name, grid, in_specs, out_specs, out_shape, args, scratch=(), side=None):
    single = not isinstance(out_shape, (tuple, list))
    if single:
        out_shape, out_specs = (out_shape,), (out_specs,)
    if side is None:
        res = pl.pallas_call(body, name=name, grid=grid, in_specs=list(in_specs), out_specs=tuple(out_specs),
                             out_shape=tuple(out_shape), scratch_shapes=list(scratch), compiler_params=_cparams())(*args)
        return res[0] if single else tuple(res)
    kind, xs = side
    n_in, n_out, n_scr, n_x = len(in_specs), len(out_shape), len(scratch), len(xs)

    def wrapped(*refs):
        ins, x_refs = refs[:n_in], refs[n_in:n_in + n_x]
        o0 = n_in + n_x
        outs, o_refs = refs[o0:o0 + n_out], refs[o0 + n_out:o0 + n_out + n_x]
        s0 = o0 + n_out + n_x
        scr, sems = refs[s0:s0 + n_scr], refs[s0 + n_scr:]
        ids = [pl.program_id(a) for a in range(len(grid))]
        first = functools.reduce(jnp.logical_and, [i == 0 for i in ids])
        last = functools.reduce(jnp.logical_and, [i == g - 1 for i, g in zip(ids, grid)])

        @pl.when(first)
        def _():
            for e in range(n_x):
                _exchange_start(kind, x_refs[e], o_refs[e], *sems[3 * e:3 * e + 3])

        body(*ins, *outs, *scr)

        @pl.when(last)
        def _():
            for e in range(n_x):
                _exchange_wait(kind, x_refs[e], o_refs[e], *sems[3 * e:3 * e + 3])

    any_spec = pl.BlockSpec(memory_space=pl.ANY)
    res = pl.pallas_call(
        wrapped, name=name, grid=grid, in_specs=list(in_specs) + [any_spec] * n_x,
        out_specs=tuple(out_specs) + (any_spec,) * n_x,
        out_shape=tuple(out_shape) + tuple(_exchange_shape(kind, x) for x in xs),
        scratch_shapes=list(scratch) + list(_EXCHANGE_SEMS) * n_x, compiler_params=_cparams())(*args, *xs)
    return tuple(res[:n_out]) + (tuple(res[n_out:]),)


def _mm(a, b, mode, name, out_dtype=F32, bias=None, side=None, a2=None, tm_cap=1024, tn_cap=1408, tk_cap=1408):
    w2 = 0 if a2 is None else a2.shape[1]
    assert a2 is None or (mode in ("nn", "tn") and a2.shape[0] == a.shape[0])
    if mode == "nn":
        (M, K), (K2, N) = a.shape, b.shape
        K += w2
    elif mode == "nt":
        (M, K), (N, K2) = a.shape, b.shape
    else:
        (K, M), (K2, N) = a.shape, b.shape
        M += w2
    assert K == K2, (a.shape, b.shape, mode)
    if mode == "tn":
        tm_cap, tk_cap = 256, 8192
    if mode == "nn":
        tk_cap = 2816
    if mode == "nt" and M > 16:
        tm_cap = 2048
    tm = _tile(math.gcd(M, w2) if mode == "tn" else M, tm_cap, 128 if mode == "tn" else 16)
    tn = _tile(N, tn_cap, 128)
    tk = _tile(math.gcd(K, w2) if mode == "nn" else K, tk_cap, 128)
    nk = K // tk
    n1 = a.shape[1] // (tm if mode == "tn" else tk)
    ca, cb = {"nn": (1, 0), "nt": (1, 1), "tn": (0, 0)}[mode]
    has_bias = bias is not None

    def body(*refs):
        refs = list(refs)
        a_ref = refs.pop(0)
        a2_ref = refs.pop(0) if a2 is not None else None
        b_ref = refs.pop(0)
        bias_ref = refs.pop(0) if has_bias else None
        o_ref, acc_ref = refs
        k = pl.program_id(2)

        def finish(r):
            if has_bias:
                r = r + bias_ref[...]
            o_ref[...] = r.astype(out_dtype)

        def step(lhs_ref):
            part = _dot(lhs_ref[...].astype(BF16), b_ref[...].astype(BF16), ca, cb)
            if nk == 1:
                finish(part)
                return

            @pl.when(k == 0)
            def _():
                acc_ref[...] = part

            @pl.when(k > 0)
            def _():
                acc_ref[...] += part

        if a2 is None:
            step(a_ref)
        else:
            col = pl.program_id(0) if mode == "tn" else k
            pl.when(col < n1)(lambda: step(a_ref))
            pl.when(col >= n1)(lambda: step(a2_ref))

        if nk > 1:
            pl.when(k == nk - 1)(lambda: finish(acc_ref[...]))

    if mode == "tn":
        a_specs = [pl.BlockSpec((tk, tm), lambda i, j, k: (k, jnp.minimum(i, n1 - 1)))]
        if a2 is not None:
            a_specs.append(pl.BlockSpec((tk, tm), lambda i, j, k: (k, jnp.maximum(i - n1, 0))))
    else:
        a_specs = [pl.BlockSpec((tm, tk), lambda i, j, k: (i, jnp.minimum(k, n1 - 1)))]
        if a2 is not None:
            a_specs.append(pl.BlockSpec((tm, tk), lambda i, j, k: (i, jnp.maximum(k - n1, 0))))
    if mode == "nt":
        b_spec = pl.BlockSpec((tn, tk), lambda i, j, k: (j, k))
    else:
        b_spec = pl.BlockSpec((tk, tn), lambda i, j, k: (k, j))
    in_specs = a_specs + [b_spec]
    args = [a] + ([a2] if a2 is not None else []) + [b]
    if has_bias:
        in_specs.append(pl.BlockSpec((1, tn), lambda i, j, k: (0, j)))
        args.append(bias)
    return _call(body, name, (M // tm, N // tn, nk), in_specs, pl.BlockSpec((tm, tn), lambda i, j, k: (i, j)),
                 jax.ShapeDtypeStruct((M, N), out_dtype), args,
                 scratch=[pltpu.VMEM((tm, tn) if nk > 1 else (8, 128), F32)], side=side)


ROW_TILE = 1024


def _vec_spec(n):
    return pl.BlockSpec((1, n), lambda t: (0, 0))


def _norm_mod_fwd(x, g, sc, sh, name, y=None, gate=None):
    T, D = x.shape
    tr = _tile(T, ROW_TILE, 16)
    has_y = y is not None

    def body(*refs):
        if has_y:
            x_ref, y_ref, gate_ref, g_ref, sc_ref, sh_ref, xn_ref, h_ref = refs
            xv = x_ref[...] + gate_ref[...] * y_ref[...].astype(F32)
            xn_ref[...] = xv
        else:
            x_ref, g_ref, sc_ref, sh_ref, h_ref = refs
            xv = x_ref[...]
        r = lax.rsqrt(jnp.mean(xv * xv, axis=-1, keepdims=True) + NORM_EPS)
        h = (xv * r * g_ref[...]) * (1.0 + sc_ref[...]) + sh_ref[...]
        h_ref[...] = h.astype(BF16)

    row = pl.BlockSpec((tr, D), lambda t: (t, 0))
    vec = _vec_spec(D)
    if has_y:
        in_specs, args = [row, row, vec, vec, vec, vec], [x, y, gate, g, sc, sh]
        out_shape = (jax.ShapeDtypeStruct((T, D), F32), jax.ShapeDtypeStruct((T, D), BF16))
        out_specs = (row, row)
    else:
        in_specs, args = [row, vec, vec, vec], [x, g, sc, sh]
        out_shape = jax.ShapeDtypeStruct((T, D), BF16)
        out_specs = row
    return _call(body, name, (T // tr,), in_specs, out_specs, out_shape, args)


def _norm_mod_bwd(dh, x, g, sc, sh, dxn, name, y_up=None, gate_up=None):
    T, D = x.shape
    tr = _tile(T, ROW_TILE, 16)
    fused = y_up is not None

    def body(*refs):
        if fused:
            (dh_ref, x_ref, g_ref, sc_ref, sh_ref, dxn_ref, y_ref, gate_ref,
             dx_ref, dg_ref, dsc_ref, dsh_ref, dy_ref, dgate_ref, dbias_ref) = refs
            sums = (dg_ref, dsc_ref, dsh_ref, dgate_ref, dbias_ref)
        else:
            dh_ref, x_ref, g_ref, sc_ref, sh_ref, dxn_ref, dx_ref, dg_ref, dsc_ref, dsh_ref = refs
            sums = (dg_ref, dsc_ref, dsh_ref)

        @pl.when(pl.program_id(0) == 0)
        def _():
            for r_ in sums:
                r_[...] = jnp.zeros_like(r_)
        dh = dh_ref[...].astype(F32)
        xv = x_ref[...]
        gv = g_ref[...]
        r = lax.rsqrt(jnp.mean(xv * xv, axis=-1, keepdims=True) + NORM_EPS)
        yv = xv * r
        dsh_ref[...] += jnp.sum(dh, axis=0, keepdims=True)
        dsc_ref[...] += jnp.sum(dh * (yv * gv), axis=0, keepdims=True)
        dn = dh * (1.0 + sc_ref[...])
        dg_ref[...] += jnp.sum(dn * yv, axis=0, keepdims=True)
        dy = dn * gv
        dx = dxn_ref[...] + r * (dy - yv * jnp.mean(dy * yv, axis=-1, keepdims=True))
        dx_ref[...] = dx
        if fused:
            dyu = dx * gate_ref[...]
            dy_ref[...] = dyu.astype(BF16)
            dgate_ref[...] += jnp.sum(dx * y_ref[...].astype(F32), axis=0, keepdims=True)
            dbias_ref[...] += jnp.sum(dyu, axis=0, keepdims=True)

    row = pl.BlockSpec((tr, D), lambda t: (t, 0))
    vec = _vec_spec(D)
    vshape = jax.ShapeDtypeStruct((1, D), F32)
    in_specs, args = [row, row, vec, vec, vec, row], [dh, x, g, sc, sh, dxn]
    out_specs, out_shape = [row, vec, vec, vec], [jax.ShapeDtypeStruct((T, D), F32), vshape, vshape, vshape]
    if fused:
        in_specs, args = in_specs + [row, vec], args + [y_up, gate_up]
        out_specs, out_shape = out_specs + [row, vec, vec], out_shape + [jax.ShapeDtypeStruct((T, D), BF16), vshape, vshape]
    return _call(body, name, (T // tr,), in_specs, tuple(out_specs), tuple(out_shape), args)


def _final_loss(x, y, gate, tgt, g, name):
    T, D = x.shape
    tr = _tile(T, ROW_TILE, 16)

    def body(x_ref, y_ref, gate_ref, tgt_ref, g_ref, dx_ref, dg_ref, loss_ref, dyu_ref, dgate_ref):
        @pl.when(pl.program_id(0) == 0)
        def _():
            for r_ in (dg_ref, loss_ref, dgate_ref):
                r_[...] = jnp.zeros_like(r_)
        yu = y_ref[...].astype(F32)
        gate = gate_ref[...]
        xv = x_ref[...] + gate * yu
        gv = g_ref[...]
        r = lax.rsqrt(jnp.mean(xv * xv, axis=-1, keepdims=True) + NORM_EPS)
        yv = xv * r
        e = yv * gv - tgt_ref[...]
        per_row = jnp.mean(e * e, axis=-1, keepdims=True)
        loss_ref[...] += 0.5 * jnp.sum(per_row, axis=0, keepdims=True)
        dout = e * (1.0 / D)
        dg_ref[...] += jnp.sum(dout * yv, axis=0, keepdims=True)
        dy = dout * gv
        dx = r * (dy - yv * jnp.mean(dy * yv, axis=-1, keepdims=True))
        dx_ref[...] = dx
        dyu_ref[...] = (dx * gate).astype(BF16)
        dgate_ref[...] += jnp.sum(dx * yu, axis=0, keepdims=True)

    row = pl.BlockSpec((tr, D), lambda t: (t, 0))
    vec = _vec_spec(D)
    return _call(body, name, (T // tr,), [row, row, vec, row, vec], (row, vec, _vec_spec(128), row, vec),
                 (jax.ShapeDtypeStruct((T, D), F32), jax.ShapeDtypeStruct((1, D), F32),
                  jax.ShapeDtypeStruct((1, 128), F32), jax.ShapeDtypeStruct((T, D), BF16),
                  jax.ShapeDtypeStruct((1, D), F32)), (x, y, gate, tgt, g))


FFN_HALO = 16
FFN_PAD = 8
FFN_CHUNK = 64
FFN_ROWS_FWD = 8192
FFN_ROWS_BWD = 8192


def _lane_groups(cw):
    lw = 128 if cw % 128 == 0 else cw
    return lw, [pl.ds(g * lw, lw) for g in range(cw // lw)]


def _ffn_mid_fwd(z, dw, b, name, side=None):
    T, F2 = z.shape
    F = F2 // 2
    K = dw.shape[0]
    H, P = FFN_HALO, FFN_PAD
    tr = _tile(T, FFN_ROWS_FWD, H)
    cw = _tile(F, 512, 128)
    nc = F // cw
    rc = _tile(tr, FFN_CHUNK, 16)
    n_chunks = tr // rc
    assert n_chunks % 2 == 0 and rc >= H
    lw, groups = _lane_groups(cw)

    def body(gc_ref, gh_ref, uc_ref, uh_ref, wg_ref, wu_ref, bg_ref, bu_ref, a_ref, xg_s, xu_s):
        first = pl.program_id(1) == 0
        halves = ((xg_s, gc_ref, gh_ref, wg_ref, bg_ref), (xu_s, uc_ref, uh_ref, wu_ref, bu_ref))

        for s, c_ref, h_ref, _, _ in halves:
            for g, ls in enumerate(groups):
                s[0, g, 0:P, :] = jnp.where(first, 0.0, h_ref[:, ls].astype(F32)[H - P:H])
                s[0, g, P:P + rc, :] = c_ref[0:rc, ls].astype(F32)

        def stage(slot, c):
            start = pl.multiple_of(c * rc - H, H)
            for s, c_ref, _, _, _ in halves:
                for g, ls in enumerate(groups):
                    s[slot, g] = c_ref[pl.ds(start, rc + H), ls].astype(F32)[H - P:]

        def compute(slot, c):
            r0 = pl.multiple_of(c * rc, rc)
            for g, ls in enumerate(groups):
                vals = []
                for s, _, _, w_ref, b_ref in halves:
                    acc = b_ref[:, ls] + w_ref[K - 1:K, ls] * s[slot, g, P:P + rc, :]
                    for k in range(K - 1):
                        o = P - (K - 1 - k)
                        acc = acc + w_ref[k:k + 1, ls] * s[slot, g, o:o + rc, :]
                    vals.append(acc)
                gv, uv = vals
                a_ref[pl.ds(r0, rc), ls] = (gv * _sigmoid(gv) * uv).astype(BF16)

        def pair(i, carry):
            c0 = 2 * i
            compute(0, c0)
            stage(1, c0 + 1)
            compute(1, c0 + 1)
            stage(0, jnp.minimum(c0 + 2, n_chunks - 1))
            return carry

        lax.fori_loop(0, n_chunks // 2, pair, 0)

    rb = tr // H
    in_specs = [
        pl.BlockSpec((tr, cw), lambda j, t: (t, j)),
        pl.BlockSpec((H, cw), lambda j, t: (jnp.maximum(t * rb - 1, 0), j)),
        pl.BlockSpec((tr, cw), lambda j, t: (t, j + nc)),
        pl.BlockSpec((H, cw), lambda j, t: (jnp.maximum(t * rb - 1, 0), j + nc)),
        pl.BlockSpec((K, cw), lambda j, t: (0, j)),
        pl.BlockSpec((K, cw), lambda j, t: (0, j + nc)),
        pl.BlockSpec((1, cw), lambda j, t: (0, j)),
        pl.BlockSpec((1, cw), lambda j, t: (0, j + nc)),
    ]
    return _call(body, name, (nc, T // tr), in_specs, pl.BlockSpec((tr, cw), lambda j, t: (t, j)),
                 jax.ShapeDtypeStruct((T, F), BF16), (z, z, z, z, dw, dw, b, b),
                 scratch=[pltpu.VMEM((2, len(groups), P + rc, lw), F32)] * 2, side=side)


def _ffn_mid_bwd(z, da, dw, b, name, side=None):
    T, F2 = z.shape
    F = F2 // 2
    K = dw.shape[0]
    H, P = FFN_HALO, FFN_PAD
    tr = _tile(T, FFN_ROWS_BWD, H)
    cw = _tile(F, 512, 128)
    nc = F // cw
    nt = T // tr
    rb = tr // H
    rc = _tile(tr, FFN_CHUNK, 16)
    n_chunks = tr // rc
    assert n_chunks % 2 == 0 and rc >= H
    ext = rc + P
    lw, groups = _lane_groups(cw)

    def body(gp_ref, gc_ref, gn_ref, up_ref, uc_ref, un_ref, dac_ref, dan_ref, wg_ref, wu_ref, bg_ref, bu_ref,
             dzg_ref, dzu_ref, dwg_ref, dwu_ref, dbg_ref, dbu_ref, xg_s, xu_s, da_s, dg_s, du_s, acc_s):
        t = pl.program_id(1)
        first = t == 0
        last = t == nt - 1

        @pl.when(first)
        def _():
            for r in (dwg_ref, dwu_ref, dbg_ref, dbu_ref):
                r[...] = jnp.zeros_like(r)

        xs = ((xg_s, gp_ref, gc_ref, gn_ref), (xu_s, up_ref, uc_ref, un_ref))
        acc_s[...] = jnp.zeros_like(acc_s)

        def stage_first():
            for s, p_ref, c_ref, _ in xs:
                for g, ls in enumerate(groups):
                    s[0, g, 0:P, :] = jnp.where(first, 0.0, p_ref[:, ls].astype(F32)[H - P:H])
                    s[0, g, P:2 * P + rc, :] = c_ref[0:rc + H, ls].astype(F32)[:rc + P]
            for g, ls in enumerate(groups):
                da_s[0, g] = dac_ref[0:rc + H, ls].astype(F32)[:ext]

        def stage(slot, c):
            start = pl.multiple_of(c * rc - H, H)
            for s, _, c_ref, _ in xs:
                for g, ls in enumerate(groups):
                    s[slot, g] = c_ref[pl.ds(start, rc + 2 * H), ls].astype(F32)[H - P:H + rc + P]
            for g, ls in enumerate(groups):
                da_s[slot, g] = dac_ref[pl.ds(pl.multiple_of(c * rc, rc), rc + H), ls].astype(F32)[:ext]

        def stage_last(slot):
            for s, _, c_ref, n_ref in xs:
                for g, ls in enumerate(groups):
                    s[slot, g, 0:P + rc, :] = c_ref[tr - rc - H:tr, ls].astype(F32)[H - P:]
                    s[slot, g, P + rc:2 * P + rc, :] = jnp.where(last, 0.0, n_ref[:, ls].astype(F32)[0:P])
            for g, ls in enumerate(groups):
                da_s[slot, g, 0:rc, :] = dac_ref[tr - rc:tr, ls].astype(F32)
                da_s[slot, g, rc:ext, :] = jnp.where(last, 0.0, dan_ref[:, ls].astype(F32)[0:P])

        def fold(v):
            return jnp.sum(v.reshape(rc // 8, 8, lw), axis=0)

        def compute(slot, c):
            r0 = c * rc if isinstance(c, int) else pl.multiple_of(c * rc, rc)
            for g, ls in enumerate(groups):
                def conv_ext(s, w_ref, b_ref):
                    acc = b_ref[:, ls] + w_ref[K - 1:K, ls] * s[slot, g, P:P + ext, :]
                    for k in range(K - 1):
                        o = P - (K - 1 - k)
                        acc = acc + w_ref[k:k + 1, ls] * s[slot, g, o:o + ext, :]
                    return acc

                gv = conv_ext(xg_s, wg_ref, bg_ref)
                uv = conv_ext(xu_s, wu_ref, bu_ref)
                dav = da_s[slot, g]
                sg = _sigmoid(gv)
                dg = dav * uv * (sg * (1.0 + gv * (1.0 - sg)))
                du = dav * (gv * sg)
                dg_s[g] = dg
                du_s[g] = du
                for idx, (d_s, dval, x_s, w_ref, dz_ref) in enumerate(
                        ((dg_s, dg, xg_s, wg_ref, dzg_ref), (du_s, du, xu_s, wu_ref, dzu_ref))):
                    cur = dval[:rc]
                    acc_s[idx, K, g] += fold(cur)
                    for k in range(K):
                        o = P - (K - 1 - k)
                        acc_s[idx, k, g] += fold(cur * x_s[slot, g, o:o + rc, :])
                    dz = w_ref[K - 1:K, ls] * cur
                    for k in range(K - 1):
                        dz = dz + w_ref[k:k + 1, ls] * d_s[g, K - 1 - k:K - 1 - k + rc, :]
                    dz_ref[pl.ds(r0, rc), ls] = dz.astype(BF16)

        def pair(i, carry):
            c0 = 2 * i
            compute(0, c0)
            stage(1, c0 + 1)
            compute(1, c0 + 1)
            stage(0, c0 + 2)
            return carry

        stage_first()
        lax.fori_loop(0, n_chunks // 2 - 1, pair, 0)
        compute(0, n_chunks - 2)
        stage_last(1)
        compute(1, n_chunks - 1)
        for idx, (dw_ref, db_ref) in enumerate(((dwg_ref, dbg_ref), (dwu_ref, dbu_ref))):
            for g, ls in enumerate(groups):
                db_ref[:, ls] += jnp.sum(acc_s[idx, K, g], axis=0, keepdims=True)
                dw_ref[:, ls] += jnp.concatenate(
                    [jnp.sum(acc_s[idx, k, g], axis=0, keepdims=True) for k in range(K)], axis=0)

    def prev(t):
        return jnp.maximum(t * rb - 1, 0)

    def nxt(t):
        return jnp.minimum((t + 1) * rb, T // H - 1)

    def zspecs(off):
        return [pl.BlockSpec((H, cw), lambda j, t: (prev(t), j + off)),
                pl.BlockSpec((tr, cw), lambda j, t: (t, j + off)),
                pl.BlockSpec((H, cw), lambda j, t: (nxt(t), j + off))]

    in_specs = zspecs(0) + zspecs(nc) + [
        pl.BlockSpec((tr, cw), lambda j, t: (t, j)),
        pl.BlockSpec((H, cw), lambda j, t: (nxt(t), j)),
        pl.BlockSpec((K, cw), lambda j, t: (0, j)),
        pl.BlockSpec((K, cw), lambda j, t: (0, j + nc)),
        pl.BlockSpec((1, cw), lambda j, t: (0, j)),
        pl.BlockSpec((1, cw), lambda j, t: (0, j + nc)),
    ]
    out_shape = (jax.ShapeDtypeStruct((T, F), BF16), jax.ShapeDtypeStruct((T, F), BF16),
                 jax.ShapeDtypeStruct((K, F), F32), jax.ShapeDtypeStruct((K, F), F32),
                 jax.ShapeDtypeStruct((1, F), F32), jax.ShapeDtypeStruct((1, F), F32))
    out_specs = (pl.BlockSpec((tr, cw), lambda j, t: (t, j)), pl.BlockSpec((tr, cw), lambda j, t: (t, j)),
                 pl.BlockSpec((K, cw), lambda j, t: (0, j)), pl.BlockSpec((K, cw), lambda j, t: (0, j)),
                 pl.BlockSpec((1, cw), lambda j, t: (0, j)), pl.BlockSpec((1, cw), lambda j, t: (0, j)))
    ng = len(groups)
    scratch = [pltpu.VMEM((2, ng, 2 * P + rc, lw), F32), pltpu.VMEM((2, ng, 2 * P + rc, lw), F32),
               pltpu.VMEM((2, ng, ext, lw), F32), pltpu.VMEM((ng, ext, lw), F32), pltpu.VMEM((ng, ext, lw), F32),
               pltpu.VMEM((2, K + 1, ng, 8, lw), F32)]
    res = _call(body, name, (nc, nt), in_specs, out_specs, out_shape, (z, z, z, z, z, z, da, da, dw, dw, b, b),
                scratch=scratch, side=side)
    dzg, dzu, dwg, dwu, dbg, dbu = res[:6]
    return (dzg, dzu, jnp.concatenate([dwg, dwu], axis=1), jnp.concatenate([dbg, dbu], axis=1)) + tuple(res[6:])


CONV_HALO = 32
CONV_ROWS = 512
CONV_CHUNK = 64


def _glu_window(s, p_ref, halo_ref, first, D, groups, lw, H, tr):
    for g, ls in enumerate(groups):
        gs = pl.ds(D + g * lw, lw)
        s[g, 0:H, :] = jnp.where(first, 0.0, halo_ref[:, ls].astype(F32) * _sigmoid(halo_ref[:, gs].astype(F32)))
        s[g, H:H + tr, :] = p_ref[:, ls].astype(F32) * _sigmoid(p_ref[:, gs].astype(F32))


def _convmix_fwd(p, dw, dwb, lng, lnb, name, side=None):
    T, D2 = p.shape
    D = D2 // 2
    K = dw.shape[0]
    H = CONV_HALO
    tr = _tile(T, CONV_ROWS, H)
    rb = tr // H
    rc = _tile(tr, CONV_CHUNK, 16)
    lw, groups = _lane_groups(D)

    def body(pc_ref, ph_ref, w_ref, wb_ref, lng_ref, lnb_ref, zc_ref, s_ref, zg_s):
        first = pl.program_id(0) == 0
        _glu_window(zg_s, pc_ref, ph_ref, first, D, groups, lw, H, tr)

        def chunk(ci, carry):
            r0 = pl.multiple_of(ci * rc, rc)
            for g, ls in enumerate(groups):
                acc = wb_ref[:, ls] + w_ref[K - 1:K, ls] * zg_s[g, pl.ds(pl.multiple_of(r0 + H, 8), rc), :]
                for k in range(K - 1):
                    acc = acc + w_ref[k:k + 1, ls] * zg_s[g, pl.ds(r0 + H - (K - 1 - k), rc), :]
                zc_ref[pl.ds(r0, rc), ls] = acc
            return carry

        lax.fori_loop(0, tr // rc, chunk, 0)
        acc = zc_ref[...]
        mu = jnp.mean(acc, axis=-1, keepdims=True)
        xc = acc - mu
        rstd = lax.rsqrt(jnp.mean(xc * xc, axis=-1, keepdims=True) + NORM_EPS)
        ln = xc * rstd * lng_ref[...] + lnb_ref[...]
        s_ref[...] = (ln * _sigmoid(ln)).astype(BF16)

    in_specs = [pl.BlockSpec((tr, D2), lambda t: (t, 0)),
                pl.BlockSpec((H, D2), lambda t: (jnp.maximum(t * rb - 1, 0), 0)),
                pl.BlockSpec((K, D), lambda t: (0, 0)), _vec_spec(D), _vec_spec(D), _vec_spec(D)]
    row = pl.BlockSpec((tr, D), lambda t: (t, 0))
    return _call(body, name, (T // tr,), in_specs, (row, row),
                 (jax.ShapeDtypeStruct((T, D), F32), jax.ShapeDtypeStruct((T, D), BF16)), (p, p, dw, dwb, lng, lnb),
                 scratch=[pltpu.VMEM((len(groups), H + tr, lw), F32)], side=side)


def _convmix_bwd(ds, zc, p, dw, lng, lnb, name, side=None):
    T, D = zc.shape
    D2 = 2 * D
    K = dw.shape[0]
    H = CONV_HALO
    tr = _tile(T, CONV_ROWS, H)
    rb = tr // H
    nt = T // tr
    ext = tr + H
    rc = _tile(tr, CONV_CHUNK, 16)
    lw, groups = _lane_groups(D)

    def body(dsc_ref, dsn_ref, zcc_ref, zcn_ref, pp_ref, pc_ref, w_ref, lng_ref, lnb_ref,
             dp_ref, dbin_ref, ddw_ref, ddwb_ref, dlng_ref, dlnb_ref, zg_s, dzc_s, acc_s):
        t = pl.program_id(0)
        first = t == 0
        last = t == nt - 1

        @pl.when(first)
        def _():
            for r in (dbin_ref, ddw_ref, ddwb_ref, dlng_ref, dlnb_ref):
                r[...] = jnp.zeros_like(r)

        dsv = jnp.concatenate([dsc_ref[...].astype(F32), jnp.where(last, 0.0, dsn_ref[...].astype(F32))], axis=0)
        zcv = jnp.concatenate([zcc_ref[...], zcn_ref[...]], axis=0)
        lg = lng_ref[...]
        mu = jnp.mean(zcv, axis=-1, keepdims=True)
        xc = zcv - mu
        rstd = lax.rsqrt(jnp.mean(xc * xc, axis=-1, keepdims=True) + NORM_EPS)
        xh = xc * rstd
        ln = xh * lg + lnb_ref[...]
        sg = _sigmoid(ln)
        dln = dsv * (sg * (1.0 + ln * (1.0 - sg)))
        dlnc = _rows(dln, 0, tr)
        dlnb_ref[...] += jnp.sum(dlnc, axis=0, keepdims=True)
        dlng_ref[...] += jnp.sum(dlnc * _rows(xh, 0, tr), axis=0, keepdims=True)
        dxh = dln * lg
        dzc = rstd * (dxh - jnp.mean(dxh, axis=-1, keepdims=True) - xh * jnp.mean(dxh * xh, axis=-1, keepdims=True))
        dzcc = _rows(dzc, 0, tr)
        ddwb_ref[...] += jnp.sum(dzcc, axis=0, keepdims=True)

        for g in range(len(groups)):
            dzc_s[g] = dzc[:, g * lw:(g + 1) * lw]
        _glu_window(zg_s, pc_ref, pp_ref, first, D, groups, lw, H, tr)
        acc_s[...] = jnp.zeros_like(acc_s)

        def fold(v):
            return jnp.sum(v.reshape(rc // 8, 8, lw), axis=0)

        def chunk(ci, carry):
            r0 = pl.multiple_of(ci * rc, rc)
            for g, ls in enumerate(groups):
                gs = pl.ds(D + g * lw, lw)
                cur = dzc_s[g, pl.ds(r0, rc), :]
                dzg = w_ref[K - 1:K, ls] * cur
                for k in range(K):
                    acc_s[k, g] += fold(cur * zg_s[g, pl.ds(r0 + H - (K - 1 - k), rc), :])
                    if k < K - 1:
                        dzg = dzg + w_ref[k:k + 1, ls] * dzc_s[g, pl.ds(r0 + (K - 1 - k), rc), :]
                ac = pc_ref[pl.ds(r0, rc), ls].astype(F32)
                sgg = _sigmoid(pc_ref[pl.ds(r0, rc), gs].astype(F32))
                dpa = dzg * sgg
                dpg = dzg * ac * sgg * (1.0 - sgg)
                dp_ref[pl.ds(r0, rc), ls] = dpa.astype(BF16)
                dp_ref[pl.ds(r0, rc), gs] = dpg.astype(BF16)
                acc_s[K, g] += fold(dpa)
                acc_s[K + 1, g] += fold(dpg)
            return carry

        lax.fori_loop(0, tr // rc, chunk, 0)
        for g, ls in enumerate(groups):
            ddw_ref[:, ls] += jnp.concatenate([jnp.sum(acc_s[k, g], axis=0, keepdims=True) for k in range(K)], axis=0)
            dbin_ref[:, ls] += jnp.sum(acc_s[K, g], axis=0, keepdims=True)
            dbin_ref[:, pl.ds(D + g * lw, lw)] += jnp.sum(acc_s[K + 1, g], axis=0, keepdims=True)

    def prev(t):
        return jnp.maximum(t * rb - 1, 0)

    def nxt(t):
        return jnp.minimum((t + 1) * rb, T // H - 1)

    in_specs = [pl.BlockSpec((tr, D), lambda t: (t, 0)), pl.BlockSpec((H, D), lambda t: (nxt(t), 0)),
                pl.BlockSpec((tr, D), lambda t: (t, 0)), pl.BlockSpec((H, D), lambda t: (nxt(t), 0)),
                pl.BlockSpec((H, D2), lambda t: (prev(t), 0)), pl.BlockSpec((tr, D2), lambda t: (t, 0)),
                pl.BlockSpec((K, D), lambda t: (0, 0)), _vec_spec(D), _vec_spec(D)]
    out_shape = (jax.ShapeDtypeStruct((T, D2), BF16), jax.ShapeDtypeStruct((1, D2), F32),
                 jax.ShapeDtypeStruct((K, D), F32), jax.ShapeDtypeStruct((1, D), F32),
                 jax.ShapeDtypeStruct((1, D), F32), jax.ShapeDtypeStruct((1, D), F32))
    out_specs = (pl.BlockSpec((tr, D2), lambda t: (t, 0)), _vec_spec(D2), pl.BlockSpec((K, D), lambda t: (0, 0)),
                 _vec_spec(D), _vec_spec(D), _vec_spec(D))
    ng = len(groups)
    scratch = [pltpu.VMEM((ng, H + tr, lw), F32), pltpu.VMEM((ng, ext, lw), F32), pltpu.VMEM((K + 2, ng, 8, lw), F32)]
    return _call(body, name, (nt,), in_specs, out_specs, out_shape, (ds, ds, zc, zc, p, p, dw, lng, lnb),
                 scratch=scratch, side=side)


_INV_SQRT2 = 1.0 / math.sqrt(2.0)
_INV_SQRT2PI = 1.0 / math.sqrt(2.0 * math.pi)


def _gelu(x):
    return 0.5 * x * (1.0 + lax.erf(x * _INV_SQRT2))


def _gelu_grad(x):
    return 0.5 * (1.0 + lax.erf(x * _INV_SQRT2)) + x * jnp.exp(-0.5 * x * x) * _INV_SQRT2PI


def _tril_mask(n, transposed=False):
    r = lax.broadcasted_iota(jnp.int32, (n, n), 0)
    c = lax.broadcasted_iota(jnp.int32, (n, n), 1)
    return (r <= c) if transposed else (r >= c)


def _sgu_fwd(p, lng, lnb, ws, bs_t, name):
    T, S2 = p.shape
    S = S2 // 2
    G, C, _ = ws.shape
    gd = S // G
    cpt = 2 if (T // C) % 2 == 0 else 1
    tr = C * cpt

    def body(p_ref, lng_ref, lnb_ref, ws_ref, bst_ref, m_ref):
        mask = _tril_mask(C)
        lg, lb = lng_ref[...], lnb_ref[...]
        bst = bst_ref[...]
        for ci in range(cpt):
            pc = p_ref[ci * C:(ci + 1) * C, :].astype(F32)
            z = _gelu(pc)
            u, v = z[:, :S], z[:, S:]
            mu = jnp.mean(v, axis=-1, keepdims=True)
            xc = v - mu
            rstd = lax.rsqrt(jnp.mean(xc * xc, axis=-1, keepdims=True) + NORM_EPS)
            vn = (xc * rstd * lg + lb).astype(BF16)
            outs = []
            for g in range(G):
                wm = jnp.where(mask, ws_ref[g], 0.0).astype(BF16)
                vs = _dot(wm, vn[:, g * gd:(g + 1) * gd], 1, 0) + bst[:, g:g + 1]
                outs.append(u[:, g * gd:(g + 1) * gd] * vs)
            m_ref[ci * C:(ci + 1) * C, :] = jnp.concatenate(outs, axis=1).astype(BF16)

    in_specs = [pl.BlockSpec((tr, S2), lambda t: (t, 0)), _vec_spec(S), _vec_spec(S),
                pl.BlockSpec((G, C, C), lambda t: (0, 0, 0)), pl.BlockSpec((C, G), lambda t: (0, 0))]
    return _call(body, name, (T // tr,), in_specs, pl.BlockSpec((tr, S), lambda t: (t, 0)),
                 jax.ShapeDtypeStruct((T, S), BF16), (p, lng, lnb, ws, bs_t))


def _sgu_bwd(p, dm, lng, lnb, ws, bs_t, name, side=None):
    T, S2 = p.shape
    S = S2 // 2
    G, C, _ = ws.shape
    gd = S // G
    nt = T // C

    def body(p_ref, dm_ref, lng_ref, lnb_ref, ws_ref, bst_ref, dp_ref, dbin_ref, dlng_ref, dlnb_ref, dws_ref, dbst_ref):
        @pl.when(pl.program_id(0) == 0)
        def _():
            for r in (dbin_ref, dlng_ref, dlnb_ref, dws_ref, dbst_ref):
                r[...] = jnp.zeros_like(r)

        mask = _tril_mask(C)
        lg, lb = lng_ref[...], lnb_ref[...]
        bst = bst_ref[...]
        pc = p_ref[...].astype(F32)
        dmv = dm_ref[...].astype(F32)
        z = _gelu(pc)
        u, v = z[:, :S], z[:, S:]
        mu = jnp.mean(v, axis=-1, keepdims=True)
        xc = v - mu
        rstd = lax.rsqrt(jnp.mean(xc * xc, axis=-1, keepdims=True) + NORM_EPS)
        vh = xc * rstd
        vn = (vh * lg + lb).astype(BF16)
        dus, dvns, dbcols = [], [], []
        for g in range(G):
            sl = slice(g * gd, (g + 1) * gd)
            wm = jnp.where(mask, ws_ref[g], 0.0).astype(BF16)
            vs = _dot(wm, vn[:, sl], 1, 0) + bst[:, g:g + 1]
            dmg = dmv[:, sl]
            dus.append(dmg * vs)
            dvs = dmg * u[:, sl]
            dbcols.append(jnp.sum(dvs, axis=-1, keepdims=True))
            dvsb = dvs.astype(BF16)
            dws_ref[g] += jnp.where(mask, _dot(dvsb, vn[:, sl], 1, 1), 0.0)
            dvns.append(_dot(wm, dvsb, 0, 0))
        dbst_ref[...] += jnp.concatenate(dbcols, axis=1)
        dvn = jnp.concatenate(dvns, axis=1)
        dlnb_ref[...] += jnp.sum(dvn, axis=0, keepdims=True)
        dlng_ref[...] += jnp.sum(dvn * vh, axis=0, keepdims=True)
        dvh = dvn * lg
        dv = rstd * (dvh - jnp.mean(dvh, axis=-1, keepdims=True) - vh * jnp.mean(dvh * vh, axis=-1, keepdims=True))
        dz = jnp.concatenate([jnp.concatenate(dus, axis=1), dv], axis=1)
        dpv = dz * _gelu_grad(pc)
        dp_ref[...] = dpv.astype(BF16)
        dbin_ref[...] += jnp.sum(dpv, axis=0, keepdims=True)

    in_specs = [pl.BlockSpec((C, S2), lambda t: (t, 0)), pl.BlockSpec((C, S), lambda t: (t, 0)),
                _vec_spec(S), _vec_spec(S), pl.BlockSpec((G, C, C), lambda t: (0, 0, 0)),
                pl.BlockSpec((C, G), lambda t: (0, 0))]
    out_shape = (jax.ShapeDtypeStruct((T, S2), BF16), jax.ShapeDtypeStruct((1, S2), F32),
                 jax.ShapeDtypeStruct((1, S), F32), jax.ShapeDtypeStruct((1, S), F32),
                 jax.ShapeDtypeStruct((G, C, C), F32), jax.ShapeDtypeStruct((C, G), F32))
    out_specs = (pl.BlockSpec((C, S2), lambda t: (t, 0)), _vec_spec(S2), _vec_spec(S), _vec_spec(S),
                 pl.BlockSpec((G, C, C), lambda t: (0, 0, 0)), pl.BlockSpec((C, G), lambda t: (0, 0)))
    return _call(body, name, (nt,), in_specs, out_specs, out_shape, (p, dm, lng, lnb, ws, bs_t), side=side)


def _alibi_slope(h, n_heads):
    return 2.0 ** (-8.0 * (h + 1) / n_heads)


def _attn_bias(n_heads):
    B = BLOCK
    qi = jnp.arange(B)[:, None]
    kj = jnp.arange(2 * B)[None, :]
    dist = qi + B - kj
    band = (dist >= 0) & (dist < B)
    slopes = jnp.array([_alibi_slope(h, n_heads) for h in range(n_heads)], F32)
    out = []
    for valid in (band & (kj >= B), band):
        b = jnp.where(valid[None], -slopes[:, None, None] * dist.astype(F32)[None], NEG_INF)
        b = jnp.where(kj[None] == 0, 0.0, b)
        out.append(b.reshape(n_heads // 2, 2, B, 2 * B).transpose(0, 2, 1, 3).reshape(n_heads // 2, B, 4 * B))
    return jnp.stack(out)


def _sink_rows(sinks):
    return jnp.pad(sinks.astype(F32)[:, None], ((0, 0), (0, BLOCK - 1)))


def _block_diag2(x):
    z = jnp.zeros_like(x)
    return jnp.concatenate([jnp.concatenate([x, z], axis=1), jnp.concatenate([z, x], axis=1)], axis=0)


def _attn_core(q_ref, kp_ref, kc_ref, vp_ref, vc_ref, sink_ref, bias_ref, n_heads):
    B, hd = BLOCK, HEAD_DIM
    ppg = n_heads // N_KV // 2
    pairs = range(n_heads // 2)
    row0 = lax.broadcasted_iota(jnp.int32, (2 * B, N_KV * hd), 0) == 0
    k2 = jnp.concatenate([kp_ref[...], kc_ref[...]], axis=0)
    v2 = jnp.concatenate([vp_ref[...], vc_ref[...]], axis=0)
    k2 = jnp.where(row0, jnp.zeros_like(k2), k2)
    v2 = jnp.where(row0, jnp.zeros_like(v2), v2)
    scale = hd ** -0.5
    kks = [_block_diag2(k2[:, kv * hd:(kv + 1) * hd] * scale) for kv in range(N_KV)]
    vvs = [_block_diag2(v2[:, kv * hd:(kv + 1) * hd]) for kv in range(N_KV)]
    first_rows = lax.broadcasted_iota(jnp.int32, (4 * B, 4 * hd), 0) < 2 * B
    first_lanes = lax.broadcasted_iota(jnp.int32, (4 * B, 4 * hd), 1) < 2 * hd
    ones = (first_rows == first_lanes).astype(BF16)
    vxs = [jnp.concatenate([vv, ones], axis=1) for vv in vvs]
    ss = [_dot(q_ref[:, pr * 2 * hd:(pr + 1) * 2 * hd], kks[pr // ppg], 1, 1) + bias_ref[pr] for pr in pairs]
    halves = []
    for h in range(n_heads):
        s = ss[h // 2][:, (h % 2) * 2 * B:(h % 2 + 1) * 2 * B]
        halves.append(jnp.concatenate([s[:, :B] + sink_ref[h:h + 1, :], s[:, B:]], axis=1))
    mxs = [jnp.max(s, axis=-1, keepdims=True) for s in halves]
    es = [jnp.exp(s - m) for s, m in zip(halves, mxs)]
    rs = [_dot(jnp.concatenate([es[2 * pr], es[2 * pr + 1]], axis=1).astype(BF16), vxs[pr // ppg], 1, 0) for pr in pairs]
    return kks, vvs, es, rs


def _attn_fwd(qkv, sink_rows, bias, n_heads, name, side=None):
    T = qkv.shape[0]
    B = BLOCK
    hd = HEAD_DIM
    HQ = n_heads * hd
    KVW = N_KV * hd
    assert (n_heads // N_KV) % 2 == 0, "heads are processed in pairs that share a key/value head"
    nb = T // B
    kcol = HQ // KVW

    def body(q_ref, kp_ref, kc_ref, vp_ref, vc_ref, sink_ref, bias_ref, o_ref):
        _, _, _, rs = _attn_core(q_ref, kp_ref, kc_ref, vp_ref, vc_ref, sink_ref, bias_ref, n_heads)
        low = lax.broadcasted_iota(jnp.int32, (B, 2 * hd), 1) < hd
        for pr, r in enumerate(rs):
            inv = jnp.where(low, 1.0 / r[:, 2 * hd:4 * hd], 1.0 / r[:, 4 * hd:])
            o_ref[:, pr * 2 * hd:(pr + 1) * 2 * hd] = (r[:, :2 * hd] * inv).astype(BF16)

    def prev(n):
        return jnp.maximum(n - 1, 0)

    in_specs = [pl.BlockSpec((B, HQ), lambda n: (n, 0)),
                pl.BlockSpec((B, KVW), lambda n: (prev(n), kcol)), pl.BlockSpec((B, KVW), lambda n: (n, kcol)),
                pl.BlockSpec((B, KVW), lambda n: (prev(n), kcol + 1)), pl.BlockSpec((B, KVW), lambda n: (n, kcol + 1)),
                pl.BlockSpec((n_heads, B), lambda n: (0, 0)),
                pl.BlockSpec((None, n_heads // 2, B, 4 * B), lambda n: (jnp.minimum(n, 1), 0, 0, 0))]
    return _call(body, name, (nb,), in_specs, pl.BlockSpec((B, HQ), lambda n: (n, 0)),
                 jax.ShapeDtypeStruct((T, HQ), BF16), (qkv, qkv, qkv, qkv, qkv, sink_rows, bias), side=side)


def _attn_bwd(qkv, do, sink_rows, bias, n_heads, name, side=None):
    T = qkv.shape[0]
    B = BLOCK
    hd = HEAD_DIM
    HQ = n_heads * hd
    KVW = N_KV * hd
    group = n_heads // N_KV
    assert group % 2 == 0, "heads are processed in pairs that share a key/value head"
    ppg = group // 2
    nb = T // B
    kcol = HQ // KVW
    scale = hd ** -0.5

    def body(q_ref, kp_ref, kc_ref, vp_ref, vc_ref, do_ref, sink_ref, bias_ref,
             dq_ref, dkv_ref, dbq_ref, dbk_ref, dbv_ref, dsink_ref, ck_ref, cv_ref):
        n = pl.program_id(0)
        first = n == 0

        @pl.when(first)
        def _():
            for r in (dbq_ref, dbk_ref, dbv_ref, dsink_ref, ck_ref, cv_ref):
                r[...] = jnp.zeros_like(r)

        @pl.when(n < nb)
        def _():
            pairs = range(n_heads // 2)
            heads = range(n_heads)
            lanes = [slice(pr * 2 * hd, (pr + 1) * 2 * hd) for pr in pairs]
            kks, vvs, es, rs = _attn_core(q_ref, kp_ref, kc_ref, vp_ref, vc_ref, sink_ref, bias_ref, n_heads)
            low = lax.broadcasted_iota(jnp.int32, (B, 2 * hd), 1) < hd
            slot0 = lax.broadcasted_iota(jnp.int32, (B, B), 1) == 0
            lane_head = lax.broadcasted_iota(jnp.int32, (2 * hd, 4 * hd), 0) < hd
            out_head = lax.broadcasted_iota(jnp.int32, (2 * hd, 4 * hd), 1) < 2 * hd
            ones2 = (lane_head == out_head).astype(BF16)
            invs = [1.0 / rs[h // 2][:, (2 + 2 * (h % 2)) * hd:(4 + 2 * (h % 2)) * hd] for h in heads]
            dops = [do_ref[:, lanes[pr]] for pr in pairs]
            qps = [q_ref[:, lanes[pr]] for pr in pairs]
            dos = [dops[pr].astype(F32) * rs[pr][:, :2 * hd] * jnp.where(low, invs[2 * pr], invs[2 * pr + 1])
                   for pr in pairs]
            his = [t.astype(BF16) for t in dos]
            los = [(t - hi.astype(F32)).astype(BF16) for t, hi in zip(dos, his)]
            dsums = [_dot(hi, ones2, 1, 0) + _dot(lo, ones2, 1, 0) for hi, lo in zip(his, los)]
            dps = [_dot(dops[pr], vvs[pr // ppg], 1, 1) for pr in pairs]
            wide = lambda a: jnp.concatenate([a, a], axis=1)
            p_s = [es[h] * wide(invs[h]) for h in heads]
            ds_s = [p_s[h] * (dps[h // 2][:, (h % 2) * 2 * B:(h % 2 + 1) * 2 * B]
                              - wide(dsums[h // 2][:, (h % 2) * 2 * hd:(h % 2 + 1) * 2 * hd])) for h in heads]
            for h in heads:
                dsink_ref[h:h + 1, :] += jnp.sum(ds_s[h][:, :B], axis=0, keepdims=True)

            def without_slot0(a):
                return jnp.concatenate([jnp.where(slot0, 0.0, a[:, :B]), a[:, B:]], axis=1).astype(BF16)

            ds_b = [without_slot0(a) for a in ds_s]
            p_b = [without_slot0(a) for a in p_s]
            dqs = [_dot(jnp.concatenate([ds_b[2 * pr], ds_b[2 * pr + 1]], axis=1), kks[pr // ppg], 1, 0) for pr in pairs]
            for pr in pairs:
                dq_ref[:, lanes[pr]] = dqs[pr].astype(BF16)
                dbq_ref[:, lanes[pr]] += jnp.sum(dqs[pr], axis=0, keepdims=True)
            keeps = [low, jnp.logical_not(low)]
            qms = [jnp.where(keeps[h % 2], qps[h // 2], jnp.zeros_like(qps[0])) for h in heads]
            dms = [jnp.where(keeps[h % 2], dops[h // 2], jnp.zeros_like(dops[0])) for h in heads]
            dkh = [_dot(ds_b[h], qms[h], 0, 0) for h in heads]
            dvh = [_dot(p_b[h], dms[h], 0, 0) for h in heads]
            dks, dvs = [], []
            for kv in range(N_KV):
                tk = functools.reduce(jnp.add, dkh[kv * group:(kv + 1) * group])
                tv = functools.reduce(jnp.add, dvh[kv * group:(kv + 1) * group])
                dks.append((tk[:, :hd] + tk[:, hd:]) * scale)
                dvs.append(tv[:, :hd] + tv[:, hd:])
            dk2 = jnp.concatenate(dks, axis=1)
            dv2 = jnp.concatenate(dvs, axis=1)
            dbk_ref[...] += jnp.sum(dk2, axis=0, keepdims=True)
            dbv_ref[...] += jnp.sum(dv2, axis=0, keepdims=True)
            dkv_ref[:, :KVW] = (ck_ref[...] + dk2[:B]).astype(BF16)
            dkv_ref[:, KVW:] = (cv_ref[...] + dv2[:B]).astype(BF16)
            ck_ref[...] = dk2[B:]
            cv_ref[...] = dv2[B:]

        @pl.when(n == nb)
        def _():
            dkv_ref[:, :KVW] = ck_ref[...].astype(BF16)
            dkv_ref[:, KVW:] = cv_ref[...].astype(BF16)

    def cur(n):
        return jnp.minimum(n, nb - 1)

    def prev(n):
        return jnp.maximum(cur(n) - 1, 0)

    def outp(n):
        return jnp.maximum(n - 1, 0)

    in_specs = [pl.BlockSpec((B, HQ), lambda n: (cur(n), 0)),
                pl.BlockSpec((B, KVW), lambda n: (prev(n), kcol)), pl.BlockSpec((B, KVW), lambda n: (cur(n), kcol)),
                pl.BlockSpec((B, KVW), lambda n: (prev(n), kcol + 1)), pl.BlockSpec((B, KVW), lambda n: (cur(n), kcol + 1)),
                pl.BlockSpec((B, HQ), lambda n: (cur(n), 0)), pl.BlockSpec((n_heads, B), lambda n: (0, 0)),
                pl.BlockSpec((None, n_heads // 2, B, 4 * B), lambda n: (jnp.minimum(n, 1), 0, 0, 0))]
    out_shape = (jax.ShapeDtypeStruct((T, HQ), BF16), jax.ShapeDtypeStruct((T, 2 * KVW), BF16),
                 jax.ShapeDtypeStruct((1, HQ), F32), jax.ShapeDtypeStruct((1, KVW), F32),
                 jax.ShapeDtypeStruct((1, KVW), F32), jax.ShapeDtypeStruct((n_heads, B), F32))
    out_specs = (pl.BlockSpec((B, HQ), lambda n: (cur(n), 0)), pl.BlockSpec((B, 2 * KVW), lambda n: (outp(n), 0)),
                 _vec_spec(HQ), _vec_spec(KVW), _vec_spec(KVW), pl.BlockSpec((n_heads, B), lambda n: (0, 0)))
    return _call(body, name, (nb + 1,), in_specs, out_specs, out_shape, (qkv, qkv, qkv, qkv, qkv, do, sink_rows, bias),
                 scratch=[pltpu.VMEM((B, KVW), F32), pltpu.VMEM((B, KVW), F32)], side=side)


def _sum8(r, name):
    _, R, C = r.shape
    tr = _tile(R, 512, 16)

    def body(r_ref, o_ref):
        acc = r_ref[0].astype(F32)
        for d in range(1, N_DEV):
            acc = acc + r_ref[d].astype(F32)
        o_ref[...] = acc

    return _call(body, name, (R // tr,), [pl.BlockSpec((N_DEV, tr, C), lambda i: (0, i, 0))],
                 pl.BlockSpec((tr, C), lambda i: (i, 0)), jax.ShapeDtypeStruct((R, C), F32), (r,))


def _adamw(w, g, m, v, name):
    R, C = w.shape
    tr = _tile(R, 512, 8)
    c1 = 1.0 - ADAM_B1 ** ADAM_STEP
    c2 = 1.0 - ADAM_B2 ** ADAM_STEP

    def body(w_ref, g_ref, m_ref, v_ref, d_ref, nm_ref, nv_ref):
        gv = g_ref[...]
        nm = ADAM_B1 * m_ref[...] + (1.0 - ADAM_B1) * gv
        nv = ADAM_B2 * v_ref[...] + (1.0 - ADAM_B2) * (gv * gv)
        nm_ref[...] = nm
        nv_ref[...] = nv
        d_ref[...] = -ADAM_LR * ((nm / c1) / (jnp.sqrt(nv / c2) + ADAM_EPS) + ADAM_WD * w_ref[...])

    spec = pl.BlockSpec((tr, C), lambda i: (i, 0))
    shp = jax.ShapeDtypeStruct((R, C), F32)
    return _call(body, name, (R // tr,), [spec] * 4, (spec,) * 3, (shp, shp, shp), (w, g, m, v))


def _adamw_nd(w, g, m, v, name):
    shape = w.shape
    c = shape[-1]
    f = lambda a: a.reshape(-1, c)
    d, nm, nv = _adamw(f(w), f(g), f(m), f(v), name)
    return d.reshape(shape), nm.reshape(shape), nv.reshape(shape)


def _pack(arrs, width):
    flat = jnp.concatenate([a.reshape(-1).astype(F32) for a in arrs])
    n = flat.shape[0]
    quantum = 8 * width
    total = -(-n // quantum) * quantum
    return jnp.pad(flat, (0, total - n)).reshape(-1, width)


def _unpack(flat, shapes):
    out, off = [], 0
    for s in shapes:
        n = math.prod(s)
        out.append(flat[off:off + n].reshape(s))
        off += n
    return out


def kernel(x, c, norm1_g, norm2_g, ada_w, ada_b, attn_wqkv, attn_bqkv, attn_sinks, attn_wo, attn_bo, conv_w_in, conv_b_in, conv_dw, conv_dw_b, conv_ln_g, conv_ln_b, conv_w_out, conv_b_out, sgu_w_in, sgu_b_in, sgu_ln_g, sgu_ln_b, sgu_ws, sgu_bs, sgu_w_out, sgu_b_out, ffn_w_in, ffn_dw, ffn_dw_b, ffn_w_out, final_g, loss_target, m_norm1_g, m_norm2_g, m_ada_w, m_ada_b, m_attn_wqkv, m_attn_bqkv, m_attn_sinks, m_attn_wo, m_attn_bo, m_conv_w_in, m_conv_b_in, m_conv_dw, m_conv_dw_b, m_conv_ln_g, m_conv_ln_b, m_conv_w_out, m_conv_b_out, m_sgu_w_in, m_sgu_b_in, m_sgu_ln_g, m_sgu_ln_b, m_sgu_ws, m_sgu_bs, m_sgu_w_out, m_sgu_b_out, m_ffn_w_in, m_ffn_dw, m_ffn_dw_b, m_ffn_w_out, m_final_g, v_norm1_g, v_norm2_g, v_ada_w, v_ada_b, v_attn_wqkv, v_attn_bqkv, v_attn_sinks, v_attn_wo, v_attn_bo, v_conv_w_in, v_conv_b_in, v_conv_dw, v_conv_dw_b, v_conv_ln_g, v_conv_ln_b, v_conv_w_out, v_conv_b_out, v_sgu_w_in, v_sgu_b_in, v_sgu_ln_g, v_sgu_ln_b, v_sgu_ws, v_sgu_bs, v_sgu_w_out, v_sgu_b_out, v_ffn_w_in, v_ffn_dw, v_ffn_dw_b, v_ffn_w_out, v_final_g):
    W = dict(norm1_g=norm1_g, norm2_g=norm2_g, ada_w=ada_w, ada_b=ada_b, attn_wqkv=attn_wqkv, attn_bqkv=attn_bqkv, attn_sinks=attn_sinks, attn_wo=attn_wo, attn_bo=attn_bo, conv_w_in=conv_w_in, conv_b_in=conv_b_in, conv_dw=conv_dw, conv_dw_b=conv_dw_b, conv_ln_g=conv_ln_g, conv_ln_b=conv_ln_b, conv_w_out=conv_w_out, conv_b_out=conv_b_out, sgu_w_in=sgu_w_in, sgu_b_in=sgu_b_in, sgu_ln_g=sgu_ln_g, sgu_ln_b=sgu_ln_b, sgu_ws=sgu_ws, sgu_bs=sgu_bs, sgu_w_out=sgu_w_out, sgu_b_out=sgu_b_out, ffn_w_in=ffn_w_in, ffn_dw=ffn_dw, ffn_dw_b=ffn_dw_b, ffn_w_out=ffn_w_out, final_g=final_g)
    MOM = dict(norm1_g=m_norm1_g, norm2_g=m_norm2_g, ada_w=m_ada_w, ada_b=m_ada_b, attn_wqkv=m_attn_wqkv, attn_bqkv=m_attn_bqkv, attn_sinks=m_attn_sinks, attn_wo=m_attn_wo, attn_bo=m_attn_bo, conv_w_in=m_conv_w_in, conv_b_in=m_conv_b_in, conv_dw=m_conv_dw, conv_dw_b=m_conv_dw_b, conv_ln_g=m_conv_ln_g, conv_ln_b=m_conv_ln_b, conv_w_out=m_conv_w_out, conv_b_out=m_conv_b_out, sgu_w_in=m_sgu_w_in, sgu_b_in=m_sgu_b_in, sgu_ln_g=m_sgu_ln_g, sgu_ln_b=m_sgu_ln_b, sgu_ws=m_sgu_ws, sgu_bs=m_sgu_bs, sgu_w_out=m_sgu_w_out, sgu_b_out=m_sgu_b_out, ffn_w_in=m_ffn_w_in, ffn_dw=m_ffn_dw, ffn_dw_b=m_ffn_dw_b, ffn_w_out=m_ffn_w_out, final_g=m_final_g)
    VAR = dict(norm1_g=v_norm1_g, norm2_g=v_norm2_g, ada_w=v_ada_w, ada_b=v_ada_b, attn_wqkv=v_attn_wqkv, attn_bqkv=v_attn_bqkv, attn_sinks=v_attn_sinks, attn_wo=v_attn_wo, attn_bo=v_attn_bo, conv_w_in=v_conv_w_in, conv_b_in=v_conv_b_in, conv_dw=v_conv_dw, conv_dw_b=v_conv_dw_b, conv_ln_g=v_conv_ln_g, conv_ln_b=v_conv_ln_b, conv_w_out=v_conv_w_out, conv_b_out=v_conv_b_out, sgu_w_in=v_sgu_w_in, sgu_b_in=v_sgu_b_in, sgu_ln_g=v_sgu_ln_g, sgu_ln_b=v_sgu_ln_b, sgu_ws=v_sgu_ws, sgu_bs=v_sgu_bs, sgu_w_out=v_sgu_w_out, sgu_b_out=v_sgu_b_out, ffn_w_in=v_ffn_w_in, ffn_dw=v_ffn_dw, ffn_dw_b=v_ffn_dw_b, ffn_w_out=v_ffn_w_out, final_g=v_final_g)
    ORDER = list(W)

    _, T, D = x.shape
    depth = norm1_g.shape[0]
    n_heads = D // HEAD_DIM
    me = 4 * lax.axis_index("x") + 2 * lax.axis_index("y") + lax.axis_index("c")
    x0 = x.reshape(T, D)
    tgt = loss_target.reshape(T, D)

    small_sharded = ["attn_bqkv", "attn_bo", "conv_dw", "sgu_b_in", "sgu_ln_g", "sgu_ln_b", "sgu_b_out", "ffn_dw"]
    s_in = [c] + [W[n] for n in small_sharded]
    s_shapes = [a.shape for a in s_in]
    gathered_small, first_landed = _exchange("gather", [_pack(s_in, 128), attn_wqkv[0].T.astype(BF16)], "gather_first")
    gathered_small = gathered_small.reshape(N_DEV, -1)
    s_offs = [sum(math.prod(s) for s in s_shapes[:i]) for i in range(len(s_shapes))]

    def full(idx):
        shp = s_shapes[idx]
        a = gathered_small[:, s_offs[idx]:s_offs[idx] + math.prod(shp)].reshape((N_DEV,) + shp)
        return jnp.moveaxis(a, 0, -2).reshape(shp[:-1] + (N_DEV * shp[-1],))

    c_all = full(0).reshape(N_DEV, D)
    F_small = {n: full(1 + i) for i, n in enumerate(small_sharded)}
    attn_bias = _attn_bias(n_heads)

    c_act = c_all * jax.nn.sigmoid(c_all)
    c_pad = jnp.pad(c_act, ((0, 8), (0, 0))).astype(BF16)
    n_ada = ada_w.shape[-1]
    ada_cols = lax.dynamic_slice_in_dim(ada_b, me * n_ada, n_ada, axis=1)
    mod_loc = jnp.stack([_mm(c_pad, ada_w[i].astype(BF16), "nn", "ada_mod", bias=ada_cols[i:i + 1])
                         for i in range(depth)])
    mod_all = _all_gather(mod_loc, "gather_mod")
    mod_mine = lax.dynamic_index_in_dim(mod_all, me, axis=2, keepdims=False)
    mod = jnp.transpose(mod_mine, (1, 0, 2)).reshape(depth, 6, 1, D)

    mixer_names = {0: ("attn_wqkv", "attn_wo"), 1: ("conv_w_in", "conv_w_out"), 2: ("sgu_w_in", "sgu_w_out")}

    def piece(part, i, which):
        w_in, w_out = ("ffn_w_in", "ffn_w_out") if part == "ffn" else mixer_names[i % 3]
        l = i if part == "ffn" else i // 3
        return (w_in, l, W[w_in].shape[-1]) if which == "in" else (w_out, l, W[w_out].shape[1])

    def both(part, i):
        return [(part, i, "in"), (part, i, "out")] if 0 <= i < depth else []

    def local_weight(key):
        n, l, _ = piece(*key)
        return (W[n][l].T if key[2] == "in" else W[n][l]).astype(BF16)

    WF = {}

    def land_weights(keys, gathered):
        for key, g in zip(keys, gathered):
            n, l, r = piece(*key)
            WF[(n, l)] = g.reshape(N_DEV * r, D)

    def with_gather(keys, run):
        if not keys:
            return run(None)
        *outs, landed = run(("gather", [local_weight(k) for k in keys]))
        land_weights(keys, landed)
        return outs[0] if len(outs) == 1 else tuple(outs)

    def vec(a):
        return a.reshape(1, -1)

    land_weights(both("mix", 0)[:1], [first_landed])

    saved = []
    xs, y_prev, gate_prev = x0, None, None
    for i in range(depth):
        sh1, sc1, g1, sh2, sc2, g2 = [mod[i, k] for k in range(6)]
        kind, j = i % 3, i // 3
        st = dict(kind=kind, j=j)
        if y_prev is None:
            h1 = _norm_mod_fwd(xs, vec(norm1_g[i]), sc1, sh1, "norm1_fwd")
        else:
            xs, h1 = _norm_mod_fwd(xs, vec(norm1_g[i]), sc1, sh1, "norm1_fwd", y=y_prev, gate=gate_prev)
        st.update(x_in=xs, h1=h1)
        first_ffn_in = [("ffn", 0, "in")] if i == 0 else []
        if kind == 0:
            qkv = with_gather(both("mix", 0)[1:] if i == 0 else [], lambda side: _mm(
                h1, WF[("attn_wqkv", j)], "nt", "attn_qkv", out_dtype=BF16, bias=vec(F_small["attn_bqkv"][j]), side=side))
            o = with_gather(first_ffn_in, lambda side: _attn_fwd(qkv, _sink_rows(attn_sinks[j]), attn_bias, n_heads,
                                                                 "attn_fwd", side=side))
            y1 = _mm(o, WF[("attn_wo", j)], "nn", "attn_out", out_dtype=BF16, bias=vec(F_small["attn_bo"][j]))
            st.update(qkv=qkv, o=o)
        elif kind == 1:
            p = _mm(h1, WF[("conv_w_in", j)], "nt", "conv_in", out_dtype=BF16, bias=vec(conv_b_in[j]))
            zc, s = with_gather(first_ffn_in, lambda side: _convmix_fwd(
                p, F_small["conv_dw"][j], vec(conv_dw_b[j]), vec(conv_ln_g[j]), vec(conv_ln_b[j]), "convmix_fwd", side=side))
            y1 = _mm(s, WF[("conv_w_out", j)], "nn", "conv_out", out_dtype=BF16, bias=vec(conv_b_out[j]))
            st.update(p=p, zc=zc, s=s)
        else:
            p = _mm(h1, WF[("sgu_w_in", j)], "nt", "sgu_in", out_dtype=BF16, bias=vec(F_small["sgu_b_in"][j]))
            mm_ = _sgu_fwd(p, vec(F_small["sgu_ln_g"][j]), vec(F_small["sgu_ln_b"][j]), sgu_ws[j], sgu_bs[j].T, "sgu_fwd")
            y1 = _mm(mm_, WF[("sgu_w_out", j)], "nn", "sgu_out", out_dtype=BF16, bias=vec(F_small["sgu_b_out"][j]))
            st.update(p=p, m=mm_)
        xs, h2 = _norm_mod_fwd(xs, vec(norm2_g[i]), sc2, sh2, "norm2_fwd", y=y1, gate=g1)
        nxt = i + 1
        z = with_gather(both("mix", nxt) + ([("ffn", 0, "out")] if i == 0 else []),
                        lambda side: _mm(h2, WF[("ffn_w_in", i)], "nt", "ffn_in", out_dtype=BF16, side=side))
        a = with_gather(both("ffn", nxt)[:1],
                        lambda side: _ffn_mid_fwd(z, F_small["ffn_dw"][i], vec(ffn_dw_b[i]), "ffn_mid_fwd", side=side))
        y2 = with_gather(both("ffn", nxt)[1:], lambda side: _mm(a, WF[("ffn_w_out", i)], "nn", "ffn_out",
                                                                      out_dtype=BF16, side=side))
        st.update(y1=y1, x_mid=xs, h2=h2, z=z, a=a, y2=y2)
        saved.append(st)
        y_prev, gate_prev = y2, g2

    dx, d_final_g, loss_row, dy2, dg2 = _final_loss(xs, y_prev, gate_prev, tgt, vec(final_g), "final_loss")
    loss = lax.psum(loss_row[0, 0], ("x", "y", "c"))

    G = {n: [None] * W[n].shape[0] for n in ORDER if n != "final_g"}
    GW = {}
    recv = {}
    dmod = [None] * depth

    def with_scatter(keys, run):
        if not keys:
            return run(None)
        sends = [GW[piece(*k)[:2]].reshape(N_DEV, piece(*k)[2], D) for k in keys]
        *outs, landed = run(("scatter", sends))
        recv.update(zip(keys, landed))
        return outs[0] if len(outs) == 1 else tuple(outs)

    for i in reversed(range(depth)):
        st = saved[i]
        sh1, sc1, g1, sh2, sc2, g2 = [mod[i, k] for k in range(6)]
        kind, j = st["kind"], st["j"]
        last_out = [("mix", 0, "out")] if i == 0 else []
        last_in = [("mix", 0, "in")] if i == 0 else []
        GW[("ffn_w_out", i)] = _mm(st["a"], dy2, "tn", "ffn_out_dw", out_dtype=BF16)
        da = _mm(dy2, WF[("ffn_w_out", i)], "nt", "ffn_out_dx", out_dtype=BF16)
        dzg, dzu, ddw, ddwb = with_scatter(both("mix", i + 1) + [("ffn", i, "out")], lambda side: _ffn_mid_bwd(
            st["z"], da, F_small["ffn_dw"][i], vec(ffn_dw_b[i]), "ffn_mid_bwd", side=side))
        G["ffn_dw"][i], G["ffn_dw_b"][i] = ddw, ddwb[0]
        GW[("ffn_w_in", i)] = _mm(dzg, st["h2"], "tn", "ffn_in_dw", out_dtype=BF16, a2=dzu)
        dh2 = with_scatter([("ffn", i, "in")],
                           lambda side: _mm(dzg, WF[("ffn_w_in", i)], "nn", "ffn_in_dx", out_dtype=BF16, a2=dzu, side=side))
        dx, dn2, dsc2, dsh2, dy1, dg1, dbo = _norm_mod_bwd(dh2, st["x_mid"], vec(norm2_g[i]), sc2, sh2, dx, "norm_bwd",
                                                           y_up=st["y1"], gate_up=g1)
        G["norm2_g"][i] = dn2[0]
        if kind == 0:
            G["attn_bo"][j] = dbo[0]
            GW[("attn_wo", j)] = _mm(st["o"], dy1, "tn", "attn_out_dw", out_dtype=BF16)
            do = _mm(dy1, WF[("attn_wo", j)], "nt", "attn_out_dx", out_dtype=BF16)
            dq, dkv, dbq, dbk, dbv, dsk = with_scatter(last_out, lambda side: _attn_bwd(
                st["qkv"], do, _sink_rows(attn_sinks[j]), attn_bias, n_heads, "attn_bwd", side=side))
            G["attn_bqkv"][j] = jnp.concatenate([dbq, dbk, dbv], axis=1)[0]
            G["attn_sinks"][j] = dsk[:, 0]
            GW[("attn_wqkv", j)] = _mm(dq, st["h1"], "tn", "attn_qkv_dw", out_dtype=BF16, a2=dkv)
            dh1 = with_scatter(last_in, lambda side: _mm(dq, WF[("attn_wqkv", j)], "nn", "attn_qkv_dx", out_dtype=BF16,
                                                         a2=dkv, side=side))
        elif kind == 1:
            G["conv_b_out"][j] = dbo[0]
            GW[("conv_w_out", j)] = _mm(st["s"], dy1, "tn", "conv_out_dw", out_dtype=BF16)
            ds = _mm(dy1, WF[("conv_w_out", j)], "nt", "conv_out_dx", out_dtype=BF16)
            dp, dbin, ddw, ddwb, dlng, dlnb = with_scatter(last_out, lambda side: _convmix_bwd(
                ds, st["zc"], st["p"], F_small["conv_dw"][j], vec(conv_ln_g[j]), vec(conv_ln_b[j]), "convmix_bwd",
                side=side))
            G["conv_b_in"][j], G["conv_dw"][j], G["conv_dw_b"][j] = dbin[0], ddw, ddwb[0]
            G["conv_ln_g"][j], G["conv_ln_b"][j] = dlng[0], dlnb[0]
            GW[("conv_w_in", j)] = _mm(dp, st["h1"], "tn", "conv_in_dw", out_dtype=BF16)
            dh1 = with_scatter(last_in, lambda side: _mm(dp, WF[("conv_w_in", j)], "nn", "conv_in_dx", out_dtype=BF16,
                                                         side=side))
        else:
            G["sgu_b_out"][j] = dbo[0]
            GW[("sgu_w_out", j)] = _mm(st["m"], dy1, "tn", "sgu_out_dw", out_dtype=BF16)
            dm = _mm(dy1, WF[("sgu_w_out", j)], "nt", "sgu_out_dx", out_dtype=BF16)
            dp, dbin, dlng, dlnb, dws, dbst = with_scatter(last_out, lambda side: _sgu_bwd(
                st["p"], dm, vec(F_small["sgu_ln_g"][j]), vec(F_small["sgu_ln_b"][j]), sgu_ws[j], sgu_bs[j].T, "sgu_bwd",
                side=side))
            G["sgu_b_in"][j], G["sgu_ln_g"][j], G["sgu_ln_b"][j] = dbin[0], dlng[0], dlnb[0]
            G["sgu_ws"][j], G["sgu_bs"][j] = dws, dbst.T
            GW[("sgu_w_in", j)] = _mm(dp, st["h1"], "tn", "sgu_in_dw", out_dtype=BF16)
            dh1 = with_scatter(last_in, lambda side: _mm(dp, WF[("sgu_w_in", j)], "nn", "sgu_in_dx", out_dtype=BF16,
                                                         side=side))
        dmod_i = lambda: jnp.concatenate([dsh1, dsc1, dg1, dsh2, dsc2, dg2], axis=1)[0]
        if i > 0:
            dx, dn1, dsc1, dsh1, dy2_below, dg2_below, _ = _norm_mod_bwd(
                dh1, st["x_in"], vec(norm1_g[i]), sc1, sh1, dx, "norm_bwd", y_up=saved[i - 1]["y2"], gate_up=mod[i - 1, 5])
            dmod[i] = dmod_i()
            dy2, dg2 = dy2_below, dg2_below
        else:
            dx, dn1, dsc1, dsh1 = _norm_mod_bwd(dh1, st["x_in"], vec(norm1_g[i]), sc1, sh1, dx, "norm_bwd")
            dmod[i] = dmod_i()
        G["norm1_g"][i] = dn1[0]
    grad_x = dx.reshape(x.shape)

    small_names = [n for n in ORDER if n not in
                   ("ada_w", "ada_b", "attn_wqkv", "attn_wo", "conv_w_in", "conv_w_out", "sgu_w_in", "sgu_w_out",
                    "ffn_w_in", "ffn_w_out", "final_g")]
    g_small = {n: jnp.stack(G[n]) for n in small_names}
    g_small["final_g"] = d_final_g[0]
    g_small["ada_b"] = jnp.stack(dmod)
    names16 = ["ffn_dw", "conv_dw", "sgu_ws", "ffn_dw_b", "ada_b"]
    names32 = [n for n in g_small if n not in names16]
    shapes16, shapes32 = [g_small[n].shape for n in names16], [g_small[n].shape for n in names32]
    gathered32, gathered16 = _exchange("gather", [_pack([g_small[n] for n in names32], 1024),
                                                  _pack([g_small[n] for n in names16], 1024).astype(BF16)],
                                       "gather_small_grads")
    gsum = dict(zip(names32, _unpack(_sum8(gathered32, "sum_small_grads").reshape(-1), shapes32)))
    gsum.update(zip(names16, _unpack(_sum8(gathered16, "sum_small_grads").reshape(-1), shapes16)))

    def local_shard(n, a):
        if n in small_sharded:
            w = W[n].shape[-1]
            return lax.dynamic_slice_in_dim(a, me * w, w, axis=a.ndim - 1)
        return a

    gsum = {n: local_shard(n, a) for n, a in gsum.items()}

    off16 = sum(math.prod(s) for s in shapes16[:-1])
    dmod_all = gathered16.reshape(N_DEV, -1)[:, off16:off16 + math.prod(shapes16[-1])].reshape((N_DEV,) + shapes16[-1])
    dmod_cols = lax.dynamic_slice_in_dim(dmod_all, me * n_ada, n_ada, axis=2)
    dmod_pad = jnp.pad(dmod_cols, ((0, 8), (0, 0), (0, 0)))
    g_ada_w = jnp.stack([_mm(c_pad, dmod_pad[:, i, :], "tn", "ada_dw") for i in range(depth)])

    big = {}
    for i in range(depth):
        for key in both("mix", i) + both("ffn", i):
            n, l, _ = piece(*key)
            gp = _sum8(recv[key], "sum_weight_grads")
            big.setdefault(n, {})[l] = gp.T if key[2] == "in" else gp
    grads = {n: jnp.stack([v[l] for l in range(len(v))]) for n, v in big.items()}
    grads["ada_w"] = g_ada_w
    grads.update(gsum)

    delta, new_m, new_v = {}, {}, {}
    big_names = ["ada_w", "attn_wqkv", "attn_wo", "conv_w_in", "conv_w_out", "sgu_w_in", "sgu_w_out", "ffn_w_in",
                 "ffn_w_out"]
    for n in big_names:
        delta[n], new_m[n], new_v[n] = _adamw_nd(W[n], grads[n], MOM[n], VAR[n], "adamw_" + n)
    rest = [n for n in ORDER if n not in big_names]
    rest_shapes = [W[n].shape for n in rest]
    pk = lambda d: _pack([d[n] for n in rest], 128)
    d_s, m_s, v_s = _adamw(pk(W), pk(grads), pk(MOM), pk(VAR), "adamw_small")
    for n, a, b_, c_ in zip(rest, _unpack(d_s.reshape(-1), rest_shapes), _unpack(m_s.reshape(-1), rest_shapes),
                            _unpack(v_s.reshape(-1), rest_shapes)):
        delta[n], new_m[n], new_v[n] = a, b_, c_

    return (loss, grad_x, *[grads[n] for n in ORDER], *[delta[n] for n in ORDER],
            *[new_m[n] for n in ORDER], *[new_v[n] for n in ORDER])
```

```python
import functools
import math

import jax
import jax.numpy as jnp
from jax import lax
from jax.experimental import pallas as pl
from jax.experimental.pallas import tpu as pltpu

F32 = jnp.float32
BF16 = jnp.bfloat16

N_DEV = 8
HEAD_DIM = 64
N_KV = 4
BLOCK = 128
NORM_EPS = 1e-6
NEG_INF = -1e30
ADAM_LR = 0.001
ADAM_B1 = 0.9
ADAM_B2 = 0.999
ADAM_EPS = 1e-08
ADAM_WD = 0.01
ADAM_STEP = 10
V7X_VMEM_LIMIT = 56 * 1024 * 1024
MESH = pl.DeviceIdType.MESH


def _cparams():
    return pltpu.CompilerParams(vmem_limit_bytes=V7X_VMEM_LIMIT)


def _tile(n, cap, align):
    best = None
    for t in range(align, min(n, cap) + 1, align):
        if n % t == 0:
            best = t
    return best if best is not None else n


def _dot(a, b, ca, cb):
    return lax.dot_general(a, b, (((ca,), (cb,)), ((), ())), preferred_element_type=F32)


def _sigmoid(x):
    return 1.0 / (1.0 + jnp.exp(-x))


def _rows(x, start, n):
    return lax.slice_in_dim(x, start, start + n, axis=0)


def _my_coords():
    return lax.axis_index("x"), lax.axis_index("y"), lax.axis_index("c")


def _peer(m, mx, my, mc):
    px = (mx + ((m >> 2) & 1)) % 2
    py = (my + ((m >> 1) & 1)) % 2
    pc = (mc + (m & 1)) % 2
    return px, py, pc


def _exchange_copies(kind, x_ref, o_ref, send_sems, recv_sems, local_sem):
    mx, my, mc = _my_coords()
    me = 4 * mx + 2 * my + mc
    local = pltpu.make_async_copy(x_ref if kind == "gather" else x_ref.at[me], o_ref.at[me], local_sem)
    remote = []
    for m in range(1, N_DEV):
        px, py, pc = _peer(m, mx, my, mc)
        src = x_ref if kind == "gather" else x_ref.at[4 * px + 2 * py + pc]
        remote.append(pltpu.make_async_remote_copy(
            src_ref=src, dst_ref=o_ref.at[me], send_sem=send_sems.at[m - 1], recv_sem=recv_sems.at[m - 1],
            device_id=(px, py, pc), device_id_type=MESH))
    return local, remote


def _exchange_start(kind, *refs):
    local, remote = _exchange_copies(kind, *refs)
    local.start()
    for cp in remote:
        cp.start()


def _exchange_wait(kind, *refs):
    local, remote = _exchange_copies(kind, *refs)
    for cp in remote:
        cp.wait_recv()
    for cp in remote:
        cp.wait_send()
    local.wait()


_EXCHANGE_SEMS = [pltpu.SemaphoreType.DMA((N_DEV - 1,)), pltpu.SemaphoreType.DMA((N_DEV - 1,)), pltpu.SemaphoreType.DMA]


def _exchange_shape(kind, x):
    return jax.ShapeDtypeStruct(((N_DEV,) + x.shape) if kind == "gather" else x.shape, x.dtype)


def _exchange(kind, xs, name):
    n = len(xs)

    def body(*refs):
        x_refs, o_refs, sems = refs[:n], refs[n:2 * n], refs[2 * n:]
        for e in range(n):
            _exchange_start(kind, x_refs[e], o_refs[e], *sems[3 * e:3 * e + 3])
        for e in range(n):
            _exchange_wait(kind, x_refs[e], o_refs[e], *sems[3 * e:3 * e + 3])

    any_spec = pl.BlockSpec(memory_space=pl.ANY)
    return tuple(pl.pallas_call(
        body, name=name, out_shape=tuple(_exchange_shape(kind, x) for x in xs),
        in_specs=[any_spec] * n, out_specs=(any_spec,) * n, scratch_shapes=list(_EXCHANGE_SEMS) * n,
    )(*xs))


def _all_gather(x, name):
    return _exchange("gather", [x], name)[0]


def _call(body, name, grid, in_specs, out_specs, out_shape, args, scratch=(), side=None):
    single = not isinstance(out_shape, (tuple, list))
    if single:
        out_shape, out_specs = (out_shape,), (out_specs,)
    if side is None:
        res = pl.pallas_call(body, name=name, grid=grid, in_specs=list(in_specs), out_specs=tuple(out_specs),
                             out_shape=tuple(out_shape), scratch_shapes=list(scratch), compiler_params=_cparams())(*args)
        return res[0] if single else tuple(res)
    kind, xs = side
    n_in, n_out, n_scr, n_x = len(in_specs), len(out_shape), len(scratch), len(xs)

    def wrapped(*refs):
        ins, x_refs = refs[:n_in], refs[n_in:n_in + n_x]
        o0 = n_in + n_x
        outs, o_refs = refs[o0:o0 + n_out], refs[o0 + n_out:o0 + n_out + n_x]
        s0 = o0 + n_out + n_x
        scr, sems = refs[s0:s0 + n_scr], refs[s0 + n_scr:]
        ids = [pl.program_id(a) for a in range(len(grid))]
        first = functools.reduce(jnp.logical_and, [i == 0 for i in ids])
        last = functools.reduce(jnp.logical_and, [i == g - 1 for i, g in zip(ids, grid)])

        @pl.when(first)
        def _():
            for e in range(n_x):
                _exchange_start(kind, x_refs[e], o_refs[e], *sems[3 * e:3 * e + 3])

        body(*ins, *outs, *scr)

        @pl.when(last)
        def _():
            for e in range(n_x):
                _exchange_wait(kind, x_refs[e], o_refs[e], *sems[3 * e:3 * e + 3])

    any_spec = pl.BlockSpec(memory_space=pl.ANY)
    res = pl.pallas_call(
        wrapped, name=name, grid=grid, in_specs=list(in_specs) + [any_spec] * n_x,
        out_specs=tuple(out_specs) + (any_spec,) * n_x,
        out_shape=tuple(out_shape) + tuple(_exchange_shape(kind, x) for x in xs),
        scratch_shapes=list(scratch) + list(_EXCHANGE_SEMS) * n_x, compiler_params=_cparams())(*args, *xs)
    return tuple(res[:n_out]) + (tuple(res[n_out:]),)


def _mm(a, b, mode, name, out_dtype=F32, bias=None, side=None, a2=None, tm_cap=1024, tn_cap=1408, tk_cap=1408):
    w2 = 0 if a2 is None else a2.shape[1]
    assert a2 is None or (mode in ("nn", "tn") and a2.shape[0] == a.shape[0])
    if mode == "nn":
        (M, K), (K2, N) = a.shape, b.shape
        K += w2
    elif mode == "nt":
        (M, K), (N, K2) = a.shape, b.shape
    else:
        (K, M), (K2, N) = a.shape, b.shape
        M += w2
    assert K == K2, (a.shape, b.shape, mode)
    if mode == "tn":
        tm_cap, tk_cap = 256, 8192
    if mode == "nn":
        tk_cap = 2816
    if mode == "nt" or (mode == "nn" and a2 is None):
        tm_cap = 2048
    tm = _tile(math.gcd(M, w2) if mode == "tn" else M, tm_cap, 128 if mode == "tn" else 16)
    tn = _tile(N, tn_cap, 128)
    tk = _tile(math.gcd(K, w2) if mode == "nn" else K, tk_cap, 128)
    nk = K // tk
    n1 = a.shape[1] // (tm if mode == "tn" else tk)
    ca, cb = {"nn": (1, 0), "nt": (1, 1), "tn": (0, 0)}[mode]
    has_bias = bias is not None

    def body(*refs):
        refs = list(refs)
        a_ref = refs.pop(0)
        a2_ref = refs.pop(0) if a2 is not None else None
        b_ref = refs.pop(0)
        bias_ref = refs.pop(0) if has_bias else None
        o_ref, acc_ref = refs
        k = pl.program_id(2)

        def finish(r):
            if has_bias:
                r = r + bias_ref[...]
            o_ref[...] = r.astype(out_dtype)

        def step(lhs_ref):
            part = _dot(lhs_ref[...].astype(BF16), b_ref[...].astype(BF16), ca, cb)
            if nk == 1:
                finish(part)
                return

            @pl.when(k == 0)
            def _():
                acc_ref[...] = part

            @pl.when(k > 0)
            def _():
                acc_ref[...] += part

        if a2 is None:
            step(a_ref)
        else:
            col = pl.program_id(0) if mode == "tn" else k
            pl.when(col < n1)(lambda: step(a_ref))
            pl.when(col >= n1)(lambda: step(a2_ref))

        if nk > 1:
            pl.when(k == nk - 1)(lambda: finish(acc_ref[...]))

    if mode == "tn":
        a_specs = [pl.BlockSpec((tk, tm), lambda i, j, k: (k, jnp.minimum(i, n1 - 1)))]
        if a2 is not None:
            a_specs.append(pl.BlockSpec((tk, tm), lambda i, j, k: (k, jnp.maximum(i - n1, 0))))
    else:
        a_specs = [pl.BlockSpec((tm, tk), lambda i, j, k: (i, jnp.minimum(k, n1 - 1)))]
        if a2 is not None:
            a_specs.append(pl.BlockSpec((tm, tk), lambda i, j, k: (i, jnp.maximum(k - n1, 0))))
    if mode == "nt":
        b_spec = pl.BlockSpec((tn, tk), lambda i, j, k: (j, k))
    else:
        b_spec = pl.BlockSpec((tk, tn), lambda i, j, k: (k, j))
    in_specs = a_specs + [b_spec]
    args = [a] + ([a2] if a2 is not None else []) + [b]
    if has_bias:
        in_specs.append(pl.BlockSpec((1, tn), lambda i, j, k: (0, j)))
        args.append(bias)
    return _call(body, name, (M // tm, N // tn, nk), in_specs, pl.BlockSpec((tm, tn), lambda i, j, k: (i, j)),
                 jax.ShapeDtypeStruct((M, N), out_dtype), args,
                 scratch=[pltpu.VMEM((tm, tn) if nk > 1 else (8, 128), F32)], side=side)


ROW_TILE = 1024


def _vec_spec(n):
    return pl.BlockSpec((1, n), lambda t: (0, 0))


def _norm_mod_fwd(x, g, sc, sh, name, y=None, gate=None):
    T, D = x.shape
    tr = _tile(T, ROW_TILE, 16)
    has_y = y is not None

    def body(*refs):
        if has_y:
            x_ref, y_ref, gate_ref, g_ref, sc_ref, sh_ref, xn_ref, h_ref = refs
            xv = x_ref[...] + gate_ref[...] * y_ref[...].astype(F32)
            xn_ref[...] = xv
        else:
            x_ref, g_ref, sc_ref, sh_ref, h_ref = refs
            xv = x_ref[...]
        r = lax.rsqrt(jnp.mean(xv * xv, axis=-1, keepdims=True) + NORM_EPS)
        h = (xv * r * g_ref[...]) * (1.0 + sc_ref[...]) + sh_ref[...]
        h_ref[...] = h.astype(BF16)

    row = pl.BlockSpec((tr, D), lambda t: (t, 0))
    vec = _vec_spec(D)
    if has_y:
        in_specs, args = [row, row, vec, vec, vec, vec], [x, y, gate, g, sc, sh]
        out_shape = (jax.ShapeDtypeStruct((T, D), F32), jax.ShapeDtypeStruct((T, D), BF16))
        out_specs = (row, row)
    else:
        in_specs, args = [row, vec, vec, vec], [x, g, sc, sh]
        out_shape = jax.ShapeDtypeStruct((T, D), BF16)
        out_specs = row
    return _call(body, name, (T // tr,), in_specs, out_specs, out_shape, args)


def _norm_mod_bwd(dh, x, g, sc, sh, dxn, name, y_up=None, gate_up=None):
    T, D = x.shape
    tr = _tile(T, ROW_TILE, 16)
    fused = y_up is not None

    def body(*refs):
        if fused:
            (dh_ref, x_ref, g_ref, sc_ref, sh_ref, dxn_ref, y_ref, gate_ref,
             dx_ref, dg_ref, dsc_ref, dsh_ref, dy_ref, dgate_ref, dbias_ref) = refs
            sums = (dg_ref, dsc_ref, dsh_ref, dgate_ref, dbias_ref)
        else:
            dh_ref, x_ref, g_ref, sc_ref, sh_ref, dxn_ref, dx_ref, dg_ref, dsc_ref, dsh_ref = refs
            sums = (dg_ref, dsc_ref, dsh_ref)

        @pl.when(pl.program_id(0) == 0)
        def _():
            for r_ in sums:
                r_[...] = jnp.zeros_like(r_)
        dh = dh_ref[...].astype(F32)
        xv = x_ref[...]
        gv = g_ref[...]
        r = lax.rsqrt(jnp.mean(xv * xv, axis=-1, keepdims=True) + NORM_EPS)
        yv = xv * r
        dsh_ref[...] += jnp.sum(dh, axis=0, keepdims=True)
        dsc_ref[...] += jnp.sum(dh * (yv * gv), axis=0, keepdims=True)
        dn = dh * (1.0 + sc_ref[...])
        dg_ref[...] += jnp.sum(dn * yv, axis=0, keepdims=True)
        dy = dn * gv
        dx = dxn_ref[...] + r * (dy - yv * jnp.mean(dy * yv, axis=-1, keepdims=True))
        dx_ref[...] = dx
        if fused:
            dyu = dx * gate_ref[...]
            dy_ref[...] = dyu.astype(BF16)
            dgate_ref[...] += jnp.sum(dx * y_ref[...].astype(F32), axis=0, keepdims=True)
            dbias_ref[...] += jnp.sum(dyu, axis=0, keepdims=True)

    row = pl.BlockSpec((tr, D), lambda t: (t, 0))
    vec = _vec_spec(D)
    vshape = jax.ShapeDtypeStruct((1, D), F32)
    in_specs, args = [row, row, vec, vec, vec, row], [dh, x, g, sc, sh, dxn]
    out_specs, out_shape = [row, vec, vec, vec], [jax.ShapeDtypeStruct((T, D), F32), vshape, vshape, vshape]
    if fused:
        in_specs, args = in_specs + [row, vec], args + [y_up, gate_up]
        out_specs, out_shape = out_specs + [row, vec, vec], out_shape + [jax.ShapeDtypeStruct((T, D), BF16), vshape, vshape]
    return _call(body, name, (T // tr,), in_specs, tuple(out_specs), tuple(out_shape), args)


def _final_loss(x, y, gate, tgt, g, name):
    T, D = x.shape
    tr = _tile(T, ROW_TILE, 16)

    def body(x_ref, y_ref, gate_ref, tgt_ref, g_ref, dx_ref, dg_ref, loss_ref, dyu_ref, dgate_ref):
        @pl.when(pl.program_id(0) == 0)
        def _():
            for r_ in (dg_ref, loss_ref, dgate_ref):
                r_[...] = jnp.zeros_like(r_)
        yu = y_ref[...].astype(F32)
        gate = gate_ref[...]
        xv = x_ref[...] + gate * yu
        gv = g_ref[...]
        r = lax.rsqrt(jnp.mean(xv * xv, axis=-1, keepdims=True) + NORM_EPS)
        yv = xv * r
        e = yv * gv - tgt_ref[...]
        per_row = jnp.mean(e * e, axis=-1, keepdims=True)
        loss_ref[...] += 0.5 * jnp.sum(per_row, axis=0, keepdims=True)
        dout = e * (1.0 / D)
        dg_ref[...] += jnp.sum(dout * yv, axis=0, keepdims=True)
        dy = dout * gv
        dx = r * (dy - yv * jnp.mean(dy * yv, axis=-1, keepdims=True))
        dx_ref[...] = dx
        dyu_ref[...] = (dx * gate).astype(BF16)
        dgate_ref[...] += jnp.sum(dx * yu, axis=0, keepdims=True)

    row = pl.BlockSpec((tr, D), lambda t: (t, 0))
    vec = _vec_spec(D)
    return _call(body, name, (T // tr,), [row, row, vec, row, vec], (row, vec, _vec_spec(128), row, vec),
                 (jax.ShapeDtypeStruct((T, D), F32), jax.ShapeDtypeStruct((1, D), F32),
                  jax.ShapeDtypeStruct((1, 128), F32), jax.ShapeDtypeStruct((T, D), BF16),
                  jax.ShapeDtypeStruct((1, D), F32)), (x, y, gate, tgt, g))


FFN_HALO = 16
FFN_PAD = 8
FFN_CHUNK = 64
FFN_ROWS_FWD = 8192
FFN_ROWS_BWD = 8192


def _lane_groups(cw):
    lw = 128 if cw % 128 == 0 else cw
    return lw, [pl.ds(g * lw, lw) for g in range(cw // lw)]


def _ffn_mid_fwd(z, dw, b, name, side=None):
    T, F2 = z.shape
    F = F2 // 2
    K = dw.shape[0]
    H, P = FFN_HALO, FFN_PAD
    tr = _tile(T, FFN_ROWS_FWD, H)
    cw = _tile(F, 512, 128)
    nc = F // cw
    rc = _tile(tr, FFN_CHUNK, 16)
    n_chunks = tr // rc
    assert n_chunks % 2 == 0 and rc >= H
    lw, groups = _lane_groups(cw)

    def body(gc_ref, gh_ref, uc_ref, uh_ref, wg_ref, wu_ref, bg_ref, bu_ref, a_ref, xg_s, xu_s):
        first = pl.program_id(1) == 0
        halves = ((xg_s, gc_ref, gh_ref, wg_ref, bg_ref), (xu_s, uc_ref, uh_ref, wu_ref, bu_ref))

        for s, c_ref, h_ref, _, _ in halves:
            for g, ls in enumerate(groups):
                s[0, g, 0:P, :] = jnp.where(first, 0.0, h_ref[:, ls].astype(F32)[H - P:H])
                s[0, g, P:P + rc, :] = c_ref[0:rc, ls].astype(F32)

        def stage(slot, c):
            start = pl.multiple_of(c * rc - H, H)
            for s, c_ref, _, _, _ in halves:
                for g, ls in enumerate(groups):
                    s[slot, g] = c_ref[pl.ds(start, rc + H), ls].astype(F32)[H - P:]

        def compute(slot, c):
            r0 = pl.multiple_of(c * rc, rc)
            for g, ls in enumerate(groups):
                vals = []
                for s, _, _, w_ref, b_ref in halves:
                    acc = b_ref[:, ls] + w_ref[K - 1:K, ls] * s[slot, g, P:P + rc, :]
                    for k in range(K - 1):
                        o = P - (K - 1 - k)
                        acc = acc + w_ref[k:k + 1, ls] * s[slot, g, o:o + rc, :]
                    vals.append(acc)
                gv, uv = vals
                a_ref[pl.ds(r0, rc), ls] = (gv * _sigmoid(gv) * uv).astype(BF16)

        def pair(i, carry):
            c0 = 2 * i
            compute(0, c0)
            stage(1, c0 + 1)
            compute(1, c0 + 1)
            stage(0, jnp.minimum(c0 + 2, n_chunks - 1))
            return carry

        lax.fori_loop(0, n_chunks // 2, pair, 0)

    rb = tr // H
    in_specs = [
        pl.BlockSpec((tr, cw), lambda j, t: (t, j)),
        pl.BlockSpec((H, cw), lambda j, t: (jnp.maximum(t * rb - 1, 0), j)),
        pl.BlockSpec((tr, cw), lambda j, t: (t, j + nc)),
        pl.BlockSpec((H, cw), lambda j, t: (jnp.maximum(t * rb - 1, 0), j + nc)),
        pl.BlockSpec((K, cw), lambda j, t: (0, j)),
        pl.BlockSpec((K, cw), lambda j, t: (0, j + nc)),
        pl.BlockSpec((1, cw), lambda j, t: (0, j)),
        pl.BlockSpec((1, cw), lambda j, t: (0, j + nc)),
    ]
    return _call(body, name, (nc, T // tr), in_specs, pl.BlockSpec((tr, cw), lambda j, t: (t, j)),
                 jax.ShapeDtypeStruct((T, F), BF16), (z, z, z, z, dw, dw, b, b),
                 scratch=[pltpu.VMEM((2, len(groups), P + rc, lw), F32)] * 2, side=side)


def _ffn_mid_bwd(z, da, dw, b, name, side=None):
    T, F2 = z.shape
    F = F2 // 2
    K = dw.shape[0]
    H, P = FFN_HALO, FFN_PAD
    tr = _tile(T, FFN_ROWS_BWD, H)
    cw = _tile(F, 512, 128)
    nc = F // cw
    nt = T // tr
    rb = tr // H
    rc = _tile(tr, FFN_CHUNK, 16)
    n_chunks = tr // rc
    assert n_chunks % 2 == 0 and rc >= H
    ext = rc + P
    lw, groups = _lane_groups(cw)

    def body(gp_ref, gc_ref, gn_ref, up_ref, uc_ref, un_ref, dac_ref, dan_ref, wg_ref, wu_ref, bg_ref, bu_ref,
             dzg_ref, dzu_ref, dwg_ref, dwu_ref, dbg_ref, dbu_ref, xg_s, xu_s, da_s, dg_s, du_s, acc_s):
        t = pl.program_id(1)
        first = t == 0
        last = t == nt - 1

        @pl.when(first)
        def _():
            for r in (dwg_ref, dwu_ref, dbg_ref, dbu_ref):
                r[...] = jnp.zeros_like(r)

        xs = ((xg_s, gp_ref, gc_ref, gn_ref), (xu_s, up_ref, uc_ref, un_ref))
        acc_s[...] = jnp.zeros_like(acc_s)

        def stage_first():
            for s, p_ref, c_ref, _ in xs:
                for g, ls in enumerate(groups):
                    s[0, g, 0:P, :] = jnp.where(first, 0.0, p_ref[:, ls].astype(F32)[H - P:H])
                    s[0, g, P:2 * P + rc, :] = c_ref[0:rc + H, ls].astype(F32)[:rc + P]
            for g, ls in enumerate(groups):
                da_s[0, g] = dac_ref[0:rc + H, ls].astype(F32)[:ext]

        def stage(slot, c):
            start = pl.multiple_of(c * rc - H, H)
            for s, _, c_ref, _ in xs:
                for g, ls in enumerate(groups):
                    s[slot, g] = c_ref[pl.ds(start, rc + 2 * H), ls].astype(F32)[H - P:H + rc + P]
            for g, ls in enumerate(groups):
                da_s[slot, g] = dac_ref[pl.ds(pl.multiple_of(c * rc, rc), rc + H), ls].astype(F32)[:ext]

        def stage_last(slot):
            for s, _, c_ref, n_ref in xs:
                for g, ls in enumerate(groups):
                    s[slot, g, 0:P + rc, :] = c_ref[tr - rc - H:tr, ls].astype(F32)[H - P:]
                    s[slot, g, P + rc:2 * P + rc, :] = jnp.where(last, 0.0, n_ref[:, ls].astype(F32)[0:P])
            for g, ls in enumerate(groups):
                da_s[slot, g, 0:rc, :] = dac_ref[tr - rc:tr, ls].astype(F32)
                da_s[slot, g, rc:ext, :] = jnp.where(last, 0.0, dan_ref[:, ls].astype(F32)[0:P])

        def fold(v):
            return jnp.sum(v.reshape(rc // 8, 8, lw), axis=0)

        def compute(slot, c):
            r0 = c * rc if isinstance(c, int) else pl.multiple_of(c * rc, rc)
            for g, ls in enumerate(groups):
                def conv_ext(s, w_ref, b_ref):
                    acc = b_ref[:, ls] + w_ref[K - 1:K, ls] * s[slot, g, P:P + ext, :]
                    for k in range(K - 1):
                        o = P - (K - 1 - k)
                        acc = acc + w_ref[k:k + 1, ls] * s[slot, g, o:o + ext, :]
                    return acc

                gv = conv_ext(xg_s, wg_ref, bg_ref)
                uv = conv_ext(xu_s, wu_ref, bu_ref)
                dav = da_s[slot, g]
                sg = _sigmoid(gv)
                dg = dav * uv * (sg * (1.0 + gv * (1.0 - sg)))
                du = dav * (gv * sg)
                dg_s[g] = dg
                du_s[g] = du
                for idx, (d_s, dval, x_s, w_ref, dz_ref) in enumerate(
                        ((dg_s, dg, xg_s, wg_ref, dzg_ref), (du_s, du, xu_s, wu_ref, dzu_ref))):
                    cur = dval[:rc]
                    acc_s[idx, K, g] += fold(cur)
                    for k in range(K):
                        o = P - (K - 1 - k)
                        acc_s[idx, k, g] += fold(cur * x_s[slot, g, o:o + rc, :])
                    dz = w_ref[K - 1:K, ls] * cur
                    for k in range(K - 1):
                        dz = dz + w_ref[k:k + 1, ls] * d_s[g, K - 1 - k:K - 1 - k + rc, :]
                    dz_ref[pl.ds(r0, rc), ls] = dz.astype(BF16)

        def pair(i, carry):
            c0 = 2 * i
            compute(0, c0)
            stage(1, c0 + 1)
            compute(1, c0 + 1)
            stage(0, c0 + 2)
            return carry

        stage_first()
        lax.fori_loop(0, n_chunks // 2 - 1, pair, 0)
        compute(0, n_chunks - 2)
        stage_last(1)
        compute(1, n_chunks - 1)
        for idx, (dw_ref, db_ref) in enumerate(((dwg_ref, dbg_ref), (dwu_ref, dbu_ref))):
            for g, ls in enumerate(groups):
                db_ref[:, ls] += jnp.sum(acc_s[idx, K, g], axis=0, keepdims=True)
                dw_ref[:, ls] += jnp.concatenate(
                    [jnp.sum(acc_s[idx, k, g], axis=0, keepdims=True) for k in range(K)], axis=0)

    def prev(t):
        return jnp.maximum(t * rb - 1, 0)

    def nxt(t):
        return jnp.minimum((t + 1) * rb, T // H - 1)

    def zspecs(off):
        return [pl.BlockSpec((H, cw), lambda j, t: (prev(t), j + off)),
                pl.BlockSpec((tr, cw), lambda j, t: (t, j + off)),
                pl.BlockSpec((H, cw), lambda j, t: (nxt(t), j + off))]

    in_specs = zspecs(0) + zspecs(nc) + [
        pl.BlockSpec((tr, cw), lambda j, t: (t, j)),
        pl.BlockSpec((H, cw), lambda j, t: (nxt(t), j)),
        pl.BlockSpec((K, cw), lambda j, t: (0, j)),
        pl.BlockSpec((K, cw), lambda j, t: (0, j + nc)),
        pl.BlockSpec((1, cw), lambda j, t: (0, j)),
        pl.BlockSpec((1, cw), lambda j, t: (0, j + nc)),
    ]
    out_shape = (jax.ShapeDtypeStruct((T, F), BF16), jax.ShapeDtypeStruct((T, F), BF16),
                 jax.ShapeDtypeStruct((K, F), F32), jax.ShapeDtypeStruct((K, F), F32),
                 jax.ShapeDtypeStruct((1, F), F32), jax.ShapeDtypeStruct((1, F), F32))
    out_specs = (pl.BlockSpec((tr, cw), lambda j, t: (t, j)), pl.BlockSpec((tr, cw), lambda j, t: (t, j)),
                 pl.BlockSpec((K, cw), lambda j, t: (0, j)), pl.BlockSpec((K, cw), lambda j, t: (0, j)),
                 pl.BlockSpec((1, cw), lambda j, t: (0, j)), pl.BlockSpec((1, cw), lambda j, t: (0, j)))
    ng = len(groups)
    scratch = [pltpu.VMEM((2, ng, 2 * P + rc, lw), F32), pltpu.VMEM((2, ng, 2 * P + rc, lw), F32),
               pltpu.VMEM((2, ng, ext, lw), F32), pltpu.VMEM((ng, ext, lw), F32), pltpu.VMEM((ng, ext, lw), F32),
               pltpu.VMEM((2, K + 1, ng, 8, lw), F32)]
    res = _call(body, name, (nc, nt), in_specs, out_specs, out_shape, (z, z, z, z, z, z, da, da, dw, dw, b, b),
                scratch=scratch, side=side)
    dzg, dzu, dwg, dwu, dbg, dbu = res[:6]
    return (dzg, dzu, jnp.concatenate([dwg, dwu], axis=1), jnp.concatenate([dbg, dbu], axis=1)) + tuple(res[6:])


CONV_HALO = 32
CONV_ROWS = 512
CONV_CHUNK = 64


def _glu_window(s, p_ref, halo_ref, first, D, groups, lw, H, tr):
    for g, ls in enumerate(groups):
        gs = pl.ds(D + g * lw, lw)
        s[g, 0:H, :] = jnp.where(first, 0.0, halo_ref[:, ls].astype(F32) * _sigmoid(halo_ref[:, gs].astype(F32)))
        s[g, H:H + tr, :] = p_ref[:, ls].astype(F32) * _sigmoid(p_ref[:, gs].astype(F32))


def _convmix_fwd(p, dw, dwb, lng, lnb, name, side=None):
    T, D2 = p.shape
    D = D2 // 2
    K = dw.shape[0]
    H = CONV_HALO
    tr = _tile(T, CONV_ROWS, H)
    rb = tr // H
    rc = _tile(tr, CONV_CHUNK, 16)
    lw, groups = _lane_groups(D)

    def body(pc_ref, ph_ref, w_ref, wb_ref, lng_ref, lnb_ref, zc_ref, s_ref, zg_s):
        first = pl.program_id(0) == 0
        _glu_window(zg_s, pc_ref, ph_ref, first, D, groups, lw, H, tr)

        def chunk(ci, carry):
            r0 = pl.multiple_of(ci * rc, rc)
            for g, ls in enumerate(groups):
                acc = wb_ref[:, ls] + w_ref[K - 1:K, ls] * zg_s[g, pl.ds(pl.multiple_of(r0 + H, 8), rc), :]
                for k in range(K - 1):
                    acc = acc + w_ref[k:k + 1, ls] * zg_s[g, pl.ds(r0 + H - (K - 1 - k), rc), :]
                zc_ref[pl.ds(r0, rc), ls] = acc
            return carry

        lax.fori_loop(0, tr // rc, chunk, 0)
        acc = zc_ref[...]
        mu = jnp.mean(acc, axis=-1, keepdims=True)
        xc = acc - mu
        rstd = lax.rsqrt(jnp.mean(xc * xc, axis=-1, keepdims=True) + NORM_EPS)
        ln = xc * rstd * lng_ref[...] + lnb_ref[...]
        s_ref[...] = (ln * _sigmoid(ln)).astype(BF16)

    in_specs = [pl.BlockSpec((tr, D2), lambda t: (t, 0)),
                pl.BlockSpec((H, D2), lambda t: (jnp.maximum(t * rb - 1, 0), 0)),
                pl.BlockSpec((K, D), lambda t: (0, 0)), _vec_spec(D), _vec_spec(D), _vec_spec(D)]
    row = pl.BlockSpec((tr, D), lambda t: (t, 0))
    return _call(body, name, (T // tr,), in_specs, (row, row),
                 (jax.ShapeDtypeStruct((T, D), F32), jax.ShapeDtypeStruct((T, D), BF16)), (p, p, dw, dwb, lng, lnb),
                 scratch=[pltpu.VMEM((len(groups), H + tr, lw), F32)], side=side)


def _convmix_bwd(ds, zc, p, dw, lng, lnb, name, side=None):
    T, D = zc.shape
    D2 = 2 * D
    K = dw.shape[0]
    H = CONV_HALO
    tr = _tile(T, CONV_ROWS, H)
    rb = tr // H
    nt = T // tr
    ext = tr + H
    rc = _tile(tr, CONV_CHUNK, 16)
    lw, groups = _lane_groups(D)

    def body(dsc_ref, dsn_ref, zcc_ref, zcn_ref, pp_ref, pc_ref, w_ref, lng_ref, lnb_ref,
             dp_ref, dbin_ref, ddw_ref, ddwb_ref, dlng_ref, dlnb_ref, zg_s, dzc_s, acc_s):
        t = pl.program_id(0)
        first = t == 0
        last = t == nt - 1

        @pl.when(first)
        def _():
            for r in (dbin_ref, ddw_ref, ddwb_ref, dlng_ref, dlnb_ref):
                r[...] = jnp.zeros_like(r)

        dsv = jnp.concatenate([dsc_ref[...].astype(F32), jnp.where(last, 0.0, dsn_ref[...].astype(F32))], axis=0)
        zcv = jnp.concatenate([zcc_ref[...], zcn_ref[...]], axis=0)
        lg = lng_ref[...]
        mu = jnp.mean(zcv, axis=-1, keepdims=True)
        xc = zcv - mu
        rstd = lax.rsqrt(jnp.mean(xc * xc, axis=-1, keepdims=True) + NORM_EPS)
        xh = xc * rstd
        ln = xh * lg + lnb_ref[...]
        sg = _sigmoid(ln)
        dln = dsv * (sg * (1.0 + ln * (1.0 - sg)))
        dlnc = _rows(dln, 0, tr)
        dlnb_ref[...] += jnp.sum(dlnc, axis=0, keepdims=True)
        dlng_ref[...] += jnp.sum(dlnc * _rows(xh, 0, tr), axis=0, keepdims=True)
        dxh = dln * lg
        dzc = rstd * (dxh - jnp.mean(dxh, axis=-1, keepdims=True) - xh * jnp.mean(dxh * xh, axis=-1, keepdims=True))
        dzcc = _rows(dzc, 0, tr)
        ddwb_ref[...] += jnp.sum(dzcc, axis=0, keepdims=True)

        for g in range(len(groups)):
            dzc_s[g] = dzc[:, g * lw:(g + 1) * lw]
        _glu_window(zg_s, pc_ref, pp_ref, first, D, groups, lw, H, tr)
        acc_s[...] = jnp.zeros_like(acc_s)

        def fold(v):
            return jnp.sum(v.reshape(rc // 8, 8, lw), axis=0)

        def chunk(ci, carry):
            r0 = pl.multiple_of(ci * rc, rc)
            for g, ls in enumerate(groups):
                gs = pl.ds(D + g * lw, lw)
                cur = dzc_s[g, pl.ds(r0, rc), :]
                dzg = w_ref[K - 1:K, ls] * cur
                for k in range(K):
                    acc_s[k, g] += fold(cur * zg_s[g, pl.ds(r0 + H - (K - 1 - k), rc), :])
                    if k < K - 1:
                        dzg = dzg + w_ref[k:k + 1, ls] * dzc_s[g, pl.ds(r0 + (K - 1 - k), rc), :]
                ac = pc_ref[pl.ds(r0, rc), ls].astype(F32)
                sgg = _sigmoid(pc_ref[pl.ds(r0, rc), gs].astype(F32))
                dpa = dzg * sgg
                dpg = dzg * ac * sgg * (1.0 - sgg)
                dp_ref[pl.ds(r0, rc), ls] = dpa.astype(BF16)
                dp_ref[pl.ds(r0, rc), gs] = dpg.astype(BF16)
                acc_s[K, g] += fold(dpa)
                acc_s[K + 1, g] += fold(dpg)
            return carry

        lax.fori_loop(0, tr // rc, chunk, 0)
        for g, ls in enumerate(groups):
            ddw_ref[:, ls] += jnp.concatenate([jnp.sum(acc_s[k, g], axis=0, keepdims=True) for k in range(K)], axis=0)
            dbin_ref[:, ls] += jnp.sum(acc_s[K, g], axis=0, keepdims=True)
            dbin_ref[:, pl.ds(D + g * lw, lw)] += jnp.sum(acc_s[K + 1, g], axis=0, keepdims=True)

    def prev(t):
        return jnp.maximum(t * rb - 1, 0)

    def nxt(t):
        return jnp.minimum((t + 1) * rb, T // H - 1)

    in_specs = [pl.BlockSpec((tr, D), lambda t: (t, 0)), pl.BlockSpec((H, D), lambda t: (nxt(t), 0)),
                pl.BlockSpec((tr, D), lambda t: (t, 0)), pl.BlockSpec((H, D), lambda t: (nxt(t), 0)),
                pl.BlockSpec((H, D2), lambda t: (prev(t), 0)), pl.BlockSpec((tr, D2), lambda t: (t, 0)),
                pl.BlockSpec((K, D), lambda t: (0, 0)), _vec_spec(D), _vec_spec(D)]
    out_shape = (jax.ShapeDtypeStruct((T, D2), BF16), jax.ShapeDtypeStruct((1, D2), F32),
                 jax.ShapeDtypeStruct((K, D), F32), jax.ShapeDtypeStruct((1, D), F32),
                 jax.ShapeDtypeStruct((1, D), F32), jax.ShapeDtypeStruct((1, D), F32))
    out_specs = (pl.BlockSpec((tr, D2), lambda t: (t, 0)), _vec_spec(D2), pl.BlockSpec((K, D), lambda t: (0, 0)),
                 _vec_spec(D), _vec_spec(D), _vec_spec(D))
    ng = len(groups)
    scratch = [pltpu.VMEM((ng, H + tr, lw), F32), pltpu.VMEM((ng, ext, lw), F32), pltpu.VMEM((K + 2, ng, 8, lw), F32)]
    return _call(body, name, (nt,), in_specs, out_specs, out_shape, (ds, ds, zc, zc, p, p, dw, lng, lnb),
                 scratch=scratch, side=side)


_INV_SQRT2 = 1.0 / math.sqrt(2.0)
_INV_SQRT2PI = 1.0 / math.sqrt(2.0 * math.pi)


def _gelu(x):
    return 0.5 * x * (1.0 + lax.erf(x * _INV_SQRT2))


def _gelu_grad(x):
    return 0.5 * (1.0 + lax.erf(x * _INV_SQRT2)) + x * jnp.exp(-0.5 * x * x) * _INV_SQRT2PI


def _tril_mask(n, transposed=False):
    r = lax.broadcasted_iota(jnp.int32, (n, n), 0)
    c = lax.broadcasted_iota(jnp.int32, (n, n), 1)
    return (r <= c) if transposed else (r >= c)


def _sgu_fwd(p, lng, lnb, ws, bs_t, name):
    T, S2 = p.shape
    S = S2 // 2
    G, C, _ = ws.shape
    gd = S // G
    cpt = 2 if (T // C) % 2 == 0 else 1
    tr = C * cpt

    def body(p_ref, lng_ref, lnb_ref, ws_ref, bst_ref, m_ref):
        mask = _tril_mask(C)
        lg, lb = lng_ref[...], lnb_ref[...]
        bst = bst_ref[...]
        for ci in range(cpt):
            pc = p_ref[ci * C:(ci + 1) * C, :].astype(F32)
            z = _gelu(pc)
            u, v = z[:, :S], z[:, S:]
            mu = jnp.mean(v, axis=-1, keepdims=True)
            xc = v - mu
            rstd = lax.rsqrt(jnp.mean(xc * xc, axis=-1, keepdims=True) + NORM_EPS)
            vn = (xc * rstd * lg + lb).astype(BF16)
            outs = []
            for g in range(G):
                wm = jnp.where(mask, ws_ref[g], 0.0).astype(BF16)
                vs = _dot(wm, vn[:, g * gd:(g + 1) * gd], 1, 0) + bst[:, g:g + 1]
                outs.append(u[:, g * gd:(g + 1) * gd] * vs)
            m_ref[ci * C:(ci + 1) * C, :] = jnp.concatenate(outs, axis=1).astype(BF16)

    in_specs = [pl.BlockSpec((tr, S2), lambda t: (t, 0)), _vec_spec(S), _vec_spec(S),
                pl.BlockSpec((G, C, C), lambda t: (0, 0, 0)), pl.BlockSpec((C, G), lambda t: (0, 0))]
    return _call(body, name, (T // tr,), in_specs, pl.BlockSpec((tr, S), lambda t: (t, 0)),
                 jax.ShapeDtypeStruct((T, S), BF16), (p, lng, lnb, ws, bs_t))


def _sgu_bwd(p, dm, lng, lnb, ws, bs_t, name, side=None):
    T, S2 = p.shape
    S = S2 // 2
    G, C, _ = ws.shape
    gd = S // G
    nt = T // C

    def body(p_ref, dm_ref, lng_ref, lnb_ref, ws_ref, bst_ref, dp_ref, dbin_ref, dlng_ref, dlnb_ref, dws_ref, dbst_ref):
        @pl.when(pl.program_id(0) == 0)
        def _():
            for r in (dbin_ref, dlng_ref, dlnb_ref, dws_ref, dbst_ref):
                r[...] = jnp.zeros_like(r)

        mask = _tril_mask(C)
        lg, lb = lng_ref[...], lnb_ref[...]
        bst = bst_ref[...]
        pc = p_ref[...].astype(F32)
        dmv = dm_ref[...].astype(F32)
        z = _gelu(pc)
        u, v = z[:, :S], z[:, S:]
        mu = jnp.mean(v, axis=-1, keepdims=True)
        xc = v - mu
        rstd = lax.rsqrt(jnp.mean(xc * xc, axis=-1, keepdims=True) + NORM_EPS)
        vh = xc * rstd
        vn = (vh * lg + lb).astype(BF16)
        dus, dvns, dbcols = [], [], []
        for g in range(G):
            sl = slice(g * gd, (g + 1) * gd)
            wm = jnp.where(mask, ws_ref[g], 0.0).astype(BF16)
            vs = _dot(wm, vn[:, sl], 1, 0) + bst[:, g:g + 1]
            dmg = dmv[:, sl]
            dus.append(dmg * vs)
            dvs = dmg * u[:, sl]
            dbcols.append(jnp.sum(dvs, axis=-1, keepdims=True))
            dvsb = dvs.astype(BF16)
            dws_ref[g] += jnp.where(mask, _dot(dvsb, vn[:, sl], 1, 1), 0.0)
            dvns.append(_dot(wm, dvsb, 0, 0))
        dbst_ref[...] += jnp.concatenate(dbcols, axis=1)
        dvn = jnp.concatenate(dvns, axis=1)
        dlnb_ref[...] += jnp.sum(dvn, axis=0, keepdims=True)
        dlng_ref[...] += jnp.sum(dvn * vh, axis=0, keepdims=True)
        dvh = dvn * lg
        dv = rstd * (dvh - jnp.mean(dvh, axis=-1, keepdims=True) - vh * jnp.mean(dvh * vh, axis=-1, keepdims=True))
        dz = jnp.concatenate([jnp.concatenate(dus, axis=1), dv], axis=1)
        dpv = dz * _gelu_grad(pc)
        dp_ref[...] = dpv.astype(BF16)
        dbin_ref[...] += jnp.sum(dpv, axis=0, keepdims=True)

    in_specs = [pl.BlockSpec((C, S2), lambda t: (t, 0)), pl.BlockSpec((C, S), lambda t: (t, 0)),
                _vec_spec(S), _vec_spec(S), pl.BlockSpec((G, C, C), lambda t: (0, 0, 0)),
                pl.BlockSpec((C, G), lambda t: (0, 0))]
    out_shape = (jax.ShapeDtypeStruct((T, S2), BF16), jax.ShapeDtypeStruct((1, S2), F32),
                 jax.ShapeDtypeStruct((1, S), F32), jax.ShapeDtypeStruct((1, S), F32),
                 jax.ShapeDtypeStruct((G, C, C), F32), jax.ShapeDtypeStruct((C, G), F32))
    out_specs = (pl.BlockSpec((C, S2), lambda t: (t, 0)), _vec_spec(S2), _vec_spec(S), _vec_spec(S),
                 pl.BlockSpec((G, C, C), lambda t: (0, 0, 0)), pl.BlockSpec((C, G), lambda t: (0, 0)))
    return _call(body, name, (nt,), in_specs, out_specs, out_shape, (p, dm, lng, lnb, ws, bs_t), side=side)


def _alibi_slope(h, n_heads):
    return 2.0 ** (-8.0 * (h + 1) / n_heads)


def _attn_bias(n_heads):
    B = BLOCK
    qi = jnp.arange(B)[:, None]
    kj = jnp.arange(2 * B)[None, :]
    dist = qi + B - kj
    band = (dist >= 0) & (dist < B)
    slopes = jnp.array([_alibi_slope(h, n_heads) for h in range(n_heads)], F32)
    out = []
    for valid in (band & (kj >= B), band):
        b = jnp.where(valid[None], -slopes[:, None, None] * dist.astype(F32)[None], NEG_INF)
        b = jnp.where(kj[None] == 0, 0.0, b)
        out.append(b.reshape(n_heads // 2, 2, B, 2 * B).transpose(0, 2, 1, 3).reshape(n_heads // 2, B, 4 * B))
    return jnp.stack(out)


def _sink_rows(sinks):
    return jnp.pad(sinks.astype(F32)[:, None], ((0, 0), (0, BLOCK - 1)))


def _block_diag2(x):
    z = jnp.zeros_like(x)
    return jnp.concatenate([jnp.concatenate([x, z], axis=1), jnp.concatenate([z, x], axis=1)], axis=0)


def _attn_core(q_ref, kp_ref, kc_ref, vp_ref, vc_ref, sink_ref, bias_ref, n_heads):
    B, hd = BLOCK, HEAD_DIM
    ppg = n_heads // N_KV // 2
    pairs = range(n_heads // 2)
    row0 = lax.broadcasted_iota(jnp.int32, (2 * B, N_KV * hd), 0) == 0
    k2 = jnp.concatenate([kp_ref[...], kc_ref[...]], axis=0)
    v2 = jnp.concatenate([vp_ref[...], vc_ref[...]], axis=0)
    k2 = jnp.where(row0, jnp.zeros_like(k2), k2)
    v2 = jnp.where(row0, jnp.zeros_like(v2), v2)
    scale = hd ** -0.5
    kks = [_block_diag2(k2[:, kv * hd:(kv + 1) * hd] * scale) for kv in range(N_KV)]
    vvs = [_block_diag2(v2[:, kv * hd:(kv + 1) * hd]) for kv in range(N_KV)]
    first_rows = lax.broadcasted_iota(jnp.int32, (4 * B, 4 * hd), 0) < 2 * B
    first_lanes = lax.broadcasted_iota(jnp.int32, (4 * B, 4 * hd), 1) < 2 * hd
    ones = (first_rows == first_lanes).astype(BF16)
    vxs = [jnp.concatenate([vv, ones], axis=1) for vv in vvs]
    ss = [_dot(q_ref[:, pr * 2 * hd:(pr + 1) * 2 * hd], kks[pr // ppg], 1, 1) + bias_ref[pr] for pr in pairs]
    halves = []
    for h in range(n_heads):
        s = ss[h // 2][:, (h % 2) * 2 * B:(h % 2 + 1) * 2 * B]
        halves.append(jnp.concatenate([s[:, :B] + sink_ref[h:h + 1, :], s[:, B:]], axis=1))
    mxs = [jnp.max(s, axis=-1, keepdims=True) for s in halves]
    es = [jnp.exp(s - m) for s, m in zip(halves, mxs)]
    rs = [_dot(jnp.concatenate([es[2 * pr], es[2 * pr + 1]], axis=1).astype(BF16), vxs[pr // ppg], 1, 0) for pr in pairs]
    return kks, vvs, es, rs


def _attn_fwd(qkv, sink_rows, bias, n_heads, name, side=None):
    T = qkv.shape[0]
    B = BLOCK
    hd = HEAD_DIM
    HQ = n_heads * hd
    KVW = N_KV * hd
    assert (n_heads // N_KV) % 2 == 0, "heads are processed in pairs that share a key/value head"
    nb = T // B
    kcol = HQ // KVW

    def body(q_ref, kp_ref, kc_ref, vp_ref, vc_ref, sink_ref, bias_ref, o_ref):
        _, _, _, rs = _attn_core(q_ref, kp_ref, kc_ref, vp_ref, vc_ref, sink_ref, bias_ref, n_heads)
        low = lax.broadcasted_iota(jnp.int32, (B, 2 * hd), 1) < hd
        for pr, r in enumerate(rs):
            inv = jnp.where(low, 1.0 / r[:, 2 * hd:4 * hd], 1.0 / r[:, 4 * hd:])
            o_ref[:, pr * 2 * hd:(pr + 1) * 2 * hd] = (r[:, :2 * hd] * inv).astype(BF16)

    def prev(n):
        return jnp.maximum(n - 1, 0)

    in_specs = [pl.BlockSpec((B, HQ), lambda n: (n, 0)),
                pl.BlockSpec((B, KVW), lambda n: (prev(n), kcol)), pl.BlockSpec((B, KVW), lambda n: (n, kcol)),
                pl.BlockSpec((B, KVW), lambda n: (prev(n), kcol + 1)), pl.BlockSpec((B, KVW), lambda n: (n, kcol + 1)),
                pl.BlockSpec((n_heads, B), lambda n: (0, 0)),
                pl.BlockSpec((None, n_heads // 2, B, 4 * B), lambda n: (jnp.minimum(n, 1), 0, 0, 0))]
    return _call(body, name, (nb,), in_specs, pl.BlockSpec((B, HQ), lambda n: (n, 0)),
                 jax.ShapeDtypeStruct((T, HQ), BF16), (qkv, qkv, qkv, qkv, qkv, sink_rows, bias), side=side)


def _attn_bwd(qkv, do, sink_rows, bias, n_heads, name, side=None):
    T = qkv.shape[0]
    B = BLOCK
    hd = HEAD_DIM
    HQ = n_heads * hd
    KVW = N_KV * hd
    group = n_heads // N_KV
    assert group % 2 == 0, "heads are processed in pairs that share a key/value head"
    ppg = group // 2
    nb = T // B
    kcol = HQ // KVW
    scale = hd ** -0.5

    def body(q_ref, kp_ref, kc_ref, vp_ref, vc_ref, do_ref, sink_ref, bias_ref,
             dq_ref, dkv_ref, dbq_ref, dbk_ref, dbv_ref, dsink_ref, ck_ref, cv_ref):
        n = pl.program_id(0)
        first = n == 0

        @pl.when(first)
        def _():
            for r in (dbq_ref, dbk_ref, dbv_ref, dsink_ref, ck_ref, cv_ref):
                r[...] = jnp.zeros_like(r)

        @pl.when(n < nb)
        def _():
            pairs = range(n_heads // 2)
            heads = range(n_heads)
            lanes = [slice(pr * 2 * hd, (pr + 1) * 2 * hd) for pr in pairs]
            kks, vvs, es, rs = _attn_core(q_ref, kp_ref, kc_ref, vp_ref, vc_ref, sink_ref, bias_ref, n_heads)
            low = lax.broadcasted_iota(jnp.int32, (B, 2 * hd), 1) < hd
            slot0 = lax.broadcasted_iota(jnp.int32, (B, B), 1) == 0
            lane_head = lax.broadcasted_iota(jnp.int32, (2 * hd, 4 * hd), 0) < hd
            out_head = lax.broadcasted_iota(jnp.int32, (2 * hd, 4 * hd), 1) < 2 * hd
            ones2 = (lane_head == out_head).astype(BF16)
            invs = [1.0 / rs[h // 2][:, (2 + 2 * (h % 2)) * hd:(4 + 2 * (h % 2)) * hd] for h in heads]
            dops = [do_ref[:, lanes[pr]] for pr in pairs]
            qps = [q_ref[:, lanes[pr]] for pr in pairs]
            dos = [dops[pr].astype(F32) * rs[pr][:, :2 * hd] * jnp.where(low, invs[2 * pr], invs[2 * pr + 1])
                   for pr in pairs]
            his = [t.astype(BF16) for t in dos]
            los = [(t - hi.astype(F32)).astype(BF16) for t, hi in zip(dos, his)]
            dsums = [_dot(hi, ones2, 1, 0) + _dot(lo, ones2, 1, 0) for hi, lo in zip(his, los)]
            dps = [_dot(dops[pr], vvs[pr // ppg], 1, 1) for pr in pairs]
            wide = lambda a: jnp.concatenate([a, a], axis=1)
            p_s = [es[h] * wide(invs[h]) for h in heads]
            ds_s = [p_s[h] * (dps[h // 2][:, (h % 2) * 2 * B:(h % 2 + 1) * 2 * B]
                              - wide(dsums[h // 2][:, (h % 2) * 2 * hd:(h % 2 + 1) * 2 * hd])) for h in heads]
            for h in heads:
                dsink_ref[h:h + 1, :] += jnp.sum(ds_s[h][:, :B], axis=0, keepdims=True)

            def without_slot0(a):
                return jnp.concatenate([jnp.where(slot0, 0.0, a[:, :B]), a[:, B:]], axis=1).astype(BF16)

            ds_b = [without_slot0(a) for a in ds_s]
            p_b = [without_slot0(a) for a in p_s]
            dqs = [_dot(jnp.concatenate([ds_b[2 * pr], ds_b[2 * pr + 1]], axis=1), kks[pr // ppg], 1, 0) for pr in pairs]
            for pr in pairs:
                dq_ref[:, lanes[pr]] = dqs[pr].astype(BF16)
                dbq_ref[:, lanes[pr]] += jnp.sum(dqs[pr], axis=0, keepdims=True)
            keeps = [low, jnp.logical_not(low)]
            qms = [jnp.where(keeps[h % 2], qps[h // 2], jnp.zeros_like(qps[0])) for h in heads]
            dms = [jnp.where(keeps[h % 2], dops[h // 2], jnp.zeros_like(dops[0])) for h in heads]
            dkh = [_dot(ds_b[h], qms[h], 0, 0) for h in heads]
            dvh = [_dot(p_b[h], dms[h], 0, 0) for h in heads]
            dks, dvs = [], []
            for kv in range(N_KV):
                tk = functools.reduce(jnp.add, dkh[kv * group:(kv + 1) * group])
                tv = functools.reduce(jnp.add, dvh[kv * group:(kv + 1) * group])
                dks.append((tk[:, :hd] + tk[:, hd:]) * scale)
                dvs.append(tv[:, :hd] + tv[:, hd:])
            dk2 = jnp.concatenate(dks, axis=1)
            dv2 = jnp.concatenate(dvs, axis=1)
            dbk_ref[...] += jnp.sum(dk2, axis=0, keepdims=True)
            dbv_ref[...] += jnp.sum(dv2, axis=0, keepdims=True)
            dkv_ref[:, :KVW] = (ck_ref[...] + dk2[:B]).astype(BF16)
            dkv_ref[:, KVW:] = (cv_ref[...] + dv2[:B]).astype(BF16)
            ck_ref[...] = dk2[B:]
            cv_ref[...] = dv2[B:]

        @pl.when(n == nb)
        def _():
            dkv_ref[:, :KVW] = ck_ref[...].astype(BF16)
            dkv_ref[:, KVW:] = cv_ref[...].astype(BF16)

    def cur(n):
        return jnp.minimum(n, nb - 1)

    def prev(n):
        return jnp.maximum(cur(n) - 1, 0)

    def outp(n):
        return jnp.maximum(n - 1, 0)

    in_specs = [pl.BlockSpec((B, HQ), lambda n: (cur(n), 0)),
                pl.BlockSpec((B, KVW), lambda n: (prev(n), kcol)), pl.BlockSpec((B, KVW), lambda n: (cur(n), kcol)),
                pl.BlockSpec((B, KVW), lambda n: (prev(n), kcol + 1)), pl.BlockSpec((B, KVW), lambda n: (cur(n), kcol + 1)),
                pl.BlockSpec((B, HQ), lambda n: (cur(n), 0)), pl.BlockSpec((n_heads, B), lambda n: (0, 0)),
                pl.BlockSpec((None, n_heads // 2, B, 4 * B), lambda n: (jnp.minimum(n, 1), 0, 0, 0))]
    out_shape = (jax.ShapeDtypeStruct((T, HQ), BF16), jax.ShapeDtypeStruct((T, 2 * KVW), BF16),
                 jax.ShapeDtypeStruct((1, HQ), F32), jax.ShapeDtypeStruct((1, KVW), F32),
                 jax.ShapeDtypeStruct((1, KVW), F32), jax.ShapeDtypeStruct((n_heads, B), F32))
    out_specs = (pl.BlockSpec((B, HQ), lambda n: (cur(n), 0)), pl.BlockSpec((B, 2 * KVW), lambda n: (outp(n), 0)),
                 _vec_spec(HQ), _vec_spec(KVW), _vec_spec(KVW), pl.BlockSpec((n_heads, B), lambda n: (0, 0)))
    return _call(body, name, (nb + 1,), in_specs, out_specs, out_shape, (qkv, qkv, qkv, qkv, qkv, do, sink_rows, bias),
                 scratch=[pltpu.VMEM((B, KVW), F32), pltpu.VMEM((B, KVW), F32)], side=side)


def _sum8(r, name):
    _, R, C = r.shape
    tr = _tile(R, 512, 16)

    def body(r_ref, o_ref):
        acc = r_ref[0].astype(F32)
        for d in range(1, N_DEV):
            acc = acc + r_ref[d].astype(F32)
        o_ref[...] = acc

    return _call(body, name, (R // tr,), [pl.BlockSpec((N_DEV, tr, C), lambda i: (0, i, 0))],
                 pl.BlockSpec((tr, C), lambda i: (i, 0)), jax.ShapeDtypeStruct((R, C), F32), (r,))


def _adamw(w, g, m, v, name):
    R, C = w.shape
    tr = _tile(R, 512, 8)
    c1 = 1.0 - ADAM_B1 ** ADAM_STEP
    c2 = 1.0 - ADAM_B2 ** ADAM_STEP

    def body(w_ref, g_ref, m_ref, v_ref, d_ref, nm_ref, nv_ref):
        gv = g_ref[...]
        nm = ADAM_B1 * m_ref[...] + (1.0 - ADAM_B1) * gv
        nv = ADAM_B2 * v_ref[...] + (1.0 - ADAM_B2) * (gv * gv)
        nm_ref[...] = nm
        nv_ref[...] = nv
        d_ref[...] = -ADAM_LR * ((nm / c1) / (jnp.sqrt(nv / c2) + ADAM_EPS) + ADAM_WD * w_ref[...])

    spec = pl.BlockSpec((tr, C), lambda i: (i, 0))
    shp = jax.ShapeDtypeStruct((R, C), F32)
    return _call(body, name, (R // tr,), [spec] * 4, (spec,) * 3, (shp, shp, shp), (w, g, m, v))


def _adamw_nd(w, g, m, v, name):
    shape = w.shape
    c = shape[-1]
    f = lambda a: a.reshape(-1, c)
    d, nm, nv = _adamw(f(w), f(g), f(m), f(v), name)
    return d.reshape(shape), nm.reshape(shape), nv.reshape(shape)


def _pack(arrs, width):
    flat = jnp.concatenate([a.reshape(-1).astype(F32) for a in arrs])
    n = flat.shape[0]
    quantum = 8 * width
    total = -(-n // quantum) * quantum
    return jnp.pad(flat, (0, total - n)).reshape(-1, width)


def _unpack(flat, shapes):
    out, off = [], 0
    for s in shapes:
        n = math.prod(s)
        out.append(flat[off:off + n].reshape(s))
        off += n
    return out


def kernel(x, c, norm1_g, norm2_g, ada_w, ada_b, attn_wqkv, attn_bqkv, attn_sinks, attn_wo, attn_bo, conv_w_in, conv_b_in, conv_dw, conv_dw_b, conv_ln_g, conv_ln_b, conv_w_out, conv_b_out, sgu_w_in, sgu_b_in, sgu_ln_g, sgu_ln_b, sgu_ws, sgu_bs, sgu_w_out, sgu_b_out, ffn_w_in, ffn_dw, ffn_dw_b, ffn_w_out, final_g, loss_target, m_norm1_g, m_norm2_g, m_ada_w, m_ada_b, m_attn_wqkv, m_attn_bqkv, m_attn_sinks, m_attn_wo, m_attn_bo, m_conv_w_in, m_conv_b_in, m_conv_dw, m_conv_dw_b, m_conv_ln_g, m_conv_ln_b, m_conv_w_out, m_conv_b_out, m_sgu_w_in, m_sgu_b_in, m_sgu_ln_g, m_sgu_ln_b, m_sgu_ws, m_sgu_bs, m_sgu_w_out, m_sgu_b_out, m_ffn_w_in, m_ffn_dw, m_ffn_dw_b, m_ffn_w_out, m_final_g, v_norm1_g, v_norm2_g, v_ada_w, v_ada_b, v_attn_wqkv, v_attn_bqkv, v_attn_sinks, v_attn_wo, v_attn_bo, v_conv_w_in, v_conv_b_in, v_conv_dw, v_conv_dw_b, v_conv_ln_g, v_conv_ln_b, v_conv_w_out, v_conv_b_out, v_sgu_w_in, v_sgu_b_in, v_sgu_ln_g, v_sgu_ln_b, v_sgu_ws, v_sgu_bs, v_sgu_w_out, v_sgu_b_out, v_ffn_w_in, v_ffn_dw, v_ffn_dw_b, v_ffn_w_out, v_final_g):
    W = dict(norm1_g=norm1_g, norm2_g=norm2_g, ada_w=ada_w, ada_b=ada_b, attn_wqkv=attn_wqkv, attn_bqkv=attn_bqkv, attn_sinks=attn_sinks, attn_wo=attn_wo, attn_bo=attn_bo, conv_w_in=conv_w_in, conv_b_in=conv_b_in, conv_dw=conv_dw, conv_dw_b=conv_dw_b, conv_ln_g=conv_ln_g, conv_ln_b=conv_ln_b, conv_w_out=conv_w_out, conv_b_out=conv_b_out, sgu_w_in=sgu_w_in, sgu_b_in=sgu_b_in, sgu_ln_g=sgu_ln_g, sgu_ln_b=sgu_ln_b, sgu_ws=sgu_ws, sgu_bs=sgu_bs, sgu_w_out=sgu_w_out, sgu_b_out=sgu_b_out, ffn_w_in=ffn_w_in, ffn_dw=ffn_dw, ffn_dw_b=ffn_dw_b, ffn_w_out=ffn_w_out, final_g=final_g)
    MOM = dict(norm1_g=m_norm1_g, norm2_g=m_norm2_g, ada_w=m_ada_w, ada_b=m_ada_b, attn_wqkv=m_attn_wqkv, attn_bqkv=m_attn_bqkv, attn_sinks=m_attn_sinks, attn_wo=m_attn_wo, attn_bo=m_attn_bo, conv_w_in=m_conv_w_in, conv_b_in=m_conv_b_in, conv_dw=m_conv_dw, conv_dw_b=m_conv_dw_b, conv_ln_g=m_conv_ln_g, conv_ln_b=m_conv_ln_b, conv_w_out=m_conv_w_out, conv_b_out=m_conv_b_out, sgu_w_in=m_sgu_w_in, sgu_b_in=m_sgu_b_in, sgu_ln_g=m_sgu_ln_g, sgu_ln_b=m_sgu_ln_b, sgu_ws=m_sgu_ws, sgu_bs=m_sgu_bs, sgu_w_out=m_sgu_w_out, sgu_b_out=m_sgu_b_out, ffn_w_in=m_ffn_w_in, ffn_dw=m_ffn_dw, ffn_dw_b=m_ffn_dw_b, ffn_w_out=m_ffn_w_out, final_g=m_final_g)
    VAR = dict(norm1_g=v_norm1_g, norm2_g=v_norm2_g, ada_w=v_ada_w, ada_b=v_ada_b, attn_wqkv=v_attn_wqkv, attn_bqkv=v_attn_bqkv, attn_sinks=v_attn_sinks, attn_wo=v_attn_wo, attn_bo=v_attn_bo, conv_w_in=v_conv_w_in, conv_b_in=v_conv_b_in, conv_dw=v_conv_dw, conv_dw_b=v_conv_dw_b, conv_ln_g=v_conv_ln_g, conv_ln_b=v_conv_ln_b, conv_w_out=v_conv_w_out, conv_b_out=v_conv_b_out, sgu_w_in=v_sgu_w_in, sgu_b_in=v_sgu_b_in, sgu_ln_g=v_sgu_ln_g, sgu_ln_b=v_sgu_ln_b, sgu_ws=v_sgu_ws, sgu_bs=v_sgu_bs, sgu_w_out=v_sgu_w_out, sgu_b_out=v_sgu_b_out, ffn_w_in=v_ffn_w_in, ffn_dw=v_ffn_dw, ffn_dw_b=v_ffn_dw_b, ffn_w_out=v_ffn_w_out, final_g=v_final_g)
    ORDER = list(W)

    _, T, D = x.shape
    depth = norm1_g.shape[0]
    n_heads = D // HEAD_DIM
    me = 4 * lax.axis_index("x") + 2 * lax.axis_index("y") + lax.axis_index("c")
    x0 = x.reshape(T, D)
    tgt = loss_target.reshape(T, D)

    small_sharded = ["attn_bqkv", "attn_bo", "conv_dw", "sgu_b_in", "sgu_ln_g", "sgu_ln_b", "sgu_b_out", "ffn_dw"]
    s_in = [c] + [W[n] for n in small_sharded]
    s_shapes = [a.shape for a in s_in]
    gathered_small, first_landed = _exchange("gather", [_pack(s_in, 128), attn_wqkv[0].T.astype(BF16)], "gather_first")
    gathered_small = gathered_small.reshape(N_DEV, -1)
    s_offs = [sum(math.prod(s) for s in s_shapes[:i]) for i in range(len(s_shapes))]

    def full(idx):
        shp = s_shapes[idx]
        a = gathered_small[:, s_offs[idx]:s_offs[idx] + math.prod(shp)].reshape((N_DEV,) + shp)
        return jnp.moveaxis(a, 0, -2).reshape(shp[:-1] + (N_DEV * shp[-1],))

    c_all = full(0).reshape(N_DEV, D)
    F_small = {n: full(1 + i) for i, n in enumerate(small_sharded)}
    attn_bias = _attn_bias(n_heads)

    c_act = c_all * jax.nn.sigmoid(c_all)
    c_pad = jnp.pad(c_act, ((0, 8), (0, 0))).astype(BF16)
    n_ada = ada_w.shape[-1]
    ada_cols = lax.dynamic_slice_in_dim(ada_b, me * n_ada, n_ada, axis=1)
    mod_loc = jnp.stack([_mm(c_pad, ada_w[i].astype(BF16), "nn", "ada_mod", bias=ada_cols[i:i + 1])
                         for i in range(depth)])
    mod_all = _all_gather(mod_loc, "gather_mod")
    mod_mine = lax.dynamic_index_in_dim(mod_all, me, axis=2, keepdims=False)
    mod = jnp.transpose(mod_mine, (1, 0, 2)).reshape(depth, 6, 1, D)

    mixer_names = {0: ("attn_wqkv", "attn_wo"), 1: ("conv_w_in", "conv_w_out"), 2: ("sgu_w_in", "sgu_w_out")}

    def piece(part, i, which):
        w_in, w_out = ("ffn_w_in", "ffn_w_out") if part == "ffn" else mixer_names[i % 3]
        l = i if part == "ffn" else i // 3
        return (w_in, l, W[w_in].shape[-1]) if which == "in" else (w_out, l, W[w_out].shape[1])

    def both(part, i):
        return [(part, i, "in"), (part, i, "out")] if 0 <= i < depth else []

    def local_weight(key):
        n, l, _ = piece(*key)
        return (W[n][l].T if key[2] == "in" else W[n][l]).astype(BF16)

    WF = {}

    def land_weights(keys, gathered):
        for key, g in zip(keys, gathered):
            n, l, r = piece(*key)
            WF[(n, l)] = g.reshape(N_DEV * r, D)

    def with_gather(keys, run):
        if not keys:
            return run(None)
        *outs, landed = run(("gather", [local_weight(k) for k in keys]))
        land_weights(keys, landed)
        return outs[0] if len(outs) == 1 else tuple(outs)

    def vec(a):
        return a.reshape(1, -1)

    land_weights(both("mix", 0)[:1], [first_landed])

    saved = []
    xs, y_prev, gate_prev = x0, None, None
    for i in range(depth):
        sh1, sc1, g1, sh2, sc2, g2 = [mod[i, k] for k in range(6)]
        kind, j = i % 3, i // 3
        st = dict(kind=kind, j=j)
        if y_prev is None:
            h1 = _norm_mod_fwd(xs, vec(norm1_g[i]), sc1, sh1, "norm1_fwd")
        else:
            xs, h1 = _norm_mod_fwd(xs, vec(norm1_g[i]), sc1, sh1, "norm1_fwd", y=y_prev, gate=gate_prev)
        st.update(x_in=xs, h1=h1)
        first_ffn_in = [("ffn", 0, "in")] if i == 0 else []
        if kind == 0:
            qkv = with_gather(both("mix", 0)[1:] if i == 0 else [], lambda side: _mm(
                h1, WF[("attn_wqkv", j)], "nt", "attn_qkv", out_dtype=BF16, bias=vec(F_small["attn_bqkv"][j]), side=side))
            o = with_gather(first_ffn_in, lambda side: _attn_fwd(qkv, _sink_rows(attn_sinks[j]), attn_bias, n_heads,
                                                                 "attn_fwd", side=side))
            y1 = _mm(o, WF[("attn_wo", j)], "nn", "attn_out", out_dtype=BF16, bias=vec(F_small["attn_bo"][j]))
            st.update(qkv=qkv, o=o)
        elif kind == 1:
            p = _mm(h1, WF[("conv_w_in", j)], "nt", "conv_in", out_dtype=BF16, bias=vec(conv_b_in[j]))
            zc, s = with_gather(first_ffn_in, lambda side: _convmix_fwd(
                p, F_small["conv_dw"][j], vec(conv_dw_b[j]), vec(conv_ln_g[j]), vec(conv_ln_b[j]), "convmix_fwd", side=side))
            y1 = _mm(s, WF[("conv_w_out", j)], "nn", "conv_out", out_dtype=BF16, bias=vec(conv_b_out[j]))
            st.update(p=p, zc=zc, s=s)
        else:
            p = _mm(h1, WF[("sgu_w_in", j)], "nt", "sgu_in", out_dtype=BF16, bias=vec(F_small["sgu_b_in"][j]))
            mm_ = _sgu_fwd(p, vec(F_small["sgu_ln_g"][j]), vec(F_small["sgu_ln_b"][j]), sgu_ws[j], sgu_bs[j].T, "sgu_fwd")
            y1 = _mm(mm_, WF[("sgu_w_out", j)], "nn", "sgu_out", out_dtype=BF16, bias=vec(F_small["sgu_b_out"][j]))
            st.update(p=p, m=mm_)
        xs, h2 = _norm_mod_fwd(xs, vec(norm2_g[i]), sc2, sh2, "norm2_fwd", y=y1, gate=g1)
        nxt = i + 1
        z = with_gather(both("mix", nxt) + ([("ffn", 0, "out")] if i == 0 else []),
                        lambda side: _mm(h2, WF[("ffn_w_in", i)], "nt", "ffn_in", out_dtype=BF16, side=side))
        a = with_gather(both("ffn", nxt)[:1],
                        lambda side: _ffn_mid_fwd(z, F_small["ffn_dw"][i], vec(ffn_dw_b[i]), "ffn_mid_fwd", side=side))
        y2 = with_gather(both("ffn", nxt)[1:], lambda side: _mm(a, WF[("ffn_w_out", i)], "nn", "ffn_out",
                                                                      out_dtype=BF16, side=side))
        st.update(y1=y1, x_mid=xs, h2=h2, z=z, a=a, y2=y2)
        saved.append(st)
        y_prev, gate_prev = y2, g2

    dx, d_final_g, loss_row, dy2, dg2 = _final_loss(xs, y_prev, gate_prev, tgt, vec(final_g), "final_loss")
    loss = lax.psum(loss_row[0, 0], ("x", "y", "c"))

    G = {n: [None] * W[n].shape[0] for n in ORDER if n != "final_g"}
    GW = {}
    recv = {}
    dmod = [None] * depth

    def with_scatter(keys, run):
        if not keys:
            return run(None)
        sends = [GW[piece(*k)[:2]].reshape(N_DEV, piece(*k)[2], D) for k in keys]
        *outs, landed = run(("scatter", sends))
        recv.update(zip(keys, landed))
        return outs[0] if len(outs) == 1 else tuple(outs)

    for i in reversed(range(depth)):
        st = saved[i]
        sh1, sc1, g1, sh2, sc2, g2 = [mod[i, k] for k in range(6)]
        kind, j = st["kind"], st["j"]
        last_out = [("mix", 0, "out")] if i == 0 else []
        last_in = [("mix", 0, "in")] if i == 0 else []
        GW[("ffn_w_out", i)] = _mm(st["a"], dy2, "tn", "ffn_out_dw", out_dtype=BF16)
        da = _mm(dy2, WF[("ffn_w_out", i)], "nt", "ffn_out_dx", out_dtype=BF16)
        dzg, dzu, ddw, ddwb = with_scatter(both("mix", i + 1) + [("ffn", i, "out")], lambda side: _ffn_mid_bwd(
            st["z"], da, F_small["ffn_dw"][i], vec(ffn_dw_b[i]), "ffn_mid_bwd", side=side))
        G["ffn_dw"][i], G["ffn_dw_b"][i] = ddw, ddwb[0]
        GW[("ffn_w_in", i)] = _mm(dzg, st["h2"], "tn", "ffn_in_dw", out_dtype=BF16, a2=dzu)
        dh2 = with_scatter([("ffn", i, "in")],
                           lambda side: _mm(dzg, WF[("ffn_w_in", i)], "nn", "ffn_in_dx", out_dtype=BF16, a2=dzu, side=side))
        dx, dn2, dsc2, dsh2, dy1, dg1, dbo = _norm_mod_bwd(dh2, st["x_mid"], vec(norm2_g[i]), sc2, sh2, dx, "norm_bwd",
                                                           y_up=st["y1"], gate_up=g1)
        G["norm2_g"][i] = dn2[0]
        if kind == 0:
            G["attn_bo"][j] = dbo[0]
            GW[("attn_wo", j)] = _mm(st["o"], dy1, "tn", "attn_out_dw", out_dtype=BF16)
            do = _mm(dy1, WF[("attn_wo", j)], "nt", "attn_out_dx", out_dtype=BF16)
            dq, dkv, dbq, dbk, dbv, dsk = with_scatter(last_out, lambda side: _attn_bwd(
                st["qkv"], do, _sink_rows(attn_sinks[j]), attn_bias, n_heads, "attn_bwd", side=side))
            G["attn_bqkv"][j] = jnp.concatenate([dbq, dbk, dbv], axis=1)[0]
            G["attn_sinks"][j] = dsk[:, 0]
            GW[("attn_wqkv", j)] = _mm(dq, st["h1"], "tn", "attn_qkv_dw", out_dtype=BF16, a2=dkv)
            dh1 = with_scatter(last_in, lambda side: _mm(dq, WF[("attn_wqkv", j)], "nn", "attn_qkv_dx", out_dtype=BF16,
                                                         a2=dkv, side=side))
        elif kind == 1:
            G["conv_b_out"][j] = dbo[0]
            GW[("conv_w_out", j)] = _mm(st["s"], dy1, "tn", "conv_out_dw", out_dtype=BF16)
            ds = _mm(dy1, WF[("conv_w_out", j)], "nt", "conv_out_dx", out_dtype=BF16)
            dp, dbin, ddw, ddwb, dlng, dlnb = with_scatter(last_out, lambda side: _convmix_bwd(
                ds, st["zc"], st["p"], F_small["conv_dw"][j], vec(conv_ln_g[j]), vec(conv_ln_b[j]), "convmix_bwd",
                side=side))
            G["conv_b_in"][j], G["conv_dw"][j], G["conv_dw_b"][j] = dbin[0], ddw, ddwb[0]
            G["conv_ln_g"][j], G["conv_ln_b"][j] = dlng[0], dlnb[0]
            GW[("conv_w_in", j)] = _mm(dp, st["h1"], "tn", "conv_in_dw", out_dtype=BF16)
            dh1 = with_scatter(last_in, lambda side: _mm(dp, WF[("conv_w_in", j)], "nn", "conv_in_dx", out_dtype=BF16,
                                                         side=side))
        else:
            G["sgu_b_out"][j] = dbo[0]
            GW[("sgu_w_out", j)] = _mm(st["m"], dy1, "tn", "sgu_out_dw", out_dtype=BF16)
            dm = _mm(dy1, WF[("sgu_w_out", j)], "nt", "sgu_out_dx", out_dtype=BF16)
            dp, dbin, dlng, dlnb, dws, dbst = with_scatter(last_out, lambda side: _sgu_bwd(
                st["p"], dm, vec(F_small["sgu_ln_g"][j]), vec(F_small["sgu_ln_b"][j]), sgu_ws[j], sgu_bs[j].T, "sgu_bwd",
                side=side))
            G["sgu_b_in"][j], G["sgu_ln_g"][j], G["sgu_ln_b"][j] = dbin[0], dlng[0], dlnb[0]
            G["sgu_ws"][j], G["sgu_bs"][j] = dws, dbst.T
            GW[("sgu_w_in", j)] = _mm(dp, st["h1"], "tn", "sgu_in_dw", out_dtype=BF16)
            dh1 = with_scatter(last_in, lambda side: _mm(dp, WF[("sgu_w_in", j)], "nn", "sgu_in_dx", out_dtype=BF16,
                                                         side=side))
        dmod_i = lambda: jnp.concatenate([dsh1, dsc1, dg1, dsh2, dsc2, dg2], axis=1)[0]
        if i > 0:
            dx, dn1, dsc1, dsh1, dy2_below, dg2_below, _ = _norm_mod_bwd(
                dh1, st["x_in"], vec(norm1_g[i]), sc1, sh1, dx, "norm_bwd", y_up=saved[i - 1]["y2"], gate_up=mod[i - 1, 5])
            dmod[i] = dmod_i()
            dy2, dg2 = dy2_below, dg2_below
        else:
            dx, dn1, dsc1, dsh1 = _norm_mod_bwd(dh1, st["x_in"], vec(norm1_g[i]), sc1, sh1, dx, "norm_bwd")
            dmod[i] = dmod_i()
        G["norm1_g"][i] = dn1[0]
    grad_x = dx.reshape(x.shape)

    small_names = [n for n in ORDER if n not in
                   ("ada_w", "ada_b", "attn_wqkv", "attn_wo", "conv_w_in", "conv_w_out", "sgu_w_in", "sgu_w_out",
                    "ffn_w_in", "ffn_w_out", "final_g")]
    g_small = {n: jnp.stack(G[n]) for n in small_names}
    g_small["final_g"] = d_final_g[0]
    g_small["ada_b"] = jnp.stack(dmod)
    names16 = ["ffn_dw", "conv_dw", "sgu_ws", "ffn_dw_b", "ada_b"]
    names32 = [n for n in g_small if n not in names16]
    shapes16, shapes32 = [g_small[n].shape for n in names16], [g_small[n].shape for n in names32]
    gathered32, gathered16 = _exchange("gather", [_pack([g_small[n] for n in names32], 1024),
                                                  _pack([g_small[n] for n in names16], 1024).astype(BF16)],
                                       "gather_small_grads")
    gsum = dict(zip(names32, _unpack(_sum8(gathered32, "sum_small_grads").reshape(-1), shapes32)))
    gsum.update(zip(names16, _unpack(_sum8(gathered16, "sum_small_grads").reshape(-1), shapes16)))

    def local_shard(n, a):
        if n in small_sharded:
            w = W[n].shape[-1]
            return lax.dynamic_slice_in_dim(a, me * w, w, axis=a.ndim - 1)
        return a

    gsum = {n: local_shard(n, a) for n, a in gsum.items()}

    off16 = sum(math.prod(s) for s in shapes16[:-1])
    dmod_all = gathered16.reshape(N_DEV, -1)[:, off16:off16 + math.prod(shapes16[-1])].reshape((N_DEV,) + shapes16[-1])
    dmod_cols = lax.dynamic_slice_in_dim(dmod_all, me * n_ada, n_ada, axis=2)
    dmod_pad = jnp.pad(dmod_cols, ((0, 8), (0, 0), (0, 0)))
    g_ada_w = jnp.stack([_mm(c_pad, dmod_pad[:, i, :], "tn", "ada_dw") for i in range(depth)])

    big = {}
    for i in range(depth):
        for key in both("mix", i) + both("ffn", i):
            n, l, _ = piece(*key)
            gp = _sum8(recv[key], "sum_weight_grads")
            big.setdefault(n, {})[l] = gp.T if key[2] == "in" else gp
    grads = {n: jnp.stack([v[l] for l in range(len(v))]) for n, v in big.items()}
    grads["ada_w"] = g_ada_w
    grads.update(gsum)

    delta, new_m, new_v = {}, {}, {}
    big_names = ["ada_w", "attn_wqkv", "attn_wo", "conv_w_in", "conv_w_out", "sgu_w_in", "sgu_w_out", "ffn_w_in",
                 "ffn_w_out"]
    for n in big_names:
        delta[n], new_m[n], new_v[n] = _adamw_nd(W[n], grads[n], MOM[n], VAR[n], "adamw_" + n)
    rest = [n for n in ORDER if n not in big_names]
    rest_shapes = [W[n].shape for n in rest]
    pk = lambda d: _pack([d[n] for n in rest], 128)
    d_s, m_s, v_s = _adamw(pk(W), pk(grads), pk(MOM), pk(VAR), "adamw_small")
    for n, a, b_, c_ in zip(rest, _unpack(d_s.reshape(-1), rest_shapes), _unpack(m_s.reshape(-1), rest_shapes),
                            _unpack(v_s.reshape(-1), rest_shapes)):
        delta[n], new_m[n], new_v[n] = a, b_, c_

    return (loss, grad_x, *[grads[n] for n in ORDER], *[delta[n] for n in ORDER],
            *[new_m[n] for n in ORDER], *[new_v[n] for n in ORDER])
```

```python
import functools
import math

import jax
import jax.numpy as jnp
from jax import lax
from jax.experimental import pallas as pl
from jax.experimental.pallas import tpu as pltpu

F32 = jnp.float32
BF16 = jnp.bfloat16

N_DEV = 8
HEAD_DIM = 64
N_KV = 4
BLOCK = 128
NORM_EPS = 1e-6
NEG_INF = -1e30
ADAM_LR = 0.001
ADAM_B1 = 0.9
ADAM_B2 = 0.999
ADAM_EPS = 1e-08
ADAM_WD = 0.01
ADAM_STEP = 10
V7X_VMEM_LIMIT = 56 * 1024 * 1024
MESH = pl.DeviceIdType.MESH


def _cparams():
    return pltpu.CompilerParams(vmem_limit_bytes=V7X_VMEM_LIMIT)


def _tile(n, cap, align):
    best = None
    for t in range(align, min(n, cap) + 1, align):
        if n % t == 0:
            best = t
    return best if best is not None else n


def _dot(a, b, ca, cb):
    return lax.dot_general(a, b, (((ca,), (cb,)), ((), ())), preferred_element_type=F32)


def _sigmoid(x):
    return 1.0 / (1.0 + jnp.exp(-x))


def _rows(x, start, n):
    return lax.slice_in_dim(x, start, start + n, axis=0)


def _my_coords():
    return lax.axis_index("x"), lax.axis_index("y"), lax.axis_index("c")


def _peer(m, mx, my, mc):
    px = (mx + ((m >> 2) & 1)) % 2
    py = (my + ((m >> 1) & 1)) % 2
    pc = (mc + (m & 1)) % 2
    return px, py, pc


def _exchange_copies(kind, x_ref, o_ref, send_sems, recv_sems, local_sem):
    mx, my, mc = _my_coords()
    me = 4 * mx + 2 * my + mc
    local = pltpu.make_async_copy(x_ref if kind == "gather" else x_ref.at[me], o_ref.at[me], local_sem)
    remote = []
    for m in range(1, N_DEV):
        px, py, pc = _peer(m, mx, my, mc)
        src = x_ref if kind == "gather" else x_ref.at[4 * px + 2 * py + pc]
        remote.append(pltpu.make_async_remote_copy(
            src_ref=src, dst_ref=o_ref.at[me], send_sem=send_sems.at[m - 1], recv_sem=recv_sems.at[m - 1],
            device_id=(px, py, pc), device_id_type=MESH))
    return local, remote


def _exchange_start(kind, *refs):
    local, remote = _exchange_copies(kind, *refs)
    local.start()
    for cp in remote:
        cp.start()


def _exchange_wait(kind, *refs):
    local, remote = _exchange_copies(kind, *refs)
    for cp in remote:
        cp.wait_recv()
    for cp in remote:
        cp.wait_send()
    local.wait()


_EXCHANGE_SEMS = [pltpu.SemaphoreType.DMA((N_DEV - 1,)), pltpu.SemaphoreType.DMA((N_DEV - 1,)), pltpu.SemaphoreType.DMA]


def _exchange_shape(kind, x):
    return jax.ShapeDtypeStruct(((N_DEV,) + x.shape) if kind == "gather" else x.shape, x.dtype)


def _exchange(kind, xs, name):
    n = len(xs)

    def body(*refs):
        x_refs, o_refs, sems = refs[:n], refs[n:2 * n], refs[2 * n:]
        for e in range(n):
            _exchange_start(kind, x_refs[e], o_refs[e], *sems[3 * e:3 * e + 3])
        for e in range(n):
            _exchange_wait(kind, x_refs[e], o_refs[e], *sems[3 * e:3 * e + 3])

    any_spec = pl.BlockSpec(memory_space=pl.ANY)
    return tuple(pl.pallas_call(
        body, name=name, out_shape=tuple(_exchange_shape(kind, x) for x in xs),
        in_specs=[any_spec] * n, out_specs=(any_spec,) * n, scratch_shapes=list(_EXCHANGE_SEMS) * n,
    )(*xs))


def _all_gather(x, name):
    return _exchange("gather", [x], name)[0]


def _call(body, name, grid, in_specs, out_specs, out_shape, args, scratch=(), side=None):
    single = not isinstance(out_shape, (tuple, list))
    if single:
        out_shape, out_specs = (out_shape,), (out_specs,)
    if side is None:
        res = pl.pallas_call(body, name=name, grid=grid, in_specs=list(in_specs), out_specs=tuple(out_specs),
                             out_shape=tuple(out_shape), scratch_shapes=list(scratch), compiler_params=_cparams())(*args)
        return res[0] if single else tuple(res)
    kind, xs = side
    n_in, n_out, n_scr, n_x = len(in_specs), len(out_shape), len(scratch), len(xs)

    def wrapped(*refs):
        ins, x_refs = refs[:n_in], refs[n_in:n_in + n_x]
        o0 = n_in + n_x
        outs, o_refs = refs[o0:o0 + n_out], refs[o0 + n_out:o0 + n_out + n_x]
        s0 = o0 + n_out + n_x
        scr, sems = refs[s0:s0 + n_scr], refs[s0 + n_scr:]
        ids = [pl.program_id(a) for a in range(len(grid))]
        first = functools.reduce(jnp.logical_and, [i == 0 for i in ids])
        last = functools.reduce(jnp.logical_and, [i == g - 1 for i, g in zip(ids, grid)])

        @pl.when(first)
        def _():
            for e in range(n_x):
                _exchange_start(kind, x_refs[e], o_refs[e], *sems[3 * e:3 * e + 3])

        body(*ins, *outs, *scr)

        @pl.when(last)
        def _():
            for e in range(n_x):
                _exchange_wait(kind, x_refs[e], o_refs[e], *sems[3 * e:3 * e + 3])

    any_spec = pl.BlockSpec(memory_space=pl.ANY)
    res = pl.pallas_call(
        wrapped, name=name, grid=grid, in_specs=list(in_specs) + [any_spec] * n_x,
        out_specs=tuple(out_specs) + (any_spec,) * n_x,
        out_shape=tuple(out_shape) + tuple(_exchange_shape(kind, x) for x in xs),
        scratch_shapes=list(scratch) + list(_EXCHANGE_SEMS) * n_x, compiler_params=_cparams())(*args, *xs)
    return tuple(res[:n_out]) + (tuple(res[n_out:]),)


def _mm(a, b, mode, name, out_dtype=F32, bias=None, side=None, a2=None, tm_cap=1024, tn_cap=1408, tk_cap=1408):
    w2 = 0 if a2 is None else a2.shape[1]
    assert a2 is None or (mode in ("nn", "tn") and a2.shape[0] == a.shape[0])
    if mode == "nn":
        (M, K), (K2, N) = a.shape, b.shape
        K += w2
    elif mode == "nt":
        (M, K), (N, K2) = a.shape, b.shape
    else:
        (K, M), (K2, N) = a.shape, b.shape
        M += w2
    assert K == K2, (a.shape, b.shape, mode)
    if mode == "tn":
        tm_cap, tk_cap = 256, 8192
    if mode == "nn":
        tk_cap = 2816
    if mode == "nt" and M > 16:
        tm_cap = 2048
    tm = _tile(math.gcd(M, w2) if mode == "tn" else M, tm_cap, 128 if mode == "tn" else 16)
    tn = _tile(N, tn_cap, 128)
    tk = _tile(math.gcd(K, w2) if mode == "nn" else K, tk_cap, 128)
    nk = K // tk
    n1 = a.shape[1] // (tm if mode == "tn" else tk)
    ca, cb = {"nn": (1, 0), "nt": (1, 1), "tn": (0, 0)}[mode]
    has_bias = bias is not None

    def body(*refs):
        refs = list(refs)
        a_ref = refs.pop(0)
        a2_ref = refs.pop(0) if a2 is not None else None
        b_ref = refs.pop(0)
        bias_ref = refs.pop(0) if has_bias else None
        o_ref, acc_ref = refs
        k = pl.program_id(2)

        def finish(r):
            if has_bias:
                r = r + bias_ref[...]
            o_ref[...] = r.astype(out_dtype)

        def step(lhs_ref):
            part = _dot(lhs_ref[...].astype(BF16), b_ref[...].astype(BF16), ca, cb)
            if nk == 1:
                finish(part)
                return

            @pl.when(k == 0)
            def _():
                acc_ref[...] = part

            @pl.when(k > 0)
            def _():
                acc_ref[...] += part

        if a2 is None:
            step(a_ref)
        else:
            col = pl.program_id(0) if mode == "tn" else k
            pl.when(col < n1)(lambda: step(a_ref))
            pl.when(col >= n1)(lambda: step(a2_ref))

        if nk > 1:
            pl.when(k == nk - 1)(lambda: finish(acc_ref[...]))

    if mode == "tn":
        a_specs = [pl.BlockSpec((tk, tm), lambda i, j, k: (k, jnp.minimum(i, n1 - 1)))]
        if a2 is not None:
            a_specs.append(pl.BlockSpec((tk, tm), lambda i, j, k: (k, jnp.maximum(i - n1, 0))))
    else:
        a_specs = [pl.BlockSpec((tm, tk), lambda i, j, k: (i, jnp.minimum(k, n1 - 1)))]
        if a2 is not None:
            a_specs.append(pl.BlockSpec((tm, tk), lambda i, j, k: (i, jnp.maximum(k - n1, 0))))
    if mode == "nt":
        b_spec = pl.BlockSpec((tn, tk), lambda i, j, k: (j, k))
    else:
        b_spec = pl.BlockSpec((tk, tn), lambda i, j, k: (k, j))
    in_specs = a_specs + [b_spec]
    args = [a] + ([a2] if a2 is not None else []) + [b]
    if has_bias:
        in_specs.append(pl.BlockSpec((1, tn), lambda i, j, k: (0, j)))
        args.append(bias)
    return _call(body, name, (M // tm, N // tn, nk), in_specs, pl.BlockSpec((tm, tn), lambda i, j, k: (i, j)),
                 jax.ShapeDtypeStruct((M, N), out_dtype), args,
                 scratch=[pltpu.VMEM((tm, tn) if nk > 1 else (8, 128), F32)], side=side)


ROW_TILE = 1024


def _vec_spec(n):
    return pl.BlockSpec((1, n), lambda t: (0, 0))


def _norm_mod_fwd(x, g, sc, sh, name, y=None, gate=None):
    T, D = x.shape
    tr = _tile(T, ROW_TILE, 16)
    has_y = y is not None

    def body(*refs):
        if has_y:
            x_ref, y_ref, gate_ref, g_ref, sc_ref, sh_ref, xn_ref, h_ref = refs
            xv = x_ref[...] + gate_ref[...] * y_ref[...].astype(F32)
            xn_ref[...] = xv
        else:
            x_ref, g_ref, sc_ref, sh_ref, h_ref = refs
            xv = x_ref[...]
        r = lax.rsqrt(jnp.mean(xv * xv, axis=-1, keepdims=True) + NORM_EPS)
        h = (xv * r * g_ref[...]) * (1.0 + sc_ref[...]) + sh_ref[...]
        h_ref[...] = h.astype(BF16)

    row = pl.BlockSpec((tr, D), lambda t: (t, 0))
    vec = _vec_spec(D)
    if has_y:
        in_specs, args = [row, row, vec, vec, vec, vec], [x, y, gate, g, sc, sh]
        out_shape = (jax.ShapeDtypeStruct((T, D), F32), jax.ShapeDtypeStruct((T, D), BF16))
        out_specs = (row, row)
    else:
        in_specs, args = [row, vec, vec, vec], [x, g, sc, sh]
        out_shape = jax.ShapeDtypeStruct((T, D), BF16)
        out_specs = row
    return _call(body, name, (T // tr,), in_specs, out_specs, out_shape, args)


def _norm_mod_bwd(dh, x, g, sc, sh, dxn, name, y_up=None, gate_up=None):
    T, D = x.shape
    tr = _tile(T, ROW_TILE, 16)
    fused = y_up is not None

    def body(*refs):
        if fused:
            (dh_ref, x_ref, g_ref, sc_ref, sh_ref, dxn_ref, y_ref, gate_ref,
             dx_ref, dg_ref, dsc_ref, dsh_ref, dy_ref, dgate_ref, dbias_ref) = refs
            sums = (dg_ref, dsc_ref, dsh_ref, dgate_ref, dbias_ref)
        else:
            dh_ref, x_ref, g_ref, sc_ref, sh_ref, dxn_ref, dx_ref, dg_ref, dsc_ref, dsh_ref = refs
            sums = (dg_ref, dsc_ref, dsh_ref)

        @pl.when(pl.program_id(0) == 0)
        def _():
            for r_ in sums:
                r_[...] = jnp.zeros_like(r_)
        dh = dh_ref[...].astype(F32)
        xv = x_ref[...]
        gv = g_ref[...]
        r = lax.rsqrt(jnp.mean(xv * xv, axis=-1, keepdims=True) + NORM_EPS)
        yv = xv * r
        dsh_ref[...] += jnp.sum(dh, axis=0, keepdims=True)
        dsc_ref[...] += jnp.sum(dh * (yv * gv), axis=0, keepdims=True)
        dn = dh * (1.0 + sc_ref[...])
        dg_ref[...] += jnp.sum(dn * yv, axis=0, keepdims=True)
        dy = dn * gv
        dx = dxn_ref[...] + r * (dy - yv * jnp.mean(dy * yv, axis=-1, keepdims=True))
        dx_ref[...] = dx
        if fused:
            dyu = dx * gate_ref[...]
            dy_ref[...] = dyu.astype(BF16)
            dgate_ref[...] += jnp.sum(dx * y_ref[...].astype(F32), axis=0, keepdims=True)
            dbias_ref[...] += jnp.sum(dyu, axis=0, keepdims=True)

    row = pl.BlockSpec((tr, D), lambda t: (t, 0))
    vec = _vec_spec(D)
    vshape = jax.ShapeDtypeStruct((1, D), F32)
    in_specs, args = [row, row, vec, vec, vec, row], [dh, x, g, sc, sh, dxn]
    out_specs, out_shape = [row, vec, vec, vec], [jax.ShapeDtypeStruct((T, D), F32), vshape, vshape, vshape]
    if fused:
        in_specs, args = in_specs + [row, vec], args + [y_up, gate_up]
        out_specs, out_shape = out_specs + [row, vec, vec], out_shape + [jax.ShapeDtypeStruct((T, D), BF16), vshape, vshape]
    return _call(body, name, (T // tr,), in_specs, tuple(out_specs), tuple(out_shape), args)


def _final_loss(x, y, gate, tgt, g, name):
    T, D = x.shape
    tr = _tile(T, ROW_TILE, 16)

    def body(x_ref, y_ref, gate_ref, tgt_ref, g_ref, dx_ref, dg_ref, loss_ref, dyu_ref, dgate_ref):
        @pl.when(pl.program_id(0) == 0)
        def _():
            for r_ in (dg_ref, loss_ref, dgate_ref):
                r_[...] = jnp.zeros_like(r_)
        yu = y_ref[...].astype(F32)
        gate = gate_ref[...]
        xv = x_ref[...] + gate * yu
        gv = g_ref[...]
        r = lax.rsqrt(jnp.mean(xv * xv, axis=-1, keepdims=True) + NORM_EPS)
        yv = xv * r
        e = yv * gv - tgt_ref[...]
        per_row = jnp.mean(e * e, axis=-1, keepdims=True)
        loss_ref[...] += 0.5 * jnp.sum(per_row, axis=0, keepdims=True)
        dout = e * (1.0 / D)
        dg_ref[...] += jnp.sum(dout * yv, axis=0, keepdims=True)
        dy = dout * gv
        dx = r * (dy - yv * jnp.mean(dy * yv, axis=-1, keepdims=True))
        dx_ref[...] = dx
        dyu_ref[...] = (dx * gate).astype(BF16)
        dgate_ref[...] += jnp.sum(dx * yu, axis=0, keepdims=True)

    row = pl.BlockSpec((tr, D), lambda t: (t, 0))
    vec = _vec_spec(D)
    return _call(body, name, (T // tr,), [row, row, vec, row, vec], (row, vec, _vec_spec(128), row, vec),
                 (jax.ShapeDtypeStruct((T, D), F32), jax.ShapeDtypeStruct((1, D), F32),
                  jax.ShapeDtypeStruct((1, 128), F32), jax.ShapeDtypeStruct((T, D), BF16),
                  jax.ShapeDtypeStruct((1, D), F32)), (x, y, gate, tgt, g))


FFN_HALO = 16
FFN_PAD = 8
FFN_CHUNK = 64
FFN_ROWS_FWD = 8192
FFN_ROWS_BWD = 8192


def _lane_groups(cw):
    lw = 128 if cw % 128 == 0 else cw
    return lw, [pl.ds(g * lw, lw) for g in range(cw // lw)]


def _ffn_mid_fwd(z, dw, b, name, side=None):
    T, F2 = z.shape
    F = F2 // 2
    K = dw.shape[0]
    H, P = FFN_HALO, FFN_PAD
    tr = _tile(T, FFN_ROWS_FWD, H)
    cw = _tile(F, 512, 128)
    nc = F // cw
    rc = _tile(tr, FFN_CHUNK, 16)
    n_chunks = tr // rc
    assert n_chunks % 2 == 0 and rc >= H
    lw, groups = _lane_groups(cw)

    def body(gc_ref, gh_ref, uc_ref, uh_ref, wg_ref, wu_ref, bg_ref, bu_ref, a_ref, xg_s, xu_s):
        first = pl.program_id(1) == 0
        halves = ((xg_s, gc_ref, gh_ref, wg_ref, bg_ref), (xu_s, uc_ref, uh_ref, wu_ref, bu_ref))

        for s, c_ref, h_ref, _, _ in halves:
            for g, ls in enumerate(groups):
                s[0, g, 0:P, :] = jnp.where(first, 0.0, h_ref[:, ls].astype(F32)[H - P:H])
                s[0, g, P:P + rc, :] = c_ref[0:rc, ls].astype(F32)

        def stage(slot, c):
            start = pl.multiple_of(c * rc - H, H)
            for s, c_ref, _, _, _ in halves:
                for g, ls in enumerate(groups):
                    s[slot, g] = c_ref[pl.ds(start, rc + H), ls].astype(F32)[H - P:]

        def compute(slot, c):
            r0 = pl.multiple_of(c * rc, rc)
            for g, ls in enumerate(groups):
                vals = []
                for s, _, _, w_ref, b_ref in halves:
                    acc = b_ref[:, ls] + w_ref[K - 1:K, ls] * s[slot, g, P:P + rc, :]
                    for k in range(K - 1):
                        o = P - (K - 1 - k)
                        acc = acc + w_ref[k:k + 1, ls] * s[slot, g, o:o + rc, :]
                    vals.append(acc)
                gv, uv = vals
                a_ref[pl.ds(r0, rc), ls] = (gv * _sigmoid(gv) * uv).astype(BF16)

        def pair(i, carry):
            c0 = 2 * i
            compute(0, c0)
            stage(1, c0 + 1)
            compute(1, c0 + 1)
            stage(0, jnp.minimum(c0 + 2, n_chunks - 1))
            return carry

        lax.fori_loop(0, n_chunks // 2, pair, 0)

    rb = tr // H
    in_specs = [
        pl.BlockSpec((tr, cw), lambda j, t: (t, j)),
        pl.BlockSpec((H, cw), lambda j, t: (jnp.maximum(t * rb - 1, 0), j)),
        pl.BlockSpec((tr, cw), lambda j, t: (t, j + nc)),
        pl.BlockSpec((H, cw), lambda j, t: (jnp.maximum(t * rb - 1, 0), j + nc)),
        pl.BlockSpec((K, cw), lambda j, t: (0, j)),
        pl.BlockSpec((K, cw), lambda j, t: (0, j + nc)),
        pl.BlockSpec((1, cw), lambda j, t: (0, j)),
        pl.BlockSpec((1, cw), lambda j, t: (0, j + nc)),
    ]
    return _call(body, name, (nc, T // tr), in_specs, pl.BlockSpec((tr, cw), lambda j, t: (t, j)),
                 jax.ShapeDtypeStruct((T, F), BF16), (z, z, z, z, dw, dw, b, b),
                 scratch=[pltpu.VMEM((2, len(groups), P + rc, lw), F32)] * 2, side=side)


def _ffn_mid_bwd(z, da, dw, b, name, side=None):
    T, F2 = z.shape
    F = F2 // 2
    K = dw.shape[0]
    H, P = FFN_HALO, FFN_PAD
    tr = _tile(T, FFN_ROWS_BWD, H)
    cw = _tile(F, 512, 128)
    nc = F // cw
    nt = T // tr
    rb = tr // H
    rc = _tile(tr, FFN_CHUNK, 16)
    n_chunks = tr // rc
    assert n_chunks % 2 == 0 and rc >= H
    ext = rc + P
    lw, groups = _lane_groups(cw)

    def body(gp_ref, gc_ref, gn_ref, up_ref, uc_ref, un_ref, dac_ref, dan_ref, wg_ref, wu_ref, bg_ref, bu_ref,
             dzg_ref, dzu_ref, dwg_ref, dwu_ref, dbg_ref, dbu_ref, xg_s, xu_s, da_s, dg_s, du_s, acc_s):
        t = pl.program_id(1)
        first = t == 0
        last = t == nt - 1

        @pl.when(first)
        def _():
            for r in (dwg_ref, dwu_ref, dbg_ref, dbu_ref):
                r[...] = jnp.zeros_like(r)

        xs = ((xg_s, gp_ref, gc_ref, gn_ref), (xu_s, up_ref, uc_ref, un_ref))
        acc_s[...] = jnp.zeros_like(acc_s)

        def stage_first():
            for s, p_ref, c_ref, _ in xs:
                for g, ls in enumerate(groups):
                    s[0, g, 0:P, :] = jnp.where(first, 0.0, p_ref[:, ls].astype(F32)[H - P:H])
                    s[0, g, P:2 * P + rc, :] = c_ref[0:rc + H, ls].astype(F32)[:rc + P]
            for g, ls in enumerate(groups):
                da_s[0, g] = dac_ref[0:rc + H, ls].astype(F32)[:ext]

        def stage(slot, c):
            start = pl.multiple_of(c * rc - H, H)
            for s, _, c_ref, _ in xs:
                for g, ls in enumerate(groups):
                    s[slot, g] = c_ref[pl.ds(start, rc + 2 * H), ls].astype(F32)[H - P:H + rc + P]
            for g, ls in enumerate(groups):
                da_s[slot, g] = dac_ref[pl.ds(pl.multiple_of(c * rc, rc), rc + H), ls].astype(F32)[:ext]

        def stage_last(slot):
            for s, _, c_ref, n_ref in xs:
                for g, ls in enumerate(groups):
                    s[slot, g, 0:P + rc, :] = c_ref[tr - rc - H:tr, ls].astype(F32)[H - P:]
                    s[slot, g, P + rc:2 * P + rc, :] = jnp.where(last, 0.0, n_ref[:, ls].astype(F32)[0:P])
            for g, ls in enumerate(groups):
                da_s[slot, g, 0:rc, :] = dac_ref[tr - rc:tr, ls].astype(F32)
                da_s[slot, g, rc:ext, :] = jnp.where(last, 0.0, dan_ref[:, ls].astype(F32)[0:P])

        def fold(v):
            return jnp.sum(v.reshape(rc // 8, 8, lw), axis=0)

        def compute(slot, c):
            r0 = c * rc if isinstance(c, int) else pl.multiple_of(c * rc, rc)
            for g, ls in enumerate(groups):
                def conv_ext(s, w_ref, b_ref):
                    acc = b_ref[:, ls] + w_ref[K - 1:K, ls] * s[slot, g, P:P + ext, :]
                    for k in range(K - 1):
                        o = P - (K - 1 - k)
                        acc = acc + w_ref[k:k + 1, ls] * s[slot, g, o:o + ext, :]
                    return acc

                gv = conv_ext(xg_s, wg_ref, bg_ref)
                uv = conv_ext(xu_s, wu_ref, bu_ref)
                dav = da_s[slot, g]
                sg = _sigmoid(gv)
                dg = dav * uv * (sg * (1.0 + gv * (1.0 - sg)))
                du = dav * (gv * sg)
                dg_s[g] = dg
                du_s[g] = du
                for idx, (d_s, dval, x_s, w_ref, dz_ref) in enumerate(
                        ((dg_s, dg, xg_s, wg_ref, dzg_ref), (du_s, du, xu_s, wu_ref, dzu_ref))):
                    cur = dval[:rc]
                    acc_s[idx, K, g] += fold(cur)
                    for k in range(K):
                        o = P - (K - 1 - k)
                        acc_s[idx, k, g] += fold(cur * x_s[slot, g, o:o + rc, :])
                    dz = w_ref[K - 1:K, ls] * cur
                    for k in range(K - 1):
                        dz = dz + w_ref[k:k + 1, ls] * d_s[g, K - 1 - k:K - 1 - k + rc, :]
                    dz_ref[pl.ds(r0, rc), ls] = dz.astype(BF16)

        def pair(i, carry):
            c0 = 2 * i
            compute(0, c0)
            stage(1, c0 + 1)
            compute(1, c0 + 1)
            stage(0, c0 + 2)
            return carry

        stage_first()
        lax.fori_loop(0, n_chunks // 2 - 1, pair, 0)
        compute(0, n_chunks - 2)
        stage_last(1)
        compute(1, n_chunks - 1)
        for idx, (dw_ref, db_ref) in enumerate(((dwg_ref, dbg_ref), (dwu_ref, dbu_ref))):
            for g, ls in enumerate(groups):
                db_ref[:, ls] += jnp.sum(acc_s[idx, K, g], axis=0, keepdims=True)
                dw_ref[:, ls] += jnp.concatenate(
                    [jnp.sum(acc_s[idx, k, g], axis=0, keepdims=True) for k in range(K)], axis=0)

    def prev(t):
        return jnp.maximum(t * rb - 1, 0)

    def nxt(t):
        return jnp.minimum((t + 1) * rb, T // H - 1)

    def zspecs(off):
        return [pl.BlockSpec((H, cw), lambda j, t: (prev(t), j + off)),
                pl.BlockSpec((tr, cw), lambda j, t: (t, j + off)),
                pl.BlockSpec((H, cw), lambda j, t: (nxt(t), j + off))]

    in_specs = zspecs(0) + zspecs(nc) + [
        pl.BlockSpec((tr, cw), lambda j, t: (t, j)),
        pl.BlockSpec((H, cw), lambda j, t: (nxt(t), j)),
        pl.BlockSpec((K, cw), lambda j, t: (0, j)),
        pl.BlockSpec((K, cw), lambda j, t: (0, j + nc)),
        pl.BlockSpec((1, cw), lambda j, t: (0, j)),
        pl.BlockSpec((1, cw), lambda j, t: (0, j + nc)),
    ]
    out_shape = (jax.ShapeDtypeStruct((T, F), BF16), jax.ShapeDtypeStruct((T, F), BF16),
                 jax.ShapeDtypeStruct((K, F), F32), jax.ShapeDtypeStruct((K, F), F32),
                 jax.ShapeDtypeStruct((1, F), F32), jax.ShapeDtypeStruct((1, F), F32))
    out_specs = (pl.BlockSpec((tr, cw), lambda j, t: (t, j)), pl.BlockSpec((tr, cw), lambda j, t: (t, j)),
                 pl.BlockSpec((K, cw), lambda j, t: (0, j)), pl.BlockSpec((K, cw), lambda j, t: (0, j)),
                 pl.BlockSpec((1, cw), lambda j, t: (0, j)), pl.BlockSpec((1, cw), lambda j, t: (0, j)))
    ng = len(groups)
    scratch = [pltpu.VMEM((2, ng, 2 * P + rc, lw), F32), pltpu.VMEM((2, ng, 2 * P + rc, lw), F32),
               pltpu.VMEM((2, ng, ext, lw), F32), pltpu.VMEM((ng, ext, lw), F32), pltpu.VMEM((ng, ext, lw), F32),
               pltpu.VMEM((2, K + 1, ng, 8, lw), F32)]
    res = _call(body, name, (nc, nt), in_specs, out_specs, out_shape, (z, z, z, z, z, z, da, da, dw, dw, b, b),
                scratch=scratch, side=side)
    dzg, dzu, dwg, dwu, dbg, dbu = res[:6]
    return (dzg, dzu, jnp.concatenate([dwg, dwu], axis=1), jnp.concatenate([dbg, dbu], axis=1)) + tuple(res[6:])


CONV_HALO = 32
CONV_ROWS = 512
CONV_CHUNK = 64


def _glu_window(s, p_ref, halo_ref, first, D, groups, lw, H, tr):
    for g, ls in enumerate(groups):
        gs = pl.ds(D + g * lw, lw)
        s[g, 0:H, :] = jnp.where(first, 0.0, halo_ref[:, ls].astype(F32) * _sigmoid(halo_ref[:, gs].astype(F32)))
        s[g, H:H + tr, :] = p_ref[:, ls].astype(F32) * _sigmoid(p_ref[:, gs].astype(F32))


def _convmix_fwd(p, dw, dwb, lng, lnb, name, side=None):
    T, D2 = p.shape
    D = D2 // 2
    K = dw.shape[0]
    H = CONV_HALO
    tr = _tile(T, CONV_ROWS, H)
    rb = tr // H
    rc = _tile(tr, CONV_CHUNK, 16)
    lw, groups = _lane_groups(D)

    def body(pc_ref, ph_ref, w_ref, wb_ref, lng_ref, lnb_ref, zc_ref, s_ref, zg_s):
        first = pl.program_id(0) == 0
        _glu_window(zg_s, pc_ref, ph_ref, first, D, groups, lw, H, tr)

        def chunk(ci, carry):
            r0 = pl.multiple_of(ci * rc, rc)
            for g, ls in enumerate(groups):
                acc = wb_ref[:, ls] + w_ref[K - 1:K, ls] * zg_s[g, pl.ds(pl.multiple_of(r0 + H, 8), rc), :]
                for k in range(K - 1):
                    acc = acc + w_ref[k:k + 1, ls] * zg_s[g, pl.ds(r0 + H - (K - 1 - k), rc), :]
                zc_ref[pl.ds(r0, rc), ls] = acc
            return carry

        lax.fori_loop(0, tr // rc, chunk, 0)
        acc = zc_ref[...]
        mu = jnp.mean(acc, axis=-1, keepdims=True)
        xc = acc - mu
        rstd = lax.rsqrt(jnp.mean(xc * xc, axis=-1, keepdims=True) + NORM_EPS)
        ln = xc * rstd * lng_ref[...] + lnb_ref[...]
        s_ref[...] = (ln * _sigmoid(ln)).astype(BF16)

    in_specs = [pl.BlockSpec((tr, D2), lambda t: (t, 0)),
                pl.BlockSpec((H, D2), lambda t: (jnp.maximum(t * rb - 1, 0), 0)),
                pl.BlockSpec((K, D), lambda t: (0, 0)), _vec_spec(D), _vec_spec(D), _vec_spec(D)]
    row = pl.BlockSpec((tr, D), lambda t: (t, 0))
    return _call(body, name, (T // tr,), in_specs, (row, row),
                 (jax.ShapeDtypeStruct((T, D), F32), jax.ShapeDtypeStruct((T, D), BF16)), (p, p, dw, dwb, lng, lnb),
                 scratch=[pltpu.VMEM((len(groups), H + tr, lw), F32)], side=side)


def _convmix_bwd(ds, zc, p, dw, lng, lnb, name, side=None):
    T, D = zc.shape
    D2 = 2 * D
    K = dw.shape[0]
    H = CONV_HALO
    tr = _tile(T, CONV_ROWS, H)
    rb = tr // H
    nt = T // tr
    ext = tr + H
    rc = _tile(tr, CONV_CHUNK, 16)
    lw, groups = _lane_groups(D)

    def body(dsc_ref, dsn_ref, zcc_ref, zcn_ref, pp_ref, pc_ref, w_ref, lng_ref, lnb_ref,
             dp_ref, dbin_ref, ddw_ref, ddwb_ref, dlng_ref, dlnb_ref, zg_s, dzc_s, acc_s):
        t = pl.program_id(0)
        first = t == 0
        last = t == nt - 1

        @pl.when(first)
        def _():
            for r in (dbin_ref, ddw_ref, ddwb_ref, dlng_ref, dlnb_ref):
                r[...] = jnp.zeros_like(r)

        dsv = jnp.concatenate([dsc_ref[...].astype(F32), jnp.where(last, 0.0, dsn_ref[...].astype(F32))], axis=0)
        zcv = jnp.concatenate([zcc_ref[...], zcn_ref[...]], axis=0)
        lg = lng_ref[...]
        mu = jnp.mean(zcv, axis=-1, keepdims=True)
        xc = zcv - mu
        rstd = lax.rsqrt(jnp.mean(xc * xc, axis=-1, keepdims=True) + NORM_EPS)
        xh = xc * rstd
        ln = xh * lg + lnb_ref[...]
        sg = _sigmoid(ln)
        dln = dsv * (sg * (1.0 + ln * (1.0 - sg)))
        dlnc = _rows(dln, 0, tr)
        dlnb_ref[...] += jnp.sum(dlnc, axis=0, keepdims=True)
        dlng_ref[...] += jnp.sum(dlnc * _rows(xh, 0, tr), axis=0, keepdims=True)
        dxh = dln * lg
        dzc = rstd * (dxh - jnp.mean(dxh, axis=-1, keepdims=True) - xh * jnp.mean(dxh * xh, axis=-1, keepdims=True))
        dzcc = _rows(dzc, 0, tr)
        ddwb_ref[...] += jnp.sum(dzcc, axis=0, keepdims=True)

        for g in range(len(groups)):
            dzc_s[g] = dzc[:, g * lw:(g + 1) * lw]
        _glu_window(zg_s, pc_ref, pp_ref, first, D, groups, lw, H, tr)
        acc_s[...] = jnp.zeros_like(acc_s)

        def fold(v):
            return jnp.sum(v.reshape(rc // 8, 8, lw), axis=0)

        def chunk(ci, carry):
            r0 = pl.multiple_of(ci * rc, rc)
            for g, ls in enumerate(groups):
                gs = pl.ds(D + g * lw, lw)
                cur = dzc_s[g, pl.ds(r0, rc), :]
                dzg = w_ref[K - 1:K, ls] * cur
                for k in range(K):
                    acc_s[k, g] += fold(cur * zg_s[g, pl.ds(r0 + H - (K - 1 - k), rc), :])
                    if k < K - 1:
                        dzg = dzg + w_ref[k:k + 1, ls] * dzc_s[g, pl.ds(r0 + (K - 1 - k), rc), :]
                ac = pc_ref[pl.ds(r0, rc), ls].astype(F32)
                sgg = _sigmoid(pc_ref[pl.ds(r0, rc), gs].astype(F32))
                dpa = dzg * sgg
                dpg = dzg * ac * sgg * (1.0 - sgg)
                dp_ref[pl.ds(r0, rc), ls] = dpa.astype(BF16)
                dp_ref[pl.ds(r0, rc), gs] = dpg.astype(BF16)
                acc_s[K, g] += fold(dpa)
                acc_s[K + 1, g] += fold(dpg)
            return carry

        lax.fori_loop(0, tr // rc, chunk, 0)
        for g, ls in enumerate(groups):
            ddw_ref[:, ls] += jnp.concatenate([jnp.sum(acc_s[k, g], axis=0, keepdims=True) for k in range(K)], axis=0)
            dbin_ref[:, ls] += jnp.sum(acc_s[K, g], axis=0, keepdims=True)
            dbin_ref[:, pl.ds(D + g * lw, lw)] += jnp.sum(acc_s[K + 1, g], axis=0, keepdims=True)

    def prev(t):
        return jnp.maximum(t * rb - 1, 0)

    def nxt(t):
        return jnp.minimum((t + 1) * rb, T // H - 1)

    in_specs = [pl.BlockSpec((tr, D), lambda t: (t, 0)), pl.BlockSpec((H, D), lambda t: (nxt(t), 0)),
                pl.BlockSpec((tr, D), lambda t: (t, 0)), pl.BlockSpec((H, D), lambda t: (nxt(t), 0)),
                pl.BlockSpec((H, D2), lambda t: (prev(t), 0)), pl.BlockSpec((tr, D2), lambda t: (t, 0)),
                pl.BlockSpec((K, D), lambda t: (0, 0)), _vec_spec(D), _vec_spec(D)]
    out_shape = (jax.ShapeDtypeStruct((T, D2), BF16), jax.ShapeDtypeStruct((1, D2), F32),
                 jax.ShapeDtypeStruct((K, D), F32), jax.ShapeDtypeStruct((1, D), F32),
                 jax.ShapeDtypeStruct((1, D), F32), jax.ShapeDtypeStruct((1, D), F32))
    out_specs = (pl.BlockSpec((tr, D2), lambda t: (t, 0)), _vec_spec(D2), pl.BlockSpec((K, D), lambda t: (0, 0)),
                 _vec_spec(D), _vec_spec(D), _vec_spec(D))
    ng = len(groups)
    scratch = [pltpu.VMEM((ng, H + tr, lw), F32), pltpu.VMEM((ng, ext, lw), F32), pltpu.VMEM((K + 2, ng, 8, lw), F32)]
    return _call(body, name, (nt,), in_specs, out_specs, out_shape, (ds, ds, zc, zc, p, p, dw, lng, lnb),
                 scratch=scratch, side=side)


_INV_SQRT2 = 1.0 / math.sqrt(2.0)
_INV_SQRT2PI = 1.0 / math.sqrt(2.0 * math.pi)


def _gelu(x):
    return 0.5 * x * (1.0 + lax.erf(x * _INV_SQRT2))


def _gelu_grad(x):
    return 0.5 * (1.0 + lax.erf(x * _INV_SQRT2)) + x * jnp.exp(-0.5 * x * x) * _INV_SQRT2PI


def _tril_mask(n, transposed=False):
    r = lax.broadcasted_iota(jnp.int32, (n, n), 0)
    c = lax.broadcasted_iota(jnp.int32, (n, n), 1)
    return (r <= c) if transposed else (r >= c)


def _sgu_fwd(p, lng, lnb, ws, bs_t, name):
    T, S2 = p.shape
    S = S2 // 2
    G, C, _ = ws.shape
    gd = S // G
    cpt = 2 if (T // C) % 2 == 0 else 1
    tr = C * cpt

    def body(p_ref, lng_ref, lnb_ref, ws_ref, bst_ref, m_ref):
        mask = _tril_mask(C)
        lg, lb = lng_ref[...], lnb_ref[...]
        bst = bst_ref[...]
        for ci in range(cpt):
            pc = p_ref[ci * C:(ci + 1) * C, :].astype(F32)
            z = _gelu(pc)
            u, v = z[:, :S], z[:, S:]
            mu = jnp.mean(v, axis=-1, keepdims=True)
            xc = v - mu
            rstd = lax.rsqrt(jnp.mean(xc * xc, axis=-1, keepdims=True) + NORM_EPS)
            vn = (xc * rstd * lg + lb).astype(BF16)
            outs = []
            for g in range(G):
                wm = jnp.where(mask, ws_ref[g], 0.0).astype(BF16)
                vs = _dot(wm, vn[:, g * gd:(g + 1) * gd], 1, 0) + bst[:, g:g + 1]
                outs.append(u[:, g * gd:(g + 1) * gd] * vs)
            m_ref[ci * C:(ci + 1) * C, :] = jnp.concatenate(outs, axis=1).astype(BF16)

    in_specs = [pl.BlockSpec((tr, S2), lambda t: (t, 0)), _vec_spec(S), _vec_spec(S),
                pl.BlockSpec((G, C, C), lambda t: (0, 0, 0)), pl.BlockSpec((C, G), lambda t: (0, 0))]
    return _call(body, name, (T // tr,), in_specs, pl.BlockSpec((tr, S), lambda t: (t, 0)),
                 jax.ShapeDtypeStruct((T, S), BF16), (p, lng, lnb, ws, bs_t))


def _sgu_bwd(p, dm, lng, lnb, ws, bs_t, name, side=None):
    T, S2 = p.shape
    S = S2 // 2
    G, C, _ = ws.shape
    gd = S // G
    nt = T // C

    def body(p_ref, dm_ref, lng_ref, lnb_ref, ws_ref, bst_ref, dp_ref, dbin_ref, dlng_ref, dlnb_ref, dws_ref, dbst_ref):
        @pl.when(pl.program_id(0) == 0)
        def _():
            for r in (dbin_ref, dlng_ref, dlnb_ref, dws_ref, dbst_ref):
                r[...] = jnp.zeros_like(r)

        mask = _tril_mask(C)
        lg, lb = lng_ref[...], lnb_ref[...]
        bst = bst_ref[...]
        pc = p_ref[...].astype(F32)
        dmv = dm_ref[...].astype(F32)
        z = _gelu(pc)
        u, v = z[:, :S], z[:, S:]
        mu = jnp.mean(v, axis=-1, keepdims=True)
        xc = v - mu
        rstd = lax.rsqrt(jnp.mean(xc * xc, axis=-1, keepdims=True) + NORM_EPS)
        vh = xc * rstd
        vn = (vh * lg + lb).astype(BF16)
        dus, dvns, dbcols = [], [], []
        for g in range(G):
            sl = slice(g * gd, (g + 1) * gd)
            wm = jnp.where(mask, ws_ref[g], 0.0).astype(BF16)
            vs = _dot(wm, vn[:, sl], 1, 0) + bst[:, g:g + 1]
            dmg = dmv[:, sl]
            dus.append(dmg * vs)
            dvs = dmg * u[:, sl]
            dbcols.append(jnp.sum(dvs, axis=-1, keepdims=True))
            dvsb = dvs.astype(BF16)
            dws_ref[g] += jnp.where(mask, _dot(dvsb, vn[:, sl], 1, 1), 0.0)
            dvns.append(_dot(wm, dvsb, 0, 0))
        dbst_ref[...] += jnp.concatenate(dbcols, axis=1)
        dvn = jnp.concatenate(dvns, axis=1)
        dlnb_ref[...] += jnp.sum(dvn, axis=0, keepdims=True)
        dlng_ref[...] += jnp.sum(dvn * vh, axis=0, keepdims=True)
        dvh = dvn * lg
        dv = rstd * (dvh - jnp.mean(dvh, axis=-1, keepdims=True) - vh * jnp.mean(dvh * vh, axis=-1, keepdims=True))
        dz = jnp.concatenate([jnp.concatenate(dus, axis=1), dv], axis=1)
        dpv = dz * _gelu_grad(pc)
        dp_ref[...] = dpv.astype(BF16)
        dbin_ref[...] += jnp.sum(dpv, axis=0, keepdims=True)

    in_specs = [pl.BlockSpec((C, S2), lambda t: (t, 0)), pl.BlockSpec((C, S), lambda t: (t, 0)),
                _vec_spec(S), _vec_spec(S), pl.BlockSpec((G, C, C), lambda t: (0, 0, 0)),
                pl.BlockSpec((C, G), lambda t: (0, 0))]
    out_shape = (jax.ShapeDtypeStruct((T, S2), BF16), jax.ShapeDtypeStruct((1, S2), F32),
                 jax.ShapeDtypeStruct((1, S), F32), jax.ShapeDtypeStruct((1, S), F32),
                 jax.ShapeDtypeStruct((G, C, C), F32), jax.ShapeDtypeStruct((C, G), F32))
    out_specs = (pl.BlockSpec((C, S2), lambda t: (t, 0)), _vec_spec(S2), _vec_spec(S), _vec_spec(S),
                 pl.BlockSpec((G, C, C), lambda t: (0, 0, 0)), pl.BlockSpec((C, G), lambda t: (0, 0)))
    return _call(body, name, (nt,), in_specs, out_specs, out_shape, (p, dm, lng, lnb, ws, bs_t), side=side)


def _alibi_slope(h, n_heads):
    return 2.0 ** (-8.0 * (h + 1) / n_heads)


def _attn_bias(n_heads):
    B = BLOCK
    qi = jnp.arange(B)[:, None]
    kj = jnp.arange(2 * B)[None, :]
    dist = qi + B - kj
    band = (dist >= 0) & (dist < B)
    slopes = jnp.array([_alibi_slope(h, n_heads) for h in range(n_heads)], F32)
    out = []
    for valid in (band & (kj >= B), band):
        b = jnp.where(valid[None], -slopes[:, None, None] * dist.astype(F32)[None], NEG_INF)
        b = jnp.where(kj[None] == 0, 0.0, b)
        out.append(b.reshape(n_heads // 2, 2, B, 2 * B).transpose(0, 2, 1, 3).reshape(n_heads // 2, B, 4 * B))
    return jnp.stack(out)


def _sink_rows(sinks):
    return jnp.pad(sinks.astype(F32)[:, None], ((0, 0), (0, BLOCK - 1)))


def _block_diag2(x):
    z = jnp.zeros_like(x)
    return jnp.concatenate([jnp.concatenate([x, z], axis=1), jnp.concatenate([z, x], axis=1)], axis=0)


def _attn_core(q_ref, kp_ref, kc_ref, vp_ref, vc_ref, sink_ref, bias_ref, n_heads):
    B, hd = BLOCK, HEAD_DIM
    ppg = n_heads // N_KV // 2
    pairs = range(n_heads // 2)
    row0 = lax.broadcasted_iota(jnp.int32, (2 * B, N_KV * hd), 0) == 0
    k2 = jnp.concatenate([kp_ref[...], kc_ref[...]], axis=0)
    v2 = jnp.concatenate([vp_ref[...], vc_ref[...]], axis=0)
    k2 = jnp.where(row0, jnp.zeros_like(k2), k2)
    v2 = jnp.where(row0, jnp.zeros_like(v2), v2)
    scale = hd ** -0.5
    kks = [_block_diag2(k2[:, kv * hd:(kv + 1) * hd] * scale) for kv in range(N_KV)]
    vvs = [_block_diag2(v2[:, kv * hd:(kv + 1) * hd]) for kv in range(N_KV)]
    first_rows = lax.broadcasted_iota(jnp.int32, (4 * B, 4 * hd), 0) < 2 * B
    first_lanes = lax.broadcasted_iota(jnp.int32, (4 * B, 4 * hd), 1) < 2 * hd
    ones = (first_rows == first_lanes).astype(BF16)
    vxs = [jnp.concatenate([vv, ones], axis=1) for vv in vvs]
    ss = [_dot(q_ref[:, pr * 2 * hd:(pr + 1) * 2 * hd], kks[pr // ppg], 1, 1) + bias_ref[pr] for pr in pairs]
    halves = []
    for h in range(n_heads):
        s = ss[h // 2][:, (h % 2) * 2 * B:(h % 2 + 1) * 2 * B]
        halves.append(jnp.concatenate([s[:, :B] + sink_ref[h:h + 1, :], s[:, B:]], axis=1))
    mxs = [jnp.max(s, axis=-1, keepdims=True) for s in halves]
    es = [jnp.exp(s - m) for s, m in zip(halves, mxs)]
    rs = [_dot(jnp.concatenate([es[2 * pr], es[2 * pr + 1]], axis=1).astype(BF16), vxs[pr // ppg], 1, 0) for pr in pairs]
    return kks, vvs, es, rs


def _attn_fwd(qkv, sink_rows, bias, n_heads, name, side=None):
    T = qkv.shape[0]
    B = BLOCK
    hd = HEAD_DIM
    HQ = n_heads * hd
    KVW = N_KV * hd
    assert (n_heads // N_KV) % 2 == 0, "heads are processed in pairs that share a key/value head"
    nb = T // B
    kcol = HQ // KVW

    def body(q_ref, kp_ref, kc_ref, vp_ref, vc_ref, sink_ref, bias_ref, o_ref):
        _, _, _, rs = _attn_core(q_ref, kp_ref, kc_ref, vp_ref, vc_ref, sink_ref, bias_ref, n_heads)
        low = lax.broadcasted_iota(jnp.int32, (B, 2 * hd), 1) < hd
        for pr, r in enumerate(rs):
            inv = jnp.where(low, 1.0 / r[:, 2 * hd:4 * hd], 1.0 / r[:, 4 * hd:])
            o_ref[:, pr * 2 * hd:(pr + 1) * 2 * hd] = (r[:, :2 * hd] * inv).astype(BF16)

    def prev(n):
        return jnp.maximum(n - 1, 0)

    in_specs = [pl.BlockSpec((B, HQ), lambda n: (n, 0)),
                pl.BlockSpec((B, KVW), lambda n: (prev(n), kcol)), pl.BlockSpec((B, KVW), lambda n: (n, kcol)),
                pl.BlockSpec((B, KVW), lambda n: (prev(n), kcol + 1)), pl.BlockSpec((B, KVW), lambda n: (n, kcol + 1)),
                pl.BlockSpec((n_heads, B), lambda n: (0, 0)),
                pl.BlockSpec((None, n_heads // 2, B, 4 * B), lambda n: (jnp.minimum(n, 1), 0, 0, 0))]
    return _call(body, name, (nb,), in_specs, pl.BlockSpec((B, HQ), lambda n: (n, 0)),
                 jax.ShapeDtypeStruct((T, HQ), BF16), (qkv, qkv, qkv, qkv, qkv, sink_rows, bias), side=side)


def _attn_bwd(qkv, do, sink_rows, bias, n_heads, name, side=None):
    T = qkv.shape[0]
    B = BLOCK
    hd = HEAD_DIM
    HQ = n_heads * hd
    KVW = N_KV * hd
    group = n_heads // N_KV
    assert group % 2 == 0, "heads are processed in pairs that share a key/value head"
    ppg = group // 2
    nb = T // B
    kcol = HQ // KVW
    scale = hd ** -0.5

    def body(q_ref, kp_ref, kc_ref, vp_ref, vc_ref, do_ref, sink_ref, bias_ref,
             dqkv_ref, dbq_ref, dbk_ref, dbv_ref, dsink_ref, ck_ref, cv_ref, cq_ref):
        n = pl.program_id(0)
        first = n == 0

        @pl.when(first)
        def _():
            for r in (dbq_ref, dbk_ref, dbv_ref, dsink_ref, ck_ref, cv_ref, cq_ref):
                r[...] = jnp.zeros_like(r)

        dqkv_ref[:, :HQ] = cq_ref[...]

        @pl.when(n < nb)
        def _():
            pairs = range(n_heads // 2)
            heads = range(n_heads)
            lanes = [slice(pr * 2 * hd, (pr + 1) * 2 * hd) for pr in pairs]
            kks, vvs, es, rs = _attn_core(q_ref, kp_ref, kc_ref, vp_ref, vc_ref, sink_ref, bias_ref, n_heads)
            low = lax.broadcasted_iota(jnp.int32, (B, 2 * hd), 1) < hd
            slot0 = lax.broadcasted_iota(jnp.int32, (B, B), 1) == 0
            lane_head = lax.broadcasted_iota(jnp.int32, (2 * hd, 4 * hd), 0) < hd
            out_head = lax.broadcasted_iota(jnp.int32, (2 * hd, 4 * hd), 1) < 2 * hd
            ones2 = (lane_head == out_head).astype(BF16)
            invs = [1.0 / rs[h // 2][:, (2 + 2 * (h % 2)) * hd:(4 + 2 * (h % 2)) * hd] for h in heads]
            dops = [do_ref[:, lanes[pr]] for pr in pairs]
            qps = [q_ref[:, lanes[pr]] for pr in pairs]
            dos = [dops[pr].astype(F32) * rs[pr][:, :2 * hd] * jnp.where(low, invs[2 * pr], invs[2 * pr + 1])
                   for pr in pairs]
            his = [t.astype(BF16) for t in dos]
            los = [(t - hi.astype(F32)).astype(BF16) for t, hi in zip(dos, his)]
            dsums = [_dot(hi, ones2, 1, 0) + _dot(lo, ones2, 1, 0) for hi, lo in zip(his, los)]
            dps = [_dot(dops[pr], vvs[pr // ppg], 1, 1) for pr in pairs]
            wide = lambda a: jnp.concatenate([a, a], axis=1)
            p_s = [es[h] * wide(invs[h]) for h in heads]
            ds_s = [p_s[h] * (dps[h // 2][:, (h % 2) * 2 * B:(h % 2 + 1) * 2 * B]
                              - wide(dsums[h // 2][:, (h % 2) * 2 * hd:(h % 2 + 1) * 2 * hd])) for h in heads]
            for h in heads:
                dsink_ref[h:h + 1, :] += jnp.sum(ds_s[h][:, :B], axis=0, keepdims=True)

            def without_slot0(a):
                return jnp.concatenate([jnp.where(slot0, 0.0, a[:, :B]), a[:, B:]], axis=1).astype(BF16)

            ds_b = [without_slot0(a) for a in ds_s]
            p_b = [without_slot0(a) for a in p_s]
            dqs = [_dot(jnp.concatenate([ds_b[2 * pr], ds_b[2 * pr + 1]], axis=1), kks[pr // ppg], 1, 0) for pr in pairs]
            for pr in pairs:
                cq_ref[:, lanes[pr]] = dqs[pr].astype(BF16)
                dbq_ref[:, lanes[pr]] += jnp.sum(dqs[pr], axis=0, keepdims=True)
            keeps = [low, jnp.logical_not(low)]
            qms = [jnp.where(keeps[h % 2], qps[h // 2], jnp.zeros_like(qps[0])) for h in heads]
            dms = [jnp.where(keeps[h % 2], dops[h // 2], jnp.zeros_like(dops[0])) for h in heads]
            dkh = [_dot(ds_b[h], qms[h], 0, 0) for h in heads]
            dvh = [_dot(p_b[h], dms[h], 0, 0) for h in heads]
            dks, dvs = [], []
            for kv in range(N_KV):
                tk = functools.reduce(jnp.add, dkh[kv * group:(kv + 1) * group])
                tv = functools.reduce(jnp.add, dvh[kv * group:(kv + 1) * group])
                dks.append((tk[:, :hd] + tk[:, hd:]) * scale)
                dvs.append(tv[:, :hd] + tv[:, hd:])
            dk2 = jnp.concatenate(dks, axis=1)
            dv2 = jnp.concatenate(dvs, axis=1)
            dbk_ref[...] += jnp.sum(dk2, axis=0, keepdims=True)
            dbv_ref[...] += jnp.sum(dv2, axis=0, keepdims=True)
            dqkv_ref[:, HQ:HQ + KVW] = (ck_ref[...] + dk2[:B]).astype(BF16)
            dqkv_ref[:, HQ + KVW:] = (cv_ref[...] + dv2[:B]).astype(BF16)
            ck_ref[...] = dk2[B:]
            cv_ref[...] = dv2[B:]

        @pl.when(n == nb)
        def _():
            dqkv_ref[:, HQ:HQ + KVW] = ck_ref[...].astype(BF16)
            dqkv_ref[:, HQ + KVW:] = cv_ref[...].astype(BF16)

    def cur(n):
        return jnp.minimum(n, nb - 1)

    def prev(n):
        return jnp.maximum(cur(n) - 1, 0)

    def outp(n):
        return jnp.maximum(n - 1, 0)

    in_specs = [pl.BlockSpec((B, HQ), lambda n: (cur(n), 0)),
                pl.BlockSpec((B, KVW), lambda n: (prev(n), kcol)), pl.BlockSpec((B, KVW), lambda n: (cur(n), kcol)),
                pl.BlockSpec((B, KVW), lambda n: (prev(n), kcol + 1)), pl.BlockSpec((B, KVW), lambda n: (cur(n), kcol + 1)),
                pl.BlockSpec((B, HQ), lambda n: (cur(n), 0)), pl.BlockSpec((n_heads, B), lambda n: (0, 0)),
                pl.BlockSpec((None, n_heads // 2, B, 4 * B), lambda n: (jnp.minimum(n, 1), 0, 0, 0))]
    out_shape = (jax.ShapeDtypeStruct((T, HQ + 2 * KVW), BF16),
                 jax.ShapeDtypeStruct((1, HQ), F32), jax.ShapeDtypeStruct((1, KVW), F32),
                 jax.ShapeDtypeStruct((1, KVW), F32), jax.ShapeDtypeStruct((n_heads, B), F32))
    out_specs = (pl.BlockSpec((B, HQ + 2 * KVW), lambda n: (outp(n), 0)),
                 _vec_spec(HQ), _vec_spec(KVW), _vec_spec(KVW), pl.BlockSpec((n_heads, B), lambda n: (0, 0)))
    return _call(body, name, (nb + 1,), in_specs, out_specs, out_shape, (qkv, qkv, qkv, qkv, qkv, do, sink_rows, bias),
                 scratch=[pltpu.VMEM((B, KVW), F32), pltpu.VMEM((B, KVW), F32), pltpu.VMEM((B, HQ), BF16)], side=side)


def _sum8(r, name):
    _, R, C = r.shape
    tr = _tile(R, 512, 16)

    def body(r_ref, o_ref):
        acc = r_ref[0].astype(F32)
        for d in range(1, N_DEV):
            acc = acc + r_ref[d].astype(F32)
        o_ref[...] = acc

    return _call(body, name, (R // tr,), [pl.BlockSpec((N_DEV, tr, C), lambda i: (0, i, 0))],
                 pl.BlockSpec((tr, C), lambda i: (i, 0)), jax.ShapeDtypeStruct((R, C), F32), (r,))


def _adamw(w, g, m, v, name):
    R, C = w.shape
    tr = _tile(R, 512, 8)
    c1 = 1.0 - ADAM_B1 ** ADAM_STEP
    c2 = 1.0 - ADAM_B2 ** ADAM_STEP

    def body(w_ref, g_ref, m_ref, v_ref, d_ref, nm_ref, nv_ref):
        gv = g_ref[...]
        nm = ADAM_B1 * m_ref[...] + (1.0 - ADAM_B1) * gv
        nv = ADAM_B2 * v_ref[...] + (1.0 - ADAM_B2) * (gv * gv)
        nm_ref[...] = nm
        nv_ref[...] = nv
        d_ref[...] = -ADAM_LR * ((nm / c1) / (jnp.sqrt(nv / c2) + ADAM_EPS) + ADAM_WD * w_ref[...])

    spec = pl.BlockSpec((tr, C), lambda i: (i, 0))
    shp = jax.ShapeDtypeStruct((R, C), F32)
    return _call(body, name, (R // tr,), [spec] * 4, (spec,) * 3, (shp, shp, shp), (w, g, m, v))


def _adamw_nd(w, g, m, v, name):
    shape = w.shape
    c = shape[-1]
    f = lambda a: a.reshape(-1, c)
    d, nm, nv = _adamw(f(w), f(g), f(m), f(v), name)
    return d.reshape(shape), nm.reshape(shape), nv.reshape(shape)


def _pack(arrs, width):
    flat = jnp.concatenate([a.reshape(-1).astype(F32) for a in arrs])
    n = flat.shape[0]
    quantum = 8 * width
    total = -(-n // quantum) * quantum
    return jnp.pad(flat, (0, total - n)).reshape(-1, width)


def _unpack(flat, shapes):
    out, off = [], 0
    for s in shapes:
        n = math.prod(s)
        out.append(flat[off:off + n].reshape(s))
        off += n
    return out


def kernel(x, c, norm1_g, norm2_g, ada_w, ada_b, attn_wqkv, attn_bqkv, attn_sinks, attn_wo, attn_bo, conv_w_in, conv_b_in, conv_dw, conv_dw_b, conv_ln_g, conv_ln_b, conv_w_out, conv_b_out, sgu_w_in, sgu_b_in, sgu_ln_g, sgu_ln_b, sgu_ws, sgu_bs, sgu_w_out, sgu_b_out, ffn_w_in, ffn_dw, ffn_dw_b, ffn_w_out, final_g, loss_target, m_norm1_g, m_norm2_g, m_ada_w, m_ada_b, m_attn_wqkv, m_attn_bqkv, m_attn_sinks, m_attn_wo, m_attn_bo, m_conv_w_in, m_conv_b_in, m_conv_dw, m_conv_dw_b, m_conv_ln_g, m_conv_ln_b, m_conv_w_out, m_conv_b_out, m_sgu_w_in, m_sgu_b_in, m_sgu_ln_g, m_sgu_ln_b, m_sgu_ws, m_sgu_bs, m_sgu_w_out, m_sgu_b_out, m_ffn_w_in, m_ffn_dw, m_ffn_dw_b, m_ffn_w_out, m_final_g, v_norm1_g, v_norm2_g, v_ada_w, v_ada_b, v_attn_wqkv, v_attn_bqkv, v_attn_sinks, v_attn_wo, v_attn_bo, v_conv_w_in, v_conv_b_in, v_conv_dw, v_conv_dw_b, v_conv_ln_g, v_conv_ln_b, v_conv_w_out, v_conv_b_out, v_sgu_w_in, v_sgu_b_in, v_sgu_ln_g, v_sgu_ln_b, v_sgu_ws, v_sgu_bs, v_sgu_w_out, v_sgu_b_out, v_ffn_w_in, v_ffn_dw, v_ffn_dw_b, v_ffn_w_out, v_final_g):
    W = dict(norm1_g=norm1_g, norm2_g=norm2_g, ada_w=ada_w, ada_b=ada_b, attn_wqkv=attn_wqkv, attn_bqkv=attn_bqkv, attn_sinks=attn_sinks, attn_wo=attn_wo, attn_bo=attn_bo, conv_w_in=conv_w_in, conv_b_in=conv_b_in, conv_dw=conv_dw, conv_dw_b=conv_dw_b, conv_ln_g=conv_ln_g, conv_ln_b=conv_ln_b, conv_w_out=conv_w_out, conv_b_out=conv_b_out, sgu_w_in=sgu_w_in, sgu_b_in=sgu_b_in, sgu_ln_g=sgu_ln_g, sgu_ln_b=sgu_ln_b, sgu_ws=sgu_ws, sgu_bs=sgu_bs, sgu_w_out=sgu_w_out, sgu_b_out=sgu_b_out, ffn_w_in=ffn_w_in, ffn_dw=ffn_dw, ffn_dw_b=ffn_dw_b, ffn_w_out=ffn_w_out, final_g=final_g)
    MOM = dict(norm1_g=m_norm1_g, norm2_g=m_norm2_g, ada_w=m_ada_w, ada_b=m_ada_b, attn_wqkv=m_attn_wqkv, attn_bqkv=m_attn_bqkv, attn_sinks=m_attn_sinks, attn_wo=m_attn_wo, attn_bo=m_attn_bo, conv_w_in=m_conv_w_in, conv_b_in=m_conv_b_in, conv_dw=m_conv_dw, conv_dw_b=m_conv_dw_b, conv_ln_g=m_conv_ln_g, conv_ln_b=m_conv_ln_b, conv_w_out=m_conv_w_out, conv_b_out=m_conv_b_out, sgu_w_in=m_sgu_w_in, sgu_b_in=m_sgu_b_in, sgu_ln_g=m_sgu_ln_g, sgu_ln_b=m_sgu_ln_b, sgu_ws=m_sgu_ws, sgu_bs=m_sgu_bs, sgu_w_out=m_sgu_w_out, sgu_b_out=m_sgu_b_out, ffn_w_in=m_ffn_w_in, ffn_dw=m_ffn_dw, ffn_dw_b=m_ffn_dw_b, ffn_w_out=m_ffn_w_out, final_g=m_final_g)
    VAR = dict(norm1_g=v_norm1_g, norm2_g=v_norm2_g, ada_w=v_ada_w, ada_b=v_ada_b, attn_wqkv=v_attn_wqkv, attn_bqkv=v_attn_bqkv, attn_sinks=v_attn_sinks, attn_wo=v_attn_wo, attn_bo=v_attn_bo, conv_w_in=v_conv_w_in, conv_b_in=v_conv_b_in, conv_dw=v_conv_dw, conv_dw_b=v_conv_dw_b, conv_ln_g=v_conv_ln_g, conv_ln_b=v_conv_ln_b, conv_w_out=v_conv_w_out, conv_b_out=v_conv_b_out, sgu_w_in=v_sgu_w_in, sgu_b_in=v_sgu_b_in, sgu_ln_g=v_sgu_ln_g, sgu_ln_b=v_sgu_ln_b, sgu_ws=v_sgu_ws, sgu_bs=v_sgu_bs, sgu_w_out=v_sgu_w_out, sgu_b_out=v_sgu_b_out, ffn_w_in=v_ffn_w_in, ffn_dw=v_ffn_dw, ffn_dw_b=v_ffn_dw_b, ffn_w_out=v_ffn_w_out, final_g=v_final_g)
    ORDER = list(W)

    _, T, D = x.shape
    depth = norm1_g.shape[0]
    n_heads = D // HEAD_DIM
    me = 4 * lax.axis_index("x") + 2 * lax.axis_index("y") + lax.axis_index("c")
    x0 = x.reshape(T, D)
    tgt = loss_target.reshape(T, D)

    small_sharded = ["attn_bqkv", "attn_bo", "conv_dw", "sgu_b_in", "sgu_ln_g", "sgu_ln_b", "sgu_b_out", "ffn_dw"]
    s_in = [c] + [W[n] for n in small_sharded]
    s_shapes = [a.shape for a in s_in]
    gathered_small, first_landed = _exchange("gather", [_pack(s_in, 128), attn_wqkv[0].T.astype(BF16)], "gather_first")
    gathered_small = gathered_small.reshape(N_DEV, -1)
    s_offs = [sum(math.prod(s) for s in s_shapes[:i]) for i in range(len(s_shapes))]

    def full(idx):
        shp = s_shapes[idx]
        a = gathered_small[:, s_offs[idx]:s_offs[idx] + math.prod(shp)].reshape((N_DEV,) + shp)
        return jnp.moveaxis(a, 0, -2).reshape(shp[:-1] + (N_DEV * shp[-1],))

    c_all = full(0).reshape(N_DEV, D)
    F_small = {n: full(1 + i) for i, n in enumerate(small_sharded)}
    attn_bias = _attn_bias(n_heads)

    c_act = c_all * jax.nn.sigmoid(c_all)
    c_pad = jnp.pad(c_act, ((0, 8), (0, 0))).astype(BF16)
    n_ada = ada_w.shape[-1]
    ada_cols = lax.dynamic_slice_in_dim(ada_b, me * n_ada, n_ada, axis=1)
    mod_loc = jnp.stack([_mm(c_pad, ada_w[i].astype(BF16), "nn", "ada_mod", bias=ada_cols[i:i + 1])
                         for i in range(depth)])
    mod_all = _all_gather(mod_loc, "gather_mod")
    mod_mine = lax.dynamic_index_in_dim(mod_all, me, axis=2, keepdims=False)
    mod = jnp.transpose(mod_mine, (1, 0, 2)).reshape(depth, 6, 1, D)

    mixer_names = {0: ("attn_wqkv", "attn_wo"), 1: ("conv_w_in", "conv_w_out"), 2: ("sgu_w_in", "sgu_w_out")}

    def piece(part, i, which):
        w_in, w_out = ("ffn_w_in", "ffn_w_out") if part == "ffn" else mixer_names[i % 3]
        l = i if part == "ffn" else i // 3
        return (w_in, l, W[w_in].shape[-1]) if which == "in" else (w_out, l, W[w_out].shape[1])

    def both(part, i):
        return [(part, i, "in"), (part, i, "out")] if 0 <= i < depth else []

    def local_weight(key):
        n, l, _ = piece(*key)
        return (W[n][l].T if key[2] == "in" else W[n][l]).astype(BF16)

    WF = {}

    def land_weights(keys, gathered):
        for key, g in zip(keys, gathered):
            n, l, r = piece(*key)
            WF[(n, l)] = g.reshape(N_DEV * r, D)

    def with_gather(keys, run):
        if not keys:
            return run(None)
        *outs, landed = run(("gather", [local_weight(k) for k in keys]))
        land_weights(keys, landed)
        return outs[0] if len(outs) == 1 else tuple(outs)

    def vec(a):
        return a.reshape(1, -1)

    land_weights(both("mix", 0)[:1], [first_landed])

    saved = []
    xs, y_prev, gate_prev = x0, None, None
    for i in range(depth):
        sh1, sc1, g1, sh2, sc2, g2 = [mod[i, k] for k in range(6)]
        kind, j = i % 3, i // 3
        st = dict(kind=kind, j=j)
        if y_prev is None:
            h1 = _norm_mod_fwd(xs, vec(norm1_g[i]), sc1, sh1, "norm1_fwd")
        else:
            xs, h1 = _norm_mod_fwd(xs, vec(norm1_g[i]), sc1, sh1, "norm1_fwd", y=y_prev, gate=gate_prev)
        st.update(x_in=xs, h1=h1)
        first_ffn_in = [("ffn", 0, "in")] if i == 0 else []
        if kind == 0:
            qkv = with_gather(both("mix", 0)[1:] if i == 0 else [], lambda side: _mm(
                h1, WF[("attn_wqkv", j)], "nt", "attn_qkv", out_dtype=BF16, bias=vec(F_small["attn_bqkv"][j]), side=side))
            o = with_gather(first_ffn_in, lambda side: _attn_fwd(qkv, _sink_rows(attn_sinks[j]), attn_bias, n_heads,
                                                                 "attn_fwd", side=side))
            y1 = _mm(o, WF[("attn_wo", j)], "nn", "attn_out", out_dtype=BF16, bias=vec(F_small["attn_bo"][j]))
            st.update(qkv=qkv, o=o)
        elif kind == 1:
            p = _mm(h1, WF[("conv_w_in", j)], "nt", "conv_in", out_dtype=BF16, bias=vec(conv_b_in[j]))
            zc, s = with_gather(first_ffn_in, lambda side: _convmix_fwd(
                p, F_small["conv_dw"][j], vec(conv_dw_b[j]), vec(conv_ln_g[j]), vec(conv_ln_b[j]), "convmix_fwd", side=side))
            y1 = _mm(s, WF[("conv_w_out", j)], "nn", "conv_out", out_dtype=BF16, bias=vec(conv_b_out[j]))
            st.update(p=p, zc=zc, s=s)
        else:
            p = _mm(h1, WF[("sgu_w_in", j)], "nt", "sgu_in", out_dtype=BF16, bias=vec(F_small["sgu_b_in"][j]))
            mm_ = _sgu_fwd(p, vec(F_small["sgu_ln_g"][j]), vec(F_small["sgu_ln_b"][j]), sgu_ws[j], sgu_bs[j].T, "sgu_fwd")
            y1 = _mm(mm_, WF[("sgu_w_out", j)], "nn", "sgu_out", out_dtype=BF16, bias=vec(F_small["sgu_b_out"][j]))
            st.update(p=p, m=mm_)
        xs, h2 = _norm_mod_fwd(xs, vec(norm2_g[i]), sc2, sh2, "norm2_fwd", y=y1, gate=g1)
        nxt = i + 1
        z = with_gather(both("mix", nxt) + ([("ffn", 0, "out")] if i == 0 else []),
                        lambda side: _mm(h2, WF[("ffn_w_in", i)], "nt", "ffn_in", out_dtype=BF16, side=side))
        a = with_gather(both("ffn", nxt)[:1],
                        lambda side: _ffn_mid_fwd(z, F_small["ffn_dw"][i], vec(ffn_dw_b[i]), "ffn_mid_fwd", side=side))
        y2 = with_gather(both("ffn", nxt)[1:], lambda side: _mm(a, WF[("ffn_w_out", i)], "nn", "ffn_out",
                                                                      out_dtype=BF16, side=side))
        st.update(y1=y1, x_mid=xs, h2=h2, z=z, a=a, y2=y2)
        saved.append(st)
        y_prev, gate_prev = y2, g2

    dx, d_final_g, loss_row, dy2, dg2 = _final_loss(xs, y_prev, gate_prev, tgt, vec(final_g), "final_loss")
    loss = lax.psum(loss_row[0, 0], ("x", "y", "c"))

    G = {n: [None] * W[n].shape[0] for n in ORDER if n != "final_g"}
    GW = {}
    recv = {}
    dmod = [None] * depth

    def with_scatter(keys, run):
        if not keys:
            return run(None)
        sends = [GW[piece(*k)[:2]].reshape(N_DEV, piece(*k)[2], D) for k in keys]
        *outs, landed = run(("scatter", sends))
        recv.update(zip(keys, landed))
        return outs[0] if len(outs) == 1 else tuple(outs)

    for i in reversed(range(depth)):
        st = saved[i]
        sh1, sc1, g1, sh2, sc2, g2 = [mod[i, k] for k in range(6)]
        kind, j = st["kind"], st["j"]
        last_out = [("mix", 0, "out")] if i == 0 else []
        last_in = [("mix", 0, "in")] if i == 0 else []
        GW[("ffn_w_out", i)] = _mm(st["a"], dy2, "tn", "ffn_out_dw", out_dtype=BF16)
        da = _mm(dy2, WF[("ffn_w_out", i)], "nt", "ffn_out_dx", out_dtype=BF16)
        dzg, dzu, ddw, ddwb = with_scatter(both("mix", i + 1) + [("ffn", i, "out")], lambda side: _ffn_mid_bwd(
            st["z"], da, F_small["ffn_dw"][i], vec(ffn_dw_b[i]), "ffn_mid_bwd", side=side))
        G["ffn_dw"][i], G["ffn_dw_b"][i] = ddw, ddwb[0]
        GW[("ffn_w_in", i)] = _mm(dzg, st["h2"], "tn", "ffn_in_dw", out_dtype=BF16, a2=dzu)
        dh2 = with_scatter([("ffn", i, "in")],
                           lambda side: _mm(dzg, WF[("ffn_w_in", i)], "nn", "ffn_in_dx", out_dtype=BF16, a2=dzu, side=side))
        dx, dn2, dsc2, dsh2, dy1, dg1, dbo = _norm_mod_bwd(dh2, st["x_mid"], vec(norm2_g[i]), sc2, sh2, dx, "norm_bwd",
                                                           y_up=st["y1"], gate_up=g1)
        G["norm2_g"][i] = dn2[0]
        if kind == 0:
            G["attn_bo"][j] = dbo[0]
            GW[("attn_wo", j)] = _mm(st["o"], dy1, "tn", "attn_out_dw", out_dtype=BF16)
            do = _mm(dy1, WF[("attn_wo", j)], "nt", "attn_out_dx", out_dtype=BF16)
            dqkv, dbq, dbk, dbv, dsk = with_scatter(last_out, lambda side: _attn_bwd(
                st["qkv"], do, _sink_rows(attn_sinks[j]), attn_bias, n_heads, "attn_bwd", side=side))
            G["attn_bqkv"][j] = jnp.concatenate([dbq, dbk, dbv], axis=1)[0]
            G["attn_sinks"][j] = dsk[:, 0]
            GW[("attn_wqkv", j)] = _mm(dqkv, st["h1"], "tn", "attn_qkv_dw", out_dtype=BF16)
            dh1 = with_scatter(last_in, lambda side: _mm(dqkv, WF[("attn_wqkv", j)], "nn", "attn_qkv_dx", out_dtype=BF16,
                                                         side=side))
        elif kind == 1:
            G["conv_b_out"][j] = dbo[0]
            GW[("conv_w_out", j)] = _mm(st["s"], dy1, "tn", "conv_out_dw", out_dtype=BF16)
            ds = _mm(dy1, WF[("conv_w_out", j)], "nt", "conv_out_dx", out_dtype=BF16)
            dp, dbin, ddw, ddwb, dlng, dlnb = with_scatter(last_out, lambda side: _convmix_bwd(
                ds, st["zc"], st["p"], F_small["conv_dw"][j], vec(conv_ln_g[j]), vec(conv_ln_b[j]), "convmix_bwd",
                side=side))
            G["conv_b_in"][j], G["conv_dw"][j], G["conv_dw_b"][j] = dbin[0], ddw, ddwb[0]
            G["conv_ln_g"][j], G["conv_ln_b"][j] = dlng[0], dlnb[0]
            GW[("conv_w_in", j)] = _mm(dp, st["h1"], "tn", "conv_in_dw", out_dtype=BF16)
            dh1 = with_scatter(last_in, lambda side: _mm(dp, WF[("conv_w_in", j)], "nn", "conv_in_dx", out_dtype=BF16,
                                                         side=side))
        else:
            G["sgu_b_out"][j] = dbo[0]
            GW[("sgu_w_out", j)] = _mm(st["m"], dy1, "tn", "sgu_out_dw", out_dtype=BF16)
            dm = _mm(dy1, WF[("sgu_w_out", j)], "nt", "sgu_out_dx", out_dtype=BF16)
            dp, dbin, dlng, dlnb, dws, dbst = with_scatter(last_out, lambda side: _sgu_bwd(
                st["p"], dm, vec(F_small["sgu_ln_g"][j]), vec(F_small["sgu_ln_b"][j]), sgu_ws[j], sgu_bs[j].T, "sgu_bwd",
                side=side))
            G["sgu_b_in"][j], G["sgu_ln_g"][j], G["sgu_ln_b"][j] = dbin[0], dlng[0], dlnb[0]
            G["sgu_ws"][j], G["sgu_bs"][j] = dws, dbst.T
            GW[("sgu_w_in", j)] = _mm(dp, st["h1"], "tn", "sgu_in_dw", out_dtype=BF16)
            dh1 = with_scatter(last_in, lambda side: _mm(dp, WF[("sgu_w_in", j)], "nn", "sgu_in_dx", out_dtype=BF16,
                                                         side=side))
        dmod_i = lambda: jnp.concatenate([dsh1, dsc1, dg1, dsh2, dsc2, dg2], axis=1)[0]
        if i > 0:
            dx, dn1, dsc1, dsh1, dy2_below, dg2_below, _ = _norm_mod_bwd(
                dh1, st["x_in"], vec(norm1_g[i]), sc1, sh1, dx, "norm_bwd", y_up=saved[i - 1]["y2"], gate_up=mod[i - 1, 5])
            dmod[i] = dmod_i()
            dy2, dg2 = dy2_below, dg2_below
        else:
            dx, dn1, dsc1, dsh1 = _norm_mod_bwd(dh1, st["x_in"], vec(norm1_g[i]), sc1, sh1, dx, "norm_bwd")
            dmod[i] = dmod_i()
        G["norm1_g"][i] = dn1[0]
    grad_x = dx.reshape(x.shape)

    small_names = [n for n in ORDER if n not in
                   ("ada_w", "ada_b", "attn_wqkv", "attn_wo", "conv_w_in", "conv_w_out", "sgu_w_in", "sgu_w_out",
                    "ffn_w_in", "ffn_w_out", "final_g")]
    g_small = {n: jnp.stack(G[n]) for n in small_names}
    g_small["final_g"] = d_final_g[0]
    g_small["ada_b"] = jnp.stack(dmod)
    names16 = ["ffn_dw", "conv_dw", "sgu_ws", "ffn_dw_b", "ada_b"]
    names32 = [n for n in g_small if n not in names16]
    shapes16, shapes32 = [g_small[n].shape for n in names16], [g_small[n].shape for n in names32]
    gathered32, gathered16 = _exchange("gather", [_pack([g_small[n] for n in names32], 1024),
                                                  _pack([g_small[n] for n in names16], 1024).astype(BF16)],
                                       "gather_small_grads")
    gsum = dict(zip(names32, _unpack(_sum8(gathered32, "sum_small_grads").reshape(-1), shapes32)))
    gsum.update(zip(names16, _unpack(_sum8(gathered16, "sum_small_grads").reshape(-1), shapes16)))

    def local_shard(n, a):
        if n in small_sharded:
            w = W[n].shape[-1]
            return lax.dynamic_slice_in_dim(a, me * w, w, axis=a.ndim - 1)
        return a

    gsum = {n: local_shard(n, a) for n, a in gsum.items()}

    off16 = sum(math.prod(s) for s in shapes16[:-1])
    dmod_all = gathered16.reshape(N_DEV, -1)[:, off16:off16 + math.prod(shapes16[-1])].reshape((N_DEV,) + shapes16[-1])
    dmod_cols = lax.dynamic_slice_in_dim(dmod_all, me * n_ada, n_ada, axis=2)
    dmod_pad = jnp.pad(dmod_cols, ((0, 8), (0, 0), (0, 0)))
    g_ada_w = jnp.stack([_mm(c_pad, dmod_pad[:, i, :], "tn", "ada_dw") for i in range(depth)])

    big = {}
    for i in range(depth):
        for key in both("mix", i) + both("ffn", i):
            n, l, _ = piece(*key)
            gp = _sum8(recv[key], "sum_weight_grads")
            big.setdefault(n, {})[l] = gp.T if key[2] == "in" else gp
    grads = {n: jnp.stack([v[l] for l in range(len(v))]) for n, v in big.items()}
    grads["ada_w"] = g_ada_w
    grads.update(gsum)

    delta, new_m, new_v = {}, {}, {}
    big_names = ["ada_w", "attn_wqkv", "attn_wo", "conv_w_in", "conv_w_out", "sgu_w_in", "sgu_w_out", "ffn_w_in",
                 "ffn_w_out"]
    for n in big_names:
        delta[n], new_m[n], new_v[n] = _adamw_nd(W[n], grads[n], MOM[n], VAR[n], "adamw_" + n)
    rest = [n for n in ORDER if n not in big_names]
    rest_shapes = [W[n].shape for n in rest]
    pk = lambda d: _pack([d[n] for n in rest], 128)
    d_s, m_s, v_s = _adamw(pk(W), pk(grads), pk(MOM), pk(VAR), "adamw_small")
    for n, a, b_, c_ in zip(rest, _unpack(d_s.reshape(-1), rest_shapes), _unpack(m_s.reshape(-1), rest_shapes),
                            _unpack(v_s.reshape(-1), rest_shapes)):
        delta[n], new_m[n], new_v[n] = a, b_, c_

    return (loss, grad_x, *[grads[n] for n in ORDER], *[delta[n] for n in ORDER],
            *[new_m[n] for n in ORDER], *[new_v[n] for n in ORDER])
```
